```python
import jax, jax.numpy as jnp
from jax import lax
import numpy as np

D_MODEL = 1024
BATCH = 8
SEQ = 8192
DEPTH = 1

N_META = 16
HGRN_HEADS = 8
HGRN_EXPAND = 128
HGRN_FDIM = HGRN_HEADS * HGRN_EXPAND
HGRN_VDIM = D_MODEL
HGRN_HEAD_V = HGRN_VDIM // HGRN_HEADS
CONV_DIM = D_MODEL
CONV_WIDTH = 3
D_FF = 2816
FFN_CONV_WIDTH = 3
CHUNK = 64
EPS = 1e-6
IN_SIZES = (HGRN_FDIM,
            HGRN_FDIM,
            HGRN_VDIM,
            HGRN_VDIM,
            CONV_DIM,
            CONV_DIM,
            CONV_DIM,
            D_MODEL,
            D_MODEL)
IN_TOTAL = sum(IN_SIZES)

kernel_name = "hgrn2_shortconv_gated_hybrid"


def _split_points(sizes):
    pts, acc = [], 0
    for s in sizes[:-1]:
        acc += s
        pts.append(acc)
    return pts


def rmsnorm(x, w):
    xf = x.astype(jnp.float32)
    y = xf * lax.rsqrt(jnp.mean(xf * xf, axis=-1, keepdims=True) + EPS)
    return (y * w.astype(jnp.float32)).astype(x.dtype)


def causal_dwconv(x, w, b=None):
    K = w.shape[0]
    L = x.shape[1]
    xp = jnp.pad(x, ((0, 0), (K - 1, 0), (0, 0)))
    y = w[0] * xp[:, 0:L]
    for j in range(1, K):
        y = y + w[j] * xp[:, j:j + L]
    if b is not None:
        y = y + b
    return y


def layer_lower_bounds(lb_param):
    p = jax.nn.softmax(lb_param.astype(jnp.float32), axis=0)
    return jnp.cumsum(p, axis=0)[:DEPTH]


def hgrn2_chunked(q, k, v, logf):
    Bsz, H, T, DK = q.shape
    DV = v.shape[-1]
    n = T // CHUNK

    def to_chunks(a):
        return jnp.moveaxis(a.reshape(Bsz, H, n, CHUNK, a.shape[-1]), 2, 0)

    qc, kc, vc, gc = to_chunks(q), to_chunks(k), to_chunks(v), to_chunks(logf)
    causal = jnp.tril(jnp.ones((CHUNK, CHUNK), dtype=bool))[:, :, None]

    def step(S, inp):
        qb, kb, vb, gb = inp
        G = jnp.cumsum(gb, axis=-2)
        o_inter = jnp.einsum('bhtd,bhdv->bhtv', qb * jnp.exp(G), S)
        diff = G[:, :, :, None, :] - G[:, :, None, :, :]
        decay = jnp.exp(jnp.where(causal, diff, -jnp.inf))
        A = jnp.sum(qb[:, :, :, None, :] * kb[:, :, None, :, :] * decay, axis=-1)
        o = o_inter + jnp.einsum('bhts,bhsv->bhtv', A, vb)
        G_last = G[:, :, -1:, :]
        k_dec = kb * jnp.exp(G_last - G)
        S_new = jnp.exp(G_last[:, :, 0, :])[..., None] * S + jnp.einsum('bhsd,bhsv->bhdv', k_dec, vb)
        return S_new, o

    S0 = jnp.zeros((Bsz, H, DK, DV), jnp.float32)
    _, o = lax.scan(step, S0, (qc, kc, vc, gc))
    return jnp.moveaxis(o, 0, 2).reshape(Bsz, H, T, DV)


def hybrid_layer(h, lb, attn_norm_w, w_in, hgrn_norm_w, conv_w, w_out,
                 ffn_norm_w, w_up, ffn_conv_w, ffn_conv_b, w_down):
    Bsz, L, _ = h.shape
    dt = h.dtype
    u = rmsnorm(h, attn_norm_w)
    proj = u @ w_in
    q, f_raw, i_in, g_out, b_gate, c_gate, x_conv, gate_a, gate_b = jnp.split(
        proj, _split_points(IN_SIZES), axis=-1)

    q = jax.nn.silu(q.astype(jnp.float32))
    f = lb + (1.0 - lb) * jax.nn.sigmoid(f_raw.astype(jnp.float32))
    logf = jnp.log(f)
    k = 1.0 - f
    v = i_in.astype(jnp.float32)

    def heads(a):
        return jnp.transpose(a.reshape(Bsz, L, HGRN_HEADS, -1), (0, 2, 1, 3))

    pad = (-N_META) % CHUNK
    tpad = ((0, 0), (0, 0), (pad, 0), (0, 0))
    qh, kh, vh, gh = (jnp.pad(heads(a), tpad) for a in (q, k, v, logf))
    o = hgrn2_chunked(qh, kh, vh, gh)[:, :, pad:]
    o = jnp.transpose(o, (0, 2, 1, 3))
    o = rmsnorm(o, hgrn_norm_w).reshape(Bsz, L, HGRN_VDIM)
    y_a = (o * jax.nn.silu(g_out.astype(jnp.float32))).astype(dt)

    y_b = b_gate * causal_dwconv(c_gate * x_conv, conv_w)

    merged = jax.nn.sigmoid(gate_a) * y_a + jax.nn.sigmoid(gate_b) * y_b
    h = h + merged @ w_out

    u2 = rmsnorm(h, ffn_norm_w)
    a, val = jnp.split(u2 @ w_up, 2, axis=-1)
    a = causal_dwconv(a, ffn_conv_w, ffn_conv_b)
    h = h + (jax.nn.silu(a) * val) @ w_down
    return h


def _fwd_setup_inputs(seed: int = 0) -> dict:
    key = jax.random.key(seed)
    ks = jax.random.split(key, 16)
    f32 = jnp.float32
    nrm = lambda k, shape, s: jax.random.normal(k, shape, f32) * s
    return {
        "x": nrm(ks[0], (BATCH, SEQ, D_MODEL), 1.0),
        "meta_tokens": nrm(ks[1], (N_META, D_MODEL), 1.0),
        "lb_param": nrm(ks[2], (DEPTH + 1, HGRN_FDIM), 0.1),
        "attn_norm_w": 1.0 + nrm(ks[3], (DEPTH, D_MODEL), 0.02),
        "w_in": nrm(ks[4], (DEPTH, D_MODEL, IN_TOTAL), D_MODEL ** -0.5),
        "hgrn_norm_w": 1.0 + nrm(ks[5], (DEPTH, HGRN_HEAD_V), 0.02),
        "conv_w": nrm(ks[6], (DEPTH, CONV_WIDTH, CONV_DIM), CONV_WIDTH ** -0.5),
        "w_out": nrm(ks[7], (DEPTH, D_MODEL, D_MODEL), D_MODEL ** -0.5),
        "ffn_norm_w": 1.0 + nrm(ks[8], (DEPTH, D_MODEL), 0.02),
        "w_up": nrm(ks[9], (DEPTH, D_MODEL, 2 * D_FF), D_MODEL ** -0.5),
        "ffn_conv_w": nrm(ks[10], (DEPTH, FFN_CONV_WIDTH, D_FF), FFN_CONV_WIDTH ** -0.5),
        "ffn_conv_b": nrm(ks[11], (DEPTH, D_FF), 0.02),
        "w_down": nrm(ks[12], (DEPTH, D_FF, D_MODEL), D_FF ** -0.5),
        "final_norm_w": 1.0 + nrm(ks[13], (D_MODEL,), 0.02),
    }


def _fwd_reference(x, meta_tokens, lb_param, attn_norm_w, w_in, hgrn_norm_w, conv_w, w_out,
              ffn_norm_w, w_up, ffn_conv_w, ffn_conv_b, w_down, final_norm_w):
    Bsz = x.shape[0]
    meta = jnp.broadcast_to(meta_tokens.astype(x.dtype)[None], (Bsz, N_META, D_MODEL))
    h = jnp.concatenate([meta, x], axis=1)
    lbs = layer_lower_bounds(lb_param)
    for l in range(DEPTH):
        h = hybrid_layer(h, lbs[l], attn_norm_w[l], w_in[l], hgrn_norm_w[l], conv_w[l],
                         w_out[l], ffn_norm_w[l], w_up[l], ffn_conv_w[l], ffn_conv_b[l],
                         w_down[l])
    h = rmsnorm(h, final_norm_w)
    return h[:, N_META:]


import jax as _jax
import jax.numpy as _jnp

TWIN_FORMAT = 'train_step'
FWD_PARAMS = ['x', 'meta_tokens', 'lb_param', 'attn_norm_w', 'w_in', 'hgrn_norm_w', 'conv_w', 'w_out', 'ffn_norm_w', 'w_up', 'ffn_conv_w', 'ffn_conv_b', 'w_down', 'final_norm_w']
TWIN_WEIGHTS = ['meta_tokens', 'lb_param', 'attn_norm_w', 'w_in', 'hgrn_norm_w', 'conv_w', 'w_out', 'ffn_norm_w', 'w_up', 'ffn_conv_w', 'ffn_conv_b', 'w_down', 'final_norm_w']
TWIN_DIFF_INPUT = 'x'
TWIN_INPUTS = ['x', 'meta_tokens', 'lb_param', 'attn_norm_w', 'w_in', 'hgrn_norm_w', 'conv_w', 'w_out', 'ffn_norm_w', 'w_up', 'ffn_conv_w', 'ffn_conv_b', 'w_down', 'final_norm_w', 'loss_target', 'm_meta_tokens', 'm_lb_param', 'm_attn_norm_w', 'm_w_in', 'm_hgrn_norm_w', 'm_conv_w', 'm_w_out', 'm_ffn_norm_w', 'm_w_up', 'm_ffn_conv_w', 'm_ffn_conv_b', 'm_w_down', 'm_final_norm_w', 'v_meta_tokens', 'v_lb_param', 'v_attn_norm_w', 'v_w_in', 'v_hgrn_norm_w', 'v_conv_w', 'v_w_out', 'v_ffn_norm_w', 'v_w_up', 'v_ffn_conv_w', 'v_ffn_conv_b', 'v_w_down', 'v_final_norm_w']
TWIN_OUTPUTS = ['loss', 'grad_x', 'grad_meta_tokens', 'grad_lb_param', 'grad_attn_norm_w', 'grad_w_in', 'grad_hgrn_norm_w', 'grad_conv_w', 'grad_w_out', 'grad_ffn_norm_w', 'grad_w_up', 'grad_ffn_conv_w', 'grad_ffn_conv_b', 'grad_w_down', 'grad_final_norm_w', 'delta_meta_tokens', 'delta_lb_param', 'delta_attn_norm_w', 'delta_w_in', 'delta_hgrn_norm_w', 'delta_conv_w', 'delta_w_out', 'delta_ffn_norm_w', 'delta_w_up', 'delta_ffn_conv_w', 'delta_ffn_conv_b', 'delta_w_down', 'delta_final_norm_w', 'new_m_meta_tokens', 'new_m_lb_param', 'new_m_attn_norm_w', 'new_m_w_in', 'new_m_hgrn_norm_w', 'new_m_conv_w', 'new_m_w_out', 'new_m_ffn_norm_w', 'new_m_w_up', 'new_m_ffn_conv_w', 'new_m_ffn_conv_b', 'new_m_w_down', 'new_m_final_norm_w', 'new_v_meta_tokens', 'new_v_lb_param', 'new_v_attn_norm_w', 'new_v_w_in', 'new_v_hgrn_norm_w', 'new_v_conv_w', 'new_v_w_out', 'new_v_ffn_norm_w', 'new_v_w_up', 'new_v_ffn_conv_w', 'new_v_ffn_conv_b', 'new_v_w_down', 'new_v_final_norm_w']
TWIN_LEAF_KINDS = {'loss': 'loss', 'grad_x': 'grad_x', 'grad_meta_tokens': 'grad_w', 'grad_lb_param': 'grad_w', 'grad_attn_norm_w': 'grad_w', 'grad_w_in': 'grad_w', 'grad_hgrn_norm_w': 'grad_w', 'grad_conv_w': 'grad_w', 'grad_w_out': 'grad_w', 'grad_ffn_norm_w': 'grad_w', 'grad_w_up': 'grad_w', 'grad_ffn_conv_w': 'grad_w', 'grad_ffn_conv_b': 'grad_w', 'grad_w_down': 'grad_w', 'grad_final_norm_w': 'grad_w', 'delta_meta_tokens': 'delta_w', 'delta_lb_param': 'delta_w', 'delta_attn_norm_w': 'delta_w', 'delta_w_in': 'delta_w', 'delta_hgrn_norm_w': 'delta_w', 'delta_conv_w': 'delta_w', 'delta_w_out': 'delta_w', 'delta_ffn_norm_w': 'delta_w', 'delta_w_up': 'delta_w', 'delta_ffn_conv_w': 'delta_w', 'delta_ffn_conv_b': 'delta_w', 'delta_w_down': 'delta_w', 'delta_final_norm_w': 'delta_w', 'new_m_meta_tokens': 'new_m', 'new_m_lb_param': 'new_m', 'new_m_attn_norm_w': 'new_m', 'new_m_w_in': 'new_m', 'new_m_hgrn_norm_w': 'new_m', 'new_m_conv_w': 'new_m', 'new_m_w_out': 'new_m', 'new_m_ffn_norm_w': 'new_m', 'new_m_w_up': 'new_m', 'new_m_ffn_conv_w': 'new_m', 'new_m_ffn_conv_b': 'new_m', 'new_m_w_down': 'new_m', 'new_m_final_norm_w': 'new_m', 'new_v_meta_tokens': 'new_v', 'new_v_lb_param': 'new_v', 'new_v_attn_norm_w': 'new_v', 'new_v_w_in': 'new_v', 'new_v_hgrn_norm_w': 'new_v', 'new_v_conv_w': 'new_v', 'new_v_w_out': 'new_v', 'new_v_ffn_norm_w': 'new_v', 'new_v_w_up': 'new_v', 'new_v_ffn_conv_w': 'new_v', 'new_v_ffn_conv_b': 'new_v', 'new_v_w_down': 'new_v', 'new_v_final_norm_w': 'new_v'}


def _forward(args):
    return _fwd_reference(*[args[k] for k in FWD_PARAMS])


def _output_shape():
    def fwd():
        inp = _fwd_setup_inputs(0)
        return _fwd_reference(*[inp[k] for k in FWD_PARAMS])
    out = _jax.eval_shape(fwd)
    return out.shape, out.dtype

N_MICROBATCH = 1
ADAM_LR = 0.001
ADAM_B1 = 0.9
ADAM_B2 = 0.999
ADAM_EPS = 1e-08
ADAM_WD = 0.01
ADAM_STEP = 10
PER_EXAMPLE_BATCH_AXIS = {'x': 0, 'loss_target': 0}
SHARED_INPUTS = []
_WEIGHT_DTYPES = {'meta_tokens': _jnp.float32, 'lb_param': _jnp.float32, 'attn_norm_w': _jnp.float32, 'w_in': _jnp.float32, 'hgrn_norm_w': _jnp.float32, 'conv_w': _jnp.float32, 'w_out': _jnp.float32, 'ffn_norm_w': _jnp.float32, 'w_up': _jnp.float32, 'ffn_conv_w': _jnp.float32, 'ffn_conv_b': _jnp.float32, 'w_down': _jnp.float32, 'final_norm_w': _jnp.float32}
MOMENT_SCALE = {'meta_tokens': 3.735202e-03, 'lb_param': 7.928648e-03, 'attn_norm_w': 2.565479e-01, 'w_in': 8.401351e-02, 'hgrn_norm_w': 2.244968e-01, 'conv_w': 1.289700e-01, 'w_out': 1.483345e-01, 'ffn_norm_w': 1.660287e-01, 'w_up': 7.100789e-02, 'ffn_conv_w': 7.261863e-02, 'ffn_conv_b': 7.129090e-02, 'w_down': 1.165699e-01, 'final_norm_w': 6.402365e+01}


def _to_microbatches(a, axis):
    t = _jnp.moveaxis(a, axis, 0)
    t = t.reshape((N_MICROBATCH, t.shape[0] // N_MICROBATCH) + t.shape[1:])
    return _jnp.moveaxis(t, 1, axis + 1)


def setup_inputs(seed: int = 0) -> dict:
    inp = _fwd_setup_inputs(seed)
    key = _jax.random.fold_in(_jax.random.key(seed), 7919)
    shape, _ = _output_shape()
    out = dict(inp)
    out["loss_target"] = _jax.random.normal(_jax.random.fold_in(key, 0), shape, _jnp.float32)
    for i, name in enumerate(TWIN_WEIGHTS):
        w = inp[name].astype(_jnp.float32)
        if MOMENT_SCALE is None:
            s = _jnp.sqrt(_jnp.mean(_jnp.square(w)) + 1e-30)
        else:
            s = MOMENT_SCALE[name]
        km, kv = _jax.random.split(_jax.random.fold_in(key, i + 1))
        out[name] = w
        out["m_" + name] = s * _jax.random.normal(km, w.shape, _jnp.float32)
        out["v_" + name] = (s * s) * _jax.random.uniform(kv, w.shape, _jnp.float32, 0.5, 1.5)
    if N_MICROBATCH > 1:
        for name, axis in PER_EXAMPLE_BATCH_AXIS.items():
            out[name] = _to_microbatches(out[name], axis)
    return {'x': out['x'], 'meta_tokens': out['meta_tokens'], 'lb_param': out['lb_param'], 'attn_norm_w': out['attn_norm_w'], 'w_in': out['w_in'], 'hgrn_norm_w': out['hgrn_norm_w'], 'conv_w': out['conv_w'], 'w_out': out['w_out'], 'ffn_norm_w': out['ffn_norm_w'], 'w_up': out['w_up'], 'ffn_conv_w': out['ffn_conv_w'], 'ffn_conv_b': out['ffn_conv_b'], 'w_down': out['w_down'], 'final_norm_w': out['final_norm_w'], 'loss_target': out['loss_target'], 'm_meta_tokens': out['m_meta_tokens'], 'm_lb_param': out['m_lb_param'], 'm_attn_norm_w': out['m_attn_norm_w'], 'm_w_in': out['m_w_in'], 'm_hgrn_norm_w': out['m_hgrn_norm_w'], 'm_conv_w': out['m_conv_w'], 'm_w_out': out['m_w_out'], 'm_ffn_norm_w': out['m_ffn_norm_w'], 'm_w_up': out['m_w_up'], 'm_ffn_conv_w': out['m_ffn_conv_w'], 'm_ffn_conv_b': out['m_ffn_conv_b'], 'm_w_down': out['m_w_down'], 'm_final_norm_w': out['m_final_norm_w'], 'v_meta_tokens': out['v_meta_tokens'], 'v_lb_param': out['v_lb_param'], 'v_attn_norm_w': out['v_attn_norm_w'], 'v_w_in': out['v_w_in'], 'v_hgrn_norm_w': out['v_hgrn_norm_w'], 'v_conv_w': out['v_conv_w'], 'v_w_out': out['v_w_out'], 'v_ffn_norm_w': out['v_ffn_norm_w'], 'v_w_up': out['v_w_up'], 'v_ffn_conv_w': out['v_ffn_conv_w'], 'v_ffn_conv_b': out['v_ffn_conv_b'], 'v_w_down': out['v_w_down'], 'v_final_norm_w': out['v_final_norm_w']}


def _loss(weights, diff, rest, loss_target):
    with _jax.named_scope("forward"):
        args = {**rest, TWIN_DIFF_INPUT: diff, **{k: w.astype(_WEIGHT_DTYPES[k]) for k, w in weights.items()}}
        y = _forward(args)
    with _jax.named_scope("loss_head"):
        err = _jnp.square(y.astype(_jnp.float32) - loss_target)
        return 0.5 * _jnp.sum(_jnp.mean(err, axis=-1)) if err.ndim else 0.5 * err


def _adamw(w, g, m, v):
    m = ADAM_B1 * m + (1.0 - ADAM_B1) * g
    v = ADAM_B2 * v + (1.0 - ADAM_B2) * _jnp.square(g)
    m_hat = m / (1.0 - ADAM_B1 ** ADAM_STEP)
    v_hat = v / (1.0 - ADAM_B2 ** ADAM_STEP)
    delta = -ADAM_LR * (m_hat / (_jnp.sqrt(v_hat) + ADAM_EPS) + ADAM_WD * w)
    return delta, m, v


def reference(x, meta_tokens, lb_param, attn_norm_w, w_in, hgrn_norm_w, conv_w, w_out, ffn_norm_w, w_up, ffn_conv_w, ffn_conv_b, w_down, final_norm_w, loss_target, m_meta_tokens, m_lb_param, m_attn_norm_w, m_w_in, m_hgrn_norm_w, m_conv_w, m_w_out, m_ffn_norm_w, m_w_up, m_ffn_conv_w, m_ffn_conv_b, m_w_down, m_final_norm_w, v_meta_tokens, v_lb_param, v_attn_norm_w, v_w_in, v_hgrn_norm_w, v_conv_w, v_w_out, v_ffn_norm_w, v_w_up, v_ffn_conv_w, v_ffn_conv_b, v_w_down, v_final_norm_w):
    given = dict(x=x, meta_tokens=meta_tokens, lb_param=lb_param, attn_norm_w=attn_norm_w, w_in=w_in, hgrn_norm_w=hgrn_norm_w, conv_w=conv_w, w_out=w_out, ffn_norm_w=ffn_norm_w, w_up=w_up, ffn_conv_w=ffn_conv_w, ffn_conv_b=ffn_conv_b, w_down=w_down, final_norm_w=final_norm_w, loss_target=loss_target, m_meta_tokens=m_meta_tokens, m_lb_param=m_lb_param, m_attn_norm_w=m_attn_norm_w, m_w_in=m_w_in, m_hgrn_norm_w=m_hgrn_norm_w, m_conv_w=m_conv_w, m_w_out=m_w_out, m_ffn_norm_w=m_ffn_norm_w, m_w_up=m_w_up, m_ffn_conv_w=m_ffn_conv_w, m_ffn_conv_b=m_ffn_conv_b, m_w_down=m_w_down, m_final_norm_w=m_final_norm_w, v_meta_tokens=v_meta_tokens, v_lb_param=v_lb_param, v_attn_norm_w=v_attn_norm_w, v_w_in=v_w_in, v_hgrn_norm_w=v_hgrn_norm_w, v_conv_w=v_conv_w, v_w_out=v_w_out, v_ffn_norm_w=v_ffn_norm_w, v_w_up=v_w_up, v_ffn_conv_w=v_ffn_conv_w, v_ffn_conv_b=v_ffn_conv_b, v_w_down=v_w_down, v_final_norm_w=v_final_norm_w)
    weights = {n: given[n] for n in TWIN_WEIGHTS}
    shared = {n: given[n] for n in SHARED_INPUTS}
    per_example = {n: given[n] for n in ['x']}
    grad_fn = _jax.value_and_grad(_loss, argnums=(0, 1))

    def one_microbatch(ex, loss_target):
        ex = dict(ex)
        diff = ex.pop(TWIN_DIFF_INPUT)
        return grad_fn(weights, diff, {**shared, **ex}, loss_target)

    if N_MICROBATCH == 1:
        loss, (grad_w, grad_x) = one_microbatch(per_example, given["loss_target"])
    else:
        def body(carry, xs):
            loss_sum, grad_sum = carry
            l_k, (gw_k, gx_k) = one_microbatch(xs[0], xs[1])
            with _jax.named_scope("update"):
                return (loss_sum + l_k, _jax.tree.map(_jnp.add, grad_sum, gw_k)), gx_k

        init = (_jnp.zeros((), _jnp.float32), _jax.tree.map(_jnp.zeros_like, weights))
        (loss, grad_w), grad_x = _jax.lax.scan(body, init, (per_example, given["loss_target"]))
    with _jax.named_scope("update"):
        delta_w, new_m, new_v = {}, {}, {}
        for n in TWIN_WEIGHTS:
            delta_w[n], new_m[n], new_v[n] = _adamw(weights[n], grad_w[n], given["m_" + n], given["v_" + n])
    return (loss, grad_x, *[grad_w[n] for n in TWIN_WEIGHTS], *[delta_w[n] for n in TWIN_WEIGHTS],
            *[new_m[n] for n in TWIN_WEIGHTS], *[new_v[n] for n in TWIN_WEIGHTS])
```

```python
import functools

import jax
import jax.numpy as jnp
from jax import lax
from jax.experimental import pallas as pl
from jax.experimental.pallas import tpu as pltpu

F32 = jnp.float32
BF16 = jnp.bfloat16
MESH = pl.DeviceIdType.MESH

D = 1024
NH = 8
HD = 128
DFF = 2816
NMETA = 16
EPS = 1e-6
TR = 256
PAD = TR - NMETA
CH = 64
NSHARD = 4
VMEM_LIMIT = 56 * 1024 * 1024

ADAM_LR = 0.001
ADAM_B1 = 0.9
ADAM_B2 = 0.999
ADAM_EPS = 1e-08
ADAM_WD = 0.01
ADAM_STEP = 10


def _cparams(semantics=None, **kw):
    return pltpu.CompilerParams(dimension_semantics=semantics, vmem_limit_bytes=VMEM_LIMIT, **kw)


def _sigmoid(x):
    return 1.0 / (1.0 + jnp.exp(-x))


def _matmul(a, b, *, mode, bm, bn, bk, out_dtype, name, residual=None, out_shards=None):
    if mode == "tn":
        K, M = a.shape
    else:
        M, K = a.shape
    b3 = b.ndim == 3
    if b3:
        S, R, Cs = b.shape
        bcols = S * Cs
        brows = R
    else:
        brows, bcols = b.shape
    if mode == "nt":
        N = brows
        assert bcols == K
    else:
        N = bcols
        assert brows == K
    assert M % bm == 0 and N % bn == 0 and K % bk == 0, (name, M, N, K, bm, bn, bk)
    nm, nn, nk = M // bm, N // bn, K // bk

    if mode == "tn":
        a_spec = pl.BlockSpec((bk, bm), lambda i, j, k: (k, i))
    else:
        a_spec = pl.BlockSpec((bm, bk), lambda i, j, k: (i, k))
    if mode == "nt":
        if b3:
            assert bn == R and Cs % bk == 0
            per = Cs // bk
            b_spec = pl.BlockSpec((None, bn, bk), lambda i, j, k: (k // per, 0, k % per))
        else:
            b_spec = pl.BlockSpec((bn, bk), lambda i, j, k: (j, k))
    else:
        if b3:
            assert bk == R and Cs % bn == 0
            per = Cs // bn
            b_spec = pl.BlockSpec((None, bk, bn), lambda i, j, k: (j // per, 0, j % per))
        else:
            b_spec = pl.BlockSpec((bk, bn), lambda i, j, k: (k, j))
    in_specs = [a_spec, b_spec]
    args = [a, b]
    if residual is not None:
        in_specs.append(pl.BlockSpec((bm, bn), lambda i, j, k: (i, j)))
        args.append(residual)
    if out_shards is not None:
        So, Co = out_shards
        assert So * Co == N and Co % bn == 0 and bm == M
        pero = Co // bn
        out_shape = jax.ShapeDtypeStruct((So, M, Co), out_dtype)
        out_spec = pl.BlockSpec((None, bm, bn), lambda i, j, k: (j // pero, 0, j % pero))
    else:
        out_shape = jax.ShapeDtypeStruct((M, N), out_dtype)
        out_spec = pl.BlockSpec((bm, bn), lambda i, j, k: (i, j))

    def body(*refs):
        if residual is not None:
            a_ref, b_ref, r_ref, o_ref = refs[:4]
            scr = refs[4:]
        else:
            a_ref, b_ref, o_ref = refs[:3]
            r_ref = None
            scr = refs[3:]
        av = a_ref[...].astype(BF16)
        bv = b_ref[...].astype(BF16)
        if mode == "nn":
            p = jnp.dot(av, bv, preferred_element_type=F32)
        elif mode == "nt":
            p = lax.dot_general(av, bv, (((1,), (1,)), ((), ())), preferred_element_type=F32)
        else:
            p = lax.dot_general(av, bv, (((0,), (0,)), ((), ())), preferred_element_type=F32)

        def finish(acc):
            if r_ref is not None:
                acc = acc + r_ref[...]
            o_ref[...] = acc.astype(out_dtype)

        if nk == 1:
            finish(p)
        else:
            acc_ref = scr[0]
            k = pl.program_id(2)

            @pl.when(k == 0)
            def _():
                acc_ref[...] = p

            @pl.when(k > 0)
            def _():
                acc_ref[...] += p

            @pl.when(k == nk - 1)
            def _():
                finish(acc_ref[...])

    scratch = [] if nk == 1 else [pltpu.VMEM((bm, bn), F32)]
    return pl.pallas_call(
        body, name=name, grid=(nm, nn, nk), in_specs=in_specs, out_specs=out_spec, out_shape=out_shape,
        scratch_shapes=scratch, compiler_params=_cparams(("parallel", "parallel", "arbitrary")),
    )(*args)


def _norm_fwd(h, w, *, bm, name):
    T = h.shape[0]

    def body(h_ref, w_ref, o_ref):
        x = h_ref[...]
        r = lax.rsqrt(jnp.mean(x * x, axis=-1, keepdims=True) + EPS)
        o_ref[...] = (x * r * w_ref[...]).astype(BF16)

    return pl.pallas_call(
        body, name=name, grid=(T // bm,),
        in_specs=[pl.BlockSpec((bm, D), lambda i: (i, 0)), pl.BlockSpec((1, D), lambda i: (0, 0))],
        out_specs=pl.BlockSpec((bm, D), lambda i: (i, 0)),
        out_shape=jax.ShapeDtypeStruct((T, D), BF16), compiler_params=_cparams(("parallel",)),
    )(h, w)


def _norm_bwd(h, du, dres, w, *, bm, name):
    T = h.shape[0]

    def body(h_ref, du_ref, dres_ref, w_ref, dh_ref, dw_ref):
        i = pl.program_id(0)
        x = h_ref[...]
        r = lax.rsqrt(jnp.mean(x * x, axis=-1, keepdims=True) + EPS)
        n = x * r
        du_v = du_ref[...]
        dn = du_v * w_ref[...]
        dh_ref[...] = dres_ref[...] + r * (dn - n * jnp.mean(dn * n, axis=-1, keepdims=True))
        part = jnp.sum(du_v * n, axis=0, keepdims=True)

        @pl.when(i == 0)
        def _():
            dw_ref[...] = part

        @pl.when(i > 0)
        def _():
            dw_ref[...] += part

    row = pl.BlockSpec((bm, D), lambda i: (i, 0))
    vec = pl.BlockSpec((1, D), lambda i: (0, 0))
    return pl.pallas_call(
        body, name=name, grid=(T // bm,), in_specs=[row, row, row, vec], out_specs=[row, vec],
        out_shape=[jax.ShapeDtypeStruct((T, D), F32), jax.ShapeDtypeStruct((1, D), F32)],
        compiler_params=_cparams(("arbitrary",)),
    )(h, du, dres, w)


def _split3(x):
    hi = x.astype(BF16)
    r1 = x - hi.astype(F32)
    mid = r1.astype(BF16)
    lo = (r1 - mid.astype(F32)).astype(BF16)
    return hi, mid, lo


def _tri_matmul(tri_bf16, x):
    hi, mid, lo = _split3(x)
    out = jnp.dot(tri_bf16, lo, preferred_element_type=F32)
    out = out + jnp.dot(tri_bf16, mid, preferred_element_type=F32)
    return out + jnp.dot(tri_bf16, hi, preferred_element_type=F32)


def _shift_down(x, prev8, n):
    rows = x.shape[0]
    rid = lax.broadcasted_iota(jnp.int32, x.shape, 0)
    y = pltpu.roll(x, n, 0)
    for t in range(n):
        y = jnp.where(rid == t, prev8[8 - n + t:8 - n + t + 1, :], y)
    del rows
    return y


def _shift_up(x, next8, n):
    rows = x.shape[0]
    rid = lax.broadcasted_iota(jnp.int32, x.shape, 0)
    y = pltpu.roll(x, rows - n, 0)
    for t in range(n):
        y = jnp.where(rid == rows - n + t, next8[t:t + 1, :], y)
    return y


def _gates(f_raw, lb):
    sg = _sigmoid(f_raw)
    f = lb + (1.0 - lb) * sg
    return sg, f, jnp.log(f), 1.0 - f


def _lower_bound(lbp_ref):
    return _sigmoid(lbp_ref[0:1, :] - lbp_ref[1:2, :])


def _tri_masks():
    r = lax.broadcasted_iota(jnp.int32, (CH, CH), 0)
    c = lax.broadcasted_iota(jnp.int32, (CH, CH), 1)
    return r >= c, r <= c


def _ones_where(mask):
    return jnp.where(mask, 1.0, 0.0).astype(BF16)


def _dot(a, b):
    return jnp.dot(a.astype(BF16), b.astype(BF16), preferred_element_type=F32)


def _dot_nt(a, b):
    return lax.dot_general(a.astype(BF16), b.astype(BF16), (((1,), (1,)), ((), ())), preferred_element_type=F32)


def _dot_tn(a, b):
    return lax.dot_general(a.astype(BF16), b.astype(BF16), (((0,), (0,)), ((), ())), preferred_element_type=F32)


def _mix_fwd(proj, lb_param, whn, conv_w):
    T = proj.shape[0]
    nt = T // TR
    ncht = TR // CH

    def body(proj_ref, lbp_ref, whn_ref, cw_ref, m_ref, o_ref, sst_ref, st_ref, cxc_ref):
        i = pl.program_id(0)

        @pl.when(i == 0)
        def _():
            st_ref[...] = jnp.zeros_like(st_ref)
            cxc_ref[...] = jnp.zeros_like(cxc_ref)

        lb = _lower_bound(lbp_ref)
        lower, _ = _tri_masks()
        ltri = _ones_where(lower)
        whn_v = whn_ref[...]
        w0, w1, w2 = cw_ref[0:1, :], cw_ref[1:2, :], cw_ref[2:3, :]

        def chunk(c, carry):
            rows = pl.ds(pl.multiple_of(c * CH, CH), CH)
            q_raw = proj_ref[rows, 0:D]
            f_raw = proj_ref[rows, D:2 * D]
            v = proj_ref[rows, 2 * D:3 * D]
            q = q_raw * _sigmoid(q_raw)
            _, _, g, k = _gates(f_raw, lb)
            gam = _tri_matmul(ltri, g)
            gam_l = gam[CH - 1:CH, :]
            e_l = jnp.exp(gam_l)
            qt = (q * jnp.exp(gam)).astype(BF16)
            kt = (k * jnp.exp(-gam)).astype(BF16)
            khat = (k * jnp.exp(gam_l - gam)).astype(BF16)
            vb = v.astype(BF16)
            on_parts = []
            for h in range(NH):
                cs = slice(h * HD, (h + 1) * HD)
                st = st_ref[h]
                sst_ref[c, h] = st
                a = jnp.where(lower, _dot_nt(qt[:, cs], kt[:, cs]), 0.0)
                o_h = _dot_nt(qt[:, cs], st) + _dot(a, vb[:, cs])
                st_ref[h] = st * e_l[:, cs] + _dot_tn(vb[:, cs], khat[:, cs])
                o_ref[rows, cs] = o_h
                ro = lax.rsqrt(jnp.mean(o_h * o_h, axis=-1, keepdims=True) + EPS)
                on_parts.append(o_h * ro)
            on = jnp.concatenate(on_parts, axis=1)
            g_out = proj_ref[rows, 3 * D:4 * D]
            y_a = on * whn_v * (g_out * _sigmoid(g_out))
            cx = proj_ref[rows, 5 * D:6 * D] * proj_ref[rows, 6 * D:7 * D]
            prev8 = cxc_ref[...]
            cv = w0 * _shift_down(cx, prev8, 2) + w1 * _shift_down(cx, prev8, 1) + w2 * cx
            cxc_ref[...] = cx[CH - 8:CH, :]
            y_b = proj_ref[rows, 4 * D:5 * D] * cv
            m = _sigmoid(proj_ref[rows, 7 * D:8 * D]) * y_a + _sigmoid(proj_ref[rows, 8 * D:9 * D]) * y_b
            m_ref[rows, :] = m.astype(BF16)
            return carry

        lax.fori_loop(0, ncht, chunk, 0)

    return pl.pallas_call(
        body, name="mix_fwd", grid=(nt,),
        in_specs=[pl.BlockSpec((TR, 9 * D), lambda i: (i, 0)),
                  pl.BlockSpec((2, D), lambda i: (0, 0)),
                  pl.BlockSpec((1, D), lambda i: (0, 0)),
                  pl.BlockSpec((3, D), lambda i: (0, 0))],
        out_specs=[pl.BlockSpec((TR, D), lambda i: (i, 0)),
                   pl.BlockSpec((TR, D), lambda i: (i, 0)),
                   pl.BlockSpec((ncht, NH, HD, HD), lambda i: (i, 0, 0, 0))],
        out_shape=[jax.ShapeDtypeStruct((T, D), BF16), jax.ShapeDtypeStruct((T, D), F32),
                   jax.ShapeDtypeStruct((T // CH, NH, HD, HD), F32)],
        scratch_shapes=[pltpu.VMEM((NH, HD, HD), F32), pltpu.VMEM((8, D), F32)],
        compiler_params=_cparams(("arbitrary",)),
    )(proj, lb_param, whn, conv_w)


def _mix_bwd(proj, o, dm, sst, lb_param, whn, conv_w):
    T = proj.shape[0]
    nt = T // TR
    ncht = TR // CH
    tb = TR // 8

    def body(proj_ref, pc_ref, px_ref, o_ref, dm_ref, sst_ref, lbp_ref, whn_ref, cw_ref,
             dproj_ref, dlb_ref, dwhn_ref, dcw_ref, dst_ref, dcvc_ref, acc_lb, acc_hn, acc_cw):
        s = pl.program_id(0)
        tile = nt - 1 - s

        @pl.when(s == 0)
        def _():
            dst_ref[...] = jnp.zeros_like(dst_ref)
            dcvc_ref[...] = jnp.zeros_like(dcvc_ref)
            acc_lb[...] = jnp.zeros_like(acc_lb)
            acc_hn[...] = jnp.zeros_like(acc_hn)
            acc_cw[...] = jnp.zeros_like(acc_cw)

        lb = _lower_bound(lbp_ref)
        lower, upper = _tri_masks()
        ltri = _ones_where(lower)
        utri = _ones_where(upper)
        whn_v = whn_ref[...]
        w0, w1, w2 = cw_ref[0:1, :], cw_ref[1:2, :], cw_ref[2:3, :]
        cx_before_tile = jnp.where(tile > 0, pc_ref[...] * px_ref[...], 0.0)
        rid = lax.broadcasted_iota(jnp.int32, (CH, D), 0)

        def chunk(cc, carry):
            c = ncht - 1 - cc
            r0 = pl.multiple_of(c * CH, CH)
            rows = pl.ds(r0, CH)
            q_raw = proj_ref[rows, 0:D]
            f_raw = proj_ref[rows, D:2 * D]
            v = proj_ref[rows, 2 * D:3 * D]
            g_out = proj_ref[rows, 3 * D:4 * D]
            b_gate = proj_ref[rows, 4 * D:5 * D]
            c_gate = proj_ref[rows, 5 * D:6 * D]
            x_conv = proj_ref[rows, 6 * D:7 * D]
            sa = _sigmoid(proj_ref[rows, 7 * D:8 * D])
            sb = _sigmoid(proj_ref[rows, 8 * D:9 * D])
            dm_v = dm_ref[rows, :]

            sq = _sigmoid(q_raw)
            q = q_raw * sq
            sg, f, g, k = _gates(f_raw, lb)
            gam = _tri_matmul(ltri, g)
            gam_l = gam[CH - 1:CH, :]
            e_l = jnp.exp(gam_l)
            e_g = jnp.exp(gam)
            e_ng = jnp.exp(-gam)
            e_kl = jnp.exp(gam_l - gam)
            qt = q * e_g
            kt = k * e_ng
            khat = k * e_kl
            s_go = _sigmoid(g_out)
            silu_go = g_out * s_go
            cx = c_gate * x_conv
            rprev = pl.multiple_of(jnp.maximum(r0 - 8, 0), 8)
            cx_prev_in = proj_ref[pl.ds(rprev, 8), 5 * D:6 * D] * proj_ref[pl.ds(rprev, 8), 6 * D:7 * D]
            prev8 = jnp.where(c > 0, cx_prev_in, cx_before_tile)
            cx_m1 = _shift_down(cx, prev8, 1)
            cx_m2 = _shift_down(cx, prev8, 2)
            cv = w0 * cx_m2 + w1 * cx_m1 + w2 * cx
            y_b = b_gate * cv

            o_v = o_ref[rows, :]
            ro_parts, on_parts = [], []
            for h in range(NH):
                cs = slice(h * HD, (h + 1) * HD)
                o_h = o_v[:, cs]
                ro = lax.rsqrt(jnp.mean(o_h * o_h, axis=-1, keepdims=True) + EPS)
                ro_parts.append(ro)
                on_parts.append(o_h * ro)
            on = jnp.concatenate(on_parts, axis=1)
            y_a = on * whn_v * silu_go

            dy_a = dm_v * sa
            dy_b = dm_v * sb
            dproj_ref[rows, 7 * D:8 * D] = (dm_v * y_a * sa * (1.0 - sa)).astype(BF16)
            dproj_ref[rows, 8 * D:9 * D] = (dm_v * y_b * sb * (1.0 - sb)).astype(BF16)
            dproj_ref[rows, 4 * D:5 * D] = (dy_b * cv).astype(BF16)
            dcv = dy_b * b_gate
            acc_cw[0:1, :] += jnp.sum(dcv * cx_m2, axis=0, keepdims=True)
            acc_cw[1:2, :] += jnp.sum(dcv * cx_m1, axis=0, keepdims=True)
            acc_cw[2:3, :] += jnp.sum(dcv * cx, axis=0, keepdims=True)
            next8 = dcvc_ref[...]
            dcx = w2 * dcv + w1 * _shift_up(dcv, next8, 1) + w0 * _shift_up(dcv, next8, 2)
            dcvc_ref[...] = dcv[0:8, :]
            dproj_ref[rows, 5 * D:6 * D] = (dcx * x_conv).astype(BF16)
            dproj_ref[rows, 6 * D:7 * D] = (dcx * c_gate).astype(BF16)
            don = dy_a * whn_v * silu_go
            dproj_ref[rows, 3 * D:4 * D] = (dy_a * on * whn_v * (s_go * (1.0 + g_out * (1.0 - s_go)))).astype(BF16)
            acc_hn[...] += jnp.sum(dy_a * silu_go * on, axis=0, keepdims=True)

            dq_parts, dk_parts, dv_parts, dgam_parts, ext_parts = [], [], [], [], []
            for h in range(NH):
                cs = slice(h * HD, (h + 1) * HD)
                on_h = on_parts[h]
                don_h = don[:, cs]
                do_h = ro_parts[h] * (don_h - on_h * jnp.mean(don_h * on_h, axis=-1, keepdims=True))
                qt_h, kt_h, khat_h, v_h = qt[:, cs], kt[:, cs], khat[:, cs], v[:, cs]
                st = sst_ref[c, h]
                dstn = dst_ref[h]
                a_t = jnp.where(upper, _dot_nt(kt_h, qt_h), 0.0)
                da = jnp.where(lower, _dot_nt(do_h, v_h), 0.0)
                da_t = jnp.where(upper, _dot_nt(v_h, do_h), 0.0)
                dv_h = _dot(a_t, do_h) + _dot_nt(khat_h, dstn)
                dqt_state = _dot(do_h, st)
                dqt_chunk = _dot(da, kt_h)
                dkt = _dot(da_t, qt_h)
                dkhat = _dot(v_h, dstn)
                dq_h = (dqt_state + dqt_chunk) * e_g[:, cs]
                dk_h = dkt * e_ng[:, cs] + dkhat * e_kl[:, cs]
                khat_dkhat = dkhat * khat_h
                ext = (jnp.sum(khat_dkhat, axis=0, keepdims=True)
                       + e_l[:, cs] * jnp.sum(st * dstn, axis=0, keepdims=True))
                dst_ref[h] = _dot_tn(do_h, qt_h) + dstn * e_l[:, cs]
                dq_parts.append(dq_h)
                dk_parts.append(dk_h)
                dv_parts.append(dv_h)
                qt_seen = qt_h.astype(BF16).astype(F32)
                kt_seen = kt_h.astype(BF16).astype(F32)
                dgam_parts.append(qt_h * dqt_state + qt_seen * dqt_chunk - kt_seen * dkt - khat_dkhat)
                ext_parts.append(ext)
            dq = jnp.concatenate(dq_parts, axis=1)
            dk = jnp.concatenate(dk_parts, axis=1)
            dgam = jnp.concatenate(dgam_parts, axis=1)
            ext = jnp.concatenate(ext_parts, axis=1)
            dgam = dgam + jnp.where(rid == CH - 1, ext, 0.0)
            dg = _tri_matmul(utri, dgam)
            dproj_ref[rows, 0:D] = (dq * (sq * (1.0 + q_raw * (1.0 - sq)))).astype(BF16)
            df = dg / f - dk
            dproj_ref[rows, D:2 * D] = (df * (1.0 - lb) * sg * (1.0 - sg)).astype(BF16)
            dproj_ref[rows, 2 * D:3 * D] = jnp.concatenate(dv_parts, axis=1).astype(BF16)
            real = (tile * TR + r0 + rid) >= PAD
            acc_lb[...] += jnp.sum(jnp.where(real, df * (1.0 - sg), 0.0), axis=0, keepdims=True)
            return carry

        lax.fori_loop(0, ncht, chunk, 0)

        @pl.when(s == nt - 1)
        def _():
            dlb_ref[...] = acc_lb[...] * lb * (1.0 - lb)
            hn = acc_hn[...]
            tot = hn[:, 0:HD]
            for h in range(1, NH):
                tot = tot + hn[:, h * HD:(h + 1) * HD]
            dwhn_ref[...] = tot
            dcw_ref[...] = acc_cw[...]

    rev = lambda s: (nt - 1 - s, 0)
    prevc = lambda s: (jnp.maximum((nt - 1 - s) * tb - 1, 0), 5)
    prevx = lambda s: (jnp.maximum((nt - 1 - s) * tb - 1, 0), 6)
    const = lambda s: (0, 0)
    return pl.pallas_call(
        body, name="mix_bwd", grid=(nt,),
        in_specs=[pl.BlockSpec((TR, 9 * D), rev),
                  pl.BlockSpec((8, D), prevc),
                  pl.BlockSpec((8, D), prevx),
                  pl.BlockSpec((TR, D), rev),
                  pl.BlockSpec((TR, D), rev),
                  pl.BlockSpec((ncht, NH, HD, HD), lambda s: (nt - 1 - s, 0, 0, 0)),
                  pl.BlockSpec((2, D), const),
                  pl.BlockSpec((1, D), const),
                  pl.BlockSpec((3, D), const)],
        out_specs=[pl.BlockSpec((TR, 9 * D), rev),
                   pl.BlockSpec((1, D), const),
                   pl.BlockSpec((1, HD), const),
                   pl.BlockSpec((8, D), const)],
        out_shape=[jax.ShapeDtypeStruct((T, 9 * D), BF16), jax.ShapeDtypeStruct((1, D), F32),
                   jax.ShapeDtypeStruct((1, HD), F32), jax.ShapeDtypeStruct((8, D), F32)],
        scratch_shapes=[pltpu.VMEM((NH, HD, HD), F32), pltpu.VMEM((8, D), F32),
                        pltpu.VMEM((1, D), F32), pltpu.VMEM((1, D), F32), pltpu.VMEM((8, D), F32)],
        compiler_params=_cparams(("arbitrary",)),
    )(proj, proj, proj, o, dm, sst, lb_param, whn, conv_w)


def _ffn_fwd(up, fcw, fcb):
    T = up.shape[0]
    nt = T // TR

    def body(up_ref, w_ref, b_ref, gg_ref, carry_ref):
        i = pl.program_id(0)

        @pl.when(i == 0)
        def _():
            carry_ref[...] = jnp.zeros_like(carry_ref)

        w0, w1, w2 = w_ref[0:1, :], w_ref[1:2, :], w_ref[2:3, :]

        def chunk(c, carry):
            rows = pl.ds(pl.multiple_of(c * CH, CH), CH)
            a_pre = up_ref[rows, 0:DFF]
            val = up_ref[rows, DFF:2 * DFF]
            prev8 = carry_ref[...]
            a = w0 * _shift_down(a_pre, prev8, 2) + w1 * _shift_down(a_pre, prev8, 1) + w2 * a_pre + b_ref[...]
            carry_ref[...] = a_pre[CH - 8:CH, :]
            gg_ref[rows, :] = (a * _sigmoid(a) * val).astype(BF16)
            return carry

        lax.fori_loop(0, TR // CH, chunk, 0)

    return pl.pallas_call(
        body, name="ffn_fwd", grid=(nt,),
        in_specs=[pl.BlockSpec((TR, 2 * DFF), lambda i: (i, 0)),
                  pl.BlockSpec((3, DFF), lambda i: (0, 0)),
                  pl.BlockSpec((1, DFF), lambda i: (0, 0))],
        out_specs=pl.BlockSpec((TR, DFF), lambda i: (i, 0)),
        out_shape=jax.ShapeDtypeStruct((T, DFF), BF16),
        scratch_shapes=[pltpu.VMEM((8, DFF), F32)],
        compiler_params=_cparams(("arbitrary",)),
    )(up, fcw, fcb)


def _ffn_bwd(up, dgg, fcw, fcb):
    T = up.shape[0]
    nt = T // TR
    tb = TR // 8
    ncht = TR // CH

    def body(up_ref, pa_ref, dgg_ref, w_ref, b_ref, dup_ref, dfw_ref, dfb_ref, carry_ref, acc_w, acc_b):
        s = pl.program_id(0)
        tile = nt - 1 - s

        @pl.when(s == 0)
        def _():
            carry_ref[...] = jnp.zeros_like(carry_ref)
            acc_w[...] = jnp.zeros_like(acc_w)
            acc_b[...] = jnp.zeros_like(acc_b)

        w0, w1, w2 = w_ref[0:1, :], w_ref[1:2, :], w_ref[2:3, :]
        a_before_tile = jnp.where(tile > 0, pa_ref[...], 0.0)

        def chunk(cc, carry):
            c = ncht - 1 - cc
            r0 = pl.multiple_of(c * CH, CH)
            rows = pl.ds(r0, CH)
            a_pre = up_ref[rows, 0:DFF]
            val = up_ref[rows, DFF:2 * DFF]
            rprev = pl.multiple_of(jnp.maximum(r0 - 8, 0), 8)
            prev8 = jnp.where(c > 0, up_ref[pl.ds(rprev, 8), 0:DFF], a_before_tile)
            a_m1 = _shift_down(a_pre, prev8, 1)
            a_m2 = _shift_down(a_pre, prev8, 2)
            a = w0 * a_m2 + w1 * a_m1 + w2 * a_pre + b_ref[...]
            sig = _sigmoid(a)
            dgg_v = dgg_ref[rows, :]
            da = dgg_v * val * (sig * (1.0 + a * (1.0 - sig)))
            dup_ref[rows, DFF:2 * DFF] = (dgg_v * (a * sig)).astype(BF16)
            next8 = carry_ref[...]
            dup_ref[rows, 0:DFF] = (w2 * da + w1 * _shift_up(da, next8, 1) + w0 * _shift_up(da, next8, 2)).astype(BF16)
            carry_ref[...] = da[0:8, :]
            acc_w[0:1, :] += jnp.sum(da * a_m2, axis=0, keepdims=True)
            acc_w[1:2, :] += jnp.sum(da * a_m1, axis=0, keepdims=True)
            acc_w[2:3, :] += jnp.sum(da * a_pre, axis=0, keepdims=True)
            acc_b[...] += jnp.sum(da, axis=0, keepdims=True)
            return carry

        lax.fori_loop(0, ncht, chunk, 0)

        @pl.when(s == nt - 1)
        def _():
            dfw_ref[...] = acc_w[...]
            dfb_ref[...] = acc_b[...]

    rev = lambda s: (nt - 1 - s, 0)
    const = lambda s: (0, 0)
    return pl.pallas_call(
        body, name="ffn_bwd", grid=(nt,),
        in_specs=[pl.BlockSpec((TR, 2 * DFF), rev),
                  pl.BlockSpec((8, DFF), lambda s: (jnp.maximum((nt - 1 - s) * tb - 1, 0), 0)),
                  pl.BlockSpec((TR, DFF), rev),
                  pl.BlockSpec((3, DFF), const),
                  pl.BlockSpec((1, DFF), const)],
        out_specs=[pl.BlockSpec((TR, 2 * DFF), rev),
                   pl.BlockSpec((8, DFF), const),
                   pl.BlockSpec((1, DFF), const)],
        out_shape=[jax.ShapeDtypeStruct((T, 2 * DFF), BF16), jax.ShapeDtypeStruct((8, DFF), F32),
                   jax.ShapeDtypeStruct((1, DFF), F32)],
        scratch_shapes=[pltpu.VMEM((8, DFF), F32), pltpu.VMEM((8, DFF), F32), pltpu.VMEM((1, DFF), F32)],
        compiler_params=_cparams(("arbitrary",)),
    )(up, up, dgg, fcw, fcb)


def _loss_head(h2, tgt, wfin):
    T = h2.shape[0]
    nt = T // TR

    def body(h_ref, t_ref, w_ref, dh_ref, loss_ref, dw_ref):
        i = pl.program_id(0)

        @pl.when(i == 0)
        def _():
            loss_ref[...] = jnp.zeros_like(loss_ref)
            dw_ref[...] = jnp.zeros_like(dw_ref)

        x = h_ref[...]
        r = lax.rsqrt(jnp.mean(x * x, axis=-1, keepdims=True) + EPS)
        n = x * r
        w = w_ref[...]
        diff = jnp.where(i > 0, n * w - t_ref[...], 0.0)
        loss_ref[...] += jnp.sum(diff * diff, axis=0, keepdims=True) * (0.5 / D)
        dy = diff * (1.0 / D)
        dw_ref[...] += jnp.sum(dy * n, axis=0, keepdims=True)
        dn = dy * w
        dh_ref[...] = r * (dn - n * jnp.mean(dn * n, axis=-1, keepdims=True))

    row = pl.BlockSpec((TR, D), lambda i: (i, 0))
    vec = pl.BlockSpec((1, D), lambda i: (0, 0))
    return pl.pallas_call(
        body, name="loss_head", grid=(nt,),
        in_specs=[row, pl.BlockSpec((TR, D), lambda i: (jnp.maximum(i - 1, 0), 0)), vec],
        out_specs=[row, vec, vec],
        out_shape=[jax.ShapeDtypeStruct((T, D), F32), jax.ShapeDtypeStruct((1, D), F32),
                   jax.ShapeDtypeStruct((1, D), F32)],
        compiler_params=_cparams(("arbitrary",)),
    )(h2, tgt, wfin)


def _place():
    x, y, c = lax.axis_index("x"), lax.axis_index("y"), lax.axis_index("c")
    return x, y, c


_CHIP_FLIPS = ((1, 0), (0, 1), (1, 1))


def _flip(v, bit):
    return 1 - v if bit else v


def _allgather_weights(shards, meta):
    nw = len(shards)
    ncopy = nw + 1

    def body(*refs):
        ins, outs = refs[:ncopy], refs[ncopy:2 * ncopy]
        send_sems, recv_sems, local_sems = refs[2 * ncopy:]
        x, y, c = _place()
        j = 2 * x + y
        sibling = (x, y, 1 - c)
        started = []
        for w in range(ncopy):
            lc = pltpu.make_async_copy(ins[w], outs[w].at[j], local_sems.at[w])
            lc.start()
            started.append(lc)
        sends = []
        for w in range(ncopy):
            r = ins[w].shape[0]
            half = r // 2 if w < nw else r
            off = c * half if w < nw else 0
            for kk, (fx, fy) in enumerate(_CHIP_FLIPS):
                cp = pltpu.make_async_remote_copy(
                    src_ref=ins[w].at[pl.ds(off, half), :], dst_ref=outs[w].at[j, pl.ds(off, half), :],
                    send_sem=send_sems.at[w, kk], recv_sem=recv_sems.at[w, kk],
                    device_id=(_flip(x, fx), _flip(y, fy), c), device_id_type=MESH)
                cp.start()
                sends.append(cp)
        for w in range(nw):
            half = ins[w].shape[0] // 2
            off = c * half
            for kk, (fx, fy) in enumerate(_CHIP_FLIPS):
                jk = 2 * _flip(x, fx) + _flip(y, fy)
                piece = outs[w].at[jk, pl.ds(off, half), :]
                pltpu.make_async_remote_copy(
                    src_ref=piece, dst_ref=piece, send_sem=send_sems.at[w, kk], recv_sem=recv_sems.at[w, kk],
                    device_id=sibling, device_id_type=MESH).wait_recv()
                fw = pltpu.make_async_remote_copy(
                    src_ref=piece, dst_ref=piece, send_sem=send_sems.at[w, 3 + kk], recv_sem=recv_sems.at[w, 3 + kk],
                    device_id=sibling, device_id_type=MESH)
                fw.start()
                sends.append(fw)
        for kk, (fx, fy) in enumerate(_CHIP_FLIPS):
            jk = 2 * _flip(x, fx) + _flip(y, fy)
            piece = outs[nw].at[jk]
            pltpu.make_async_remote_copy(
                src_ref=piece, dst_ref=piece, send_sem=send_sems.at[nw, kk], recv_sem=recv_sems.at[nw, kk],
                device_id=sibling, device_id_type=MESH).wait_recv()
        for w in range(nw):
            half = ins[w].shape[0] // 2
            off = (1 - c) * half
            for kk, (fx, fy) in enumerate(_CHIP_FLIPS):
                jk = 2 * _flip(x, fx) + _flip(y, fy)
                piece = outs[w].at[jk, pl.ds(off, half), :]
                pltpu.make_async_remote_copy(
                    src_ref=piece, dst_ref=piece, send_sem=send_sems.at[w, 3 + kk], recv_sem=recv_sems.at[w, 3 + kk],
                    device_id=sibling, device_id_type=MESH).wait_recv()
        for cp in sends:
            cp.wait_send()
        for lc in started:
            lc.wait()

    arrays = list(shards) + [meta]
    any_spec = pl.BlockSpec(memory_space=pl.ANY)
    return pl.pallas_call(
        body, name="allgather_weights",
        in_specs=[any_spec] * ncopy, out_specs=[any_spec] * ncopy,
        out_shape=[jax.ShapeDtypeStruct((NSHARD,) + a.shape, a.dtype) for a in arrays],
        scratch_shapes=[pltpu.SemaphoreType.DMA((ncopy, 6)), pltpu.SemaphoreType.DMA((ncopy, 6)),
                        pltpu.SemaphoreType.DMA((ncopy,))],
    )(*arrays)


def _pair_exchange(grads):
    nw = len(grads)

    def body(*refs):
        ins, outs = refs[:nw], refs[nw:2 * nw]
        send_sems, recv_sems = refs[2 * nw:]
        x, y, c = _place()
        sibling = (x, y, 1 - c)
        cps = []
        for w in range(nw):
            half = ins[w].shape[1] // 2
            cp = pltpu.make_async_remote_copy(
                src_ref=ins[w].at[:, pl.ds((1 - c) * half, half), :], dst_ref=outs[w],
                send_sem=send_sems.at[w], recv_sem=recv_sems.at[w], device_id=sibling, device_id_type=MESH)
            cp.start()
            cps.append(cp)
        for cp in cps:
            cp.wait()

    any_spec = pl.BlockSpec(memory_space=pl.ANY)
    return pl.pallas_call(
        body, name="grad_pair_exchange",
        in_specs=[any_spec] * nw, out_specs=[any_spec] * nw,
        out_shape=[jax.ShapeDtypeStruct((g.shape[0], g.shape[1] // 2, g.shape[2]), g.dtype) for g in grads],
        scratch_shapes=[pltpu.SemaphoreType.DMA((nw,)), pltpu.SemaphoreType.DMA((nw,))],
    )(*grads)


def _pair_add(g, other, c_idx, *, rb, name):
    S, r, cdim = g.shape
    half = r // 2
    nb = half // rb

    def body(c_ref, g_ref, o_ref, out_ref):
        del c_ref
        out_ref[...] = g_ref[...] + o_ref[...]

    grid_spec = pltpu.PrefetchScalarGridSpec(
        num_scalar_prefetch=1, grid=(S, nb),
        in_specs=[pl.BlockSpec((None, rb, cdim), lambda s, i, c_ref: (s, c_ref[0] * nb + i, 0)),
                  pl.BlockSpec((None, rb, cdim), lambda s, i, c_ref: (s, i, 0))],
        out_specs=pl.BlockSpec((None, rb, cdim), lambda s, i, c_ref: (s, i, 0)))
    return pl.pallas_call(
        body, name=name, grid_spec=grid_spec, out_shape=jax.ShapeDtypeStruct((S, half, cdim), F32),
        compiler_params=_cparams(("parallel", "parallel")),
    )(c_idx, g, other)


def _chip_exchange(psums):
    nw = len(psums)

    def body(*refs):
        ins, outs = refs[:nw], refs[nw:2 * nw]
        send_sems, recv_sems, local_sems = refs[2 * nw:]
        x, y, c = _place()
        j = 2 * x + y
        cps, lcs = [], []
        for w in range(nw):
            lc = pltpu.make_async_copy(ins[w].at[j], outs[w].at[j], local_sems.at[w])
            lc.start()
            lcs.append(lc)
            for kk, (fx, fy) in enumerate(_CHIP_FLIPS):
                px, py = _flip(x, fx), _flip(y, fy)
                cp = pltpu.make_async_remote_copy(
                    src_ref=ins[w].at[2 * px + py], dst_ref=outs[w].at[j],
                    send_sem=send_sems.at[w, kk], recv_sem=recv_sems.at[w, kk],
                    device_id=(px, py, c), device_id_type=MESH)
                cp.start()
                cps.append(cp)
        for cp in cps:
            cp.wait()
        for lc in lcs:
            lc.wait()

    any_spec = pl.BlockSpec(memory_space=pl.ANY)
    return pl.pallas_call(
        body, name="grad_chip_exchange",
        in_specs=[any_spec] * nw, out_specs=[any_spec] * nw,
        out_shape=[jax.ShapeDtypeStruct(p.shape, p.dtype) for p in psums],
        scratch_shapes=[pltpu.SemaphoreType.DMA((nw, 3)), pltpu.SemaphoreType.DMA((nw, 3)),
                        pltpu.SemaphoreType.DMA((nw,))],
    )(*psums)


def _sum4(parts, *, rb, name):
    S, r, cdim = parts.shape

    def body(p_ref, out_ref):
        out_ref[...] = ((p_ref[0] + p_ref[1]) + p_ref[2]) + p_ref[3]

    return pl.pallas_call(
        body, name=name, grid=(r // rb,),
        in_specs=[pl.BlockSpec((S, rb, cdim), lambda i: (0, i, 0))],
        out_specs=pl.BlockSpec((rb, cdim), lambda i: (i, 0)),
        out_shape=jax.ShapeDtypeStruct((r, cdim), F32), compiler_params=_cparams(("parallel",)),
    )(parts)


def _pair_gather(halves):
    nw = len(halves)

    def body(*refs):
        ins, outs = refs[:nw], refs[nw:2 * nw]
        send_sems, recv_sems, local_sems = refs[2 * nw:]
        x, y, c = _place()
        sibling = (x, y, 1 - c)
        cps, lcs = [], []
        for w in range(nw):
            half = ins[w].shape[0]
            mine = outs[w].at[pl.ds(c * half, half), :]
            lc = pltpu.make_async_copy(ins[w], mine, local_sems.at[w])
            lc.start()
            lcs.append(lc)
            cp = pltpu.make_async_remote_copy(
                src_ref=ins[w], dst_ref=mine, send_sem=send_sems.at[w], recv_sem=recv_sems.at[w],
                device_id=sibling, device_id_type=MESH)
            cp.start()
            cps.append(cp)
        for w in range(nw):
            half = ins[w].shape[0]
            theirs = outs[w].at[pl.ds((1 - c) * half, half), :]
            pltpu.make_async_remote_copy(
                src_ref=ins[w], dst_ref=theirs, send_sem=send_sems.at[w], recv_sem=recv_sems.at[w],
                device_id=sibling, device_id_type=MESH).wait_recv()
        for cp in cps:
            cp.wait_send()
        for lc in lcs:
            lc.wait()

    any_spec = pl.BlockSpec(memory_space=pl.ANY)
    return pl.pallas_call(
        body, name="grad_pair_gather",
        in_specs=[any_spec] * nw, out_specs=[any_spec] * nw,
        out_shape=[jax.ShapeDtypeStruct((2 * h.shape[0], h.shape[1]), h.dtype) for h in halves],
        scratch_shapes=[pltpu.SemaphoreType.DMA((nw,)), pltpu.SemaphoreType.DMA((nw,)),
                        pltpu.SemaphoreType.DMA((nw,))],
    )(*halves)


def _allreduce_small(packed):
    R = packed.shape[0]
    flips = [(fx, fy, fc) for fx in (0, 1) for fy in (0, 1) for fc in (0, 1)][1:]

    def body(p_ref, out_ref, slots_ref, send_sems, recv_sems):
        x, y, c = _place()
        me = 4 * x + 2 * y + c
        slots_ref[me] = p_ref[...]
        cps = []
        for kk, (fx, fy, fc) in enumerate(flips):
            cp = pltpu.make_async_remote_copy(
                src_ref=p_ref, dst_ref=slots_ref.at[me], send_sem=send_sems.at[kk], recv_sem=recv_sems.at[kk],
                device_id=(_flip(x, fx), _flip(y, fy), _flip(c, fc)), device_id_type=MESH)
            cp.start()
            cps.append(cp)
        for cp in cps:
            cp.wait()
        tot = slots_ref[0]
        for d in range(1, 8):
            tot = tot + slots_ref[d]
        out_ref[...] = tot

    vm = pl.BlockSpec(memory_space=pltpu.VMEM)
    return pl.pallas_call(
        body, name="allreduce_small", in_specs=[vm], out_specs=vm,
        out_shape=jax.ShapeDtypeStruct((R, 128), F32),
        scratch_shapes=[pltpu.VMEM((8, R, 128), F32), pltpu.SemaphoreType.DMA((7,)), pltpu.SemaphoreType.DMA((7,))],
    )(packed)


def _adamw(w, g, m, v, *, rb, name):
    r, cdim = w.shape

    def body(w_ref, g_ref, m_ref, v_ref, d_ref, nm_ref, nv_ref):
        gv = g_ref[...]
        nm = ADAM_B1 * m_ref[...] + (1.0 - ADAM_B1) * gv
        nv = ADAM_B2 * v_ref[...] + (1.0 - ADAM_B2) * (gv * gv)
        m_hat = nm / (1.0 - ADAM_B1 ** ADAM_STEP)
        v_hat = nv / (1.0 - ADAM_B2 ** ADAM_STEP)
        d_ref[...] = -ADAM_LR * (m_hat / (jnp.sqrt(v_hat) + ADAM_EPS) + ADAM_WD * w_ref[...])
        nm_ref[...] = nm
        nv_ref[...] = nv

    spec = pl.BlockSpec((rb, cdim), lambda i: (i, 0))
    shp = jax.ShapeDtypeStruct((r, cdim), F32)
    return pl.pallas_call(
        body, name=name, grid=(r // rb,), in_specs=[spec] * 4, out_specs=[spec] * 3, out_shape=[shp] * 3,
        compiler_params=_cparams(("parallel",)),
    )(w, g, m, v)


def _pack_rows(vecs):
    rows, offs, at = [], [], 0
    for vct in vecs:
        flat = vct.reshape(-1)
        n = -(-flat.shape[0] // 128) * 128
        rows.append(jnp.pad(flat, (0, n - flat.shape[0])).reshape(-1, 128))
        offs.append(at)
        at += n // 128
    padr = -(-at // 8) * 8 - at
    if padr:
        rows.append(jnp.zeros((padr, 128), F32))
    return jnp.concatenate(rows, axis=0), offs


def _unpack_rows(packed, offs, shapes):
    out = []
    for off, shp in zip(offs, shapes):
        n = 1
        for s_ in shp:
            n *= s_
        nr = -(-n // 128)
        out.append(packed[off:off + nr].reshape(-1)[:n].reshape(shp))
    return out


def _local_step(x, tgt, meta_full, lb_param, attn_norm_w, w_in_g, hgrn_norm_w, conv_w_full, w_out_full,
                ffn_norm_w, w_up_g, fcw_full, ffn_conv_b, w_down_full, final_norm_w):
    seq = x.shape[0]
    T = TR + seq
    bm = 768 if T % 768 == 0 else TR
    h0 = jnp.concatenate([jnp.zeros((PAD, D), F32), meta_full, x], axis=0)
    whn_t = jnp.tile(hgrn_norm_w, (1, NH))

    u = _norm_fwd(h0, attn_norm_w, bm=bm, name="norm1_fwd")
    proj = _matmul(u, w_in_g, mode="nn", bm=bm, bn=1152, bk=D, out_dtype=F32, name="proj_mm")
    m, o, sst = _mix_fwd(proj, lb_param, whn_t, conv_w_full)
    h1 = _matmul(m, w_out_full, mode="nn", bm=bm, bn=D, bk=D, out_dtype=F32, name="out_mm", residual=h0)
    u2 = _norm_fwd(h1, ffn_norm_w, bm=bm, name="norm2_fwd")
    up = _matmul(u2, w_up_g, mode="nn", bm=bm, bn=1408, bk=D, out_dtype=F32, name="up_mm")
    gg = _ffn_fwd(up, fcw_full, ffn_conv_b)
    h2 = _matmul(gg, w_down_full, mode="nn", bm=bm, bn=D, bk=DFF, out_dtype=F32, name="down_mm", residual=h1)
    dh2, loss_vec, dwfin = _loss_head(h2, tgt, final_norm_w.reshape(1, D))

    dgg = _matmul(dh2, w_down_full, mode="nt", bm=bm, bn=1408, bk=D, out_dtype=F32, name="dgg_mm")
    dup, dfw, dfb = _ffn_bwd(up, dgg, fcw_full, ffn_conv_b)
    du2 = _matmul(dup, w_up_g, mode="nt", bm=bm, bn=D, bk=1408, out_dtype=F32, name="du2_mm")
    dh1, dwffn = _norm_bwd(h1, du2, dh2, ffn_norm_w, bm=bm, name="norm2_bwd")
    dmm = _matmul(dh1, w_out_full, mode="nt", bm=bm, bn=D, bk=D, out_dtype=F32, name="dm_mm")
    dproj, dlb, dwhn, dcw = _mix_bwd(proj, o, dmm, sst, lb_param, whn_t, conv_w_full)
    du = _matmul(dproj, w_in_g, mode="nt", bm=bm, bn=D, bk=1152, out_dtype=F32, name="du_mm")
    dh0, dwattn = _norm_bwd(h0, du, dh1, attn_norm_w, bm=bm, name="norm1_bwd")

    kb = bm
    g_in = _matmul(u, dproj, mode="tn", bm=D, bn=1152, bk=kb, out_dtype=F32, name="dw_in_mm",
                   out_shards=(NSHARD, 9 * D // NSHARD))
    g_out = _matmul(m, dh1, mode="tn", bm=D, bn=D, bk=kb, out_dtype=F32, name="dw_out_mm")
    g_up = _matmul(u2, dup, mode="tn", bm=D, bn=1408, bk=kb, out_dtype=F32, name="dw_up_mm",
                   out_shards=(NSHARD, 2 * DFF // NSHARD))
    g_down = _matmul(gg, dh2, mode="tn", bm=1408, bn=D, bk=kb, out_dtype=F32, name="dw_down_mm")

    grad_x = dh0[TR:]
    dmeta = dh0[PAD:TR]
    small = dict(dlb=dlb, dwattn=dwattn, dwhn=dwhn, dwffn=dwffn, dfb=dfb, dwfin=dwfin,
                 dcw=dcw[0:3], dfw=dfw[0:3], dmeta=dmeta, loss=loss_vec)
    big = (g_in, g_out.reshape(NSHARD, D // NSHARD, D), g_up, g_down.reshape(NSHARD, DFF // NSHARD, D))
    return grad_x, small, big


_SMALL_ORDER = ("dlb", "dwattn", "dwhn", "dwffn", "dfb", "dwfin", "dcw", "dfw", "dmeta", "loss")


def kernel(x, meta_tokens, lb_param, attn_norm_w, w_in, hgrn_norm_w, conv_w, w_out, ffn_norm_w, w_up, ffn_conv_w, ffn_conv_b, w_down, final_norm_w, loss_target, m_meta_tokens, m_lb_param, m_attn_norm_w, m_w_in, m_hgrn_norm_w, m_conv_w, m_w_out, m_ffn_norm_w, m_w_up, m_ffn_conv_w, m_ffn_conv_b, m_w_down, m_final_norm_w, v_meta_tokens, v_lb_param, v_attn_norm_w, v_w_in, v_hgrn_norm_w, v_conv_w, v_w_out, v_ffn_norm_w, v_w_up, v_ffn_conv_w, v_ffn_conv_b, v_w_down, v_final_norm_w):
    xi, yi, ci = _place()
    j = 2 * xi + yi
    c_idx = jnp.reshape(ci, (1,)).astype(jnp.int32)

    w_shards = (w_in[0].astype(BF16), w_out[0].astype(BF16), w_up[0].astype(BF16), w_down[0].astype(BF16))
    ds_, fs_ = D // NSHARD, DFF // NSHARD
    widen = lambda a: jnp.pad(a, ((0, 0), (0, 768 - a.shape[1])))
    rows_small = jnp.concatenate([widen(meta_tokens), widen(conv_w[0]), widen(ffn_conv_w[0]),
                                  jnp.zeros((2, 768), F32)], axis=0)
    w_in_g, w_out_g, w_up_g, w_down_g, small_g = _allgather_weights(w_shards, rows_small)
    unshard = lambda a: jnp.transpose(a, (1, 0, 2)).reshape(a.shape[1], -1)
    meta_full = unshard(small_g[:, 0:NMETA, 0:ds_])
    conv_w_full = unshard(small_g[:, NMETA:NMETA + 3, 0:ds_])
    fcw_full = unshard(small_g[:, NMETA + 3:NMETA + 6, 0:fs_])

    grad_x, small, big = _local_step(
        x[0], loss_target[0], meta_full, lb_param, attn_norm_w, w_in_g, hgrn_norm_w, conv_w_full,
        w_out_g.reshape(D, D), ffn_norm_w, w_up_g, fcw_full, ffn_conv_b, w_down_g.reshape(DFF, D), final_norm_w)

    names = _SMALL_ORDER
    packed, offs = _pack_rows([small[n] for n in names])
    red = _allreduce_small(packed)
    vals = dict(zip(names, _unpack_rows(red, offs, [small[n].shape for n in names])))
    loss = jnp.sum(vals["loss"])
    g_small = {
        "meta_tokens": lax.dynamic_slice_in_dim(vals["dmeta"], j * (D // NSHARD), D // NSHARD, axis=1),
        "lb_param": jnp.concatenate([vals["dlb"], -vals["dlb"]], axis=0),
        "attn_norm_w": vals["dwattn"],
        "hgrn_norm_w": vals["dwhn"],
        "conv_w": lax.dynamic_slice_in_dim(vals["dcw"], j * (D // NSHARD), D // NSHARD, axis=1)[None],
        "ffn_norm_w": vals["dwffn"],
        "ffn_conv_w": lax.dynamic_slice_in_dim(vals["dfw"], j * (DFF // NSHARD), DFF // NSHARD, axis=1)[None],
        "ffn_conv_b": vals["dfb"],
        "final_norm_w": vals["dwfin"].reshape(D),
    }

    others = _pair_exchange(big)
    rbs = (256, 128, 256, 176)
    psums = [_pair_add(g, o_, c_idx, rb=rb, name=f"pair_add_{n}") for g, o_, rb, n in zip(big, others, rbs, "0123")]
    parts = _chip_exchange(psums)
    halves = [_sum4(p, rb=rb, name=f"chip_sum_{n}") for p, rb, n in zip(parts, rbs, "0123")]
    g_big = _pair_gather(halves)

    weights = {"meta_tokens": meta_tokens, "lb_param": lb_param, "attn_norm_w": attn_norm_w, "w_in": w_in,
               "hgrn_norm_w": hgrn_norm_w, "conv_w": conv_w, "w_out": w_out, "ffn_norm_w": ffn_norm_w,
               "w_up": w_up, "ffn_conv_w": ffn_conv_w, "ffn_conv_b": ffn_conv_b, "w_down": w_down,
               "final_norm_w": final_norm_w}
    ms = {"meta_tokens": m_meta_tokens, "lb_param": m_lb_param, "attn_norm_w": m_attn_norm_w, "w_in": m_w_in,
          "hgrn_norm_w": m_hgrn_norm_w, "conv_w": m_conv_w, "w_out": m_w_out, "ffn_norm_w": m_ffn_norm_w,
          "w_up": m_w_up, "ffn_conv_w": m_ffn_conv_w, "ffn_conv_b": m_ffn_conv_b, "w_down": m_w_down,
          "final_norm_w": m_final_norm_w}
    vs = {"meta_tokens": v_meta_tokens, "lb_param": v_lb_param, "attn_norm_w": v_attn_norm_w, "w_in": v_w_in,
          "hgrn_norm_w": v_hgrn_norm_w, "conv_w": v_conv_w, "w_out": v_w_out, "ffn_norm_w": v_ffn_norm_w,
          "w_up": v_w_up, "ffn_conv_w": v_ffn_conv_w, "ffn_conv_b": v_ffn_conv_b, "w_down": v_w_down,
          "final_norm_w": v_final_norm_w}
    order = list(weights)
    grads, deltas, new_m, new_v = {}, {}, {}, {}

    for name, g, rb in zip(("w_in", "w_out", "w_up", "w_down"), g_big, (256, 128, 256, 176)):
        shp = weights[name].shape
        w2, m2, v2 = (a.reshape(shp[1], shp[2]) for a in (weights[name], ms[name], vs[name]))
        d_, nm_, nv_ = _adamw(w2, g, m2, v2, rb=rb, name=f"adamw_{name}")
        grads[name], deltas[name], new_m[name], new_v[name] = (a.reshape(shp) for a in (g, d_, nm_, nv_))

    small_names = [n for n in order if n not in grads]
    pw, poffs = _pack_rows([weights[n] for n in small_names])
    pg, _ = _pack_rows([g_small[n] for n in small_names])
    pm, _ = _pack_rows([ms[n] for n in small_names])
    pv, _ = _pack_rows([vs[n] for n in small_names])
    pd, pnm, pnv = _adamw(pw, pg, pm, pv, rb=pw.shape[0], name="adamw_small")
    shapes = [weights[n].shape for n in small_names]
    for n, d_, nm_, nv_ in zip(small_names, _unpack_rows(pd, poffs, shapes), _unpack_rows(pnm, poffs, shapes),
                               _unpack_rows(pnv, poffs, shapes)):
        grads[n], deltas[n], new_m[n], new_v[n] = g_small[n].reshape(weights[n].shape), d_, nm_, nv_

    return (loss, grad_x[None], *[grads[n] for n in order], *[deltas[n] for n in order],
            *[new_m[n] for n in order], *[new_v[n] for n in order])
```

```python
import functools

import jax
import jax.numpy as jnp
from jax import lax
from jax.experimental import pallas as pl
from jax.experimental.pallas import tpu as pltpu

F32 = jnp.float32
BF16 = jnp.bfloat16
MESH = pl.DeviceIdType.MESH

D = 1024
NH = 8
HD = 128
DFF = 2816
NMETA = 16
EPS = 1e-6
TR = 256
PAD = TR - NMETA
CH = 64
NSHARD = 4
VMEM_LIMIT = 56 * 1024 * 1024

ADAM_LR = 0.001
ADAM_B1 = 0.9
ADAM_B2 = 0.999
ADAM_EPS = 1e-08
ADAM_WD = 0.01
ADAM_STEP = 10


def _cparams(semantics=None, **kw):
    return pltpu.CompilerParams(dimension_semantics=semantics, vmem_limit_bytes=VMEM_LIMIT, **kw)


def _sigmoid(x):
    return 1.0 / (1.0 + jnp.exp(-x))


def _matmul(a, b, *, mode, bm, bn, bk, out_dtype, name, residual=None, out_shards=None):
    if mode == "tn":
        K, M = a.shape
    else:
        M, K = a.shape
    b3 = b.ndim == 3
    if b3:
        S, R, Cs = b.shape
        bcols = S * Cs
        brows = R
    else:
        brows, bcols = b.shape
    if mode == "nt":
        N = brows
        assert bcols == K
    else:
        N = bcols
        assert brows == K
    assert M % bm == 0 and N % bn == 0 and K % bk == 0, (name, M, N, K, bm, bn, bk)
    nm, nn, nk = M // bm, N // bn, K // bk

    if mode == "tn":
        a_spec = pl.BlockSpec((bk, bm), lambda i, j, k: (k, i))
    else:
        a_spec = pl.BlockSpec((bm, bk), lambda i, j, k: (i, k))
    if mode == "nt":
        if b3:
            assert bn == R and Cs % bk == 0
            per = Cs // bk
            b_spec = pl.BlockSpec((None, bn, bk), lambda i, j, k: (k // per, 0, k % per))
        else:
            b_spec = pl.BlockSpec((bn, bk), lambda i, j, k: (j, k))
    else:
        if b3:
            assert bk == R and Cs % bn == 0
            per = Cs // bn
            b_spec = pl.BlockSpec((None, bk, bn), lambda i, j, k: (j // per, 0, j % per))
        else:
            b_spec = pl.BlockSpec((bk, bn), lambda i, j, k: (k, j))
    in_specs = [a_spec, b_spec]
    args = [a, b]
    if residual is not None:
        in_specs.append(pl.BlockSpec((bm, bn), lambda i, j, k: (i, j)))
        args.append(residual)
    if out_shards is not None:
        So, Co = out_shards
        assert So * Co == N and Co % bn == 0 and bm == M
        pero = Co // bn
        out_shape = jax.ShapeDtypeStruct((So, M, Co), out_dtype)
        out_spec = pl.BlockSpec((None, bm, bn), lambda i, j, k: (j // pero, 0, j % pero))
    else:
        out_shape = jax.ShapeDtypeStruct((M, N), out_dtype)
        out_spec = pl.BlockSpec((bm, bn), lambda i, j, k: (i, j))

    def body(*refs):
        if residual is not None:
            a_ref, b_ref, r_ref, o_ref = refs[:4]
            scr = refs[4:]
        else:
            a_ref, b_ref, o_ref = refs[:3]
            r_ref = None
            scr = refs[3:]
        av = a_ref[...].astype(BF16)
        bv = b_ref[...].astype(BF16)
        if mode == "nn":
            p = jnp.dot(av, bv, preferred_element_type=F32)
        elif mode == "nt":
            p = lax.dot_general(av, bv, (((1,), (1,)), ((), ())), preferred_element_type=F32)
        else:
            p = lax.dot_general(av, bv, (((0,), (0,)), ((), ())), preferred_element_type=F32)

        def finish(acc):
            if r_ref is not None:
                acc = acc + r_ref[...]
            o_ref[...] = acc.astype(out_dtype)

        if nk == 1:
            finish(p)
        else:
            acc_ref = scr[0]
            k = pl.program_id(2)

            @pl.when(k == 0)
            def _():
                acc_ref[...] = p

            @pl.when(k > 0)
            def _():
                acc_ref[...] += p

            @pl.when(k == nk - 1)
            def _():
                finish(acc_ref[...])

    scratch = [] if nk == 1 else [pltpu.VMEM((bm, bn), F32)]
    return pl.pallas_call(
        body, name=name, grid=(nm, nn, nk), in_specs=in_specs, out_specs=out_spec, out_shape=out_shape,
        scratch_shapes=scratch, compiler_params=_cparams(("parallel", "parallel", "arbitrary")),
    )(*args)


def _norm_fwd(h, w, *, bm, name):
    T = h.shape[0]

    def body(h_ref, w_ref, o_ref):
        x = h_ref[...]
        r = lax.rsqrt(jnp.mean(x * x, axis=-1, keepdims=True) + EPS)
        o_ref[...] = (x * r * w_ref[...]).astype(BF16)

    return pl.pallas_call(
        body, name=name, grid=(T // bm,),
        in_specs=[pl.BlockSpec((bm, D), lambda i: (i, 0)), pl.BlockSpec((1, D), lambda i: (0, 0))],
        out_specs=pl.BlockSpec((bm, D), lambda i: (i, 0)),
        out_shape=jax.ShapeDtypeStruct((T, D), BF16), compiler_params=_cparams(("parallel",)),
    )(h, w)


def _norm_bwd(h, du, dres, w, *, bm, name):
    T = h.shape[0]

    def body(h_ref, du_ref, dres_ref, w_ref, dh_ref, dw_ref):
        i = pl.program_id(0)
        x = h_ref[...]
        r = lax.rsqrt(jnp.mean(x * x, axis=-1, keepdims=True) + EPS)
        n = x * r
        du_v = du_ref[...]
        dn = du_v * w_ref[...]
        dh_ref[...] = dres_ref[...] + r * (dn - n * jnp.mean(dn * n, axis=-1, keepdims=True))
        part = jnp.sum(du_v * n, axis=0, keepdims=True)

        @pl.when(i == 0)
        def _():
            dw_ref[...] = part

        @pl.when(i > 0)
        def _():
            dw_ref[...] += part

    row = pl.BlockSpec((bm, D), lambda i: (i, 0))
    vec = pl.BlockSpec((1, D), lambda i: (0, 0))
    return pl.pallas_call(
        body, name=name, grid=(T // bm,), in_specs=[row, row, row, vec], out_specs=[row, vec],
        out_shape=[jax.ShapeDtypeStruct((T, D), F32), jax.ShapeDtypeStruct((1, D), F32)],
        compiler_params=_cparams(("arbitrary",)),
    )(h, du, dres, w)


def _norm_bwd_input(h, du, dres, w):
    T = h.shape[0]
    nt = T // TR

    def body(h_ref, du_ref, dres_ref, w_ref, gx_ref, dmeta_ref, dw_ref):
        i = pl.program_id(0)
        x = h_ref[...]
        r = lax.rsqrt(jnp.mean(x * x, axis=-1, keepdims=True) + EPS)
        n = x * r
        du_v = du_ref[...]
        dn = du_v * w_ref[...]
        dh = dres_ref[...] + r * (dn - n * jnp.mean(dn * n, axis=-1, keepdims=True))
        gx_ref[...] = dh
        part = jnp.sum(du_v * n, axis=0, keepdims=True)

        @pl.when(i == 0)
        def _():
            dmeta_ref[...] = dh[PAD:TR, :]
            dw_ref[...] = part

        @pl.when(i > 0)
        def _():
            dw_ref[...] += part

    row = pl.BlockSpec((TR, D), lambda i: (i, 0))
    vec = pl.BlockSpec((1, D), lambda i: (0, 0))
    return pl.pallas_call(
        body, name="norm1_bwd", grid=(nt,), in_specs=[row, row, row, vec],
        out_specs=[pl.BlockSpec((TR, D), lambda i: (jnp.maximum(i - 1, 0), 0)),
                   pl.BlockSpec((NMETA, D), lambda i: (0, 0)), vec],
        out_shape=[jax.ShapeDtypeStruct((T - TR, D), F32), jax.ShapeDtypeStruct((NMETA, D), F32),
                   jax.ShapeDtypeStruct((1, D), F32)],
        compiler_params=_cparams(("arbitrary",)),
    )(h, du, dres, w)


def _split3(x):
    hi = x.astype(BF16)
    r1 = x - hi.astype(F32)
    mid = r1.astype(BF16)
    lo = (r1 - mid.astype(F32)).astype(BF16)
    return hi, mid, lo


def _tri_matmul(tri_bf16, x):
    hi, mid, lo = _split3(x)
    out = jnp.dot(tri_bf16, lo, preferred_element_type=F32)
    out = out + jnp.dot(tri_bf16, mid, preferred_element_type=F32)
    return out + jnp.dot(tri_bf16, hi, preferred_element_type=F32)


def _shift_down(x, prev8, n):
    rows = x.shape[0]
    rid = lax.broadcasted_iota(jnp.int32, x.shape, 0)
    y = pltpu.roll(x, n, 0)
    for t in range(n):
        y = jnp.where(rid == t, prev8[8 - n + t:8 - n + t + 1, :], y)
    del rows
    return y


def _shift_up(x, next8, n):
    rows = x.shape[0]
    rid = lax.broadcasted_iota(jnp.int32, x.shape, 0)
    y = pltpu.roll(x, rows - n, 0)
    for t in range(n):
        y = jnp.where(rid == rows - n + t, next8[t:t + 1, :], y)
    return y


def _gates(f_raw, lb):
    sg = _sigmoid(f_raw)
    f = lb + (1.0 - lb) * sg
    return sg, f, jnp.log(f), 1.0 - f


def _lower_bound(lbp_ref):
    return _sigmoid(lbp_ref[0:1, :] - lbp_ref[1:2, :])


def _tri_masks():
    r = lax.broadcasted_iota(jnp.int32, (CH, CH), 0)
    c = lax.broadcasted_iota(jnp.int32, (CH, CH), 1)
    return r >= c, r <= c


def _ones_where(mask):
    return jnp.where(mask, 1.0, 0.0).astype(BF16)


def _dot(a, b):
    return jnp.dot(a.astype(BF16), b.astype(BF16), preferred_element_type=F32)


def _dot_nt(a, b):
    return lax.dot_general(a.astype(BF16), b.astype(BF16), (((1,), (1,)), ((), ())), preferred_element_type=F32)


def _dot_tn(a, b):
    return lax.dot_general(a.astype(BF16), b.astype(BF16), (((0,), (0,)), ((), ())), preferred_element_type=F32)


def _mix_fwd(proj, lb_param, whn, conv_w):
    T = proj.shape[0]
    nt = T // TR
    ncht = TR // CH

    def body(proj_ref, lbp_ref, whn_ref, cw_ref, m_ref, o_ref, sst_ref, st_ref, cxc_ref):
        i = pl.program_id(0)

        @pl.when(i == 0)
        def _():
            st_ref[...] = jnp.zeros_like(st_ref)
            cxc_ref[...] = jnp.zeros_like(cxc_ref)

        lb = _lower_bound(lbp_ref)
        lower, _ = _tri_masks()
        ltri = _ones_where(lower)
        whn_v = whn_ref[...]
        w0, w1, w2 = cw_ref[0:1, :], cw_ref[1:2, :], cw_ref[2:3, :]

        def chunk(c, carry):
            rows = pl.ds(pl.multiple_of(c * CH, CH), CH)
            q_raw = proj_ref[rows, 0:D]
            f_raw = proj_ref[rows, D:2 * D]
            v = proj_ref[rows, 2 * D:3 * D]
            q = q_raw * _sigmoid(q_raw)
            _, _, g, k = _gates(f_raw, lb)
            gam = _tri_matmul(ltri, g)
            gam_l = gam[CH - 1:CH, :]
            e_l = jnp.exp(gam_l)
            qt = (q * jnp.exp(gam)).astype(BF16)
            kt = (k * jnp.exp(-gam)).astype(BF16)
            khat = (k * jnp.exp(gam_l - gam)).astype(BF16)
            vb = v.astype(BF16)
            on_parts = []
            for h in range(NH):
                cs = slice(h * HD, (h + 1) * HD)
                st = st_ref[h]
                sst_ref[c, h] = st
                a = jnp.where(lower, _dot_nt(qt[:, cs], kt[:, cs]), 0.0)
                o_h = _dot_nt(qt[:, cs], st) + _dot(a, vb[:, cs])
                st_ref[h] = st * e_l[:, cs] + _dot_tn(vb[:, cs], khat[:, cs])
                o_ref[rows, cs] = o_h
                ro = lax.rsqrt(jnp.mean(o_h * o_h, axis=-1, keepdims=True) + EPS)
                on_parts.append(o_h * ro)
            on = jnp.concatenate(on_parts, axis=1)
            g_out = proj_ref[rows, 3 * D:4 * D]
            y_a = on * whn_v * (g_out * _sigmoid(g_out))
            cx = proj_ref[rows, 5 * D:6 * D] * proj_ref[rows, 6 * D:7 * D]
            prev8 = cxc_ref[...]
            cv = w0 * _shift_down(cx, prev8, 2) + w1 * _shift_down(cx, prev8, 1) + w2 * cx
            cxc_ref[...] = cx[CH - 8:CH, :]
            y_b = proj_ref[rows, 4 * D:5 * D] * cv
            m = _sigmoid(proj_ref[rows, 7 * D:8 * D]) * y_a + _sigmoid(proj_ref[rows, 8 * D:9 * D]) * y_b
            m_ref[rows, :] = m.astype(BF16)
            return carry

        lax.fori_loop(0, ncht, chunk, 0)

    return pl.pallas_call(
        body, name="mix_fwd", grid=(nt,),
        in_specs=[pl.BlockSpec((TR, 9 * D), lambda i: (i, 0)),
                  pl.BlockSpec((2, D), lambda i: (0, 0)),
                  pl.BlockSpec((1, D), lambda i: (0, 0)),
                  pl.BlockSpec((3, D), lambda i: (0, 0))],
        out_specs=[pl.BlockSpec((TR, D), lambda i: (i, 0)),
                   pl.BlockSpec((TR, D), lambda i: (i, 0)),
                   pl.BlockSpec((ncht, NH, HD, HD), lambda i: (i, 0, 0, 0))],
        out_shape=[jax.ShapeDtypeStruct((T, D), BF16), jax.ShapeDtypeStruct((T, D), F32),
                   jax.ShapeDtypeStruct((T // CH, NH, HD, HD), F32)],
        scratch_shapes=[pltpu.VMEM((NH, HD, HD), F32), pltpu.VMEM((8, D), F32)],
        compiler_params=_cparams(("arbitrary",)),
    )(proj, lb_param, whn, conv_w)


def _mix_bwd(proj, o, dm, sst, lb_param, whn, conv_w):
    T = proj.shape[0]
    nt = T // TR
    ncht = TR // CH
    tb = TR // 8

    def body(proj_ref, pc_ref, px_ref, o_ref, dm_ref, sst_ref, lbp_ref, whn_ref, cw_ref,
             dproj_ref, dlb_ref, dwhn_ref, dcw_ref, dst_ref, dcvc_ref, acc_lb, acc_hn, acc_cw):
        s = pl.program_id(0)
        tile = nt - 1 - s

        @pl.when(s == 0)
        def _():
            dst_ref[...] = jnp.zeros_like(dst_ref)
            dcvc_ref[...] = jnp.zeros_like(dcvc_ref)
            acc_lb[...] = jnp.zeros_like(acc_lb)
            acc_hn[...] = jnp.zeros_like(acc_hn)
            acc_cw[...] = jnp.zeros_like(acc_cw)

        lb = _lower_bound(lbp_ref)
        lower, upper = _tri_masks()
        ltri = _ones_where(lower)
        utri = _ones_where(upper)
        whn_v = whn_ref[...]
        w0, w1, w2 = cw_ref[0:1, :], cw_ref[1:2, :], cw_ref[2:3, :]
        cx_before_tile = jnp.where(tile > 0, pc_ref[...] * px_ref[...], 0.0)
        rid = lax.broadcasted_iota(jnp.int32, (CH, D), 0)

        def chunk(cc, carry):
            c = ncht - 1 - cc
            r0 = pl.multiple_of(c * CH, CH)
            rows = pl.ds(r0, CH)
            q_raw = proj_ref[rows, 0:D]
            f_raw = proj_ref[rows, D:2 * D]
            v = proj_ref[rows, 2 * D:3 * D]
            g_out = proj_ref[rows, 3 * D:4 * D]
            b_gate = proj_ref[rows, 4 * D:5 * D]
            c_gate = proj_ref[rows, 5 * D:6 * D]
            x_conv = proj_ref[rows, 6 * D:7 * D]
            sa = _sigmoid(proj_ref[rows, 7 * D:8 * D])
            sb = _sigmoid(proj_ref[rows, 8 * D:9 * D])
            dm_v = dm_ref[rows, :]

            sq = _sigmoid(q_raw)
            q = q_raw * sq
            sg, f, g, k = _gates(f_raw, lb)
            gam = _tri_matmul(ltri, g)
            gam_l = gam[CH - 1:CH, :]
            e_l = jnp.exp(gam_l)
            e_g = jnp.exp(gam)
            e_ng = jnp.exp(-gam)
            e_kl = jnp.exp(gam_l - gam)
            qt = q * e_g
            kt = k * e_ng
            khat = k * e_kl
            s_go = _sigmoid(g_out)
            silu_go = g_out * s_go
            cx = c_gate * x_conv
            rprev = pl.multiple_of(jnp.maximum(r0 - 8, 0), 8)
            cx_prev_in = proj_ref[pl.ds(rprev, 8), 5 * D:6 * D] * proj_ref[pl.ds(rprev, 8), 6 * D:7 * D]
            prev8 = jnp.where(c > 0, cx_prev_in, cx_before_tile)
            cx_m1 = _shift_down(cx, prev8, 1)
            cx_m2 = _shift_down(cx, prev8, 2)
            cv = w0 * cx_m2 + w1 * cx_m1 + w2 * cx
            y_b = b_gate * cv

            o_v = o_ref[rows, :]
            ro_parts, on_parts = [], []
            for h in range(NH):
                cs = slice(h * HD, (h + 1) * HD)
                o_h = o_v[:, cs]
                ro = lax.rsqrt(jnp.mean(o_h * o_h, axis=-1, keepdims=True) + EPS)
                ro_parts.append(ro)
                on_parts.append(o_h * ro)
            on = jnp.concatenate(on_parts, axis=1)
            y_a = on * whn_v * silu_go

            dy_a = dm_v * sa
            dy_b = dm_v * sb
            dproj_ref[rows, 7 * D:8 * D] = (dm_v * y_a * sa * (1.0 - sa)).astype(BF16)
            dproj_ref[rows, 8 * D:9 * D] = (dm_v * y_b * sb * (1.0 - sb)).astype(BF16)
            dproj_ref[rows, 4 * D:5 * D] = (dy_b * cv).astype(BF16)
            dcv = dy_b * b_gate
            acc_cw[0:1, :] += jnp.sum(dcv * cx_m2, axis=0, keepdims=True)
            acc_cw[1:2, :] += jnp.sum(dcv * cx_m1, axis=0, keepdims=True)
            acc_cw[2:3, :] += jnp.sum(dcv * cx, axis=0, keepdims=True)
            next8 = dcvc_ref[...]
            dcx = w2 * dcv + w1 * _shift_up(dcv, next8, 1) + w0 * _shift_up(dcv, next8, 2)
            dcvc_ref[...] = dcv[0:8, :]
            dproj_ref[rows, 5 * D:6 * D] = (dcx * x_conv).astype(BF16)
            dproj_ref[rows, 6 * D:7 * D] = (dcx * c_gate).astype(BF16)
            don = dy_a * whn_v * silu_go
            dproj_ref[rows, 3 * D:4 * D] = (dy_a * on * whn_v * (s_go * (1.0 + g_out * (1.0 - s_go)))).astype(BF16)
            acc_hn[...] += jnp.sum(dy_a * silu_go * on, axis=0, keepdims=True)

            dq_parts, dk_parts, dv_parts, dgam_parts, ext_parts = [], [], [], [], []
            for h in range(NH):
                cs = slice(h * HD, (h + 1) * HD)
                on_h = on_parts[h]
                don_h = don[:, cs]
                do_h = ro_parts[h] * (don_h - on_h * jnp.mean(don_h * on_h, axis=-1, keepdims=True))
                qt_h, kt_h, khat_h, v_h = qt[:, cs], kt[:, cs], khat[:, cs], v[:, cs]
                st = sst_ref[c, h]
                dstn = dst_ref[h]
                a_t = jnp.where(upper, _dot_nt(kt_h, qt_h), 0.0)
                da = jnp.where(lower, _dot_nt(do_h, v_h), 0.0)
                da_t = jnp.where(upper, _dot_nt(v_h, do_h), 0.0)
                dv_h = _dot(a_t, do_h) + _dot_nt(khat_h, dstn)
                dqt_state = _dot(do_h, st)
                dqt_chunk = _dot(da, kt_h)
                dkt = _dot(da_t, qt_h)
                dkhat = _dot(v_h, dstn)
                dq_h = (dqt_state + dqt_chunk) * e_g[:, cs]
                dk_h = dkt * e_ng[:, cs] + dkhat * e_kl[:, cs]
                khat_dkhat = dkhat * khat_h
                ext = (jnp.sum(khat_dkhat, axis=0, keepdims=True)
                       + e_l[:, cs] * jnp.sum(st * dstn, axis=0, keepdims=True))
                dst_ref[h] = _dot_tn(do_h, qt_h) + dstn * e_l[:, cs]
                dq_parts.append(dq_h)
                dk_parts.append(dk_h)
                dv_parts.append(dv_h)
                qt_seen = qt_h.astype(BF16).astype(F32)
                kt_seen = kt_h.astype(BF16).astype(F32)
                dgam_parts.append(qt_h * dqt_state + qt_seen * dqt_chunk - kt_seen * dkt - khat_dkhat)
                ext_parts.append(ext)
            dq = jnp.concatenate(dq_parts, axis=1)
            dk = jnp.concatenate(dk_parts, axis=1)
            dgam = jnp.concatenate(dgam_parts, axis=1)
            ext = jnp.concatenate(ext_parts, axis=1)
            dgam = dgam + jnp.where(rid == CH - 1, ext, 0.0)
            dg = _tri_matmul(utri, dgam)
            dproj_ref[rows, 0:D] = (dq * (sq * (1.0 + q_raw * (1.0 - sq)))).astype(BF16)
            df = dg / f - dk
            dproj_ref[rows, D:2 * D] = (df * (1.0 - lb) * sg * (1.0 - sg)).astype(BF16)
            dproj_ref[rows, 2 * D:3 * D] = jnp.concatenate(dv_parts, axis=1).astype(BF16)
            real = (tile * TR + r0 + rid) >= PAD
            acc_lb[...] += jnp.sum(jnp.where(real, df * (1.0 - sg), 0.0), axis=0, keepdims=True)
            return carry

        lax.fori_loop(0, ncht, chunk, 0)

        @pl.when(s == nt - 1)
        def _():
            dlb_ref[...] = acc_lb[...] * lb * (1.0 - lb)
            hn = acc_hn[...]
            tot = hn[:, 0:HD]
            for h in range(1, NH):
                tot = tot + hn[:, h * HD:(h + 1) * HD]
            dwhn_ref[...] = tot
            dcw_ref[...] = acc_cw[0:3, :]

    rev = lambda s: (nt - 1 - s, 0)
    prevc = lambda s: (jnp.maximum((nt - 1 - s) * tb - 1, 0), 5)
    prevx = lambda s: (jnp.maximum((nt - 1 - s) * tb - 1, 0), 6)
    const = lambda s: (0, 0)
    return pl.pallas_call(
        body, name="mix_bwd", grid=(nt,),
        in_specs=[pl.BlockSpec((TR, 9 * D), rev),
                  pl.BlockSpec((8, D), prevc),
                  pl.BlockSpec((8, D), prevx),
                  pl.BlockSpec((TR, D), rev),
                  pl.BlockSpec((TR, D), rev),
                  pl.BlockSpec((ncht, NH, HD, HD), lambda s: (nt - 1 - s, 0, 0, 0)),
                  pl.BlockSpec((2, D), const),
                  pl.BlockSpec((1, D), const),
                  pl.BlockSpec((3, D), const)],
        out_specs=[pl.BlockSpec((TR, 9 * D), rev),
                   pl.BlockSpec((1, D), const),
                   pl.BlockSpec((1, HD), const),
                   pl.BlockSpec((3, D), const)],
        out_shape=[jax.ShapeDtypeStruct((T, 9 * D), BF16), jax.ShapeDtypeStruct((1, D), F32),
                   jax.ShapeDtypeStruct((1, HD), F32), jax.ShapeDtypeStruct((3, D), F32)],
        scratch_shapes=[pltpu.VMEM((NH, HD, HD), F32), pltpu.VMEM((8, D), F32),
                        pltpu.VMEM((1, D), F32), pltpu.VMEM((1, D), F32), pltpu.VMEM((8, D), F32)],
        compiler_params=_cparams(("arbitrary",)),
    )(proj, proj, proj, o, dm, sst, lb_param, whn, conv_w)


def _ffn_fwd(up, fcw, fcb):
    T = up.shape[0]
    nt = T // TR

    def body(up_ref, w_ref, b_ref, gg_ref, carry_ref):
        i = pl.program_id(0)

        @pl.when(i == 0)
        def _():
            carry_ref[...] = jnp.zeros_like(carry_ref)

        w0, w1, w2 = w_ref[0:1, :], w_ref[1:2, :], w_ref[2:3, :]

        def chunk(c, carry):
            rows = pl.ds(pl.multiple_of(c * CH, CH), CH)
            a_pre = up_ref[rows, 0:DFF]
            val = up_ref[rows, DFF:2 * DFF]
            prev8 = carry_ref[...]
            a = w0 * _shift_down(a_pre, prev8, 2) + w1 * _shift_down(a_pre, prev8, 1) + w2 * a_pre + b_ref[...]
            carry_ref[...] = a_pre[CH - 8:CH, :]
            gg_ref[rows, :] = (a * _sigmoid(a) * val).astype(BF16)
            return carry

        lax.fori_loop(0, TR // CH, chunk, 0)

    return pl.pallas_call(
        body, name="ffn_fwd", grid=(nt,),
        in_specs=[pl.BlockSpec((TR, 2 * DFF), lambda i: (i, 0)),
                  pl.BlockSpec((3, DFF), lambda i: (0, 0)),
                  pl.BlockSpec((1, DFF), lambda i: (0, 0))],
        out_specs=pl.BlockSpec((TR, DFF), lambda i: (i, 0)),
        out_shape=jax.ShapeDtypeStruct((T, DFF), BF16),
        scratch_shapes=[pltpu.VMEM((8, DFF), F32)],
        compiler_params=_cparams(("arbitrary",)),
    )(up, fcw, fcb)


def _ffn_bwd(up, dgg, fcw, fcb):
    T = up.shape[0]
    nt = T // TR
    tb = TR // 8
    ncht = TR // CH

    def body(up_ref, pa_ref, dgg_ref, w_ref, b_ref, dup_ref, dfw_ref, dfb_ref, carry_ref, acc_w, acc_b):
        s = pl.program_id(0)
        tile = nt - 1 - s

        @pl.when(s == 0)
        def _():
            carry_ref[...] = jnp.zeros_like(carry_ref)
            acc_w[...] = jnp.zeros_like(acc_w)
            acc_b[...] = jnp.zeros_like(acc_b)

        w0, w1, w2 = w_ref[0:1, :], w_ref[1:2, :], w_ref[2:3, :]
        a_before_tile = jnp.where(tile > 0, pa_ref[...], 0.0)

        def chunk(cc, carry):
            c = ncht - 1 - cc
            r0 = pl.multiple_of(c * CH, CH)
            rows = pl.ds(r0, CH)
            a_pre = up_ref[rows, 0:DFF]
            val = up_ref[rows, DFF:2 * DFF]
            rprev = pl.multiple_of(jnp.maximum(r0 - 8, 0), 8)
            prev8 = jnp.where(c > 0, up_ref[pl.ds(rprev, 8), 0:DFF], a_before_tile)
            a_m1 = _shift_down(a_pre, prev8, 1)
            a_m2 = _shift_down(a_pre, prev8, 2)
            a = w0 * a_m2 + w1 * a_m1 + w2 * a_pre + b_ref[...]
            sig = _sigmoid(a)
            dgg_v = dgg_ref[rows, :]
            da = dgg_v * val * (sig * (1.0 + a * (1.0 - sig)))
            dup_ref[rows, DFF:2 * DFF] = (dgg_v * (a * sig)).astype(BF16)
            next8 = carry_ref[...]
            dup_ref[rows, 0:DFF] = (w2 * da + w1 * _shift_up(da, next8, 1) + w0 * _shift_up(da, next8, 2)).astype(BF16)
            carry_ref[...] = da[0:8, :]
            acc_w[0:1, :] += jnp.sum(da * a_m2, axis=0, keepdims=True)
            acc_w[1:2, :] += jnp.sum(da * a_m1, axis=0, keepdims=True)
            acc_w[2:3, :] += jnp.sum(da * a_pre, axis=0, keepdims=True)
            acc_b[...] += jnp.sum(da, axis=0, keepdims=True)
            return carry

        lax.fori_loop(0, ncht, chunk, 0)

        @pl.when(s == nt - 1)
        def _():
            dfw_ref[...] = acc_w[0:3, :]
            dfb_ref[...] = acc_b[...]

    rev = lambda s: (nt - 1 - s, 0)
    const = lambda s: (0, 0)
    return pl.pallas_call(
        body, name="ffn_bwd", grid=(nt,),
        in_specs=[pl.BlockSpec((TR, 2 * DFF), rev),
                  pl.BlockSpec((8, DFF), lambda s: (jnp.maximum((nt - 1 - s) * tb - 1, 0), 0)),
                  pl.BlockSpec((TR, DFF), rev),
                  pl.BlockSpec((3, DFF), const),
                  pl.BlockSpec((1, DFF), const)],
        out_specs=[pl.BlockSpec((TR, 2 * DFF), rev),
                   pl.BlockSpec((3, DFF), const),
                   pl.BlockSpec((1, DFF), const)],
        out_shape=[jax.ShapeDtypeStruct((T, 2 * DFF), BF16), jax.ShapeDtypeStruct((3, DFF), F32),
                   jax.ShapeDtypeStruct((1, DFF), F32)],
        scratch_shapes=[pltpu.VMEM((8, DFF), F32), pltpu.VMEM((8, DFF), F32), pltpu.VMEM((1, DFF), F32)],
        compiler_params=_cparams(("arbitrary",)),
    )(up, up, dgg, fcw, fcb)


def _loss_head(h2, tgt, wfin):
    T = h2.shape[0]
    nt = T // TR

    def body(h_ref, t_ref, w_ref, dh_ref, loss_ref, dw_ref):
        i = pl.program_id(0)

        @pl.when(i == 0)
        def _():
            loss_ref[...] = jnp.zeros_like(loss_ref)
            dw_ref[...] = jnp.zeros_like(dw_ref)

        x = h_ref[...]
        r = lax.rsqrt(jnp.mean(x * x, axis=-1, keepdims=True) + EPS)
        n = x * r
        w = w_ref[...]
        diff = jnp.where(i > 0, n * w - t_ref[...], 0.0)
        loss_ref[...] += jnp.sum(diff * diff, axis=0, keepdims=True) * (0.5 / D)
        dy = diff * (1.0 / D)
        dw_ref[...] += jnp.sum(dy * n, axis=0, keepdims=True)
        dn = dy * w
        dh_ref[...] = r * (dn - n * jnp.mean(dn * n, axis=-1, keepdims=True))

    row = pl.BlockSpec((TR, D), lambda i: (i, 0))
    vec = pl.BlockSpec((1, D), lambda i: (0, 0))
    return pl.pallas_call(
        body, name="loss_head", grid=(nt,),
        in_specs=[row, pl.BlockSpec((TR, D), lambda i: (jnp.maximum(i - 1, 0), 0)), vec],
        out_specs=[row, vec, vec],
        out_shape=[jax.ShapeDtypeStruct((T, D), F32), jax.ShapeDtypeStruct((1, D), F32),
                   jax.ShapeDtypeStruct((1, D), F32)],
        compiler_params=_cparams(("arbitrary",)),
    )(h2, tgt, wfin)


def _place():
    x, y, c = lax.axis_index("x"), lax.axis_index("y"), lax.axis_index("c")
    return x, y, c


_CHIP_FLIPS = ((1, 0), (0, 1), (1, 1))


def _flip(v, bit):
    return 1 - v if bit else v


def _into_slot(w, j_idx, *, rb, dtype, name):
    r, cdim = w.shape

    def body(j_ref, w_ref, out_ref):
        del j_ref
        out_ref[...] = w_ref[...].astype(dtype)

    grid_spec = pltpu.PrefetchScalarGridSpec(
        num_scalar_prefetch=1, grid=(r // rb,),
        in_specs=[pl.BlockSpec((rb, cdim), lambda i, j_ref: (i, 0))],
        out_specs=pl.BlockSpec((None, rb, cdim), lambda i, j_ref: (j_ref[0], i, 0)))
    return pl.pallas_call(
        body, name=name, grid_spec=grid_spec, out_shape=jax.ShapeDtypeStruct((NSHARD, r, cdim), dtype),
        compiler_params=_cparams(("parallel",)),
    )(j_idx, w)


def _allgather_weights(slotted):
    ncopy = len(slotted)
    nw = ncopy - 1

    def body(*refs):
        outs = refs[ncopy:2 * ncopy]
        send_sems, recv_sems = refs[2 * ncopy:]
        x, y, c = _place()
        j = 2 * x + y
        sibling = (x, y, 1 - c)
        sends = []
        for w in range(ncopy):
            r = outs[w].shape[1]
            half = r // 2 if w < nw else r
            off = c * half if w < nw else 0
            mine = outs[w].at[j, pl.ds(off, half), :]
            for kk, (fx, fy) in enumerate(_CHIP_FLIPS):
                cp = pltpu.make_async_remote_copy(
                    src_ref=mine, dst_ref=mine, send_sem=send_sems.at[w, kk], recv_sem=recv_sems.at[w, kk],
                    device_id=(_flip(x, fx), _flip(y, fy), c), device_id_type=MESH)
                cp.start()
                sends.append(cp)
        for w in range(nw):
            half = outs[w].shape[1] // 2
            off = c * half
            for kk, (fx, fy) in enumerate(_CHIP_FLIPS):
                jk = 2 * _flip(x, fx) + _flip(y, fy)
                piece = outs[w].at[jk, pl.ds(off, half), :]
                pltpu.make_async_remote_copy(
                    src_ref=piece, dst_ref=piece, send_sem=send_sems.at[w, kk], recv_sem=recv_sems.at[w, kk],
                    device_id=sibling, device_id_type=MESH).wait_recv()
                fw = pltpu.make_async_remote_copy(
                    src_ref=piece, dst_ref=piece, send_sem=send_sems.at[w, 3 + kk], recv_sem=recv_sems.at[w, 3 + kk],
                    device_id=sibling, device_id_type=MESH)
                fw.start()
                sends.append(fw)
        for kk, (fx, fy) in enumerate(_CHIP_FLIPS):
            jk = 2 * _flip(x, fx) + _flip(y, fy)
            piece = outs[nw].at[jk]
            pltpu.make_async_remote_copy(
                src_ref=piece, dst_ref=piece, send_sem=send_sems.at[nw, kk], recv_sem=recv_sems.at[nw, kk],
                device_id=sibling, device_id_type=MESH).wait_recv()
        for w in range(nw):
            half = outs[w].shape[1] // 2
            off = (1 - c) * half
            for kk, (fx, fy) in enumerate(_CHIP_FLIPS):
                jk = 2 * _flip(x, fx) + _flip(y, fy)
                piece = outs[w].at[jk, pl.ds(off, half), :]
                pltpu.make_async_remote_copy(
                    src_ref=piece, dst_ref=piece, send_sem=send_sems.at[w, 3 + kk], recv_sem=recv_sems.at[w, 3 + kk],
                    device_id=sibling, device_id_type=MESH).wait_recv()
        for cp in sends:
            cp.wait_send()

    any_spec = pl.BlockSpec(memory_space=pl.ANY)
    return pl.pallas_call(
        body, name="allgather_weights",
        in_specs=[any_spec] * ncopy, out_specs=[any_spec] * ncopy,
        out_shape=[jax.ShapeDtypeStruct(a.shape, a.dtype) for a in slotted],
        input_output_aliases={i: i for i in range(ncopy)},
        scratch_shapes=[pltpu.SemaphoreType.DMA((ncopy, 6)), pltpu.SemaphoreType.DMA((ncopy, 6))],
    )(*slotted)


def _pair_exchange(grads):
    nw = len(grads)

    def body(*refs):
        ins, outs = refs[:nw], refs[nw:2 * nw]
        send_sems, recv_sems = refs[2 * nw:]
        x, y, c = _place()
        sibling = (x, y, 1 - c)
        cps = []
        for w in range(nw):
            half = ins[w].shape[1] // 2
            cp = pltpu.make_async_remote_copy(
                src_ref=ins[w].at[:, pl.ds((1 - c) * half, half), :], dst_ref=outs[w],
                send_sem=send_sems.at[w], recv_sem=recv_sems.at[w], device_id=sibling, device_id_type=MESH)
            cp.start()
            cps.append(cp)
        for cp in cps:
            cp.wait()

    any_spec = pl.BlockSpec(memory_space=pl.ANY)
    return pl.pallas_call(
        body, name="grad_pair_exchange",
        in_specs=[any_spec] * nw, out_specs=[any_spec] * nw,
        out_shape=[jax.ShapeDtypeStruct((g.shape[0], g.shape[1] // 2, g.shape[2]), g.dtype) for g in grads],
        scratch_shapes=[pltpu.SemaphoreType.DMA((nw,)), pltpu.SemaphoreType.DMA((nw,))],
    )(*grads)


def _pair_add(g, other, c_idx, *, rb, name):
    S, r, cdim = g.shape
    half = r // 2
    nb = half // rb

    def body(c_ref, g_ref, o_ref, out_ref):
        del c_ref
        out_ref[...] = (g_ref[...] + o_ref[...]).astype(BF16)

    grid_spec = pltpu.PrefetchScalarGridSpec(
        num_scalar_prefetch=1, grid=(S, nb),
        in_specs=[pl.BlockSpec((None, rb, cdim), lambda s, i, c_ref: (s, c_ref[0] * nb + i, 0)),
                  pl.BlockSpec((None, rb, cdim), lambda s, i, c_ref: (s, i, 0))],
        out_specs=pl.BlockSpec((None, rb, cdim), lambda s, i, c_ref: (s, i, 0)))
    return pl.pallas_call(
        body, name=name, grid_spec=grid_spec, out_shape=jax.ShapeDtypeStruct((S, half, cdim), BF16),
        compiler_params=_cparams(("parallel", "parallel")),
    )(c_idx, g, other)


def _chip_exchange(psums):
    nw = len(psums)

    def body(*refs):
        ins, outs = refs[:nw], refs[nw:2 * nw]
        send_sems, recv_sems = refs[2 * nw:]
        x, y, c = _place()
        cps = []
        for w in range(nw):
            for kk, (fx, fy) in enumerate(_CHIP_FLIPS):
                px, py = _flip(x, fx), _flip(y, fy)
                cp = pltpu.make_async_remote_copy(
                    src_ref=ins[w].at[2 * px + py], dst_ref=outs[w].at[kk],
                    send_sem=send_sems.at[w, kk], recv_sem=recv_sems.at[w, kk],
                    device_id=(px, py, c), device_id_type=MESH)
                cp.start()
                cps.append(cp)
        for cp in cps:
            cp.wait()

    any_spec = pl.BlockSpec(memory_space=pl.ANY)
    return pl.pallas_call(
        body, name="grad_chip_exchange",
        in_specs=[any_spec] * nw, out_specs=[any_spec] * nw,
        out_shape=[jax.ShapeDtypeStruct((3,) + p.shape[1:], p.dtype) for p in psums],
        scratch_shapes=[pltpu.SemaphoreType.DMA((nw, 3)), pltpu.SemaphoreType.DMA((nw, 3))],
    )(*psums)


def _chip_sum(psum, parts, cj_idx, *, rb, name):
    S, half, cdim = psum.shape
    nb = half // rb

    def body(cj_ref, own_ref, p_ref, out_ref):
        del cj_ref
        f = lambda v: v.astype(F32)
        out_ref[...] = ((f(own_ref[...]) + f(p_ref[0])) + f(p_ref[1])) + f(p_ref[2])

    grid_spec = pltpu.PrefetchScalarGridSpec(
        num_scalar_prefetch=1, grid=(nb,),
        in_specs=[pl.BlockSpec((None, rb, cdim), lambda i, cj: (cj[1], i, 0)),
                  pl.BlockSpec((3, rb, cdim), lambda i, cj: (0, i, 0))],
        out_specs=pl.BlockSpec((rb, cdim), lambda i, cj: (cj[0] * nb + i, 0)))
    return pl.pallas_call(
        body, name=name, grid_spec=grid_spec, out_shape=jax.ShapeDtypeStruct((2 * half, cdim), F32),
        compiler_params=_cparams(("parallel",)),
    )(cj_idx, psum, parts)


def _pair_gather(grads):
    nw = len(grads)

    def body(*refs):
        outs = refs[nw:2 * nw]
        send_sems, recv_sems = refs[2 * nw:]
        x, y, c = _place()
        sibling = (x, y, 1 - c)
        cps = []
        for w in range(nw):
            half = outs[w].shape[0] // 2
            mine = outs[w].at[pl.ds(c * half, half), :]
            cp = pltpu.make_async_remote_copy(
                src_ref=mine, dst_ref=mine, send_sem=send_sems.at[w], recv_sem=recv_sems.at[w],
                device_id=sibling, device_id_type=MESH)
            cp.start()
            cps.append(cp)
        for w in range(nw):
            half = outs[w].shape[0] // 2
            theirs = outs[w].at[pl.ds((1 - c) * half, half), :]
            pltpu.make_async_remote_copy(
                src_ref=theirs, dst_ref=theirs, send_sem=send_sems.at[w], recv_sem=recv_sems.at[w],
                device_id=sibling, device_id_type=MESH).wait_recv()
        for cp in cps:
            cp.wait_send()

    any_spec = pl.BlockSpec(memory_space=pl.ANY)
    return pl.pallas_call(
        body, name="grad_pair_gather",
        in_specs=[any_spec] * nw, out_specs=[any_spec] * nw,
        out_shape=[jax.ShapeDtypeStruct(g.shape, g.dtype) for g in grads],
        input_output_aliases={i: i for i in range(nw)},
        scratch_shapes=[pltpu.SemaphoreType.DMA((nw,)), pltpu.SemaphoreType.DMA((nw,))],
    )(*grads)


def _allreduce_small(pieces):
    n = len(pieces)
    width = max(p.shape[1] for p in pieces)
    starts, at = [], 0
    for p in pieces:
        if p.shape[0] > 1:
            at = -(-at // 8) * 8
        starts.append(at)
        at += p.shape[0]
    rows = -(-at // 8) * 8
    flips = [(fx, fy, fc) for fx in (0, 1) for fy in (0, 1) for fc in (0, 1)][1:]

    def body(*refs):
        ins, outs = refs[:n], refs[n:2 * n]
        mine_ref, slots_ref, send_sems, recv_sems = refs[2 * n:]
        x, y, c = _place()
        me = 4 * x + 2 * y + c
        mine_ref[...] = jnp.zeros_like(mine_ref)
        for p_ref, st in zip(ins, starts):
            r, wd = p_ref.shape
            mine_ref[st:st + r, 0:wd] = p_ref[...]
        slots_ref[me] = mine_ref[...]
        cps = []
        for kk, (fx, fy, fc) in enumerate(flips):
            cp = pltpu.make_async_remote_copy(
                src_ref=mine_ref, dst_ref=slots_ref.at[me], send_sem=send_sems.at[kk], recv_sem=recv_sems.at[kk],
                device_id=(_flip(x, fx), _flip(y, fy), _flip(c, fc)), device_id_type=MESH)
            cp.start()
            cps.append(cp)
        for cp in cps:
            cp.wait()
        tot = slots_ref[0]
        for d in range(1, 8):
            tot = tot + slots_ref[d]
        mine_ref[...] = tot
        for idx, (o_ref, p_ref, st) in enumerate(zip(outs, ins, starts)):
            r, wd = p_ref.shape
            val = mine_ref[st:st + r, 0:wd]
            o_ref[...] = jnp.sum(val, keepdims=True) if idx == n - 1 else val

    vm = pl.BlockSpec(memory_space=pltpu.VMEM)
    out_shape = [jax.ShapeDtypeStruct(p.shape, F32) for p in pieces[:-1]] + [jax.ShapeDtypeStruct((1, 1), F32)]
    return pl.pallas_call(
        body, name="allreduce_small", in_specs=[vm] * n, out_specs=[vm] * n, out_shape=out_shape,
        scratch_shapes=[pltpu.VMEM((rows, width), F32), pltpu.VMEM((8, rows, width), F32),
                        pltpu.SemaphoreType.DMA((7,)), pltpu.SemaphoreType.DMA((7,))],
    )(*pieces)


def _adamw(w, g, m, v, *, rb, name):
    r, cdim = w.shape

    def body(w_ref, g_ref, m_ref, v_ref, d_ref, nm_ref, nv_ref):
        d_ref[...], nm_ref[...], nv_ref[...] = _adamw_update(w_ref[...], g_ref[...], m_ref[...], v_ref[...])

    spec = pl.BlockSpec((rb, cdim), lambda i: (i, 0))
    shp = jax.ShapeDtypeStruct((r, cdim), F32)
    return pl.pallas_call(
        body, name=name, grid=(r // rb,), in_specs=[spec] * 4, out_specs=[spec] * 3, out_shape=[shp] * 3,
        compiler_params=_cparams(("parallel",)),
    )(w, g, m, v)


def _adamw_update(w, g, m, v):
    nm = ADAM_B1 * m + (1.0 - ADAM_B1) * g
    nv = ADAM_B2 * v + (1.0 - ADAM_B2) * (g * g)
    m_hat = nm / (1.0 - ADAM_B1 ** ADAM_STEP)
    v_hat = nv / (1.0 - ADAM_B2 ** ADAM_STEP)
    return -ADAM_LR * (m_hat / (jnp.sqrt(v_hat) + ADAM_EPS) + ADAM_WD * w), nm, nv


def _adamw_small(params):
    n = len(params)

    def body(*refs):
        ins, outs = refs[:4 * n], refs[4 * n:]
        for p in range(n):
            w_ref, g_ref, m_ref, v_ref = ins[4 * p:4 * p + 4]
            d, nm, nv = _adamw_update(w_ref[...], g_ref[...], m_ref[...], v_ref[...])
            outs[3 * p][...] = d
            outs[3 * p + 1][...] = nm
            outs[3 * p + 2][...] = nv

    vm = pl.BlockSpec(memory_space=pltpu.VMEM)
    flat = [a for p in params for a in p]
    out_shape = [jax.ShapeDtypeStruct(p[0].shape, F32) for p in params for _ in range(3)]
    res = pl.pallas_call(
        body, name="adamw_small", in_specs=[vm] * (4 * n), out_specs=[vm] * (3 * n), out_shape=out_shape,
    )(*flat)
    return [tuple(res[3 * p:3 * p + 3]) for p in range(n)]


def _local_step(x, tgt, meta_full, lb_param, attn_norm_w, w_in_g, hgrn_norm_w, conv_w_full, w_out_full,
                ffn_norm_w, w_up_g, fcw_full, ffn_conv_b, w_down_full, final_norm_w):
    seq = x.shape[0]
    T = TR + seq
    bm = 768 if T % 768 == 0 else TR
    h0 = jnp.concatenate([jnp.zeros((PAD, D), F32), meta_full, x], axis=0)
    whn_t = jnp.tile(hgrn_norm_w, (1, NH))

    u = _norm_fwd(h0, attn_norm_w, bm=bm, name="norm1_fwd")
    proj = _matmul(u, w_in_g, mode="nn", bm=bm, bn=1152, bk=D, out_dtype=F32, name="proj_mm")
    m, o, sst = _mix_fwd(proj, lb_param, whn_t, conv_w_full)
    h1 = _matmul(m, w_out_full, mode="nn", bm=bm, bn=D, bk=D, out_dtype=F32, name="out_mm", residual=h0)
    u2 = _norm_fwd(h1, ffn_norm_w, bm=bm, name="norm2_fwd")
    up = _matmul(u2, w_up_g, mode="nn", bm=bm, bn=1408, bk=D, out_dtype=F32, name="up_mm")
    gg = _ffn_fwd(up, fcw_full, ffn_conv_b)
    h2 = _matmul(gg, w_down_full, mode="nn", bm=bm, bn=D, bk=DFF, out_dtype=F32, name="down_mm", residual=h1)
    dh2, loss_vec, dwfin = _loss_head(h2, tgt, final_norm_w.reshape(1, D))

    dgg = _matmul(dh2, w_down_full, mode="nt", bm=bm, bn=1408, bk=D, out_dtype=F32, name="dgg_mm")
    dup, dfw, dfb = _ffn_bwd(up, dgg, fcw_full, ffn_conv_b)
    du2 = _matmul(dup, w_up_g, mode="nt", bm=bm, bn=D, bk=1408, out_dtype=F32, name="du2_mm")
    dh1, dwffn = _norm_bwd(h1, du2, dh2, ffn_norm_w, bm=bm, name="norm2_bwd")
    dmm = _matmul(dh1, w_out_full, mode="nt", bm=bm, bn=D, bk=D, out_dtype=F32, name="dm_mm")
    dproj, dlb, dwhn, dcw = _mix_bwd(proj, o, dmm, sst, lb_param, whn_t, conv_w_full)
    du = _matmul(dproj, w_in_g, mode="nt", bm=bm, bn=D, bk=1152, out_dtype=F32, name="du_mm")
    grad_x, dmeta, dwattn = _norm_bwd_input(h0, du, dh1, attn_norm_w)

    kb = bm
    g_in = _matmul(u, dproj, mode="tn", bm=D, bn=1152, bk=kb, out_dtype=F32, name="dw_in_mm",
                   out_shards=(NSHARD, 9 * D // NSHARD))
    g_out = _matmul(m, dh1, mode="tn", bm=D, bn=D, bk=kb, out_dtype=F32, name="dw_out_mm")
    g_up = _matmul(u2, dup, mode="tn", bm=D, bn=1408, bk=kb, out_dtype=F32, name="dw_up_mm",
                   out_shards=(NSHARD, 2 * DFF // NSHARD))
    g_down = _matmul(gg, dh2, mode="tn", bm=1408, bn=D, bk=kb, out_dtype=F32, name="dw_down_mm")

    small = dict(dlb=dlb, dwattn=dwattn, dwhn=dwhn, dwffn=dwffn, dfb=dfb, dwfin=dwfin,
                 dcw=dcw, dfw=dfw, dmeta=dmeta, loss=loss_vec)
    big = (g_in, g_out.reshape(NSHARD, D // NSHARD, D), g_up, g_down.reshape(NSHARD, DFF // NSHARD, D))
    return grad_x, small, big


_SMALL_ORDER = ("dmeta", "dcw", "dfw", "dlb", "dwattn", "dwhn", "dwffn", "dfb", "dwfin", "loss")


def kernel(x, meta_tokens, lb_param, attn_norm_w, w_in, hgrn_norm_w, conv_w, w_out, ffn_norm_w, w_up, ffn_conv_w, ffn_conv_b, w_down, final_norm_w, loss_target, m_meta_tokens, m_lb_param, m_attn_norm_w, m_w_in, m_hgrn_norm_w, m_conv_w, m_w_out, m_ffn_norm_w, m_w_up, m_ffn_conv_w, m_ffn_conv_b, m_w_down, m_final_norm_w, v_meta_tokens, v_lb_param, v_attn_norm_w, v_w_in, v_hgrn_norm_w, v_conv_w, v_w_out, v_ffn_norm_w, v_w_up, v_ffn_conv_w, v_ffn_conv_b, v_w_down, v_final_norm_w):
    xi, yi, ci = _place()
    j = 2 * xi + yi
    c_idx = jnp.reshape(ci, (1,)).astype(jnp.int32)

    j_idx = jnp.reshape(j, (1,)).astype(jnp.int32)
    ds_, fs_ = D // NSHARD, DFF // NSHARD
    widen = lambda a: jnp.pad(a, ((0, 0), (0, 768 - a.shape[1])))
    rows_small = jnp.concatenate([widen(meta_tokens), widen(conv_w[0]), widen(ffn_conv_w[0]),
                                  jnp.zeros((2, 768), F32)], axis=0)
    slotted = [_into_slot(w[0], j_idx, rb=rb, dtype=BF16, name=f"slot_{n}")
               for w, rb, n in ((w_in, 256, "w_in"), (w_out, 128, "w_out"), (w_up, 256, "w_up"), (w_down, 176, "w_down"))]
    slotted.append(_into_slot(rows_small, j_idx, rb=rows_small.shape[0], dtype=F32, name="slot_small"))
    w_in_g, w_out_g, w_up_g, w_down_g, small_g = _allgather_weights(slotted)
    unshard = lambda a: jnp.transpose(a, (1, 0, 2)).reshape(a.shape[1], -1)
    meta_full = unshard(small_g[:, 0:NMETA, 0:ds_])
    conv_w_full = unshard(small_g[:, NMETA:NMETA + 3, 0:ds_])
    fcw_full = unshard(small_g[:, NMETA + 3:NMETA + 6, 0:fs_])

    grad_x, small, big = _local_step(
        x[0], loss_target[0], meta_full, lb_param, attn_norm_w, w_in_g, hgrn_norm_w, conv_w_full,
        w_out_g.reshape(D, D), ffn_norm_w, w_up_g, fcw_full, ffn_conv_b, w_down_g.reshape(DFF, D), final_norm_w)

    names = _SMALL_ORDER
    vals = dict(zip(names, _allreduce_small([small[n] for n in names])))
    loss = vals["loss"].reshape(())
    g_small = {
        "meta_tokens": lax.dynamic_slice_in_dim(vals["dmeta"], j * (D // NSHARD), D // NSHARD, axis=1),
        "lb_param": jnp.concatenate([vals["dlb"], -vals["dlb"]], axis=0),
        "attn_norm_w": vals["dwattn"],
        "hgrn_norm_w": vals["dwhn"],
        "conv_w": lax.dynamic_slice_in_dim(vals["dcw"], j * (D // NSHARD), D // NSHARD, axis=1)[None],
        "ffn_norm_w": vals["dwffn"],
        "ffn_conv_w": lax.dynamic_slice_in_dim(vals["dfw"], j * (DFF // NSHARD), DFF // NSHARD, axis=1)[None],
        "ffn_conv_b": vals["dfb"],
        "final_norm_w": vals["dwfin"].reshape(D),
    }

    others = _pair_exchange(big)
    rbs = (256, 128, 256, 176)
    psums = [_pair_add(g, o_, c_idx, rb=rb, name=f"pair_add_{n}") for g, o_, rb, n in zip(big, others, rbs, "0123")]
    parts = _chip_exchange(psums)
    cj_idx = jnp.stack([ci, j]).astype(jnp.int32)
    halves = [_chip_sum(ps, p, cj_idx, rb=rb, name=f"chip_sum_{n}") for ps, p, rb, n in zip(psums, parts, rbs, "0123")]
    g_big = _pair_gather(halves)

    weights = {"meta_tokens": meta_tokens, "lb_param": lb_param, "attn_norm_w": attn_norm_w, "w_in": w_in,
               "hgrn_norm_w": hgrn_norm_w, "conv_w": conv_w, "w_out": w_out, "ffn_norm_w": ffn_norm_w,
               "w_up": w_up, "ffn_conv_w": ffn_conv_w, "ffn_conv_b": ffn_conv_b, "w_down": w_down,
               "final_norm_w": final_norm_w}
    ms = {"meta_tokens": m_meta_tokens, "lb_param": m_lb_param, "attn_norm_w": m_attn_norm_w, "w_in": m_w_in,
          "hgrn_norm_w": m_hgrn_norm_w, "conv_w": m_conv_w, "w_out": m_w_out, "ffn_norm_w": m_ffn_norm_w,
          "w_up": m_w_up, "ffn_conv_w": m_ffn_conv_w, "ffn_conv_b": m_ffn_conv_b, "w_down": m_w_down,
          "final_norm_w": m_final_norm_w}
    vs = {"meta_tokens": v_meta_tokens, "lb_param": v_lb_param, "attn_norm_w": v_attn_norm_w, "w_in": v_w_in,
          "hgrn_norm_w": v_hgrn_norm_w, "conv_w": v_conv_w, "w_out": v_w_out, "ffn_norm_w": v_ffn_norm_w,
          "w_up": v_w_up, "ffn_conv_w": v_ffn_conv_w, "ffn_conv_b": v_ffn_conv_b, "w_down": v_w_down,
          "final_norm_w": v_final_norm_w}
    order = list(weights)
    grads, deltas, new_m, new_v = {}, {}, {}, {}

    for name, g, rb in zip(("w_in", "w_out", "w_up", "w_down"), g_big, (256, 128, 256, 176)):
        shp = weights[name].shape
        w2, m2, v2 = (a.reshape(shp[1], shp[2]) for a in (weights[name], ms[name], vs[name]))
        d_, nm_, nv_ = _adamw(w2, g, m2, v2, rb=rb, name=f"adamw_{name}")
        grads[name], deltas[name], new_m[name], new_v[name] = (a.reshape(shp) for a in (g, d_, nm_, nv_))

    small_names = [n for n in order if n not in grads]
    as2d = lambda a: a.reshape(-1, a.shape[-1])
    res = _adamw_small([tuple(as2d(a) for a in (weights[n], g_small[n], ms[n], vs[n])) for n in small_names])
    for n, (d_, nm_, nv_) in zip(small_names, res):
        shp = weights[n].shape
        grads[n], deltas[n], new_m[n], new_v[n] = (a.reshape(shp) for a in (g_small[n], d_, nm_, nv_))

    return (loss, grad_x[None], *[grads[n] for n in order], *[deltas[n] for n in order],
            *[new_m[n] for n in order], *[new_v[n] for n in order])
```

```python
import functools

import jax
import jax.numpy as jnp
from jax import lax
from jax.experimental import pallas as pl
from jax.experimental.pallas import tpu as pltpu

F32 = jnp.float32
BF16 = jnp.bfloat16
MESH = pl.DeviceIdType.MESH

D = 1024
NH = 8
HD = 128
DFF = 2816
NMETA = 16
EPS = 1e-6
TR = 256
PAD = TR - NMETA
CH = 64
NSHARD = 4
VMEM_LIMIT = 56 * 1024 * 1024

ADAM_LR = 0.001
ADAM_B1 = 0.9
ADAM_B2 = 0.999
ADAM_EPS = 1e-08
ADAM_WD = 0.01
ADAM_STEP = 10


def _cparams(semantics=None, **kw):
    return pltpu.CompilerParams(dimension_semantics=semantics, vmem_limit_bytes=VMEM_LIMIT, **kw)


def _sigmoid(x):
    return 1.0 / (1.0 + jnp.exp(-x))


def _matmul(a, b, *, mode, bm, bn, bk, out_dtype, name, residual=None, out_shards=None):
    if mode == "tn":
        K, M = a.shape
    else:
        M, K = a.shape
    b3 = b.ndim == 3
    if b3:
        S, R, Cs = b.shape
        bcols = S * Cs
        brows = R
    else:
        brows, bcols = b.shape
    if mode == "nt":
        N = brows
        assert bcols == K
    else:
        N = bcols
        assert brows == K
    assert M % bm == 0 and N % bn == 0 and K % bk == 0, (name, M, N, K, bm, bn, bk)
    nm, nn, nk = M // bm, N // bn, K // bk

    if mode == "tn":
        a_spec = pl.BlockSpec((bk, bm), lambda i, j, k: (k, i))
    else:
        a_spec = pl.BlockSpec((bm, bk), lambda i, j, k: (i, k))
    if mode == "nt":
        if b3:
            assert bn == R and Cs % bk == 0
            per = Cs // bk
            b_spec = pl.BlockSpec((None, bn, bk), lambda i, j, k: (k // per, 0, k % per))
        else:
            b_spec = pl.BlockSpec((bn, bk), lambda i, j, k: (j, k))
    else:
        if b3:
            assert bk == R and Cs % bn == 0
            per = Cs // bn
            b_spec = pl.BlockSpec((None, bk, bn), lambda i, j, k: (j // per, 0, j % per))
        else:
            b_spec = pl.BlockSpec((bk, bn), lambda i, j, k: (k, j))
    in_specs = [a_spec, b_spec]
    args = [a, b]
    if residual is not None:
        in_specs.append(pl.BlockSpec((bm, bn), lambda i, j, k: (i, j)))
        args.append(residual)
    if out_shards is not None:
        So, Co = out_shards
        assert So * Co == N and Co % bn == 0 and bm == M
        pero = Co // bn
        out_shape = jax.ShapeDtypeStruct((So, M, Co), out_dtype)
        out_spec = pl.BlockSpec((None, bm, bn), lambda i, j, k: (j // pero, 0, j % pero))
    else:
        out_shape = jax.ShapeDtypeStruct((M, N), out_dtype)
        out_spec = pl.BlockSpec((bm, bn), lambda i, j, k: (i, j))

    def body(*refs):
        if residual is not None:
            a_ref, b_ref, r_ref, o_ref = refs[:4]
            scr = refs[4:]
        else:
            a_ref, b_ref, o_ref = refs[:3]
            r_ref = None
            scr = refs[3:]
        av = a_ref[...].astype(BF16)
        bv = b_ref[...].astype(BF16)
        if mode == "nn":
            p = jnp.dot(av, bv, preferred_element_type=F32)
        elif mode == "nt":
            p = lax.dot_general(av, bv, (((1,), (1,)), ((), ())), preferred_element_type=F32)
        else:
            p = lax.dot_general(av, bv, (((0,), (0,)), ((), ())), preferred_element_type=F32)

        def finish(acc):
            if r_ref is not None:
                acc = acc + r_ref[...]
            o_ref[...] = acc.astype(out_dtype)

        if nk == 1:
            finish(p)
        else:
            acc_ref = scr[0]
            k = pl.program_id(2)

            @pl.when(k == 0)
            def _():
                acc_ref[...] = p

            @pl.when(k > 0)
            def _():
                acc_ref[...] += p

            @pl.when(k == nk - 1)
            def _():
                finish(acc_ref[...])

    scratch = [] if nk == 1 else [pltpu.VMEM((bm, bn), F32)]
    return pl.pallas_call(
        body, name=name, grid=(nm, nn, nk), in_specs=in_specs, out_specs=out_spec, out_shape=out_shape,
        scratch_shapes=scratch, compiler_params=_cparams(("parallel", "parallel", "arbitrary")),
    )(*args)


def _norm_fwd(h, w, *, bm, name):
    T = h.shape[0]

    def body(h_ref, w_ref, o_ref):
        x = h_ref[...]
        r = lax.rsqrt(jnp.mean(x * x, axis=-1, keepdims=True) + EPS)
        o_ref[...] = (x * r * w_ref[...]).astype(BF16)

    return pl.pallas_call(
        body, name=name, grid=(T // bm,),
        in_specs=[pl.BlockSpec((bm, D), lambda i: (i, 0)), pl.BlockSpec((1, D), lambda i: (0, 0))],
        out_specs=pl.BlockSpec((bm, D), lambda i: (i, 0)),
        out_shape=jax.ShapeDtypeStruct((T, D), BF16), compiler_params=_cparams(("parallel",)),
    )(h, w)


def _norm_bwd(h, du, dres, w, *, bm, name):
    T = h.shape[0]

    def body(h_ref, du_ref, dres_ref, w_ref, dh_ref, dw_ref):
        i = pl.program_id(0)
        x = h_ref[...]
        r = lax.rsqrt(jnp.mean(x * x, axis=-1, keepdims=True) + EPS)
        n = x * r
        du_v = du_ref[...]
        dn = du_v * w_ref[...]
        dh_ref[...] = dres_ref[...] + r * (dn - n * jnp.mean(dn * n, axis=-1, keepdims=True))
        part = jnp.sum(du_v * n, axis=0, keepdims=True)

        @pl.when(i == 0)
        def _():
            dw_ref[...] = part

        @pl.when(i > 0)
        def _():
            dw_ref[...] += part

    row = pl.BlockSpec((bm, D), lambda i: (i, 0))
    vec = pl.BlockSpec((1, D), lambda i: (0, 0))
    return pl.pallas_call(
        body, name=name, grid=(T // bm,), in_specs=[row, row, row, vec], out_specs=[row, vec],
        out_shape=[jax.ShapeDtypeStruct((T, D), F32), jax.ShapeDtypeStruct((1, D), F32)],
        compiler_params=_cparams(("arbitrary",)),
    )(h, du, dres, w)


def _norm_bwd_input(h, du, dres, w):
    T = h.shape[0]
    nt = T // TR

    def body(h_ref, du_ref, dres_ref, w_ref, gx_ref, dmeta_ref, dw_ref):
        i = pl.program_id(0)
        x = h_ref[...]
        r = lax.rsqrt(jnp.mean(x * x, axis=-1, keepdims=True) + EPS)
        n = x * r
        du_v = du_ref[...]
        dn = du_v * w_ref[...]
        dh = dres_ref[...] + r * (dn - n * jnp.mean(dn * n, axis=-1, keepdims=True))
        gx_ref[...] = dh
        part = jnp.sum(du_v * n, axis=0, keepdims=True)

        @pl.when(i == 0)
        def _():
            dmeta_ref[...] = dh[PAD:TR, :]
            dw_ref[...] = part

        @pl.when(i > 0)
        def _():
            dw_ref[...] += part

    row = pl.BlockSpec((TR, D), lambda i: (i, 0))
    vec = pl.BlockSpec((1, D), lambda i: (0, 0))
    return pl.pallas_call(
        body, name="norm1_bwd", grid=(nt,), in_specs=[row, row, row, vec],
        out_specs=[pl.BlockSpec((TR, D), lambda i: (jnp.maximum(i - 1, 0), 0)),
                   pl.BlockSpec((NMETA, D), lambda i: (0, 0)), vec],
        out_shape=[jax.ShapeDtypeStruct((T - TR, D), F32), jax.ShapeDtypeStruct((NMETA, D), F32),
                   jax.ShapeDtypeStruct((1, D), F32)],
        compiler_params=_cparams(("arbitrary",)),
    )(h, du, dres, w)


def _split3(x):
    hi = x.astype(BF16)
    r1 = x - hi.astype(F32)
    mid = r1.astype(BF16)
    lo = (r1 - mid.astype(F32)).astype(BF16)
    return hi, mid, lo


def _tri_matmul(tri_bf16, x):
    hi, mid, lo = _split3(x)
    out = jnp.dot(tri_bf16, lo, preferred_element_type=F32)
    out = out + jnp.dot(tri_bf16, mid, preferred_element_type=F32)
    return out + jnp.dot(tri_bf16, hi, preferred_element_type=F32)


def _shift_down(x, prev8, n):
    rows = x.shape[0]
    rid = lax.broadcasted_iota(jnp.int32, x.shape, 0)
    y = pltpu.roll(x, n, 0)
    for t in range(n):
        y = jnp.where(rid == t, prev8[8 - n + t:8 - n + t + 1, :], y)
    del rows
    return y


def _shift_up(x, next8, n):
    rows = x.shape[0]
    rid = lax.broadcasted_iota(jnp.int32, x.shape, 0)
    y = pltpu.roll(x, rows - n, 0)
    for t in range(n):
        y = jnp.where(rid == rows - n + t, next8[t:t + 1, :], y)
    return y


def _gates(f_raw, lb):
    sg = _sigmoid(f_raw)
    f = lb + (1.0 - lb) * sg
    return sg, f, jnp.log(f), 1.0 - f


def _lower_bound(lbp_ref):
    return _sigmoid(lbp_ref[0:1, :] - lbp_ref[1:2, :])


def _tri_masks():
    r = lax.broadcasted_iota(jnp.int32, (CH, CH), 0)
    c = lax.broadcasted_iota(jnp.int32, (CH, CH), 1)
    return r >= c, r <= c


def _ones_where(mask):
    return jnp.where(mask, 1.0, 0.0).astype(BF16)


def _dot(a, b):
    return jnp.dot(a.astype(BF16), b.astype(BF16), preferred_element_type=F32)


def _dot_nt(a, b):
    return lax.dot_general(a.astype(BF16), b.astype(BF16), (((1,), (1,)), ((), ())), preferred_element_type=F32)


def _dot_tn(a, b):
    return lax.dot_general(a.astype(BF16), b.astype(BF16), (((0,), (0,)), ((), ())), preferred_element_type=F32)


def _mix_block_fwd(h0, lb_param, wattn, whn, conv_w, w_in_g, w_out):
    T = h0.shape[0]
    nt = T // TR
    ncht = TR // CH
    nsl = w_in_g.shape[0]
    wsl = w_in_g.shape[2]
    wcol = wsl // 2

    def body(h_ref, lbp_ref, wa_ref, whn_ref, cw_ref, win_ref, wout_ref,
             u_ref, projb_ref, o_ref, sst_ref, m_ref, h1_ref, proj_ref, st_ref, cxc_ref):
        i = pl.program_id(0)

        @pl.when(i == 0)
        def _():
            st_ref[...] = jnp.zeros_like(st_ref)
            cxc_ref[...] = jnp.zeros_like(cxc_ref)

        x = h_ref[...]
        r1 = lax.rsqrt(jnp.mean(x * x, axis=-1, keepdims=True) + EPS)
        u = (x * r1 * wa_ref[...]).astype(BF16)
        u_ref[...] = u
        for s in range(nsl):
            for hf in range(2):
                cols = slice(s * wsl + hf * wcol, s * wsl + (hf + 1) * wcol)
                p = jnp.dot(u, win_ref[s, :, hf * wcol:(hf + 1) * wcol], preferred_element_type=F32)
                proj_ref[:, cols] = p
                projb_ref[:, cols] = p.astype(BF16)

        lb = _lower_bound(lbp_ref)
        lower, _ = _tri_masks()
        ltri = _ones_where(lower)
        whn_v = whn_ref[...]
        w0, w1, w2 = cw_ref[0:1, :], cw_ref[1:2, :], cw_ref[2:3, :]

        def chunk(c, carry):
            rows = pl.ds(pl.multiple_of(c * CH, CH), CH)
            q_raw = proj_ref[rows, 0:D]
            f_raw = proj_ref[rows, D:2 * D]
            v = proj_ref[rows, 2 * D:3 * D]
            q = q_raw * _sigmoid(q_raw)
            _, _, g, k = _gates(f_raw, lb)
            gam = _tri_matmul(ltri, g)
            gam_l = gam[CH - 1:CH, :]
            e_l = jnp.exp(gam_l)
            qt = (q * jnp.exp(gam)).astype(BF16)
            kt = (k * jnp.exp(-gam)).astype(BF16)
            khat = (k * jnp.exp(gam_l - gam)).astype(BF16)
            vb = v.astype(BF16)
            on_parts = []
            for h in range(NH):
                cs = slice(h * HD, (h + 1) * HD)
                st = st_ref[h]
                sst_ref[c, h] = st
                a = jnp.where(lower, _dot_nt(qt[:, cs], kt[:, cs]), 0.0)
                o_h = _dot_nt(qt[:, cs], st) + _dot(a, vb[:, cs])
                st_ref[h] = st * e_l[:, cs] + _dot_tn(vb[:, cs], khat[:, cs])
                o_ref[rows, cs] = o_h
                ro = lax.rsqrt(jnp.mean(o_h * o_h, axis=-1, keepdims=True) + EPS)
                on_parts.append(o_h * ro)
            on = jnp.concatenate(on_parts, axis=1)
            g_out = proj_ref[rows, 3 * D:4 * D]
            y_a = on * whn_v * (g_out * _sigmoid(g_out))
            cx = proj_ref[rows, 5 * D:6 * D] * proj_ref[rows, 6 * D:7 * D]
            prev8 = cxc_ref[...]
            cv = w0 * _shift_down(cx, prev8, 2) + w1 * _shift_down(cx, prev8, 1) + w2 * cx
            cxc_ref[...] = cx[CH - 8:CH, :]
            y_b = proj_ref[rows, 4 * D:5 * D] * cv
            m = _sigmoid(proj_ref[rows, 7 * D:8 * D]) * y_a + _sigmoid(proj_ref[rows, 8 * D:9 * D]) * y_b
            m_ref[rows, :] = m.astype(BF16)
            return carry

        lax.fori_loop(0, ncht, chunk, 0)
        h1_ref[...] = x + jnp.dot(m_ref[...], wout_ref[...], preferred_element_type=F32)

    row = lambda w: pl.BlockSpec((TR, w), lambda i: (i, 0))
    vec = lambda r: pl.BlockSpec((r, D), lambda i: (0, 0))
    vm = pl.BlockSpec(memory_space=pltpu.VMEM)
    return pl.pallas_call(
        body, name="mix_block_fwd", grid=(nt,),
        in_specs=[row(D), vec(2), vec(1), vec(1), vec(3), vm, vm],
        out_specs=[row(D), row(9 * D), row(D), pl.BlockSpec((ncht, NH, HD, HD), lambda i: (i, 0, 0, 0)),
                   row(D), row(D)],
        out_shape=[jax.ShapeDtypeStruct((T, D), BF16), jax.ShapeDtypeStruct((T, 9 * D), BF16),
                   jax.ShapeDtypeStruct((T, D), F32), jax.ShapeDtypeStruct((T // CH, NH, HD, HD), F32),
                   jax.ShapeDtypeStruct((T, D), BF16), jax.ShapeDtypeStruct((T, D), F32)],
        scratch_shapes=[pltpu.VMEM((TR, 9 * D), F32), pltpu.VMEM((NH, HD, HD), F32), pltpu.VMEM((8, D), F32)],
        compiler_params=_cparams(("arbitrary",)),
    )(h0, lb_param, wattn, whn, conv_w, w_in_g, w_out)


def _mix_block_bwd(dh1, projb, o, sst, lb_param, whn, conv_w, w_out):
    T = projb.shape[0]
    nt = T // TR
    ncht = TR // CH
    tb16 = TR // 16

    def body(dh1_ref, proj_ref, pc_ref, px_ref, o_ref, sst_ref, lbp_ref, whn_ref, cw_ref, wout_ref,
             dproj_ref, dlb_ref, dwhn_ref, dcw_ref, dm_ref, dst_ref, dcvc_ref, acc_lb, acc_hn, acc_cw):
        s = pl.program_id(0)
        tile = nt - 1 - s

        @pl.when(s == 0)
        def _():
            dst_ref[...] = jnp.zeros_like(dst_ref)
            dcvc_ref[...] = jnp.zeros_like(dcvc_ref)
            acc_lb[...] = jnp.zeros_like(acc_lb)
            acc_hn[...] = jnp.zeros_like(acc_hn)
            acc_cw[...] = jnp.zeros_like(acc_cw)

        dm_ref[...] = _dot_nt(dh1_ref[...], wout_ref[...])
        lb = _lower_bound(lbp_ref)
        lower, upper = _tri_masks()
        ltri = _ones_where(lower)
        utri = _ones_where(upper)
        whn_v = whn_ref[...]
        w0, w1, w2 = cw_ref[0:1, :], cw_ref[1:2, :], cw_ref[2:3, :]
        cx_before_tile = jnp.where(tile > 0, (pc_ref[...].astype(F32) * px_ref[...].astype(F32))[8:16, :], 0.0)
        rid = lax.broadcasted_iota(jnp.int32, (CH, D), 0)

        def chunk(cc, carry):
            c = ncht - 1 - cc
            r0 = pl.multiple_of(c * CH, CH)
            rows = pl.ds(r0, CH)
            slab = lambda n: proj_ref[rows, n * D:(n + 1) * D].astype(F32)
            q_raw, f_raw, v, g_out, b_gate, c_gate, x_conv = (slab(n) for n in range(7))
            sa = _sigmoid(slab(7))
            sb = _sigmoid(slab(8))
            dm_v = dm_ref[rows, :]

            sq = _sigmoid(q_raw)
            q = q_raw * sq
            sg, f, g, k = _gates(f_raw, lb)
            gam = _tri_matmul(ltri, g)
            gam_l = gam[CH - 1:CH, :]
            e_l = jnp.exp(gam_l)
            e_g = jnp.exp(gam)
            e_ng = jnp.exp(-gam)
            e_kl = jnp.exp(gam_l - gam)
            qt = q * e_g
            kt = k * e_ng
            khat = k * e_kl
            s_go = _sigmoid(g_out)
            silu_go = g_out * s_go
            cx = c_gate * x_conv
            rprev = pl.ds(pl.multiple_of(jnp.maximum(r0 - 16, 0), 16), 16)
            cx_prev_in = (proj_ref[rprev, 5 * D:6 * D].astype(F32) * proj_ref[rprev, 6 * D:7 * D].astype(F32))[8:16, :]
            prev8 = jnp.where(c > 0, cx_prev_in, cx_before_tile)
            cx_m1 = _shift_down(cx, prev8, 1)
            cx_m2 = _shift_down(cx, prev8, 2)
            cv = w0 * cx_m2 + w1 * cx_m1 + w2 * cx
            y_b = b_gate * cv

            o_v = o_ref[rows, :]
            ro_parts, on_parts = [], []
            for h in range(NH):
                cs = slice(h * HD, (h + 1) * HD)
                o_h = o_v[:, cs]
                ro = lax.rsqrt(jnp.mean(o_h * o_h, axis=-1, keepdims=True) + EPS)
                ro_parts.append(ro)
                on_parts.append(o_h * ro)
            on = jnp.concatenate(on_parts, axis=1)
            y_a = on * whn_v * silu_go

            dy_a = dm_v * sa
            dy_b = dm_v * sb
            dproj_ref[rows, 7 * D:8 * D] = (dm_v * y_a * sa * (1.0 - sa)).astype(BF16)
            dproj_ref[rows, 8 * D:9 * D] = (dm_v * y_b * sb * (1.0 - sb)).astype(BF16)
            dproj_ref[rows, 4 * D:5 * D] = (dy_b * cv).astype(BF16)
            dcv = dy_b * b_gate
            acc_cw[0:1, :] += jnp.sum(dcv * cx_m2, axis=0, keepdims=True)
            acc_cw[1:2, :] += jnp.sum(dcv * cx_m1, axis=0, keepdims=True)
            acc_cw[2:3, :] += jnp.sum(dcv * cx, axis=0, keepdims=True)
            next8 = dcvc_ref[...]
            dcx = w2 * dcv + w1 * _shift_up(dcv, next8, 1) + w0 * _shift_up(dcv, next8, 2)
            dcvc_ref[...] = dcv[0:8, :]
            dproj_ref[rows, 5 * D:6 * D] = (dcx * x_conv).astype(BF16)
            dproj_ref[rows, 6 * D:7 * D] = (dcx * c_gate).astype(BF16)
            don = dy_a * whn_v * silu_go
            dproj_ref[rows, 3 * D:4 * D] = (dy_a * on * whn_v * (s_go * (1.0 + g_out * (1.0 - s_go)))).astype(BF16)
            acc_hn[...] += jnp.sum(dy_a * silu_go * on, axis=0, keepdims=True)

            dq_parts, dk_parts, dv_parts, dgam_parts, ext_parts = [], [], [], [], []
            for h in range(NH):
                cs = slice(h * HD, (h + 1) * HD)
                on_h = on_parts[h]
                don_h = don[:, cs]
                do_h = ro_parts[h] * (don_h - on_h * jnp.mean(don_h * on_h, axis=-1, keepdims=True))
                qt_h, kt_h, khat_h, v_h = qt[:, cs], kt[:, cs], khat[:, cs], v[:, cs]
                st = sst_ref[c, h]
                dstn = dst_ref[h]
                a_t = jnp.where(upper, _dot_nt(kt_h, qt_h), 0.0)
                da = jnp.where(lower, _dot_nt(do_h, v_h), 0.0)
                da_t = jnp.where(upper, _dot_nt(v_h, do_h), 0.0)
                dv_h = _dot(a_t, do_h) + _dot_nt(khat_h, dstn)
                dqt_state = _dot(do_h, st)
                dqt_chunk = _dot(da, kt_h)
                dkt = _dot(da_t, qt_h)
                dkhat = _dot(v_h, dstn)
                dq_h = (dqt_state + dqt_chunk) * e_g[:, cs]
                dk_h = dkt * e_ng[:, cs] + dkhat * e_kl[:, cs]
                khat_dkhat = dkhat * khat_h
                ext = (jnp.sum(khat_dkhat, axis=0, keepdims=True)
                       + e_l[:, cs] * jnp.sum(st * dstn, axis=0, keepdims=True))
                dst_ref[h] = _dot_tn(do_h, qt_h) + dstn * e_l[:, cs]
                dq_parts.append(dq_h)
                dk_parts.append(dk_h)
                dv_parts.append(dv_h)
                qt_seen = qt_h.astype(BF16).astype(F32)
                kt_seen = kt_h.astype(BF16).astype(F32)
                dgam_parts.append(qt_h * dqt_state + qt_seen * dqt_chunk - kt_seen * dkt - khat_dkhat)
                ext_parts.append(ext)
            dq = jnp.concatenate(dq_parts, axis=1)
            dk = jnp.concatenate(dk_parts, axis=1)
            dgam = jnp.concatenate(dgam_parts, axis=1)
            ext = jnp.concatenate(ext_parts, axis=1)
            dgam = dgam + jnp.where(rid == CH - 1, ext, 0.0)
            dg = _tri_matmul(utri, dgam)
            dproj_ref[rows, 0:D] = (dq * (sq * (1.0 + q_raw * (1.0 - sq)))).astype(BF16)
            df = dg / f - dk
            dproj_ref[rows, D:2 * D] = (df * (1.0 - lb) * sg * (1.0 - sg)).astype(BF16)
            dproj_ref[rows, 2 * D:3 * D] = jnp.concatenate(dv_parts, axis=1).astype(BF16)
            real = (tile * TR + r0 + rid) >= PAD
            acc_lb[...] += jnp.sum(jnp.where(real, df * (1.0 - sg), 0.0), axis=0, keepdims=True)
            return carry

        lax.fori_loop(0, ncht, chunk, 0)

        @pl.when(s == nt - 1)
        def _():
            dlb_ref[...] = acc_lb[...] * lb * (1.0 - lb)
            hn = acc_hn[...]
            tot = hn[:, 0:HD]
            for h in range(1, NH):
                tot = tot + hn[:, h * HD:(h + 1) * HD]
            dwhn_ref[...] = tot
            dcw_ref[...] = acc_cw[0:3, :]

    rev = lambda s: (nt - 1 - s, 0)
    prevc = lambda s: (jnp.maximum((nt - 1 - s) * tb16 - 1, 0), 5)
    prevx = lambda s: (jnp.maximum((nt - 1 - s) * tb16 - 1, 0), 6)
    const = lambda s: (0, 0)
    return pl.pallas_call(
        body, name="mix_block_bwd", grid=(nt,),
        in_specs=[pl.BlockSpec((TR, D), rev),
                  pl.BlockSpec((TR, 9 * D), rev),
                  pl.BlockSpec((16, D), prevc),
                  pl.BlockSpec((16, D), prevx),
                  pl.BlockSpec((TR, D), rev),
                  pl.BlockSpec((ncht, NH, HD, HD), lambda s: (nt - 1 - s, 0, 0, 0)),
                  pl.BlockSpec((2, D), const),
                  pl.BlockSpec((1, D), const),
                  pl.BlockSpec((3, D), const),
                  pl.BlockSpec(memory_space=pltpu.VMEM)],
        out_specs=[pl.BlockSpec((TR, 9 * D), rev),
                   pl.BlockSpec((1, D), const),
                   pl.BlockSpec((1, HD), const),
                   pl.BlockSpec((3, D), const)],
        out_shape=[jax.ShapeDtypeStruct((T, 9 * D), BF16), jax.ShapeDtypeStruct((1, D), F32),
                   jax.ShapeDtypeStruct((1, HD), F32), jax.ShapeDtypeStruct((3, D), F32)],
        scratch_shapes=[pltpu.VMEM((TR, D), F32), pltpu.VMEM((NH, HD, HD), F32), pltpu.VMEM((8, D), F32),
                        pltpu.VMEM((1, D), F32), pltpu.VMEM((1, D), F32), pltpu.VMEM((8, D), F32)],
        compiler_params=_cparams(("arbitrary",)),
    )(dh1, projb, projb, projb, o, sst, lb_param, whn, conv_w, w_out)


def _ffn_fwd(up, fcw, fcb):
    T = up.shape[0]
    nt = T // TR

    def body(up_ref, w_ref, b_ref, gg_ref, carry_ref):
        i = pl.program_id(0)

        @pl.when(i == 0)
        def _():
            carry_ref[...] = jnp.zeros_like(carry_ref)

        w0, w1, w2 = w_ref[0:1, :], w_ref[1:2, :], w_ref[2:3, :]

        def chunk(c, carry):
            rows = pl.ds(pl.multiple_of(c * CH, CH), CH)
            a_pre = up_ref[rows, 0:DFF]
            val = up_ref[rows, DFF:2 * DFF]
            prev8 = carry_ref[...]
            a = w0 * _shift_down(a_pre, prev8, 2) + w1 * _shift_down(a_pre, prev8, 1) + w2 * a_pre + b_ref[...]
            carry_ref[...] = a_pre[CH - 8:CH, :]
            gg_ref[rows, :] = (a * _sigmoid(a) * val).astype(BF16)
            return carry

        lax.fori_loop(0, TR // CH, chunk, 0)

    return pl.pallas_call(
        body, name="ffn_fwd", grid=(nt,),
        in_specs=[pl.BlockSpec((TR, 2 * DFF), lambda i: (i, 0)),
                  pl.BlockSpec((3, DFF), lambda i: (0, 0)),
                  pl.BlockSpec((1, DFF), lambda i: (0, 0))],
        out_specs=pl.BlockSpec((TR, DFF), lambda i: (i, 0)),
        out_shape=jax.ShapeDtypeStruct((T, DFF), BF16),
        scratch_shapes=[pltpu.VMEM((8, DFF), F32)],
        compiler_params=_cparams(("arbitrary",)),
    )(up, fcw, fcb)


def _ffn_bwd(up, dgg, fcw, fcb):
    T = up.shape[0]
    nt = T // TR
    tb = TR // 8
    ncht = TR // CH

    def body(up_ref, pa_ref, dgg_ref, w_ref, b_ref, dup_ref, dfw_ref, dfb_ref, carry_ref, acc_w, acc_b):
        s = pl.program_id(0)
        tile = nt - 1 - s

        @pl.when(s == 0)
        def _():
            carry_ref[...] = jnp.zeros_like(carry_ref)
            acc_w[...] = jnp.zeros_like(acc_w)
            acc_b[...] = jnp.zeros_like(acc_b)

        w0, w1, w2 = w_ref[0:1, :], w_ref[1:2, :], w_ref[2:3, :]
        a_before_tile = jnp.where(tile > 0, pa_ref[...], 0.0)

        def chunk(cc, carry):
            c = ncht - 1 - cc
            r0 = pl.multiple_of(c * CH, CH)
            rows = pl.ds(r0, CH)
            a_pre = up_ref[rows, 0:DFF]
            val = up_ref[rows, DFF:2 * DFF]
            rprev = pl.multiple_of(jnp.maximum(r0 - 8, 0), 8)
            prev8 = jnp.where(c > 0, up_ref[pl.ds(rprev, 8), 0:DFF], a_before_tile)
            a_m1 = _shift_down(a_pre, prev8, 1)
            a_m2 = _shift_down(a_pre, prev8, 2)
            a = w0 * a_m2 + w1 * a_m1 + w2 * a_pre + b_ref[...]
            sig = _sigmoid(a)
            dgg_v = dgg_ref[rows, :]
            da = dgg_v * val * (sig * (1.0 + a * (1.0 - sig)))
            dup_ref[rows, DFF:2 * DFF] = (dgg_v * (a * sig)).astype(BF16)
            next8 = carry_ref[...]
            dup_ref[rows, 0:DFF] = (w2 * da + w1 * _shift_up(da, next8, 1) + w0 * _shift_up(da, next8, 2)).astype(BF16)
            carry_ref[...] = da[0:8, :]
            acc_w[0:1, :] += jnp.sum(da * a_m2, axis=0, keepdims=True)
            acc_w[1:2, :] += jnp.sum(da * a_m1, axis=0, keepdims=True)
            acc_w[2:3, :] += jnp.sum(da * a_pre, axis=0, keepdims=True)
            acc_b[...] += jnp.sum(da, axis=0, keepdims=True)
            return carry

        lax.fori_loop(0, ncht, chunk, 0)

        @pl.when(s == nt - 1)
        def _():
            dfw_ref[...] = acc_w[0:3, :]
            dfb_ref[...] = acc_b[...]

    rev = lambda s: (nt - 1 - s, 0)
    const = lambda s: (0, 0)
    return pl.pallas_call(
        body, name="ffn_bwd", grid=(nt,),
        in_specs=[pl.BlockSpec((TR, 2 * DFF), rev),
                  pl.BlockSpec((8, DFF), lambda s: (jnp.maximum((nt - 1 - s) * tb - 1, 0), 0)),
                  pl.BlockSpec((TR, DFF), rev),
                  pl.BlockSpec((3, DFF), const),
                  pl.BlockSpec((1, DFF), const)],
        out_specs=[pl.BlockSpec((TR, 2 * DFF), rev),
                   pl.BlockSpec((3, DFF), const),
                   pl.BlockSpec((1, DFF), const)],
        out_shape=[jax.ShapeDtypeStruct((T, 2 * DFF), BF16), jax.ShapeDtypeStruct((3, DFF), F32),
                   jax.ShapeDtypeStruct((1, DFF), F32)],
        scratch_shapes=[pltpu.VMEM((8, DFF), F32), pltpu.VMEM((8, DFF), F32), pltpu.VMEM((1, DFF), F32)],
        compiler_params=_cparams(("arbitrary",)),
    )(up, up, dgg, fcw, fcb)


def _loss_head(h2, tgt, wfin):
    T = h2.shape[0]
    nt = T // TR

    def body(h_ref, t_ref, w_ref, dh_ref, loss_ref, dw_ref):
        i = pl.program_id(0)

        @pl.when(i == 0)
        def _():
            loss_ref[...] = jnp.zeros_like(loss_ref)
            dw_ref[...] = jnp.zeros_like(dw_ref)

        x = h_ref[...]
        r = lax.rsqrt(jnp.mean(x * x, axis=-1, keepdims=True) + EPS)
        n = x * r
        w = w_ref[...]
        diff = jnp.where(i > 0, n * w - t_ref[...], 0.0)
        loss_ref[...] += jnp.sum(diff * diff, axis=0, keepdims=True) * (0.5 / D)
        dy = diff * (1.0 / D)
        dw_ref[...] += jnp.sum(dy * n, axis=0, keepdims=True)
        dn = dy * w
        dh_ref[...] = r * (dn - n * jnp.mean(dn * n, axis=-1, keepdims=True))

    row = pl.BlockSpec((TR, D), lambda i: (i, 0))
    vec = pl.BlockSpec((1, D), lambda i: (0, 0))
    return pl.pallas_call(
        body, name="loss_head", grid=(nt,),
        in_specs=[row, pl.BlockSpec((TR, D), lambda i: (jnp.maximum(i - 1, 0), 0)), vec],
        out_specs=[row, vec, vec],
        out_shape=[jax.ShapeDtypeStruct((T, D), F32), jax.ShapeDtypeStruct((1, D), F32),
                   jax.ShapeDtypeStruct((1, D), F32)],
        compiler_params=_cparams(("arbitrary",)),
    )(h2, tgt, wfin)


def _ffn_block_fwd(h1, tgt, wffn, w_up_g, fcw, fcb, w_down, wfin):
    T = h1.shape[0]
    nt = T // TR
    nsl = w_up_g.shape[0]
    wsl = w_up_g.shape[2]

    def body(h_ref, t_ref, wn_ref, wup_ref, cw_ref, cb_ref, wdn_ref, wf_ref,
             u2_ref, upb_ref, gg_ref, dh_ref, loss_ref, dwf_ref, up_scr, carry_ref):
        i = pl.program_id(0)

        @pl.when(i == 0)
        def _():
            carry_ref[...] = jnp.zeros_like(carry_ref)
            loss_ref[...] = jnp.zeros_like(loss_ref)
            dwf_ref[...] = jnp.zeros_like(dwf_ref)

        x = h_ref[...]
        r2 = lax.rsqrt(jnp.mean(x * x, axis=-1, keepdims=True) + EPS)
        u2 = (x * r2 * wn_ref[...]).astype(BF16)
        u2_ref[...] = u2
        for s in range(nsl):
            up_s = jnp.dot(u2, wup_ref[s], preferred_element_type=F32)
            up_scr[:, s * wsl:(s + 1) * wsl] = up_s
            upb_ref[:, s * wsl:(s + 1) * wsl] = up_s.astype(BF16)
        w0, w1, w2 = cw_ref[0:1, :], cw_ref[1:2, :], cw_ref[2:3, :]

        def chunk(c, carry):
            rows = pl.ds(pl.multiple_of(c * CH, CH), CH)
            a_pre = up_scr[rows, 0:DFF]
            val = up_scr[rows, DFF:2 * DFF]
            prev8 = carry_ref[...]
            a = w0 * _shift_down(a_pre, prev8, 2) + w1 * _shift_down(a_pre, prev8, 1) + w2 * a_pre + cb_ref[...]
            carry_ref[...] = a_pre[CH - 8:CH, :]
            gg_ref[rows, :] = (a * _sigmoid(a) * val).astype(BF16)
            return carry

        lax.fori_loop(0, TR // CH, chunk, 0)
        h2 = x + jnp.dot(gg_ref[...], wdn_ref[...], preferred_element_type=F32)
        r3 = lax.rsqrt(jnp.mean(h2 * h2, axis=-1, keepdims=True) + EPS)
        n3 = h2 * r3
        wf = wf_ref[...]
        diff = jnp.where(i > 0, n3 * wf - t_ref[...], 0.0)
        loss_ref[...] += jnp.sum(diff * diff, axis=0, keepdims=True) * (0.5 / D)
        dy = diff * (1.0 / D)
        dwf_ref[...] += jnp.sum(dy * n3, axis=0, keepdims=True)
        dn = dy * wf
        dh_ref[...] = r3 * (dn - n3 * jnp.mean(dn * n3, axis=-1, keepdims=True))

    row = lambda w: pl.BlockSpec((TR, w), lambda i: (i, 0))
    vec = lambda w, r=1: pl.BlockSpec((r, w), lambda i: (0, 0))
    vm = pl.BlockSpec(memory_space=pltpu.VMEM)
    return pl.pallas_call(
        body, name="ffn_block_fwd", grid=(nt,),
        in_specs=[row(D), pl.BlockSpec((TR, D), lambda i: (jnp.maximum(i - 1, 0), 0)), vec(D), vm,
                  vec(DFF, 3), vec(DFF), vm, vec(D)],
        out_specs=[row(D), row(2 * DFF), row(DFF), row(D), vec(D), vec(D)],
        out_shape=[jax.ShapeDtypeStruct((T, D), BF16), jax.ShapeDtypeStruct((T, 2 * DFF), BF16),
                   jax.ShapeDtypeStruct((T, DFF), BF16), jax.ShapeDtypeStruct((T, D), F32),
                   jax.ShapeDtypeStruct((1, D), F32), jax.ShapeDtypeStruct((1, D), F32)],
        scratch_shapes=[pltpu.VMEM((TR, 2 * DFF), F32), pltpu.VMEM((8, DFF), F32)],
        compiler_params=_cparams(("arbitrary",)),
    )(h1, tgt, wffn, w_up_g, fcw, fcb, w_down, wfin)


def _ffn_block_bwd(dh2, upb, h1, wffn, w_up_g, fcw, fcb, w_down):
    T = h1.shape[0]
    nt = T // TR
    ncht = TR // CH
    nsl = w_up_g.shape[0]
    wsl = w_up_g.shape[2]
    tb16 = TR // 16

    def body(dh2_ref, up_ref, pa_ref, h_ref, wn_ref, wup_ref, cw_ref, cb_ref, wdn_ref,
             dup_ref, dh1_ref, dfw_ref, dfb_ref, dwn_ref, dgg_scr, carry_ref, acc_w, acc_b, acc_n):
        s = pl.program_id(0)
        tile = nt - 1 - s

        @pl.when(s == 0)
        def _():
            carry_ref[...] = jnp.zeros_like(carry_ref)
            acc_w[...] = jnp.zeros_like(acc_w)
            acc_b[...] = jnp.zeros_like(acc_b)
            acc_n[...] = jnp.zeros_like(acc_n)

        dh2 = dh2_ref[...]
        dgg_scr[...] = _dot_nt(dh2, wdn_ref[...])
        w0, w1, w2 = cw_ref[0:1, :], cw_ref[1:2, :], cw_ref[2:3, :]
        a_before_tile = jnp.where(tile > 0, pa_ref[...].astype(F32)[8:16, :], 0.0)

        def chunk(cc, carry):
            c = ncht - 1 - cc
            r0 = pl.multiple_of(c * CH, CH)
            rows = pl.ds(r0, CH)
            a_pre = up_ref[rows, 0:DFF].astype(F32)
            val = up_ref[rows, DFF:2 * DFF].astype(F32)
            rprev = pl.multiple_of(jnp.maximum(r0 - 16, 0), 16)
            prev_in = up_ref[pl.ds(rprev, 16), 0:DFF].astype(F32)[8:16, :]
            prev8 = jnp.where(c > 0, prev_in, a_before_tile)
            a_m1 = _shift_down(a_pre, prev8, 1)
            a_m2 = _shift_down(a_pre, prev8, 2)
            a = w0 * a_m2 + w1 * a_m1 + w2 * a_pre + cb_ref[...]
            sig = _sigmoid(a)
            dgg_v = dgg_scr[rows, :]
            da = dgg_v * val * (sig * (1.0 + a * (1.0 - sig)))
            dup_ref[rows, DFF:2 * DFF] = (dgg_v * (a * sig)).astype(BF16)
            next8 = carry_ref[...]
            dup_ref[rows, 0:DFF] = (w2 * da + w1 * _shift_up(da, next8, 1) + w0 * _shift_up(da, next8, 2)).astype(BF16)
            carry_ref[...] = da[0:8, :]
            acc_w[0:1, :] += jnp.sum(da * a_m2, axis=0, keepdims=True)
            acc_w[1:2, :] += jnp.sum(da * a_m1, axis=0, keepdims=True)
            acc_w[2:3, :] += jnp.sum(da * a_pre, axis=0, keepdims=True)
            acc_b[...] += jnp.sum(da, axis=0, keepdims=True)
            return carry

        lax.fori_loop(0, ncht, chunk, 0)
        du2 = _dot_nt(dup_ref[:, 0:wsl], wup_ref[0])
        for sl in range(1, nsl):
            du2 = du2 + _dot_nt(dup_ref[:, sl * wsl:(sl + 1) * wsl], wup_ref[sl])
        x = h_ref[...]
        r2 = lax.rsqrt(jnp.mean(x * x, axis=-1, keepdims=True) + EPS)
        n2 = x * r2
        dn = du2 * wn_ref[...]
        dh1_ref[...] = dh2 + r2 * (dn - n2 * jnp.mean(dn * n2, axis=-1, keepdims=True))
        acc_n[...] += jnp.sum(du2 * n2, axis=0, keepdims=True)

        @pl.when(s == nt - 1)
        def _():
            dfw_ref[...] = acc_w[0:3, :]
            dfb_ref[...] = acc_b[...]
            dwn_ref[...] = acc_n[...]

    rev = lambda w: pl.BlockSpec((TR, w), lambda s: (nt - 1 - s, 0))
    vec = lambda w, r=1: pl.BlockSpec((r, w), lambda s: (0, 0))
    vm = pl.BlockSpec(memory_space=pltpu.VMEM)
    return pl.pallas_call(
        body, name="ffn_block_bwd", grid=(nt,),
        in_specs=[rev(D), rev(2 * DFF),
                  pl.BlockSpec((16, DFF), lambda s: (jnp.maximum((nt - 1 - s) * tb16 - 1, 0), 0)),
                  rev(D), vec(D), vm, vec(DFF, 3), vec(DFF), vm],
        out_specs=[rev(2 * DFF), rev(D), vec(DFF, 3), vec(DFF), vec(D)],
        out_shape=[jax.ShapeDtypeStruct((T, 2 * DFF), BF16), jax.ShapeDtypeStruct((T, D), F32),
                   jax.ShapeDtypeStruct((3, DFF), F32), jax.ShapeDtypeStruct((1, DFF), F32),
                   jax.ShapeDtypeStruct((1, D), F32)],
        scratch_shapes=[pltpu.VMEM((TR, DFF), F32), pltpu.VMEM((8, DFF), F32), pltpu.VMEM((8, DFF), F32),
                        pltpu.VMEM((1, DFF), F32), pltpu.VMEM((1, D), F32)],
        compiler_params=_cparams(("arbitrary",)),
    )(dh2, upb, upb, h1, wffn, w_up_g, fcw, fcb, w_down)


def _place():
    x, y, c = lax.axis_index("x"), lax.axis_index("y"), lax.axis_index("c")
    return x, y, c


_CHIP_FLIPS = ((1, 0), (0, 1), (1, 1))


def _flip(v, bit):
    return 1 - v if bit else v


def _into_slot(w, j_idx, *, rb, dtype, name):
    r, cdim = w.shape

    def body(j_ref, w_ref, out_ref):
        del j_ref
        out_ref[...] = w_ref[...].astype(dtype)

    grid_spec = pltpu.PrefetchScalarGridSpec(
        num_scalar_prefetch=1, grid=(r // rb,),
        in_specs=[pl.BlockSpec((rb, cdim), lambda i, j_ref: (i, 0))],
        out_specs=pl.BlockSpec((None, rb, cdim), lambda i, j_ref: (j_ref[0], i, 0)))
    return pl.pallas_call(
        body, name=name, grid_spec=grid_spec, out_shape=jax.ShapeDtypeStruct((NSHARD, r, cdim), dtype),
        compiler_params=_cparams(("parallel",)),
    )(j_idx, w)


def _allgather_weights(slotted):
    ncopy = len(slotted)
    nw = ncopy - 1

    def body(*refs):
        outs = refs[ncopy:2 * ncopy]
        send_sems, recv_sems = refs[2 * ncopy:]
        x, y, c = _place()
        j = 2 * x + y
        sibling = (x, y, 1 - c)
        sends = []
        for w in range(ncopy):
            r = outs[w].shape[1]
            half = r // 2 if w < nw else r
            off = c * half if w < nw else 0
            mine = outs[w].at[j, pl.ds(off, half), :]
            for kk, (fx, fy) in enumerate(_CHIP_FLIPS):
                cp = pltpu.make_async_remote_copy(
                    src_ref=mine, dst_ref=mine, send_sem=send_sems.at[w, kk], recv_sem=recv_sems.at[w, kk],
                    device_id=(_flip(x, fx), _flip(y, fy), c), device_id_type=MESH)
                cp.start()
                sends.append(cp)
        for w in range(nw):
            half = outs[w].shape[1] // 2
            off = c * half
            for kk, (fx, fy) in enumerate(_CHIP_FLIPS):
                jk = 2 * _flip(x, fx) + _flip(y, fy)
                piece = outs[w].at[jk, pl.ds(off, half), :]
                pltpu.make_async_remote_copy(
                    src_ref=piece, dst_ref=piece, send_sem=send_sems.at[w, kk], recv_sem=recv_sems.at[w, kk],
                    device_id=sibling, device_id_type=MESH).wait_recv()
                fw = pltpu.make_async_remote_copy(
                    src_ref=piece, dst_ref=piece, send_sem=send_sems.at[w, 3 + kk], recv_sem=recv_sems.at[w, 3 + kk],
                    device_id=sibling, device_id_type=MESH)
                fw.start()
                sends.append(fw)
        for kk, (fx, fy) in enumerate(_CHIP_FLIPS):
            jk = 2 * _flip(x, fx) + _flip(y, fy)
            piece = outs[nw].at[jk]
            pltpu.make_async_remote_copy(
                src_ref=piece, dst_ref=piece, send_sem=send_sems.at[nw, kk], recv_sem=recv_sems.at[nw, kk],
                device_id=sibling, device_id_type=MESH).wait_recv()
        for w in range(nw):
            half = outs[w].shape[1] // 2
            off = (1 - c) * half
            for kk, (fx, fy) in enumerate(_CHIP_FLIPS):
                jk = 2 * _flip(x, fx) + _flip(y, fy)
                piece = outs[w].at[jk, pl.ds(off, half), :]
                pltpu.make_async_remote_copy(
                    src_ref=piece, dst_ref=piece, send_sem=send_sems.at[w, 3 + kk], recv_sem=recv_sems.at[w, 3 + kk],
                    device_id=sibling, device_id_type=MESH).wait_recv()
        for cp in sends:
            cp.wait_send()

    any_spec = pl.BlockSpec(memory_space=pl.ANY)
    return pl.pallas_call(
        body, name="allgather_weights",
        in_specs=[any_spec] * ncopy, out_specs=[any_spec] * ncopy,
        out_shape=[jax.ShapeDtypeStruct(a.shape, a.dtype) for a in slotted],
        input_output_aliases={i: i for i in range(ncopy)},
        scratch_shapes=[pltpu.SemaphoreType.DMA((ncopy, 6)), pltpu.SemaphoreType.DMA((ncopy, 6))],
    )(*slotted)


def _pair_exchange(grads):
    nw = len(grads)

    def body(*refs):
        ins, outs = refs[:nw], refs[nw:2 * nw]
        send_sems, recv_sems = refs[2 * nw:]
        x, y, c = _place()
        sibling = (x, y, 1 - c)
        cps = []
        for w in range(nw):
            half = ins[w].shape[1] // 2
            cp = pltpu.make_async_remote_copy(
                src_ref=ins[w].at[:, pl.ds((1 - c) * half, half), :], dst_ref=outs[w],
                send_sem=send_sems.at[w], recv_sem=recv_sems.at[w], device_id=sibling, device_id_type=MESH)
            cp.start()
            cps.append(cp)
        for cp in cps:
            cp.wait()

    any_spec = pl.BlockSpec(memory_space=pl.ANY)
    return pl.pallas_call(
        body, name="grad_pair_exchange",
        in_specs=[any_spec] * nw, out_specs=[any_spec] * nw,
        out_shape=[jax.ShapeDtypeStruct((g.shape[0], g.shape[1] // 2, g.shape[2]), g.dtype) for g in grads],
        scratch_shapes=[pltpu.SemaphoreType.DMA((nw,)), pltpu.SemaphoreType.DMA((nw,))],
    )(*grads)


def _pair_add(g, other, c_idx, *, rb, name):
    S, r, cdim = g.shape
    half = r // 2
    nb = half // rb

    def body(c_ref, g_ref, o_ref, out_ref):
        del c_ref
        out_ref[...] = (g_ref[...] + o_ref[...]).astype(BF16)

    grid_spec = pltpu.PrefetchScalarGridSpec(
        num_scalar_prefetch=1, grid=(S, nb),
        in_specs=[pl.BlockSpec((None, rb, cdim), lambda s, i, c_ref: (s, c_ref[0] * nb + i, 0)),
                  pl.BlockSpec((None, rb, cdim), lambda s, i, c_ref: (s, i, 0))],
        out_specs=pl.BlockSpec((None, rb, cdim), lambda s, i, c_ref: (s, i, 0)))
    return pl.pallas_call(
        body, name=name, grid_spec=grid_spec, out_shape=jax.ShapeDtypeStruct((S, half, cdim), BF16),
        compiler_params=_cparams(("parallel", "parallel")),
    )(c_idx, g, other)


def _chip_exchange(psums):
    nw = len(psums)

    def body(*refs):
        ins, outs = refs[:nw], refs[nw:2 * nw]
        send_sems, recv_sems = refs[2 * nw:]
        x, y, c = _place()
        cps = []
        for w in range(nw):
            for kk, (fx, fy) in enumerate(_CHIP_FLIPS):
                px, py = _flip(x, fx), _flip(y, fy)
                cp = pltpu.make_async_remote_copy(
                    src_ref=ins[w].at[2 * px + py], dst_ref=outs[w].at[kk],
                    send_sem=send_sems.at[w, kk], recv_sem=recv_sems.at[w, kk],
                    device_id=(px, py, c), device_id_type=MESH)
                cp.start()
                cps.append(cp)
        for cp in cps:
            cp.wait()

    any_spec = pl.BlockSpec(memory_space=pl.ANY)
    return pl.pallas_call(
        body, name="grad_chip_exchange",
        in_specs=[any_spec] * nw, out_specs=[any_spec] * nw,
        out_shape=[jax.ShapeDtypeStruct((3,) + p.shape[1:], p.dtype) for p in psums],
        scratch_shapes=[pltpu.SemaphoreType.DMA((nw, 3)), pltpu.SemaphoreType.DMA((nw, 3))],
    )(*psums)


def _chip_sum(psum, parts, cj_idx, *, rb, name):
    S, half, cdim = psum.shape
    nb = half // rb

    def body(cj_ref, own_ref, p_ref, out_ref):
        del cj_ref
        f = lambda v: v.astype(F32)
        out_ref[...] = ((f(own_ref[...]) + f(p_ref[0])) + f(p_ref[1])) + f(p_ref[2])

    grid_spec = pltpu.PrefetchScalarGridSpec(
        num_scalar_prefetch=1, grid=(nb,),
        in_specs=[pl.BlockSpec((None, rb, cdim), lambda i, cj: (cj[1], i, 0)),
                  pl.BlockSpec((3, rb, cdim), lambda i, cj: (0, i, 0))],
        out_specs=pl.BlockSpec((rb, cdim), lambda i, cj: (cj[0] * nb + i, 0)))
    return pl.pallas_call(
        body, name=name, grid_spec=grid_spec, out_shape=jax.ShapeDtypeStruct((2 * half, cdim), F32),
        compiler_params=_cparams(("parallel",)),
    )(cj_idx, psum, parts)


def _pair_gather(grads):
    nw = len(grads)

    def body(*refs):
        outs = refs[nw:2 * nw]
        send_sems, recv_sems = refs[2 * nw:]
        x, y, c = _place()
        sibling = (x, y, 1 - c)
        cps = []
        for w in range(nw):
            half = outs[w].shape[0] // 2
            mine = outs[w].at[pl.ds(c * half, half), :]
            cp = pltpu.make_async_remote_copy(
                src_ref=mine, dst_ref=mine, send_sem=send_sems.at[w], recv_sem=recv_sems.at[w],
                device_id=sibling, device_id_type=MESH)
            cp.start()
            cps.append(cp)
        for w in range(nw):
            half = outs[w].shape[0] // 2
            theirs = outs[w].at[pl.ds((1 - c) * half, half), :]
            pltpu.make_async_remote_copy(
                src_ref=theirs, dst_ref=theirs, send_sem=send_sems.at[w], recv_sem=recv_sems.at[w],
                device_id=sibling, device_id_type=MESH).wait_recv()
        for cp in cps:
            cp.wait_send()

    any_spec = pl.BlockSpec(memory_space=pl.ANY)
    return pl.pallas_call(
        body, name="grad_pair_gather",
        in_specs=[any_spec] * nw, out_specs=[any_spec] * nw,
        out_shape=[jax.ShapeDtypeStruct(g.shape, g.dtype) for g in grads],
        input_output_aliases={i: i for i in range(nw)},
        scratch_shapes=[pltpu.SemaphoreType.DMA((nw,)), pltpu.SemaphoreType.DMA((nw,))],
    )(*grads)


def _allreduce_small(pieces):
    n = len(pieces)
    width = max(p.shape[1] for p in pieces)
    starts, at = [], 0
    for p in pieces:
        if p.shape[0] > 1:
            at = -(-at // 8) * 8
        starts.append(at)
        at += p.shape[0]
    rows = -(-at // 8) * 8
    flips = [(fx, fy, fc) for fx in (0, 1) for fy in (0, 1) for fc in (0, 1)][1:]

    def body(*refs):
        ins, outs = refs[:n], refs[n:2 * n]
        mine_ref, slots_ref, send_sems, recv_sems = refs[2 * n:]
        x, y, c = _place()
        me = 4 * x + 2 * y + c
        mine_ref[...] = jnp.zeros_like(mine_ref)
        for p_ref, st in zip(ins, starts):
            r, wd = p_ref.shape
            mine_ref[st:st + r, 0:wd] = p_ref[...]
        slots_ref[me] = mine_ref[...]
        cps = []
        for kk, (fx, fy, fc) in enumerate(flips):
            cp = pltpu.make_async_remote_copy(
                src_ref=mine_ref, dst_ref=slots_ref.at[me], send_sem=send_sems.at[kk], recv_sem=recv_sems.at[kk],
                device_id=(_flip(x, fx), _flip(y, fy), _flip(c, fc)), device_id_type=MESH)
            cp.start()
            cps.append(cp)
        for cp in cps:
            cp.wait()
        tot = slots_ref[0]
        for d in range(1, 8):
            tot = tot + slots_ref[d]
        mine_ref[...] = tot
        for idx, (o_ref, p_ref, st) in enumerate(zip(outs, ins, starts)):
            r, wd = p_ref.shape
            val = mine_ref[st:st + r, 0:wd]
            o_ref[...] = jnp.sum(val, keepdims=True) if idx == n - 1 else val

    vm = pl.BlockSpec(memory_space=pltpu.VMEM)
    out_shape = [jax.ShapeDtypeStruct(p.shape, F32) for p in pieces[:-1]] + [jax.ShapeDtypeStruct((1, 1), F32)]
    return pl.pallas_call(
        body, name="allreduce_small", in_specs=[vm] * n, out_specs=[vm] * n, out_shape=out_shape,
        scratch_shapes=[pltpu.VMEM((rows, width), F32), pltpu.VMEM((8, rows, width), F32),
                        pltpu.SemaphoreType.DMA((7,)), pltpu.SemaphoreType.DMA((7,))],
    )(*pieces)


def _adamw(w, g, m, v, *, rb, name):
    r, cdim = w.shape

    def body(w_ref, g_ref, m_ref, v_ref, d_ref, nm_ref, nv_ref):
        d_ref[...], nm_ref[...], nv_ref[...] = _adamw_update(w_ref[...], g_ref[...], m_ref[...], v_ref[...])

    spec = pl.BlockSpec((rb, cdim), lambda i: (i, 0))
    shp = jax.ShapeDtypeStruct((r, cdim), F32)
    return pl.pallas_call(
        body, name=name, grid=(r // rb,), in_specs=[spec] * 4, out_specs=[spec] * 3, out_shape=[shp] * 3,
        compiler_params=_cparams(("parallel",)),
    )(w, g, m, v)


def _adamw_update(w, g, m, v):
    nm = ADAM_B1 * m + (1.0 - ADAM_B1) * g
    nv = ADAM_B2 * v + (1.0 - ADAM_B2) * (g * g)
    m_hat = nm / (1.0 - ADAM_B1 ** ADAM_STEP)
    v_hat = nv / (1.0 - ADAM_B2 ** ADAM_STEP)
    return -ADAM_LR * (m_hat / (jnp.sqrt(v_hat) + ADAM_EPS) + ADAM_WD * w), nm, nv


def _adamw_small(params):
    n = len(params)

    def body(*refs):
        ins, outs = refs[:4 * n], refs[4 * n:]
        for p in range(n):
            w_ref, g_ref, m_ref, v_ref = ins[4 * p:4 * p + 4]
            d, nm, nv = _adamw_update(w_ref[...], g_ref[...], m_ref[...], v_ref[...])
            outs[3 * p][...] = d
            outs[3 * p + 1][...] = nm
            outs[3 * p + 2][...] = nv

    vm = pl.BlockSpec(memory_space=pltpu.VMEM)
    flat = [a for p in params for a in p]
    out_shape = [jax.ShapeDtypeStruct(p[0].shape, F32) for p in params for _ in range(3)]
    res = pl.pallas_call(
        body, name="adamw_small", in_specs=[vm] * (4 * n), out_specs=[vm] * (3 * n), out_shape=out_shape,
    )(*flat)
    return [tuple(res[3 * p:3 * p + 3]) for p in range(n)]


def _local_step(x, tgt, meta_full, lb_param, attn_norm_w, w_in_g, hgrn_norm_w, conv_w_full, w_out_full,
                ffn_norm_w, w_up_g, fcw_full, ffn_conv_b, w_down_full, final_norm_w):
    seq = x.shape[0]
    T = TR + seq
    bm = 768 if T % 768 == 0 else TR
    h0 = jnp.concatenate([jnp.zeros((PAD, D), F32), meta_full, x], axis=0)
    whn_t = jnp.tile(hgrn_norm_w, (1, NH))

    u, projb, o, sst, m, h1 = _mix_block_fwd(h0, lb_param, attn_norm_w, whn_t, conv_w_full, w_in_g, w_out_full)
    u2, upb, gg, dh2, loss_vec, dwfin = _ffn_block_fwd(
        h1, tgt, ffn_norm_w, w_up_g, fcw_full, ffn_conv_b, w_down_full, final_norm_w.reshape(1, D))

    dup, dh1, dfw, dfb, dwffn = _ffn_block_bwd(dh2, upb, h1, ffn_norm_w, w_up_g, fcw_full, ffn_conv_b, w_down_full)
    dproj, dlb, dwhn, dcw = _mix_block_bwd(dh1, projb, o, sst, lb_param, whn_t, conv_w_full, w_out_full)
    du = _matmul(dproj, w_in_g, mode="nt", bm=bm, bn=D, bk=1152, out_dtype=F32, name="du_mm")
    grad_x, dmeta, dwattn = _norm_bwd_input(h0, du, dh1, attn_norm_w)

    kb = bm
    g_in = _matmul(u, dproj, mode="tn", bm=D, bn=1152, bk=kb, out_dtype=F32, name="dw_in_mm",
                   out_shards=(NSHARD, 9 * D // NSHARD))
    g_out = _matmul(m, dh1, mode="tn", bm=D, bn=D, bk=kb, out_dtype=F32, name="dw_out_mm")
    g_up = _matmul(u2, dup, mode="tn", bm=D, bn=1408, bk=kb, out_dtype=F32, name="dw_up_mm",
                   out_shards=(NSHARD, 2 * DFF // NSHARD))
    g_down = _matmul(gg, dh2, mode="tn", bm=1408, bn=D, bk=kb, out_dtype=F32, name="dw_down_mm")

    small = dict(dlb=dlb, dwattn=dwattn, dwhn=dwhn, dwffn=dwffn, dfb=dfb, dwfin=dwfin,
                 dcw=dcw, dfw=dfw, dmeta=dmeta, loss=loss_vec)
    big = (g_in, g_out.reshape(NSHARD, D // NSHARD, D), g_up, g_down.reshape(NSHARD, DFF // NSHARD, D))
    return grad_x, small, big


_SMALL_ORDER = ("dmeta", "dcw", "dfw", "dlb", "dwattn", "dwhn", "dwffn", "dfb", "dwfin", "loss")


def kernel(x, meta_tokens, lb_param, attn_norm_w, w_in, hgrn_norm_w, conv_w, w_out, ffn_norm_w, w_up, ffn_conv_w, ffn_conv_b, w_down, final_norm_w, loss_target, m_meta_tokens, m_lb_param, m_attn_norm_w, m_w_in, m_hgrn_norm_w, m_conv_w, m_w_out, m_ffn_norm_w, m_w_up, m_ffn_conv_w, m_ffn_conv_b, m_w_down, m_final_norm_w, v_meta_tokens, v_lb_param, v_attn_norm_w, v_w_in, v_hgrn_norm_w, v_conv_w, v_w_out, v_ffn_norm_w, v_w_up, v_ffn_conv_w, v_ffn_conv_b, v_w_down, v_final_norm_w):
    xi, yi, ci = _place()
    j = 2 * xi + yi
    c_idx = jnp.reshape(ci, (1,)).astype(jnp.int32)

    j_idx = jnp.reshape(j, (1,)).astype(jnp.int32)
    ds_, fs_ = D // NSHARD, DFF // NSHARD
    widen = lambda a: jnp.pad(a, ((0, 0), (0, 768 - a.shape[1])))
    rows_small = jnp.concatenate([widen(meta_tokens), widen(conv_w[0]), widen(ffn_conv_w[0]),
                                  jnp.zeros((2, 768), F32)], axis=0)
    slotted = [_into_slot(w[0], j_idx, rb=rb, dtype=BF16, name=f"slot_{n}")
               for w, rb, n in ((w_in, 256, "w_in"), (w_out, 128, "w_out"), (w_up, 256, "w_up"), (w_down, 176, "w_down"))]
    slotted.append(_into_slot(rows_small, j_idx, rb=rows_small.shape[0], dtype=F32, name="slot_small"))
    w_in_g, w_out_g, w_up_g, w_down_g, small_g = _allgather_weights(slotted)
    unshard = lambda a: jnp.transpose(a, (1, 0, 2)).reshape(a.shape[1], -1)
    meta_full = unshard(small_g[:, 0:NMETA, 0:ds_])
    conv_w_full = unshard(small_g[:, NMETA:NMETA + 3, 0:ds_])
    fcw_full = unshard(small_g[:, NMETA + 3:NMETA + 6, 0:fs_])

    grad_x, small, big = _local_step(
        x[0], loss_target[0], meta_full, lb_param, attn_norm_w, w_in_g, hgrn_norm_w, conv_w_full,
        w_out_g.reshape(D, D), ffn_norm_w, w_up_g, fcw_full, ffn_conv_b, w_down_g.reshape(DFF, D), final_norm_w)

    names = _SMALL_ORDER
    vals = dict(zip(names, _allreduce_small([small[n] for n in names])))
    loss = vals["loss"].reshape(())
    g_small = {
        "meta_tokens": lax.dynamic_slice_in_dim(vals["dmeta"], j * (D // NSHARD), D // NSHARD, axis=1),
        "lb_param": jnp.concatenate([vals["dlb"], -vals["dlb"]], axis=0),
        "attn_norm_w": vals["dwattn"],
        "hgrn_norm_w": vals["dwhn"],
        "conv_w": lax.dynamic_slice_in_dim(vals["dcw"], j * (D // NSHARD), D // NSHARD, axis=1)[None],
        "ffn_norm_w": vals["dwffn"],
        "ffn_conv_w": lax.dynamic_slice_in_dim(vals["dfw"], j * (DFF // NSHARD), DFF // NSHARD, axis=1)[None],
        "ffn_conv_b": vals["dfb"],
        "final_norm_w": vals["dwfin"].reshape(D),
    }

    others = _pair_exchange(big)
    rbs = (256, 128, 256, 176)
    psums = [_pair_add(g, o_, c_idx, rb=rb, name=f"pair_add_{n}") for g, o_, rb, n in zip(big, others, rbs, "0123")]
    parts = _chip_exchange(psums)
    cj_idx = jnp.stack([ci, j]).astype(jnp.int32)
    halves = [_chip_sum(ps, p, cj_idx, rb=rb, name=f"chip_sum_{n}") for ps, p, rb, n in zip(psums, parts, rbs, "0123")]
    g_big = _pair_gather(halves)

    weights = {"meta_tokens": meta_tokens, "lb_param": lb_param, "attn_norm_w": attn_norm_w, "w_in": w_in,
               "hgrn_norm_w": hgrn_norm_w, "conv_w": conv_w, "w_out": w_out, "ffn_norm_w": ffn_norm_w,
               "w_up": w_up, "ffn_conv_w": ffn_conv_w, "ffn_conv_b": ffn_conv_b, "w_down": w_down,
               "final_norm_w": final_norm_w}
    ms = {"meta_tokens": m_meta_tokens, "lb_param": m_lb_param, "attn_norm_w": m_attn_norm_w, "w_in": m_w_in,
          "hgrn_norm_w": m_hgrn_norm_w, "conv_w": m_conv_w, "w_out": m_w_out, "ffn_norm_w": m_ffn_norm_w,
          "w_up": m_w_up, "ffn_conv_w": m_ffn_conv_w, "ffn_conv_b": m_ffn_conv_b, "w_down": m_w_down,
          "final_norm_w": m_final_norm_w}
    vs = {"meta_tokens": v_meta_tokens, "lb_param": v_lb_param, "attn_norm_w": v_attn_norm_w, "w_in": v_w_in,
          "hgrn_norm_w": v_hgrn_norm_w, "conv_w": v_conv_w, "w_out": v_w_out, "ffn_norm_w": v_ffn_norm_w,
          "w_up": v_w_up, "ffn_conv_w": v_ffn_conv_w, "ffn_conv_b": v_ffn_conv_b, "w_down": v_w_down,
          "final_norm_w": v_final_norm_w}
    order = list(weights)
    grads, deltas, new_m, new_v = {}, {}, {}, {}

    for name, g, rb in zip(("w_in", "w_out", "w_up", "w_down"), g_big, (256, 128, 256, 176)):
        shp = weights[name].shape
        w2, m2, v2 = (a.reshape(shp[1], shp[2]) for a in (weights[name], ms[name], vs[name]))
        d_, nm_, nv_ = _adamw(w2, g, m2, v2, rb=rb, name=f"adamw_{name}")
        grads[name], deltas[name], new_m[name], new_v[name] = (a.reshape(shp) for a in (g, d_, nm_, nv_))

    small_names = [n for n in order if n not in grads]
    as2d = lambda a: a.reshape(-1, a.shape[-1])
    res = _adamw_small([tuple(as2d(a) for a in (weights[n], g_small[n], ms[n], vs[n])) for n in small_names])
    for n, (d_, nm_, nv_) in zip(small_names, res):
        shp = weights[n].shape
        grads[n], deltas[n], new_m[n], new_v[n] = (a.reshape(shp) for a in (g_small[n], d_, nm_, nv_))

    return (loss, grad_x[None], *[grads[n] for n in order], *[deltas[n] for n in order],
            *[new_m[n] for n in order], *[new_v[n] for n in order])
```

```python
import functools

import jax
import jax.numpy as jnp
from jax import lax
from jax.experimental import pallas as pl
from jax.experimental.pallas import tpu as pltpu

F32 = jnp.float32
BF16 = jnp.bfloat16
MESH = pl.DeviceIdType.MESH

D = 1024
NH = 8
HD = 128
DFF = 2816
NMETA = 16
EPS = 1e-6
TR = 256
PAD = TR - NMETA
CH = 64
NSHARD = 4
VMEM_LIMIT = 56 * 1024 * 1024

ADAM_LR = 0.001
ADAM_B1 = 0.9
ADAM_B2 = 0.999
ADAM_EPS = 1e-08
ADAM_WD = 0.01
ADAM_STEP = 10


def _cparams(semantics=None, **kw):
    return pltpu.CompilerParams(dimension_semantics=semantics, vmem_limit_bytes=VMEM_LIMIT, **kw)


def _sigmoid(x):
    return 1.0 / (1.0 + jnp.exp(-x))


def _matmul(a, b, *, mode, bm, bn, bk, out_dtype, name, residual=None, out_shards=None):
    if mode == "tn":
        K, M = a.shape
    else:
        M, K = a.shape
    b3 = b.ndim == 3
    if b3:
        S, R, Cs = b.shape
        bcols = S * Cs
        brows = R
    else:
        brows, bcols = b.shape
    if mode == "nt":
        N = brows
        assert bcols == K
    else:
        N = bcols
        assert brows == K
    assert M % bm == 0 and N % bn == 0 and K % bk == 0, (name, M, N, K, bm, bn, bk)
    nm, nn, nk = M // bm, N // bn, K // bk

    if mode == "tn":
        a_spec = pl.BlockSpec((bk, bm), lambda i, j, k: (k, i))
    else:
        a_spec = pl.BlockSpec((bm, bk), lambda i, j, k: (i, k))
    if mode == "nt":
        if b3:
            assert bn == R and Cs % bk == 0
            per = Cs // bk
            b_spec = pl.BlockSpec((None, bn, bk), lambda i, j, k: (k // per, 0, k % per))
        else:
            b_spec = pl.BlockSpec((bn, bk), lambda i, j, k: (j, k))
    else:
        if b3:
            assert bk == R and Cs % bn == 0
            per = Cs // bn
            b_spec = pl.BlockSpec((None, bk, bn), lambda i, j, k: (j // per, 0, j % per))
        else:
            b_spec = pl.BlockSpec((bk, bn), lambda i, j, k: (k, j))
    in_specs = [a_spec, b_spec]
    args = [a, b]
    if residual is not None:
        in_specs.append(pl.BlockSpec((bm, bn), lambda i, j, k: (i, j)))
        args.append(residual)
    if out_shards is not None:
        So, Co = out_shards
        assert So * Co == N and Co % bn == 0 and bm == M
        pero = Co // bn
        out_shape = jax.ShapeDtypeStruct((So, M, Co), out_dtype)
        out_spec = pl.BlockSpec((None, bm, bn), lambda i, j, k: (j // pero, 0, j % pero))
    else:
        out_shape = jax.ShapeDtypeStruct((M, N), out_dtype)
        out_spec = pl.BlockSpec((bm, bn), lambda i, j, k: (i, j))

    def body(*refs):
        if residual is not None:
            a_ref, b_ref, r_ref, o_ref = refs[:4]
            scr = refs[4:]
        else:
            a_ref, b_ref, o_ref = refs[:3]
            r_ref = None
            scr = refs[3:]
        av = a_ref[...].astype(BF16)
        bv = b_ref[...].astype(BF16)
        if mode == "nn":
            p = jnp.dot(av, bv, preferred_element_type=F32)
        elif mode == "nt":
            p = lax.dot_general(av, bv, (((1,), (1,)), ((), ())), preferred_element_type=F32)
        else:
            p = lax.dot_general(av, bv, (((0,), (0,)), ((), ())), preferred_element_type=F32)

        def finish(acc):
            if r_ref is not None:
                acc = acc + r_ref[...]
            o_ref[...] = acc.astype(out_dtype)

        if nk == 1:
            finish(p)
        else:
            acc_ref = scr[0]
            k = pl.program_id(2)

            @pl.when(k == 0)
            def _():
                acc_ref[...] = p

            @pl.when(k > 0)
            def _():
                acc_ref[...] += p

            @pl.when(k == nk - 1)
            def _():
                finish(acc_ref[...])

    scratch = [] if nk == 1 else [pltpu.VMEM((bm, bn), F32)]
    return pl.pallas_call(
        body, name=name, grid=(nm, nn, nk), in_specs=in_specs, out_specs=out_spec, out_shape=out_shape,
        scratch_shapes=scratch, compiler_params=_cparams(("parallel", "parallel", "arbitrary")),
    )(*args)


def _norm_fwd(h, w, *, bm, name):
    T = h.shape[0]

    def body(h_ref, w_ref, o_ref):
        x = h_ref[...]
        r = lax.rsqrt(jnp.mean(x * x, axis=-1, keepdims=True) + EPS)
        o_ref[...] = (x * r * w_ref[...]).astype(BF16)

    return pl.pallas_call(
        body, name=name, grid=(T // bm,),
        in_specs=[pl.BlockSpec((bm, D), lambda i: (i, 0)), pl.BlockSpec((1, D), lambda i: (0, 0))],
        out_specs=pl.BlockSpec((bm, D), lambda i: (i, 0)),
        out_shape=jax.ShapeDtypeStruct((T, D), BF16), compiler_params=_cparams(("parallel",)),
    )(h, w)


def _norm_bwd(h, du, dres, w, *, bm, name):
    T = h.shape[0]

    def body(h_ref, du_ref, dres_ref, w_ref, dh_ref, dw_ref):
        i = pl.program_id(0)
        x = h_ref[...]
        r = lax.rsqrt(jnp.mean(x * x, axis=-1, keepdims=True) + EPS)
        n = x * r
        du_v = du_ref[...]
        dn = du_v * w_ref[...]
        dh_ref[...] = dres_ref[...] + r * (dn - n * jnp.mean(dn * n, axis=-1, keepdims=True))
        part = jnp.sum(du_v * n, axis=0, keepdims=True)

        @pl.when(i == 0)
        def _():
            dw_ref[...] = part

        @pl.when(i > 0)
        def _():
            dw_ref[...] += part

    row = pl.BlockSpec((bm, D), lambda i: (i, 0))
    vec = pl.BlockSpec((1, D), lambda i: (0, 0))
    return pl.pallas_call(
        body, name=name, grid=(T // bm,), in_specs=[row, row, row, vec], out_specs=[row, vec],
        out_shape=[jax.ShapeDtypeStruct((T, D), F32), jax.ShapeDtypeStruct((1, D), F32)],
        compiler_params=_cparams(("arbitrary",)),
    )(h, du, dres, w)


def _input_grad_block(dproj, h, dres, w, w_in_g):
    T = h.shape[0]
    nt = T // TR
    nsl = w_in_g.shape[0]
    wsl = w_in_g.shape[2]
    wcol = wsl // 2

    def body(dp_ref, h_ref, dres_ref, w_ref, win_ref, gx_ref, dmeta_ref, dw_ref):
        i = pl.program_id(0)
        x = h_ref[...]
        r = lax.rsqrt(jnp.mean(x * x, axis=-1, keepdims=True) + EPS)
        n = x * r
        du_v = None
        for s in range(nsl):
            for hf in range(2):
                part = _dot_nt(dp_ref[:, s * wsl + hf * wcol:s * wsl + (hf + 1) * wcol],
                               win_ref[s, :, hf * wcol:(hf + 1) * wcol])
                du_v = part if du_v is None else du_v + part
        dn = du_v * w_ref[...]
        dh = dres_ref[...] + r * (dn - n * jnp.mean(dn * n, axis=-1, keepdims=True))
        gx_ref[...] = dh
        part = jnp.sum(du_v * n, axis=0, keepdims=True)

        @pl.when(i == 0)
        def _():
            dmeta_ref[...] = dh[PAD:TR, :]
            dw_ref[...] = part

        @pl.when(i > 0)
        def _():
            dw_ref[...] += part

    row = pl.BlockSpec((TR, D), lambda i: (i, 0))
    vec = pl.BlockSpec((1, D), lambda i: (0, 0))
    return pl.pallas_call(
        body, name="input_grad_block", grid=(nt,),
        in_specs=[pl.BlockSpec((TR, 9 * D), lambda i: (i, 0)), row, row, vec, pl.BlockSpec(memory_space=pltpu.VMEM)],
        out_specs=[pl.BlockSpec((TR, D), lambda i: (jnp.maximum(i - 1, 0), 0)),
                   pl.BlockSpec((NMETA, D), lambda i: (0, 0)), vec],
        out_shape=[jax.ShapeDtypeStruct((T - TR, D), F32), jax.ShapeDtypeStruct((NMETA, D), F32),
                   jax.ShapeDtypeStruct((1, D), F32)],
        compiler_params=_cparams(("arbitrary",)),
    )(dproj, h, dres, w, w_in_g)


def _split3(x):
    hi = x.astype(BF16)
    r1 = x - hi.astype(F32)
    mid = r1.astype(BF16)
    lo = (r1 - mid.astype(F32)).astype(BF16)
    return hi, mid, lo


def _tri_matmul(tri_bf16, x):
    hi, mid, lo = _split3(x)
    out = jnp.dot(tri_bf16, lo, preferred_element_type=F32)
    out = out + jnp.dot(tri_bf16, mid, preferred_element_type=F32)
    return out + jnp.dot(tri_bf16, hi, preferred_element_type=F32)


def _shift_down(x, prev8, n):
    rows = x.shape[0]
    rid = lax.broadcasted_iota(jnp.int32, x.shape, 0)
    y = pltpu.roll(x, n, 0)
    for t in range(n):
        y = jnp.where(rid == t, prev8[8 - n + t:8 - n + t + 1, :], y)
    del rows
    return y


def _shift_up(x, next8, n):
    rows = x.shape[0]
    rid = lax.broadcasted_iota(jnp.int32, x.shape, 0)
    y = pltpu.roll(x, rows - n, 0)
    for t in range(n):
        y = jnp.where(rid == rows - n + t, next8[t:t + 1, :], y)
    return y


def _gates(f_raw, lb):
    sg = _sigmoid(f_raw)
    f = lb + (1.0 - lb) * sg
    return sg, f, jnp.log(f), 1.0 - f


def _lower_bound(lbp_ref):
    return _sigmoid(lbp_ref[0:1, :] - lbp_ref[1:2, :])


def _tri_masks():
    r = lax.broadcasted_iota(jnp.int32, (CH, CH), 0)
    c = lax.broadcasted_iota(jnp.int32, (CH, CH), 1)
    return r >= c, r <= c


def _ones_where(mask):
    return jnp.where(mask, 1.0, 0.0).astype(BF16)


def _dot(a, b):
    return jnp.dot(a.astype(BF16), b.astype(BF16), preferred_element_type=F32)


def _dot_nt(a, b):
    return lax.dot_general(a.astype(BF16), b.astype(BF16), (((1,), (1,)), ((), ())), preferred_element_type=F32)


def _dot_tn(a, b):
    return lax.dot_general(a.astype(BF16), b.astype(BF16), (((0,), (0,)), ((), ())), preferred_element_type=F32)


def _mix_block_fwd(h0, lb_param, wattn, whn, conv_w, w_in_g, w_out):
    T = h0.shape[0]
    nt = T // TR
    ncht = TR // CH
    nsl = w_in_g.shape[0]
    wsl = w_in_g.shape[2]
    wcol = wsl // 2

    def body(h_ref, lbp_ref, wa_ref, whn_ref, cw_ref, win_ref, wout_ref,
             ut_ref, projb_ref, o_ref, sst_ref, mt_ref, h1_ref, proj_ref, m_ref, st_ref, cxc_ref):
        i = pl.program_id(0)

        @pl.when(i == 0)
        def _():
            st_ref[...] = jnp.zeros_like(st_ref)
            cxc_ref[...] = jnp.zeros_like(cxc_ref)

        x = h_ref[...]
        r1 = lax.rsqrt(jnp.mean(x * x, axis=-1, keepdims=True) + EPS)
        u_f = x * r1 * wa_ref[...]
        u = u_f.astype(BF16)
        ut_ref[...] = u_f.T.astype(BF16)
        for s in range(nsl):
            for hf in range(2):
                cols = slice(s * wsl + hf * wcol, s * wsl + (hf + 1) * wcol)
                p = jnp.dot(u, win_ref[s, :, hf * wcol:(hf + 1) * wcol], preferred_element_type=F32)
                proj_ref[:, cols] = p
                projb_ref[:, cols] = p.astype(BF16)

        lb = _lower_bound(lbp_ref)
        lower, _ = _tri_masks()
        ltri = _ones_where(lower)
        whn_v = whn_ref[...]
        w0, w1, w2 = cw_ref[0:1, :], cw_ref[1:2, :], cw_ref[2:3, :]

        def chunk(c, carry):
            rows = pl.ds(pl.multiple_of(c * CH, CH), CH)
            q_raw = proj_ref[rows, 0:D]
            f_raw = proj_ref[rows, D:2 * D]
            v = proj_ref[rows, 2 * D:3 * D]
            q = q_raw * _sigmoid(q_raw)
            _, _, g, k = _gates(f_raw, lb)
            gam = _tri_matmul(ltri, g)
            gam_l = gam[CH - 1:CH, :]
            e_l = jnp.exp(gam_l)
            qt = (q * jnp.exp(gam)).astype(BF16)
            kt = (k * jnp.exp(-gam)).astype(BF16)
            khat = (k * jnp.exp(gam_l - gam)).astype(BF16)
            vb = v.astype(BF16)
            on_parts = []
            for h in range(NH):
                cs = slice(h * HD, (h + 1) * HD)
                st = st_ref[h]
                sst_ref[c, h] = st
                a = jnp.where(lower, _dot_nt(qt[:, cs], kt[:, cs]), 0.0)
                o_h = _dot_nt(qt[:, cs], st) + _dot(a, vb[:, cs])
                st_ref[h] = st * e_l[:, cs] + _dot_tn(vb[:, cs], khat[:, cs])
                o_ref[rows, cs] = o_h
                ro = lax.rsqrt(jnp.mean(o_h * o_h, axis=-1, keepdims=True) + EPS)
                on_parts.append(o_h * ro)
            on = jnp.concatenate(on_parts, axis=1)
            g_out = proj_ref[rows, 3 * D:4 * D]
            y_a = on * whn_v * (g_out * _sigmoid(g_out))
            cx = proj_ref[rows, 5 * D:6 * D] * proj_ref[rows, 6 * D:7 * D]
            prev8 = cxc_ref[...]
            cv = w0 * _shift_down(cx, prev8, 2) + w1 * _shift_down(cx, prev8, 1) + w2 * cx
            cxc_ref[...] = cx[CH - 8:CH, :]
            y_b = proj_ref[rows, 4 * D:5 * D] * cv
            m = _sigmoid(proj_ref[rows, 7 * D:8 * D]) * y_a + _sigmoid(proj_ref[rows, 8 * D:9 * D]) * y_b
            m_ref[rows, :] = m
            return carry

        lax.fori_loop(0, ncht, chunk, 0)
        m_v = m_ref[...]
        h1_ref[...] = x + jnp.dot(m_v.astype(BF16), wout_ref[...], preferred_element_type=F32)
        mt_ref[...] = m_v.T.astype(BF16)

    row = lambda w: pl.BlockSpec((TR, w), lambda i: (i, 0))
    col = lambda w: pl.BlockSpec((w, TR), lambda i: (0, i))
    vec = lambda r: pl.BlockSpec((r, D), lambda i: (0, 0))
    vm = pl.BlockSpec(memory_space=pltpu.VMEM)
    return pl.pallas_call(
        body, name="mix_block_fwd", grid=(nt,),
        in_specs=[row(D), vec(2), vec(1), vec(1), vec(3), vm, vm],
        out_specs=[col(D), row(9 * D), row(D), pl.BlockSpec((ncht, NH, HD, HD), lambda i: (i, 0, 0, 0)),
                   col(D), row(D)],
        out_shape=[jax.ShapeDtypeStruct((D, T), BF16), jax.ShapeDtypeStruct((T, 9 * D), BF16),
                   jax.ShapeDtypeStruct((T, D), F32), jax.ShapeDtypeStruct((T // CH, NH, HD, HD), F32),
                   jax.ShapeDtypeStruct((D, T), BF16), jax.ShapeDtypeStruct((T, D), F32)],
        scratch_shapes=[pltpu.VMEM((TR, 9 * D), F32), pltpu.VMEM((TR, D), F32), pltpu.VMEM((NH, HD, HD), F32),
                        pltpu.VMEM((8, D), F32)],
        compiler_params=_cparams(("arbitrary",)),
    )(h0, lb_param, wattn, whn, conv_w, w_in_g, w_out)


def _mix_block_bwd(dh1, projb, o, sst, lb_param, whn, conv_w, w_out):
    T = projb.shape[0]
    nt = T // TR
    ncht = TR // CH
    tb16 = TR // 16

    def body(dh1_ref, proj_ref, pc_ref, px_ref, o_ref, sst_ref, lbp_ref, whn_ref, cw_ref, wout_ref,
             dproj_ref, dlb_ref, dwhn_ref, dcw_ref, dm_ref, dst_ref, dcvc_ref, acc_lb, acc_hn, acc_cw):
        s = pl.program_id(0)
        tile = nt - 1 - s

        @pl.when(s == 0)
        def _():
            dst_ref[...] = jnp.zeros_like(dst_ref)
            dcvc_ref[...] = jnp.zeros_like(dcvc_ref)
            acc_lb[...] = jnp.zeros_like(acc_lb)
            acc_hn[...] = jnp.zeros_like(acc_hn)
            acc_cw[...] = jnp.zeros_like(acc_cw)

        dm_ref[...] = _dot_nt(dh1_ref[...], wout_ref[...])
        lb = _lower_bound(lbp_ref)
        lower, upper = _tri_masks()
        ltri = _ones_where(lower)
        utri = _ones_where(upper)
        whn_v = whn_ref[...]
        w0, w1, w2 = cw_ref[0:1, :], cw_ref[1:2, :], cw_ref[2:3, :]
        cx_before_tile = jnp.where(tile > 0, (pc_ref[...].astype(F32) * px_ref[...].astype(F32))[8:16, :], 0.0)
        rid = lax.broadcasted_iota(jnp.int32, (CH, D), 0)

        def chunk(cc, carry):
            c = ncht - 1 - cc
            r0 = pl.multiple_of(c * CH, CH)
            rows = pl.ds(r0, CH)
            slab = lambda n: proj_ref[rows, n * D:(n + 1) * D].astype(F32)
            q_raw, f_raw, v, g_out, b_gate, c_gate, x_conv = (slab(n) for n in range(7))
            sa = _sigmoid(slab(7))
            sb = _sigmoid(slab(8))
            dm_v = dm_ref[rows, :]

            sq = _sigmoid(q_raw)
            q = q_raw * sq
            sg, f, g, k = _gates(f_raw, lb)
            gam = _tri_matmul(ltri, g)
            gam_l = gam[CH - 1:CH, :]
            e_l = jnp.exp(gam_l)
            e_g = jnp.exp(gam)
            e_ng = jnp.exp(-gam)
            e_kl = jnp.exp(gam_l - gam)
            qt = q * e_g
            kt = k * e_ng
            khat = k * e_kl
            s_go = _sigmoid(g_out)
            silu_go = g_out * s_go
            cx = c_gate * x_conv
            rprev = pl.ds(pl.multiple_of(jnp.maximum(r0 - 16, 0), 16), 16)
            cx_prev_in = (proj_ref[rprev, 5 * D:6 * D].astype(F32) * proj_ref[rprev, 6 * D:7 * D].astype(F32))[8:16, :]
            prev8 = jnp.where(c > 0, cx_prev_in, cx_before_tile)
            cx_m1 = _shift_down(cx, prev8, 1)
            cx_m2 = _shift_down(cx, prev8, 2)
            cv = w0 * cx_m2 + w1 * cx_m1 + w2 * cx
            y_b = b_gate * cv

            o_v = o_ref[rows, :]
            ro_parts, on_parts = [], []
            for h in range(NH):
                cs = slice(h * HD, (h + 1) * HD)
                o_h = o_v[:, cs]
                ro = lax.rsqrt(jnp.mean(o_h * o_h, axis=-1, keepdims=True) + EPS)
                ro_parts.append(ro)
                on_parts.append(o_h * ro)
            on = jnp.concatenate(on_parts, axis=1)
            y_a = on * whn_v * silu_go

            dy_a = dm_v * sa
            dy_b = dm_v * sb
            dproj_ref[rows, 7 * D:8 * D] = (dm_v * y_a * sa * (1.0 - sa)).astype(BF16)
            dproj_ref[rows, 8 * D:9 * D] = (dm_v * y_b * sb * (1.0 - sb)).astype(BF16)
            dproj_ref[rows, 4 * D:5 * D] = (dy_b * cv).astype(BF16)
            dcv = dy_b * b_gate
            acc_cw[0:1, :] += jnp.sum(dcv * cx_m2, axis=0, keepdims=True)
            acc_cw[1:2, :] += jnp.sum(dcv * cx_m1, axis=0, keepdims=True)
            acc_cw[2:3, :] += jnp.sum(dcv * cx, axis=0, keepdims=True)
            next8 = dcvc_ref[...]
            dcx = w2 * dcv + w1 * _shift_up(dcv, next8, 1) + w0 * _shift_up(dcv, next8, 2)
            dcvc_ref[...] = dcv[0:8, :]
            dproj_ref[rows, 5 * D:6 * D] = (dcx * x_conv).astype(BF16)
            dproj_ref[rows, 6 * D:7 * D] = (dcx * c_gate).astype(BF16)
            don = dy_a * whn_v * silu_go
            dproj_ref[rows, 3 * D:4 * D] = (dy_a * on * whn_v * (s_go * (1.0 + g_out * (1.0 - s_go)))).astype(BF16)
            acc_hn[...] += jnp.sum(dy_a * silu_go * on, axis=0, keepdims=True)

            dq_parts, dk_parts, dv_parts, dgam_parts, ext_parts = [], [], [], [], []
            for h in range(NH):
                cs = slice(h * HD, (h + 1) * HD)
                on_h = on_parts[h]
                don_h = don[:, cs]
                do_h = ro_parts[h] * (don_h - on_h * jnp.mean(don_h * on_h, axis=-1, keepdims=True))
                qt_h, kt_h, khat_h, v_h = qt[:, cs], kt[:, cs], khat[:, cs], v[:, cs]
                st = sst_ref[c, h]
                dstn = dst_ref[h]
                a_t = jnp.where(upper, _dot_nt(kt_h, qt_h), 0.0)
                da = jnp.where(lower, _dot_nt(do_h, v_h), 0.0)
                da_t = jnp.where(upper, _dot_nt(v_h, do_h), 0.0)
                dv_h = _dot(a_t, do_h) + _dot_nt(khat_h, dstn)
                dqt_state = _dot(do_h, st)
                dqt_chunk = _dot(da, kt_h)
                dkt = _dot(da_t, qt_h)
                dkhat = _dot(v_h, dstn)
                dq_h = (dqt_state + dqt_chunk) * e_g[:, cs]
                dk_h = dkt * e_ng[:, cs] + dkhat * e_kl[:, cs]
                khat_dkhat = dkhat * khat_h
                ext = (jnp.sum(khat_dkhat, axis=0, keepdims=True)
                       + e_l[:, cs] * jnp.sum(st * dstn, axis=0, keepdims=True))
                dst_ref[h] = _dot_tn(do_h, qt_h) + dstn * e_l[:, cs]
                dq_parts.append(dq_h)
                dk_parts.append(dk_h)
                dv_parts.append(dv_h)
                qt_seen = qt_h.astype(BF16).astype(F32)
                kt_seen = kt_h.astype(BF16).astype(F32)
                dgam_parts.append(qt_h * dqt_state + qt_seen * dqt_chunk - kt_seen * dkt - khat_dkhat)
                ext_parts.append(ext)
            dq = jnp.concatenate(dq_parts, axis=1)
            dk = jnp.concatenate(dk_parts, axis=1)
            dgam = jnp.concatenate(dgam_parts, axis=1)
            ext = jnp.concatenate(ext_parts, axis=1)
            dgam = dgam + jnp.where(rid == CH - 1, ext, 0.0)
            dg = _tri_matmul(utri, dgam)
            dproj_ref[rows, 0:D] = (dq * (sq * (1.0 + q_raw * (1.0 - sq)))).astype(BF16)
            df = dg / f - dk
            dproj_ref[rows, D:2 * D] = (df * (1.0 - lb) * sg * (1.0 - sg)).astype(BF16)
            dproj_ref[rows, 2 * D:3 * D] = jnp.concatenate(dv_parts, axis=1).astype(BF16)
            real = (tile * TR + r0 + rid) >= PAD
            acc_lb[...] += jnp.sum(jnp.where(real, df * (1.0 - sg), 0.0), axis=0, keepdims=True)
            return carry

        lax.fori_loop(0, ncht, chunk, 0)

        @pl.when(s == nt - 1)
        def _():
            dlb_ref[...] = acc_lb[...] * lb * (1.0 - lb)
            hn = acc_hn[...]
            tot = hn[:, 0:HD]
            for h in range(1, NH):
                tot = tot + hn[:, h * HD:(h + 1) * HD]
            dwhn_ref[...] = tot
            dcw_ref[...] = acc_cw[0:3, :]

    rev = lambda s: (nt - 1 - s, 0)
    prevc = lambda s: (jnp.maximum((nt - 1 - s) * tb16 - 1, 0), 5)
    prevx = lambda s: (jnp.maximum((nt - 1 - s) * tb16 - 1, 0), 6)
    const = lambda s: (0, 0)
    return pl.pallas_call(
        body, name="mix_block_bwd", grid=(nt,),
        in_specs=[pl.BlockSpec((TR, D), rev),
                  pl.BlockSpec((TR, 9 * D), rev),
                  pl.BlockSpec((16, D), prevc),
                  pl.BlockSpec((16, D), prevx),
                  pl.BlockSpec((TR, D), rev),
                  pl.BlockSpec((ncht, NH, HD, HD), lambda s: (nt - 1 - s, 0, 0, 0)),
                  pl.BlockSpec((2, D), const),
                  pl.BlockSpec((1, D), const),
                  pl.BlockSpec((3, D), const),
                  pl.BlockSpec(memory_space=pltpu.VMEM)],
        out_specs=[pl.BlockSpec((TR, 9 * D), rev),
                   pl.BlockSpec((1, D), const),
                   pl.BlockSpec((1, HD), const),
                   pl.BlockSpec((3, D), const)],
        out_shape=[jax.ShapeDtypeStruct((T, 9 * D), BF16), jax.ShapeDtypeStruct((1, D), F32),
                   jax.ShapeDtypeStruct((1, HD), F32), jax.ShapeDtypeStruct((3, D), F32)],
        scratch_shapes=[pltpu.VMEM((TR, D), F32), pltpu.VMEM((NH, HD, HD), F32), pltpu.VMEM((8, D), F32),
                        pltpu.VMEM((1, D), F32), pltpu.VMEM((1, D), F32), pltpu.VMEM((8, D), F32)],
        compiler_params=_cparams(("arbitrary",)),
    )(dh1, projb, projb, projb, o, sst, lb_param, whn, conv_w, w_out)


def _ffn_fwd(up, fcw, fcb):
    T = up.shape[0]
    nt = T // TR

    def body(up_ref, w_ref, b_ref, gg_ref, carry_ref):
        i = pl.program_id(0)

        @pl.when(i == 0)
        def _():
            carry_ref[...] = jnp.zeros_like(carry_ref)

        w0, w1, w2 = w_ref[0:1, :], w_ref[1:2, :], w_ref[2:3, :]

        def chunk(c, carry):
            rows = pl.ds(pl.multiple_of(c * CH, CH), CH)
            a_pre = up_ref[rows, 0:DFF]
            val = up_ref[rows, DFF:2 * DFF]
            prev8 = carry_ref[...]
            a = w0 * _shift_down(a_pre, prev8, 2) + w1 * _shift_down(a_pre, prev8, 1) + w2 * a_pre + b_ref[...]
            carry_ref[...] = a_pre[CH - 8:CH, :]
            gg_ref[rows, :] = (a * _sigmoid(a) * val).astype(BF16)
            return carry

        lax.fori_loop(0, TR // CH, chunk, 0)

    return pl.pallas_call(
        body, name="ffn_fwd", grid=(nt,),
        in_specs=[pl.BlockSpec((TR, 2 * DFF), lambda i: (i, 0)),
                  pl.BlockSpec((3, DFF), lambda i: (0, 0)),
                  pl.BlockSpec((1, DFF), lambda i: (0, 0))],
        out_specs=pl.BlockSpec((TR, DFF), lambda i: (i, 0)),
        out_shape=jax.ShapeDtypeStruct((T, DFF), BF16),
        scratch_shapes=[pltpu.VMEM((8, DFF), F32)],
        compiler_params=_cparams(("arbitrary",)),
    )(up, fcw, fcb)


def _ffn_bwd(up, dgg, fcw, fcb):
    T = up.shape[0]
    nt = T // TR
    tb = TR // 8
    ncht = TR // CH

    def body(up_ref, pa_ref, dgg_ref, w_ref, b_ref, dup_ref, dfw_ref, dfb_ref, carry_ref, acc_w, acc_b):
        s = pl.program_id(0)
        tile = nt - 1 - s

        @pl.when(s == 0)
        def _():
            carry_ref[...] = jnp.zeros_like(carry_ref)
            acc_w[...] = jnp.zeros_like(acc_w)
            acc_b[...] = jnp.zeros_like(acc_b)

        w0, w1, w2 = w_ref[0:1, :], w_ref[1:2, :], w_ref[2:3, :]
        a_before_tile = jnp.where(tile > 0, pa_ref[...], 0.0)

        def chunk(cc, carry):
            c = ncht - 1 - cc
            r0 = pl.multiple_of(c * CH, CH)
            rows = pl.ds(r0, CH)
            a_pre = up_ref[rows, 0:DFF]
            val = up_ref[rows, DFF:2 * DFF]
            rprev = pl.multiple_of(jnp.maximum(r0 - 8, 0), 8)
            prev8 = jnp.where(c > 0, up_ref[pl.ds(rprev, 8), 0:DFF], a_before_tile)
            a_m1 = _shift_down(a_pre, prev8, 1)
            a_m2 = _shift_down(a_pre, prev8, 2)
            a = w0 * a_m2 + w1 * a_m1 + w2 * a_pre + b_ref[...]
            sig = _sigmoid(a)
            dgg_v = dgg_ref[rows, :]
            da = dgg_v * val * (sig * (1.0 + a * (1.0 - sig)))
            dup_ref[rows, DFF:2 * DFF] = (dgg_v * (a * sig)).astype(BF16)
            next8 = carry_ref[...]
            dup_ref[rows, 0:DFF] = (w2 * da + w1 * _shift_up(da, next8, 1) + w0 * _shift_up(da, next8, 2)).astype(BF16)
            carry_ref[...] = da[0:8, :]
            acc_w[0:1, :] += jnp.sum(da * a_m2, axis=0, keepdims=True)
            acc_w[1:2, :] += jnp.sum(da * a_m1, axis=0, keepdims=True)
            acc_w[2:3, :] += jnp.sum(da * a_pre, axis=0, keepdims=True)
            acc_b[...] += jnp.sum(da, axis=0, keepdims=True)
            return carry

        lax.fori_loop(0, ncht, chunk, 0)

        @pl.when(s == nt - 1)
        def _():
            dfw_ref[...] = acc_w[0:3, :]
            dfb_ref[...] = acc_b[...]

    rev = lambda s: (nt - 1 - s, 0)
    const = lambda s: (0, 0)
    return pl.pallas_call(
        body, name="ffn_bwd", grid=(nt,),
        in_specs=[pl.BlockSpec((TR, 2 * DFF), rev),
                  pl.BlockSpec((8, DFF), lambda s: (jnp.maximum((nt - 1 - s) * tb - 1, 0), 0)),
                  pl.BlockSpec((TR, DFF), rev),
                  pl.BlockSpec((3, DFF), const),
                  pl.BlockSpec((1, DFF), const)],
        out_specs=[pl.BlockSpec((TR, 2 * DFF), rev),
                   pl.BlockSpec((3, DFF), const),
                   pl.BlockSpec((1, DFF), const)],
        out_shape=[jax.ShapeDtypeStruct((T, 2 * DFF), BF16), jax.ShapeDtypeStruct((3, DFF), F32),
                   jax.ShapeDtypeStruct((1, DFF), F32)],
        scratch_shapes=[pltpu.VMEM((8, DFF), F32), pltpu.VMEM((8, DFF), F32), pltpu.VMEM((1, DFF), F32)],
        compiler_params=_cparams(("arbitrary",)),
    )(up, up, dgg, fcw, fcb)


def _loss_head(h2, tgt, wfin):
    T = h2.shape[0]
    nt = T // TR

    def body(h_ref, t_ref, w_ref, dh_ref, loss_ref, dw_ref):
        i = pl.program_id(0)

        @pl.when(i == 0)
        def _():
            loss_ref[...] = jnp.zeros_like(loss_ref)
            dw_ref[...] = jnp.zeros_like(dw_ref)

        x = h_ref[...]
        r = lax.rsqrt(jnp.mean(x * x, axis=-1, keepdims=True) + EPS)
        n = x * r
        w = w_ref[...]
        diff = jnp.where(i > 0, n * w - t_ref[...], 0.0)
        loss_ref[...] += jnp.sum(diff * diff, axis=0, keepdims=True) * (0.5 / D)
        dy = diff * (1.0 / D)
        dw_ref[...] += jnp.sum(dy * n, axis=0, keepdims=True)
        dn = dy * w
        dh_ref[...] = r * (dn - n * jnp.mean(dn * n, axis=-1, keepdims=True))

    row = pl.BlockSpec((TR, D), lambda i: (i, 0))
    vec = pl.BlockSpec((1, D), lambda i: (0, 0))
    return pl.pallas_call(
        body, name="loss_head", grid=(nt,),
        in_specs=[row, pl.BlockSpec((TR, D), lambda i: (jnp.maximum(i - 1, 0), 0)), vec],
        out_specs=[row, vec, vec],
        out_shape=[jax.ShapeDtypeStruct((T, D), F32), jax.ShapeDtypeStruct((1, D), F32),
                   jax.ShapeDtypeStruct((1, D), F32)],
        compiler_params=_cparams(("arbitrary",)),
    )(h2, tgt, wfin)


def _ffn_block_fwd(h1, tgt, wffn, w_up_g, fcw, fcb, w_down, wfin):
    T = h1.shape[0]
    nt = T // TR
    nsl = w_up_g.shape[0]
    wsl = w_up_g.shape[2]

    def body(h_ref, t_ref, wn_ref, wup_ref, cw_ref, cb_ref, wdn_ref, wf_ref,
             u2t_ref, upb_ref, ggt_ref, dh_ref, loss_ref, dwf_ref, up_scr, gg_ref, carry_ref):
        i = pl.program_id(0)

        @pl.when(i == 0)
        def _():
            carry_ref[...] = jnp.zeros_like(carry_ref)
            loss_ref[...] = jnp.zeros_like(loss_ref)
            dwf_ref[...] = jnp.zeros_like(dwf_ref)

        x = h_ref[...]
        r2 = lax.rsqrt(jnp.mean(x * x, axis=-1, keepdims=True) + EPS)
        u2_f = x * r2 * wn_ref[...]
        u2 = u2_f.astype(BF16)
        u2t_ref[...] = u2_f.T.astype(BF16)
        for s in range(nsl):
            up_s = jnp.dot(u2, wup_ref[s], preferred_element_type=F32)
            up_scr[:, s * wsl:(s + 1) * wsl] = up_s
            upb_ref[:, s * wsl:(s + 1) * wsl] = up_s.astype(BF16)
        w0, w1, w2 = cw_ref[0:1, :], cw_ref[1:2, :], cw_ref[2:3, :]

        def chunk(c, carry):
            rows = pl.ds(pl.multiple_of(c * CH, CH), CH)
            a_pre = up_scr[rows, 0:DFF]
            val = up_scr[rows, DFF:2 * DFF]
            prev8 = carry_ref[...]
            a = w0 * _shift_down(a_pre, prev8, 2) + w1 * _shift_down(a_pre, prev8, 1) + w2 * a_pre + cb_ref[...]
            carry_ref[...] = a_pre[CH - 8:CH, :]
            gg_ref[rows, :] = a * _sigmoid(a) * val
            return carry

        lax.fori_loop(0, TR // CH, chunk, 0)
        gg_v = gg_ref[...]
        ggt_ref[...] = gg_v.T.astype(BF16)
        h2 = x + jnp.dot(gg_v.astype(BF16), wdn_ref[...], preferred_element_type=F32)
        r3 = lax.rsqrt(jnp.mean(h2 * h2, axis=-1, keepdims=True) + EPS)
        n3 = h2 * r3
        wf = wf_ref[...]
        diff = jnp.where(i > 0, n3 * wf - t_ref[...], 0.0)
        loss_ref[...] += jnp.sum(diff * diff, axis=0, keepdims=True) * (0.5 / D)
        dy = diff * (1.0 / D)
        dwf_ref[...] += jnp.sum(dy * n3, axis=0, keepdims=True)
        dn = dy * wf
        dh_ref[...] = r3 * (dn - n3 * jnp.mean(dn * n3, axis=-1, keepdims=True))

    row = lambda w: pl.BlockSpec((TR, w), lambda i: (i, 0))
    col = lambda w: pl.BlockSpec((w, TR), lambda i: (0, i))
    vec = lambda w, r=1: pl.BlockSpec((r, w), lambda i: (0, 0))
    vm = pl.BlockSpec(memory_space=pltpu.VMEM)
    return pl.pallas_call(
        body, name="ffn_block_fwd", grid=(nt,),
        in_specs=[row(D), pl.BlockSpec((TR, D), lambda i: (jnp.maximum(i - 1, 0), 0)), vec(D), vm,
                  vec(DFF, 3), vec(DFF), vm, vec(D)],
        out_specs=[col(D), row(2 * DFF), col(DFF), row(D), vec(D), vec(D)],
        out_shape=[jax.ShapeDtypeStruct((D, T), BF16), jax.ShapeDtypeStruct((T, 2 * DFF), BF16),
                   jax.ShapeDtypeStruct((DFF, T), BF16), jax.ShapeDtypeStruct((T, D), F32),
                   jax.ShapeDtypeStruct((1, D), F32), jax.ShapeDtypeStruct((1, D), F32)],
        scratch_shapes=[pltpu.VMEM((TR, 2 * DFF), F32), pltpu.VMEM((TR, DFF), F32), pltpu.VMEM((8, DFF), F32)],
        compiler_params=_cparams(("arbitrary",)),
    )(h1, tgt, wffn, w_up_g, fcw, fcb, w_down, wfin)


def _ffn_block_bwd(dh2, upb, h1, wffn, w_up_g, fcw, fcb, w_down):
    T = h1.shape[0]
    nt = T // TR
    ncht = TR // CH
    nsl = w_up_g.shape[0]
    wsl = w_up_g.shape[2]
    tb16 = TR // 16

    def body(dh2_ref, up_ref, pa_ref, h_ref, wn_ref, wup_ref, cw_ref, cb_ref, wdn_ref,
             dup_ref, dh1_ref, dfw_ref, dfb_ref, dwn_ref, dgg_scr, carry_ref, acc_w, acc_b, acc_n):
        s = pl.program_id(0)
        tile = nt - 1 - s

        @pl.when(s == 0)
        def _():
            carry_ref[...] = jnp.zeros_like(carry_ref)
            acc_w[...] = jnp.zeros_like(acc_w)
            acc_b[...] = jnp.zeros_like(acc_b)
            acc_n[...] = jnp.zeros_like(acc_n)

        dh2 = dh2_ref[...]
        dgg_scr[...] = _dot_nt(dh2, wdn_ref[...])
        w0, w1, w2 = cw_ref[0:1, :], cw_ref[1:2, :], cw_ref[2:3, :]
        a_before_tile = jnp.where(tile > 0, pa_ref[...].astype(F32)[8:16, :], 0.0)

        def chunk(cc, carry):
            c = ncht - 1 - cc
            r0 = pl.multiple_of(c * CH, CH)
            rows = pl.ds(r0, CH)
            a_pre = up_ref[rows, 0:DFF].astype(F32)
            val = up_ref[rows, DFF:2 * DFF].astype(F32)
            rprev = pl.multiple_of(jnp.maximum(r0 - 16, 0), 16)
            prev_in = up_ref[pl.ds(rprev, 16), 0:DFF].astype(F32)[8:16, :]
            prev8 = jnp.where(c > 0, prev_in, a_before_tile)
            a_m1 = _shift_down(a_pre, prev8, 1)
            a_m2 = _shift_down(a_pre, prev8, 2)
            a = w0 * a_m2 + w1 * a_m1 + w2 * a_pre + cb_ref[...]
            sig = _sigmoid(a)
            dgg_v = dgg_scr[rows, :]
            da = dgg_v * val * (sig * (1.0 + a * (1.0 - sig)))
            dup_ref[rows, DFF:2 * DFF] = (dgg_v * (a * sig)).astype(BF16)
            next8 = carry_ref[...]
            dup_ref[rows, 0:DFF] = (w2 * da + w1 * _shift_up(da, next8, 1) + w0 * _shift_up(da, next8, 2)).astype(BF16)
            carry_ref[...] = da[0:8, :]
            acc_w[0:1, :] += jnp.sum(da * a_m2, axis=0, keepdims=True)
            acc_w[1:2, :] += jnp.sum(da * a_m1, axis=0, keepdims=True)
            acc_w[2:3, :] += jnp.sum(da * a_pre, axis=0, keepdims=True)
            acc_b[...] += jnp.sum(da, axis=0, keepdims=True)
            return carry

        lax.fori_loop(0, ncht, chunk, 0)
        du2 = _dot_nt(dup_ref[:, 0:wsl], wup_ref[0])
        for sl in range(1, nsl):
            du2 = du2 + _dot_nt(dup_ref[:, sl * wsl:(sl + 1) * wsl], wup_ref[sl])
        x = h_ref[...]
        r2 = lax.rsqrt(jnp.mean(x * x, axis=-1, keepdims=True) + EPS)
        n2 = x * r2
        dn = du2 * wn_ref[...]
        dh1_ref[...] = dh2 + r2 * (dn - n2 * jnp.mean(dn * n2, axis=-1, keepdims=True))
        acc_n[...] += jnp.sum(du2 * n2, axis=0, keepdims=True)

        @pl.when(s == nt - 1)
        def _():
            dfw_ref[...] = acc_w[0:3, :]
            dfb_ref[...] = acc_b[...]
            dwn_ref[...] = acc_n[...]

    rev = lambda w: pl.BlockSpec((TR, w), lambda s: (nt - 1 - s, 0))
    vec = lambda w, r=1: pl.BlockSpec((r, w), lambda s: (0, 0))
    vm = pl.BlockSpec(memory_space=pltpu.VMEM)
    return pl.pallas_call(
        body, name="ffn_block_bwd", grid=(nt,),
        in_specs=[rev(D), rev(2 * DFF),
                  pl.BlockSpec((16, DFF), lambda s: (jnp.maximum((nt - 1 - s) * tb16 - 1, 0), 0)),
                  rev(D), vec(D), vm, vec(DFF, 3), vec(DFF), vm],
        out_specs=[rev(2 * DFF), rev(D), vec(DFF, 3), vec(DFF), vec(D)],
        out_shape=[jax.ShapeDtypeStruct((T, 2 * DFF), BF16), jax.ShapeDtypeStruct((T, D), F32),
                   jax.ShapeDtypeStruct((3, DFF), F32), jax.ShapeDtypeStruct((1, DFF), F32),
                   jax.ShapeDtypeStruct((1, D), F32)],
        scratch_shapes=[pltpu.VMEM((TR, DFF), F32), pltpu.VMEM((8, DFF), F32), pltpu.VMEM((8, DFF), F32),
                        pltpu.VMEM((1, DFF), F32), pltpu.VMEM((1, D), F32)],
        compiler_params=_cparams(("arbitrary",)),
    )(dh2, upb, upb, h1, wffn, w_up_g, fcw, fcb, w_down)


def _place():
    x, y, c = lax.axis_index("x"), lax.axis_index("y"), lax.axis_index("c")
    return x, y, c


_CHIP_FLIPS = ((1, 0), (0, 1), (1, 1))


def _flip(v, bit):
    return 1 - v if bit else v


def _into_slot(w, j_idx, *, rb, dtype, name):
    r, cdim = w.shape

    def body(j_ref, w_ref, out_ref):
        del j_ref
        out_ref[...] = w_ref[...].astype(dtype)

    grid_spec = pltpu.PrefetchScalarGridSpec(
        num_scalar_prefetch=1, grid=(r // rb,),
        in_specs=[pl.BlockSpec((rb, cdim), lambda i, j_ref: (i, 0))],
        out_specs=pl.BlockSpec((None, rb, cdim), lambda i, j_ref: (j_ref[0], i, 0)))
    return pl.pallas_call(
        body, name=name, grid_spec=grid_spec, out_shape=jax.ShapeDtypeStruct((NSHARD, r, cdim), dtype),
        compiler_params=_cparams(("parallel",)),
    )(j_idx, w)


def _allgather_weights(slotted):
    ncopy = len(slotted)
    nw = ncopy - 1

    def body(*refs):
        outs = refs[ncopy:2 * ncopy]
        send_sems, recv_sems = refs[2 * ncopy:]
        x, y, c = _place()
        j = 2 * x + y
        sibling = (x, y, 1 - c)
        sends = []
        for w in range(ncopy):
            r = outs[w].shape[1]
            half = r // 2 if w < nw else r
            off = c * half if w < nw else 0
            mine = outs[w].at[j, pl.ds(off, half), :]
            for kk, (fx, fy) in enumerate(_CHIP_FLIPS):
                cp = pltpu.make_async_remote_copy(
                    src_ref=mine, dst_ref=mine, send_sem=send_sems.at[w, kk], recv_sem=recv_sems.at[w, kk],
                    device_id=(_flip(x, fx), _flip(y, fy), c), device_id_type=MESH)
                cp.start()
                sends.append(cp)
        for w in range(nw):
            half = outs[w].shape[1] // 2
            off = c * half
            for kk, (fx, fy) in enumerate(_CHIP_FLIPS):
                jk = 2 * _flip(x, fx) + _flip(y, fy)
                piece = outs[w].at[jk, pl.ds(off, half), :]
                pltpu.make_async_remote_copy(
                    src_ref=piece, dst_ref=piece, send_sem=send_sems.at[w, kk], recv_sem=recv_sems.at[w, kk],
                    device_id=sibling, device_id_type=MESH).wait_recv()
                fw = pltpu.make_async_remote_copy(
                    src_ref=piece, dst_ref=piece, send_sem=send_sems.at[w, 3 + kk], recv_sem=recv_sems.at[w, 3 + kk],
                    device_id=sibling, device_id_type=MESH)
                fw.start()
                sends.append(fw)
        for kk, (fx, fy) in enumerate(_CHIP_FLIPS):
            jk = 2 * _flip(x, fx) + _flip(y, fy)
            piece = outs[nw].at[jk]
            pltpu.make_async_remote_copy(
                src_ref=piece, dst_ref=piece, send_sem=send_sems.at[nw, kk], recv_sem=recv_sems.at[nw, kk],
                device_id=sibling, device_id_type=MESH).wait_recv()
        for w in range(nw):
            half = outs[w].shape[1] // 2
            off = (1 - c) * half
            for kk, (fx, fy) in enumerate(_CHIP_FLIPS):
                jk = 2 * _flip(x, fx) + _flip(y, fy)
                piece = outs[w].at[jk, pl.ds(off, half), :]
                pltpu.make_async_remote_copy(
                    src_ref=piece, dst_ref=piece, send_sem=send_sems.at[w, 3 + kk], recv_sem=recv_sems.at[w, 3 + kk],
                    device_id=sibling, device_id_type=MESH).wait_recv()
        for cp in sends:
            cp.wait_send()

    any_spec = pl.BlockSpec(memory_space=pl.ANY)
    return pl.pallas_call(
        body, name="allgather_weights",
        in_specs=[any_spec] * ncopy, out_specs=[any_spec] * ncopy,
        out_shape=[jax.ShapeDtypeStruct(a.shape, a.dtype) for a in slotted],
        input_output_aliases={i: i for i in range(ncopy)},
        scratch_shapes=[pltpu.SemaphoreType.DMA((ncopy, 6)), pltpu.SemaphoreType.DMA((ncopy, 6))],
    )(*slotted)


def _pair_exchange(grads):
    nw = len(grads)

    def body(*refs):
        ins, outs = refs[:nw], refs[nw:2 * nw]
        send_sems, recv_sems = refs[2 * nw:]
        x, y, c = _place()
        sibling = (x, y, 1 - c)
        cps = []
        for w in range(nw):
            half = ins[w].shape[1] // 2
            cp = pltpu.make_async_remote_copy(
                src_ref=ins[w].at[:, pl.ds((1 - c) * half, half), :], dst_ref=outs[w],
                send_sem=send_sems.at[w], recv_sem=recv_sems.at[w], device_id=sibling, device_id_type=MESH)
            cp.start()
            cps.append(cp)
        for cp in cps:
            cp.wait()

    any_spec = pl.BlockSpec(memory_space=pl.ANY)
    return pl.pallas_call(
        body, name="grad_pair_exchange",
        in_specs=[any_spec] * nw, out_specs=[any_spec] * nw,
        out_shape=[jax.ShapeDtypeStruct((g.shape[0], g.shape[1] // 2, g.shape[2]), g.dtype) for g in grads],
        scratch_shapes=[pltpu.SemaphoreType.DMA((nw,)), pltpu.SemaphoreType.DMA((nw,))],
    )(*grads)


def _pair_add(g, other, c_idx, *, rb, name):
    S, r, cdim = g.shape
    half = r // 2
    nb = half // rb

    def body(c_ref, g_ref, o_ref, out_ref):
        del c_ref
        out_ref[...] = (g_ref[...] + o_ref[...]).astype(BF16)

    grid_spec = pltpu.PrefetchScalarGridSpec(
        num_scalar_prefetch=1, grid=(S, nb),
        in_specs=[pl.BlockSpec((None, rb, cdim), lambda s, i, c_ref: (s, c_ref[0] * nb + i, 0)),
                  pl.BlockSpec((None, rb, cdim), lambda s, i, c_ref: (s, i, 0))],
        out_specs=pl.BlockSpec((None, rb, cdim), lambda s, i, c_ref: (s, i, 0)))
    return pl.pallas_call(
        body, name=name, grid_spec=grid_spec, out_shape=jax.ShapeDtypeStruct((S, half, cdim), BF16),
        compiler_params=_cparams(("parallel", "parallel")),
    )(c_idx, g, other)


def _chip_exchange(psums):
    nw = len(psums)

    def body(*refs):
        ins, outs = refs[:nw], refs[nw:2 * nw]
        send_sems, recv_sems = refs[2 * nw:]
        x, y, c = _place()
        cps = []
        for w in range(nw):
            for kk, (fx, fy) in enumerate(_CHIP_FLIPS):
                px, py = _flip(x, fx), _flip(y, fy)
                cp = pltpu.make_async_remote_copy(
                    src_ref=ins[w].at[2 * px + py], dst_ref=outs[w].at[kk],
                    send_sem=send_sems.at[w, kk], recv_sem=recv_sems.at[w, kk],
                    device_id=(px, py, c), device_id_type=MESH)
                cp.start()
                cps.append(cp)
        for cp in cps:
            cp.wait()

    any_spec = pl.BlockSpec(memory_space=pl.ANY)
    return pl.pallas_call(
        body, name="grad_chip_exchange",
        in_specs=[any_spec] * nw, out_specs=[any_spec] * nw,
        out_shape=[jax.ShapeDtypeStruct((3,) + p.shape[1:], p.dtype) for p in psums],
        scratch_shapes=[pltpu.SemaphoreType.DMA((nw, 3)), pltpu.SemaphoreType.DMA((nw, 3))],
    )(*psums)


def _chip_sum(psum, parts, cj_idx, *, rb, name):
    S, half, cdim = psum.shape
    nb = half // rb

    def body(cj_ref, own_ref, p_ref, out_ref):
        del cj_ref
        f = lambda v: v.astype(F32)
        out_ref[...] = ((f(own_ref[...]) + f(p_ref[0])) + f(p_ref[1])) + f(p_ref[2])

    grid_spec = pltpu.PrefetchScalarGridSpec(
        num_scalar_prefetch=1, grid=(nb,),
        in_specs=[pl.BlockSpec((None, rb, cdim), lambda i, cj: (cj[1], i, 0)),
                  pl.BlockSpec((3, rb, cdim), lambda i, cj: (0, i, 0))],
        out_specs=pl.BlockSpec((rb, cdim), lambda i, cj: (cj[0] * nb + i, 0)))
    return pl.pallas_call(
        body, name=name, grid_spec=grid_spec, out_shape=jax.ShapeDtypeStruct((2 * half, cdim), F32),
        compiler_params=_cparams(("parallel",)),
    )(cj_idx, psum, parts)


def _pair_gather(grads):
    nw = len(grads)

    def body(*refs):
        outs = refs[nw:2 * nw]
        send_sems, recv_sems = refs[2 * nw:]
        x, y, c = _place()
        sibling = (x, y, 1 - c)
        cps = []
        for w in range(nw):
            half = outs[w].shape[0] // 2
            mine = outs[w].at[pl.ds(c * half, half), :]
            cp = pltpu.make_async_remote_copy(
                src_ref=mine, dst_ref=mine, send_sem=send_sems.at[w], recv_sem=recv_sems.at[w],
                device_id=sibling, device_id_type=MESH)
            cp.start()
            cps.append(cp)
        for w in range(nw):
            half = outs[w].shape[0] // 2
            theirs = outs[w].at[pl.ds((1 - c) * half, half), :]
            pltpu.make_async_remote_copy(
                src_ref=theirs, dst_ref=theirs, send_sem=send_sems.at[w], recv_sem=recv_sems.at[w],
                device_id=sibling, device_id_type=MESH).wait_recv()
        for cp in cps:
            cp.wait_send()

    any_spec = pl.BlockSpec(memory_space=pl.ANY)
    return pl.pallas_call(
        body, name="grad_pair_gather",
        in_specs=[any_spec] * nw, out_specs=[any_spec] * nw,
        out_shape=[jax.ShapeDtypeStruct(g.shape, g.dtype) for g in grads],
        input_output_aliases={i: i for i in range(nw)},
        scratch_shapes=[pltpu.SemaphoreType.DMA((nw,)), pltpu.SemaphoreType.DMA((nw,))],
    )(*grads)


def _allreduce_small(pieces):
    n = len(pieces)
    width = max(p.shape[1] for p in pieces)
    starts, at = [], 0
    for p in pieces:
        if p.shape[0] > 1:
            at = -(-at // 8) * 8
        starts.append(at)
        at += p.shape[0]
    rows = -(-at // 8) * 8
    flips = [(fx, fy, fc) for fx in (0, 1) for fy in (0, 1) for fc in (0, 1)][1:]

    def body(*refs):
        ins, outs = refs[:n], refs[n:2 * n]
        mine_ref, slots_ref, send_sems, recv_sems = refs[2 * n:]
        x, y, c = _place()
        me = 4 * x + 2 * y + c
        mine_ref[...] = jnp.zeros_like(mine_ref)
        for p_ref, st in zip(ins, starts):
            r, wd = p_ref.shape
            mine_ref[st:st + r, 0:wd] = p_ref[...]
        slots_ref[me] = mine_ref[...]
        cps = []
        for kk, (fx, fy, fc) in enumerate(flips):
            cp = pltpu.make_async_remote_copy(
                src_ref=mine_ref, dst_ref=slots_ref.at[me], send_sem=send_sems.at[kk], recv_sem=recv_sems.at[kk],
                device_id=(_flip(x, fx), _flip(y, fy), _flip(c, fc)), device_id_type=MESH)
            cp.start()
            cps.append(cp)
        for cp in cps:
            cp.wait()
        tot = slots_ref[0]
        for d in range(1, 8):
            tot = tot + slots_ref[d]
        mine_ref[...] = tot
        for idx, (o_ref, p_ref, st) in enumerate(zip(outs, ins, starts)):
            r, wd = p_ref.shape
            val = mine_ref[st:st + r, 0:wd]
            o_ref[...] = jnp.sum(val, keepdims=True) if idx == n - 1 else val

    vm = pl.BlockSpec(memory_space=pltpu.VMEM)
    out_shape = [jax.ShapeDtypeStruct(p.shape, F32) for p in pieces[:-1]] + [jax.ShapeDtypeStruct((1, 1), F32)]
    return pl.pallas_call(
        body, name="allreduce_small", in_specs=[vm] * n, out_specs=[vm] * n, out_shape=out_shape,
        scratch_shapes=[pltpu.VMEM((rows, width), F32), pltpu.VMEM((8, rows, width), F32),
                        pltpu.SemaphoreType.DMA((7,)), pltpu.SemaphoreType.DMA((7,))],
    )(*pieces)


def _adamw(w, g, m, v, *, rb, name):
    r, cdim = w.shape

    def body(w_ref, g_ref, m_ref, v_ref, d_ref, nm_ref, nv_ref):
        d_ref[...], nm_ref[...], nv_ref[...] = _adamw_update(w_ref[...], g_ref[...], m_ref[...], v_ref[...])

    spec = pl.BlockSpec((rb, cdim), lambda i: (i, 0))
    shp = jax.ShapeDtypeStruct((r, cdim), F32)
    return pl.pallas_call(
        body, name=name, grid=(r // rb,), in_specs=[spec] * 4, out_specs=[spec] * 3, out_shape=[shp] * 3,
        compiler_params=_cparams(("parallel",)),
    )(w, g, m, v)


def _adamw_update(w, g, m, v):
    nm = ADAM_B1 * m + (1.0 - ADAM_B1) * g
    nv = ADAM_B2 * v + (1.0 - ADAM_B2) * (g * g)
    m_hat = nm / (1.0 - ADAM_B1 ** ADAM_STEP)
    v_hat = nv / (1.0 - ADAM_B2 ** ADAM_STEP)
    return -ADAM_LR * (m_hat / (jnp.sqrt(v_hat) + ADAM_EPS) + ADAM_WD * w), nm, nv


def _adamw_small(params):
    n = len(params)

    def body(*refs):
        ins, outs = refs[:4 * n], refs[4 * n:]
        for p in range(n):
            w_ref, g_ref, m_ref, v_ref = ins[4 * p:4 * p + 4]
            d, nm, nv = _adamw_update(w_ref[...], g_ref[...], m_ref[...], v_ref[...])
            outs[3 * p][...] = d
            outs[3 * p + 1][...] = nm
            outs[3 * p + 2][...] = nv

    vm = pl.BlockSpec(memory_space=pltpu.VMEM)
    flat = [a for p in params for a in p]
    out_shape = [jax.ShapeDtypeStruct(p[0].shape, F32) for p in params for _ in range(3)]
    res = pl.pallas_call(
        body, name="adamw_small", in_specs=[vm] * (4 * n), out_specs=[vm] * (3 * n), out_shape=out_shape,
    )(*flat)
    return [tuple(res[3 * p:3 * p + 3]) for p in range(n)]


def _local_step(x, tgt, meta_full, lb_param, attn_norm_w, w_in_g, hgrn_norm_w, conv_w_full, w_out_full,
                ffn_norm_w, w_up_g, fcw_full, ffn_conv_b, w_down_full, final_norm_w):
    seq = x.shape[0]
    T = TR + seq
    bm = 768 if T % 768 == 0 else TR
    h0 = jnp.concatenate([jnp.zeros((PAD, D), F32), meta_full, x], axis=0)
    whn_t = jnp.tile(hgrn_norm_w, (1, NH))

    ut, projb, o, sst, mt, h1 = _mix_block_fwd(h0, lb_param, attn_norm_w, whn_t, conv_w_full, w_in_g, w_out_full)
    u2t, upb, ggt, dh2, loss_vec, dwfin = _ffn_block_fwd(
        h1, tgt, ffn_norm_w, w_up_g, fcw_full, ffn_conv_b, w_down_full, final_norm_w.reshape(1, D))

    dup, dh1, dfw, dfb, dwffn = _ffn_block_bwd(dh2, upb, h1, ffn_norm_w, w_up_g, fcw_full, ffn_conv_b, w_down_full)
    dproj, dlb, dwhn, dcw = _mix_block_bwd(dh1, projb, o, sst, lb_param, whn_t, conv_w_full, w_out_full)
    grad_x, dmeta, dwattn = _input_grad_block(dproj, h0, dh1, attn_norm_w, w_in_g)

    kb = bm
    g_in = _matmul(ut, dproj, mode="nn", bm=D, bn=1152, bk=kb, out_dtype=F32, name="dw_in_mm",
                   out_shards=(NSHARD, 9 * D // NSHARD))
    g_out = _matmul(mt, dh1, mode="nn", bm=D, bn=D, bk=kb, out_dtype=F32, name="dw_out_mm")
    g_up = _matmul(u2t, dup, mode="nn", bm=D, bn=1408, bk=kb, out_dtype=F32, name="dw_up_mm",
                   out_shards=(NSHARD, 2 * DFF // NSHARD))
    g_down = _matmul(ggt, dh2, mode="nn", bm=1408, bn=D, bk=kb, out_dtype=F32, name="dw_down_mm")

    small = dict(dlb=dlb, dwattn=dwattn, dwhn=dwhn, dwffn=dwffn, dfb=dfb, dwfin=dwfin,
                 dcw=dcw, dfw=dfw, dmeta=dmeta, loss=loss_vec)
    big = (g_in, g_out.reshape(NSHARD, D // NSHARD, D), g_up, g_down.reshape(NSHARD, DFF // NSHARD, D))
    return grad_x, small, big


_SMALL_ORDER = ("dmeta", "dcw", "dfw", "dlb", "dwattn", "dwhn", "dwffn", "dfb", "dwfin", "loss")


def kernel(x, meta_tokens, lb_param, attn_norm_w, w_in, hgrn_norm_w, conv_w, w_out, ffn_norm_w, w_up, ffn_conv_w, ffn_conv_b, w_down, final_norm_w, loss_target, m_meta_tokens, m_lb_param, m_attn_norm_w, m_w_in, m_hgrn_norm_w, m_conv_w, m_w_out, m_ffn_norm_w, m_w_up, m_ffn_conv_w, m_ffn_conv_b, m_w_down, m_final_norm_w, v_meta_tokens, v_lb_param, v_attn_norm_w, v_w_in, v_hgrn_norm_w, v_conv_w, v_w_out, v_ffn_norm_w, v_w_up, v_ffn_conv_w, v_ffn_conv_b, v_w_down, v_final_norm_w):
    xi, yi, ci = _place()
    j = 2 * xi + yi
    c_idx = jnp.reshape(ci, (1,)).astype(jnp.int32)

    j_idx = jnp.reshape(j, (1,)).astype(jnp.int32)
    ds_, fs_ = D // NSHARD, DFF // NSHARD
    widen = lambda a: jnp.pad(a, ((0, 0), (0, 768 - a.shape[1])))
    rows_small = jnp.concatenate([widen(meta_tokens), widen(conv_w[0]), widen(ffn_conv_w[0]),
                                  jnp.zeros((2, 768), F32)], axis=0)
    slotted = [_into_slot(w[0], j_idx, rb=rb, dtype=BF16, name=f"slot_{n}")
               for w, rb, n in ((w_in, 256, "w_in"), (w_out, 128, "w_out"), (w_up, 256, "w_up"), (w_down, 176, "w_down"))]
    slotted.append(_into_slot(rows_small, j_idx, rb=rows_small.shape[0], dtype=F32, name="slot_small"))
    w_in_g, w_out_g, w_up_g, w_down_g, small_g = _allgather_weights(slotted)
    unshard = lambda a: jnp.transpose(a, (1, 0, 2)).reshape(a.shape[1], -1)
    meta_full = unshard(small_g[:, 0:NMETA, 0:ds_])
    conv_w_full = unshard(small_g[:, NMETA:NMETA + 3, 0:ds_])
    fcw_full = unshard(small_g[:, NMETA + 3:NMETA + 6, 0:fs_])

    grad_x, small, big = _local_step(
        x[0], loss_target[0], meta_full, lb_param, attn_norm_w, w_in_g, hgrn_norm_w, conv_w_full,
        w_out_g.reshape(D, D), ffn_norm_w, w_up_g, fcw_full, ffn_conv_b, w_down_g.reshape(DFF, D), final_norm_w)

    names = _SMALL_ORDER
    vals = dict(zip(names, _allreduce_small([small[n] for n in names])))
    loss = vals["loss"].reshape(())
    g_small = {
        "meta_tokens": lax.dynamic_slice_in_dim(vals["dmeta"], j * (D // NSHARD), D // NSHARD, axis=1),
        "lb_param": jnp.concatenate([vals["dlb"], -vals["dlb"]], axis=0),
        "attn_norm_w": vals["dwattn"],
        "hgrn_norm_w": vals["dwhn"],
        "conv_w": lax.dynamic_slice_in_dim(vals["dcw"], j * (D // NSHARD), D // NSHARD, axis=1)[None],
        "ffn_norm_w": vals["dwffn"],
        "ffn_conv_w": lax.dynamic_slice_in_dim(vals["dfw"], j * (DFF // NSHARD), DFF // NSHARD, axis=1)[None],
        "ffn_conv_b": vals["dfb"],
        "final_norm_w": vals["dwfin"].reshape(D),
    }

    others = _pair_exchange(big)
    rbs = (256, 128, 256, 176)
    psums = [_pair_add(g, o_, c_idx, rb=rb, name=f"pair_add_{n}") for g, o_, rb, n in zip(big, others, rbs, "0123")]
    parts = _chip_exchange(psums)
    cj_idx = jnp.stack([ci, j]).astype(jnp.int32)
    halves = [_chip_sum(ps, p, cj_idx, rb=rb, name=f"chip_sum_{n}") for ps, p, rb, n in zip(psums, parts, rbs, "0123")]
    g_big = _pair_gather(halves)

    weights = {"meta_tokens": meta_tokens, "lb_param": lb_param, "attn_norm_w": attn_norm_w, "w_in": w_in,
               "hgrn_norm_w": hgrn_norm_w, "conv_w": conv_w, "w_out": w_out, "ffn_norm_w": ffn_norm_w,
               "w_up": w_up, "ffn_conv_w": ffn_conv_w, "ffn_conv_b": ffn_conv_b, "w_down": w_down,
               "final_norm_w": final_norm_w}
    ms = {"meta_tokens": m_meta_tokens, "lb_param": m_lb_param, "attn_norm_w": m_attn_norm_w, "w_in": m_w_in,
          "hgrn_norm_w": m_hgrn_norm_w, "conv_w": m_conv_w, "w_out": m_w_out, "ffn_norm_w": m_ffn_norm_w,
          "w_up": m_w_up, "ffn_conv_w": m_ffn_conv_w, "ffn_conv_b": m_ffn_conv_b, "w_down": m_w_down,
          "final_norm_w": m_final_norm_w}
    vs = {"meta_tokens": v_meta_tokens, "lb_param": v_lb_param, "attn_norm_w": v_attn_norm_w, "w_in": v_w_in,
          "hgrn_norm_w": v_hgrn_norm_w, "conv_w": v_conv_w, "w_out": v_w_out, "ffn_norm_w": v_ffn_norm_w,
          "w_up": v_w_up, "ffn_conv_w": v_ffn_conv_w, "ffn_conv_b": v_ffn_conv_b, "w_down": v_w_down,
          "final_norm_w": v_final_norm_w}
    order = list(weights)
    grads, deltas, new_m, new_v = {}, {}, {}, {}

    for name, g, rb in zip(("w_in", "w_out", "w_up", "w_down"), g_big, (256, 128, 256, 176)):
        shp = weights[name].shape
        w2, m2, v2 = (a.reshape(shp[1], shp[2]) for a in (weights[name], ms[name], vs[name]))
        d_, nm_, nv_ = _adamw(w2, g, m2, v2, rb=rb, name=f"adamw_{name}")
        grads[name], deltas[name], new_m[name], new_v[name] = (a.reshape(shp) for a in (g, d_, nm_, nv_))

    small_names = [n for n in order if n not in grads]
    as2d = lambda a: a.reshape(-1, a.shape[-1])
    res = _adamw_small([tuple(as2d(a) for a in (weights[n], g_small[n], ms[n], vs[n])) for n in small_names])
    for n, (d_, nm_, nv_) in zip(small_names, res):
        shp = weights[n].shape
        grads[n], deltas[n], new_m[n], new_v[n] = (a.reshape(shp) for a in (g_small[n], d_, nm_, nv_))

    return (loss, grad_x[None], *[grads[n] for n in order], *[deltas[n] for n in order],
            *[new_m[n] for n in order], *[new_v[n] for n in order])
```

```python
import functools

import jax
import jax.numpy as jnp
from jax import lax
from jax.experimental import pallas as pl
from jax.experimental.pallas import tpu as pltpu

F32 = jnp.float32
BF16 = jnp.bfloat16
MESH = pl.DeviceIdType.MESH

D = 1024
NH = 8
HD = 128
DFF = 2816
NMETA = 16
EPS = 1e-6
TR = 256
PAD = TR - NMETA
CH = 64
NSHARD = 4
VMEM_LIMIT = 56 * 1024 * 1024

ADAM_LR = 0.001
ADAM_B1 = 0.9
ADAM_B2 = 0.999
ADAM_EPS = 1e-08
ADAM_WD = 0.01
ADAM_STEP = 10


def _cparams(semantics=None, **kw):
    return pltpu.CompilerParams(dimension_semantics=semantics, vmem_limit_bytes=VMEM_LIMIT, **kw)


def _sigmoid(x):
    return 1.0 / (1.0 + jnp.exp(-x))


def _matmul(a, b, *, mode, bm, bn, bk, out_dtype, name, residual=None, out_shards=None):
    if mode == "tn":
        K, M = a.shape
    else:
        M, K = a.shape
    b3 = b.ndim == 3
    if b3:
        S, R, Cs = b.shape
        bcols = S * Cs
        brows = R
    else:
        brows, bcols = b.shape
    if mode == "nt":
        N = brows
        assert bcols == K
    else:
        N = bcols
        assert brows == K
    assert M % bm == 0 and N % bn == 0 and K % bk == 0, (name, M, N, K, bm, bn, bk)
    nm, nn, nk = M // bm, N // bn, K // bk

    if mode == "tn":
        a_spec = pl.BlockSpec((bk, bm), lambda i, j, k: (k, i))
    else:
        a_spec = pl.BlockSpec((bm, bk), lambda i, j, k: (i, k))
    if mode == "nt":
        if b3:
            assert bn == R and Cs % bk == 0
            per = Cs // bk
            b_spec = pl.BlockSpec((None, bn, bk), lambda i, j, k: (k // per, 0, k % per))
        else:
            b_spec = pl.BlockSpec((bn, bk), lambda i, j, k: (j, k))
    else:
        if b3:
            assert bk == R and Cs % bn == 0
            per = Cs // bn
            b_spec = pl.BlockSpec((None, bk, bn), lambda i, j, k: (j // per, 0, j % per))
        else:
            b_spec = pl.BlockSpec((bk, bn), lambda i, j, k: (k, j))
    in_specs = [a_spec, b_spec]
    args = [a, b]
    if residual is not None:
        in_specs.append(pl.BlockSpec((bm, bn), lambda i, j, k: (i, j)))
        args.append(residual)
    if out_shards is not None:
        So, Co = out_shards
        assert So * Co == N and Co % bn == 0 and bm == M
        pero = Co // bn
        out_shape = jax.ShapeDtypeStruct((So, M, Co), out_dtype)
        out_spec = pl.BlockSpec((None, bm, bn), lambda i, j, k: (j // pero, 0, j % pero))
    else:
        out_shape = jax.ShapeDtypeStruct((M, N), out_dtype)
        out_spec = pl.BlockSpec((bm, bn), lambda i, j, k: (i, j))

    assert mode == "nn" and residual is None and out_dtype == F32

    def body(a_ref, b_ref, o_ref):
        @pl.when(pl.program_id(2) == 0)
        def _():
            o_ref[...] = jnp.zeros_like(o_ref)

        o_ref[...] += jnp.dot(a_ref[...].astype(BF16), b_ref[...].astype(BF16), preferred_element_type=F32)

    return pl.pallas_call(
        body, name=name, grid=(nm, nn, nk), in_specs=in_specs, out_specs=out_spec, out_shape=out_shape,
        compiler_params=_cparams(("parallel", "parallel", "arbitrary")),
    )(*args)


def _norm_fwd(h, w, *, bm, name):
    T = h.shape[0]

    def body(h_ref, w_ref, o_ref):
        x = h_ref[...]
        r = lax.rsqrt(jnp.mean(x * x, axis=-1, keepdims=True) + EPS)
        o_ref[...] = (x * r * w_ref[...]).astype(BF16)

    return pl.pallas_call(
        body, name=name, grid=(T // bm,),
        in_specs=[pl.BlockSpec((bm, D), lambda i: (i, 0)), pl.BlockSpec((1, D), lambda i: (0, 0))],
        out_specs=pl.BlockSpec((bm, D), lambda i: (i, 0)),
        out_shape=jax.ShapeDtypeStruct((T, D), BF16), compiler_params=_cparams(("parallel",)),
    )(h, w)


def _norm_bwd(h, du, dres, w, *, bm, name):
    T = h.shape[0]

    def body(h_ref, du_ref, dres_ref, w_ref, dh_ref, dw_ref):
        i = pl.program_id(0)
        x = h_ref[...]
        r = lax.rsqrt(jnp.mean(x * x, axis=-1, keepdims=True) + EPS)
        n = x * r
        du_v = du_ref[...]
        dn = du_v * w_ref[...]
        dh_ref[...] = dres_ref[...] + r * (dn - n * jnp.mean(dn * n, axis=-1, keepdims=True))
        part = jnp.sum(du_v * n, axis=0, keepdims=True)

        @pl.when(i == 0)
        def _():
            dw_ref[...] = part

        @pl.when(i > 0)
        def _():
            dw_ref[...] += part

    row = pl.BlockSpec((bm, D), lambda i: (i, 0))
    vec = pl.BlockSpec((1, D), lambda i: (0, 0))
    return pl.pallas_call(
        body, name=name, grid=(T // bm,), in_specs=[row, row, row, vec], out_specs=[row, vec],
        out_shape=[jax.ShapeDtypeStruct((T, D), F32), jax.ShapeDtypeStruct((1, D), F32)],
        compiler_params=_cparams(("arbitrary",)),
    )(h, du, dres, w)


def _input_grad_block(dproj, h, dres, w, w_in_g, psums):
    T = h.shape[0]
    nt = T // TR
    nsl = w_in_g.shape[0]
    wsl = w_in_g.shape[2]
    wcol = wsl // 2
    nex = len(psums)

    def body(*refs):
        dp_ref, h_ref, dres_ref, w_ref, win_ref = refs[:5]
        gx_ref, dmeta_ref, dw_ref = refs[5 + nex:8 + nex]
        exchange = _ChipExchange(refs[5:5 + nex], refs[8 + nex:8 + 2 * nex], refs[-2], refs[-1])
        i = pl.program_id(0)

        @pl.when(i == 0)
        def _():
            exchange.send()

        @pl.when(i == nt - 1)
        def _():
            exchange.finish()

        x = h_ref[...]
        r = lax.rsqrt(jnp.mean(x * x, axis=-1, keepdims=True) + EPS)
        n = x * r
        du_v = None
        for s in range(nsl):
            for hf in range(2):
                part = _dot_nt(dp_ref[:, s * wsl + hf * wcol:s * wsl + (hf + 1) * wcol],
                               win_ref[s, :, hf * wcol:(hf + 1) * wcol])
                du_v = part if du_v is None else du_v + part
        dn = du_v * w_ref[...]
        dh = dres_ref[...] + r * (dn - n * jnp.mean(dn * n, axis=-1, keepdims=True))
        gx_ref[...] = dh
        part = jnp.sum(du_v * n, axis=0, keepdims=True)

        @pl.when(i == 0)
        def _():
            dmeta_ref[...] = dh[PAD:TR, :]
            dw_ref[...] = part

        @pl.when(i > 0)
        def _():
            dw_ref[...] += part

    row = pl.BlockSpec((TR, D), lambda i: (i, 0))
    vec = pl.BlockSpec((1, D), lambda i: (0, 0))
    hbm = pl.BlockSpec(memory_space=pl.ANY)
    return pl.pallas_call(
        body, name="input_grad_block", grid=(nt,),
        in_specs=[pl.BlockSpec((TR, 9 * D), lambda i: (i, 0)), row, row, vec,
                  pl.BlockSpec(memory_space=pltpu.VMEM)] + [hbm] * nex,
        out_specs=[pl.BlockSpec((TR, D), lambda i: (jnp.maximum(i - 1, 0), 0)),
                   pl.BlockSpec((NMETA, D), lambda i: (0, 0)), vec] + [hbm] * nex,
        out_shape=[jax.ShapeDtypeStruct((T - TR, D), F32), jax.ShapeDtypeStruct((NMETA, D), F32),
                   jax.ShapeDtypeStruct((1, D), F32)]
        + [jax.ShapeDtypeStruct((3,) + p.shape[1:], p.dtype) for p in psums],
        scratch_shapes=[pltpu.SemaphoreType.DMA((nex, 3)), pltpu.SemaphoreType.DMA((nex, 3))],
        compiler_params=_cparams(("arbitrary",)),
    )(dproj, h, dres, w, w_in_g, *psums)


def _split3(x):
    hi = x.astype(BF16)
    r1 = x - hi.astype(F32)
    mid = r1.astype(BF16)
    lo = (r1 - mid.astype(F32)).astype(BF16)
    return hi, mid, lo


def _tri_matmul(tri_bf16, x):
    hi, mid, lo = _split3(x)
    out = jnp.dot(tri_bf16, lo, preferred_element_type=F32)
    out = out + jnp.dot(tri_bf16, mid, preferred_element_type=F32)
    return out + jnp.dot(tri_bf16, hi, preferred_element_type=F32)


def _shift_down(x, prev8, n):
    rows = x.shape[0]
    rid = lax.broadcasted_iota(jnp.int32, x.shape, 0)
    y = pltpu.roll(x, n, 0)
    for t in range(n):
        y = jnp.where(rid == t, prev8[8 - n + t:8 - n + t + 1, :], y)
    del rows
    return y


def _shift_up(x, next8, n):
    rows = x.shape[0]
    rid = lax.broadcasted_iota(jnp.int32, x.shape, 0)
    y = pltpu.roll(x, rows - n, 0)
    for t in range(n):
        y = jnp.where(rid == rows - n + t, next8[t:t + 1, :], y)
    return y


def _gates(f_raw, lb):
    sg = _sigmoid(f_raw)
    f = lb + (1.0 - lb) * sg
    return sg, f, jnp.log(f), 1.0 - f


def _lower_bound(lbp_ref):
    return _sigmoid(lbp_ref[0:1, :] - lbp_ref[1:2, :])


def _tri_masks():
    r = lax.broadcasted_iota(jnp.int32, (CH, CH), 0)
    c = lax.broadcasted_iota(jnp.int32, (CH, CH), 1)
    return r >= c, r <= c


def _ones_where(mask):
    return jnp.where(mask, 1.0, 0.0).astype(BF16)


def _dot(a, b):
    return jnp.dot(a.astype(BF16), b.astype(BF16), preferred_element_type=F32)


def _dot_nt(a, b):
    return lax.dot_general(a.astype(BF16), b.astype(BF16), (((1,), (1,)), ((), ())), preferred_element_type=F32)


def _dot_tn(a, b):
    return lax.dot_general(a.astype(BF16), b.astype(BF16), (((0,), (0,)), ((), ())), preferred_element_type=F32)


def _mix_block_fwd(h0, lb_param, wattn, whn, conv_w, w_in_g, w_out, late):
    T = h0.shape[0]
    nt = T // TR
    ncht = TR // CH
    nsl = w_in_g.shape[0]
    wsl = w_in_g.shape[2]
    wcol = wsl // 2

    nlate = len(late)

    def body(*refs):
        h_ref, lbp_ref, wa_ref, whn_ref, cw_ref, win_ref, wout_ref = refs[:7]
        ut_ref, projb_ref, o_ref, sst_ref, mt_ref, h1_ref = refs[7 + nlate:13 + nlate]
        late_refs = refs[13 + nlate:13 + 2 * nlate]
        proj_ref, m_ref, st_ref, cxc_ref, send_sems, recv_sems = refs[13 + 2 * nlate:]
        i = pl.program_id(0)
        gather = _Gather(late_refs, send_sems, recv_sems, (False,) * nlate)

        @pl.when(i == 0)
        def _():
            st_ref[...] = jnp.zeros_like(st_ref)
            cxc_ref[...] = jnp.zeros_like(cxc_ref)
            gather.send()

        @pl.when(i == nt // 2)
        def _():
            gather.relay()

        x = h_ref[...]
        r1 = lax.rsqrt(jnp.mean(x * x, axis=-1, keepdims=True) + EPS)
        u_f = x * r1 * wa_ref[...]
        u = u_f.astype(BF16)
        ut_ref[...] = u_f.T.astype(BF16)
        for s in range(nsl):
            for hf in range(2):
                cols = slice(s * wsl + hf * wcol, s * wsl + (hf + 1) * wcol)
                p = jnp.dot(u, win_ref[s, :, hf * wcol:(hf + 1) * wcol], preferred_element_type=F32)
                proj_ref[:, cols] = p
                projb_ref[:, cols] = p.astype(BF16)

        lb = _lower_bound(lbp_ref)
        lower, _ = _tri_masks()
        ltri = _ones_where(lower)
        whn_v = whn_ref[...]
        w0, w1, w2 = cw_ref[0:1, :], cw_ref[1:2, :], cw_ref[2:3, :]

        def chunk(c, carry):
            rows = pl.ds(pl.multiple_of(c * CH, CH), CH)
            q_raw = proj_ref[rows, 0:D]
            f_raw = proj_ref[rows, D:2 * D]
            v = proj_ref[rows, 2 * D:3 * D]
            q = q_raw * _sigmoid(q_raw)
            _, _, g, k = _gates(f_raw, lb)
            gam = _tri_matmul(ltri, g)
            gam_l = gam[CH - 1:CH, :]
            e_l = jnp.exp(gam_l)
            qt = (q * jnp.exp(gam)).astype(BF16)
            kt = (k * jnp.exp(-gam)).astype(BF16)
            khat = (k * jnp.exp(gam_l - gam)).astype(BF16)
            vb = v.astype(BF16)
            on_parts = []
            for h in range(NH):
                cs = slice(h * HD, (h + 1) * HD)
                st = st_ref[h]
                sst_ref[c, h] = st
                a = jnp.where(lower, _dot_nt(qt[:, cs], kt[:, cs]), 0.0)
                o_h = _dot_nt(qt[:, cs], st) + _dot(a, vb[:, cs])
                st_ref[h] = st * e_l[:, cs] + _dot_tn(vb[:, cs], khat[:, cs])
                o_ref[rows, cs] = o_h
                ro = lax.rsqrt(jnp.mean(o_h * o_h, axis=-1, keepdims=True) + EPS)
                on_parts.append(o_h * ro)
            on = jnp.concatenate(on_parts, axis=1)
            g_out = proj_ref[rows, 3 * D:4 * D]
            y_a = on * whn_v * (g_out * _sigmoid(g_out))
            cx = proj_ref[rows, 5 * D:6 * D] * proj_ref[rows, 6 * D:7 * D]
            prev8 = cxc_ref[...]
            cv = w0 * _shift_down(cx, prev8, 2) + w1 * _shift_down(cx, prev8, 1) + w2 * cx
            cxc_ref[...] = cx[CH - 8:CH, :]
            y_b = proj_ref[rows, 4 * D:5 * D] * cv
            m = _sigmoid(proj_ref[rows, 7 * D:8 * D]) * y_a + _sigmoid(proj_ref[rows, 8 * D:9 * D]) * y_b
            m_ref[rows, :] = m
            return carry

        lax.fori_loop(0, ncht, chunk, 0)
        m_v = m_ref[...]
        h1_ref[...] = x + jnp.dot(m_v.astype(BF16), wout_ref[...], preferred_element_type=F32)
        mt_ref[...] = m_v.T.astype(BF16)

        @pl.when(i == nt - 1)
        def _():
            gather.finish()

    row = lambda w: pl.BlockSpec((TR, w), lambda i: (i, 0))
    col = lambda w: pl.BlockSpec((w, TR), lambda i: (0, i))
    vec = lambda r: pl.BlockSpec((r, D), lambda i: (0, 0))
    vm = pl.BlockSpec(memory_space=pltpu.VMEM)
    hbm = pl.BlockSpec(memory_space=pl.ANY)
    return pl.pallas_call(
        body, name="mix_block_fwd", grid=(nt,),
        in_specs=[row(D), vec(2), vec(1), vec(1), vec(3), vm, vm] + [hbm] * nlate,
        out_specs=[col(D), row(9 * D), row(D), pl.BlockSpec((ncht, NH, HD, HD), lambda i: (i, 0, 0, 0)),
                   col(D), row(D)] + [hbm] * nlate,
        out_shape=[jax.ShapeDtypeStruct((D, T), BF16), jax.ShapeDtypeStruct((T, 9 * D), BF16),
                   jax.ShapeDtypeStruct((T, D), F32), jax.ShapeDtypeStruct((T // CH, NH, HD, HD), F32),
                   jax.ShapeDtypeStruct((D, T), BF16), jax.ShapeDtypeStruct((T, D), F32)]
        + [jax.ShapeDtypeStruct(a.shape, a.dtype) for a in late],
        input_output_aliases={7 + n: 6 + n for n in range(nlate)},
        scratch_shapes=[pltpu.VMEM((TR, 9 * D), F32), pltpu.VMEM((TR, D), F32), pltpu.VMEM((NH, HD, HD), F32),
                        pltpu.VMEM((8, D), F32), pltpu.SemaphoreType.DMA((nlate, 6)),
                        pltpu.SemaphoreType.DMA((nlate, 6))],
        compiler_params=_cparams(("arbitrary",)),
    )(h0, lb_param, wattn, whn, conv_w, w_in_g, w_out, *late)


def _mix_block_bwd(dh1, projb, o, sst, lb_param, whn, conv_w, w_out, psums):
    T = projb.shape[0]
    nt = T // TR
    ncht = TR // CH
    tb16 = TR // 16
    nex = len(psums)

    def body(*refs):
        dh1_ref, proj_ref, pc_ref, px_ref, o_ref, sst_ref, lbp_ref, whn_ref, cw_ref, wout_ref = refs[:10]
        dproj_ref, dlb_ref, dwhn_ref, dcw_ref = refs[10 + nex:14 + nex]
        exchange = _ChipExchange(refs[10:10 + nex], refs[14 + nex:14 + 2 * nex], refs[-2], refs[-1])
        dm_ref, dst_ref, dcvc_ref, acc_lb, acc_hn, acc_cw = refs[14 + 2 * nex:-2]
        s = pl.program_id(0)
        tile = nt - 1 - s

        @pl.when(s == 0)
        def _():
            dst_ref[...] = jnp.zeros_like(dst_ref)
            dcvc_ref[...] = jnp.zeros_like(dcvc_ref)
            acc_lb[...] = jnp.zeros_like(acc_lb)
            acc_hn[...] = jnp.zeros_like(acc_hn)
            acc_cw[...] = jnp.zeros_like(acc_cw)
            exchange.send()

        dm_ref[...] = _dot_nt(dh1_ref[...], wout_ref[...])
        lb = _lower_bound(lbp_ref)
        lower, upper = _tri_masks()
        ltri = _ones_where(lower)
        utri = _ones_where(upper)
        whn_v = whn_ref[...]
        w0, w1, w2 = cw_ref[0:1, :], cw_ref[1:2, :], cw_ref[2:3, :]
        cx_before_tile = jnp.where(tile > 0, (pc_ref[...].astype(F32) * px_ref[...].astype(F32))[8:16, :], 0.0)
        rid = lax.broadcasted_iota(jnp.int32, (CH, D), 0)

        def chunk(cc, carry):
            c = ncht - 1 - cc
            r0 = pl.multiple_of(c * CH, CH)
            rows = pl.ds(r0, CH)
            slab = lambda n: proj_ref[rows, n * D:(n + 1) * D].astype(F32)
            q_raw, f_raw, v, g_out, b_gate, c_gate, x_conv = (slab(n) for n in range(7))
            sa = _sigmoid(slab(7))
            sb = _sigmoid(slab(8))
            dm_v = dm_ref[rows, :]

            sq = _sigmoid(q_raw)
            q = q_raw * sq
            sg, f, g, k = _gates(f_raw, lb)
            gam = _tri_matmul(ltri, g)
            gam_l = gam[CH - 1:CH, :]
            e_l = jnp.exp(gam_l)
            e_g = jnp.exp(gam)
            e_ng = jnp.exp(-gam)
            e_kl = jnp.exp(gam_l - gam)
            qt = q * e_g
            kt = k * e_ng
            khat = k * e_kl
            s_go = _sigmoid(g_out)
            silu_go = g_out * s_go
            cx = c_gate * x_conv
            rprev = pl.ds(pl.multiple_of(jnp.maximum(r0 - 16, 0), 16), 16)
            cx_prev_in = (proj_ref[rprev, 5 * D:6 * D].astype(F32) * proj_ref[rprev, 6 * D:7 * D].astype(F32))[8:16, :]
            prev8 = jnp.where(c > 0, cx_prev_in, cx_before_tile)
            cx_m1 = _shift_down(cx, prev8, 1)
            cx_m2 = _shift_down(cx, prev8, 2)
            cv = w0 * cx_m2 + w1 * cx_m1 + w2 * cx
            y_b = b_gate * cv

            o_v = o_ref[rows, :]
            ro_parts, on_parts = [], []
            for h in range(NH):
                cs = slice(h * HD, (h + 1) * HD)
                o_h = o_v[:, cs]
                ro = lax.rsqrt(jnp.mean(o_h * o_h, axis=-1, keepdims=True) + EPS)
                ro_parts.append(ro)
                on_parts.append(o_h * ro)
            on = jnp.concatenate(on_parts, axis=1)
            y_a = on * whn_v * silu_go

            dy_a = dm_v * sa
            dy_b = dm_v * sb
            dproj_ref[rows, 7 * D:8 * D] = (dm_v * y_a * sa * (1.0 - sa)).astype(BF16)
            dproj_ref[rows, 8 * D:9 * D] = (dm_v * y_b * sb * (1.0 - sb)).astype(BF16)
            dproj_ref[rows, 4 * D:5 * D] = (dy_b * cv).astype(BF16)
            dcv = dy_b * b_gate
            acc_cw[0:1, :] += jnp.sum(dcv * cx_m2, axis=0, keepdims=True)
            acc_cw[1:2, :] += jnp.sum(dcv * cx_m1, axis=0, keepdims=True)
            acc_cw[2:3, :] += jnp.sum(dcv * cx, axis=0, keepdims=True)
            next8 = dcvc_ref[...]
            dcx = w2 * dcv + w1 * _shift_up(dcv, next8, 1) + w0 * _shift_up(dcv, next8, 2)
            dcvc_ref[...] = dcv[0:8, :]
            dproj_ref[rows, 5 * D:6 * D] = (dcx * x_conv).astype(BF16)
            dproj_ref[rows, 6 * D:7 * D] = (dcx * c_gate).astype(BF16)
            don = dy_a * whn_v * silu_go
            dproj_ref[rows, 3 * D:4 * D] = (dy_a * on * whn_v * (s_go * (1.0 + g_out * (1.0 - s_go)))).astype(BF16)
            acc_hn[...] += jnp.sum(dy_a * silu_go * on, axis=0, keepdims=True)

            dq_parts, dk_parts, dv_parts, dgam_parts, ext_parts = [], [], [], [], []
            for h in range(NH):
                cs = slice(h * HD, (h + 1) * HD)
                on_h = on_parts[h]
                don_h = don[:, cs]
                do_h = ro_parts[h] * (don_h - on_h * jnp.mean(don_h * on_h, axis=-1, keepdims=True))
                qt_h, kt_h, khat_h, v_h = qt[:, cs], kt[:, cs], khat[:, cs], v[:, cs]
                st = sst_ref[c, h]
                dstn = dst_ref[h]
                a_t = jnp.where(upper, _dot_nt(kt_h, qt_h), 0.0)
                da = jnp.where(lower, _dot_nt(do_h, v_h), 0.0)
                da_t = jnp.where(upper, _dot_nt(v_h, do_h), 0.0)
                dv_h = _dot(a_t, do_h) + _dot_nt(khat_h, dstn)
                dqt_state = _dot(do_h, st)
                dqt_chunk = _dot(da, kt_h)
                dkt = _dot(da_t, qt_h)
                dkhat = _dot(v_h, dstn)
                dq_h = (dqt_state + dqt_chunk) * e_g[:, cs]
                dk_h = dkt * e_ng[:, cs] + dkhat * e_kl[:, cs]
                khat_dkhat = dkhat * khat_h
                ext = (jnp.sum(khat_dkhat, axis=0, keepdims=True)
                       + e_l[:, cs] * jnp.sum(st * dstn, axis=0, keepdims=True))
                dst_ref[h] = _dot_tn(do_h, qt_h) + dstn * e_l[:, cs]
                dq_parts.append(dq_h)
                dk_parts.append(dk_h)
                dv_parts.append(dv_h)
                qt_seen = qt_h.astype(BF16).astype(F32)
                kt_seen = kt_h.astype(BF16).astype(F32)
                dgam_parts.append(qt_h * dqt_state + qt_seen * dqt_chunk - kt_seen * dkt - khat_dkhat)
                ext_parts.append(ext)
            dq = jnp.concatenate(dq_parts, axis=1)
            dk = jnp.concatenate(dk_parts, axis=1)
            dgam = jnp.concatenate(dgam_parts, axis=1)
            ext = jnp.concatenate(ext_parts, axis=1)
            dgam = dgam + jnp.where(rid == CH - 1, ext, 0.0)
            dg = _tri_matmul(utri, dgam)
            dproj_ref[rows, 0:D] = (dq * (sq * (1.0 + q_raw * (1.0 - sq)))).astype(BF16)
            df = dg / f - dk
            dproj_ref[rows, D:2 * D] = (df * (1.0 - lb) * sg * (1.0 - sg)).astype(BF16)
            dproj_ref[rows, 2 * D:3 * D] = jnp.concatenate(dv_parts, axis=1).astype(BF16)
            real = (tile * TR + r0 + rid) >= PAD
            acc_lb[...] += jnp.sum(jnp.where(real, df * (1.0 - sg), 0.0), axis=0, keepdims=True)
            return carry

        lax.fori_loop(0, ncht, chunk, 0)

        @pl.when(s == nt - 1)
        def _():
            dlb_ref[...] = acc_lb[...] * lb * (1.0 - lb)
            hn = acc_hn[...]
            tot = hn[:, 0:HD]
            for h in range(1, NH):
                tot = tot + hn[:, h * HD:(h + 1) * HD]
            dwhn_ref[...] = tot
            dcw_ref[...] = acc_cw[0:3, :]
            exchange.finish()

    hbm = pl.BlockSpec(memory_space=pl.ANY)
    rev = lambda s: (nt - 1 - s, 0)
    prevc = lambda s: (jnp.maximum((nt - 1 - s) * tb16 - 1, 0), 5)
    prevx = lambda s: (jnp.maximum((nt - 1 - s) * tb16 - 1, 0), 6)
    const = lambda s: (0, 0)
    return pl.pallas_call(
        body, name="mix_block_bwd", grid=(nt,),
        in_specs=[pl.BlockSpec((TR, D), rev),
                  pl.BlockSpec((TR, 9 * D), rev),
                  pl.BlockSpec((16, D), prevc),
                  pl.BlockSpec((16, D), prevx),
                  pl.BlockSpec((TR, D), rev),
                  pl.BlockSpec((ncht, NH, HD, HD), lambda s: (nt - 1 - s, 0, 0, 0)),
                  pl.BlockSpec((2, D), const),
                  pl.BlockSpec((1, D), const),
                  pl.BlockSpec((3, D), const),
                  pl.BlockSpec(memory_space=pltpu.VMEM)] + [hbm] * nex,
        out_specs=[pl.BlockSpec((TR, 9 * D), rev),
                   pl.BlockSpec((1, D), const),
                   pl.BlockSpec((1, HD), const),
                   pl.BlockSpec((3, D), const)] + [hbm] * nex,
        out_shape=[jax.ShapeDtypeStruct((T, 9 * D), BF16), jax.ShapeDtypeStruct((1, D), F32),
                   jax.ShapeDtypeStruct((1, HD), F32), jax.ShapeDtypeStruct((3, D), F32)]
        + [jax.ShapeDtypeStruct((3,) + p.shape[1:], p.dtype) for p in psums],
        scratch_shapes=[pltpu.VMEM((TR, D), F32), pltpu.VMEM((NH, HD, HD), F32), pltpu.VMEM((8, D), F32),
                        pltpu.VMEM((1, D), F32), pltpu.VMEM((1, D), F32), pltpu.VMEM((8, D), F32),
                        pltpu.SemaphoreType.DMA((nex, 3)), pltpu.SemaphoreType.DMA((nex, 3))],
        compiler_params=_cparams(("arbitrary",)),
    )(dh1, projb, projb, projb, o, sst, lb_param, whn, conv_w, w_out, *psums)


def _ffn_fwd(up, fcw, fcb):
    T = up.shape[0]
    nt = T // TR

    def body(up_ref, w_ref, b_ref, gg_ref, carry_ref):
        i = pl.program_id(0)

        @pl.when(i == 0)
        def _():
            carry_ref[...] = jnp.zeros_like(carry_ref)

        w0, w1, w2 = w_ref[0:1, :], w_ref[1:2, :], w_ref[2:3, :]

        def chunk(c, carry):
            rows = pl.ds(pl.multiple_of(c * CH, CH), CH)
            a_pre = up_ref[rows, 0:DFF]
            val = up_ref[rows, DFF:2 * DFF]
            prev8 = carry_ref[...]
            a = w0 * _shift_down(a_pre, prev8, 2) + w1 * _shift_down(a_pre, prev8, 1) + w2 * a_pre + b_ref[...]
            carry_ref[...] = a_pre[CH - 8:CH, :]
            gg_ref[rows, :] = (a * _sigmoid(a) * val).astype(BF16)
            return carry

        lax.fori_loop(0, TR // CH, chunk, 0)

    return pl.pallas_call(
        body, name="ffn_fwd", grid=(nt,),
        in_specs=[pl.BlockSpec((TR, 2 * DFF), lambda i: (i, 0)),
                  pl.BlockSpec((3, DFF), lambda i: (0, 0)),
                  pl.BlockSpec((1, DFF), lambda i: (0, 0))],
        out_specs=pl.BlockSpec((TR, DFF), lambda i: (i, 0)),
        out_shape=jax.ShapeDtypeStruct((T, DFF), BF16),
        scratch_shapes=[pltpu.VMEM((8, DFF), F32)],
        compiler_params=_cparams(("arbitrary",)),
    )(up, fcw, fcb)


def _ffn_bwd(up, dgg, fcw, fcb):
    T = up.shape[0]
    nt = T // TR
    tb = TR // 8
    ncht = TR // CH

    def body(up_ref, pa_ref, dgg_ref, w_ref, b_ref, dup_ref, dfw_ref, dfb_ref, carry_ref, acc_w, acc_b):
        s = pl.program_id(0)
        tile = nt - 1 - s

        @pl.when(s == 0)
        def _():
            carry_ref[...] = jnp.zeros_like(carry_ref)
            acc_w[...] = jnp.zeros_like(acc_w)
            acc_b[...] = jnp.zeros_like(acc_b)

        w0, w1, w2 = w_ref[0:1, :], w_ref[1:2, :], w_ref[2:3, :]
        a_before_tile = jnp.where(tile > 0, pa_ref[...], 0.0)

        def chunk(cc, carry):
            c = ncht - 1 - cc
            r0 = pl.multiple_of(c * CH, CH)
            rows = pl.ds(r0, CH)
            a_pre = up_ref[rows, 0:DFF]
            val = up_ref[rows, DFF:2 * DFF]
            rprev = pl.multiple_of(jnp.maximum(r0 - 8, 0), 8)
            prev8 = jnp.where(c > 0, up_ref[pl.ds(rprev, 8), 0:DFF], a_before_tile)
            a_m1 = _shift_down(a_pre, prev8, 1)
            a_m2 = _shift_down(a_pre, prev8, 2)
            a = w0 * a_m2 + w1 * a_m1 + w2 * a_pre + b_ref[...]
            sig = _sigmoid(a)
            dgg_v = dgg_ref[rows, :]
            da = dgg_v * val * (sig * (1.0 + a * (1.0 - sig)))
            dup_ref[rows, DFF:2 * DFF] = (dgg_v * (a * sig)).astype(BF16)
            next8 = carry_ref[...]
            dup_ref[rows, 0:DFF] = (w2 * da + w1 * _shift_up(da, next8, 1) + w0 * _shift_up(da, next8, 2)).astype(BF16)
            carry_ref[...] = da[0:8, :]
            acc_w[0:1, :] += jnp.sum(da * a_m2, axis=0, keepdims=True)
            acc_w[1:2, :] += jnp.sum(da * a_m1, axis=0, keepdims=True)
            acc_w[2:3, :] += jnp.sum(da * a_pre, axis=0, keepdims=True)
            acc_b[...] += jnp.sum(da, axis=0, keepdims=True)
            return carry

        lax.fori_loop(0, ncht, chunk, 0)

        @pl.when(s == nt - 1)
        def _():
            dfw_ref[...] = acc_w[0:3, :]
            dfb_ref[...] = acc_b[...]

    rev = lambda s: (nt - 1 - s, 0)
    const = lambda s: (0, 0)
    return pl.pallas_call(
        body, name="ffn_bwd", grid=(nt,),
        in_specs=[pl.BlockSpec((TR, 2 * DFF), rev),
                  pl.BlockSpec((8, DFF), lambda s: (jnp.maximum((nt - 1 - s) * tb - 1, 0), 0)),
                  pl.BlockSpec((TR, DFF), rev),
                  pl.BlockSpec((3, DFF), const),
                  pl.BlockSpec((1, DFF), const)],
        out_specs=[pl.BlockSpec((TR, 2 * DFF), rev),
                   pl.BlockSpec((3, DFF), const),
                   pl.BlockSpec((1, DFF), const)],
        out_shape=[jax.ShapeDtypeStruct((T, 2 * DFF), BF16), jax.ShapeDtypeStruct((3, DFF), F32),
                   jax.ShapeDtypeStruct((1, DFF), F32)],
        scratch_shapes=[pltpu.VMEM((8, DFF), F32), pltpu.VMEM((8, DFF), F32), pltpu.VMEM((1, DFF), F32)],
        compiler_params=_cparams(("arbitrary",)),
    )(up, up, dgg, fcw, fcb)


def _loss_head(h2, tgt, wfin):
    T = h2.shape[0]
    nt = T // TR

    def body(h_ref, t_ref, w_ref, dh_ref, loss_ref, dw_ref):
        i = pl.program_id(0)

        @pl.when(i == 0)
        def _():
            loss_ref[...] = jnp.zeros_like(loss_ref)
            dw_ref[...] = jnp.zeros_like(dw_ref)

        x = h_ref[...]
        r = lax.rsqrt(jnp.mean(x * x, axis=-1, keepdims=True) + EPS)
        n = x * r
        w = w_ref[...]
        diff = jnp.where(i > 0, n * w - t_ref[...], 0.0)
        loss_ref[...] += jnp.sum(diff * diff, axis=0, keepdims=True) * (0.5 / D)
        dy = diff * (1.0 / D)
        dw_ref[...] += jnp.sum(dy * n, axis=0, keepdims=True)
        dn = dy * w
        dh_ref[...] = r * (dn - n * jnp.mean(dn * n, axis=-1, keepdims=True))

    row = pl.BlockSpec((TR, D), lambda i: (i, 0))
    vec = pl.BlockSpec((1, D), lambda i: (0, 0))
    return pl.pallas_call(
        body, name="loss_head", grid=(nt,),
        in_specs=[row, pl.BlockSpec((TR, D), lambda i: (jnp.maximum(i - 1, 0), 0)), vec],
        out_specs=[row, vec, vec],
        out_shape=[jax.ShapeDtypeStruct((T, D), F32), jax.ShapeDtypeStruct((1, D), F32),
                   jax.ShapeDtypeStruct((1, D), F32)],
        compiler_params=_cparams(("arbitrary",)),
    )(h2, tgt, wfin)


def _ffn_block_fwd(h1, tgt, wffn, w_up_g, fcw, fcb, w_down, wfin):
    T = h1.shape[0]
    nt = T // TR
    nsl = w_up_g.shape[0]
    wsl = w_up_g.shape[2]

    def body(h_ref, t_ref, wn_ref, wup_ref, cw_ref, cb_ref, wdn_ref, wf_ref,
             u2t_ref, upb_ref, ggt_ref, dh_ref, loss_ref, dwf_ref, up_scr, gg_ref, carry_ref):
        i = pl.program_id(0)

        @pl.when(i == 0)
        def _():
            carry_ref[...] = jnp.zeros_like(carry_ref)
            loss_ref[...] = jnp.zeros_like(loss_ref)
            dwf_ref[...] = jnp.zeros_like(dwf_ref)

        x = h_ref[...]
        r2 = lax.rsqrt(jnp.mean(x * x, axis=-1, keepdims=True) + EPS)
        u2_f = x * r2 * wn_ref[...]
        u2 = u2_f.astype(BF16)
        u2t_ref[...] = u2_f.T.astype(BF16)
        for s in range(nsl):
            up_s = jnp.dot(u2, wup_ref[s], preferred_element_type=F32)
            up_scr[:, s * wsl:(s + 1) * wsl] = up_s
            upb_ref[:, s * wsl:(s + 1) * wsl] = up_s.astype(BF16)
        w0, w1, w2 = cw_ref[0:1, :], cw_ref[1:2, :], cw_ref[2:3, :]

        def chunk(c, carry):
            rows = pl.ds(pl.multiple_of(c * CH, CH), CH)
            a_pre = up_scr[rows, 0:DFF]
            val = up_scr[rows, DFF:2 * DFF]
            prev8 = carry_ref[...]
            a = w0 * _shift_down(a_pre, prev8, 2) + w1 * _shift_down(a_pre, prev8, 1) + w2 * a_pre + cb_ref[...]
            carry_ref[...] = a_pre[CH - 8:CH, :]
            gg_ref[rows, :] = a * _sigmoid(a) * val
            return carry

        lax.fori_loop(0, TR // CH, chunk, 0)
        gg_v = gg_ref[...]
        ggt_ref[...] = gg_v.T.astype(BF16)
        h2 = x + jnp.dot(gg_v.astype(BF16), wdn_ref[...], preferred_element_type=F32)
        r3 = lax.rsqrt(jnp.mean(h2 * h2, axis=-1, keepdims=True) + EPS)
        n3 = h2 * r3
        wf = wf_ref[...]
        diff = jnp.where(i > 0, n3 * wf - t_ref[...], 0.0)
        loss_ref[...] += jnp.sum(diff * diff, axis=0, keepdims=True) * (0.5 / D)
        dy = diff * (1.0 / D)
        dwf_ref[...] += jnp.sum(dy * n3, axis=0, keepdims=True)
        dn = dy * wf
        dh_ref[...] = r3 * (dn - n3 * jnp.mean(dn * n3, axis=-1, keepdims=True))

    row = lambda w: pl.BlockSpec((TR, w), lambda i: (i, 0))
    col = lambda w: pl.BlockSpec((w, TR), lambda i: (0, i))
    vec = lambda w, r=1: pl.BlockSpec((r, w), lambda i: (0, 0))
    vm = pl.BlockSpec(memory_space=pltpu.VMEM)
    return pl.pallas_call(
        body, name="ffn_block_fwd", grid=(nt,),
        in_specs=[row(D), pl.BlockSpec((TR, D), lambda i: (jnp.maximum(i - 1, 0), 0)), vec(D), vm,
                  vec(DFF, 3), vec(DFF), vm, vec(D)],
        out_specs=[col(D), row(2 * DFF), col(DFF), row(D), vec(D), vec(D)],
        out_shape=[jax.ShapeDtypeStruct((D, T), BF16), jax.ShapeDtypeStruct((T, 2 * DFF), BF16),
                   jax.ShapeDtypeStruct((DFF, T), BF16), jax.ShapeDtypeStruct((T, D), F32),
                   jax.ShapeDtypeStruct((1, D), F32), jax.ShapeDtypeStruct((1, D), F32)],
        scratch_shapes=[pltpu.VMEM((TR, 2 * DFF), F32), pltpu.VMEM((TR, DFF), F32), pltpu.VMEM((8, DFF), F32)],
        compiler_params=_cparams(("arbitrary",)),
    )(h1, tgt, wffn, w_up_g, fcw, fcb, w_down, wfin)


def _ffn_block_bwd(dh2, upb, h1, wffn, w_up_g, fcw, fcb, w_down):
    T = h1.shape[0]
    nt = T // TR
    ncht = TR // CH
    nsl = w_up_g.shape[0]
    wsl = w_up_g.shape[2]
    tb16 = TR // 16

    def body(dh2_ref, up_ref, pa_ref, h_ref, wn_ref, wup_ref, cw_ref, cb_ref, wdn_ref,
             dup_ref, dh1_ref, dfw_ref, dfb_ref, dwn_ref, dgg_scr, carry_ref, acc_w, acc_b, acc_n):
        s = pl.program_id(0)
        tile = nt - 1 - s

        @pl.when(s == 0)
        def _():
            carry_ref[...] = jnp.zeros_like(carry_ref)
            acc_w[...] = jnp.zeros_like(acc_w)
            acc_b[...] = jnp.zeros_like(acc_b)
            acc_n[...] = jnp.zeros_like(acc_n)

        dh2 = dh2_ref[...]
        dgg_scr[...] = _dot_nt(dh2, wdn_ref[...])
        w0, w1, w2 = cw_ref[0:1, :], cw_ref[1:2, :], cw_ref[2:3, :]
        a_before_tile = jnp.where(tile > 0, pa_ref[...].astype(F32)[8:16, :], 0.0)

        def chunk(cc, carry):
            c = ncht - 1 - cc
            r0 = pl.multiple_of(c * CH, CH)
            rows = pl.ds(r0, CH)
            a_pre = up_ref[rows, 0:DFF].astype(F32)
            val = up_ref[rows, DFF:2 * DFF].astype(F32)
            rprev = pl.multiple_of(jnp.maximum(r0 - 16, 0), 16)
            prev_in = up_ref[pl.ds(rprev, 16), 0:DFF].astype(F32)[8:16, :]
            prev8 = jnp.where(c > 0, prev_in, a_before_tile)
            a_m1 = _shift_down(a_pre, prev8, 1)
            a_m2 = _shift_down(a_pre, prev8, 2)
            a = w0 * a_m2 + w1 * a_m1 + w2 * a_pre + cb_ref[...]
            sig = _sigmoid(a)
            dgg_v = dgg_scr[rows, :]
            da = dgg_v * val * (sig * (1.0 + a * (1.0 - sig)))
            dup_ref[rows, DFF:2 * DFF] = (dgg_v * (a * sig)).astype(BF16)
            next8 = carry_ref[...]
            dup_ref[rows, 0:DFF] = (w2 * da + w1 * _shift_up(da, next8, 1) + w0 * _shift_up(da, next8, 2)).astype(BF16)
            carry_ref[...] = da[0:8, :]
            acc_w[0:1, :] += jnp.sum(da * a_m2, axis=0, keepdims=True)
            acc_w[1:2, :] += jnp.sum(da * a_m1, axis=0, keepdims=True)
            acc_w[2:3, :] += jnp.sum(da * a_pre, axis=0, keepdims=True)
            acc_b[...] += jnp.sum(da, axis=0, keepdims=True)
            return carry

        lax.fori_loop(0, ncht, chunk, 0)
        du2 = _dot_nt(dup_ref[:, 0:wsl], wup_ref[0])
        for sl in range(1, nsl):
            du2 = du2 + _dot_nt(dup_ref[:, sl * wsl:(sl + 1) * wsl], wup_ref[sl])
        x = h_ref[...]
        r2 = lax.rsqrt(jnp.mean(x * x, axis=-1, keepdims=True) + EPS)
        n2 = x * r2
        dn = du2 * wn_ref[...]
        dh1_ref[...] = dh2 + r2 * (dn - n2 * jnp.mean(dn * n2, axis=-1, keepdims=True))
        acc_n[...] += jnp.sum(du2 * n2, axis=0, keepdims=True)

        @pl.when(s == nt - 1)
        def _():
            dfw_ref[...] = acc_w[0:3, :]
            dfb_ref[...] = acc_b[...]
            dwn_ref[...] = acc_n[...]

    rev = lambda w: pl.BlockSpec((TR, w), lambda s: (nt - 1 - s, 0))
    vec = lambda w, r=1: pl.BlockSpec((r, w), lambda s: (0, 0))
    vm = pl.BlockSpec(memory_space=pltpu.VMEM)
    return pl.pallas_call(
        body, name="ffn_block_bwd", grid=(nt,),
        in_specs=[rev(D), rev(2 * DFF),
                  pl.BlockSpec((16, DFF), lambda s: (jnp.maximum((nt - 1 - s) * tb16 - 1, 0), 0)),
                  rev(D), vec(D), vm, vec(DFF, 3), vec(DFF), vm],
        out_specs=[rev(2 * DFF), rev(D), vec(DFF, 3), vec(DFF), vec(D)],
        out_shape=[jax.ShapeDtypeStruct((T, 2 * DFF), BF16), jax.ShapeDtypeStruct((T, D), F32),
                   jax.ShapeDtypeStruct((3, DFF), F32), jax.ShapeDtypeStruct((1, DFF), F32),
                   jax.ShapeDtypeStruct((1, D), F32)],
        scratch_shapes=[pltpu.VMEM((TR, DFF), F32), pltpu.VMEM((8, DFF), F32), pltpu.VMEM((8, DFF), F32),
                        pltpu.VMEM((1, DFF), F32), pltpu.VMEM((1, D), F32)],
        compiler_params=_cparams(("arbitrary",)),
    )(dh2, upb, upb, h1, wffn, w_up_g, fcw, fcb, w_down)


def _place():
    x, y, c = lax.axis_index("x"), lax.axis_index("y"), lax.axis_index("c")
    return x, y, c


_CHIP_FLIPS = ((1, 0), (0, 1), (1, 1))


def _flip(v, bit):
    return 1 - v if bit else v


def _into_slot(w, j_idx, *, rb, dtype, name):
    r, cdim = w.shape

    def body(j_ref, w_ref, out_ref):
        del j_ref
        out_ref[...] = w_ref[...].astype(dtype)

    grid_spec = pltpu.PrefetchScalarGridSpec(
        num_scalar_prefetch=1, grid=(r // rb,),
        in_specs=[pl.BlockSpec((rb, cdim), lambda i, j_ref: (i, 0))],
        out_specs=pl.BlockSpec((None, rb, cdim), lambda i, j_ref: (j_ref[0], i, 0)))
    return pl.pallas_call(
        body, name=name, grid_spec=grid_spec, out_shape=jax.ShapeDtypeStruct((NSHARD, r, cdim), dtype),
        compiler_params=_cparams(("parallel",)),
    )(j_idx, w)


class _Gather:
    def __init__(self, outs, send_sems, recv_sems, whole):
        self.outs, self.send_sems, self.recv_sems, self.whole = outs, send_sems, recv_sems, whole
        self.x, self.y, self.c = _place()
        self.j = 2 * self.x + self.y
        self.sibling = (self.x, self.y, 1 - self.c)

    def _rows(self, w, core):
        r = self.outs[w].shape[1]
        return pl.ds(0, r) if self.whole[w] else pl.ds(core * (r // 2), r // 2)

    def _copy(self, w, slot, core, sem, to):
        piece = self.outs[w].at[slot, self._rows(w, core), :]
        return pltpu.make_async_remote_copy(
            src_ref=piece, dst_ref=piece, send_sem=self.send_sems.at[w, sem], recv_sem=self.recv_sems.at[w, sem],
            device_id=to, device_id_type=MESH)

    def _chips(self):
        for kk, (fx, fy) in enumerate(_CHIP_FLIPS):
            px, py = _flip(self.x, fx), _flip(self.y, fy)
            yield kk, 2 * px + py, (px, py, self.c)

    def send(self):
        for w in range(len(self.outs)):
            for kk, _, to in self._chips():
                self._copy(w, self.j, self.c, kk, to).start()

    def relay(self):
        for w in range(len(self.outs)):
            for kk, jk, _ in self._chips():
                self._copy(w, jk, self.c, kk, self.sibling).wait_recv()
                if not self.whole[w]:
                    self._copy(w, jk, self.c, 3 + kk, self.sibling).start()

    def finish(self):
        for w in range(len(self.outs)):
            for kk, jk, to in self._chips():
                self._copy(w, self.j, self.c, kk, to).wait_send()
                if not self.whole[w]:
                    self._copy(w, jk, 1 - self.c, 3 + kk, self.sibling).wait_recv()
                    self._copy(w, jk, self.c, 3 + kk, self.sibling).wait_send()


def _allgather_weights(slotted, whole):
    n = len(slotted)

    def body(*refs):
        g = _Gather(refs[n:2 * n], refs[2 * n], refs[2 * n + 1], whole)
        g.send()
        g.relay()
        g.finish()

    any_spec = pl.BlockSpec(memory_space=pl.ANY)
    return pl.pallas_call(
        body, name="allgather_weights",
        in_specs=[any_spec] * n, out_specs=[any_spec] * n,
        out_shape=[jax.ShapeDtypeStruct(a.shape, a.dtype) for a in slotted],
        input_output_aliases={i: i for i in range(n)},
        scratch_shapes=[pltpu.SemaphoreType.DMA((n, 6)), pltpu.SemaphoreType.DMA((n, 6))],
    )(*slotted)


class _ChipExchange:
    def __init__(self, ins, outs, send_sems, recv_sems):
        self.ins, self.outs, self.send_sems, self.recv_sems = ins, outs, send_sems, recv_sems
        self.x, self.y, self.c = _place()

    def _copies(self):
        for w in range(len(self.ins)):
            for kk, (fx, fy) in enumerate(_CHIP_FLIPS):
                px, py = _flip(self.x, fx), _flip(self.y, fy)
                yield pltpu.make_async_remote_copy(
                    src_ref=self.ins[w].at[2 * px + py], dst_ref=self.outs[w].at[kk],
                    send_sem=self.send_sems.at[w, kk], recv_sem=self.recv_sems.at[w, kk],
                    device_id=(px, py, self.c), device_id_type=MESH)

    def send(self):
        for cp in self._copies():
            cp.start()

    def finish(self):
        for cp in self._copies():
            cp.wait()


def _pair_exchange(grads, name):
    nw = len(grads)

    def body(*refs):
        ins, outs = refs[:nw], refs[nw:2 * nw]
        send_sems, recv_sems = refs[2 * nw:]
        x, y, c = _place()
        sibling = (x, y, 1 - c)
        cps = []
        for w in range(nw):
            half = ins[w].shape[1] // 2
            cp = pltpu.make_async_remote_copy(
                src_ref=ins[w].at[:, pl.ds((1 - c) * half, half), :], dst_ref=outs[w],
                send_sem=send_sems.at[w], recv_sem=recv_sems.at[w], device_id=sibling, device_id_type=MESH)
            cp.start()
            cps.append(cp)
        for cp in cps:
            cp.wait()

    any_spec = pl.BlockSpec(memory_space=pl.ANY)
    return pl.pallas_call(
        body, name=name,
        in_specs=[any_spec] * nw, out_specs=[any_spec] * nw,
        out_shape=[jax.ShapeDtypeStruct((g.shape[0], g.shape[1] // 2, g.shape[2]), g.dtype) for g in grads],
        scratch_shapes=[pltpu.SemaphoreType.DMA((nw,)), pltpu.SemaphoreType.DMA((nw,))],
    )(*grads)


def _pair_add(g, other, c_idx, *, rb, name):
    S, r, cdim = g.shape
    half = r // 2
    nb = half // rb

    def body(c_ref, g_ref, o_ref, out_ref):
        del c_ref
        out_ref[...] = (g_ref[...] + o_ref[...]).astype(BF16)

    grid_spec = pltpu.PrefetchScalarGridSpec(
        num_scalar_prefetch=1, grid=(S, nb),
        in_specs=[pl.BlockSpec((None, rb, cdim), lambda s, i, c_ref: (s, c_ref[0] * nb + i, 0)),
                  pl.BlockSpec((None, rb, cdim), lambda s, i, c_ref: (s, i, 0))],
        out_specs=pl.BlockSpec((None, rb, cdim), lambda s, i, c_ref: (s, i, 0)))
    return pl.pallas_call(
        body, name=name, grid_spec=grid_spec, out_shape=jax.ShapeDtypeStruct((S, half, cdim), BF16),
        compiler_params=_cparams(("parallel", "parallel")),
    )(c_idx, g, other)


def _chip_exchange(psums):
    nw = len(psums)

    def body(*refs):
        ex = _ChipExchange(refs[:nw], refs[nw:2 * nw], refs[2 * nw], refs[2 * nw + 1])
        ex.send()
        ex.finish()

    any_spec = pl.BlockSpec(memory_space=pl.ANY)
    return pl.pallas_call(
        body, name="grad_chip_exchange",
        in_specs=[any_spec] * nw, out_specs=[any_spec] * nw,
        out_shape=[jax.ShapeDtypeStruct((3,) + p.shape[1:], p.dtype) for p in psums],
        scratch_shapes=[pltpu.SemaphoreType.DMA((nw, 3)), pltpu.SemaphoreType.DMA((nw, 3))],
    )(*psums)


def _chip_sum(psum, parts, cj_idx, *, rb, name):
    S, half, cdim = psum.shape
    nb = half // rb

    def body(cj_ref, own_ref, p_ref, out_ref):
        del cj_ref
        f = lambda v: v.astype(F32)
        out_ref[...] = ((f(own_ref[...]) + f(p_ref[0])) + f(p_ref[1])) + f(p_ref[2])

    grid_spec = pltpu.PrefetchScalarGridSpec(
        num_scalar_prefetch=1, grid=(nb,),
        in_specs=[pl.BlockSpec((None, rb, cdim), lambda i, cj: (cj[1], i, 0)),
                  pl.BlockSpec((3, rb, cdim), lambda i, cj: (0, i, 0))],
        out_specs=pl.BlockSpec((rb, cdim), lambda i, cj: (cj[0] * nb + i, 0)))
    return pl.pallas_call(
        body, name=name, grid_spec=grid_spec, out_shape=jax.ShapeDtypeStruct((2 * half, cdim), F32),
        compiler_params=_cparams(("parallel",)),
    )(cj_idx, psum, parts)


def _pair_gather(grads):
    nw = len(grads)

    def body(*refs):
        outs = refs[nw:2 * nw]
        send_sems, recv_sems = refs[2 * nw:]
        x, y, c = _place()
        sibling = (x, y, 1 - c)
        cps = []
        for w in range(nw):
            half = outs[w].shape[0] // 2
            mine = outs[w].at[pl.ds(c * half, half), :]
            cp = pltpu.make_async_remote_copy(
                src_ref=mine, dst_ref=mine, send_sem=send_sems.at[w], recv_sem=recv_sems.at[w],
                device_id=sibling, device_id_type=MESH)
            cp.start()
            cps.append(cp)
        for w in range(nw):
            half = outs[w].shape[0] // 2
            theirs = outs[w].at[pl.ds((1 - c) * half, half), :]
            pltpu.make_async_remote_copy(
                src_ref=theirs, dst_ref=theirs, send_sem=send_sems.at[w], recv_sem=recv_sems.at[w],
                device_id=sibling, device_id_type=MESH).wait_recv()
        for cp in cps:
            cp.wait_send()

    any_spec = pl.BlockSpec(memory_space=pl.ANY)
    return pl.pallas_call(
        body, name="grad_pair_gather",
        in_specs=[any_spec] * nw, out_specs=[any_spec] * nw,
        out_shape=[jax.ShapeDtypeStruct(g.shape, g.dtype) for g in grads],
        input_output_aliases={i: i for i in range(nw)},
        scratch_shapes=[pltpu.SemaphoreType.DMA((nw,)), pltpu.SemaphoreType.DMA((nw,))],
    )(*grads)


def _allreduce_small(pieces):
    n = len(pieces)
    width = max(p.shape[1] for p in pieces)
    starts, at = [], 0
    for p in pieces:
        if p.shape[0] > 1:
            at = -(-at // 8) * 8
        starts.append(at)
        at += p.shape[0]
    rows = -(-at // 8) * 8
    flips = [(fx, fy, fc) for fx in (0, 1) for fy in (0, 1) for fc in (0, 1)][1:]

    def body(*refs):
        ins, outs = refs[:n], refs[n:2 * n]
        mine_ref, slots_ref, send_sems, recv_sems = refs[2 * n:]
        x, y, c = _place()
        me = 4 * x + 2 * y + c
        mine_ref[...] = jnp.zeros_like(mine_ref)
        for p_ref, st in zip(ins, starts):
            r, wd = p_ref.shape
            mine_ref[st:st + r, 0:wd] = p_ref[...]
        slots_ref[me] = mine_ref[...]
        cps = []
        for kk, (fx, fy, fc) in enumerate(flips):
            cp = pltpu.make_async_remote_copy(
                src_ref=mine_ref, dst_ref=slots_ref.at[me], send_sem=send_sems.at[kk], recv_sem=recv_sems.at[kk],
                device_id=(_flip(x, fx), _flip(y, fy), _flip(c, fc)), device_id_type=MESH)
            cp.start()
            cps.append(cp)
        for cp in cps:
            cp.wait()
        tot = slots_ref[0]
        for d in range(1, 8):
            tot = tot + slots_ref[d]
        mine_ref[...] = tot
        for idx, (o_ref, p_ref, st) in enumerate(zip(outs, ins, starts)):
            r, wd = p_ref.shape
            val = mine_ref[st:st + r, 0:wd]
            o_ref[...] = jnp.sum(val, keepdims=True) if idx == n - 1 else val

    vm = pl.BlockSpec(memory_space=pltpu.VMEM)
    out_shape = [jax.ShapeDtypeStruct(p.shape, F32) for p in pieces[:-1]] + [jax.ShapeDtypeStruct((1, 1), F32)]
    return pl.pallas_call(
        body, name="allreduce_small", in_specs=[vm] * n, out_specs=[vm] * n, out_shape=out_shape,
        scratch_shapes=[pltpu.VMEM((rows, width), F32), pltpu.VMEM((8, rows, width), F32),
                        pltpu.SemaphoreType.DMA((7,)), pltpu.SemaphoreType.DMA((7,))],
    )(*pieces)


def _adamw(w, g, m, v, *, rb, name):
    r, cdim = w.shape

    def body(w_ref, g_ref, m_ref, v_ref, d_ref, nm_ref, nv_ref):
        d_ref[...], nm_ref[...], nv_ref[...] = _adamw_update(w_ref[...], g_ref[...], m_ref[...], v_ref[...])

    spec = pl.BlockSpec((rb, cdim), lambda i: (i, 0))
    shp = jax.ShapeDtypeStruct((r, cdim), F32)
    return pl.pallas_call(
        body, name=name, grid=(r // rb,), in_specs=[spec] * 4, out_specs=[spec] * 3, out_shape=[shp] * 3,
        compiler_params=_cparams(("parallel",)),
    )(w, g, m, v)


def _adamw_update(w, g, m, v):
    nm = ADAM_B1 * m + (1.0 - ADAM_B1) * g
    nv = ADAM_B2 * v + (1.0 - ADAM_B2) * (g * g)
    m_hat = nm / (1.0 - ADAM_B1 ** ADAM_STEP)
    v_hat = nv / (1.0 - ADAM_B2 ** ADAM_STEP)
    return -ADAM_LR * (m_hat / (jnp.sqrt(v_hat) + ADAM_EPS) + ADAM_WD * w), nm, nv


def _adamw_small(params):
    n = len(params)

    def body(*refs):
        ins, outs = refs[:4 * n], refs[4 * n:]
        for p in range(n):
            w_ref, g_ref, m_ref, v_ref = ins[4 * p:4 * p + 4]
            d, nm, nv = _adamw_update(w_ref[...], g_ref[...], m_ref[...], v_ref[...])
            outs[3 * p][...] = d
            outs[3 * p + 1][...] = nm
            outs[3 * p + 2][...] = nv

    vm = pl.BlockSpec(memory_space=pltpu.VMEM)
    flat = [a for p in params for a in p]
    out_shape = [jax.ShapeDtypeStruct(p[0].shape, F32) for p in params for _ in range(3)]
    res = pl.pallas_call(
        body, name="adamw_small", in_specs=[vm] * (4 * n), out_specs=[vm] * (3 * n), out_shape=out_shape,
    )(*flat)
    return [tuple(res[3 * p:3 * p + 3]) for p in range(n)]


_PAIR_ADD_ROWS = {"w_in": 256, "w_out": 128, "w_up": 256, "w_down": 176}


def _pair_sums(grads, names, c_idx, tag):
    others = _pair_exchange(grads, name=f"grad_pair_exchange_{tag}")
    return [_pair_add(g, o_, c_idx, rb=_PAIR_ADD_ROWS[n], name=f"pair_add_{n}") for g, o_, n in zip(grads, others, names)]


def _local_step(x, tgt, meta_full, lb_param, attn_norm_w, w_in_g, hgrn_norm_w, conv_w_full, w_out_full,
                ffn_norm_w, late_slotted, fcw_full, ffn_conv_b, final_norm_w, c_idx, cj_idx):
    seq = x.shape[0]
    T = TR + seq
    h0 = jnp.concatenate([jnp.zeros((PAD, D), F32), meta_full, x], axis=0)
    whn_t = jnp.tile(hgrn_norm_w, (1, NH))

    ut, projb, o, sst, mt, h1, w_up_g, w_down_g = _mix_block_fwd(
        h0, lb_param, attn_norm_w, whn_t, conv_w_full, w_in_g, w_out_full, late_slotted)
    w_down_full = w_down_g.reshape(DFF, D)
    u2t, upb, ggt, dh2, loss_vec, dwfin = _ffn_block_fwd(
        h1, tgt, ffn_norm_w, w_up_g, fcw_full, ffn_conv_b, w_down_full, final_norm_w.reshape(1, D))

    dup, dh1, dfw, dfb, dwffn = _ffn_block_bwd(dh2, upb, h1, ffn_norm_w, w_up_g, fcw_full, ffn_conv_b, w_down_full)
    kb = 1408 if T % 1408 == 0 else TR
    g_up = _matmul(u2t, dup, mode="nn", bm=D, bn=1408, bk=kb, out_dtype=F32, name="dw_up_mm",
                   out_shards=(NSHARD, 2 * DFF // NSHARD))
    g_down = _matmul(ggt, dh2, mode="nn", bm=1408, bn=D, bk=kb, out_dtype=F32, name="dw_down_mm")
    ps_ffn = _pair_sums([g_up, g_down.reshape(NSHARD, DFF // NSHARD, D)], ("w_up", "w_down"), c_idx, "ffn")

    dproj, dlb, dwhn, dcw, *parts_ffn = _mix_block_bwd(
        dh1, projb, o, sst, lb_param, whn_t, conv_w_full, w_out_full, ps_ffn)
    g_in = _matmul(ut, dproj, mode="nn", bm=D, bn=1152, bk=kb, out_dtype=F32, name="dw_in_mm",
                   out_shards=(NSHARD, 9 * D // NSHARD))
    g_out = _matmul(mt, dh1, mode="nn", bm=D, bn=D, bk=kb, out_dtype=F32, name="dw_out_mm")
    ps_mix = _pair_sums([g_in, g_out.reshape(NSHARD, D // NSHARD, D)], ("w_in", "w_out"), c_idx, "mix")

    grad_x, dmeta, dwattn, *parts_mix = _input_grad_block(dproj, h0, dh1, attn_norm_w, w_in_g, ps_mix)

    halves = [_chip_sum(ps, p, cj_idx, rb=_PAIR_ADD_ROWS[n], name=f"chip_sum_{n}")
              for ps, p, n in zip(ps_mix + ps_ffn, parts_mix + parts_ffn, ("w_in", "w_out", "w_up", "w_down"))]
    small = dict(dlb=dlb, dwattn=dwattn, dwhn=dwhn, dwffn=dwffn, dfb=dfb, dwfin=dwfin,
                 dcw=dcw, dfw=dfw, dmeta=dmeta, loss=loss_vec)
    return grad_x, small, halves


_SMALL_ORDER = ("dmeta", "dcw", "dfw", "dlb", "dwattn", "dwhn", "dwffn", "dfb", "dwfin", "loss")


def kernel(x, meta_tokens, lb_param, attn_norm_w, w_in, hgrn_norm_w, conv_w, w_out, ffn_norm_w, w_up, ffn_conv_w, ffn_conv_b, w_down, final_norm_w, loss_target, m_meta_tokens, m_lb_param, m_attn_norm_w, m_w_in, m_hgrn_norm_w, m_conv_w, m_w_out, m_ffn_norm_w, m_w_up, m_ffn_conv_w, m_ffn_conv_b, m_w_down, m_final_norm_w, v_meta_tokens, v_lb_param, v_attn_norm_w, v_w_in, v_hgrn_norm_w, v_conv_w, v_w_out, v_ffn_norm_w, v_w_up, v_ffn_conv_w, v_ffn_conv_b, v_w_down, v_final_norm_w):
    xi, yi, ci = _place()
    j = 2 * xi + yi
    c_idx = jnp.reshape(ci, (1,)).astype(jnp.int32)

    j_idx = jnp.reshape(j, (1,)).astype(jnp.int32)
    ds_, fs_ = D // NSHARD, DFF // NSHARD
    widen = lambda a: jnp.pad(a, ((0, 0), (0, 768 - a.shape[1])))
    rows_small = jnp.concatenate([widen(meta_tokens), widen(conv_w[0]), widen(ffn_conv_w[0]),
                                  jnp.zeros((2, 768), F32)], axis=0)
    slotted = [_into_slot(w[0], j_idx, rb=rb, dtype=BF16, name=f"slot_{n}")
               for w, rb, n in ((w_in, 256, "w_in"), (w_out, 128, "w_out"), (w_up, 256, "w_up"), (w_down, 176, "w_down"))]
    s_in, s_out, s_up, s_down = slotted
    s_small = _into_slot(rows_small, j_idx, rb=rows_small.shape[0], dtype=F32, name="slot_small")
    w_in_g, w_out_g, small_g = _allgather_weights([s_in, s_out, s_small], (False, False, True))
    unshard = lambda a: jnp.transpose(a, (1, 0, 2)).reshape(a.shape[1], -1)
    meta_full = unshard(small_g[:, 0:NMETA, 0:ds_])
    conv_w_full = unshard(small_g[:, NMETA:NMETA + 3, 0:ds_])
    fcw_full = unshard(small_g[:, NMETA + 3:NMETA + 6, 0:fs_])

    cj_idx = jnp.stack([ci, j]).astype(jnp.int32)
    grad_x, small, halves = _local_step(
        x[0], loss_target[0], meta_full, lb_param, attn_norm_w, w_in_g, hgrn_norm_w, conv_w_full,
        w_out_g.reshape(D, D), ffn_norm_w, [s_up, s_down], fcw_full, ffn_conv_b, final_norm_w, c_idx, cj_idx)

    names = _SMALL_ORDER
    vals = dict(zip(names, _allreduce_small([small[n] for n in names])))
    loss = vals["loss"].reshape(())
    g_small = {
        "meta_tokens": lax.dynamic_slice_in_dim(vals["dmeta"], j * (D // NSHARD), D // NSHARD, axis=1),
        "lb_param": jnp.concatenate([vals["dlb"], -vals["dlb"]], axis=0),
        "attn_norm_w": vals["dwattn"],
        "hgrn_norm_w": vals["dwhn"],
        "conv_w": lax.dynamic_slice_in_dim(vals["dcw"], j * (D // NSHARD), D // NSHARD, axis=1)[None],
        "ffn_norm_w": vals["dwffn"],
        "ffn_conv_w": lax.dynamic_slice_in_dim(vals["dfw"], j * (DFF // NSHARD), DFF // NSHARD, axis=1)[None],
        "ffn_conv_b": vals["dfb"],
        "final_norm_w": vals["dwfin"].reshape(D),
    }

    g_big = _pair_gather(halves)

    weights = {"meta_tokens": meta_tokens, "lb_param": lb_param, "attn_norm_w": attn_norm_w, "w_in": w_in,
               "hgrn_norm_w": hgrn_norm_w, "conv_w": conv_w, "w_out": w_out, "ffn_norm_w": ffn_norm_w,
               "w_up": w_up, "ffn_conv_w": ffn_conv_w, "ffn_conv_b": ffn_conv_b, "w_down": w_down,
               "final_norm_w": final_norm_w}
    ms = {"meta_tokens": m_meta_tokens, "lb_param": m_lb_param, "attn_norm_w": m_attn_norm_w, "w_in": m_w_in,
          "hgrn_norm_w": m_hgrn_norm_w, "conv_w": m_conv_w, "w_out": m_w_out, "ffn_norm_w": m_ffn_norm_w,
          "w_up": m_w_up, "ffn_conv_w": m_ffn_conv_w, "ffn_conv_b": m_ffn_conv_b, "w_down": m_w_down,
          "final_norm_w": m_final_norm_w}
    vs = {"meta_tokens": v_meta_tokens, "lb_param": v_lb_param, "attn_norm_w": v_attn_norm_w, "w_in": v_w_in,
          "hgrn_norm_w": v_hgrn_norm_w, "conv_w": v_conv_w, "w_out": v_w_out, "ffn_norm_w": v_ffn_norm_w,
          "w_up": v_w_up, "ffn_conv_w": v_ffn_conv_w, "ffn_conv_b": v_ffn_conv_b, "w_down": v_w_down,
          "final_norm_w": v_final_norm_w}
    order = list(weights)
    grads, deltas, new_m, new_v = {}, {}, {}, {}

    for name, g, rb in zip(("w_in", "w_out", "w_up", "w_down"), g_big, (256, 128, 256, 176)):
        shp = weights[name].shape
        w2, m2, v2 = (a.reshape(shp[1], shp[2]) for a in (weights[name], ms[name], vs[name]))
        d_, nm_, nv_ = _adamw(w2, g, m2, v2, rb=rb, name=f"adamw_{name}")
        grads[name], deltas[name], new_m[name], new_v[name] = (a.reshape(shp) for a in (g, d_, nm_, nv_))

    small_names = [n for n in order if n not in grads]
    as2d = lambda a: a.reshape(-1, a.shape[-1])
    res = _adamw_small([tuple(as2d(a) for a in (weights[n], g_small[n], ms[n], vs[n])) for n in small_names])
    for n, (d_, nm_, nv_) in zip(small_names, res):
        shp = weights[n].shape
        grads[n], deltas[n], new_m[n], new_v[n] = (a.reshape(shp) for a in (g_small[n], d_, nm_, nv_))

    return (loss, grad_x[None], *[grads[n] for n in order], *[deltas[n] for n in order],
            *[new_m[n] for n in order], *[new_v[n] for n in order])
```

```python
import functools

import jax
import jax.numpy as jnp
from jax import lax
from jax.experimental import pallas as pl
from jax.experimental.pallas import tpu as pltpu

F32 = jnp.float32
BF16 = jnp.bfloat16
MESH = pl.DeviceIdType.MESH

D = 1024
NH = 8
HD = 128
DFF = 2816
NMETA = 16
EPS = 1e-6
TR = 256
PAD = TR - NMETA
CH = 64
NSHARD = 4
VMEM_LIMIT = 56 * 1024 * 1024

ADAM_LR = 0.001
ADAM_B1 = 0.9
ADAM_B2 = 0.999
ADAM_EPS = 1e-08
ADAM_WD = 0.01
ADAM_STEP = 10


def _cparams(semantics=None, **kw):
    return pltpu.CompilerParams(dimension_semantics=semantics, vmem_limit_bytes=VMEM_LIMIT, **kw)


def _sigmoid(x):
    return 0.5 * jnp.tanh(0.5 * x) + 0.5


def _matmul(a, b, *, mode, bm, bn, bk, out_dtype, name, residual=None, out_shards=None):
    if mode == "tn":
        K, M = a.shape
    else:
        M, K = a.shape
    b3 = b.ndim == 3
    if b3:
        S, R, Cs = b.shape
        bcols = S * Cs
        brows = R
    else:
        brows, bcols = b.shape
    if mode == "nt":
        N = brows
        assert bcols == K
    else:
        N = bcols
        assert brows == K
    assert M % bm == 0 and N % bn == 0 and K % bk == 0, (name, M, N, K, bm, bn, bk)
    nm, nn, nk = M // bm, N // bn, K // bk

    if mode == "tn":
        a_spec = pl.BlockSpec((bk, bm), lambda i, j, k: (k, i))
    else:
        a_spec = pl.BlockSpec((bm, bk), lambda i, j, k: (i, k))
    if mode == "nt":
        if b3:
            assert bn == R and Cs % bk == 0
            per = Cs // bk
            b_spec = pl.BlockSpec((None, bn, bk), lambda i, j, k: (k // per, 0, k % per))
        else:
            b_spec = pl.BlockSpec((bn, bk), lambda i, j, k: (j, k))
    else:
        if b3:
            assert bk == R and Cs % bn == 0
            per = Cs // bn
            b_spec = pl.BlockSpec((None, bk, bn), lambda i, j, k: (j // per, 0, j % per))
        else:
            b_spec = pl.BlockSpec((bk, bn), lambda i, j, k: (k, j))
    in_specs = [a_spec, b_spec]
    args = [a, b]
    if residual is not None:
        in_specs.append(pl.BlockSpec((bm, bn), lambda i, j, k: (i, j)))
        args.append(residual)
    if out_shards is not None:
        So, Co = out_shards
        assert So * Co == N and Co % bn == 0 and bm == M
        pero = Co // bn
        out_shape = jax.ShapeDtypeStruct((So, M, Co), out_dtype)
        out_spec = pl.BlockSpec((None, bm, bn), lambda i, j, k: (j // pero, 0, j % pero))
    else:
        out_shape = jax.ShapeDtypeStruct((M, N), out_dtype)
        out_spec = pl.BlockSpec((bm, bn), lambda i, j, k: (i, j))

    assert mode == "nn" and residual is None and out_dtype == F32

    def body(a_ref, b_ref, o_ref):
        @pl.when(pl.program_id(2) == 0)
        def _():
            o_ref[...] = jnp.zeros_like(o_ref)

        o_ref[...] += jnp.dot(a_ref[...].astype(BF16), b_ref[...].astype(BF16), preferred_element_type=F32)

    return pl.pallas_call(
        body, name=name, grid=(nm, nn, nk), in_specs=in_specs, out_specs=out_spec, out_shape=out_shape,
        compiler_params=_cparams(("parallel", "parallel", "arbitrary")),
    )(*args)


def _norm_fwd(h, w, *, bm, name):
    T = h.shape[0]

    def body(h_ref, w_ref, o_ref):
        x = h_ref[...]
        r = lax.rsqrt(jnp.mean(x * x, axis=-1, keepdims=True) + EPS)
        o_ref[...] = (x * r * w_ref[...]).astype(BF16)

    return pl.pallas_call(
        body, name=name, grid=(T // bm,),
        in_specs=[pl.BlockSpec((bm, D), lambda i: (i, 0)), pl.BlockSpec((1, D), lambda i: (0, 0))],
        out_specs=pl.BlockSpec((bm, D), lambda i: (i, 0)),
        out_shape=jax.ShapeDtypeStruct((T, D), BF16), compiler_params=_cparams(("parallel",)),
    )(h, w)


def _norm_bwd(h, du, dres, w, *, bm, name):
    T = h.shape[0]

    def body(h_ref, du_ref, dres_ref, w_ref, dh_ref, dw_ref):
        i = pl.program_id(0)
        x = h_ref[...]
        r = lax.rsqrt(jnp.mean(x * x, axis=-1, keepdims=True) + EPS)
        n = x * r
        du_v = du_ref[...]
        dn = du_v * w_ref[...]
        dh_ref[...] = dres_ref[...] + r * (dn - n * jnp.mean(dn * n, axis=-1, keepdims=True))
        part = jnp.sum(du_v * n, axis=0, keepdims=True)

        @pl.when(i == 0)
        def _():
            dw_ref[...] = part

        @pl.when(i > 0)
        def _():
            dw_ref[...] += part

    row = pl.BlockSpec((bm, D), lambda i: (i, 0))
    vec = pl.BlockSpec((1, D), lambda i: (0, 0))
    return pl.pallas_call(
        body, name=name, grid=(T // bm,), in_specs=[row, row, row, vec], out_specs=[row, vec],
        out_shape=[jax.ShapeDtypeStruct((T, D), F32), jax.ShapeDtypeStruct((1, D), F32)],
        compiler_params=_cparams(("arbitrary",)),
    )(h, du, dres, w)


def _input_grad_block(dproj, x_seq, head_tile, dres, w, w_in_g, psums):
    T = TR + x_seq.shape[0]
    nt = T // TR
    nsl = w_in_g.shape[0]
    wsl = w_in_g.shape[2]
    wcol = wsl // 2
    nex = len(psums)

    def body(*refs):
        dp_ref, x_ref, head_ref, dres_ref, w_ref, win_ref = refs[:6]
        gx_ref, dmeta_ref, dw_ref = refs[6 + nex:9 + nex]
        exchange = _ChipExchange(refs[6:6 + nex], refs[9 + nex:9 + 2 * nex], refs[-2], refs[-1])
        i = pl.program_id(0)

        @pl.when(i == 0)
        def _():
            exchange.send()

        @pl.when(i == nt - 1)
        def _():
            exchange.finish()

        x = jnp.where(i == 0, head_ref[...], x_ref[...])
        r = lax.rsqrt(jnp.mean(x * x, axis=-1, keepdims=True) + EPS)
        n = x * r
        du_v = None
        for s in range(nsl):
            for hf in range(2):
                part = _dot_nt(dp_ref[:, s * wsl + hf * wcol:s * wsl + (hf + 1) * wcol],
                               win_ref[s, :, hf * wcol:(hf + 1) * wcol])
                du_v = part if du_v is None else du_v + part
        dn = du_v * w_ref[...]
        dh = dres_ref[...] + r * (dn - n * jnp.mean(dn * n, axis=-1, keepdims=True))
        gx_ref[...] = dh
        part = jnp.sum(du_v * n, axis=0, keepdims=True)

        @pl.when(i == 0)
        def _():
            dmeta_ref[...] = dh[PAD:TR, :]
            dw_ref[...] = part

        @pl.when(i > 0)
        def _():
            dw_ref[...] += part

    row = pl.BlockSpec((TR, D), lambda i: (i, 0))
    vec = pl.BlockSpec((1, D), lambda i: (0, 0))
    hbm = pl.BlockSpec(memory_space=pl.ANY)
    return pl.pallas_call(
        body, name="input_grad_block", grid=(nt,),
        in_specs=[pl.BlockSpec((TR, 9 * D), lambda i: (i, 0)),
                  pl.BlockSpec((TR, D), lambda i: (jnp.maximum(i - 1, 0), 0)), pl.BlockSpec((TR, D), lambda i: (0, 0)),
                  row, vec, pl.BlockSpec(memory_space=pltpu.VMEM)] + [hbm] * nex,
        out_specs=[pl.BlockSpec((TR, D), lambda i: (jnp.maximum(i - 1, 0), 0)),
                   pl.BlockSpec((NMETA, D), lambda i: (0, 0)), vec] + [hbm] * nex,
        out_shape=[jax.ShapeDtypeStruct((T - TR, D), F32), jax.ShapeDtypeStruct((NMETA, D), F32),
                   jax.ShapeDtypeStruct((1, D), F32)]
        + [jax.ShapeDtypeStruct((3,) + p.shape[1:], p.dtype) for p in psums],
        scratch_shapes=[pltpu.SemaphoreType.DMA((nex, 3)), pltpu.SemaphoreType.DMA((nex, 3))],
        compiler_params=_cparams(("arbitrary",)),
    )(dproj, x_seq, head_tile, dres, w, w_in_g, *psums)


def _split3(x):
    hi = x.astype(BF16)
    r1 = x - hi.astype(F32)
    mid = r1.astype(BF16)
    lo = (r1 - mid.astype(F32)).astype(BF16)
    return hi, mid, lo


def _tri_matmul(tri_bf16, x):
    hi, mid, lo = _split3(x)
    out = jnp.dot(tri_bf16, lo, preferred_element_type=F32)
    out = out + jnp.dot(tri_bf16, mid, preferred_element_type=F32)
    return out + jnp.dot(tri_bf16, hi, preferred_element_type=F32)


def _shift_down(x, prev8, n):
    rows = x.shape[0]
    rid = lax.broadcasted_iota(jnp.int32, x.shape, 0)
    y = pltpu.roll(x, n, 0)
    for t in range(n):
        y = jnp.where(rid == t, prev8[8 - n + t:8 - n + t + 1, :], y)
    del rows
    return y


def _shift_up(x, next8, n):
    rows = x.shape[0]
    rid = lax.broadcasted_iota(jnp.int32, x.shape, 0)
    y = pltpu.roll(x, rows - n, 0)
    for t in range(n):
        y = jnp.where(rid == rows - n + t, next8[t:t + 1, :], y)
    return y


def _gates(f_raw, lb):
    sg = _sigmoid(f_raw)
    f = lb + (1.0 - lb) * sg
    return sg, f, jnp.log(f), 1.0 - f


def _lower_bound(lbp_ref):
    return _sigmoid(lbp_ref[0:1, :] - lbp_ref[1:2, :])


def _tri_masks():
    r = lax.broadcasted_iota(jnp.int32, (CH, CH), 0)
    c = lax.broadcasted_iota(jnp.int32, (CH, CH), 1)
    return r >= c, r <= c


def _ones_where(mask):
    return jnp.where(mask, 1.0, 0.0).astype(BF16)


def _dot(a, b):
    return jnp.dot(a.astype(BF16), b.astype(BF16), preferred_element_type=F32)


def _dot_nt(a, b):
    return lax.dot_general(a.astype(BF16), b.astype(BF16), (((1,), (1,)), ((), ())), preferred_element_type=F32)


def _dot_tn(a, b):
    return lax.dot_general(a.astype(BF16), b.astype(BF16), (((0,), (0,)), ((), ())), preferred_element_type=F32)


def _mix_block_fwd(x_seq, head_tile, lb_param, wattn, whn, conv_w, w_in_g, w_out, late):
    T = TR + x_seq.shape[0]
    nt = T // TR
    ncht = TR // CH
    nsl = w_in_g.shape[0]
    wsl = w_in_g.shape[2]
    wcol = wsl // 2

    nlate = len(late)

    def body(*refs):
        x_ref, head_ref, lbp_ref, wa_ref, whn_ref, cw_ref, win_ref, wout_ref = refs[:8]
        ut_ref, projb_ref, o_ref, sst_ref, mt_ref, h1_ref = refs[8 + nlate:14 + nlate]
        late_refs = refs[14 + nlate:14 + 2 * nlate]
        proj_ref, m_ref, st_ref, cxc_ref, send_sems, recv_sems = refs[14 + 2 * nlate:]
        i = pl.program_id(0)
        gather = _Gather(late_refs, send_sems, recv_sems, (False,) * nlate)

        @pl.when(i == 0)
        def _():
            st_ref[...] = jnp.zeros_like(st_ref)
            cxc_ref[...] = jnp.zeros_like(cxc_ref)
            gather.send()

        @pl.when(i == nt // 2)
        def _():
            gather.relay()

        x = jnp.where(i == 0, head_ref[...], x_ref[...])
        r1 = lax.rsqrt(jnp.mean(x * x, axis=-1, keepdims=True) + EPS)
        u_f = x * r1 * wa_ref[...]
        u = u_f.astype(BF16)
        ut_ref[...] = u_f.T.astype(BF16)
        for s in range(nsl):
            for hf in range(2):
                cols = slice(s * wsl + hf * wcol, s * wsl + (hf + 1) * wcol)
                p = jnp.dot(u, win_ref[s, :, hf * wcol:(hf + 1) * wcol], preferred_element_type=F32)
                proj_ref[:, cols] = p
                projb_ref[:, cols] = p.astype(BF16)

        lb = _lower_bound(lbp_ref)
        lower, _ = _tri_masks()
        ltri = _ones_where(lower)
        whn_v = whn_ref[...]
        w0, w1, w2 = cw_ref[0:1, :], cw_ref[1:2, :], cw_ref[2:3, :]

        def chunk(c, carry):
            rows = pl.ds(pl.multiple_of(c * CH, CH), CH)
            q_raw = proj_ref[rows, 0:D]
            f_raw = proj_ref[rows, D:2 * D]
            v = proj_ref[rows, 2 * D:3 * D]
            q = q_raw * _sigmoid(q_raw)
            _, _, g, k = _gates(f_raw, lb)
            gam = _tri_matmul(ltri, g)
            gam_l = gam[CH - 1:CH, :]
            e_l = jnp.exp(gam_l)
            qt = (q * jnp.exp(gam)).astype(BF16)
            kt = (k * jnp.exp(-gam)).astype(BF16)
            khat = (k * jnp.exp(gam_l - gam)).astype(BF16)
            vb = v.astype(BF16)
            on_parts = []
            for h in range(NH):
                cs = slice(h * HD, (h + 1) * HD)
                st = st_ref[h]
                sst_ref[c, h] = st
                a = jnp.where(lower, _dot_nt(qt[:, cs], kt[:, cs]), 0.0)
                o_h = _dot_nt(qt[:, cs], st) + _dot(a, vb[:, cs])
                st_ref[h] = st * e_l[:, cs] + _dot_tn(vb[:, cs], khat[:, cs])
                o_ref[rows, cs] = o_h
                ro = lax.rsqrt(jnp.mean(o_h * o_h, axis=-1, keepdims=True) + EPS)
                on_parts.append(o_h * ro)
            on = jnp.concatenate(on_parts, axis=1)
            g_out = proj_ref[rows, 3 * D:4 * D]
            y_a = on * whn_v * (g_out * _sigmoid(g_out))
            cx = proj_ref[rows, 5 * D:6 * D] * proj_ref[rows, 6 * D:7 * D]
            prev8 = cxc_ref[...]
            cv = w0 * _shift_down(cx, prev8, 2) + w1 * _shift_down(cx, prev8, 1) + w2 * cx
            cxc_ref[...] = cx[CH - 8:CH, :]
            y_b = proj_ref[rows, 4 * D:5 * D] * cv
            m = _sigmoid(proj_ref[rows, 7 * D:8 * D]) * y_a + _sigmoid(proj_ref[rows, 8 * D:9 * D]) * y_b
            m_ref[rows, :] = m
            return carry

        lax.fori_loop(0, ncht, chunk, 0)
        m_v = m_ref[...]
        h1_ref[...] = x + jnp.dot(m_v.astype(BF16), wout_ref[...], preferred_element_type=F32)
        mt_ref[...] = m_v.T.astype(BF16)

        @pl.when(i == nt - 1)
        def _():
            gather.finish()

    row = lambda w: pl.BlockSpec((TR, w), lambda i: (i, 0))
    col = lambda w: pl.BlockSpec((w, TR), lambda i: (0, i))
    vec = lambda r: pl.BlockSpec((r, D), lambda i: (0, 0))
    vm = pl.BlockSpec(memory_space=pltpu.VMEM)
    hbm = pl.BlockSpec(memory_space=pl.ANY)
    return pl.pallas_call(
        body, name="mix_block_fwd", grid=(nt,),
        in_specs=[pl.BlockSpec((TR, D), lambda i: (jnp.maximum(i - 1, 0), 0)), pl.BlockSpec((TR, D), lambda i: (0, 0)),
                  vec(2), vec(1), vec(1), vec(3), vm, vm] + [hbm] * nlate,
        out_specs=[col(D), row(9 * D), row(D), pl.BlockSpec((ncht, NH, HD, HD), lambda i: (i, 0, 0, 0)),
                   col(D), row(D)] + [hbm] * nlate,
        out_shape=[jax.ShapeDtypeStruct((D, T), BF16), jax.ShapeDtypeStruct((T, 9 * D), BF16),
                   jax.ShapeDtypeStruct((T, D), F32), jax.ShapeDtypeStruct((T // CH, NH, HD, HD), F32),
                   jax.ShapeDtypeStruct((D, T), BF16), jax.ShapeDtypeStruct((T, D), F32)]
        + [jax.ShapeDtypeStruct(a.shape, a.dtype) for a in late],
        input_output_aliases={8 + n: 6 + n for n in range(nlate)},
        scratch_shapes=[pltpu.VMEM((TR, 9 * D), F32), pltpu.VMEM((TR, D), F32), pltpu.VMEM((NH, HD, HD), F32),
                        pltpu.VMEM((8, D), F32), pltpu.SemaphoreType.DMA((nlate, 6)),
                        pltpu.SemaphoreType.DMA((nlate, 6))],
        compiler_params=_cparams(("arbitrary",)),
    )(x_seq, head_tile, lb_param, wattn, whn, conv_w, w_in_g, w_out, *late)


def _mix_block_bwd(dh1, projb, o, sst, lb_param, whn, conv_w, w_out, psums):
    T = projb.shape[0]
    nt = T // TR
    ncht = TR // CH
    tb16 = TR // 16
    nex = len(psums)

    def body(*refs):
        dh1_ref, proj_ref, pc_ref, px_ref, o_ref, sst_ref, lbp_ref, whn_ref, cw_ref, wout_ref = refs[:10]
        dproj_ref, dlb_ref, dwhn_ref, dcw_ref = refs[10 + nex:14 + nex]
        exchange = _ChipExchange(refs[10:10 + nex], refs[14 + nex:14 + 2 * nex], refs[-2], refs[-1])
        dm_ref, dst_ref, dcvc_ref, acc_lb, acc_hn, acc_cw = refs[14 + 2 * nex:-2]
        s = pl.program_id(0)
        tile = nt - 1 - s

        @pl.when(s == 0)
        def _():
            dst_ref[...] = jnp.zeros_like(dst_ref)
            dcvc_ref[...] = jnp.zeros_like(dcvc_ref)
            acc_lb[...] = jnp.zeros_like(acc_lb)
            acc_hn[...] = jnp.zeros_like(acc_hn)
            acc_cw[...] = jnp.zeros_like(acc_cw)
            exchange.send()

        dm_ref[...] = _dot_nt(dh1_ref[...], wout_ref[...])
        lb = _lower_bound(lbp_ref)
        lower, upper = _tri_masks()
        ltri = _ones_where(lower)
        utri = _ones_where(upper)
        whn_v = whn_ref[...]
        w0, w1, w2 = cw_ref[0:1, :], cw_ref[1:2, :], cw_ref[2:3, :]
        cx_before_tile = jnp.where(tile > 0, (pc_ref[...].astype(F32) * px_ref[...].astype(F32))[8:16, :], 0.0)
        rid = lax.broadcasted_iota(jnp.int32, (CH, D), 0)

        def chunk(cc, carry):
            c = ncht - 1 - cc
            r0 = pl.multiple_of(c * CH, CH)
            rows = pl.ds(r0, CH)
            slab = lambda n: proj_ref[rows, n * D:(n + 1) * D].astype(F32)
            q_raw, f_raw, v, g_out, b_gate, c_gate, x_conv = (slab(n) for n in range(7))
            sa = _sigmoid(slab(7))
            sb = _sigmoid(slab(8))
            dm_v = dm_ref[rows, :]

            sq = _sigmoid(q_raw)
            q = q_raw * sq
            sg, f, g, k = _gates(f_raw, lb)
            gam = _tri_matmul(ltri, g)
            gam_l = gam[CH - 1:CH, :]
            e_l = jnp.exp(gam_l)
            e_g = jnp.exp(gam)
            e_ng = jnp.exp(-gam)
            e_kl = jnp.exp(gam_l - gam)
            qt = q * e_g
            kt = k * e_ng
            khat = k * e_kl
            s_go = _sigmoid(g_out)
            silu_go = g_out * s_go
            cx = c_gate * x_conv
            rprev = pl.ds(pl.multiple_of(jnp.maximum(r0 - 16, 0), 16), 16)
            cx_prev_in = (proj_ref[rprev, 5 * D:6 * D].astype(F32) * proj_ref[rprev, 6 * D:7 * D].astype(F32))[8:16, :]
            prev8 = jnp.where(c > 0, cx_prev_in, cx_before_tile)
            cx_m1 = _shift_down(cx, prev8, 1)
            cx_m2 = _shift_down(cx, prev8, 2)
            cv = w0 * cx_m2 + w1 * cx_m1 + w2 * cx
            y_b = b_gate * cv

            o_v = o_ref[rows, :]
            ro_parts, on_parts = [], []
            for h in range(NH):
                cs = slice(h * HD, (h + 1) * HD)
                o_h = o_v[:, cs]
                ro = lax.rsqrt(jnp.mean(o_h * o_h, axis=-1, keepdims=True) + EPS)
                ro_parts.append(ro)
                on_parts.append(o_h * ro)
            on = jnp.concatenate(on_parts, axis=1)
            y_a = on * whn_v * silu_go

            dy_a = dm_v * sa
            dy_b = dm_v * sb
            dproj_ref[rows, 7 * D:8 * D] = (dm_v * y_a * sa * (1.0 - sa)).astype(BF16)
            dproj_ref[rows, 8 * D:9 * D] = (dm_v * y_b * sb * (1.0 - sb)).astype(BF16)
            dproj_ref[rows, 4 * D:5 * D] = (dy_b * cv).astype(BF16)
            dcv = dy_b * b_gate
            acc_cw[0:1, :] += jnp.sum(dcv * cx_m2, axis=0, keepdims=True)
            acc_cw[1:2, :] += jnp.sum(dcv * cx_m1, axis=0, keepdims=True)
            acc_cw[2:3, :] += jnp.sum(dcv * cx, axis=0, keepdims=True)
            next8 = dcvc_ref[...]
            dcx = w2 * dcv + w1 * _shift_up(dcv, next8, 1) + w0 * _shift_up(dcv, next8, 2)
            dcvc_ref[...] = dcv[0:8, :]
            dproj_ref[rows, 5 * D:6 * D] = (dcx * x_conv).astype(BF16)
            dproj_ref[rows, 6 * D:7 * D] = (dcx * c_gate).astype(BF16)
            don = dy_a * whn_v * silu_go
            dproj_ref[rows, 3 * D:4 * D] = (dy_a * on * whn_v * (s_go * (1.0 + g_out * (1.0 - s_go)))).astype(BF16)
            acc_hn[...] += jnp.sum(dy_a * silu_go * on, axis=0, keepdims=True)

            dq_parts, dk_parts, dv_parts, dgam_parts, ext_parts = [], [], [], [], []
            for h in range(NH):
                cs = slice(h * HD, (h + 1) * HD)
                on_h = on_parts[h]
                don_h = don[:, cs]
                do_h = ro_parts[h] * (don_h - on_h * jnp.mean(don_h * on_h, axis=-1, keepdims=True))
                qt_h, kt_h, khat_h, v_h = qt[:, cs], kt[:, cs], khat[:, cs], v[:, cs]
                st = sst_ref[c, h]
                dstn = dst_ref[h]
                a_t = jnp.where(upper, _dot_nt(kt_h, qt_h), 0.0)
                da = jnp.where(lower, _dot_nt(do_h, v_h), 0.0)
                da_t = jnp.where(upper, _dot_nt(v_h, do_h), 0.0)
                dv_h = _dot(a_t, do_h) + _dot_nt(khat_h, dstn)
                dqt_state = _dot(do_h, st)
                dqt_chunk = _dot(da, kt_h)
                dkt = _dot(da_t, qt_h)
                dkhat = _dot(v_h, dstn)
                dq_h = (dqt_state + dqt_chunk) * e_g[:, cs]
                dk_h = dkt * e_ng[:, cs] + dkhat * e_kl[:, cs]
                khat_dkhat = dkhat * khat_h
                ext = (jnp.sum(khat_dkhat, axis=0, keepdims=True)
                       + e_l[:, cs] * jnp.sum(st * dstn, axis=0, keepdims=True))
                dst_ref[h] = _dot_tn(do_h, qt_h) + dstn * e_l[:, cs]
                dq_parts.append(dq_h)
                dk_parts.append(dk_h)
                dv_parts.append(dv_h)
                qt_seen = qt_h.astype(BF16).astype(F32)
                kt_seen = kt_h.astype(BF16).astype(F32)
                dgam_parts.append(qt_h * dqt_state + qt_seen * dqt_chunk - kt_seen * dkt - khat_dkhat)
                ext_parts.append(ext)
            dq = jnp.concatenate(dq_parts, axis=1)
            dk = jnp.concatenate(dk_parts, axis=1)
            dgam = jnp.concatenate(dgam_parts, axis=1)
            ext = jnp.concatenate(ext_parts, axis=1)
            dgam = dgam + jnp.where(rid == CH - 1, ext, 0.0)
            dg = _tri_matmul(utri, dgam)
            dproj_ref[rows, 0:D] = (dq * (sq * (1.0 + q_raw * (1.0 - sq)))).astype(BF16)
            df = dg / f - dk
            dproj_ref[rows, D:2 * D] = (df * (1.0 - lb) * sg * (1.0 - sg)).astype(BF16)
            dproj_ref[rows, 2 * D:3 * D] = jnp.concatenate(dv_parts, axis=1).astype(BF16)
            real = (tile * TR + r0 + rid) >= PAD
            acc_lb[...] += jnp.sum(jnp.where(real, df * (1.0 - sg), 0.0), axis=0, keepdims=True)
            return carry

        lax.fori_loop(0, ncht, chunk, 0)

        @pl.when(s == nt - 1)
        def _():
            dlb_ref[...] = acc_lb[...] * lb * (1.0 - lb)
            hn = acc_hn[...]
            tot = hn[:, 0:HD]
            for h in range(1, NH):
                tot = tot + hn[:, h * HD:(h + 1) * HD]
            dwhn_ref[...] = tot
            dcw_ref[...] = acc_cw[0:3, :]
            exchange.finish()

    hbm = pl.BlockSpec(memory_space=pl.ANY)
    rev = lambda s: (nt - 1 - s, 0)
    prevc = lambda s: (jnp.maximum((nt - 1 - s) * tb16 - 1, 0), 5)
    prevx = lambda s: (jnp.maximum((nt - 1 - s) * tb16 - 1, 0), 6)
    const = lambda s: (0, 0)
    return pl.pallas_call(
        body, name="mix_block_bwd", grid=(nt,),
        in_specs=[pl.BlockSpec((TR, D), rev),
                  pl.BlockSpec((TR, 9 * D), rev),
                  pl.BlockSpec((16, D), prevc),
                  pl.BlockSpec((16, D), prevx),
                  pl.BlockSpec((TR, D), rev),
                  pl.BlockSpec((ncht, NH, HD, HD), lambda s: (nt - 1 - s, 0, 0, 0)),
                  pl.BlockSpec((2, D), const),
                  pl.BlockSpec((1, D), const),
                  pl.BlockSpec((3, D), const),
                  pl.BlockSpec(memory_space=pltpu.VMEM)] + [hbm] * nex,
        out_specs=[pl.BlockSpec((TR, 9 * D), rev),
                   pl.BlockSpec((1, D), const),
                   pl.BlockSpec((1, HD), const),
                   pl.BlockSpec((3, D), const)] + [hbm] * nex,
        out_shape=[jax.ShapeDtypeStruct((T, 9 * D), BF16), jax.ShapeDtypeStruct((1, D), F32),
                   jax.ShapeDtypeStruct((1, HD), F32), jax.ShapeDtypeStruct((3, D), F32)]
        + [jax.ShapeDtypeStruct((3,) + p.shape[1:], p.dtype) for p in psums],
        scratch_shapes=[pltpu.VMEM((TR, D), F32), pltpu.VMEM((NH, HD, HD), F32), pltpu.VMEM((8, D), F32),
                        pltpu.VMEM((1, D), F32), pltpu.VMEM((1, D), F32), pltpu.VMEM((8, D), F32),
                        pltpu.SemaphoreType.DMA((nex, 3)), pltpu.SemaphoreType.DMA((nex, 3))],
        compiler_params=_cparams(("arbitrary",)),
    )(dh1, projb, projb, projb, o, sst, lb_param, whn, conv_w, w_out, *psums)


def _ffn_fwd(up, fcw, fcb):
    T = up.shape[0]
    nt = T // TR

    def body(up_ref, w_ref, b_ref, gg_ref, carry_ref):
        i = pl.program_id(0)

        @pl.when(i == 0)
        def _():
            carry_ref[...] = jnp.zeros_like(carry_ref)

        w0, w1, w2 = w_ref[0:1, :], w_ref[1:2, :], w_ref[2:3, :]

        def chunk(c, carry):
            rows = pl.ds(pl.multiple_of(c * CH, CH), CH)
            a_pre = up_ref[rows, 0:DFF]
            val = up_ref[rows, DFF:2 * DFF]
            prev8 = carry_ref[...]
            a = w0 * _shift_down(a_pre, prev8, 2) + w1 * _shift_down(a_pre, prev8, 1) + w2 * a_pre + b_ref[...]
            carry_ref[...] = a_pre[CH - 8:CH, :]
            gg_ref[rows, :] = (a * _sigmoid(a) * val).astype(BF16)
            return carry

        lax.fori_loop(0, TR // CH, chunk, 0)

    return pl.pallas_call(
        body, name="ffn_fwd", grid=(nt,),
        in_specs=[pl.BlockSpec((TR, 2 * DFF), lambda i: (i, 0)),
                  pl.BlockSpec((3, DFF), lambda i: (0, 0)),
                  pl.BlockSpec((1, DFF), lambda i: (0, 0))],
        out_specs=pl.BlockSpec((TR, DFF), lambda i: (i, 0)),
        out_shape=jax.ShapeDtypeStruct((T, DFF), BF16),
        scratch_shapes=[pltpu.VMEM((8, DFF), F32)],
        compiler_params=_cparams(("arbitrary",)),
    )(up, fcw, fcb)


def _ffn_bwd(up, dgg, fcw, fcb):
    T = up.shape[0]
    nt = T // TR
    tb = TR // 8
    ncht = TR // CH

    def body(up_ref, pa_ref, dgg_ref, w_ref, b_ref, dup_ref, dfw_ref, dfb_ref, carry_ref, acc_w, acc_b):
        s = pl.program_id(0)
        tile = nt - 1 - s

        @pl.when(s == 0)
        def _():
            carry_ref[...] = jnp.zeros_like(carry_ref)
            acc_w[...] = jnp.zeros_like(acc_w)
            acc_b[...] = jnp.zeros_like(acc_b)

        w0, w1, w2 = w_ref[0:1, :], w_ref[1:2, :], w_ref[2:3, :]
        a_before_tile = jnp.where(tile > 0, pa_ref[...], 0.0)

        def chunk(cc, carry):
            c = ncht - 1 - cc
            r0 = pl.multiple_of(c * CH, CH)
            rows = pl.ds(r0, CH)
            a_pre = up_ref[rows, 0:DFF]
            val = up_ref[rows, DFF:2 * DFF]
            rprev = pl.multiple_of(jnp.maximum(r0 - 8, 0), 8)
            prev8 = jnp.where(c > 0, up_ref[pl.ds(rprev, 8), 0:DFF], a_before_tile)
            a_m1 = _shift_down(a_pre, prev8, 1)
            a_m2 = _shift_down(a_pre, prev8, 2)
            a = w0 * a_m2 + w1 * a_m1 + w2 * a_pre + b_ref[...]
            sig = _sigmoid(a)
            dgg_v = dgg_ref[rows, :]
            da = dgg_v * val * (sig * (1.0 + a * (1.0 - sig)))
            dup_ref[rows, DFF:2 * DFF] = (dgg_v * (a * sig)).astype(BF16)
            next8 = carry_ref[...]
            dup_ref[rows, 0:DFF] = (w2 * da + w1 * _shift_up(da, next8, 1) + w0 * _shift_up(da, next8, 2)).astype(BF16)
            carry_ref[...] = da[0:8, :]
            acc_w[0:1, :] += jnp.sum(da * a_m2, axis=0, keepdims=True)
            acc_w[1:2, :] += jnp.sum(da * a_m1, axis=0, keepdims=True)
            acc_w[2:3, :] += jnp.sum(da * a_pre, axis=0, keepdims=True)
            acc_b[...] += jnp.sum(da, axis=0, keepdims=True)
            return carry

        lax.fori_loop(0, ncht, chunk, 0)

        @pl.when(s == nt - 1)
        def _():
            dfw_ref[...] = acc_w[0:3, :]
            dfb_ref[...] = acc_b[...]

    rev = lambda s: (nt - 1 - s, 0)
    const = lambda s: (0, 0)
    return pl.pallas_call(
        body, name="ffn_bwd", grid=(nt,),
        in_specs=[pl.BlockSpec((TR, 2 * DFF), rev),
                  pl.BlockSpec((8, DFF), lambda s: (jnp.maximum((nt - 1 - s) * tb - 1, 0), 0)),
                  pl.BlockSpec((TR, DFF), rev),
                  pl.BlockSpec((3, DFF), const),
                  pl.BlockSpec((1, DFF), const)],
        out_specs=[pl.BlockSpec((TR, 2 * DFF), rev),
                   pl.BlockSpec((3, DFF), const),
                   pl.BlockSpec((1, DFF), const)],
        out_shape=[jax.ShapeDtypeStruct((T, 2 * DFF), BF16), jax.ShapeDtypeStruct((3, DFF), F32),
                   jax.ShapeDtypeStruct((1, DFF), F32)],
        scratch_shapes=[pltpu.VMEM((8, DFF), F32), pltpu.VMEM((8, DFF), F32), pltpu.VMEM((1, DFF), F32)],
        compiler_params=_cparams(("arbitrary",)),
    )(up, up, dgg, fcw, fcb)


def _loss_head(h2, tgt, wfin):
    T = h2.shape[0]
    nt = T // TR

    def body(h_ref, t_ref, w_ref, dh_ref, loss_ref, dw_ref):
        i = pl.program_id(0)

        @pl.when(i == 0)
        def _():
            loss_ref[...] = jnp.zeros_like(loss_ref)
            dw_ref[...] = jnp.zeros_like(dw_ref)

        x = h_ref[...]
        r = lax.rsqrt(jnp.mean(x * x, axis=-1, keepdims=True) + EPS)
        n = x * r
        w = w_ref[...]
        diff = jnp.where(i > 0, n * w - t_ref[...], 0.0)
        loss_ref[...] += jnp.sum(diff * diff, axis=0, keepdims=True) * (0.5 / D)
        dy = diff * (1.0 / D)
        dw_ref[...] += jnp.sum(dy * n, axis=0, keepdims=True)
        dn = dy * w
        dh_ref[...] = r * (dn - n * jnp.mean(dn * n, axis=-1, keepdims=True))

    row = pl.BlockSpec((TR, D), lambda i: (i, 0))
    vec = pl.BlockSpec((1, D), lambda i: (0, 0))
    return pl.pallas_call(
        body, name="loss_head", grid=(nt,),
        in_specs=[row, pl.BlockSpec((TR, D), lambda i: (jnp.maximum(i - 1, 0), 0)), vec],
        out_specs=[row, vec, vec],
        out_shape=[jax.ShapeDtypeStruct((T, D), F32), jax.ShapeDtypeStruct((1, D), F32),
                   jax.ShapeDtypeStruct((1, D), F32)],
        compiler_params=_cparams(("arbitrary",)),
    )(h2, tgt, wfin)


def _ffn_block_fwd(h1, tgt, wffn, w_up_g, fcw, fcb, w_down, wfin):
    T = h1.shape[0]
    nt = T // TR
    nsl = w_up_g.shape[0]
    wsl = w_up_g.shape[2]

    def body(h_ref, t_ref, wn_ref, wup_ref, cw_ref, cb_ref, wdn_ref, wf_ref,
             u2t_ref, upb_ref, ggt_ref, dh_ref, loss_ref, dwf_ref, up_scr, gg_ref, carry_ref):
        i = pl.program_id(0)

        @pl.when(i == 0)
        def _():
            carry_ref[...] = jnp.zeros_like(carry_ref)
            loss_ref[...] = jnp.zeros_like(loss_ref)
            dwf_ref[...] = jnp.zeros_like(dwf_ref)

        x = h_ref[...]
        r2 = lax.rsqrt(jnp.mean(x * x, axis=-1, keepdims=True) + EPS)
        u2_f = x * r2 * wn_ref[...]
        u2 = u2_f.astype(BF16)
        u2t_ref[...] = u2_f.T.astype(BF16)
        for s in range(nsl):
            up_s = jnp.dot(u2, wup_ref[s], preferred_element_type=F32)
            up_scr[:, s * wsl:(s + 1) * wsl] = up_s
            upb_ref[:, s * wsl:(s + 1) * wsl] = up_s.astype(BF16)
        w0, w1, w2 = cw_ref[0:1, :], cw_ref[1:2, :], cw_ref[2:3, :]

        def chunk(c, carry):
            rows = pl.ds(pl.multiple_of(c * CH, CH), CH)
            a_pre = up_scr[rows, 0:DFF]
            val = up_scr[rows, DFF:2 * DFF]
            prev8 = carry_ref[...]
            a = w0 * _shift_down(a_pre, prev8, 2) + w1 * _shift_down(a_pre, prev8, 1) + w2 * a_pre + cb_ref[...]
            carry_ref[...] = a_pre[CH - 8:CH, :]
            gg_ref[rows, :] = a * _sigmoid(a) * val
            return carry

        lax.fori_loop(0, TR // CH, chunk, 0)
        gg_v = gg_ref[...]
        ggt_ref[...] = gg_v.T.astype(BF16)
        h2 = x + jnp.dot(gg_v.astype(BF16), wdn_ref[...], preferred_element_type=F32)
        r3 = lax.rsqrt(jnp.mean(h2 * h2, axis=-1, keepdims=True) + EPS)
        n3 = h2 * r3
        wf = wf_ref[...]
        diff = jnp.where(i > 0, n3 * wf - t_ref[...], 0.0)
        loss_ref[...] += jnp.sum(diff * diff, axis=0, keepdims=True) * (0.5 / D)
        dy = diff * (1.0 / D)
        dwf_ref[...] += jnp.sum(dy * n3, axis=0, keepdims=True)
        dn = dy * wf
        dh_ref[...] = r3 * (dn - n3 * jnp.mean(dn * n3, axis=-1, keepdims=True))

    row = lambda w: pl.BlockSpec((TR, w), lambda i: (i, 0))
    col = lambda w: pl.BlockSpec((w, TR), lambda i: (0, i))
    vec = lambda w, r=1: pl.BlockSpec((r, w), lambda i: (0, 0))
    vm = pl.BlockSpec(memory_space=pltpu.VMEM)
    return pl.pallas_call(
        body, name="ffn_block_fwd", grid=(nt,),
        in_specs=[row(D), pl.BlockSpec((TR, D), lambda i: (jnp.maximum(i - 1, 0), 0)), vec(D), vm,
                  vec(DFF, 3), vec(DFF), vm, vec(D)],
        out_specs=[col(D), row(2 * DFF), col(DFF), row(D), vec(D), vec(D)],
        out_shape=[jax.ShapeDtypeStruct((D, T), BF16), jax.ShapeDtypeStruct((T, 2 * DFF), BF16),
                   jax.ShapeDtypeStruct((DFF, T), BF16), jax.ShapeDtypeStruct((T, D), F32),
                   jax.ShapeDtypeStruct((1, D), F32), jax.ShapeDtypeStruct((1, D), F32)],
        scratch_shapes=[pltpu.VMEM((TR, 2 * DFF), F32), pltpu.VMEM((TR, DFF), F32), pltpu.VMEM((8, DFF), F32)],
        compiler_params=_cparams(("arbitrary",)),
    )(h1, tgt, wffn, w_up_g, fcw, fcb, w_down, wfin)


def _ffn_block_bwd(dh2, upb, h1, wffn, w_up_g, fcw, fcb, w_down):
    T = h1.shape[0]
    nt = T // TR
    ncht = TR // CH
    nsl = w_up_g.shape[0]
    wsl = w_up_g.shape[2]
    tb16 = TR // 16

    def body(dh2_ref, up_ref, pa_ref, h_ref, wn_ref, wup_ref, cw_ref, cb_ref, wdn_ref,
             dup_ref, dh1_ref, dfw_ref, dfb_ref, dwn_ref, dgg_scr, carry_ref, acc_w, acc_b, acc_n):
        s = pl.program_id(0)
        tile = nt - 1 - s

        @pl.when(s == 0)
        def _():
            carry_ref[...] = jnp.zeros_like(carry_ref)
            acc_w[...] = jnp.zeros_like(acc_w)
            acc_b[...] = jnp.zeros_like(acc_b)
            acc_n[...] = jnp.zeros_like(acc_n)

        dh2 = dh2_ref[...]
        dgg_scr[...] = _dot_nt(dh2, wdn_ref[...])
        w0, w1, w2 = cw_ref[0:1, :], cw_ref[1:2, :], cw_ref[2:3, :]
        a_before_tile = jnp.where(tile > 0, pa_ref[...].astype(F32)[8:16, :], 0.0)

        def chunk(cc, carry):
            c = ncht - 1 - cc
            r0 = pl.multiple_of(c * CH, CH)
            rows = pl.ds(r0, CH)
            a_pre = up_ref[rows, 0:DFF].astype(F32)
            val = up_ref[rows, DFF:2 * DFF].astype(F32)
            rprev = pl.multiple_of(jnp.maximum(r0 - 16, 0), 16)
            prev_in = up_ref[pl.ds(rprev, 16), 0:DFF].astype(F32)[8:16, :]
            prev8 = jnp.where(c > 0, prev_in, a_before_tile)
            a_m1 = _shift_down(a_pre, prev8, 1)
            a_m2 = _shift_down(a_pre, prev8, 2)
            a = w0 * a_m2 + w1 * a_m1 + w2 * a_pre + cb_ref[...]
            sig = _sigmoid(a)
            dgg_v = dgg_scr[rows, :]
            da = dgg_v * val * (sig * (1.0 + a * (1.0 - sig)))
            dup_ref[rows, DFF:2 * DFF] = (dgg_v * (a * sig)).astype(BF16)
            next8 = carry_ref[...]
            dup_ref[rows, 0:DFF] = (w2 * da + w1 * _shift_up(da, next8, 1) + w0 * _shift_up(da, next8, 2)).astype(BF16)
            carry_ref[...] = da[0:8, :]
            acc_w[0:1, :] += jnp.sum(da * a_m2, axis=0, keepdims=True)
            acc_w[1:2, :] += jnp.sum(da * a_m1, axis=0, keepdims=True)
            acc_w[2:3, :] += jnp.sum(da * a_pre, axis=0, keepdims=True)
            acc_b[...] += jnp.sum(da, axis=0, keepdims=True)
            return carry

        lax.fori_loop(0, ncht, chunk, 0)
        du2 = _dot_nt(dup_ref[:, 0:wsl], wup_ref[0])
        for sl in range(1, nsl):
            du2 = du2 + _dot_nt(dup_ref[:, sl * wsl:(sl + 1) * wsl], wup_ref[sl])
        x = h_ref[...]
        r2 = lax.rsqrt(jnp.mean(x * x, axis=-1, keepdims=True) + EPS)
        n2 = x * r2
        dn = du2 * wn_ref[...]
        dh1_ref[...] = dh2 + r2 * (dn - n2 * jnp.mean(dn * n2, axis=-1, keepdims=True))
        acc_n[...] += jnp.sum(du2 * n2, axis=0, keepdims=True)

        @pl.when(s == nt - 1)
        def _():
            dfw_ref[...] = acc_w[0:3, :]
            dfb_ref[...] = acc_b[...]
            dwn_ref[...] = acc_n[...]

    rev = lambda w: pl.BlockSpec((TR, w), lambda s: (nt - 1 - s, 0))
    vec = lambda w, r=1: pl.BlockSpec((r, w), lambda s: (0, 0))
    vm = pl.BlockSpec(memory_space=pltpu.VMEM)
    return pl.pallas_call(
        body, name="ffn_block_bwd", grid=(nt,),
        in_specs=[rev(D), rev(2 * DFF),
                  pl.BlockSpec((16, DFF), lambda s: (jnp.maximum((nt - 1 - s) * tb16 - 1, 0), 0)),
                  rev(D), vec(D), vm, vec(DFF, 3), vec(DFF), vm],
        out_specs=[rev(2 * DFF), rev(D), vec(DFF, 3), vec(DFF), vec(D)],
        out_shape=[jax.ShapeDtypeStruct((T, 2 * DFF), BF16), jax.ShapeDtypeStruct((T, D), F32),
                   jax.ShapeDtypeStruct((3, DFF), F32), jax.ShapeDtypeStruct((1, DFF), F32),
                   jax.ShapeDtypeStruct((1, D), F32)],
        scratch_shapes=[pltpu.VMEM((TR, DFF), F32), pltpu.VMEM((8, DFF), F32), pltpu.VMEM((8, DFF), F32),
                        pltpu.VMEM((1, DFF), F32), pltpu.VMEM((1, D), F32)],
        compiler_params=_cparams(("arbitrary",)),
    )(dh2, upb, upb, h1, wffn, w_up_g, fcw, fcb, w_down)


def _place():
    x, y, c = lax.axis_index("x"), lax.axis_index("y"), lax.axis_index("c")
    return x, y, c


_CHIP_FLIPS = ((1, 0), (0, 1), (1, 1))


def _flip(v, bit):
    return 1 - v if bit else v


def _into_slot(w, j_idx, *, rb, dtype, name):
    r, cdim = w.shape

    def body(j_ref, w_ref, out_ref):
        del j_ref
        out_ref[...] = w_ref[...].astype(dtype)

    grid_spec = pltpu.PrefetchScalarGridSpec(
        num_scalar_prefetch=1, grid=(r // rb,),
        in_specs=[pl.BlockSpec((rb, cdim), lambda i, j_ref: (i, 0))],
        out_specs=pl.BlockSpec((None, rb, cdim), lambda i, j_ref: (j_ref[0], i, 0)))
    return pl.pallas_call(
        body, name=name, grid_spec=grid_spec, out_shape=jax.ShapeDtypeStruct((NSHARD, r, cdim), dtype),
        compiler_params=_cparams(("parallel",)),
    )(j_idx, w)


class _Gather:
    def __init__(self, outs, send_sems, recv_sems, whole):
        self.outs, self.send_sems, self.recv_sems, self.whole = outs, send_sems, recv_sems, whole
        self.x, self.y, self.c = _place()
        self.j = 2 * self.x + self.y
        self.sibling = (self.x, self.y, 1 - self.c)

    def _rows(self, w, core):
        r = self.outs[w].shape[1]
        return pl.ds(0, r) if self.whole[w] else pl.ds(core * (r // 2), r // 2)

    def _copy(self, w, slot, core, sem, to):
        piece = self.outs[w].at[slot, self._rows(w, core), :]
        return pltpu.make_async_remote_copy(
            src_ref=piece, dst_ref=piece, send_sem=self.send_sems.at[w, sem], recv_sem=self.recv_sems.at[w, sem],
            device_id=to, device_id_type=MESH)

    def _chips(self):
        for kk, (fx, fy) in enumerate(_CHIP_FLIPS):
            px, py = _flip(self.x, fx), _flip(self.y, fy)
            yield kk, 2 * px + py, (px, py, self.c)

    def send(self):
        for w in range(len(self.outs)):
            for kk, _, to in self._chips():
                self._copy(w, self.j, self.c, kk, to).start()

    def relay(self):
        for w in range(len(self.outs)):
            for kk, jk, _ in self._chips():
                self._copy(w, jk, self.c, kk, self.sibling).wait_recv()
                if not self.whole[w]:
                    self._copy(w, jk, self.c, 3 + kk, self.sibling).start()

    def finish(self):
        for w in range(len(self.outs)):
            for kk, jk, to in self._chips():
                self._copy(w, self.j, self.c, kk, to).wait_send()
                if not self.whole[w]:
                    self._copy(w, jk, 1 - self.c, 3 + kk, self.sibling).wait_recv()
                    self._copy(w, jk, self.c, 3 + kk, self.sibling).wait_send()


def _allgather_weights(slotted, whole):
    n = len(slotted)

    def body(*refs):
        g = _Gather(refs[n:2 * n], refs[2 * n], refs[2 * n + 1], whole)
        g.send()
        g.relay()
        g.finish()

    any_spec = pl.BlockSpec(memory_space=pl.ANY)
    return pl.pallas_call(
        body, name="allgather_weights",
        in_specs=[any_spec] * n, out_specs=[any_spec] * n,
        out_shape=[jax.ShapeDtypeStruct(a.shape, a.dtype) for a in slotted],
        input_output_aliases={i: i for i in range(n)},
        scratch_shapes=[pltpu.SemaphoreType.DMA((n, 6)), pltpu.SemaphoreType.DMA((n, 6))],
    )(*slotted)


class _ChipExchange:
    def __init__(self, ins, outs, send_sems, recv_sems):
        self.ins, self.outs, self.send_sems, self.recv_sems = ins, outs, send_sems, recv_sems
        self.x, self.y, self.c = _place()

    def _copies(self):
        for w in range(len(self.ins)):
            for kk, (fx, fy) in enumerate(_CHIP_FLIPS):
                px, py = _flip(self.x, fx), _flip(self.y, fy)
                yield pltpu.make_async_remote_copy(
                    src_ref=self.ins[w].at[2 * px + py], dst_ref=self.outs[w].at[kk],
                    send_sem=self.send_sems.at[w, kk], recv_sem=self.recv_sems.at[w, kk],
                    device_id=(px, py, self.c), device_id_type=MESH)

    def send(self):
        for cp in self._copies():
            cp.start()

    def finish(self):
        for cp in self._copies():
            cp.wait()


def _pair_exchange(grads, name):
    nw = len(grads)

    def body(*refs):
        ins, outs = refs[:nw], refs[nw:2 * nw]
        send_sems, recv_sems = refs[2 * nw:]
        x, y, c = _place()
        sibling = (x, y, 1 - c)
        cps = []
        for w in range(nw):
            half = ins[w].shape[1] // 2
            cp = pltpu.make_async_remote_copy(
                src_ref=ins[w].at[:, pl.ds((1 - c) * half, half), :], dst_ref=outs[w],
                send_sem=send_sems.at[w], recv_sem=recv_sems.at[w], device_id=sibling, device_id_type=MESH)
            cp.start()
            cps.append(cp)
        for cp in cps:
            cp.wait()

    any_spec = pl.BlockSpec(memory_space=pl.ANY)
    return pl.pallas_call(
        body, name=name,
        in_specs=[any_spec] * nw, out_specs=[any_spec] * nw,
        out_shape=[jax.ShapeDtypeStruct((g.shape[0], g.shape[1] // 2, g.shape[2]), g.dtype) for g in grads],
        scratch_shapes=[pltpu.SemaphoreType.DMA((nw,)), pltpu.SemaphoreType.DMA((nw,))],
    )(*grads)


def _pair_add(g, other, c_idx, *, rb, name):
    S, r, cdim = g.shape
    half = r // 2
    nb = half // rb

    def body(c_ref, g_ref, o_ref, out_ref):
        del c_ref
        out_ref[...] = (g_ref[...] + o_ref[...]).astype(BF16)

    grid_spec = pltpu.PrefetchScalarGridSpec(
        num_scalar_prefetch=1, grid=(S, nb),
        in_specs=[pl.BlockSpec((None, rb, cdim), lambda s, i, c_ref: (s, c_ref[0] * nb + i, 0)),
                  pl.BlockSpec((None, rb, cdim), lambda s, i, c_ref: (s, i, 0))],
        out_specs=pl.BlockSpec((None, rb, cdim), lambda s, i, c_ref: (s, i, 0)))
    return pl.pallas_call(
        body, name=name, grid_spec=grid_spec, out_shape=jax.ShapeDtypeStruct((S, half, cdim), BF16),
        compiler_params=_cparams(("parallel", "parallel")),
    )(c_idx, g, other)


def _chip_exchange(psums):
    nw = len(psums)

    def body(*refs):
        ex = _ChipExchange(refs[:nw], refs[nw:2 * nw], refs[2 * nw], refs[2 * nw + 1])
        ex.send()
        ex.finish()

    any_spec = pl.BlockSpec(memory_space=pl.ANY)
    return pl.pallas_call(
        body, name="grad_chip_exchange",
        in_specs=[any_spec] * nw, out_specs=[any_spec] * nw,
        out_shape=[jax.ShapeDtypeStruct((3,) + p.shape[1:], p.dtype) for p in psums],
        scratch_shapes=[pltpu.SemaphoreType.DMA((nw, 3)), pltpu.SemaphoreType.DMA((nw, 3))],
    )(*psums)


def _chip_sum(psum, parts, cj_idx, *, rb, name):
    S, half, cdim = psum.shape
    nb = half // rb

    def body(cj_ref, own_ref, p_ref, out_ref):
        del cj_ref
        f = lambda v: v.astype(F32)
        out_ref[...] = ((f(own_ref[...]) + f(p_ref[0])) + f(p_ref[1])) + f(p_ref[2])

    grid_spec = pltpu.PrefetchScalarGridSpec(
        num_scalar_prefetch=1, grid=(nb,),
        in_specs=[pl.BlockSpec((None, rb, cdim), lambda i, cj: (cj[1], i, 0)),
                  pl.BlockSpec((3, rb, cdim), lambda i, cj: (0, i, 0))],
        out_specs=pl.BlockSpec((rb, cdim), lambda i, cj: (cj[0] * nb + i, 0)))
    return pl.pallas_call(
        body, name=name, grid_spec=grid_spec, out_shape=jax.ShapeDtypeStruct((2 * half, cdim), F32),
        compiler_params=_cparams(("parallel",)),
    )(cj_idx, psum, parts)


SLAB_W = 1024


def _final_exchange(pieces, grads):
    n, nw = len(pieces), len(grads)
    segs, at = [], 0
    for idx, p in enumerate(pieces):
        r, wd = p.shape
        for c0 in range(0, wd, SLAB_W):
            if r > 1:
                at = -(-at // 8) * 8
            segs.append((idx, c0, min(SLAB_W, wd - c0), at))
            at += r
    rows = -(-at // 8) * 8
    flips = [(fx, fy, fc) for fx in (0, 1) for fy in (0, 1) for fc in (0, 1)][1:]

    def body(*refs):
        ins = refs[:n]
        outs = refs[n + nw:2 * n + nw]
        g_refs = refs[2 * n + nw:2 * n + 2 * nw]
        mine_ref, slots_ref, send_sems, recv_sems, gsend_sems, grecv_sems = refs[2 * n + 2 * nw:]
        x, y, c = _place()
        me = 4 * x + 2 * y + c
        sibling = (x, y, 1 - c)

        def swap(w, core):
            half = g_refs[w].shape[0] // 2
            rows_ = g_refs[w].at[pl.ds(core * half, half), :]
            return pltpu.make_async_remote_copy(
                src_ref=rows_, dst_ref=rows_, send_sem=gsend_sems.at[w], recv_sem=grecv_sems.at[w],
                device_id=sibling, device_id_type=MESH)

        for w in range(nw):
            swap(w, c).start()
        mine_ref[...] = jnp.zeros_like(mine_ref)
        for idx, c0, wd, st in segs:
            r = ins[idx].shape[0]
            mine_ref[st:st + r, 0:wd] = ins[idx][:, c0:c0 + wd]
        slots_ref[me] = mine_ref[...]
        cps = []
        for kk, (fx, fy, fc) in enumerate(flips):
            cp = pltpu.make_async_remote_copy(
                src_ref=mine_ref, dst_ref=slots_ref.at[me], send_sem=send_sems.at[kk], recv_sem=recv_sems.at[kk],
                device_id=(_flip(x, fx), _flip(y, fy), _flip(c, fc)), device_id_type=MESH)
            cp.start()
            cps.append(cp)
        for cp in cps:
            cp.wait()
        tot = slots_ref[0]
        for d in range(1, 8):
            tot = tot + slots_ref[d]
        mine_ref[...] = tot
        for idx, c0, wd, st in segs:
            r = ins[idx].shape[0]
            val = mine_ref[st:st + r, 0:wd]
            if idx == n - 1:
                outs[idx][...] = jnp.sum(val, keepdims=True)
            else:
                outs[idx][:, c0:c0 + wd] = val
        for w in range(nw):
            swap(w, 1 - c).wait_recv()
            swap(w, c).wait_send()

    vm = pl.BlockSpec(memory_space=pltpu.VMEM)
    hbm = pl.BlockSpec(memory_space=pl.ANY)
    out_shape = ([jax.ShapeDtypeStruct(p.shape, F32) for p in pieces[:-1]] + [jax.ShapeDtypeStruct((1, 1), F32)]
                 + [jax.ShapeDtypeStruct(g.shape, g.dtype) for g in grads])
    res = pl.pallas_call(
        body, name="final_exchange", in_specs=[vm] * n + [hbm] * nw, out_specs=[vm] * n + [hbm] * nw,
        out_shape=out_shape, input_output_aliases={n + w: n + w for w in range(nw)},
        scratch_shapes=[pltpu.VMEM((rows, SLAB_W), F32), pltpu.VMEM((8, rows, SLAB_W), F32),
                        pltpu.SemaphoreType.DMA((7,)), pltpu.SemaphoreType.DMA((7,)),
                        pltpu.SemaphoreType.DMA((nw,)), pltpu.SemaphoreType.DMA((nw,))],
    )(*pieces, *grads)
    return res[:n], res[n:]


def _adamw(w, g, m, v, *, rb, name):
    r, cdim = w.shape

    def body(w_ref, g_ref, m_ref, v_ref, go_ref, d_ref, nm_ref, nv_ref):
        go_ref[...] = g_ref[...]
        d_ref[...], nm_ref[...], nv_ref[...] = _adamw_update(w_ref[...], g_ref[...], m_ref[...], v_ref[...])

    spec = pl.BlockSpec((rb, cdim), lambda i: (i, 0))
    shp = jax.ShapeDtypeStruct((r, cdim), F32)
    return pl.pallas_call(
        body, name=name, grid=(r // rb,), in_specs=[spec] * 4, out_specs=[spec] * 4, out_shape=[shp] * 4,
        compiler_params=_cparams(("parallel",)),
    )(w, g, m, v)


def _adamw_update(w, g, m, v):
    nm = ADAM_B1 * m + (1.0 - ADAM_B1) * g
    nv = ADAM_B2 * v + (1.0 - ADAM_B2) * (g * g)
    m_hat = nm / (1.0 - ADAM_B1 ** ADAM_STEP)
    v_hat = nv / (1.0 - ADAM_B2 ** ADAM_STEP)
    return -ADAM_LR * (m_hat / (jnp.sqrt(v_hat) + ADAM_EPS) + ADAM_WD * w), nm, nv


def _adamw_small(params):
    n = len(params)

    def body(*refs):
        ins, outs = refs[:4 * n], refs[4 * n:]
        for p in range(n):
            w_ref, g_ref, m_ref, v_ref = ins[4 * p:4 * p + 4]
            d, nm, nv = _adamw_update(w_ref[...], g_ref[...], m_ref[...], v_ref[...])
            outs[3 * p][...] = d
            outs[3 * p + 1][...] = nm
            outs[3 * p + 2][...] = nv

    vm = pl.BlockSpec(memory_space=pltpu.VMEM)
    flat = [a for p in params for a in p]
    out_shape = [jax.ShapeDtypeStruct(p[0].shape, F32) for p in params for _ in range(3)]
    res = pl.pallas_call(
        body, name="adamw_small", in_specs=[vm] * (4 * n), out_specs=[vm] * (3 * n), out_shape=out_shape,
    )(*flat)
    return [tuple(res[3 * p:3 * p + 3]) for p in range(n)]


_PAIR_ADD_ROWS = {"w_in": 256, "w_out": 128, "w_up": 256, "w_down": 176}


def _pair_sums(grads, names, c_idx, tag):
    others = _pair_exchange(grads, name=f"grad_pair_exchange_{tag}")
    return [_pair_add(g, o_, c_idx, rb=_PAIR_ADD_ROWS[n], name=f"pair_add_{n}") for g, o_, n in zip(grads, others, names)]


def _local_step(x, tgt, meta_full, lb_param, attn_norm_w, w_in_g, hgrn_norm_w, conv_w_full, w_out_full,
                ffn_norm_w, late_slotted, fcw_full, ffn_conv_b, final_norm_w, c_idx, cj_idx):
    seq = x.shape[0]
    T = TR + seq
    head_tile = jnp.concatenate([jnp.zeros((PAD, D), F32), meta_full], axis=0)
    whn_t = jnp.tile(hgrn_norm_w, (1, NH))

    ut, projb, o, sst, mt, h1, w_up_g, w_down_g = _mix_block_fwd(
        x, head_tile, lb_param, attn_norm_w, whn_t, conv_w_full, w_in_g, w_out_full, late_slotted)
    w_down_full = w_down_g.reshape(DFF, D)
    u2t, upb, ggt, dh2, loss_vec, dwfin = _ffn_block_fwd(
        h1, tgt, ffn_norm_w, w_up_g, fcw_full, ffn_conv_b, w_down_full, final_norm_w.reshape(1, D))

    dup, dh1, dfw, dfb, dwffn = _ffn_block_bwd(dh2, upb, h1, ffn_norm_w, w_up_g, fcw_full, ffn_conv_b, w_down_full)
    kb = 1408 if T % 1408 == 0 else TR
    g_up = _matmul(u2t, dup, mode="nn", bm=D, bn=1408, bk=kb, out_dtype=F32, name="dw_up_mm",
                   out_shards=(NSHARD, 2 * DFF // NSHARD))
    g_down = _matmul(ggt, dh2, mode="nn", bm=1408, bn=D, bk=kb, out_dtype=F32, name="dw_down_mm")
    ps_ffn = _pair_sums([g_up, g_down.reshape(NSHARD, DFF // NSHARD, D)], ("w_up", "w_down"), c_idx, "ffn")

    dproj, dlb, dwhn, dcw, *parts_ffn = _mix_block_bwd(
        dh1, projb, o, sst, lb_param, whn_t, conv_w_full, w_out_full, ps_ffn)
    g_in = _matmul(ut, dproj, mode="nn", bm=D, bn=1152, bk=kb, out_dtype=F32, name="dw_in_mm",
                   out_shards=(NSHARD, 9 * D // NSHARD))
    g_out = _matmul(mt, dh1, mode="nn", bm=D, bn=D, bk=kb, out_dtype=F32, name="dw_out_mm")
    ps_mix = _pair_sums([g_in, g_out.reshape(NSHARD, D // NSHARD, D)], ("w_in", "w_out"), c_idx, "mix")

    grad_x, dmeta, dwattn, *parts_mix = _input_grad_block(dproj, x, head_tile, dh1, attn_norm_w, w_in_g, ps_mix)

    halves = [_chip_sum(ps, p, cj_idx, rb=_PAIR_ADD_ROWS[n], name=f"chip_sum_{n}")
              for ps, p, n in zip(ps_mix + ps_ffn, parts_mix + parts_ffn, ("w_in", "w_out", "w_up", "w_down"))]
    small = dict(dlb=dlb, dwattn=dwattn, dwhn=dwhn, dwffn=dwffn, dfb=dfb, dwfin=dwfin,
                 dcw=dcw, dfw=dfw, dmeta=dmeta, loss=loss_vec)
    return grad_x, small, halves


_SMALL_ORDER = ("dmeta", "dcw", "dfw", "dlb", "dwattn", "dwhn", "dwffn", "dfb", "dwfin", "loss")


def kernel(x, meta_tokens, lb_param, attn_norm_w, w_in, hgrn_norm_w, conv_w, w_out, ffn_norm_w, w_up, ffn_conv_w, ffn_conv_b, w_down, final_norm_w, loss_target, m_meta_tokens, m_lb_param, m_attn_norm_w, m_w_in, m_hgrn_norm_w, m_conv_w, m_w_out, m_ffn_norm_w, m_w_up, m_ffn_conv_w, m_ffn_conv_b, m_w_down, m_final_norm_w, v_meta_tokens, v_lb_param, v_attn_norm_w, v_w_in, v_hgrn_norm_w, v_conv_w, v_w_out, v_ffn_norm_w, v_w_up, v_ffn_conv_w, v_ffn_conv_b, v_w_down, v_final_norm_w):
    xi, yi, ci = _place()
    j = 2 * xi + yi
    c_idx = jnp.reshape(ci, (1,)).astype(jnp.int32)

    j_idx = jnp.reshape(j, (1,)).astype(jnp.int32)
    ds_, fs_ = D // NSHARD, DFF // NSHARD
    widen = lambda a: jnp.pad(a, ((0, 0), (0, 768 - a.shape[1])))
    rows_small = jnp.concatenate([widen(meta_tokens), widen(conv_w[0]), widen(ffn_conv_w[0]),
                                  jnp.zeros((2, 768), F32)], axis=0)
    slotted = [_into_slot(w[0], j_idx, rb=rb, dtype=BF16, name=f"slot_{n}")
               for w, rb, n in ((w_in, 256, "w_in"), (w_out, 128, "w_out"), (w_up, 256, "w_up"), (w_down, 176, "w_down"))]
    s_in, s_out, s_up, s_down = slotted
    s_small = _into_slot(rows_small, j_idx, rb=rows_small.shape[0], dtype=F32, name="slot_small")
    w_in_g, w_out_g, small_g = _allgather_weights([s_in, s_out, s_small], (False, False, True))
    unshard = lambda a: jnp.transpose(a, (1, 0, 2)).reshape(a.shape[1], -1)
    meta_full = unshard(small_g[:, 0:NMETA, 0:ds_])
    conv_w_full = unshard(small_g[:, NMETA:NMETA + 3, 0:ds_])
    fcw_full = unshard(small_g[:, NMETA + 3:NMETA + 6, 0:fs_])

    cj_idx = jnp.stack([ci, j]).astype(jnp.int32)
    grad_x, small, halves = _local_step(
        x[0], loss_target[0], meta_full, lb_param, attn_norm_w, w_in_g, hgrn_norm_w, conv_w_full,
        w_out_g.reshape(D, D), ffn_norm_w, [s_up, s_down], fcw_full, ffn_conv_b, final_norm_w, c_idx, cj_idx)

    names = _SMALL_ORDER
    small_sums, g_big = _final_exchange([small[n] for n in names], halves)
    vals = dict(zip(names, small_sums))
    loss = vals["loss"].reshape(())
    g_small = {
        "meta_tokens": lax.dynamic_slice_in_dim(vals["dmeta"], j * (D // NSHARD), D // NSHARD, axis=1),
        "lb_param": jnp.concatenate([vals["dlb"], -vals["dlb"]], axis=0),
        "attn_norm_w": vals["dwattn"],
        "hgrn_norm_w": vals["dwhn"],
        "conv_w": lax.dynamic_slice_in_dim(vals["dcw"], j * (D // NSHARD), D // NSHARD, axis=1)[None],
        "ffn_norm_w": vals["dwffn"],
        "ffn_conv_w": lax.dynamic_slice_in_dim(vals["dfw"], j * (DFF // NSHARD), DFF // NSHARD, axis=1)[None],
        "ffn_conv_b": vals["dfb"],
        "final_norm_w": vals["dwfin"].reshape(D),
    }


    weights = {"meta_tokens": meta_tokens, "lb_param": lb_param, "attn_norm_w": attn_norm_w, "w_in": w_in,
               "hgrn_norm_w": hgrn_norm_w, "conv_w": conv_w, "w_out": w_out, "ffn_norm_w": ffn_norm_w,
               "w_up": w_up, "ffn_conv_w": ffn_conv_w, "ffn_conv_b": ffn_conv_b, "w_down": w_down,
               "final_norm_w": final_norm_w}
    ms = {"meta_tokens": m_meta_tokens, "lb_param": m_lb_param, "attn_norm_w": m_attn_norm_w, "w_in": m_w_in,
          "hgrn_norm_w": m_hgrn_norm_w, "conv_w": m_conv_w, "w_out": m_w_out, "ffn_norm_w": m_ffn_norm_w,
          "w_up": m_w_up, "ffn_conv_w": m_ffn_conv_w, "ffn_conv_b": m_ffn_conv_b, "w_down": m_w_down,
          "final_norm_w": m_final_norm_w}
    vs = {"meta_tokens": v_meta_tokens, "lb_param": v_lb_param, "attn_norm_w": v_attn_norm_w, "w_in": v_w_in,
          "hgrn_norm_w": v_hgrn_norm_w, "conv_w": v_conv_w, "w_out": v_w_out, "ffn_norm_w": v_ffn_norm_w,
          "w_up": v_w_up, "ffn_conv_w": v_ffn_conv_w, "ffn_conv_b": v_ffn_conv_b, "w_down": v_w_down,
          "final_norm_w": v_final_norm_w}
    order = list(weights)
    grads, deltas, new_m, new_v = {}, {}, {}, {}

    for name, g, rb in zip(("w_in", "w_out", "w_up", "w_down"), g_big, (256, 128, 256, 176)):
        shp = weights[name].shape
        w2, m2, v2 = (a.reshape(shp[1], shp[2]) for a in (weights[name], ms[name], vs[name]))
        g_, d_, nm_, nv_ = _adamw(w2, g, m2, v2, rb=rb, name=f"adamw_{name}")
        grads[name], deltas[name], new_m[name], new_v[name] = (a.reshape(shp) for a in (g_, d_, nm_, nv_))

    small_names = [n for n in order if n not in grads]
    as2d = lambda a: a.reshape(-1, a.shape[-1])
    res = _adamw_small([tuple(as2d(a) for a in (weights[n], g_small[n], ms[n], vs[n])) for n in small_names])
    for n, (d_, nm_, nv_) in zip(small_names, res):
        shp = weights[n].shape
        grads[n], deltas[n], new_m[n], new_v[n] = (a.reshape(shp) for a in (g_small[n], d_, nm_, nv_))

    return (loss, grad_x[None], *[grads[n] for n in order], *[deltas[n] for n in order],
            *[new_m[n] for n in order], *[new_v[n] for n in order])
```

```python
import functools

import jax
import jax.numpy as jnp
from jax import lax
from jax.experimental import pallas as pl
from jax.experimental.pallas import tpu as pltpu

F32 = jnp.float32
BF16 = jnp.bfloat16
MESH = pl.DeviceIdType.MESH

D = 1024
NH = 8
HD = 128
DFF = 2816
NMETA = 16
EPS = 1e-6
TR = 256
PAD = TR - NMETA
CH = 64
NSHARD = 4
VMEM_LIMIT = 56 * 1024 * 1024

ADAM_LR = 0.001
ADAM_B1 = 0.9
ADAM_B2 = 0.999
ADAM_EPS = 1e-08
ADAM_WD = 0.01
ADAM_STEP = 10


def _cparams(semantics=None, **kw):
    return pltpu.CompilerParams(dimension_semantics=semantics, vmem_limit_bytes=VMEM_LIMIT, **kw)


def _sigmoid(x):
    return 0.5 * jnp.tanh(0.5 * x) + 0.5


def _matmul(a, b, *, mode, bm, bn, bk, out_dtype, name, residual=None, out_shards=None):
    if mode == "tn":
        K, M = a.shape
    else:
        M, K = a.shape
    b3 = b.ndim == 3
    if b3:
        S, R, Cs = b.shape
        bcols = S * Cs
        brows = R
    else:
        brows, bcols = b.shape
    if mode == "nt":
        N = brows
        assert bcols == K
    else:
        N = bcols
        assert brows == K
    assert M % bm == 0 and N % bn == 0 and K % bk == 0, (name, M, N, K, bm, bn, bk)
    nm, nn, nk = M // bm, N // bn, K // bk

    if mode == "tn":
        a_spec = pl.BlockSpec((bk, bm), lambda i, j, k: (k, i))
    else:
        a_spec = pl.BlockSpec((bm, bk), lambda i, j, k: (i, k))
    if mode == "nt":
        if b3:
            assert bn == R and Cs % bk == 0
            per = Cs // bk
            b_spec = pl.BlockSpec((None, bn, bk), lambda i, j, k: (k // per, 0, k % per))
        else:
            b_spec = pl.BlockSpec((bn, bk), lambda i, j, k: (j, k))
    else:
        if b3:
            assert bk == R and Cs % bn == 0
            per = Cs // bn
            b_spec = pl.BlockSpec((None, bk, bn), lambda i, j, k: (j // per, 0, j % per))
        else:
            b_spec = pl.BlockSpec((bk, bn), lambda i, j, k: (k, j))
    in_specs = [a_spec, b_spec]
    args = [a, b]
    if residual is not None:
        in_specs.append(pl.BlockSpec((bm, bn), lambda i, j, k: (i, j)))
        args.append(residual)
    if out_shards is not None:
        So, Co = out_shards
        assert So * Co == N and Co % bn == 0 and bm == M
        pero = Co // bn
        out_shape = jax.ShapeDtypeStruct((So, M, Co), out_dtype)
        out_spec = pl.BlockSpec((None, bm, bn), lambda i, j, k: (j // pero, 0, j % pero))
    else:
        out_shape = jax.ShapeDtypeStruct((M, N), out_dtype)
        out_spec = pl.BlockSpec((bm, bn), lambda i, j, k: (i, j))

    assert mode == "nn" and residual is None and out_dtype == F32

    def body(a_ref, b_ref, o_ref):
        @pl.when(pl.program_id(2) == 0)
        def _():
            o_ref[...] = jnp.zeros_like(o_ref)

        o_ref[...] += jnp.dot(a_ref[...].astype(BF16), b_ref[...].astype(BF16), preferred_element_type=F32)

    return pl.pallas_call(
        body, name=name, grid=(nm, nn, nk), in_specs=in_specs, out_specs=out_spec, out_shape=out_shape,
        compiler_params=_cparams(("parallel", "parallel", "arbitrary")),
    )(*args)


def _norm_fwd(h, w, *, bm, name):
    T = h.shape[0]

    def body(h_ref, w_ref, o_ref):
        x = h_ref[...]
        r = lax.rsqrt(jnp.mean(x * x, axis=-1, keepdims=True) + EPS)
        o_ref[...] = (x * r * w_ref[...]).astype(BF16)

    return pl.pallas_call(
        body, name=name, grid=(T // bm,),
        in_specs=[pl.BlockSpec((bm, D), lambda i: (i, 0)), pl.BlockSpec((1, D), lambda i: (0, 0))],
        out_specs=pl.BlockSpec((bm, D), lambda i: (i, 0)),
        out_shape=jax.ShapeDtypeStruct((T, D), BF16), compiler_params=_cparams(("parallel",)),
    )(h, w)


def _norm_bwd(h, du, dres, w, *, bm, name):
    T = h.shape[0]

    def body(h_ref, du_ref, dres_ref, w_ref, dh_ref, dw_ref):
        i = pl.program_id(0)
        x = h_ref[...]
        r = lax.rsqrt(jnp.mean(x * x, axis=-1, keepdims=True) + EPS)
        n = x * r
        du_v = du_ref[...]
        dn = du_v * w_ref[...]
        dh_ref[...] = dres_ref[...] + r * (dn - n * jnp.mean(dn * n, axis=-1, keepdims=True))
        part = jnp.sum(du_v * n, axis=0, keepdims=True)

        @pl.when(i == 0)
        def _():
            dw_ref[...] = part

        @pl.when(i > 0)
        def _():
            dw_ref[...] += part

    row = pl.BlockSpec((bm, D), lambda i: (i, 0))
    vec = pl.BlockSpec((1, D), lambda i: (0, 0))
    return pl.pallas_call(
        body, name=name, grid=(T // bm,), in_specs=[row, row, row, vec], out_specs=[row, vec],
        out_shape=[jax.ShapeDtypeStruct((T, D), F32), jax.ShapeDtypeStruct((1, D), F32)],
        compiler_params=_cparams(("arbitrary",)),
    )(h, du, dres, w)


def _input_grad_block(dproj, x_seq, head_tile, dres, w, w_in_g, psums):
    T = TR + x_seq.shape[0]
    nt = T // TR
    nsl = w_in_g.shape[0]
    wsl = w_in_g.shape[2]
    wcol = wsl // 2
    nex = len(psums)

    def body(*refs):
        dp_ref, x_ref, head_ref, dres_ref, w_ref, win_ref = refs[:6]
        gx_ref, dmeta_ref, dw_ref = refs[6 + nex:9 + nex]
        exchange = _ChipExchange(refs[6:6 + nex], refs[9 + nex:9 + 2 * nex], refs[-2], refs[-1])
        i = pl.program_id(0)

        @pl.when(i == 0)
        def _():
            exchange.send()

        @pl.when(i == nt - 1)
        def _():
            exchange.finish()

        x = jnp.where(i == 0, head_ref[...], x_ref[...])
        r = lax.rsqrt(jnp.mean(x * x, axis=-1, keepdims=True) + EPS)
        n = x * r
        du_v = None
        for s in range(nsl):
            for hf in range(2):
                part = _dot_nt(dp_ref[:, s * wsl + hf * wcol:s * wsl + (hf + 1) * wcol],
                               win_ref[s, :, hf * wcol:(hf + 1) * wcol])
                du_v = part if du_v is None else du_v + part
        dn = du_v * w_ref[...]
        dh = dres_ref[...] + r * (dn - n * jnp.mean(dn * n, axis=-1, keepdims=True))
        gx_ref[...] = dh
        part = jnp.sum(du_v * n, axis=0, keepdims=True)

        @pl.when(i == 0)
        def _():
            dmeta_ref[...] = dh[PAD:TR, :]
            dw_ref[...] = part

        @pl.when(i > 0)
        def _():
            dw_ref[...] += part

    row = pl.BlockSpec((TR, D), lambda i: (i, 0))
    vec = pl.BlockSpec((1, D), lambda i: (0, 0))
    hbm = pl.BlockSpec(memory_space=pl.ANY)
    return pl.pallas_call(
        body, name="input_grad_block", grid=(nt,),
        in_specs=[pl.BlockSpec((TR, 9 * D), lambda i: (i, 0)),
                  pl.BlockSpec((TR, D), lambda i: (jnp.maximum(i - 1, 0), 0)), pl.BlockSpec((TR, D), lambda i: (0, 0)),
                  row, vec, pl.BlockSpec(memory_space=pltpu.VMEM)] + [hbm] * nex,
        out_specs=[pl.BlockSpec((TR, D), lambda i: (jnp.maximum(i - 1, 0), 0)),
                   pl.BlockSpec((NMETA, D), lambda i: (0, 0)), vec] + [hbm] * nex,
        out_shape=[jax.ShapeDtypeStruct((T - TR, D), F32), jax.ShapeDtypeStruct((NMETA, D), F32),
                   jax.ShapeDtypeStruct((1, D), F32)]
        + [jax.ShapeDtypeStruct((3,) + p.shape[1:], p.dtype) for p in psums],
        scratch_shapes=[pltpu.SemaphoreType.DMA((nex, 3)), pltpu.SemaphoreType.DMA((nex, 3))],
        compiler_params=_cparams(("arbitrary",)),
    )(dproj, x_seq, head_tile, dres, w, w_in_g, *psums)


def _split3(x):
    hi = x.astype(BF16)
    r1 = x - hi.astype(F32)
    mid = r1.astype(BF16)
    lo = (r1 - mid.astype(F32)).astype(BF16)
    return hi, mid, lo


def _tri_matmul(tri_bf16, x):
    hi, mid, lo = _split3(x)
    out = jnp.dot(tri_bf16, lo, preferred_element_type=F32)
    out = out + jnp.dot(tri_bf16, mid, preferred_element_type=F32)
    return out + jnp.dot(tri_bf16, hi, preferred_element_type=F32)


def _shift_down(x, prev8, n):
    rows = x.shape[0]
    rid = lax.broadcasted_iota(jnp.int32, x.shape, 0)
    y = pltpu.roll(x, n, 0)
    for t in range(n):
        y = jnp.where(rid == t, prev8[8 - n + t:8 - n + t + 1, :], y)
    del rows
    return y


def _shift_up(x, next8, n):
    rows = x.shape[0]
    rid = lax.broadcasted_iota(jnp.int32, x.shape, 0)
    y = pltpu.roll(x, rows - n, 0)
    for t in range(n):
        y = jnp.where(rid == rows - n + t, next8[t:t + 1, :], y)
    return y


def _gates(f_raw, lb):
    sg = _sigmoid(f_raw)
    f = lb + (1.0 - lb) * sg
    return sg, f, jnp.log(f), 1.0 - f


def _lower_bound(lbp_ref):
    return _sigmoid(lbp_ref[0:1, :] - lbp_ref[1:2, :])


def _tri_masks():
    r = lax.broadcasted_iota(jnp.int32, (CH, CH), 0)
    c = lax.broadcasted_iota(jnp.int32, (CH, CH), 1)
    return r >= c, r <= c


def _ones_where(mask):
    return jnp.where(mask, 1.0, 0.0).astype(BF16)


def _dot(a, b):
    return jnp.dot(a.astype(BF16), b.astype(BF16), preferred_element_type=F32)


def _dot_nt(a, b):
    return lax.dot_general(a.astype(BF16), b.astype(BF16), (((1,), (1,)), ((), ())), preferred_element_type=F32)


def _dot_tn(a, b):
    return lax.dot_general(a.astype(BF16), b.astype(BF16), (((0,), (0,)), ((), ())), preferred_element_type=F32)


def _mix_block_fwd(x_seq, head_tile, lb_param, wattn, whn, conv_w, w_in_g, w_out, late):
    T = TR + x_seq.shape[0]
    nt = T // TR
    ncht = TR // CH
    nsl = w_in_g.shape[0]
    wsl = w_in_g.shape[2]
    wcol = wsl // 2

    nlate = len(late)

    def body(*refs):
        x_ref, head_ref, lbp_ref, wa_ref, whn_ref, cw_ref, win_ref, wout_ref = refs[:8]
        ut_ref, projb_ref, o_ref, sst_ref, mt_ref, h1_ref = refs[8 + nlate:14 + nlate]
        late_refs = refs[14 + nlate:14 + 2 * nlate]
        proj_ref, m_ref, st_ref, cxc_ref, send_sems, recv_sems = refs[14 + 2 * nlate:]
        i = pl.program_id(0)
        gather = _Gather(late_refs, send_sems, recv_sems, (False,) * nlate)

        @pl.when(i == 0)
        def _():
            st_ref[...] = jnp.zeros_like(st_ref)
            cxc_ref[...] = jnp.zeros_like(cxc_ref)
            gather.send()

        @pl.when(i == nt // 2)
        def _():
            gather.relay()

        x = jnp.where(i == 0, head_ref[...], x_ref[...])
        r1 = lax.rsqrt(jnp.mean(x * x, axis=-1, keepdims=True) + EPS)
        u_f = x * r1 * wa_ref[...]
        u = u_f.astype(BF16)
        ut_ref[...] = u_f.T.astype(BF16)
        for s in range(nsl):
            for hf in range(2):
                cols = slice(s * wsl + hf * wcol, s * wsl + (hf + 1) * wcol)
                p = jnp.dot(u, win_ref[s, :, hf * wcol:(hf + 1) * wcol], preferred_element_type=F32)
                proj_ref[:, cols] = p
                projb_ref[:, cols] = p.astype(BF16)

        lb = _lower_bound(lbp_ref)
        lower, _ = _tri_masks()
        ltri = _ones_where(lower)
        whn_v = whn_ref[...]
        w0, w1, w2 = cw_ref[0:1, :], cw_ref[1:2, :], cw_ref[2:3, :]

        def chunk(c, carry):
            rows = pl.ds(pl.multiple_of(c * CH, CH), CH)
            q_raw = proj_ref[rows, 0:D]
            f_raw = proj_ref[rows, D:2 * D]
            v = proj_ref[rows, 2 * D:3 * D]
            q = q_raw * _sigmoid(q_raw)
            _, _, g, k = _gates(f_raw, lb)
            gam = _tri_matmul(ltri, g)
            gam_l = gam[CH - 1:CH, :]
            e_l = jnp.exp(gam_l)
            qt = (q * jnp.exp(gam)).astype(BF16)
            kt = (k * jnp.exp(-gam)).astype(BF16)
            khat = (k * jnp.exp(gam_l - gam)).astype(BF16)
            vb = v.astype(BF16)
            on_parts = []
            for h in range(NH):
                cs = slice(h * HD, (h + 1) * HD)
                st = st_ref[h]
                sst_ref[c, h] = st
                a = jnp.where(lower, _dot_nt(qt[:, cs], kt[:, cs]), 0.0)
                o_h = _dot_nt(qt[:, cs], st) + _dot(a, vb[:, cs])
                st_ref[h] = st * e_l[:, cs] + _dot_tn(vb[:, cs], khat[:, cs])
                o_ref[rows, cs] = o_h
                ro = lax.rsqrt(jnp.mean(o_h * o_h, axis=-1, keepdims=True) + EPS)
                on_parts.append(o_h * ro)
            on = jnp.concatenate(on_parts, axis=1)
            g_out = proj_ref[rows, 3 * D:4 * D]
            y_a = on * whn_v * (g_out * _sigmoid(g_out))
            cx = proj_ref[rows, 5 * D:6 * D] * proj_ref[rows, 6 * D:7 * D]
            prev8 = cxc_ref[...]
            cv = w0 * _shift_down(cx, prev8, 2) + w1 * _shift_down(cx, prev8, 1) + w2 * cx
            cxc_ref[...] = cx[CH - 8:CH, :]
            y_b = proj_ref[rows, 4 * D:5 * D] * cv
            m = _sigmoid(proj_ref[rows, 7 * D:8 * D]) * y_a + _sigmoid(proj_ref[rows, 8 * D:9 * D]) * y_b
            m_ref[rows, :] = m
            return carry

        lax.fori_loop(0, ncht, chunk, 0)
        m_v = m_ref[...]
        h1_ref[...] = x + jnp.dot(m_v.astype(BF16), wout_ref[...], preferred_element_type=F32)
        mt_ref[...] = m_v.T.astype(BF16)

        @pl.when(i == nt - 1)
        def _():
            gather.finish()

    row = lambda w: pl.BlockSpec((TR, w), lambda i: (i, 0))
    col = lambda w: pl.BlockSpec((w, TR), lambda i: (0, i))
    vec = lambda r: pl.BlockSpec((r, D), lambda i: (0, 0))
    vm = pl.BlockSpec(memory_space=pltpu.VMEM)
    hbm = pl.BlockSpec(memory_space=pl.ANY)
    return pl.pallas_call(
        body, name="mix_block_fwd", grid=(nt,),
        in_specs=[pl.BlockSpec((TR, D), lambda i: (jnp.maximum(i - 1, 0), 0)), pl.BlockSpec((TR, D), lambda i: (0, 0)),
                  vec(2), vec(1), vec(1), vec(3), vm, vm] + [hbm] * nlate,
        out_specs=[col(D), row(9 * D), row(D), pl.BlockSpec((ncht, NH, HD, HD), lambda i: (i, 0, 0, 0)),
                   col(D), row(D)] + [hbm] * nlate,
        out_shape=[jax.ShapeDtypeStruct((D, T), BF16), jax.ShapeDtypeStruct((T, 9 * D), BF16),
                   jax.ShapeDtypeStruct((T, D), F32), jax.ShapeDtypeStruct((T // CH, NH, HD, HD), F32),
                   jax.ShapeDtypeStruct((D, T), BF16), jax.ShapeDtypeStruct((T, D), F32)]
        + [jax.ShapeDtypeStruct(a.shape, a.dtype) for a in late],
        input_output_aliases={8 + n: 6 + n for n in range(nlate)},
        scratch_shapes=[pltpu.VMEM((TR, 9 * D), F32), pltpu.VMEM((TR, D), F32), pltpu.VMEM((NH, HD, HD), F32),
                        pltpu.VMEM((8, D), F32), pltpu.SemaphoreType.DMA((nlate, 6)),
                        pltpu.SemaphoreType.DMA((nlate, 6))],
        compiler_params=_cparams(("arbitrary",)),
    )(x_seq, head_tile, lb_param, wattn, whn, conv_w, w_in_g, w_out, *late)


def _mix_block_bwd(dh1, projb, o, sst, lb_param, whn, conv_w, w_out, psums):
    T = projb.shape[0]
    nt = T // TR
    ncht = TR // CH
    tb16 = TR // 16
    nex = len(psums)

    def body(*refs):
        dh1_ref, proj_ref, pc_ref, px_ref, o_ref, sst_ref, lbp_ref, whn_ref, cw_ref, wout_ref = refs[:10]
        dproj_ref, dlb_ref, dwhn_ref, dcw_ref = refs[10 + nex:14 + nex]
        exchange = _ChipExchange(refs[10:10 + nex], refs[14 + nex:14 + 2 * nex], refs[-2], refs[-1])
        dm_ref, dst_ref, dcvc_ref, acc_lb, acc_hn, acc_cw = refs[14 + 2 * nex:-2]
        s = pl.program_id(0)
        tile = nt - 1 - s

        @pl.when(s == 0)
        def _():
            dst_ref[...] = jnp.zeros_like(dst_ref)
            dcvc_ref[...] = jnp.zeros_like(dcvc_ref)
            acc_lb[...] = jnp.zeros_like(acc_lb)
            acc_hn[...] = jnp.zeros_like(acc_hn)
            acc_cw[...] = jnp.zeros_like(acc_cw)
            exchange.send()

        dm_ref[...] = _dot_nt(dh1_ref[...], wout_ref[...])
        lb = _lower_bound(lbp_ref)
        lower, upper = _tri_masks()
        ltri = _ones_where(lower)
        utri = _ones_where(upper)
        whn_v = whn_ref[...]
        w0, w1, w2 = cw_ref[0:1, :], cw_ref[1:2, :], cw_ref[2:3, :]
        cx_before_tile = jnp.where(tile > 0, (pc_ref[...].astype(F32) * px_ref[...].astype(F32))[8:16, :], 0.0)
        rid = lax.broadcasted_iota(jnp.int32, (CH, D), 0)

        def chunk(cc, carry):
            c = ncht - 1 - cc
            r0 = pl.multiple_of(c * CH, CH)
            rows = pl.ds(r0, CH)
            slab = lambda n: proj_ref[rows, n * D:(n + 1) * D].astype(F32)
            q_raw, f_raw, v, g_out, b_gate, c_gate, x_conv = (slab(n) for n in range(7))
            sa = _sigmoid(slab(7))
            sb = _sigmoid(slab(8))
            dm_v = dm_ref[rows, :]

            sq = _sigmoid(q_raw)
            q = q_raw * sq
            sg, f, g, k = _gates(f_raw, lb)
            gam = _tri_matmul(ltri, g)
            gam_l = gam[CH - 1:CH, :]
            e_l = jnp.exp(gam_l)
            e_g = jnp.exp(gam)
            e_ng = jnp.exp(-gam)
            e_kl = jnp.exp(gam_l - gam)
            qt = q * e_g
            kt = k * e_ng
            khat = k * e_kl
            s_go = _sigmoid(g_out)
            silu_go = g_out * s_go
            cx = c_gate * x_conv
            rprev = pl.ds(pl.multiple_of(jnp.maximum(r0 - 16, 0), 16), 16)
            cx_prev_in = (proj_ref[rprev, 5 * D:6 * D].astype(F32) * proj_ref[rprev, 6 * D:7 * D].astype(F32))[8:16, :]
            prev8 = jnp.where(c > 0, cx_prev_in, cx_before_tile)
            cx_m1 = _shift_down(cx, prev8, 1)
            cx_m2 = _shift_down(cx, prev8, 2)
            cv = w0 * cx_m2 + w1 * cx_m1 + w2 * cx
            y_b = b_gate * cv

            o_v = o_ref[rows, :]
            ro_parts, on_parts = [], []
            for h in range(NH):
                cs = slice(h * HD, (h + 1) * HD)
                o_h = o_v[:, cs]
                ro = lax.rsqrt(jnp.mean(o_h * o_h, axis=-1, keepdims=True) + EPS)
                ro_parts.append(ro)
                on_parts.append(o_h * ro)
            on = jnp.concatenate(on_parts, axis=1)
            y_a = on * whn_v * silu_go

            dy_a = dm_v * sa
            dy_b = dm_v * sb
            dproj_ref[rows, 7 * D:8 * D] = (dm_v * y_a * sa * (1.0 - sa)).astype(BF16)
            dproj_ref[rows, 8 * D:9 * D] = (dm_v * y_b * sb * (1.0 - sb)).astype(BF16)
            dproj_ref[rows, 4 * D:5 * D] = (dy_b * cv).astype(BF16)
            dcv = dy_b * b_gate
            acc_cw[0:1, :] += jnp.sum(dcv * cx_m2, axis=0, keepdims=True)
            acc_cw[1:2, :] += jnp.sum(dcv * cx_m1, axis=0, keepdims=True)
            acc_cw[2:3, :] += jnp.sum(dcv * cx, axis=0, keepdims=True)
            next8 = dcvc_ref[...]
            dcx = w2 * dcv + w1 * _shift_up(dcv, next8, 1) + w0 * _shift_up(dcv, next8, 2)
            dcvc_ref[...] = dcv[0:8, :]
            dproj_ref[rows, 5 * D:6 * D] = (dcx * x_conv).astype(BF16)
            dproj_ref[rows, 6 * D:7 * D] = (dcx * c_gate).astype(BF16)
            don = dy_a * whn_v * silu_go
            dproj_ref[rows, 3 * D:4 * D] = (dy_a * on * whn_v * (s_go * (1.0 + g_out * (1.0 - s_go)))).astype(BF16)
            acc_hn[...] += jnp.sum(dy_a * silu_go * on, axis=0, keepdims=True)

            dq_parts, dk_parts, dv_parts, dgam_parts, ext_parts = [], [], [], [], []
            for h in range(NH):
                cs = slice(h * HD, (h + 1) * HD)
                on_h = on_parts[h]
                don_h = don[:, cs]
                do_h = ro_parts[h] * (don_h - on_h * jnp.mean(don_h * on_h, axis=-1, keepdims=True))
                qt_h, kt_h, khat_h, v_h = qt[:, cs], kt[:, cs], khat[:, cs], v[:, cs]
                st = sst_ref[c, h]
                dstn = dst_ref[h]
                a_t = jnp.where(upper, _dot_nt(kt_h, qt_h), 0.0)
                da = jnp.where(lower, _dot_nt(do_h, v_h), 0.0)
                da_t = jnp.where(upper, _dot_nt(v_h, do_h), 0.0)
                dv_h = _dot(a_t, do_h) + _dot_nt(khat_h, dstn)
                dqt_state = _dot(do_h, st)
                dqt_chunk = _dot(da, kt_h)
                dkt = _dot(da_t, qt_h)
                dkhat = _dot(v_h, dstn)
                dq_h = (dqt_state + dqt_chunk) * e_g[:, cs]
                dk_h = dkt * e_ng[:, cs] + dkhat * e_kl[:, cs]
                khat_dkhat = dkhat * khat_h
                ext = (jnp.sum(khat_dkhat, axis=0, keepdims=True)
                       + e_l[:, cs] * jnp.sum(st * dstn, axis=0, keepdims=True))
                dst_ref[h] = _dot_tn(do_h, qt_h) + dstn * e_l[:, cs]
                dq_parts.append(dq_h)
                dk_parts.append(dk_h)
                dv_parts.append(dv_h)
                qt_seen = qt_h.astype(BF16).astype(F32)
                kt_seen = kt_h.astype(BF16).astype(F32)
                dgam_parts.append(qt_h * dqt_state + qt_seen * dqt_chunk - kt_seen * dkt - khat_dkhat)
                ext_parts.append(ext)
            dq = jnp.concatenate(dq_parts, axis=1)
            dk = jnp.concatenate(dk_parts, axis=1)
            dgam = jnp.concatenate(dgam_parts, axis=1)
            ext = jnp.concatenate(ext_parts, axis=1)
            dgam = dgam + jnp.where(rid == CH - 1, ext, 0.0)
            dg = _tri_matmul(utri, dgam)
            dproj_ref[rows, 0:D] = (dq * (sq * (1.0 + q_raw * (1.0 - sq)))).astype(BF16)
            df = dg / f - dk
            dproj_ref[rows, D:2 * D] = (df * (1.0 - lb) * sg * (1.0 - sg)).astype(BF16)
            dproj_ref[rows, 2 * D:3 * D] = jnp.concatenate(dv_parts, axis=1).astype(BF16)
            real = (tile * TR + r0 + rid) >= PAD
            acc_lb[...] += jnp.sum(jnp.where(real, df * (1.0 - sg), 0.0), axis=0, keepdims=True)
            return carry

        lax.fori_loop(0, ncht, chunk, 0)

        @pl.when(s == nt - 1)
        def _():
            dlb_ref[...] = acc_lb[...] * lb * (1.0 - lb)
            hn = acc_hn[...]
            tot = hn[:, 0:HD]
            for h in range(1, NH):
                tot = tot + hn[:, h * HD:(h + 1) * HD]
            dwhn_ref[...] = tot
            dcw_ref[...] = acc_cw[0:3, :]
            exchange.finish()

    hbm = pl.BlockSpec(memory_space=pl.ANY)
    rev = lambda s: (nt - 1 - s, 0)
    prevc = lambda s: (jnp.maximum((nt - 1 - s) * tb16 - 1, 0), 5)
    prevx = lambda s: (jnp.maximum((nt - 1 - s) * tb16 - 1, 0), 6)
    const = lambda s: (0, 0)
    return pl.pallas_call(
        body, name="mix_block_bwd", grid=(nt,),
        in_specs=[pl.BlockSpec((TR, D), rev),
                  pl.BlockSpec((TR, 9 * D), rev),
                  pl.BlockSpec((16, D), prevc),
                  pl.BlockSpec((16, D), prevx),
                  pl.BlockSpec((TR, D), rev),
                  pl.BlockSpec((ncht, NH, HD, HD), lambda s: (nt - 1 - s, 0, 0, 0)),
                  pl.BlockSpec((2, D), const),
                  pl.BlockSpec((1, D), const),
                  pl.BlockSpec((3, D), const),
                  pl.BlockSpec(memory_space=pltpu.VMEM)] + [hbm] * nex,
        out_specs=[pl.BlockSpec((TR, 9 * D), rev),
                   pl.BlockSpec((1, D), const),
                   pl.BlockSpec((1, HD), const),
                   pl.BlockSpec((3, D), const)] + [hbm] * nex,
        out_shape=[jax.ShapeDtypeStruct((T, 9 * D), BF16), jax.ShapeDtypeStruct((1, D), F32),
                   jax.ShapeDtypeStruct((1, HD), F32), jax.ShapeDtypeStruct((3, D), F32)]
        + [jax.ShapeDtypeStruct((3,) + p.shape[1:], p.dtype) for p in psums],
        scratch_shapes=[pltpu.VMEM((TR, D), F32), pltpu.VMEM((NH, HD, HD), F32), pltpu.VMEM((8, D), F32),
                        pltpu.VMEM((1, D), F32), pltpu.VMEM((1, D), F32), pltpu.VMEM((8, D), F32),
                        pltpu.SemaphoreType.DMA((nex, 3)), pltpu.SemaphoreType.DMA((nex, 3))],
        compiler_params=_cparams(("arbitrary",)),
    )(dh1, projb, projb, projb, o, sst, lb_param, whn, conv_w, w_out, *psums)


def _ffn_fwd(up, fcw, fcb):
    T = up.shape[0]
    nt = T // TR

    def body(up_ref, w_ref, b_ref, gg_ref, carry_ref):
        i = pl.program_id(0)

        @pl.when(i == 0)
        def _():
            carry_ref[...] = jnp.zeros_like(carry_ref)

        w0, w1, w2 = w_ref[0:1, :], w_ref[1:2, :], w_ref[2:3, :]

        def chunk(c, carry):
            rows = pl.ds(pl.multiple_of(c * CH, CH), CH)
            a_pre = up_ref[rows, 0:DFF]
            val = up_ref[rows, DFF:2 * DFF]
            prev8 = carry_ref[...]
            a = w0 * _shift_down(a_pre, prev8, 2) + w1 * _shift_down(a_pre, prev8, 1) + w2 * a_pre + b_ref[...]
            carry_ref[...] = a_pre[CH - 8:CH, :]
            gg_ref[rows, :] = (a * _sigmoid(a) * val).astype(BF16)
            return carry

        lax.fori_loop(0, TR // CH, chunk, 0)

    return pl.pallas_call(
        body, name="ffn_fwd", grid=(nt,),
        in_specs=[pl.BlockSpec((TR, 2 * DFF), lambda i: (i, 0)),
                  pl.BlockSpec((3, DFF), lambda i: (0, 0)),
                  pl.BlockSpec((1, DFF), lambda i: (0, 0))],
        out_specs=pl.BlockSpec((TR, DFF), lambda i: (i, 0)),
        out_shape=jax.ShapeDtypeStruct((T, DFF), BF16),
        scratch_shapes=[pltpu.VMEM((8, DFF), F32)],
        compiler_params=_cparams(("arbitrary",)),
    )(up, fcw, fcb)


def _ffn_bwd(up, dgg, fcw, fcb):
    T = up.shape[0]
    nt = T // TR
    tb = TR // 8
    ncht = TR // CH

    def body(up_ref, pa_ref, dgg_ref, w_ref, b_ref, dup_ref, dfw_ref, dfb_ref, carry_ref, acc_w, acc_b):
        s = pl.program_id(0)
        tile = nt - 1 - s

        @pl.when(s == 0)
        def _():
            carry_ref[...] = jnp.zeros_like(carry_ref)
            acc_w[...] = jnp.zeros_like(acc_w)
            acc_b[...] = jnp.zeros_like(acc_b)

        w0, w1, w2 = w_ref[0:1, :], w_ref[1:2, :], w_ref[2:3, :]
        a_before_tile = jnp.where(tile > 0, pa_ref[...], 0.0)

        def chunk(cc, carry):
            c = ncht - 1 - cc
            r0 = pl.multiple_of(c * CH, CH)
            rows = pl.ds(r0, CH)
            a_pre = up_ref[rows, 0:DFF]
            val = up_ref[rows, DFF:2 * DFF]
            rprev = pl.multiple_of(jnp.maximum(r0 - 8, 0), 8)
            prev8 = jnp.where(c > 0, up_ref[pl.ds(rprev, 8), 0:DFF], a_before_tile)
            a_m1 = _shift_down(a_pre, prev8, 1)
            a_m2 = _shift_down(a_pre, prev8, 2)
            a = w0 * a_m2 + w1 * a_m1 + w2 * a_pre + b_ref[...]
            sig = _sigmoid(a)
            dgg_v = dgg_ref[rows, :]
            da = dgg_v * val * (sig * (1.0 + a * (1.0 - sig)))
            dup_ref[rows, DFF:2 * DFF] = (dgg_v * (a * sig)).astype(BF16)
            next8 = carry_ref[...]
            dup_ref[rows, 0:DFF] = (w2 * da + w1 * _shift_up(da, next8, 1) + w0 * _shift_up(da, next8, 2)).astype(BF16)
            carry_ref[...] = da[0:8, :]
            acc_w[0:1, :] += jnp.sum(da * a_m2, axis=0, keepdims=True)
            acc_w[1:2, :] += jnp.sum(da * a_m1, axis=0, keepdims=True)
            acc_w[2:3, :] += jnp.sum(da * a_pre, axis=0, keepdims=True)
            acc_b[...] += jnp.sum(da, axis=0, keepdims=True)
            return carry

        lax.fori_loop(0, ncht, chunk, 0)

        @pl.when(s == nt - 1)
        def _():
            dfw_ref[...] = acc_w[0:3, :]
            dfb_ref[...] = acc_b[...]

    rev = lambda s: (nt - 1 - s, 0)
    const = lambda s: (0, 0)
    return pl.pallas_call(
        body, name="ffn_bwd", grid=(nt,),
        in_specs=[pl.BlockSpec((TR, 2 * DFF), rev),
                  pl.BlockSpec((8, DFF), lambda s: (jnp.maximum((nt - 1 - s) * tb - 1, 0), 0)),
                  pl.BlockSpec((TR, DFF), rev),
                  pl.BlockSpec((3, DFF), const),
                  pl.BlockSpec((1, DFF), const)],
        out_specs=[pl.BlockSpec((TR, 2 * DFF), rev),
                   pl.BlockSpec((3, DFF), const),
                   pl.BlockSpec((1, DFF), const)],
        out_shape=[jax.ShapeDtypeStruct((T, 2 * DFF), BF16), jax.ShapeDtypeStruct((3, DFF), F32),
                   jax.ShapeDtypeStruct((1, DFF), F32)],
        scratch_shapes=[pltpu.VMEM((8, DFF), F32), pltpu.VMEM((8, DFF), F32), pltpu.VMEM((1, DFF), F32)],
        compiler_params=_cparams(("arbitrary",)),
    )(up, up, dgg, fcw, fcb)


def _loss_head(h2, tgt, wfin):
    T = h2.shape[0]
    nt = T // TR

    def body(h_ref, t_ref, w_ref, dh_ref, loss_ref, dw_ref):
        i = pl.program_id(0)

        @pl.when(i == 0)
        def _():
            loss_ref[...] = jnp.zeros_like(loss_ref)
            dw_ref[...] = jnp.zeros_like(dw_ref)

        x = h_ref[...]
        r = lax.rsqrt(jnp.mean(x * x, axis=-1, keepdims=True) + EPS)
        n = x * r
        w = w_ref[...]
        diff = jnp.where(i > 0, n * w - t_ref[...], 0.0)
        loss_ref[...] += jnp.sum(diff * diff, axis=0, keepdims=True) * (0.5 / D)
        dy = diff * (1.0 / D)
        dw_ref[...] += jnp.sum(dy * n, axis=0, keepdims=True)
        dn = dy * w
        dh_ref[...] = r * (dn - n * jnp.mean(dn * n, axis=-1, keepdims=True))

    row = pl.BlockSpec((TR, D), lambda i: (i, 0))
    vec = pl.BlockSpec((1, D), lambda i: (0, 0))
    return pl.pallas_call(
        body, name="loss_head", grid=(nt,),
        in_specs=[row, pl.BlockSpec((TR, D), lambda i: (jnp.maximum(i - 1, 0), 0)), vec],
        out_specs=[row, vec, vec],
        out_shape=[jax.ShapeDtypeStruct((T, D), F32), jax.ShapeDtypeStruct((1, D), F32),
                   jax.ShapeDtypeStruct((1, D), F32)],
        compiler_params=_cparams(("arbitrary",)),
    )(h2, tgt, wfin)


def _ffn_block_fwd(h1, tgt, wffn, w_up_g, fcw, fcb, w_down, wfin):
    T = h1.shape[0]
    nt = T // TR
    nsl = w_up_g.shape[0]
    wsl = w_up_g.shape[2]

    def body(h_ref, t_ref, wn_ref, wup_ref, cw_ref, cb_ref, wdn_ref, wf_ref,
             u2t_ref, upb_ref, ggt_ref, dh_ref, loss_ref, dwf_ref, up_scr, gg_ref, carry_ref):
        i = pl.program_id(0)

        @pl.when(i == 0)
        def _():
            carry_ref[...] = jnp.zeros_like(carry_ref)
            loss_ref[...] = jnp.zeros_like(loss_ref)
            dwf_ref[...] = jnp.zeros_like(dwf_ref)

        x = h_ref[...]
        r2 = lax.rsqrt(jnp.mean(x * x, axis=-1, keepdims=True) + EPS)
        u2_f = x * r2 * wn_ref[...]
        u2 = u2_f.astype(BF16)
        u2t_ref[...] = u2_f.T.astype(BF16)
        for s in range(nsl):
            up_s = jnp.dot(u2, wup_ref[s], preferred_element_type=F32)
            up_scr[:, s * wsl:(s + 1) * wsl] = up_s
            upb_ref[:, s * wsl:(s + 1) * wsl] = up_s.astype(BF16)
        w0, w1, w2 = cw_ref[0:1, :], cw_ref[1:2, :], cw_ref[2:3, :]

        def chunk(c, carry):
            rows = pl.ds(pl.multiple_of(c * CH, CH), CH)
            a_pre = up_scr[rows, 0:DFF]
            val = up_scr[rows, DFF:2 * DFF]
            prev8 = carry_ref[...]
            a = w0 * _shift_down(a_pre, prev8, 2) + w1 * _shift_down(a_pre, prev8, 1) + w2 * a_pre + cb_ref[...]
            carry_ref[...] = a_pre[CH - 8:CH, :]
            gg_ref[rows, :] = a * _sigmoid(a) * val
            return carry

        lax.fori_loop(0, TR // CH, chunk, 0)
        gg_v = gg_ref[...]
        ggt_ref[...] = gg_v.T.astype(BF16)
        h2 = x + jnp.dot(gg_v.astype(BF16), wdn_ref[...], preferred_element_type=F32)
        r3 = lax.rsqrt(jnp.mean(h2 * h2, axis=-1, keepdims=True) + EPS)
        n3 = h2 * r3
        wf = wf_ref[...]
        diff = jnp.where(i > 0, n3 * wf - t_ref[...], 0.0)
        loss_ref[...] += jnp.sum(diff * diff, axis=0, keepdims=True) * (0.5 / D)
        dy = diff * (1.0 / D)
        dwf_ref[...] += jnp.sum(dy * n3, axis=0, keepdims=True)
        dn = dy * wf
        dh_ref[...] = r3 * (dn - n3 * jnp.mean(dn * n3, axis=-1, keepdims=True))

    row = lambda w: pl.BlockSpec((TR, w), lambda i: (i, 0))
    col = lambda w: pl.BlockSpec((w, TR), lambda i: (0, i))
    vec = lambda w, r=1: pl.BlockSpec((r, w), lambda i: (0, 0))
    vm = pl.BlockSpec(memory_space=pltpu.VMEM)
    return pl.pallas_call(
        body, name="ffn_block_fwd", grid=(nt,),
        in_specs=[row(D), pl.BlockSpec((TR, D), lambda i: (jnp.maximum(i - 1, 0), 0)), vec(D), vm,
                  vec(DFF, 3), vec(DFF), vm, vec(D)],
        out_specs=[col(D), row(2 * DFF), col(DFF), row(D), vec(D), vec(D)],
        out_shape=[jax.ShapeDtypeStruct((D, T), BF16), jax.ShapeDtypeStruct((T, 2 * DFF), BF16),
                   jax.ShapeDtypeStruct((DFF, T), BF16), jax.ShapeDtypeStruct((T, D), F32),
                   jax.ShapeDtypeStruct((1, D), F32), jax.ShapeDtypeStruct((1, D), F32)],
        scratch_shapes=[pltpu.VMEM((TR, 2 * DFF), F32), pltpu.VMEM((TR, DFF), F32), pltpu.VMEM((8, DFF), F32)],
        compiler_params=_cparams(("arbitrary",)),
    )(h1, tgt, wffn, w_up_g, fcw, fcb, w_down, wfin)


def _ffn_block_bwd_pipelined(dh2, upb, h1, wffn, w_up_g, fcw, fcb, w_down):
    T = h1.shape[0]
    nt = T // TR
    ncht = TR // CH
    nsl = w_up_g.shape[0]
    wsl = w_up_g.shape[2]
    tb16 = TR // 16
    assert nsl == ncht
    step = -(-DFF // (ncht * 128)) * 128
    parts = [(c0, min(c0 + step, DFF)) for c0 in range(0, DFF, step)]
    assert len(parts) == ncht

    def body(dh2n_ref, dh2p_ref, up_ref, pa_ref, h_ref, wn_ref, wup_ref, cw_ref, cb_ref, wdn_ref,
             dup_ref, dh1_ref, dfw_ref, dfb_ref, dwn_ref,
             dgg_ring, dup_ring, carry_ref, acc_w, acc_b, acc_n):
        s = pl.program_id(0)
        slot = lax.rem(s, 2)
        other = 1 - slot

        @pl.when(s == 0)
        def _():
            carry_ref[...] = jnp.zeros_like(carry_ref)
            acc_w[...] = jnp.zeros_like(acc_w)
            acc_b[...] = jnp.zeros_like(acc_b)
            acc_n[...] = jnp.zeros_like(acc_n)
            dup_ring[1] = jnp.zeros((TR, 2 * DFF), BF16)
            dgg_ring[0] = _dot_nt(dh2p_ref[...], wdn_ref[...])

        def norm_bwd(du2, valid):
            x = h_ref[...]
            dh2p = dh2p_ref[...]
            r2 = lax.rsqrt(jnp.mean(x * x, axis=-1, keepdims=True) + EPS)
            n2 = x * r2
            dn = du2 * wn_ref[...]
            dh1_ref[...] = dh2p + r2 * (dn - n2 * jnp.mean(dn * n2, axis=-1, keepdims=True))
            acc_n[...] += jnp.where(valid, jnp.sum(du2 * n2, axis=0, keepdims=True), 0.0)

        @pl.when(s < nt)
        def _():
            tile = nt - 1 - s
            w0, w1, w2 = cw_ref[0:1, :], cw_ref[1:2, :], cw_ref[2:3, :]
            a_before_tile = jnp.where(tile > 0, pa_ref[...].astype(F32)[8:16, :], 0.0)
            dh2n = dh2n_ref[...].astype(BF16)
            du2 = None
            for idx in range(ncht):
                c = ncht - 1 - idx
                r0 = c * CH
                rows = slice(r0, r0 + CH)
                a_pre = up_ref[rows, 0:DFF].astype(F32)
                val = up_ref[rows, DFF:2 * DFF].astype(F32)
                prev8 = up_ref[r0 - 16:r0, 0:DFF].astype(F32)[8:16, :] if c > 0 else a_before_tile
                a_m1 = _shift_down(a_pre, prev8, 1)
                a_m2 = _shift_down(a_pre, prev8, 2)
                a = w0 * a_m2 + w1 * a_m1 + w2 * a_pre + cb_ref[...]
                sig = _sigmoid(a)
                dgg_v = dgg_ring[slot, rows, :]
                da = dgg_v * val * (sig * (1.0 + a * (1.0 - sig)))
                dval = (dgg_v * (a * sig)).astype(BF16)
                next8 = carry_ref[...]
                da_pre = (w2 * da + w1 * _shift_up(da, next8, 1) + w0 * _shift_up(da, next8, 2)).astype(BF16)
                carry_ref[...] = da[0:8, :]
                dup_ref[rows, 0:DFF] = da_pre
                dup_ref[rows, DFF:2 * DFF] = dval
                dup_ring[slot, rows, 0:DFF] = da_pre
                dup_ring[slot, rows, DFF:2 * DFF] = dval
                acc_w[0:1, :] += jnp.sum(da * a_m2, axis=0, keepdims=True)
                acc_w[1:2, :] += jnp.sum(da * a_m1, axis=0, keepdims=True)
                acc_w[2:3, :] += jnp.sum(da * a_pre, axis=0, keepdims=True)
                acc_b[...] += jnp.sum(da, axis=0, keepdims=True)
                c0, c1 = parts[idx]
                dgg_ring[other, :, c0:c1] = _dot_nt(dh2n, wdn_ref[c0:c1, :])
                part = _dot_nt(dup_ring[other, :, idx * wsl:(idx + 1) * wsl], wup_ref[idx])
                du2 = part if du2 is None else du2 + part
            norm_bwd(du2, s > 0)

        @pl.when(s == nt)
        def _():
            du2 = _dot_nt(dup_ring[other, :, 0:wsl], wup_ref[0])
            for sl in range(1, nsl):
                du2 = du2 + _dot_nt(dup_ring[other, :, sl * wsl:(sl + 1) * wsl], wup_ref[sl])
            norm_bwd(du2, True)
            dfw_ref[...] = acc_w[0:3, :]
            dfb_ref[...] = acc_b[...]
            dwn_ref[...] = acc_n[...]

    gate_tile = lambda s: jnp.maximum(nt - 1 - s, 0)
    next_tile = lambda s: jnp.maximum(nt - 2 - s, 0)
    prev_tile = lambda s: jnp.minimum(nt - s, nt - 1)
    vec = lambda w, r=1: pl.BlockSpec((r, w), lambda s: (0, 0))
    vm = pl.BlockSpec(memory_space=pltpu.VMEM)
    return pl.pallas_call(
        body, name="ffn_block_bwd", grid=(nt + 1,),
        in_specs=[pl.BlockSpec((TR, D), lambda s: (next_tile(s), 0)),
                  pl.BlockSpec((TR, D), lambda s: (prev_tile(s), 0)),
                  pl.BlockSpec((TR, 2 * DFF), lambda s: (gate_tile(s), 0)),
                  pl.BlockSpec((16, DFF), lambda s: (jnp.maximum(gate_tile(s) * tb16 - 1, 0), 0)),
                  pl.BlockSpec((TR, D), lambda s: (prev_tile(s), 0)),
                  vec(D), vm, vec(DFF, 3), vec(DFF), vm],
        out_specs=[pl.BlockSpec((TR, 2 * DFF), lambda s: (gate_tile(s), 0)),
                   pl.BlockSpec((TR, D), lambda s: (prev_tile(s), 0)),
                   vec(DFF, 3), vec(DFF), vec(D)],
        out_shape=[jax.ShapeDtypeStruct((T, 2 * DFF), BF16), jax.ShapeDtypeStruct((T, D), F32),
                   jax.ShapeDtypeStruct((3, DFF), F32), jax.ShapeDtypeStruct((1, DFF), F32),
                   jax.ShapeDtypeStruct((1, D), F32)],
        scratch_shapes=[pltpu.VMEM((2, TR, DFF), F32), pltpu.VMEM((2, TR, 2 * DFF), BF16),
                        pltpu.VMEM((8, DFF), F32), pltpu.VMEM((8, DFF), F32), pltpu.VMEM((1, DFF), F32),
                        pltpu.VMEM((1, D), F32)],
        compiler_params=_cparams(("arbitrary",)),
    )(dh2, dh2, upb, upb, h1, wffn, w_up_g, fcw, fcb, w_down)


def _ffn_block_bwd(dh2, upb, h1, wffn, w_up_g, fcw, fcb, w_down):
    T = h1.shape[0]
    nt = T // TR
    ncht = TR // CH
    nsl = w_up_g.shape[0]
    wsl = w_up_g.shape[2]
    tb16 = TR // 16

    def body(dh2_ref, up_ref, pa_ref, h_ref, wn_ref, wup_ref, cw_ref, cb_ref, wdn_ref,
             dup_ref, dh1_ref, dfw_ref, dfb_ref, dwn_ref, dgg_scr, carry_ref, acc_w, acc_b, acc_n):
        s = pl.program_id(0)
        tile = nt - 1 - s

        @pl.when(s == 0)
        def _():
            carry_ref[...] = jnp.zeros_like(carry_ref)
            acc_w[...] = jnp.zeros_like(acc_w)
            acc_b[...] = jnp.zeros_like(acc_b)
            acc_n[...] = jnp.zeros_like(acc_n)

        dh2 = dh2_ref[...]
        dgg_scr[...] = _dot_nt(dh2, wdn_ref[...])
        w0, w1, w2 = cw_ref[0:1, :], cw_ref[1:2, :], cw_ref[2:3, :]
        a_before_tile = jnp.where(tile > 0, pa_ref[...].astype(F32)[8:16, :], 0.0)

        def chunk(cc, carry):
            c = ncht - 1 - cc
            r0 = pl.multiple_of(c * CH, CH)
            rows = pl.ds(r0, CH)
            a_pre = up_ref[rows, 0:DFF].astype(F32)
            val = up_ref[rows, DFF:2 * DFF].astype(F32)
            rprev = pl.multiple_of(jnp.maximum(r0 - 16, 0), 16)
            prev_in = up_ref[pl.ds(rprev, 16), 0:DFF].astype(F32)[8:16, :]
            prev8 = jnp.where(c > 0, prev_in, a_before_tile)
            a_m1 = _shift_down(a_pre, prev8, 1)
            a_m2 = _shift_down(a_pre, prev8, 2)
            a = w0 * a_m2 + w1 * a_m1 + w2 * a_pre + cb_ref[...]
            sig = _sigmoid(a)
            dgg_v = dgg_scr[rows, :]
            da = dgg_v * val * (sig * (1.0 + a * (1.0 - sig)))
            dup_ref[rows, DFF:2 * DFF] = (dgg_v * (a * sig)).astype(BF16)
            next8 = carry_ref[...]
            dup_ref[rows, 0:DFF] = (w2 * da + w1 * _shift_up(da, next8, 1) + w0 * _shift_up(da, next8, 2)).astype(BF16)
            carry_ref[...] = da[0:8, :]
            acc_w[0:1, :] += jnp.sum(da * a_m2, axis=0, keepdims=True)
            acc_w[1:2, :] += jnp.sum(da * a_m1, axis=0, keepdims=True)
            acc_w[2:3, :] += jnp.sum(da * a_pre, axis=0, keepdims=True)
            acc_b[...] += jnp.sum(da, axis=0, keepdims=True)
            return carry

        lax.fori_loop(0, ncht, chunk, 0)
        du2 = _dot_nt(dup_ref[:, 0:wsl], wup_ref[0])
        for sl in range(1, nsl):
            du2 = du2 + _dot_nt(dup_ref[:, sl * wsl:(sl + 1) * wsl], wup_ref[sl])
        x = h_ref[...]
        r2 = lax.rsqrt(jnp.mean(x * x, axis=-1, keepdims=True) + EPS)
        n2 = x * r2
        dn = du2 * wn_ref[...]
        dh1_ref[...] = dh2 + r2 * (dn - n2 * jnp.mean(dn * n2, axis=-1, keepdims=True))
        acc_n[...] += jnp.sum(du2 * n2, axis=0, keepdims=True)

        @pl.when(s == nt - 1)
        def _():
            dfw_ref[...] = acc_w[0:3, :]
            dfb_ref[...] = acc_b[...]
            dwn_ref[...] = acc_n[...]

    rev = lambda w: pl.BlockSpec((TR, w), lambda s: (nt - 1 - s, 0))
    vec = lambda w, r=1: pl.BlockSpec((r, w), lambda s: (0, 0))
    vm = pl.BlockSpec(memory_space=pltpu.VMEM)
    return pl.pallas_call(
        body, name="ffn_block_bwd", grid=(nt,),
        in_specs=[rev(D), rev(2 * DFF),
                  pl.BlockSpec((16, DFF), lambda s: (jnp.maximum((nt - 1 - s) * tb16 - 1, 0), 0)),
                  rev(D), vec(D), vm, vec(DFF, 3), vec(DFF), vm],
        out_specs=[rev(2 * DFF), rev(D), vec(DFF, 3), vec(DFF), vec(D)],
        out_shape=[jax.ShapeDtypeStruct((T, 2 * DFF), BF16), jax.ShapeDtypeStruct((T, D), F32),
                   jax.ShapeDtypeStruct((3, DFF), F32), jax.ShapeDtypeStruct((1, DFF), F32),
                   jax.ShapeDtypeStruct((1, D), F32)],
        scratch_shapes=[pltpu.VMEM((TR, DFF), F32), pltpu.VMEM((8, DFF), F32), pltpu.VMEM((8, DFF), F32),
                        pltpu.VMEM((1, DFF), F32), pltpu.VMEM((1, D), F32)],
        compiler_params=_cparams(("arbitrary",)),
    )(dh2, upb, upb, h1, wffn, w_up_g, fcw, fcb, w_down)


def _place():
    x, y, c = lax.axis_index("x"), lax.axis_index("y"), lax.axis_index("c")
    return x, y, c


_CHIP_FLIPS = ((1, 0), (0, 1), (1, 1))


def _flip(v, bit):
    return 1 - v if bit else v


def _into_slot(w, j_idx, *, rb, dtype, name):
    r, cdim = w.shape

    def body(j_ref, w_ref, out_ref):
        del j_ref
        out_ref[...] = w_ref[...].astype(dtype)

    grid_spec = pltpu.PrefetchScalarGridSpec(
        num_scalar_prefetch=1, grid=(r // rb,),
        in_specs=[pl.BlockSpec((rb, cdim), lambda i, j_ref: (i, 0))],
        out_specs=pl.BlockSpec((None, rb, cdim), lambda i, j_ref: (j_ref[0], i, 0)))
    return pl.pallas_call(
        body, name=name, grid_spec=grid_spec, out_shape=jax.ShapeDtypeStruct((NSHARD, r, cdim), dtype),
        compiler_params=_cparams(("parallel",)),
    )(j_idx, w)


class _Gather:
    def __init__(self, outs, send_sems, recv_sems, whole):
        self.outs, self.send_sems, self.recv_sems, self.whole = outs, send_sems, recv_sems, whole
        self.x, self.y, self.c = _place()
        self.j = 2 * self.x + self.y
        self.sibling = (self.x, self.y, 1 - self.c)

    def _rows(self, w, core):
        r = self.outs[w].shape[1]
        return pl.ds(0, r) if self.whole[w] else pl.ds(core * (r // 2), r // 2)

    def _copy(self, w, slot, core, sem, to):
        piece = self.outs[w].at[slot, self._rows(w, core), :]
        return pltpu.make_async_remote_copy(
            src_ref=piece, dst_ref=piece, send_sem=self.send_sems.at[w, sem], recv_sem=self.recv_sems.at[w, sem],
            device_id=to, device_id_type=MESH)

    def _chips(self):
        for kk, (fx, fy) in enumerate(_CHIP_FLIPS):
            px, py = _flip(self.x, fx), _flip(self.y, fy)
            yield kk, 2 * px + py, (px, py, self.c)

    def send(self):
        for w in range(len(self.outs)):
            for kk, _, to in self._chips():
                self._copy(w, self.j, self.c, kk, to).start()

    def relay(self):
        for w in range(len(self.outs)):
            for kk, jk, _ in self._chips():
                self._copy(w, jk, self.c, kk, self.sibling).wait_recv()
                if not self.whole[w]:
                    self._copy(w, jk, self.c, 3 + kk, self.sibling).start()

    def finish(self):
        for w in range(len(self.outs)):
            for kk, jk, to in self._chips():
                self._copy(w, self.j, self.c, kk, to).wait_send()
                if not self.whole[w]:
                    self._copy(w, jk, 1 - self.c, 3 + kk, self.sibling).wait_recv()
                    self._copy(w, jk, self.c, 3 + kk, self.sibling).wait_send()


def _allgather_weights(slotted, whole):
    n = len(slotted)

    def body(*refs):
        g = _Gather(refs[n:2 * n], refs[2 * n], refs[2 * n + 1], whole)
        g.send()
        g.relay()
        g.finish()

    any_spec = pl.BlockSpec(memory_space=pl.ANY)
    return pl.pallas_call(
        body, name="allgather_weights",
        in_specs=[any_spec] * n, out_specs=[any_spec] * n,
        out_shape=[jax.ShapeDtypeStruct(a.shape, a.dtype) for a in slotted],
        input_output_aliases={i: i for i in range(n)},
        scratch_shapes=[pltpu.SemaphoreType.DMA((n, 6)), pltpu.SemaphoreType.DMA((n, 6))],
    )(*slotted)


class _ChipExchange:
    def __init__(self, ins, outs, send_sems, recv_sems):
        self.ins, self.outs, self.send_sems, self.recv_sems = ins, outs, send_sems, recv_sems
        self.x, self.y, self.c = _place()

    def _copies(self):
        for w in range(len(self.ins)):
            for kk, (fx, fy) in enumerate(_CHIP_FLIPS):
                px, py = _flip(self.x, fx), _flip(self.y, fy)
                yield pltpu.make_async_remote_copy(
                    src_ref=self.ins[w].at[2 * px + py], dst_ref=self.outs[w].at[kk],
                    send_sem=self.send_sems.at[w, kk], recv_sem=self.recv_sems.at[w, kk],
                    device_id=(px, py, self.c), device_id_type=MESH)

    def send(self):
        for cp in self._copies():
            cp.start()

    def finish(self):
        for cp in self._copies():
            cp.wait()


def _pair_exchange(grads, name):
    nw = len(grads)

    def body(*refs):
        ins, outs = refs[:nw], refs[nw:2 * nw]
        send_sems, recv_sems = refs[2 * nw:]
        x, y, c = _place()
        sibling = (x, y, 1 - c)
        cps = []
        for w in range(nw):
            half = ins[w].shape[1] // 2
            cp = pltpu.make_async_remote_copy(
                src_ref=ins[w].at[:, pl.ds((1 - c) * half, half), :], dst_ref=outs[w],
                send_sem=send_sems.at[w], recv_sem=recv_sems.at[w], device_id=sibling, device_id_type=MESH)
            cp.start()
            cps.append(cp)
        for cp in cps:
            cp.wait()

    any_spec = pl.BlockSpec(memory_space=pl.ANY)
    return pl.pallas_call(
        body, name=name,
        in_specs=[any_spec] * nw, out_specs=[any_spec] * nw,
        out_shape=[jax.ShapeDtypeStruct((g.shape[0], g.shape[1] // 2, g.shape[2]), g.dtype) for g in grads],
        scratch_shapes=[pltpu.SemaphoreType.DMA((nw,)), pltpu.SemaphoreType.DMA((nw,))],
    )(*grads)


def _pair_add(g, other, c_idx, *, rb, name):
    S, r, cdim = g.shape
    half = r // 2
    nb = half // rb

    def body(c_ref, g_ref, o_ref, out_ref):
        del c_ref
        out_ref[...] = (g_ref[...] + o_ref[...]).astype(BF16)

    grid_spec = pltpu.PrefetchScalarGridSpec(
        num_scalar_prefetch=1, grid=(S, nb),
        in_specs=[pl.BlockSpec((None, rb, cdim), lambda s, i, c_ref: (s, c_ref[0] * nb + i, 0)),
                  pl.BlockSpec((None, rb, cdim), lambda s, i, c_ref: (s, i, 0))],
        out_specs=pl.BlockSpec((None, rb, cdim), lambda s, i, c_ref: (s, i, 0)))
    return pl.pallas_call(
        body, name=name, grid_spec=grid_spec, out_shape=jax.ShapeDtypeStruct((S, half, cdim), BF16),
        compiler_params=_cparams(("parallel", "parallel")),
    )(c_idx, g, other)


def _chip_exchange(psums):
    nw = len(psums)

    def body(*refs):
        ex = _ChipExchange(refs[:nw], refs[nw:2 * nw], refs[2 * nw], refs[2 * nw + 1])
        ex.send()
        ex.finish()

    any_spec = pl.BlockSpec(memory_space=pl.ANY)
    return pl.pallas_call(
        body, name="grad_chip_exchange",
        in_specs=[any_spec] * nw, out_specs=[any_spec] * nw,
        out_shape=[jax.ShapeDtypeStruct((3,) + p.shape[1:], p.dtype) for p in psums],
        scratch_shapes=[pltpu.SemaphoreType.DMA((nw, 3)), pltpu.SemaphoreType.DMA((nw, 3))],
    )(*psums)


def _chip_sum(psum, parts, cj_idx, *, rb, name):
    S, half, cdim = psum.shape
    nb = half // rb

    def body(cj_ref, own_ref, p_ref, out_ref):
        del cj_ref
        f = lambda v: v.astype(F32)
        out_ref[...] = ((f(own_ref[...]) + f(p_ref[0])) + f(p_ref[1])) + f(p_ref[2])

    grid_spec = pltpu.PrefetchScalarGridSpec(
        num_scalar_prefetch=1, grid=(nb,),
        in_specs=[pl.BlockSpec((None, rb, cdim), lambda i, cj: (cj[1], i, 0)),
                  pl.BlockSpec((3, rb, cdim), lambda i, cj: (0, i, 0))],
        out_specs=pl.BlockSpec((rb, cdim), lambda i, cj: (cj[0] * nb + i, 0)))
    return pl.pallas_call(
        body, name=name, grid_spec=grid_spec, out_shape=jax.ShapeDtypeStruct((2 * half, cdim), F32),
        compiler_params=_cparams(("parallel",)),
    )(cj_idx, psum, parts)


SLAB_W = 1024


def _final_exchange(pieces, grads):
    n, nw = len(pieces), len(grads)
    segs, at = [], 0
    for idx, p in enumerate(pieces):
        r, wd = p.shape
        for c0 in range(0, wd, SLAB_W):
            if r > 1:
                at = -(-at // 8) * 8
            segs.append((idx, c0, min(SLAB_W, wd - c0), at))
            at += r
    rows = -(-at // 8) * 8
    flips = [(fx, fy, fc) for fx in (0, 1) for fy in (0, 1) for fc in (0, 1)][1:]

    def body(*refs):
        ins = refs[:n]
        outs = refs[n + nw:2 * n + nw]
        g_refs = refs[2 * n + nw:2 * n + 2 * nw]
        mine_ref, slots_ref, send_sems, recv_sems, gsend_sems, grecv_sems = refs[2 * n + 2 * nw:]
        x, y, c = _place()
        me = 4 * x + 2 * y + c
        sibling = (x, y, 1 - c)

        def swap(w, core):
            half = g_refs[w].shape[0] // 2
            rows_ = g_refs[w].at[pl.ds(core * half, half), :]
            return pltpu.make_async_remote_copy(
                src_ref=rows_, dst_ref=rows_, send_sem=gsend_sems.at[w], recv_sem=grecv_sems.at[w],
                device_id=sibling, device_id_type=MESH)

        for w in range(nw):
            swap(w, c).start()
        mine_ref[...] = jnp.zeros_like(mine_ref)
        for idx, c0, wd, st in segs:
            r = ins[idx].shape[0]
            mine_ref[st:st + r, 0:wd] = ins[idx][:, c0:c0 + wd]
        slots_ref[me] = mine_ref[...]
        cps = []
        for kk, (fx, fy, fc) in enumerate(flips):
            cp = pltpu.make_async_remote_copy(
                src_ref=mine_ref, dst_ref=slots_ref.at[me], send_sem=send_sems.at[kk], recv_sem=recv_sems.at[kk],
                device_id=(_flip(x, fx), _flip(y, fy), _flip(c, fc)), device_id_type=MESH)
            cp.start()
            cps.append(cp)
        for cp in cps:
            cp.wait()
        tot = slots_ref[0]
        for d in range(1, 8):
            tot = tot + slots_ref[d]
        mine_ref[...] = tot
        for idx, c0, wd, st in segs:
            r = ins[idx].shape[0]
            val = mine_ref[st:st + r, 0:wd]
            if idx == n - 1:
                outs[idx][...] = jnp.sum(val, keepdims=True)
            else:
                outs[idx][:, c0:c0 + wd] = val
        for w in range(nw):
            swap(w, 1 - c).wait_recv()
            swap(w, c).wait_send()

    vm = pl.BlockSpec(memory_space=pltpu.VMEM)
    hbm = pl.BlockSpec(memory_space=pl.ANY)
    out_shape = ([jax.ShapeDtypeStruct(p.shape, F32) for p in pieces[:-1]] + [jax.ShapeDtypeStruct((1, 1), F32)]
                 + [jax.ShapeDtypeStruct(g.shape, g.dtype) for g in grads])
    res = pl.pallas_call(
        body, name="final_exchange", in_specs=[vm] * n + [hbm] * nw, out_specs=[vm] * n + [hbm] * nw,
        out_shape=out_shape, input_output_aliases={n + w: n + w for w in range(nw)},
        scratch_shapes=[pltpu.VMEM((rows, SLAB_W), F32), pltpu.VMEM((8, rows, SLAB_W), F32),
                        pltpu.SemaphoreType.DMA((7,)), pltpu.SemaphoreType.DMA((7,)),
                        pltpu.SemaphoreType.DMA((nw,)), pltpu.SemaphoreType.DMA((nw,))],
    )(*pieces, *grads)
    return res[:n], res[n:]


def _adamw(w, g, m, v, *, rb, name):
    r, cdim = w.shape

    def body(w_ref, g_ref, m_ref, v_ref, go_ref, d_ref, nm_ref, nv_ref):
        go_ref[...] = g_ref[...]
        d_ref[...], nm_ref[...], nv_ref[...] = _adamw_update(w_ref[...], g_ref[...], m_ref[...], v_ref[...])

    spec = pl.BlockSpec((rb, cdim), lambda i: (i, 0))
    shp = jax.ShapeDtypeStruct((r, cdim), F32)
    return pl.pallas_call(
        body, name=name, grid=(r // rb,), in_specs=[spec] * 4, out_specs=[spec] * 4, out_shape=[shp] * 4,
        compiler_params=_cparams(("parallel",)),
    )(w, g, m, v)


def _adamw_update(w, g, m, v):
    nm = ADAM_B1 * m + (1.0 - ADAM_B1) * g
    nv = ADAM_B2 * v + (1.0 - ADAM_B2) * (g * g)
    m_hat = nm / (1.0 - ADAM_B1 ** ADAM_STEP)
    v_hat = nv / (1.0 - ADAM_B2 ** ADAM_STEP)
    return -ADAM_LR * (m_hat / (jnp.sqrt(v_hat) + ADAM_EPS) + ADAM_WD * w), nm, nv


def _adamw_small(params):
    n = len(params)

    def body(*refs):
        ins, outs = refs[:4 * n], refs[4 * n:]
        for p in range(n):
            w_ref, g_ref, m_ref, v_ref = ins[4 * p:4 * p + 4]
            d, nm, nv = _adamw_update(w_ref[...], g_ref[...], m_ref[...], v_ref[...])
            outs[3 * p][...] = d
            outs[3 * p + 1][...] = nm
            outs[3 * p + 2][...] = nv

    vm = pl.BlockSpec(memory_space=pltpu.VMEM)
    flat = [a for p in params for a in p]
    out_shape = [jax.ShapeDtypeStruct(p[0].shape, F32) for p in params for _ in range(3)]
    res = pl.pallas_call(
        body, name="adamw_small", in_specs=[vm] * (4 * n), out_specs=[vm] * (3 * n), out_shape=out_shape,
    )(*flat)
    return [tuple(res[3 * p:3 * p + 3]) for p in range(n)]


_PAIR_ADD_ROWS = {"w_in": 256, "w_out": 128, "w_up": 256, "w_down": 176}


def _pair_sums(grads, names, c_idx, tag):
    others = _pair_exchange(grads, name=f"grad_pair_exchange_{tag}")
    return [_pair_add(g, o_, c_idx, rb=_PAIR_ADD_ROWS[n], name=f"pair_add_{n}") for g, o_, n in zip(grads, others, names)]


def _local_step(x, tgt, meta_full, lb_param, attn_norm_w, w_in_g, hgrn_norm_w, conv_w_full, w_out_full,
                ffn_norm_w, late_slotted, fcw_full, ffn_conv_b, final_norm_w, c_idx, cj_idx):
    seq = x.shape[0]
    T = TR + seq
    head_tile = jnp.concatenate([jnp.zeros((PAD, D), F32), meta_full], axis=0)
    whn_t = jnp.tile(hgrn_norm_w, (1, NH))

    ut, projb, o, sst, mt, h1, w_up_g, w_down_g = _mix_block_fwd(
        x, head_tile, lb_param, attn_norm_w, whn_t, conv_w_full, w_in_g, w_out_full, late_slotted)
    w_down_full = w_down_g.reshape(DFF, D)
    u2t, upb, ggt, dh2, loss_vec, dwfin = _ffn_block_fwd(
        h1, tgt, ffn_norm_w, w_up_g, fcw_full, ffn_conv_b, w_down_full, final_norm_w.reshape(1, D))

    dup, dh1, dfw, dfb, dwffn = _ffn_block_bwd_pipelined(
        dh2, upb, h1, ffn_norm_w, w_up_g, fcw_full, ffn_conv_b, w_down_full)
    kb = 1408 if T % 1408 == 0 else TR
    g_up = _matmul(u2t, dup, mode="nn", bm=D, bn=1408, bk=kb, out_dtype=F32, name="dw_up_mm",
                   out_shards=(NSHARD, 2 * DFF // NSHARD))
    g_down = _matmul(ggt, dh2, mode="nn", bm=1408, bn=D, bk=kb, out_dtype=F32, name="dw_down_mm")
    ps_ffn = _pair_sums([g_up, g_down.reshape(NSHARD, DFF // NSHARD, D)], ("w_up", "w_down"), c_idx, "ffn")

    dproj, dlb, dwhn, dcw, *parts_ffn = _mix_block_bwd(
        dh1, projb, o, sst, lb_param, whn_t, conv_w_full, w_out_full, ps_ffn)
    g_in = _matmul(ut, dproj, mode="nn", bm=D, bn=1152, bk=kb, out_dtype=F32, name="dw_in_mm",
                   out_shards=(NSHARD, 9 * D // NSHARD))
    g_out = _matmul(mt, dh1, mode="nn", bm=D, bn=D, bk=kb, out_dtype=F32, name="dw_out_mm")
    ps_mix = _pair_sums([g_in, g_out.reshape(NSHARD, D // NSHARD, D)], ("w_in", "w_out"), c_idx, "mix")

    grad_x, dmeta, dwattn, *parts_mix = _input_grad_block(dproj, x, head_tile, dh1, attn_norm_w, w_in_g, ps_mix)

    halves = [_chip_sum(ps, p, cj_idx, rb=_PAIR_ADD_ROWS[n], name=f"chip_sum_{n}")
              for ps, p, n in zip(ps_mix + ps_ffn, parts_mix + parts_ffn, ("w_in", "w_out", "w_up", "w_down"))]
    small = dict(dlb=dlb, dwattn=dwattn, dwhn=dwhn, dwffn=dwffn, dfb=dfb, dwfin=dwfin,
                 dcw=dcw, dfw=dfw, dmeta=dmeta, loss=loss_vec)
    return grad_x, small, halves


_SMALL_ORDER = ("dmeta", "dcw", "dfw", "dlb", "dwattn", "dwhn", "dwffn", "dfb", "dwfin", "loss")


def kernel(x, meta_tokens, lb_param, attn_norm_w, w_in, hgrn_norm_w, conv_w, w_out, ffn_norm_w, w_up, ffn_conv_w, ffn_conv_b, w_down, final_norm_w, loss_target, m_meta_tokens, m_lb_param, m_attn_norm_w, m_w_in, m_hgrn_norm_w, m_conv_w, m_w_out, m_ffn_norm_w, m_w_up, m_ffn_conv_w, m_ffn_conv_b, m_w_down, m_final_norm_w, v_meta_tokens, v_lb_param, v_attn_norm_w, v_w_in, v_hgrn_norm_w, v_conv_w, v_w_out, v_ffn_norm_w, v_w_up, v_ffn_conv_w, v_ffn_conv_b, v_w_down, v_final_norm_w):
    xi, yi, ci = _place()
    j = 2 * xi + yi
    c_idx = jnp.reshape(ci, (1,)).astype(jnp.int32)

    j_idx = jnp.reshape(j, (1,)).astype(jnp.int32)
    ds_, fs_ = D // NSHARD, DFF // NSHARD
    widen = lambda a: jnp.pad(a, ((0, 0), (0, 768 - a.shape[1])))
    rows_small = jnp.concatenate([widen(meta_tokens), widen(conv_w[0]), widen(ffn_conv_w[0]),
                                  jnp.zeros((2, 768), F32)], axis=0)
    slotted = [_into_slot(w[0], j_idx, rb=rb, dtype=BF16, name=f"slot_{n}")
               for w, rb, n in ((w_in, 256, "w_in"), (w_out, 128, "w_out"), (w_up, 256, "w_up"), (w_down, 176, "w_down"))]
    s_in, s_out, s_up, s_down = slotted
    s_small = _into_slot(rows_small, j_idx, rb=rows_small.shape[0], dtype=F32, name="slot_small")
    w_in_g, w_out_g, small_g = _allgather_weights([s_in, s_out, s_small], (False, False, True))
    unshard = lambda a: jnp.transpose(a, (1, 0, 2)).reshape(a.shape[1], -1)
    meta_full = unshard(small_g[:, 0:NMETA, 0:ds_])
    conv_w_full = unshard(small_g[:, NMETA:NMETA + 3, 0:ds_])
    fcw_full = unshard(small_g[:, NMETA + 3:NMETA + 6, 0:fs_])

    cj_idx = jnp.stack([ci, j]).astype(jnp.int32)
    grad_x, small, halves = _local_step(
        x[0], loss_target[0], meta_full, lb_param, attn_norm_w, w_in_g, hgrn_norm_w, conv_w_full,
        w_out_g.reshape(D, D), ffn_norm_w, [s_up, s_down], fcw_full, ffn_conv_b, final_norm_w, c_idx, cj_idx)

    names = _SMALL_ORDER
    small_sums, g_big = _final_exchange([small[n] for n in names], halves)
    vals = dict(zip(names, small_sums))
    loss = vals["loss"].reshape(())
    g_small = {
        "meta_tokens": lax.dynamic_slice_in_dim(vals["dmeta"], j * (D // NSHARD), D // NSHARD, axis=1),
        "lb_param": jnp.concatenate([vals["dlb"], -vals["dlb"]], axis=0),
        "attn_norm_w": vals["dwattn"],
        "hgrn_norm_w": vals["dwhn"],
        "conv_w": lax.dynamic_slice_in_dim(vals["dcw"], j * (D // NSHARD), D // NSHARD, axis=1)[None],
        "ffn_norm_w": vals["dwffn"],
        "ffn_conv_w": lax.dynamic_slice_in_dim(vals["dfw"], j * (DFF // NSHARD), DFF // NSHARD, axis=1)[None],
        "ffn_conv_b": vals["dfb"],
        "final_norm_w": vals["dwfin"].reshape(D),
    }


    weights = {"meta_tokens": meta_tokens, "lb_param": lb_param, "attn_norm_w": attn_norm_w, "w_in": w_in,
               "hgrn_norm_w": hgrn_norm_w, "conv_w": conv_w, "w_out": w_out, "ffn_norm_w": ffn_norm_w,
               "w_up": w_up, "ffn_conv_w": ffn_conv_w, "ffn_conv_b": ffn_conv_b, "w_down": w_down,
               "final_norm_w": final_norm_w}
    ms = {"meta_tokens": m_meta_tokens, "lb_param": m_lb_param, "attn_norm_w": m_attn_norm_w, "w_in": m_w_in,
          "hgrn_norm_w": m_hgrn_norm_w, "conv_w": m_conv_w, "w_out": m_w_out, "ffn_norm_w": m_ffn_norm_w,
          "w_up": m_w_up, "ffn_conv_w": m_ffn_conv_w, "ffn_conv_b": m_ffn_conv_b, "w_down": m_w_down,
          "final_norm_w": m_final_norm_w}
    vs = {"meta_tokens": v_meta_tokens, "lb_param": v_lb_param, "attn_norm_w": v_attn_norm_w, "w_in": v_w_in,
          "hgrn_norm_w": v_hgrn_norm_w, "conv_w": v_conv_w, "w_out": v_w_out, "ffn_norm_w": v_ffn_norm_w,
          "w_up": v_w_up, "ffn_conv_w": v_ffn_conv_w, "ffn_conv_b": v_ffn_conv_b, "w_down": v_w_down,
          "final_norm_w": v_final_norm_w}
    order = list(weights)
    grads, deltas, new_m, new_v = {}, {}, {}, {}

    for name, g, rb in zip(("w_in", "w_out", "w_up", "w_down"), g_big, (256, 128, 256, 176)):
        shp = weights[name].shape
        w2, m2, v2 = (a.reshape(shp[1], shp[2]) for a in (weights[name], ms[name], vs[name]))
        g_, d_, nm_, nv_ = _adamw(w2, g, m2, v2, rb=rb, name=f"adamw_{name}")
        grads[name], deltas[name], new_m[name], new_v[name] = (a.reshape(shp) for a in (g_, d_, nm_, nv_))

    small_names = [n for n in order if n not in grads]
    as2d = lambda a: a.reshape(-1, a.shape[-1])
    res = _adamw_small([tuple(as2d(a) for a in (weights[n], g_small[n], ms[n], vs[n])) for n in small_names])
    for n, (d_, nm_, nv_) in zip(small_names, res):
        shp = weights[n].shape
        grads[n], deltas[n], new_m[n], new_v[n] = (a.reshape(shp) for a in (g_small[n], d_, nm_, nv_))

    return (loss, grad_x[None], *[grads[n] for n in order], *[deltas[n] for n in order],
            *[new_m[n] for n in order], *[new_v[n] for n in order])
```

```python
import functools

import jax
import jax.numpy as jnp
from jax import lax
from jax.experimental import pallas as pl
from jax.experimental.pallas import tpu as pltpu

F32 = jnp.float32
BF16 = jnp.bfloat16
MESH = pl.DeviceIdType.MESH

D = 1024
NH = 8
HD = 128
DFF = 2816
NMETA = 16
EPS = 1e-6
TR = 256
PAD = TR - NMETA
CH = 64
NSHARD = 4
VMEM_LIMIT = 56 * 1024 * 1024

ADAM_LR = 0.001
ADAM_B1 = 0.9
ADAM_B2 = 0.999
ADAM_EPS = 1e-08
ADAM_WD = 0.01
ADAM_STEP = 10


def _cparams(semantics=None, **kw):
    return pltpu.CompilerParams(dimension_semantics=semantics, vmem_limit_bytes=VMEM_LIMIT, **kw)


def _sigmoid(x):
    return 0.5 * jnp.tanh(0.5 * x) + 0.5


def _matmul(a, b, *, mode, bm, bn, bk, out_dtype, name, residual=None, out_shards=None):
    if mode == "tn":
        K, M = a.shape
    else:
        M, K = a.shape
    b3 = b.ndim == 3
    if b3:
        S, R, Cs = b.shape
        bcols = S * Cs
        brows = R
    else:
        brows, bcols = b.shape
    if mode == "nt":
        N = brows
        assert bcols == K
    else:
        N = bcols
        assert brows == K
    assert M % bm == 0 and N % bn == 0 and K % bk == 0, (name, M, N, K, bm, bn, bk)
    nm, nn, nk = M // bm, N // bn, K // bk

    if mode == "tn":
        a_spec = pl.BlockSpec((bk, bm), lambda i, j, k: (k, i))
    else:
        a_spec = pl.BlockSpec((bm, bk), lambda i, j, k: (i, k))
    if mode == "nt":
        if b3:
            assert bn == R and Cs % bk == 0
            per = Cs // bk
            b_spec = pl.BlockSpec((None, bn, bk), lambda i, j, k: (k // per, 0, k % per))
        else:
            b_spec = pl.BlockSpec((bn, bk), lambda i, j, k: (j, k))
    else:
        if b3:
            assert bk == R and Cs % bn == 0
            per = Cs // bn
            b_spec = pl.BlockSpec((None, bk, bn), lambda i, j, k: (j // per, 0, j % per))
        else:
            b_spec = pl.BlockSpec((bk, bn), lambda i, j, k: (k, j))
    in_specs = [a_spec, b_spec]
    args = [a, b]
    if residual is not None:
        in_specs.append(pl.BlockSpec((bm, bn), lambda i, j, k: (i, j)))
        args.append(residual)
    if out_shards is not None:
        So, Co = out_shards
        assert So * Co == N and Co % bn == 0 and bm == M
        pero = Co // bn
        out_shape = jax.ShapeDtypeStruct((So, M, Co), out_dtype)
        out_spec = pl.BlockSpec((None, bm, bn), lambda i, j, k: (j // pero, 0, j % pero))
    else:
        out_shape = jax.ShapeDtypeStruct((M, N), out_dtype)
        out_spec = pl.BlockSpec((bm, bn), lambda i, j, k: (i, j))

    assert mode == "nn" and residual is None and out_dtype == F32

    def body(a_ref, b_ref, o_ref):
        @pl.when(pl.program_id(2) == 0)
        def _():
            o_ref[...] = jnp.zeros_like(o_ref)

        o_ref[...] += jnp.dot(a_ref[...].astype(BF16), b_ref[...].astype(BF16), preferred_element_type=F32)

    return pl.pallas_call(
        body, name=name, grid=(nm, nn, nk), in_specs=in_specs, out_specs=out_spec, out_shape=out_shape,
        compiler_params=_cparams(("parallel", "parallel", "arbitrary")),
    )(*args)


def _norm_fwd(h, w, *, bm, name):
    T = h.shape[0]

    def body(h_ref, w_ref, o_ref):
        x = h_ref[...]
        r = lax.rsqrt(jnp.mean(x * x, axis=-1, keepdims=True) + EPS)
        o_ref[...] = (x * r * w_ref[...]).astype(BF16)

    return pl.pallas_call(
        body, name=name, grid=(T // bm,),
        in_specs=[pl.BlockSpec((bm, D), lambda i: (i, 0)), pl.BlockSpec((1, D), lambda i: (0, 0))],
        out_specs=pl.BlockSpec((bm, D), lambda i: (i, 0)),
        out_shape=jax.ShapeDtypeStruct((T, D), BF16), compiler_params=_cparams(("parallel",)),
    )(h, w)


def _norm_bwd(h, du, dres, w, *, bm, name):
    T = h.shape[0]

    def body(h_ref, du_ref, dres_ref, w_ref, dh_ref, dw_ref):
        i = pl.program_id(0)
        x = h_ref[...]
        r = lax.rsqrt(jnp.mean(x * x, axis=-1, keepdims=True) + EPS)
        n = x * r
        du_v = du_ref[...]
        dn = du_v * w_ref[...]
        dh_ref[...] = dres_ref[...] + r * (dn - n * jnp.mean(dn * n, axis=-1, keepdims=True))
        part = jnp.sum(du_v * n, axis=0, keepdims=True)

        @pl.when(i == 0)
        def _():
            dw_ref[...] = part

        @pl.when(i > 0)
        def _():
            dw_ref[...] += part

    row = pl.BlockSpec((bm, D), lambda i: (i, 0))
    vec = pl.BlockSpec((1, D), lambda i: (0, 0))
    return pl.pallas_call(
        body, name=name, grid=(T // bm,), in_specs=[row, row, row, vec], out_specs=[row, vec],
        out_shape=[jax.ShapeDtypeStruct((T, D), F32), jax.ShapeDtypeStruct((1, D), F32)],
        compiler_params=_cparams(("arbitrary",)),
    )(h, du, dres, w)


def _input_grad_block(dproj, x_seq, head_tile, dres, w, w_in_g, psums):
    T = TR + x_seq.shape[0]
    nt = T // TR
    nsl = w_in_g.shape[0]
    wsl = w_in_g.shape[2]
    wcol = wsl // 2
    nex = len(psums)

    def body(*refs):
        dp_ref, x_ref, head_ref, dres_ref, w_ref, win_ref = refs[:6]
        gx_ref, dmeta_ref, dw_ref = refs[6 + nex:9 + nex]
        exchange = _ChipExchange(refs[6:6 + nex], refs[9 + nex:9 + 2 * nex], refs[-2], refs[-1])
        i = pl.program_id(0)

        @pl.when(i == 0)
        def _():
            exchange.send()

        @pl.when(i == nt - 1)
        def _():
            exchange.finish()

        x = jnp.where(i == 0, head_ref[...], x_ref[...])
        r = lax.rsqrt(jnp.mean(x * x, axis=-1, keepdims=True) + EPS)
        n = x * r
        du_v = None
        for s in range(nsl):
            for hf in range(2):
                part = _dot_nt(dp_ref[:, s * wsl + hf * wcol:s * wsl + (hf + 1) * wcol],
                               win_ref[s, :, hf * wcol:(hf + 1) * wcol])
                du_v = part if du_v is None else du_v + part
        dn = du_v * w_ref[...]
        dh = dres_ref[...] + r * (dn - n * jnp.mean(dn * n, axis=-1, keepdims=True))
        gx_ref[...] = dh
        part = jnp.sum(du_v * n, axis=0, keepdims=True)

        @pl.when(i == 0)
        def _():
            dmeta_ref[...] = dh[PAD:TR, :]
            dw_ref[...] = part

        @pl.when(i > 0)
        def _():
            dw_ref[...] += part

    row = pl.BlockSpec((TR, D), lambda i: (i, 0))
    vec = pl.BlockSpec((1, D), lambda i: (0, 0))
    hbm = pl.BlockSpec(memory_space=pl.ANY)
    return pl.pallas_call(
        body, name="input_grad_block", grid=(nt,),
        in_specs=[pl.BlockSpec((TR, 9 * D), lambda i: (i, 0)),
                  pl.BlockSpec((TR, D), lambda i: (jnp.maximum(i - 1, 0), 0)), pl.BlockSpec((TR, D), lambda i: (0, 0)),
                  row, vec, pl.BlockSpec(memory_space=pltpu.VMEM)] + [hbm] * nex,
        out_specs=[pl.BlockSpec((TR, D), lambda i: (jnp.maximum(i - 1, 0), 0)),
                   pl.BlockSpec((NMETA, D), lambda i: (0, 0)), vec] + [hbm] * nex,
        out_shape=[jax.ShapeDtypeStruct((T - TR, D), F32), jax.ShapeDtypeStruct((NMETA, D), F32),
                   jax.ShapeDtypeStruct((1, D), F32)]
        + [jax.ShapeDtypeStruct((3,) + p.shape[1:], p.dtype) for p in psums],
        scratch_shapes=[pltpu.SemaphoreType.DMA((nex, 3)), pltpu.SemaphoreType.DMA((nex, 3))],
        compiler_params=_cparams(("arbitrary",)),
    )(dproj, x_seq, head_tile, dres, w, w_in_g, *psums)


def _tri_matmul(tri_bf16, x):
    hi = x.astype(BF16)
    lo = (x - hi.astype(F32)).astype(BF16)
    return jnp.dot(tri_bf16, lo, preferred_element_type=F32) + jnp.dot(tri_bf16, hi, preferred_element_type=F32)


def _shift_down(x, prev8, n):
    rows = x.shape[0]
    return pltpu.roll(jnp.concatenate([prev8, x], axis=0), n, 0)[8:8 + rows, :]


def _shift_up(x, next8, n):
    rows = x.shape[0]
    return pltpu.roll(jnp.concatenate([x, next8], axis=0), rows + 8 - n, 0)[0:rows, :]


def _gates(f_raw, lb):
    sg = _sigmoid(f_raw)
    f = lb + (1.0 - lb) * sg
    return sg, f, jnp.log(f), 1.0 - f


def _lower_bound(lbp_ref):
    return _sigmoid(lbp_ref[0:1, :] - lbp_ref[1:2, :])


def _tri_masks():
    r = lax.broadcasted_iota(jnp.int32, (CH, CH), 0)
    c = lax.broadcasted_iota(jnp.int32, (CH, CH), 1)
    return r >= c, r <= c


def _ones_where(mask):
    return jnp.where(mask, 1.0, 0.0).astype(BF16)


def _dot(a, b):
    return jnp.dot(a.astype(BF16), b.astype(BF16), preferred_element_type=F32)


def _dot_nt(a, b):
    return lax.dot_general(a.astype(BF16), b.astype(BF16), (((1,), (1,)), ((), ())), preferred_element_type=F32)


def _dot_tn(a, b):
    return lax.dot_general(a.astype(BF16), b.astype(BF16), (((0,), (0,)), ((), ())), preferred_element_type=F32)


def _mix_block_fwd(x_seq, head_tile, lb_param, wattn, whn, conv_w, w_in_g, w_out, late):
    T = TR + x_seq.shape[0]
    nt = T // TR
    ncht = TR // CH
    nsl = w_in_g.shape[0]
    wsl = w_in_g.shape[2]
    wcol = wsl // 2

    nlate = len(late)

    def body(*refs):
        x_ref, head_ref, lbp_ref, wa_ref, whn_ref, cw_ref, win_ref, wout_ref = refs[:8]
        ut_ref, projb_ref, o_ref, sst_ref, mt_ref, h1_ref = refs[8 + nlate:14 + nlate]
        late_refs = refs[14 + nlate:14 + 2 * nlate]
        proj_ref, m_ref, st_ref, cxc_ref, send_sems, recv_sems = refs[14 + 2 * nlate:]
        i = pl.program_id(0)
        gather = _Gather(late_refs, send_sems, recv_sems, (False,) * nlate)

        @pl.when(i == 0)
        def _():
            st_ref[...] = jnp.zeros_like(st_ref)
            cxc_ref[...] = jnp.zeros_like(cxc_ref)
            gather.send()

        @pl.when(i == nt // 2)
        def _():
            gather.relay()

        x = jnp.where(i == 0, head_ref[...], x_ref[...])
        r1 = lax.rsqrt(jnp.mean(x * x, axis=-1, keepdims=True) + EPS)
        u_f = x * r1 * wa_ref[...]
        u = u_f.astype(BF16)
        ut_ref[...] = u_f.T.astype(BF16)
        for s in range(nsl):
            for hf in range(2):
                cols = slice(s * wsl + hf * wcol, s * wsl + (hf + 1) * wcol)
                p = jnp.dot(u, win_ref[s, :, hf * wcol:(hf + 1) * wcol], preferred_element_type=F32)
                proj_ref[:, cols] = p
                projb_ref[:, cols] = p.astype(BF16)

        lb = _lower_bound(lbp_ref)
        lower, _ = _tri_masks()
        ltri = _ones_where(lower)
        whn_v = whn_ref[...]
        w0, w1, w2 = cw_ref[0:1, :], cw_ref[1:2, :], cw_ref[2:3, :]

        def chunk(c, carry):
            rows = pl.ds(pl.multiple_of(c * CH, CH), CH)
            q_raw = proj_ref[rows, 0:D]
            f_raw = proj_ref[rows, D:2 * D]
            v = proj_ref[rows, 2 * D:3 * D]
            q = q_raw * _sigmoid(q_raw)
            _, _, g, k = _gates(f_raw, lb)
            gam = _tri_matmul(ltri, g)
            gam_l = gam[CH - 1:CH, :]
            e_l = jnp.exp(gam_l)
            qt = (q * jnp.exp(gam)).astype(BF16)
            kt = (k * jnp.exp(-gam)).astype(BF16)
            khat = (k * jnp.exp(gam_l - gam)).astype(BF16)
            vb = v.astype(BF16)
            on_parts = []
            for h in range(NH):
                cs = slice(h * HD, (h + 1) * HD)
                st = st_ref[h]
                sst_ref[c, h] = st
                a = jnp.where(lower, _dot_nt(qt[:, cs], kt[:, cs]), 0.0)
                o_h = _dot_nt(qt[:, cs], st) + _dot(a, vb[:, cs])
                st_ref[h] = st * e_l[:, cs] + _dot_tn(vb[:, cs], khat[:, cs])
                o_ref[rows, cs] = o_h
                ro = lax.rsqrt(jnp.mean(o_h * o_h, axis=-1, keepdims=True) + EPS)
                on_parts.append(o_h * ro)
            on = jnp.concatenate(on_parts, axis=1)
            g_out = proj_ref[rows, 3 * D:4 * D]
            y_a = on * whn_v * (g_out * _sigmoid(g_out))
            cx = proj_ref[rows, 5 * D:6 * D] * proj_ref[rows, 6 * D:7 * D]
            prev8 = cxc_ref[...]
            cv = w0 * _shift_down(cx, prev8, 2) + w1 * _shift_down(cx, prev8, 1) + w2 * cx
            cxc_ref[...] = cx[CH - 8:CH, :]
            y_b = proj_ref[rows, 4 * D:5 * D] * cv
            m = _sigmoid(proj_ref[rows, 7 * D:8 * D]) * y_a + _sigmoid(proj_ref[rows, 8 * D:9 * D]) * y_b
            m_ref[rows, :] = m
            return carry

        lax.fori_loop(0, ncht, chunk, 0)
        m_v = m_ref[...]
        h1_ref[...] = x + jnp.dot(m_v.astype(BF16), wout_ref[...], preferred_element_type=F32)
        mt_ref[...] = m_v.T.astype(BF16)

        @pl.when(i == nt - 1)
        def _():
            gather.finish()

    row = lambda w: pl.BlockSpec((TR, w), lambda i: (i, 0))
    col = lambda w: pl.BlockSpec((w, TR), lambda i: (0, i))
    vec = lambda r: pl.BlockSpec((r, D), lambda i: (0, 0))
    vm = pl.BlockSpec(memory_space=pltpu.VMEM)
    hbm = pl.BlockSpec(memory_space=pl.ANY)
    return pl.pallas_call(
        body, name="mix_block_fwd", grid=(nt,),
        in_specs=[pl.BlockSpec((TR, D), lambda i: (jnp.maximum(i - 1, 0), 0)), pl.BlockSpec((TR, D), lambda i: (0, 0)),
                  vec(2), vec(1), vec(1), vec(3), vm, vm] + [hbm] * nlate,
        out_specs=[col(D), row(9 * D), row(D), pl.BlockSpec((ncht, NH, HD, HD), lambda i: (i, 0, 0, 0)),
                   col(D), row(D)] + [hbm] * nlate,
        out_shape=[jax.ShapeDtypeStruct((D, T), BF16), jax.ShapeDtypeStruct((T, 9 * D), BF16),
                   jax.ShapeDtypeStruct((T, D), F32), jax.ShapeDtypeStruct((T // CH, NH, HD, HD), F32),
                   jax.ShapeDtypeStruct((D, T), BF16), jax.ShapeDtypeStruct((T, D), F32)]
        + [jax.ShapeDtypeStruct(a.shape, a.dtype) for a in late],
        input_output_aliases={8 + n: 6 + n for n in range(nlate)},
        scratch_shapes=[pltpu.VMEM((TR, 9 * D), F32), pltpu.VMEM((TR, D), F32), pltpu.VMEM((NH, HD, HD), F32),
                        pltpu.VMEM((8, D), F32), pltpu.SemaphoreType.DMA((nlate, 6)),
                        pltpu.SemaphoreType.DMA((nlate, 6))],
        compiler_params=_cparams(("arbitrary",)),
    )(x_seq, head_tile, lb_param, wattn, whn, conv_w, w_in_g, w_out, *late)


def _mix_block_bwd(dh1, projb, o, sst, lb_param, whn, conv_w, w_out, psums):
    T = projb.shape[0]
    nt = T // TR
    ncht = TR // CH
    tb16 = TR // 16
    nex = len(psums)

    def body(*refs):
        dh1_ref, proj_ref, pc_ref, px_ref, o_ref, sst_ref, lbp_ref, whn_ref, cw_ref, wout_ref = refs[:10]
        dproj_ref, dlb_ref, dwhn_ref, dcw_ref = refs[10 + nex:14 + nex]
        exchange = _ChipExchange(refs[10:10 + nex], refs[14 + nex:14 + 2 * nex], refs[-2], refs[-1])
        dm_ref, dst_ref, dcvc_ref, acc_lb, acc_hn, acc_cw = refs[14 + 2 * nex:-2]
        s = pl.program_id(0)
        tile = nt - 1 - s

        @pl.when(s == 0)
        def _():
            dst_ref[...] = jnp.zeros_like(dst_ref)
            dcvc_ref[...] = jnp.zeros_like(dcvc_ref)
            acc_lb[...] = jnp.zeros_like(acc_lb)
            acc_hn[...] = jnp.zeros_like(acc_hn)
            acc_cw[...] = jnp.zeros_like(acc_cw)
            exchange.send()

        dm_ref[...] = _dot_nt(dh1_ref[...], wout_ref[...])
        lb = _lower_bound(lbp_ref)
        lower, upper = _tri_masks()
        ltri = _ones_where(lower)
        utri = _ones_where(upper)
        whn_v = whn_ref[...]
        w0, w1, w2 = cw_ref[0:1, :], cw_ref[1:2, :], cw_ref[2:3, :]
        cx_before_tile = jnp.where(tile > 0, (pc_ref[...].astype(F32) * px_ref[...].astype(F32))[8:16, :], 0.0)
        rid = lax.broadcasted_iota(jnp.int32, (CH, D), 0)

        def chunk(cc, carry):
            c = ncht - 1 - cc
            r0 = pl.multiple_of(c * CH, CH)
            rows = pl.ds(r0, CH)
            slab = lambda n: proj_ref[rows, n * D:(n + 1) * D].astype(F32)
            q_raw, f_raw, v, g_out, b_gate, c_gate, x_conv = (slab(n) for n in range(7))
            sa = _sigmoid(slab(7))
            sb = _sigmoid(slab(8))
            dm_v = dm_ref[rows, :]

            sq = _sigmoid(q_raw)
            q = q_raw * sq
            sg, f, g, k = _gates(f_raw, lb)
            gam = _tri_matmul(ltri, g)
            gam_l = gam[CH - 1:CH, :]
            e_l = jnp.exp(gam_l)
            e_g = jnp.exp(gam)
            e_ng = jnp.exp(-gam)
            e_kl = jnp.exp(gam_l - gam)
            qt = q * e_g
            kt = k * e_ng
            khat = k * e_kl
            qt_b, kt_b, khat_b, v_b = qt.astype(BF16), kt.astype(BF16), khat.astype(BF16), v.astype(BF16)
            qt_seen, kt_seen = qt_b.astype(F32), kt_b.astype(F32)
            s_go = _sigmoid(g_out)
            silu_go = g_out * s_go
            cx = c_gate * x_conv
            rprev = pl.ds(pl.multiple_of(jnp.maximum(r0 - 16, 0), 16), 16)
            cx_prev_in = (proj_ref[rprev, 5 * D:6 * D].astype(F32) * proj_ref[rprev, 6 * D:7 * D].astype(F32))[8:16, :]
            prev8 = jnp.where(c > 0, cx_prev_in, cx_before_tile)
            cx_m1 = _shift_down(cx, prev8, 1)
            cx_m2 = _shift_down(cx, prev8, 2)
            cv = w0 * cx_m2 + w1 * cx_m1 + w2 * cx
            y_b = b_gate * cv

            o_v = o_ref[rows, :]
            ro_parts, on_parts = [], []
            for h in range(NH):
                cs = slice(h * HD, (h + 1) * HD)
                o_h = o_v[:, cs]
                ro = lax.rsqrt(jnp.mean(o_h * o_h, axis=-1, keepdims=True) + EPS)
                ro_parts.append(ro)
                on_parts.append(o_h * ro)
            on = jnp.concatenate(on_parts, axis=1)
            y_a = on * whn_v * silu_go

            dy_a = dm_v * sa
            dy_b = dm_v * sb
            dproj_ref[rows, 7 * D:8 * D] = (dm_v * y_a * sa * (1.0 - sa)).astype(BF16)
            dproj_ref[rows, 8 * D:9 * D] = (dm_v * y_b * sb * (1.0 - sb)).astype(BF16)
            dproj_ref[rows, 4 * D:5 * D] = (dy_b * cv).astype(BF16)
            dcv = dy_b * b_gate
            acc_cw[0:1, :] += jnp.sum(dcv * cx_m2, axis=0, keepdims=True)
            acc_cw[1:2, :] += jnp.sum(dcv * cx_m1, axis=0, keepdims=True)
            acc_cw[2:3, :] += jnp.sum(dcv * cx, axis=0, keepdims=True)
            next8 = dcvc_ref[...]
            dcx = w2 * dcv + w1 * _shift_up(dcv, next8, 1) + w0 * _shift_up(dcv, next8, 2)
            dcvc_ref[...] = dcv[0:8, :]
            dproj_ref[rows, 5 * D:6 * D] = (dcx * x_conv).astype(BF16)
            dproj_ref[rows, 6 * D:7 * D] = (dcx * c_gate).astype(BF16)
            don = dy_a * whn_v * silu_go
            dproj_ref[rows, 3 * D:4 * D] = (dy_a * on * whn_v * (s_go * (1.0 + g_out * (1.0 - s_go)))).astype(BF16)
            acc_hn[...] += jnp.sum(dy_a * silu_go * on, axis=0, keepdims=True)

            dq_parts, dk_parts, dv_parts, dgam_parts, ext_parts = [], [], [], [], []
            for h in range(NH):
                cs = slice(h * HD, (h + 1) * HD)
                on_h = on_parts[h]
                don_h = don[:, cs]
                do_h = ro_parts[h] * (don_h - on_h * jnp.mean(don_h * on_h, axis=-1, keepdims=True))
                qt_h, kt_h, khat_h, v_h = qt_b[:, cs], kt_b[:, cs], khat_b[:, cs], v_b[:, cs]
                do_b = do_h.astype(BF16)
                st = sst_ref[c, h]
                dstn = dst_ref[h]
                dstn_b = dstn.astype(BF16)
                a_t = jnp.where(upper, _dot_nt(kt_h, qt_h), 0.0)
                da = jnp.where(lower, _dot_nt(do_b, v_h), 0.0)
                da_t = jnp.where(upper, _dot_nt(v_h, do_b), 0.0)
                dv_h = _dot(a_t, do_b) + _dot_nt(khat_h, dstn_b)
                dqt_state = _dot(do_b, st)
                dqt_chunk = _dot(da, kt_h)
                dkt = _dot(da_t, qt_h)
                dkhat = _dot(v_h, dstn_b)
                dq_h = (dqt_state + dqt_chunk) * e_g[:, cs]
                dk_h = dkt * e_ng[:, cs] + dkhat * e_kl[:, cs]
                khat_dkhat = dkhat * khat[:, cs]
                ext = (jnp.sum(khat_dkhat, axis=0, keepdims=True)
                       + e_l[:, cs] * jnp.sum(st * dstn, axis=0, keepdims=True))
                dst_ref[h] = _dot_tn(do_b, qt_h) + dstn * e_l[:, cs]
                dq_parts.append(dq_h)
                dk_parts.append(dk_h)
                dv_parts.append(dv_h)
                dgam_parts.append(qt[:, cs] * dqt_state + qt_seen[:, cs] * dqt_chunk - kt_seen[:, cs] * dkt
                                  - khat_dkhat)
                ext_parts.append(ext)
            dq = jnp.concatenate(dq_parts, axis=1)
            dk = jnp.concatenate(dk_parts, axis=1)
            dgam = jnp.concatenate(dgam_parts, axis=1)
            ext = jnp.concatenate(ext_parts, axis=1)
            dgam = dgam + jnp.where(rid == CH - 1, ext, 0.0)
            dg = _tri_matmul(utri, dgam)
            dproj_ref[rows, 0:D] = (dq * (sq * (1.0 + q_raw * (1.0 - sq)))).astype(BF16)
            df = dg * jnp.exp(-g) - dk
            dproj_ref[rows, D:2 * D] = (df * (1.0 - lb) * sg * (1.0 - sg)).astype(BF16)
            dproj_ref[rows, 2 * D:3 * D] = jnp.concatenate(dv_parts, axis=1).astype(BF16)
            real = (tile * TR + r0 + rid) >= PAD
            acc_lb[...] += jnp.sum(jnp.where(real, df * (1.0 - sg), 0.0), axis=0, keepdims=True)
            return carry

        lax.fori_loop(0, ncht, chunk, 0)

        @pl.when(s == nt - 1)
        def _():
            dlb_ref[...] = acc_lb[...] * lb * (1.0 - lb)
            hn = acc_hn[...]
            tot = hn[:, 0:HD]
            for h in range(1, NH):
                tot = tot + hn[:, h * HD:(h + 1) * HD]
            dwhn_ref[...] = tot
            dcw_ref[...] = acc_cw[0:3, :]
            exchange.finish()

    hbm = pl.BlockSpec(memory_space=pl.ANY)
    rev = lambda s: (nt - 1 - s, 0)
    prevc = lambda s: (jnp.maximum((nt - 1 - s) * tb16 - 1, 0), 5)
    prevx = lambda s: (jnp.maximum((nt - 1 - s) * tb16 - 1, 0), 6)
    const = lambda s: (0, 0)
    return pl.pallas_call(
        body, name="mix_block_bwd", grid=(nt,),
        in_specs=[pl.BlockSpec((TR, D), rev),
                  pl.BlockSpec((TR, 9 * D), rev),
                  pl.BlockSpec((16, D), prevc),
                  pl.BlockSpec((16, D), prevx),
                  pl.BlockSpec((TR, D), rev),
                  pl.BlockSpec((ncht, NH, HD, HD), lambda s: (nt - 1 - s, 0, 0, 0)),
                  pl.BlockSpec((2, D), const),
                  pl.BlockSpec((1, D), const),
                  pl.BlockSpec((3, D), const),
                  pl.BlockSpec(memory_space=pltpu.VMEM)] + [hbm] * nex,
        out_specs=[pl.BlockSpec((TR, 9 * D), rev),
                   pl.BlockSpec((1, D), const),
                   pl.BlockSpec((1, HD), const),
                   pl.BlockSpec((3, D), const)] + [hbm] * nex,
        out_shape=[jax.ShapeDtypeStruct((T, 9 * D), BF16), jax.ShapeDtypeStruct((1, D), F32),
                   jax.ShapeDtypeStruct((1, HD), F32), jax.ShapeDtypeStruct((3, D), F32)]
        + [jax.ShapeDtypeStruct((3,) + p.shape[1:], p.dtype) for p in psums],
        scratch_shapes=[pltpu.VMEM((TR, D), F32), pltpu.VMEM((NH, HD, HD), F32), pltpu.VMEM((8, D), F32),
                        pltpu.VMEM((1, D), F32), pltpu.VMEM((1, D), F32), pltpu.VMEM((8, D), F32),
                        pltpu.SemaphoreType.DMA((nex, 3)), pltpu.SemaphoreType.DMA((nex, 3))],
        compiler_params=_cparams(("arbitrary",)),
    )(dh1, projb, projb, projb, o, sst, lb_param, whn, conv_w, w_out, *psums)


def _ffn_fwd(up, fcw, fcb):
    T = up.shape[0]
    nt = T // TR

    def body(up_ref, w_ref, b_ref, gg_ref, carry_ref):
        i = pl.program_id(0)

        @pl.when(i == 0)
        def _():
            carry_ref[...] = jnp.zeros_like(carry_ref)

        w0, w1, w2 = w_ref[0:1, :], w_ref[1:2, :], w_ref[2:3, :]

        def chunk(c, carry):
            rows = pl.ds(pl.multiple_of(c * CH, CH), CH)
            a_pre = up_ref[rows, 0:DFF]
            val = up_ref[rows, DFF:2 * DFF]
            prev8 = carry_ref[...]
            a = w0 * _shift_down(a_pre, prev8, 2) + w1 * _shift_down(a_pre, prev8, 1) + w2 * a_pre + b_ref[...]
            carry_ref[...] = a_pre[CH - 8:CH, :]
            gg_ref[rows, :] = (a * _sigmoid(a) * val).astype(BF16)
            return carry

        lax.fori_loop(0, TR // CH, chunk, 0)

    return pl.pallas_call(
        body, name="ffn_fwd", grid=(nt,),
        in_specs=[pl.BlockSpec((TR, 2 * DFF), lambda i: (i, 0)),
                  pl.BlockSpec((3, DFF), lambda i: (0, 0)),
                  pl.BlockSpec((1, DFF), lambda i: (0, 0))],
        out_specs=pl.BlockSpec((TR, DFF), lambda i: (i, 0)),
        out_shape=jax.ShapeDtypeStruct((T, DFF), BF16),
        scratch_shapes=[pltpu.VMEM((8, DFF), F32)],
        compiler_params=_cparams(("arbitrary",)),
    )(up, fcw, fcb)


def _ffn_bwd(up, dgg, fcw, fcb):
    T = up.shape[0]
    nt = T // TR
    tb = TR // 8
    ncht = TR // CH

    def body(up_ref, pa_ref, dgg_ref, w_ref, b_ref, dup_ref, dfw_ref, dfb_ref, carry_ref, acc_w, acc_b):
        s = pl.program_id(0)
        tile = nt - 1 - s

        @pl.when(s == 0)
        def _():
            carry_ref[...] = jnp.zeros_like(carry_ref)
            acc_w[...] = jnp.zeros_like(acc_w)
            acc_b[...] = jnp.zeros_like(acc_b)

        w0, w1, w2 = w_ref[0:1, :], w_ref[1:2, :], w_ref[2:3, :]
        a_before_tile = jnp.where(tile > 0, pa_ref[...], 0.0)

        def chunk(cc, carry):
            c = ncht - 1 - cc
            r0 = pl.multiple_of(c * CH, CH)
            rows = pl.ds(r0, CH)
            a_pre = up_ref[rows, 0:DFF]
            val = up_ref[rows, DFF:2 * DFF]
            rprev = pl.multiple_of(jnp.maximum(r0 - 8, 0), 8)
            prev8 = jnp.where(c > 0, up_ref[pl.ds(rprev, 8), 0:DFF], a_before_tile)
            a_m1 = _shift_down(a_pre, prev8, 1)
            a_m2 = _shift_down(a_pre, prev8, 2)
            a = w0 * a_m2 + w1 * a_m1 + w2 * a_pre + b_ref[...]
            sig = _sigmoid(a)
            dgg_v = dgg_ref[rows, :]
            da = dgg_v * val * (sig * (1.0 + a * (1.0 - sig)))
            dup_ref[rows, DFF:2 * DFF] = (dgg_v * (a * sig)).astype(BF16)
            next8 = carry_ref[...]
            dup_ref[rows, 0:DFF] = (w2 * da + w1 * _shift_up(da, next8, 1) + w0 * _shift_up(da, next8, 2)).astype(BF16)
            carry_ref[...] = da[0:8, :]
            acc_w[0:1, :] += jnp.sum(da * a_m2, axis=0, keepdims=True)
            acc_w[1:2, :] += jnp.sum(da * a_m1, axis=0, keepdims=True)
            acc_w[2:3, :] += jnp.sum(da * a_pre, axis=0, keepdims=True)
            acc_b[...] += jnp.sum(da, axis=0, keepdims=True)
            return carry

        lax.fori_loop(0, ncht, chunk, 0)

        @pl.when(s == nt - 1)
        def _():
            dfw_ref[...] = acc_w[0:3, :]
            dfb_ref[...] = acc_b[...]

    rev = lambda s: (nt - 1 - s, 0)
    const = lambda s: (0, 0)
    return pl.pallas_call(
        body, name="ffn_bwd", grid=(nt,),
        in_specs=[pl.BlockSpec((TR, 2 * DFF), rev),
                  pl.BlockSpec((8, DFF), lambda s: (jnp.maximum((nt - 1 - s) * tb - 1, 0), 0)),
                  pl.BlockSpec((TR, DFF), rev),
                  pl.BlockSpec((3, DFF), const),
                  pl.BlockSpec((1, DFF), const)],
        out_specs=[pl.BlockSpec((TR, 2 * DFF), rev),
                   pl.BlockSpec((3, DFF), const),
                   pl.BlockSpec((1, DFF), const)],
        out_shape=[jax.ShapeDtypeStruct((T, 2 * DFF), BF16), jax.ShapeDtypeStruct((3, DFF), F32),
                   jax.ShapeDtypeStruct((1, DFF), F32)],
        scratch_shapes=[pltpu.VMEM((8, DFF), F32), pltpu.VMEM((8, DFF), F32), pltpu.VMEM((1, DFF), F32)],
        compiler_params=_cparams(("arbitrary",)),
    )(up, up, dgg, fcw, fcb)


def _loss_head(h2, tgt, wfin):
    T = h2.shape[0]
    nt = T // TR

    def body(h_ref, t_ref, w_ref, dh_ref, loss_ref, dw_ref):
        i = pl.program_id(0)

        @pl.when(i == 0)
        def _():
            loss_ref[...] = jnp.zeros_like(loss_ref)
            dw_ref[...] = jnp.zeros_like(dw_ref)

        x = h_ref[...]
        r = lax.rsqrt(jnp.mean(x * x, axis=-1, keepdims=True) + EPS)
        n = x * r
        w = w_ref[...]
        diff = jnp.where(i > 0, n * w - t_ref[...], 0.0)
        loss_ref[...] += jnp.sum(diff * diff, axis=0, keepdims=True) * (0.5 / D)
        dy = diff * (1.0 / D)
        dw_ref[...] += jnp.sum(dy * n, axis=0, keepdims=True)
        dn = dy * w
        dh_ref[...] = r * (dn - n * jnp.mean(dn * n, axis=-1, keepdims=True))

    row = pl.BlockSpec((TR, D), lambda i: (i, 0))
    vec = pl.BlockSpec((1, D), lambda i: (0, 0))
    return pl.pallas_call(
        body, name="loss_head", grid=(nt,),
        in_specs=[row, pl.BlockSpec((TR, D), lambda i: (jnp.maximum(i - 1, 0), 0)), vec],
        out_specs=[row, vec, vec],
        out_shape=[jax.ShapeDtypeStruct((T, D), F32), jax.ShapeDtypeStruct((1, D), F32),
                   jax.ShapeDtypeStruct((1, D), F32)],
        compiler_params=_cparams(("arbitrary",)),
    )(h2, tgt, wfin)


def _ffn_block_fwd(h1, tgt, wffn, w_up_g, fcw, fcb, w_down, wfin):
    T = h1.shape[0]
    nt = T // TR
    nsl = w_up_g.shape[0]
    wsl = w_up_g.shape[2]

    def body(h_ref, t_ref, wn_ref, wup_ref, cw_ref, cb_ref, wdn_ref, wf_ref,
             u2t_ref, upb_ref, ggt_ref, dh_ref, loss_ref, dwf_ref, up_scr, gg_ref, carry_ref):
        i = pl.program_id(0)

        @pl.when(i == 0)
        def _():
            carry_ref[...] = jnp.zeros_like(carry_ref)
            loss_ref[...] = jnp.zeros_like(loss_ref)
            dwf_ref[...] = jnp.zeros_like(dwf_ref)

        x = h_ref[...]
        r2 = lax.rsqrt(jnp.mean(x * x, axis=-1, keepdims=True) + EPS)
        u2_f = x * r2 * wn_ref[...]
        u2 = u2_f.astype(BF16)
        u2t_ref[...] = u2_f.T.astype(BF16)
        for s in range(nsl):
            up_s = jnp.dot(u2, wup_ref[s], preferred_element_type=F32)
            up_scr[:, s * wsl:(s + 1) * wsl] = up_s
            upb_ref[:, s * wsl:(s + 1) * wsl] = up_s.astype(BF16)
        w0, w1, w2 = cw_ref[0:1, :], cw_ref[1:2, :], cw_ref[2:3, :]

        def chunk(c, carry):
            rows = pl.ds(pl.multiple_of(c * CH, CH), CH)
            a_pre = up_scr[rows, 0:DFF]
            val = up_scr[rows, DFF:2 * DFF]
            prev8 = carry_ref[...]
            a = w0 * _shift_down(a_pre, prev8, 2) + w1 * _shift_down(a_pre, prev8, 1) + w2 * a_pre + cb_ref[...]
            carry_ref[...] = a_pre[CH - 8:CH, :]
            gg_ref[rows, :] = a * _sigmoid(a) * val
            return carry

        lax.fori_loop(0, TR // CH, chunk, 0)
        gg_v = gg_ref[...]
        ggt_ref[...] = gg_v.T.astype(BF16)
        h2 = x + jnp.dot(gg_v.astype(BF16), wdn_ref[...], preferred_element_type=F32)
        r3 = lax.rsqrt(jnp.mean(h2 * h2, axis=-1, keepdims=True) + EPS)
        n3 = h2 * r3
        wf = wf_ref[...]
        diff = jnp.where(i > 0, n3 * wf - t_ref[...], 0.0)
        loss_ref[...] += jnp.sum(diff * diff, axis=0, keepdims=True) * (0.5 / D)
        dy = diff * (1.0 / D)
        dwf_ref[...] += jnp.sum(dy * n3, axis=0, keepdims=True)
        dn = dy * wf
        dh_ref[...] = r3 * (dn - n3 * jnp.mean(dn * n3, axis=-1, keepdims=True))

    row = lambda w: pl.BlockSpec((TR, w), lambda i: (i, 0))
    col = lambda w: pl.BlockSpec((w, TR), lambda i: (0, i))
    vec = lambda w, r=1: pl.BlockSpec((r, w), lambda i: (0, 0))
    vm = pl.BlockSpec(memory_space=pltpu.VMEM)
    return pl.pallas_call(
        body, name="ffn_block_fwd", grid=(nt,),
        in_specs=[row(D), pl.BlockSpec((TR, D), lambda i: (jnp.maximum(i - 1, 0), 0)), vec(D), vm,
                  vec(DFF, 3), vec(DFF), vm, vec(D)],
        out_specs=[col(D), row(2 * DFF), col(DFF), row(D), vec(D), vec(D)],
        out_shape=[jax.ShapeDtypeStruct((D, T), BF16), jax.ShapeDtypeStruct((T, 2 * DFF), BF16),
                   jax.ShapeDtypeStruct((DFF, T), BF16), jax.ShapeDtypeStruct((T, D), F32),
                   jax.ShapeDtypeStruct((1, D), F32), jax.ShapeDtypeStruct((1, D), F32)],
        scratch_shapes=[pltpu.VMEM((TR, 2 * DFF), F32), pltpu.VMEM((TR, DFF), F32), pltpu.VMEM((8, DFF), F32)],
        compiler_params=_cparams(("arbitrary",)),
    )(h1, tgt, wffn, w_up_g, fcw, fcb, w_down, wfin)


def _ffn_block_bwd_pipelined(dh2, upb, h1, wffn, w_up_g, fcw, fcb, w_down):
    T = h1.shape[0]
    nt = T // TR
    ncht = TR // CH
    nsl = w_up_g.shape[0]
    wsl = w_up_g.shape[2]
    tb16 = TR // 16
    assert nsl == ncht
    step = -(-DFF // (ncht * 128)) * 128
    parts = [(c0, min(c0 + step, DFF)) for c0 in range(0, DFF, step)]
    assert len(parts) == ncht

    def body(dh2n_ref, dh2p_ref, up_ref, pa_ref, h_ref, wn_ref, wup_ref, cw_ref, cb_ref, wdn_ref,
             dup_ref, dh1_ref, dfw_ref, dfb_ref, dwn_ref,
             dgg_ring, dup_ring, carry_ref, acc_w, acc_b, acc_n):
        s = pl.program_id(0)
        slot = lax.rem(s, 2)
        other = 1 - slot

        @pl.when(s == 0)
        def _():
            carry_ref[...] = jnp.zeros_like(carry_ref)
            acc_w[...] = jnp.zeros_like(acc_w)
            acc_b[...] = jnp.zeros_like(acc_b)
            acc_n[...] = jnp.zeros_like(acc_n)
            dup_ring[1] = jnp.zeros((TR, 2 * DFF), BF16)
            dgg_ring[0] = _dot_nt(dh2p_ref[...], wdn_ref[...])

        def norm_bwd(du2, valid):
            x = h_ref[...]
            dh2p = dh2p_ref[...]
            r2 = lax.rsqrt(jnp.mean(x * x, axis=-1, keepdims=True) + EPS)
            n2 = x * r2
            dn = du2 * wn_ref[...]
            dh1_ref[...] = dh2p + r2 * (dn - n2 * jnp.mean(dn * n2, axis=-1, keepdims=True))
            acc_n[...] += jnp.where(valid, jnp.sum(du2 * n2, axis=0, keepdims=True), 0.0)

        @pl.when(s < nt)
        def _():
            tile = nt - 1 - s
            w0, w1, w2 = cw_ref[0:1, :], cw_ref[1:2, :], cw_ref[2:3, :]
            a_before_tile = jnp.where(tile > 0, pa_ref[...].astype(F32)[8:16, :], 0.0)
            dh2n = dh2n_ref[...].astype(BF16)
            du2 = None
            for idx in range(ncht):
                c = ncht - 1 - idx
                r0 = c * CH
                rows = slice(r0, r0 + CH)
                a_pre = up_ref[rows, 0:DFF].astype(F32)
                val = up_ref[rows, DFF:2 * DFF].astype(F32)
                prev8 = up_ref[r0 - 16:r0, 0:DFF].astype(F32)[8:16, :] if c > 0 else a_before_tile
                a_m1 = _shift_down(a_pre, prev8, 1)
                a_m2 = _shift_down(a_pre, prev8, 2)
                a = w0 * a_m2 + w1 * a_m1 + w2 * a_pre + cb_ref[...]
                sig = _sigmoid(a)
                dgg_v = dgg_ring[slot, rows, :]
                da = dgg_v * val * (sig * (1.0 + a * (1.0 - sig)))
                dval = (dgg_v * (a * sig)).astype(BF16)
                next8 = carry_ref[...]
                da_pre = (w2 * da + w1 * _shift_up(da, next8, 1) + w0 * _shift_up(da, next8, 2)).astype(BF16)
                carry_ref[...] = da[0:8, :]
                dup_ref[rows, 0:DFF] = da_pre
                dup_ref[rows, DFF:2 * DFF] = dval
                dup_ring[slot, rows, 0:DFF] = da_pre
                dup_ring[slot, rows, DFF:2 * DFF] = dval
                acc_w[0:1, :] += jnp.sum(da * a_m2, axis=0, keepdims=True)
                acc_w[1:2, :] += jnp.sum(da * a_m1, axis=0, keepdims=True)
                acc_w[2:3, :] += jnp.sum(da * a_pre, axis=0, keepdims=True)
                acc_b[...] += jnp.sum(da, axis=0, keepdims=True)
                c0, c1 = parts[idx]
                dgg_ring[other, :, c0:c1] = _dot_nt(dh2n, wdn_ref[c0:c1, :])
                part = _dot_nt(dup_ring[other, :, idx * wsl:(idx + 1) * wsl], wup_ref[idx])
                du2 = part if du2 is None else du2 + part
            norm_bwd(du2, s > 0)

        @pl.when(s == nt)
        def _():
            du2 = _dot_nt(dup_ring[other, :, 0:wsl], wup_ref[0])
            for sl in range(1, nsl):
                du2 = du2 + _dot_nt(dup_ring[other, :, sl * wsl:(sl + 1) * wsl], wup_ref[sl])
            norm_bwd(du2, True)
            dfw_ref[...] = acc_w[0:3, :]
            dfb_ref[...] = acc_b[...]
            dwn_ref[...] = acc_n[...]

    gate_tile = lambda s: jnp.maximum(nt - 1 - s, 0)
    next_tile = lambda s: jnp.maximum(nt - 2 - s, 0)
    prev_tile = lambda s: jnp.minimum(nt - s, nt - 1)
    vec = lambda w, r=1: pl.BlockSpec((r, w), lambda s: (0, 0))
    vm = pl.BlockSpec(memory_space=pltpu.VMEM)
    return pl.pallas_call(
        body, name="ffn_block_bwd", grid=(nt + 1,),
        in_specs=[pl.BlockSpec((TR, D), lambda s: (next_tile(s), 0)),
                  pl.BlockSpec((TR, D), lambda s: (prev_tile(s), 0)),
                  pl.BlockSpec((TR, 2 * DFF), lambda s: (gate_tile(s), 0)),
                  pl.BlockSpec((16, DFF), lambda s: (jnp.maximum(gate_tile(s) * tb16 - 1, 0), 0)),
                  pl.BlockSpec((TR, D), lambda s: (prev_tile(s), 0)),
                  vec(D), vm, vec(DFF, 3), vec(DFF), vm],
        out_specs=[pl.BlockSpec((TR, 2 * DFF), lambda s: (gate_tile(s), 0)),
                   pl.BlockSpec((TR, D), lambda s: (prev_tile(s), 0)),
                   vec(DFF, 3), vec(DFF), vec(D)],
        out_shape=[jax.ShapeDtypeStruct((T, 2 * DFF), BF16), jax.ShapeDtypeStruct((T, D), F32),
                   jax.ShapeDtypeStruct((3, DFF), F32), jax.ShapeDtypeStruct((1, DFF), F32),
                   jax.ShapeDtypeStruct((1, D), F32)],
        scratch_shapes=[pltpu.VMEM((2, TR, DFF), F32), pltpu.VMEM((2, TR, 2 * DFF), BF16),
                        pltpu.VMEM((8, DFF), F32), pltpu.VMEM((8, DFF), F32), pltpu.VMEM((1, DFF), F32),
                        pltpu.VMEM((1, D), F32)],
        compiler_params=_cparams(("arbitrary",)),
    )(dh2, dh2, upb, upb, h1, wffn, w_up_g, fcw, fcb, w_down)


def _ffn_block_bwd(dh2, upb, h1, wffn, w_up_g, fcw, fcb, w_down):
    T = h1.shape[0]
    nt = T // TR
    ncht = TR // CH
    nsl = w_up_g.shape[0]
    wsl = w_up_g.shape[2]
    tb16 = TR // 16

    def body(dh2_ref, up_ref, pa_ref, h_ref, wn_ref, wup_ref, cw_ref, cb_ref, wdn_ref,
             dup_ref, dh1_ref, dfw_ref, dfb_ref, dwn_ref, dgg_scr, carry_ref, acc_w, acc_b, acc_n):
        s = pl.program_id(0)
        tile = nt - 1 - s

        @pl.when(s == 0)
        def _():
            carry_ref[...] = jnp.zeros_like(carry_ref)
            acc_w[...] = jnp.zeros_like(acc_w)
            acc_b[...] = jnp.zeros_like(acc_b)
            acc_n[...] = jnp.zeros_like(acc_n)

        dh2 = dh2_ref[...]
        dgg_scr[...] = _dot_nt(dh2, wdn_ref[...])
        w0, w1, w2 = cw_ref[0:1, :], cw_ref[1:2, :], cw_ref[2:3, :]
        a_before_tile = jnp.where(tile > 0, pa_ref[...].astype(F32)[8:16, :], 0.0)

        def chunk(cc, carry):
            c = ncht - 1 - cc
            r0 = pl.multiple_of(c * CH, CH)
            rows = pl.ds(r0, CH)
            a_pre = up_ref[rows, 0:DFF].astype(F32)
            val = up_ref[rows, DFF:2 * DFF].astype(F32)
            rprev = pl.multiple_of(jnp.maximum(r0 - 16, 0), 16)
            prev_in = up_ref[pl.ds(rprev, 16), 0:DFF].astype(F32)[8:16, :]
            prev8 = jnp.where(c > 0, prev_in, a_before_tile)
            a_m1 = _shift_down(a_pre, prev8, 1)
            a_m2 = _shift_down(a_pre, prev8, 2)
            a = w0 * a_m2 + w1 * a_m1 + w2 * a_pre + cb_ref[...]
            sig = _sigmoid(a)
            dgg_v = dgg_scr[rows, :]
            da = dgg_v * val * (sig * (1.0 + a * (1.0 - sig)))
            dup_ref[rows, DFF:2 * DFF] = (dgg_v * (a * sig)).astype(BF16)
            next8 = carry_ref[...]
            dup_ref[rows, 0:DFF] = (w2 * da + w1 * _shift_up(da, next8, 1) + w0 * _shift_up(da, next8, 2)).astype(BF16)
            carry_ref[...] = da[0:8, :]
            acc_w[0:1, :] += jnp.sum(da * a_m2, axis=0, keepdims=True)
            acc_w[1:2, :] += jnp.sum(da * a_m1, axis=0, keepdims=True)
            acc_w[2:3, :] += jnp.sum(da * a_pre, axis=0, keepdims=True)
            acc_b[...] += jnp.sum(da, axis=0, keepdims=True)
            return carry

        lax.fori_loop(0, ncht, chunk, 0)
        du2 = _dot_nt(dup_ref[:, 0:wsl], wup_ref[0])
        for sl in range(1, nsl):
            du2 = du2 + _dot_nt(dup_ref[:, sl * wsl:(sl + 1) * wsl], wup_ref[sl])
        x = h_ref[...]
        r2 = lax.rsqrt(jnp.mean(x * x, axis=-1, keepdims=True) + EPS)
        n2 = x * r2
        dn = du2 * wn_ref[...]
        dh1_ref[...] = dh2 + r2 * (dn - n2 * jnp.mean(dn * n2, axis=-1, keepdims=True))
        acc_n[...] += jnp.sum(du2 * n2, axis=0, keepdims=True)

        @pl.when(s == nt - 1)
        def _():
            dfw_ref[...] = acc_w[0:3, :]
            dfb_ref[...] = acc_b[...]
            dwn_ref[...] = acc_n[...]

    rev = lambda w: pl.BlockSpec((TR, w), lambda s: (nt - 1 - s, 0))
    vec = lambda w, r=1: pl.BlockSpec((r, w), lambda s: (0, 0))
    vm = pl.BlockSpec(memory_space=pltpu.VMEM)
    return pl.pallas_call(
        body, name="ffn_block_bwd", grid=(nt,),
        in_specs=[rev(D), rev(2 * DFF),
                  pl.BlockSpec((16, DFF), lambda s: (jnp.maximum((nt - 1 - s) * tb16 - 1, 0), 0)),
                  rev(D), vec(D), vm, vec(DFF, 3), vec(DFF), vm],
        out_specs=[rev(2 * DFF), rev(D), vec(DFF, 3), vec(DFF), vec(D)],
        out_shape=[jax.ShapeDtypeStruct((T, 2 * DFF), BF16), jax.ShapeDtypeStruct((T, D), F32),
                   jax.ShapeDtypeStruct((3, DFF), F32), jax.ShapeDtypeStruct((1, DFF), F32),
                   jax.ShapeDtypeStruct((1, D), F32)],
        scratch_shapes=[pltpu.VMEM((TR, DFF), F32), pltpu.VMEM((8, DFF), F32), pltpu.VMEM((8, DFF), F32),
                        pltpu.VMEM((1, DFF), F32), pltpu.VMEM((1, D), F32)],
        compiler_params=_cparams(("arbitrary",)),
    )(dh2, upb, upb, h1, wffn, w_up_g, fcw, fcb, w_down)


def _place():
    x, y, c = lax.axis_index("x"), lax.axis_index("y"), lax.axis_index("c")
    return x, y, c


_CHIP_FLIPS = ((1, 0), (0, 1), (1, 1))


def _flip(v, bit):
    return 1 - v if bit else v


def _into_slot(w, j_idx, *, rb, dtype, name):
    r, cdim = w.shape

    def body(j_ref, w_ref, out_ref):
        del j_ref
        out_ref[...] = w_ref[...].astype(dtype)

    grid_spec = pltpu.PrefetchScalarGridSpec(
        num_scalar_prefetch=1, grid=(r // rb,),
        in_specs=[pl.BlockSpec((rb, cdim), lambda i, j_ref: (i, 0))],
        out_specs=pl.BlockSpec((None, rb, cdim), lambda i, j_ref: (j_ref[0], i, 0)))
    return pl.pallas_call(
        body, name=name, grid_spec=grid_spec, out_shape=jax.ShapeDtypeStruct((NSHARD, r, cdim), dtype),
        compiler_params=_cparams(("parallel",)),
    )(j_idx, w)


class _Gather:
    def __init__(self, outs, send_sems, recv_sems, whole):
        self.outs, self.send_sems, self.recv_sems, self.whole = outs, send_sems, recv_sems, whole
        self.x, self.y, self.c = _place()
        self.j = 2 * self.x + self.y
        self.sibling = (self.x, self.y, 1 - self.c)

    def _rows(self, w, core):
        r = self.outs[w].shape[1]
        return pl.ds(0, r) if self.whole[w] else pl.ds(core * (r // 2), r // 2)

    def _copy(self, w, slot, core, sem, to):
        piece = self.outs[w].at[slot, self._rows(w, core), :]
        return pltpu.make_async_remote_copy(
            src_ref=piece, dst_ref=piece, send_sem=self.send_sems.at[w, sem], recv_sem=self.recv_sems.at[w, sem],
            device_id=to, device_id_type=MESH)

    def _chips(self):
        for kk, (fx, fy) in enumerate(_CHIP_FLIPS):
            px, py = _flip(self.x, fx), _flip(self.y, fy)
            yield kk, 2 * px + py, (px, py, self.c)

    def send(self):
        for w in range(len(self.outs)):
            for kk, _, to in self._chips():
                self._copy(w, self.j, self.c, kk, to).start()

    def relay(self):
        for w in range(len(self.outs)):
            for kk, jk, _ in self._chips():
                self._copy(w, jk, self.c, kk, self.sibling).wait_recv()
                if not self.whole[w]:
                    self._copy(w, jk, self.c, 3 + kk, self.sibling).start()

    def finish(self):
        for w in range(len(self.outs)):
            for kk, jk, to in self._chips():
                self._copy(w, self.j, self.c, kk, to).wait_send()
                if not self.whole[w]:
                    self._copy(w, jk, 1 - self.c, 3 + kk, self.sibling).wait_recv()
                    self._copy(w, jk, self.c, 3 + kk, self.sibling).wait_send()


def _allgather_weights(slotted, whole):
    n = len(slotted)

    def body(*refs):
        g = _Gather(refs[n:2 * n], refs[2 * n], refs[2 * n + 1], whole)
        g.send()
        g.relay()
        g.finish()

    any_spec = pl.BlockSpec(memory_space=pl.ANY)
    return pl.pallas_call(
        body, name="allgather_weights",
        in_specs=[any_spec] * n, out_specs=[any_spec] * n,
        out_shape=[jax.ShapeDtypeStruct(a.shape, a.dtype) for a in slotted],
        input_output_aliases={i: i for i in range(n)},
        scratch_shapes=[pltpu.SemaphoreType.DMA((n, 6)), pltpu.SemaphoreType.DMA((n, 6))],
    )(*slotted)


class _ChipExchange:
    def __init__(self, ins, outs, send_sems, recv_sems):
        self.ins, self.outs, self.send_sems, self.recv_sems = ins, outs, send_sems, recv_sems
        self.x, self.y, self.c = _place()

    def _copies(self):
        for w in range(len(self.ins)):
            for kk, (fx, fy) in enumerate(_CHIP_FLIPS):
                px, py = _flip(self.x, fx), _flip(self.y, fy)
                yield pltpu.make_async_remote_copy(
                    src_ref=self.ins[w].at[2 * px + py], dst_ref=self.outs[w].at[kk],
                    send_sem=self.send_sems.at[w, kk], recv_sem=self.recv_sems.at[w, kk],
                    device_id=(px, py, self.c), device_id_type=MESH)

    def send(self):
        for cp in self._copies():
            cp.start()

    def finish(self):
        for cp in self._copies():
            cp.wait()


def _pair_exchange(grads, name):
    nw = len(grads)

    def body(*refs):
        ins, outs = refs[:nw], refs[nw:2 * nw]
        send_sems, recv_sems = refs[2 * nw:]
        x, y, c = _place()
        sibling = (x, y, 1 - c)
        cps = []
        for w in range(nw):
            half = ins[w].shape[1] // 2
            cp = pltpu.make_async_remote_copy(
                src_ref=ins[w].at[:, pl.ds((1 - c) * half, half), :], dst_ref=outs[w],
                send_sem=send_sems.at[w], recv_sem=recv_sems.at[w], device_id=sibling, device_id_type=MESH)
            cp.start()
            cps.append(cp)
        for cp in cps:
            cp.wait()

    any_spec = pl.BlockSpec(memory_space=pl.ANY)
    return pl.pallas_call(
        body, name=name,
        in_specs=[any_spec] * nw, out_specs=[any_spec] * nw,
        out_shape=[jax.ShapeDtypeStruct((g.shape[0], g.shape[1] // 2, g.shape[2]), g.dtype) for g in grads],
        scratch_shapes=[pltpu.SemaphoreType.DMA((nw,)), pltpu.SemaphoreType.DMA((nw,))],
    )(*grads)


def _pair_add(g, other, c_idx, *, rb, name):
    S, r, cdim = g.shape
    half = r // 2
    nb = half // rb

    def body(c_ref, g_ref, o_ref, out_ref):
        del c_ref
        out_ref[...] = (g_ref[...] + o_ref[...]).astype(BF16)

    grid_spec = pltpu.PrefetchScalarGridSpec(
        num_scalar_prefetch=1, grid=(S, nb),
        in_specs=[pl.BlockSpec((None, rb, cdim), lambda s, i, c_ref: (s, c_ref[0] * nb + i, 0)),
                  pl.BlockSpec((None, rb, cdim), lambda s, i, c_ref: (s, i, 0))],
        out_specs=pl.BlockSpec((None, rb, cdim), lambda s, i, c_ref: (s, i, 0)))
    return pl.pallas_call(
        body, name=name, grid_spec=grid_spec, out_shape=jax.ShapeDtypeStruct((S, half, cdim), BF16),
        compiler_params=_cparams(("parallel", "parallel")),
    )(c_idx, g, other)


def _chip_exchange(psums):
    nw = len(psums)

    def body(*refs):
        ex = _ChipExchange(refs[:nw], refs[nw:2 * nw], refs[2 * nw], refs[2 * nw + 1])
        ex.send()
        ex.finish()

    any_spec = pl.BlockSpec(memory_space=pl.ANY)
    return pl.pallas_call(
        body, name="grad_chip_exchange",
        in_specs=[any_spec] * nw, out_specs=[any_spec] * nw,
        out_shape=[jax.ShapeDtypeStruct((3,) + p.shape[1:], p.dtype) for p in psums],
        scratch_shapes=[pltpu.SemaphoreType.DMA((nw, 3)), pltpu.SemaphoreType.DMA((nw, 3))],
    )(*psums)


def _chip_sum(psum, parts, cj_idx, *, rb, name):
    S, half, cdim = psum.shape
    nb = half // rb

    def body(cj_ref, own_ref, p_ref, out_ref):
        del cj_ref
        f = lambda v: v.astype(F32)
        out_ref[...] = ((f(own_ref[...]) + f(p_ref[0])) + f(p_ref[1])) + f(p_ref[2])

    grid_spec = pltpu.PrefetchScalarGridSpec(
        num_scalar_prefetch=1, grid=(nb,),
        in_specs=[pl.BlockSpec((None, rb, cdim), lambda i, cj: (cj[1], i, 0)),
                  pl.BlockSpec((3, rb, cdim), lambda i, cj: (0, i, 0))],
        out_specs=pl.BlockSpec((rb, cdim), lambda i, cj: (cj[0] * nb + i, 0)))
    return pl.pallas_call(
        body, name=name, grid_spec=grid_spec, out_shape=jax.ShapeDtypeStruct((2 * half, cdim), F32),
        compiler_params=_cparams(("parallel",)),
    )(cj_idx, psum, parts)


SLAB_W = 1024


def _final_exchange(pieces, grads):
    n, nw = len(pieces), len(grads)
    segs, at = [], 0
    for idx, p in enumerate(pieces):
        r, wd = p.shape
        for c0 in range(0, wd, SLAB_W):
            if r > 1:
                at = -(-at // 8) * 8
            segs.append((idx, c0, min(SLAB_W, wd - c0), at))
            at += r
    rows = -(-at // 8) * 8
    flips = [(fx, fy, fc) for fx in (0, 1) for fy in (0, 1) for fc in (0, 1)][1:]

    def body(*refs):
        ins = refs[:n]
        outs = refs[n + nw:2 * n + nw]
        g_refs = refs[2 * n + nw:2 * n + 2 * nw]
        mine_ref, slots_ref, send_sems, recv_sems, gsend_sems, grecv_sems = refs[2 * n + 2 * nw:]
        x, y, c = _place()
        me = 4 * x + 2 * y + c
        sibling = (x, y, 1 - c)

        def swap(w, core):
            half = g_refs[w].shape[0] // 2
            rows_ = g_refs[w].at[pl.ds(core * half, half), :]
            return pltpu.make_async_remote_copy(
                src_ref=rows_, dst_ref=rows_, send_sem=gsend_sems.at[w], recv_sem=grecv_sems.at[w],
                device_id=sibling, device_id_type=MESH)

        for w in range(nw):
            swap(w, c).start()
        mine_ref[...] = jnp.zeros_like(mine_ref)
        for idx, c0, wd, st in segs:
            r = ins[idx].shape[0]
            mine_ref[st:st + r, 0:wd] = ins[idx][:, c0:c0 + wd]
        slots_ref[me] = mine_ref[...]
        cps = []
        for kk, (fx, fy, fc) in enumerate(flips):
            cp = pltpu.make_async_remote_copy(
                src_ref=mine_ref, dst_ref=slots_ref.at[me], send_sem=send_sems.at[kk], recv_sem=recv_sems.at[kk],
                device_id=(_flip(x, fx), _flip(y, fy), _flip(c, fc)), device_id_type=MESH)
            cp.start()
            cps.append(cp)
        for cp in cps:
            cp.wait()
        tot = slots_ref[0]
        for d in range(1, 8):
            tot = tot + slots_ref[d]
        mine_ref[...] = tot
        for idx, c0, wd, st in segs:
            r = ins[idx].shape[0]
            val = mine_ref[st:st + r, 0:wd]
            if idx == n - 1:
                outs[idx][...] = jnp.sum(val, keepdims=True)
            else:
                outs[idx][:, c0:c0 + wd] = val
        for w in range(nw):
            swap(w, 1 - c).wait_recv()
            swap(w, c).wait_send()

    vm = pl.BlockSpec(memory_space=pltpu.VMEM)
    hbm = pl.BlockSpec(memory_space=pl.ANY)
    out_shape = ([jax.ShapeDtypeStruct(p.shape, F32) for p in pieces[:-1]] + [jax.ShapeDtypeStruct((1, 1), F32)]
                 + [jax.ShapeDtypeStruct(g.shape, g.dtype) for g in grads])
    res = pl.pallas_call(
        body, name="final_exchange", in_specs=[vm] * n + [hbm] * nw, out_specs=[vm] * n + [hbm] * nw,
        out_shape=out_shape, input_output_aliases={n + w: n + w for w in range(nw)},
        scratch_shapes=[pltpu.VMEM((rows, SLAB_W), F32), pltpu.VMEM((8, rows, SLAB_W), F32),
                        pltpu.SemaphoreType.DMA((7,)), pltpu.SemaphoreType.DMA((7,)),
                        pltpu.SemaphoreType.DMA((nw,)), pltpu.SemaphoreType.DMA((nw,))],
    )(*pieces, *grads)
    return res[:n], res[n:]


def _adamw(w, g, m, v, *, rb, name):
    r, cdim = w.shape

    def body(w_ref, g_ref, m_ref, v_ref, go_ref, d_ref, nm_ref, nv_ref):
        go_ref[...] = g_ref[...]
        d_ref[...], nm_ref[...], nv_ref[...] = _adamw_update(w_ref[...], g_ref[...], m_ref[...], v_ref[...])

    spec = pl.BlockSpec((rb, cdim), lambda i: (i, 0))
    shp = jax.ShapeDtypeStruct((r, cdim), F32)
    return pl.pallas_call(
        body, name=name, grid=(r // rb,), in_specs=[spec] * 4, out_specs=[spec] * 4, out_shape=[shp] * 4,
        compiler_params=_cparams(("parallel",)),
    )(w, g, m, v)


def _adamw_update(w, g, m, v):
    nm = ADAM_B1 * m + (1.0 - ADAM_B1) * g
    nv = ADAM_B2 * v + (1.0 - ADAM_B2) * (g * g)
    m_hat = nm / (1.0 - ADAM_B1 ** ADAM_STEP)
    v_hat = nv / (1.0 - ADAM_B2 ** ADAM_STEP)
    return -ADAM_LR * (m_hat / (jnp.sqrt(v_hat) + ADAM_EPS) + ADAM_WD * w), nm, nv


def _adamw_small(params):
    n = len(params)

    def body(*refs):
        ins, outs = refs[:4 * n], refs[4 * n:]
        for p in range(n):
            w_ref, g_ref, m_ref, v_ref = ins[4 * p:4 * p + 4]
            d, nm, nv = _adamw_update(w_ref[...], g_ref[...], m_ref[...], v_ref[...])
            outs[3 * p][...] = d
            outs[3 * p + 1][...] = nm
            outs[3 * p + 2][...] = nv

    vm = pl.BlockSpec(memory_space=pltpu.VMEM)
    flat = [a for p in params for a in p]
    out_shape = [jax.ShapeDtypeStruct(p[0].shape, F32) for p in params for _ in range(3)]
    res = pl.pallas_call(
        body, name="adamw_small", in_specs=[vm] * (4 * n), out_specs=[vm] * (3 * n), out_shape=out_shape,
    )(*flat)
    return [tuple(res[3 * p:3 * p + 3]) for p in range(n)]


_PAIR_ADD_ROWS = {"w_in": 256, "w_out": 128, "w_up": 256, "w_down": 176}


def _pair_sums(grads, names, c_idx, tag):
    others = _pair_exchange(grads, name=f"grad_pair_exchange_{tag}")
    return [_pair_add(g, o_, c_idx, rb=_PAIR_ADD_ROWS[n], name=f"pair_add_{n}") for g, o_, n in zip(grads, others, names)]


def _local_step(x, tgt, meta_full, lb_param, attn_norm_w, w_in_g, hgrn_norm_w, conv_w_full, w_out_full,
                ffn_norm_w, late_slotted, fcw_full, ffn_conv_b, final_norm_w, c_idx, cj_idx):
    seq = x.shape[0]
    T = TR + seq
    head_tile = jnp.concatenate([jnp.zeros((PAD, D), F32), meta_full], axis=0)
    whn_t = jnp.tile(hgrn_norm_w, (1, NH))

    ut, projb, o, sst, mt, h1, w_up_g, w_down_g = _mix_block_fwd(
        x, head_tile, lb_param, attn_norm_w, whn_t, conv_w_full, w_in_g, w_out_full, late_slotted)
    w_down_full = w_down_g.reshape(DFF, D)
    u2t, upb, ggt, dh2, loss_vec, dwfin = _ffn_block_fwd(
        h1, tgt, ffn_norm_w, w_up_g, fcw_full, ffn_conv_b, w_down_full, final_norm_w.reshape(1, D))

    dup, dh1, dfw, dfb, dwffn = _ffn_block_bwd_pipelined(
        dh2, upb, h1, ffn_norm_w, w_up_g, fcw_full, ffn_conv_b, w_down_full)
    kb = 1408 if T % 1408 == 0 else TR
    g_up = _matmul(u2t, dup, mode="nn", bm=D, bn=1408, bk=kb, out_dtype=F32, name="dw_up_mm",
                   out_shards=(NSHARD, 2 * DFF // NSHARD))
    g_down = _matmul(ggt, dh2, mode="nn", bm=1408, bn=D, bk=kb, out_dtype=F32, name="dw_down_mm")
    ps_ffn = _pair_sums([g_up, g_down.reshape(NSHARD, DFF // NSHARD, D)], ("w_up", "w_down"), c_idx, "ffn")

    dproj, dlb, dwhn, dcw, *parts_ffn = _mix_block_bwd(
        dh1, projb, o, sst, lb_param, whn_t, conv_w_full, w_out_full, ps_ffn)
    g_in = _matmul(ut, dproj, mode="nn", bm=D, bn=2304, bk=kb, out_dtype=F32, name="dw_in_mm",
                   out_shards=(NSHARD, 9 * D // NSHARD))
    g_out = _matmul(mt, dh1, mode="nn", bm=D, bn=D, bk=kb, out_dtype=F32, name="dw_out_mm")
    ps_mix = _pair_sums([g_in, g_out.reshape(NSHARD, D // NSHARD, D)], ("w_in", "w_out"), c_idx, "mix")

    grad_x, dmeta, dwattn, *parts_mix = _input_grad_block(dproj, x, head_tile, dh1, attn_norm_w, w_in_g, ps_mix)

    halves = [_chip_sum(ps, p, cj_idx, rb=_PAIR_ADD_ROWS[n], name=f"chip_sum_{n}")
              for ps, p, n in zip(ps_mix + ps_ffn, parts_mix + parts_ffn, ("w_in", "w_out", "w_up", "w_down"))]
    small = dict(dlb=dlb, dwattn=dwattn, dwhn=dwhn, dwffn=dwffn, dfb=dfb, dwfin=dwfin,
                 dcw=dcw, dfw=dfw, dmeta=dmeta, loss=loss_vec)
    return grad_x, small, halves


_SMALL_ORDER = ("dmeta", "dcw", "dfw", "dlb", "dwattn", "dwhn", "dwffn", "dfb", "dwfin", "loss")


def kernel(x, meta_tokens, lb_param, attn_norm_w, w_in, hgrn_norm_w, conv_w, w_out, ffn_norm_w, w_up, ffn_conv_w, ffn_conv_b, w_down, final_norm_w, loss_target, m_meta_tokens, m_lb_param, m_attn_norm_w, m_w_in, m_hgrn_norm_w, m_conv_w, m_w_out, m_ffn_norm_w, m_w_up, m_ffn_conv_w, m_ffn_conv_b, m_w_down, m_final_norm_w, v_meta_tokens, v_lb_param, v_attn_norm_w, v_w_in, v_hgrn_norm_w, v_conv_w, v_w_out, v_ffn_norm_w, v_w_up, v_ffn_conv_w, v_ffn_conv_b, v_w_down, v_final_norm_w):
    xi, yi, ci = _place()
    j = 2 * xi + yi
    c_idx = jnp.reshape(ci, (1,)).astype(jnp.int32)

    j_idx = jnp.reshape(j, (1,)).astype(jnp.int32)
    ds_, fs_ = D // NSHARD, DFF // NSHARD
    widen = lambda a: jnp.pad(a, ((0, 0), (0, 768 - a.shape[1])))
    rows_small = jnp.concatenate([widen(meta_tokens), widen(conv_w[0]), widen(ffn_conv_w[0]),
                                  jnp.zeros((2, 768), F32)], axis=0)
    slotted = [_into_slot(w[0], j_idx, rb=rb, dtype=BF16, name=f"slot_{n}")
               for w, rb, n in ((w_in, 256, "w_in"), (w_out, 128, "w_out"), (w_up, 256, "w_up"), (w_down, 176, "w_down"))]
    s_in, s_out, s_up, s_down = slotted
    s_small = _into_slot(rows_small, j_idx, rb=rows_small.shape[0], dtype=F32, name="slot_small")
    w_in_g, w_out_g, small_g = _allgather_weights([s_in, s_out, s_small], (False, False, True))
    unshard = lambda a: jnp.transpose(a, (1, 0, 2)).reshape(a.shape[1], -1)
    meta_full = unshard(small_g[:, 0:NMETA, 0:ds_])
    conv_w_full = unshard(small_g[:, NMETA:NMETA + 3, 0:ds_])
    fcw_full = unshard(small_g[:, NMETA + 3:NMETA + 6, 0:fs_])

    cj_idx = jnp.stack([ci, j]).astype(jnp.int32)
    grad_x, small, halves = _local_step(
        x[0], loss_target[0], meta_full, lb_param, attn_norm_w, w_in_g, hgrn_norm_w, conv_w_full,
        w_out_g.reshape(D, D), ffn_norm_w, [s_up, s_down], fcw_full, ffn_conv_b, final_norm_w, c_idx, cj_idx)

    names = _SMALL_ORDER
    small_sums, g_big = _final_exchange([small[n] for n in names], halves)
    vals = dict(zip(names, small_sums))
    loss = vals["loss"].reshape(())
    g_small = {
        "meta_tokens": lax.dynamic_slice_in_dim(vals["dmeta"], j * (D // NSHARD), D // NSHARD, axis=1),
        "lb_param": jnp.concatenate([vals["dlb"], -vals["dlb"]], axis=0),
        "attn_norm_w": vals["dwattn"],
        "hgrn_norm_w": vals["dwhn"],
        "conv_w": lax.dynamic_slice_in_dim(vals["dcw"], j * (D // NSHARD), D // NSHARD, axis=1)[None],
        "ffn_norm_w": vals["dwffn"],
        "ffn_conv_w": lax.dynamic_slice_in_dim(vals["dfw"], j * (DFF // NSHARD), DFF // NSHARD, axis=1)[None],
        "ffn_conv_b": vals["dfb"],
        "final_norm_w": vals["dwfin"].reshape(D),
    }


    weights = {"meta_tokens": meta_tokens, "lb_param": lb_param, "attn_norm_w": attn_norm_w, "w_in": w_in,
               "hgrn_norm_w": hgrn_norm_w, "conv_w": conv_w, "w_out": w_out, "ffn_norm_w": ffn_norm_w,
               "w_up": w_up, "ffn_conv_w": ffn_conv_w, "ffn_conv_b": ffn_conv_b, "w_down": w_down,
               "final_norm_w": final_norm_w}
    ms = {"meta_tokens": m_meta_tokens, "lb_param": m_lb_param, "attn_norm_w": m_attn_norm_w, "w_in": m_w_in,
          "hgrn_norm_w": m_hgrn_norm_w, "conv_w": m_conv_w, "w_out": m_w_out, "ffn_norm_w": m_ffn_norm_w,
          "w_up": m_w_up, "ffn_conv_w": m_ffn_conv_w, "ffn_conv_b": m_ffn_conv_b, "w_down": m_w_down,
          "final_norm_w": m_final_norm_w}
    vs = {"meta_tokens": v_meta_tokens, "lb_param": v_lb_param, "attn_norm_w": v_attn_norm_w, "w_in": v_w_in,
          "hgrn_norm_w": v_hgrn_norm_w, "conv_w": v_conv_w, "w_out": v_w_out, "ffn_norm_w": v_ffn_norm_w,
          "w_up": v_w_up, "ffn_conv_w": v_ffn_conv_w, "ffn_conv_b": v_ffn_conv_b, "w_down": v_w_down,
          "final_norm_w": v_final_norm_w}
    order = list(weights)
    grads, deltas, new_m, new_v = {}, {}, {}, {}

    for name, g, rb in zip(("w_in", "w_out", "w_up", "w_down"), g_big, (256, 128, 256, 176)):
        shp = weights[name].shape
        w2, m2, v2 = (a.reshape(shp[1], shp[2]) for a in (weights[name], ms[name], vs[name]))
        g_, d_, nm_, nv_ = _adamw(w2, g, m2, v2, rb=rb, name=f"adamw_{name}")
        grads[name], deltas[name], new_m[name], new_v[name] = (a.reshape(shp) for a in (g_, d_, nm_, nv_))

    small_names = [n for n in order if n not in grads]
    as2d = lambda a: a.reshape(-1, a.shape[-1])
    res = _adamw_small([tuple(as2d(a) for a in (weights[n], g_small[n], ms[n], vs[n])) for n in small_names])
    for n, (d_, nm_, nv_) in zip(small_names, res):
        shp = weights[n].shape
        grads[n], deltas[n], new_m[n], new_v[n] = (a.reshape(shp) for a in (g_small[n], d_, nm_, nv_))

    return (loss, grad_x[None], *[grads[n] for n in order], *[deltas[n] for n in order],
            *[new_m[n] for n in order], *[new_v[n] for n in order])
```

```python
import functools

import jax
import jax.numpy as jnp
from jax import lax
from jax.experimental import pallas as pl
from jax.experimental.pallas import tpu as pltpu

F32 = jnp.float32
BF16 = jnp.bfloat16
MESH = pl.DeviceIdType.MESH

D = 1024
NH = 8
HD = 128
DFF = 2816
NMETA = 16
EPS = 1e-6
TR = 256
PAD = TR - NMETA
CH = 64
NSHARD = 4
VMEM_LIMIT = 62 * 1024 * 1024

ADAM_LR = 0.001
ADAM_B1 = 0.9
ADAM_B2 = 0.999
ADAM_EPS = 1e-08
ADAM_WD = 0.01
ADAM_STEP = 10


def _cparams(semantics=None, **kw):
    return pltpu.CompilerParams(dimension_semantics=semantics, vmem_limit_bytes=VMEM_LIMIT, **kw)


def _sigmoid(x):
    return 0.5 * jnp.tanh(0.5 * x) + 0.5


def _weight_grad(at, b, *, bn, bk, name, shard_cols=None):
    M, T = at.shape
    N = b.shape[1]
    assert b.shape[0] == T and T % bk == 0 and N % bn == 0, (name, M, N, T, bn, bk)
    if shard_cols is None:
        nsh = 1
        out_shape = jax.ShapeDtypeStruct((M, N), F32)
        out_spec = pl.BlockSpec((M, bn), lambda j, k: (0, j))
    else:
        assert bn % shard_cols == 0 and N % shard_cols == 0
        nsh = bn // shard_cols
        out_shape = jax.ShapeDtypeStruct((N // shard_cols, M, shard_cols), F32)
        out_spec = pl.BlockSpec((nsh, M, shard_cols), lambda j, k: (j, 0, 0))

    def body(a_ref, b_ref, o_ref):
        @pl.when(pl.program_id(1) == 0)
        def _():
            o_ref[...] = jnp.zeros_like(o_ref)

        p = jnp.dot(a_ref[...].astype(BF16), b_ref[...].astype(BF16), preferred_element_type=F32)
        if shard_cols is None:
            o_ref[...] += p
        else:
            for q in range(nsh):
                o_ref[q] += p[:, q * shard_cols:(q + 1) * shard_cols]

    return pl.pallas_call(
        body, name=name, grid=(N // bn, T // bk),
        in_specs=[pl.BlockSpec((M, bk), lambda j, k: (0, k)), pl.BlockSpec((bk, bn), lambda j, k: (k, j))],
        out_specs=out_spec, out_shape=out_shape, compiler_params=_cparams(("parallel", "arbitrary")),
    )(at, b)


def _matmul(a, b, *, mode, bm, bn, bk, out_dtype, name, residual=None, out_shards=None):
    if mode == "tn":
        K, M = a.shape
    else:
        M, K = a.shape
    b3 = b.ndim == 3
    if b3:
        S, R, Cs = b.shape
        bcols = S * Cs
        brows = R
    else:
        brows, bcols = b.shape
    if mode == "nt":
        N = brows
        assert bcols == K
    else:
        N = bcols
        assert brows == K
    assert M % bm == 0 and N % bn == 0 and K % bk == 0, (name, M, N, K, bm, bn, bk)
    nm, nn, nk = M // bm, N // bn, K // bk

    if mode == "tn":
        a_spec = pl.BlockSpec((bk, bm), lambda i, j, k: (k, i))
    else:
        a_spec = pl.BlockSpec((bm, bk), lambda i, j, k: (i, k))
    if mode == "nt":
        if b3:
            assert bn == R and Cs % bk == 0
            per = Cs // bk
            b_spec = pl.BlockSpec((None, bn, bk), lambda i, j, k: (k // per, 0, k % per))
        else:
            b_spec = pl.BlockSpec((bn, bk), lambda i, j, k: (j, k))
    else:
        if b3:
            assert bk == R and Cs % bn == 0
            per = Cs // bn
            b_spec = pl.BlockSpec((None, bk, bn), lambda i, j, k: (j // per, 0, j % per))
        else:
            b_spec = pl.BlockSpec((bk, bn), lambda i, j, k: (k, j))
    in_specs = [a_spec, b_spec]
    args = [a, b]
    if residual is not None:
        in_specs.append(pl.BlockSpec((bm, bn), lambda i, j, k: (i, j)))
        args.append(residual)
    if out_shards is not None:
        So, Co = out_shards
        assert So * Co == N and Co % bn == 0 and bm == M
        pero = Co // bn
        out_shape = jax.ShapeDtypeStruct((So, M, Co), out_dtype)
        out_spec = pl.BlockSpec((None, bm, bn), lambda i, j, k: (j // pero, 0, j % pero))
    else:
        out_shape = jax.ShapeDtypeStruct((M, N), out_dtype)
        out_spec = pl.BlockSpec((bm, bn), lambda i, j, k: (i, j))

    assert mode == "nn" and residual is None and out_dtype == F32

    def body(a_ref, b_ref, o_ref):
        @pl.when(pl.program_id(2) == 0)
        def _():
            o_ref[...] = jnp.zeros_like(o_ref)

        o_ref[...] += jnp.dot(a_ref[...].astype(BF16), b_ref[...].astype(BF16), preferred_element_type=F32)

    return pl.pallas_call(
        body, name=name, grid=(nm, nn, nk), in_specs=in_specs, out_specs=out_spec, out_shape=out_shape,
        compiler_params=_cparams(("parallel", "parallel", "arbitrary")),
    )(*args)


def _norm_fwd(h, w, *, bm, name):
    T = h.shape[0]

    def body(h_ref, w_ref, o_ref):
        x = h_ref[...]
        r = lax.rsqrt(jnp.mean(x * x, axis=-1, keepdims=True) + EPS)
        o_ref[...] = (x * r * w_ref[...]).astype(BF16)

    return pl.pallas_call(
        body, name=name, grid=(T // bm,),
        in_specs=[pl.BlockSpec((bm, D), lambda i: (i, 0)), pl.BlockSpec((1, D), lambda i: (0, 0))],
        out_specs=pl.BlockSpec((bm, D), lambda i: (i, 0)),
        out_shape=jax.ShapeDtypeStruct((T, D), BF16), compiler_params=_cparams(("parallel",)),
    )(h, w)


def _norm_bwd(h, du, dres, w, *, bm, name):
    T = h.shape[0]

    def body(h_ref, du_ref, dres_ref, w_ref, dh_ref, dw_ref):
        i = pl.program_id(0)
        x = h_ref[...]
        r = lax.rsqrt(jnp.mean(x * x, axis=-1, keepdims=True) + EPS)
        n = x * r
        du_v = du_ref[...]
        dn = du_v * w_ref[...]
        dh_ref[...] = dres_ref[...] + r * (dn - n * jnp.mean(dn * n, axis=-1, keepdims=True))
        part = jnp.sum(du_v * n, axis=0, keepdims=True)

        @pl.when(i == 0)
        def _():
            dw_ref[...] = part

        @pl.when(i > 0)
        def _():
            dw_ref[...] += part

    row = pl.BlockSpec((bm, D), lambda i: (i, 0))
    vec = pl.BlockSpec((1, D), lambda i: (0, 0))
    return pl.pallas_call(
        body, name=name, grid=(T // bm,), in_specs=[row, row, row, vec], out_specs=[row, vec],
        out_shape=[jax.ShapeDtypeStruct((T, D), F32), jax.ShapeDtypeStruct((1, D), F32)],
        compiler_params=_cparams(("arbitrary",)),
    )(h, du, dres, w)


def _input_grad_block(dproj, x_seq, head_tile, dres, w, w_in_g, psums):
    T = TR + x_seq.shape[0]
    nt = T // TR
    nsl = w_in_g.shape[0]
    wsl = w_in_g.shape[2]
    wcol = wsl // 2
    nex = len(psums)

    def body(*refs):
        dp_ref, x_ref, head_ref, dres_ref, w_ref, win_ref = refs[:6]
        gx_ref, dmeta_ref, dw_ref = refs[6 + nex:9 + nex]
        exchange = _ChipExchange(refs[6:6 + nex], refs[9 + nex:9 + 2 * nex], refs[-2], refs[-1])
        i = pl.program_id(0)

        @pl.when(i == 0)
        def _():
            exchange.send()

        @pl.when(i == nt - 1)
        def _():
            exchange.finish()

        x = jnp.where(i == 0, head_ref[...], x_ref[...])
        r = lax.rsqrt(jnp.mean(x * x, axis=-1, keepdims=True) + EPS)
        n = x * r
        du_v = None
        for s in range(nsl):
            for hf in range(2):
                part = _dot_nt(dp_ref[:, s * wsl + hf * wcol:s * wsl + (hf + 1) * wcol],
                               win_ref[s, :, hf * wcol:(hf + 1) * wcol])
                du_v = part if du_v is None else du_v + part
        dn = du_v * w_ref[...]
        dh = dres_ref[...] + r * (dn - n * jnp.mean(dn * n, axis=-1, keepdims=True))
        gx_ref[...] = dh
        part = jnp.sum(du_v * n, axis=0, keepdims=True)

        @pl.when(i == 0)
        def _():
            dmeta_ref[...] = dh[PAD:TR, :]
            dw_ref[...] = part

        @pl.when(i > 0)
        def _():
            dw_ref[...] += part

    row = pl.BlockSpec((TR, D), lambda i: (i, 0))
    vec = pl.BlockSpec((1, D), lambda i: (0, 0))
    hbm = pl.BlockSpec(memory_space=pl.ANY)
    return pl.pallas_call(
        body, name="input_grad_block", grid=(nt,),
        in_specs=[pl.BlockSpec((TR, 9 * D), lambda i: (i, 0)),
                  pl.BlockSpec((TR, D), lambda i: (jnp.maximum(i - 1, 0), 0)), pl.BlockSpec((TR, D), lambda i: (0, 0)),
                  row, vec, pl.BlockSpec(memory_space=pltpu.VMEM)] + [hbm] * nex,
        out_specs=[pl.BlockSpec((TR, D), lambda i: (jnp.maximum(i - 1, 0), 0)),
                   pl.BlockSpec((NMETA, D), lambda i: (0, 0)), vec] + [hbm] * nex,
        out_shape=[jax.ShapeDtypeStruct((T - TR, D), F32), jax.ShapeDtypeStruct((NMETA, D), F32),
                   jax.ShapeDtypeStruct((1, D), F32)]
        + [jax.ShapeDtypeStruct((3,) + p.shape[1:], p.dtype) for p in psums],
        scratch_shapes=[pltpu.SemaphoreType.DMA((nex, 3)), pltpu.SemaphoreType.DMA((nex, 3))],
        compiler_params=_cparams(("arbitrary",)),
    )(dproj, x_seq, head_tile, dres, w, w_in_g, *psums)


def _tri_matmul(tri_bf16, x):
    hi = x.astype(BF16)
    lo = (x - hi.astype(F32)).astype(BF16)
    return jnp.dot(tri_bf16, lo, preferred_element_type=F32) + jnp.dot(tri_bf16, hi, preferred_element_type=F32)


def _shift_down(x, prev8, n):
    rows = x.shape[0]
    return pltpu.roll(jnp.concatenate([prev8, x], axis=0), n, 0)[8:8 + rows, :]


def _shift_up(x, next8, n):
    rows = x.shape[0]
    return pltpu.roll(jnp.concatenate([x, next8], axis=0), rows + 8 - n, 0)[0:rows, :]


def _gates(f_raw, lb):
    sg = _sigmoid(f_raw)
    f = lb + (1.0 - lb) * sg
    return sg, f, jnp.log(f), 1.0 - f


def _lower_bound(lbp_ref):
    return _sigmoid(lbp_ref[0:1, :] - lbp_ref[1:2, :])


def _tri_masks():
    r = lax.broadcasted_iota(jnp.int32, (CH, CH), 0)
    c = lax.broadcasted_iota(jnp.int32, (CH, CH), 1)
    return r >= c, r <= c


def _ones_where(mask):
    return jnp.where(mask, 1.0, 0.0).astype(BF16)


def _dot(a, b):
    return jnp.dot(a.astype(BF16), b.astype(BF16), preferred_element_type=F32)


def _dot_nt(a, b):
    return lax.dot_general(a.astype(BF16), b.astype(BF16), (((1,), (1,)), ((), ())), preferred_element_type=F32)


def _dot_tn(a, b):
    return lax.dot_general(a.astype(BF16), b.astype(BF16), (((0,), (0,)), ((), ())), preferred_element_type=F32)


def _mix_block_fwd(x_seq, head_tile, lb_param, wattn, whn, conv_w, w_in_g, w_out, late):
    T = TR + x_seq.shape[0]
    nt = T // TR
    ncht = TR // CH
    nsl = w_in_g.shape[0]
    wsl = w_in_g.shape[2]
    wcol = wsl // 2

    nlate = len(late)

    def body(*refs):
        x_ref, head_ref, lbp_ref, wa_ref, whn_ref, cw_ref, win_ref, wout_ref = refs[:8]
        ut_ref, projb_ref, o_ref, sst_ref, mt_ref, h1_ref = refs[8 + nlate:14 + nlate]
        late_refs = refs[14 + nlate:14 + 2 * nlate]
        proj_ref, m_ref, st_ref, cxc_ref, send_sems, recv_sems = refs[14 + 2 * nlate:]
        i = pl.program_id(0)
        gather = _Gather(late_refs, send_sems, recv_sems, (False,) * nlate)

        @pl.when(i == 0)
        def _():
            st_ref[...] = jnp.zeros_like(st_ref)
            cxc_ref[...] = jnp.zeros_like(cxc_ref)
            gather.send()

        @pl.when(i == nt // 2)
        def _():
            gather.relay()

        x = jnp.where(i == 0, head_ref[...], x_ref[...])
        r1 = lax.rsqrt(jnp.mean(x * x, axis=-1, keepdims=True) + EPS)
        u_f = x * r1 * wa_ref[...]
        u = u_f.astype(BF16)
        ut_ref[...] = u_f.T.astype(BF16)
        for s in range(nsl):
            for hf in range(2):
                cols = slice(s * wsl + hf * wcol, s * wsl + (hf + 1) * wcol)
                p = jnp.dot(u, win_ref[s, :, hf * wcol:(hf + 1) * wcol], preferred_element_type=F32)
                proj_ref[:, cols] = p
                projb_ref[:, cols] = p.astype(BF16)

        lb = _lower_bound(lbp_ref)
        lower, _ = _tri_masks()
        ltri = _ones_where(lower)
        whn_v = whn_ref[...]
        w0, w1, w2 = cw_ref[0:1, :], cw_ref[1:2, :], cw_ref[2:3, :]

        def chunk(c, carry):
            rows = pl.ds(pl.multiple_of(c * CH, CH), CH)
            q_raw = proj_ref[rows, 0:D]
            f_raw = proj_ref[rows, D:2 * D]
            v = proj_ref[rows, 2 * D:3 * D]
            q = q_raw * _sigmoid(q_raw)
            _, _, g, k = _gates(f_raw, lb)
            gam = _tri_matmul(ltri, g)
            gam_l = gam[CH - 1:CH, :]
            e_l = jnp.exp(gam_l)
            qt = (q * jnp.exp(gam)).astype(BF16)
            kt = (k * jnp.exp(-gam)).astype(BF16)
            khat = (k * jnp.exp(gam_l - gam)).astype(BF16)
            vb = v.astype(BF16)
            on_parts = []
            for h in range(NH):
                cs = slice(h * HD, (h + 1) * HD)
                st = st_ref[h]
                sst_ref[c, h] = st
                a = jnp.where(lower, _dot_nt(qt[:, cs], kt[:, cs]), 0.0)
                o_h = _dot_nt(qt[:, cs], st) + _dot(a, vb[:, cs])
                st_ref[h] = st * e_l[:, cs] + _dot_tn(vb[:, cs], khat[:, cs])
                o_ref[rows, cs] = o_h
                ro = lax.rsqrt(jnp.mean(o_h * o_h, axis=-1, keepdims=True) + EPS)
                on_parts.append(o_h * ro)
            on = jnp.concatenate(on_parts, axis=1)
            g_out = proj_ref[rows, 3 * D:4 * D]
            y_a = on * whn_v * (g_out * _sigmoid(g_out))
            cx = proj_ref[rows, 5 * D:6 * D] * proj_ref[rows, 6 * D:7 * D]
            prev8 = cxc_ref[...]
            cv = w0 * _shift_down(cx, prev8, 2) + w1 * _shift_down(cx, prev8, 1) + w2 * cx
            cxc_ref[...] = cx[CH - 8:CH, :]
            y_b = proj_ref[rows, 4 * D:5 * D] * cv
            m = _sigmoid(proj_ref[rows, 7 * D:8 * D]) * y_a + _sigmoid(proj_ref[rows, 8 * D:9 * D]) * y_b
            m_ref[rows, :] = m
            return carry

        lax.fori_loop(0, ncht, chunk, 0)
        m_v = m_ref[...]
        h1_ref[...] = x + jnp.dot(m_v.astype(BF16), wout_ref[...], preferred_element_type=F32)
        mt_ref[...] = m_v.T.astype(BF16)

        @pl.when(i == nt - 1)
        def _():
            gather.finish()

    row = lambda w: pl.BlockSpec((TR, w), lambda i: (i, 0))
    col = lambda w: pl.BlockSpec((w, TR), lambda i: (0, i))
    vec = lambda r: pl.BlockSpec((r, D), lambda i: (0, 0))
    vm = pl.BlockSpec(memory_space=pltpu.VMEM)
    hbm = pl.BlockSpec(memory_space=pl.ANY)
    return pl.pallas_call(
        body, name="mix_block_fwd", grid=(nt,),
        in_specs=[pl.BlockSpec((TR, D), lambda i: (jnp.maximum(i - 1, 0), 0)), pl.BlockSpec((TR, D), lambda i: (0, 0)),
                  vec(2), vec(1), vec(1), vec(3), vm, vm] + [hbm] * nlate,
        out_specs=[col(D), row(9 * D), row(D), pl.BlockSpec((ncht, NH, HD, HD), lambda i: (i, 0, 0, 0)),
                   col(D), row(D)] + [hbm] * nlate,
        out_shape=[jax.ShapeDtypeStruct((D, T), BF16), jax.ShapeDtypeStruct((T, 9 * D), BF16),
                   jax.ShapeDtypeStruct((T, D), F32), jax.ShapeDtypeStruct((T // CH, NH, HD, HD), F32),
                   jax.ShapeDtypeStruct((D, T), BF16), jax.ShapeDtypeStruct((T, D), F32)]
        + [jax.ShapeDtypeStruct(a.shape, a.dtype) for a in late],
        input_output_aliases={8 + n: 6 + n for n in range(nlate)},
        scratch_shapes=[pltpu.VMEM((TR, 9 * D), F32), pltpu.VMEM((TR, D), F32), pltpu.VMEM((NH, HD, HD), F32),
                        pltpu.VMEM((8, D), F32), pltpu.SemaphoreType.DMA((nlate, 6)),
                        pltpu.SemaphoreType.DMA((nlate, 6))],
        compiler_params=_cparams(("arbitrary",)),
    )(x_seq, head_tile, lb_param, wattn, whn, conv_w, w_in_g, w_out, *late)


def _mix_block_bwd(dh1, projb, o, sst, lb_param, whn, conv_w, w_out, psums):
    T = projb.shape[0]
    nt = T // TR
    ncht = TR // CH
    tb16 = TR // 16
    nex = len(psums)

    def body(*refs):
        dh1_ref, proj_ref, pc_ref, px_ref, o_ref, sst_ref, lbp_ref, whn_ref, cw_ref, wout_ref = refs[:10]
        dproj_ref, dlb_ref, dwhn_ref, dcw_ref = refs[10 + nex:14 + nex]
        exchange = _ChipExchange(refs[10:10 + nex], refs[14 + nex:14 + 2 * nex], refs[-2], refs[-1])
        dm_ref, dst_ref, dcvc_ref, acc_lb, acc_hn, acc_cw = refs[14 + 2 * nex:-2]
        s = pl.program_id(0)
        tile = nt - 1 - s

        @pl.when(s == 0)
        def _():
            dst_ref[...] = jnp.zeros_like(dst_ref)
            dcvc_ref[...] = jnp.zeros_like(dcvc_ref)
            acc_lb[...] = jnp.zeros_like(acc_lb)
            acc_hn[...] = jnp.zeros_like(acc_hn)
            acc_cw[...] = jnp.zeros_like(acc_cw)
            exchange.send()

        dm_ref[...] = _dot_nt(dh1_ref[...], wout_ref[...])
        lb = _lower_bound(lbp_ref)
        lower, upper = _tri_masks()
        ltri = _ones_where(lower)
        utri = _ones_where(upper)
        whn_v = whn_ref[...]
        w0, w1, w2 = cw_ref[0:1, :], cw_ref[1:2, :], cw_ref[2:3, :]
        cx_before_tile = jnp.where(tile > 0, (pc_ref[...].astype(F32) * px_ref[...].astype(F32))[8:16, :], 0.0)
        rid = lax.broadcasted_iota(jnp.int32, (CH, D), 0)

        def chunk(cc, carry):
            c = ncht - 1 - cc
            r0 = pl.multiple_of(c * CH, CH)
            rows = pl.ds(r0, CH)
            slab = lambda n: proj_ref[rows, n * D:(n + 1) * D].astype(F32)
            q_raw, f_raw, v, g_out, b_gate, c_gate, x_conv = (slab(n) for n in range(7))
            sa = _sigmoid(slab(7))
            sb = _sigmoid(slab(8))
            dm_v = dm_ref[rows, :]

            sq = _sigmoid(q_raw)
            q = q_raw * sq
            sg, f, g, k = _gates(f_raw, lb)
            gam = _tri_matmul(ltri, g)
            gam_l = gam[CH - 1:CH, :]
            e_l = jnp.exp(gam_l)
            e_g = jnp.exp(gam)
            e_ng = jnp.exp(-gam)
            e_kl = jnp.exp(gam_l - gam)
            qt = q * e_g
            kt = k * e_ng
            khat = k * e_kl
            qt_b, kt_b, khat_b, v_b = qt.astype(BF16), kt.astype(BF16), khat.astype(BF16), v.astype(BF16)
            qt_seen, kt_seen = qt_b.astype(F32), kt_b.astype(F32)
            s_go = _sigmoid(g_out)
            silu_go = g_out * s_go
            cx = c_gate * x_conv
            rprev = pl.ds(pl.multiple_of(jnp.maximum(r0 - 16, 0), 16), 16)
            cx_prev_in = (proj_ref[rprev, 5 * D:6 * D].astype(F32) * proj_ref[rprev, 6 * D:7 * D].astype(F32))[8:16, :]
            prev8 = jnp.where(c > 0, cx_prev_in, cx_before_tile)
            cx_m1 = _shift_down(cx, prev8, 1)
            cx_m2 = _shift_down(cx, prev8, 2)
            cv = w0 * cx_m2 + w1 * cx_m1 + w2 * cx
            y_b = b_gate * cv

            o_v = o_ref[rows, :]
            ro_parts, on_parts = [], []
            for h in range(NH):
                cs = slice(h * HD, (h + 1) * HD)
                o_h = o_v[:, cs]
                ro = lax.rsqrt(jnp.mean(o_h * o_h, axis=-1, keepdims=True) + EPS)
                ro_parts.append(ro)
                on_parts.append(o_h * ro)
            on = jnp.concatenate(on_parts, axis=1)
            y_a = on * whn_v * silu_go

            dy_a = dm_v * sa
            dy_b = dm_v * sb
            dproj_ref[rows, 7 * D:8 * D] = (dm_v * y_a * sa * (1.0 - sa)).astype(BF16)
            dproj_ref[rows, 8 * D:9 * D] = (dm_v * y_b * sb * (1.0 - sb)).astype(BF16)
            dproj_ref[rows, 4 * D:5 * D] = (dy_b * cv).astype(BF16)
            dcv = dy_b * b_gate
            acc_cw[0:1, :] += jnp.sum(dcv * cx_m2, axis=0, keepdims=True)
            acc_cw[1:2, :] += jnp.sum(dcv * cx_m1, axis=0, keepdims=True)
            acc_cw[2:3, :] += jnp.sum(dcv * cx, axis=0, keepdims=True)
            next8 = dcvc_ref[...]
            dcx = w2 * dcv + w1 * _shift_up(dcv, next8, 1) + w0 * _shift_up(dcv, next8, 2)
            dcvc_ref[...] = dcv[0:8, :]
            dproj_ref[rows, 5 * D:6 * D] = (dcx * x_conv).astype(BF16)
            dproj_ref[rows, 6 * D:7 * D] = (dcx * c_gate).astype(BF16)
            don = dy_a * whn_v * silu_go
            dproj_ref[rows, 3 * D:4 * D] = (dy_a * on * whn_v * (s_go * (1.0 + g_out * (1.0 - s_go)))).astype(BF16)
            acc_hn[...] += jnp.sum(dy_a * silu_go * on, axis=0, keepdims=True)

            dq_parts, dk_parts, dv_parts, dgam_parts, ext_parts = [], [], [], [], []
            for h in range(NH):
                cs = slice(h * HD, (h + 1) * HD)
                on_h = on_parts[h]
                don_h = don[:, cs]
                do_h = ro_parts[h] * (don_h - on_h * jnp.mean(don_h * on_h, axis=-1, keepdims=True))
                qt_h, kt_h, khat_h, v_h = qt_b[:, cs], kt_b[:, cs], khat_b[:, cs], v_b[:, cs]
                do_b = do_h.astype(BF16)
                st = sst_ref[c, h]
                dstn = dst_ref[h]
                dstn_b = dstn.astype(BF16)
                a_t = jnp.where(upper, _dot_nt(kt_h, qt_h), 0.0)
                da = jnp.where(lower, _dot_nt(do_b, v_h), 0.0)
                da_t = jnp.where(upper, _dot_nt(v_h, do_b), 0.0)
                dv_h = _dot(a_t, do_b) + _dot_nt(khat_h, dstn_b)
                dqt_state = _dot(do_b, st)
                dqt_chunk = _dot(da, kt_h)
                dkt = _dot(da_t, qt_h)
                dkhat = _dot(v_h, dstn_b)
                dq_h = (dqt_state + dqt_chunk) * e_g[:, cs]
                dk_h = dkt * e_ng[:, cs] + dkhat * e_kl[:, cs]
                khat_dkhat = dkhat * khat[:, cs]
                ext = (jnp.sum(khat_dkhat, axis=0, keepdims=True)
                       + e_l[:, cs] * jnp.sum(st * dstn, axis=0, keepdims=True))
                dst_ref[h] = _dot_tn(do_b, qt_h) + dstn * e_l[:, cs]
                dq_parts.append(dq_h)
                dk_parts.append(dk_h)
                dv_parts.append(dv_h)
                dgam_parts.append(qt[:, cs] * dqt_state + qt_seen[:, cs] * dqt_chunk - kt_seen[:, cs] * dkt
                                  - khat_dkhat)
                ext_parts.append(ext)
            dq = jnp.concatenate(dq_parts, axis=1)
            dk = jnp.concatenate(dk_parts, axis=1)
            dgam = jnp.concatenate(dgam_parts, axis=1)
            ext = jnp.concatenate(ext_parts, axis=1)
            dgam = dgam + jnp.where(rid == CH - 1, ext, 0.0)
            dg = _tri_matmul(utri, dgam)
            dproj_ref[rows, 0:D] = (dq * (sq * (1.0 + q_raw * (1.0 - sq)))).astype(BF16)
            df = dg * jnp.exp(-g) - dk
            dproj_ref[rows, D:2 * D] = (df * (1.0 - lb) * sg * (1.0 - sg)).astype(BF16)
            dproj_ref[rows, 2 * D:3 * D] = jnp.concatenate(dv_parts, axis=1).astype(BF16)
            real = (tile * TR + r0 + rid) >= PAD
            acc_lb[...] += jnp.sum(jnp.where(real, df * (1.0 - sg), 0.0), axis=0, keepdims=True)
            return carry

        lax.fori_loop(0, ncht, chunk, 0)

        @pl.when(s == nt - 1)
        def _():
            dlb_ref[...] = acc_lb[...] * lb * (1.0 - lb)
            hn = acc_hn[...]
            tot = hn[:, 0:HD]
            for h in range(1, NH):
                tot = tot + hn[:, h * HD:(h + 1) * HD]
            dwhn_ref[...] = tot
            dcw_ref[...] = acc_cw[0:3, :]
            exchange.finish()

    hbm = pl.BlockSpec(memory_space=pl.ANY)
    rev = lambda s: (nt - 1 - s, 0)
    prevc = lambda s: (jnp.maximum((nt - 1 - s) * tb16 - 1, 0), 5)
    prevx = lambda s: (jnp.maximum((nt - 1 - s) * tb16 - 1, 0), 6)
    const = lambda s: (0, 0)
    return pl.pallas_call(
        body, name="mix_block_bwd", grid=(nt,),
        in_specs=[pl.BlockSpec((TR, D), rev),
                  pl.BlockSpec((TR, 9 * D), rev),
                  pl.BlockSpec((16, D), prevc),
                  pl.BlockSpec((16, D), prevx),
                  pl.BlockSpec((TR, D), rev),
                  pl.BlockSpec((ncht, NH, HD, HD), lambda s: (nt - 1 - s, 0, 0, 0)),
                  pl.BlockSpec((2, D), const),
                  pl.BlockSpec((1, D), const),
                  pl.BlockSpec((3, D), const),
                  pl.BlockSpec(memory_space=pltpu.VMEM)] + [hbm] * nex,
        out_specs=[pl.BlockSpec((TR, 9 * D), rev),
                   pl.BlockSpec((1, D), const),
                   pl.BlockSpec((1, HD), const),
                   pl.BlockSpec((3, D), const)] + [hbm] * nex,
        out_shape=[jax.ShapeDtypeStruct((T, 9 * D), BF16), jax.ShapeDtypeStruct((1, D), F32),
                   jax.ShapeDtypeStruct((1, HD), F32), jax.ShapeDtypeStruct((3, D), F32)]
        + [jax.ShapeDtypeStruct((3,) + p.shape[1:], p.dtype) for p in psums],
        scratch_shapes=[pltpu.VMEM((TR, D), F32), pltpu.VMEM((NH, HD, HD), F32), pltpu.VMEM((8, D), F32),
                        pltpu.VMEM((1, D), F32), pltpu.VMEM((1, D), F32), pltpu.VMEM((8, D), F32),
                        pltpu.SemaphoreType.DMA((nex, 3)), pltpu.SemaphoreType.DMA((nex, 3))],
        compiler_params=_cparams(("arbitrary",)),
    )(dh1, projb, projb, projb, o, sst, lb_param, whn, conv_w, w_out, *psums)


def _ffn_fwd(up, fcw, fcb):
    T = up.shape[0]
    nt = T // TR

    def body(up_ref, w_ref, b_ref, gg_ref, carry_ref):
        i = pl.program_id(0)

        @pl.when(i == 0)
        def _():
            carry_ref[...] = jnp.zeros_like(carry_ref)

        w0, w1, w2 = w_ref[0:1, :], w_ref[1:2, :], w_ref[2:3, :]

        def chunk(c, carry):
            rows = pl.ds(pl.multiple_of(c * CH, CH), CH)
            a_pre = up_ref[rows, 0:DFF]
            val = up_ref[rows, DFF:2 * DFF]
            prev8 = carry_ref[...]
            a = w0 * _shift_down(a_pre, prev8, 2) + w1 * _shift_down(a_pre, prev8, 1) + w2 * a_pre + b_ref[...]
            carry_ref[...] = a_pre[CH - 8:CH, :]
            gg_ref[rows, :] = (a * _sigmoid(a) * val).astype(BF16)
            return carry

        lax.fori_loop(0, TR // CH, chunk, 0)

    return pl.pallas_call(
        body, name="ffn_fwd", grid=(nt,),
        in_specs=[pl.BlockSpec((TR, 2 * DFF), lambda i: (i, 0)),
                  pl.BlockSpec((3, DFF), lambda i: (0, 0)),
                  pl.BlockSpec((1, DFF), lambda i: (0, 0))],
        out_specs=pl.BlockSpec((TR, DFF), lambda i: (i, 0)),
        out_shape=jax.ShapeDtypeStruct((T, DFF), BF16),
        scratch_shapes=[pltpu.VMEM((8, DFF), F32)],
        compiler_params=_cparams(("arbitrary",)),
    )(up, fcw, fcb)


def _ffn_bwd(up, dgg, fcw, fcb):
    T = up.shape[0]
    nt = T // TR
    tb = TR // 8
    ncht = TR // CH

    def body(up_ref, pa_ref, dgg_ref, w_ref, b_ref, dup_ref, dfw_ref, dfb_ref, carry_ref, acc_w, acc_b):
        s = pl.program_id(0)
        tile = nt - 1 - s

        @pl.when(s == 0)
        def _():
            carry_ref[...] = jnp.zeros_like(carry_ref)
            acc_w[...] = jnp.zeros_like(acc_w)
            acc_b[...] = jnp.zeros_like(acc_b)

        w0, w1, w2 = w_ref[0:1, :], w_ref[1:2, :], w_ref[2:3, :]
        a_before_tile = jnp.where(tile > 0, pa_ref[...], 0.0)

        def chunk(cc, carry):
            c = ncht - 1 - cc
            r0 = pl.multiple_of(c * CH, CH)
            rows = pl.ds(r0, CH)
            a_pre = up_ref[rows, 0:DFF]
            val = up_ref[rows, DFF:2 * DFF]
            rprev = pl.multiple_of(jnp.maximum(r0 - 8, 0), 8)
            prev8 = jnp.where(c > 0, up_ref[pl.ds(rprev, 8), 0:DFF], a_before_tile)
            a_m1 = _shift_down(a_pre, prev8, 1)
            a_m2 = _shift_down(a_pre, prev8, 2)
            a = w0 * a_m2 + w1 * a_m1 + w2 * a_pre + b_ref[...]
            sig = _sigmoid(a)
            dgg_v = dgg_ref[rows, :]
            da = dgg_v * val * (sig * (1.0 + a * (1.0 - sig)))
            dup_ref[rows, DFF:2 * DFF] = (dgg_v * (a * sig)).astype(BF16)
            next8 = carry_ref[...]
            dup_ref[rows, 0:DFF] = (w2 * da + w1 * _shift_up(da, next8, 1) + w0 * _shift_up(da, next8, 2)).astype(BF16)
            carry_ref[...] = da[0:8, :]
            acc_w[0:1, :] += jnp.sum(da * a_m2, axis=0, keepdims=True)
            acc_w[1:2, :] += jnp.sum(da * a_m1, axis=0, keepdims=True)
            acc_w[2:3, :] += jnp.sum(da * a_pre, axis=0, keepdims=True)
            acc_b[...] += jnp.sum(da, axis=0, keepdims=True)
            return carry

        lax.fori_loop(0, ncht, chunk, 0)

        @pl.when(s == nt - 1)
        def _():
            dfw_ref[...] = acc_w[0:3, :]
            dfb_ref[...] = acc_b[...]

    rev = lambda s: (nt - 1 - s, 0)
    const = lambda s: (0, 0)
    return pl.pallas_call(
        body, name="ffn_bwd", grid=(nt,),
        in_specs=[pl.BlockSpec((TR, 2 * DFF), rev),
                  pl.BlockSpec((8, DFF), lambda s: (jnp.maximum((nt - 1 - s) * tb - 1, 0), 0)),
                  pl.BlockSpec((TR, DFF), rev),
                  pl.BlockSpec((3, DFF), const),
                  pl.BlockSpec((1, DFF), const)],
        out_specs=[pl.BlockSpec((TR, 2 * DFF), rev),
                   pl.BlockSpec((3, DFF), const),
                   pl.BlockSpec((1, DFF), const)],
        out_shape=[jax.ShapeDtypeStruct((T, 2 * DFF), BF16), jax.ShapeDtypeStruct((3, DFF), F32),
                   jax.ShapeDtypeStruct((1, DFF), F32)],
        scratch_shapes=[pltpu.VMEM((8, DFF), F32), pltpu.VMEM((8, DFF), F32), pltpu.VMEM((1, DFF), F32)],
        compiler_params=_cparams(("arbitrary",)),
    )(up, up, dgg, fcw, fcb)


def _loss_head(h2, tgt, wfin):
    T = h2.shape[0]
    nt = T // TR

    def body(h_ref, t_ref, w_ref, dh_ref, loss_ref, dw_ref):
        i = pl.program_id(0)

        @pl.when(i == 0)
        def _():
            loss_ref[...] = jnp.zeros_like(loss_ref)
            dw_ref[...] = jnp.zeros_like(dw_ref)

        x = h_ref[...]
        r = lax.rsqrt(jnp.mean(x * x, axis=-1, keepdims=True) + EPS)
        n = x * r
        w = w_ref[...]
        diff = jnp.where(i > 0, n * w - t_ref[...], 0.0)
        loss_ref[...] += jnp.sum(diff * diff, axis=0, keepdims=True) * (0.5 / D)
        dy = diff * (1.0 / D)
        dw_ref[...] += jnp.sum(dy * n, axis=0, keepdims=True)
        dn = dy * w
        dh_ref[...] = r * (dn - n * jnp.mean(dn * n, axis=-1, keepdims=True))

    row = pl.BlockSpec((TR, D), lambda i: (i, 0))
    vec = pl.BlockSpec((1, D), lambda i: (0, 0))
    return pl.pallas_call(
        body, name="loss_head", grid=(nt,),
        in_specs=[row, pl.BlockSpec((TR, D), lambda i: (jnp.maximum(i - 1, 0), 0)), vec],
        out_specs=[row, vec, vec],
        out_shape=[jax.ShapeDtypeStruct((T, D), F32), jax.ShapeDtypeStruct((1, D), F32),
                   jax.ShapeDtypeStruct((1, D), F32)],
        compiler_params=_cparams(("arbitrary",)),
    )(h2, tgt, wfin)


def _ffn_block_fwd(h1, tgt, wffn, w_up_g, fcw, fcb, w_down, wfin):
    T = h1.shape[0]
    nt = T // TR
    nsl = w_up_g.shape[0]
    wsl = w_up_g.shape[2]

    def body(h_ref, t_ref, wn_ref, wup_ref, cw_ref, cb_ref, wdn_ref, wf_ref,
             u2t_ref, upb_ref, ggt_ref, dh_ref, loss_ref, dwf_ref, up_scr, gg_ref, carry_ref):
        i = pl.program_id(0)

        @pl.when(i == 0)
        def _():
            carry_ref[...] = jnp.zeros_like(carry_ref)
            loss_ref[...] = jnp.zeros_like(loss_ref)
            dwf_ref[...] = jnp.zeros_like(dwf_ref)

        x = h_ref[...]
        r2 = lax.rsqrt(jnp.mean(x * x, axis=-1, keepdims=True) + EPS)
        u2_f = x * r2 * wn_ref[...]
        u2 = u2_f.astype(BF16)
        u2t_ref[...] = u2_f.T.astype(BF16)
        for s in range(nsl):
            up_s = jnp.dot(u2, wup_ref[s], preferred_element_type=F32)
            up_scr[:, s * wsl:(s + 1) * wsl] = up_s
            upb_ref[:, s * wsl:(s + 1) * wsl] = up_s.astype(BF16)
        w0, w1, w2 = cw_ref[0:1, :], cw_ref[1:2, :], cw_ref[2:3, :]

        def chunk(c, carry):
            rows = pl.ds(pl.multiple_of(c * CH, CH), CH)
            a_pre = up_scr[rows, 0:DFF]
            val = up_scr[rows, DFF:2 * DFF]
            prev8 = carry_ref[...]
            a = w0 * _shift_down(a_pre, prev8, 2) + w1 * _shift_down(a_pre, prev8, 1) + w2 * a_pre + cb_ref[...]
            carry_ref[...] = a_pre[CH - 8:CH, :]
            gg_ref[rows, :] = a * _sigmoid(a) * val
            return carry

        lax.fori_loop(0, TR // CH, chunk, 0)
        gg_v = gg_ref[...]
        ggt_ref[...] = gg_v.T.astype(BF16)
        h2 = x + jnp.dot(gg_v.astype(BF16), wdn_ref[...], preferred_element_type=F32)
        r3 = lax.rsqrt(jnp.mean(h2 * h2, axis=-1, keepdims=True) + EPS)
        n3 = h2 * r3
        wf = wf_ref[...]
        diff = jnp.where(i > 0, n3 * wf - t_ref[...], 0.0)
        loss_ref[...] += jnp.sum(diff * diff, axis=0, keepdims=True) * (0.5 / D)
        dy = diff * (1.0 / D)
        dwf_ref[...] += jnp.sum(dy * n3, axis=0, keepdims=True)
        dn = dy * wf
        dh_ref[...] = r3 * (dn - n3 * jnp.mean(dn * n3, axis=-1, keepdims=True))

    row = lambda w: pl.BlockSpec((TR, w), lambda i: (i, 0))
    col = lambda w: pl.BlockSpec((w, TR), lambda i: (0, i))
    vec = lambda w, r=1: pl.BlockSpec((r, w), lambda i: (0, 0))
    vm = pl.BlockSpec(memory_space=pltpu.VMEM)
    return pl.pallas_call(
        body, name="ffn_block_fwd", grid=(nt,),
        in_specs=[row(D), pl.BlockSpec((TR, D), lambda i: (jnp.maximum(i - 1, 0), 0)), vec(D), vm,
                  vec(DFF, 3), vec(DFF), vm, vec(D)],
        out_specs=[col(D), row(2 * DFF), col(DFF), row(D), vec(D), vec(D)],
        out_shape=[jax.ShapeDtypeStruct((D, T), BF16), jax.ShapeDtypeStruct((T, 2 * DFF), BF16),
                   jax.ShapeDtypeStruct((DFF, T), BF16), jax.ShapeDtypeStruct((T, D), F32),
                   jax.ShapeDtypeStruct((1, D), F32), jax.ShapeDtypeStruct((1, D), F32)],
        scratch_shapes=[pltpu.VMEM((TR, 2 * DFF), F32), pltpu.VMEM((TR, DFF), F32), pltpu.VMEM((8, DFF), F32)],
        compiler_params=_cparams(("arbitrary",)),
    )(h1, tgt, wffn, w_up_g, fcw, fcb, w_down, wfin)


def _ffn_block_bwd_pipelined(dh2, upb, h1, wffn, w_up_g, fcw, fcb, w_down):
    T = h1.shape[0]
    nt = T // TR
    ncht = TR // CH
    nsl = w_up_g.shape[0]
    wsl = w_up_g.shape[2]
    tb16 = TR // 16
    assert nsl == ncht
    step = -(-DFF // (ncht * 128)) * 128
    parts = [(c0, min(c0 + step, DFF)) for c0 in range(0, DFF, step)]
    assert len(parts) == ncht

    def body(dh2n_ref, dh2p_ref, up_ref, pa_ref, h_ref, wn_ref, wup_ref, cw_ref, cb_ref, wdn_ref,
             dup_ref, dh1_ref, dfw_ref, dfb_ref, dwn_ref,
             dgg_ring, dup_ring, carry_ref, acc_w, acc_b, acc_n):
        s = pl.program_id(0)
        slot = lax.rem(s, 2)
        other = 1 - slot

        @pl.when(s == 0)
        def _():
            carry_ref[...] = jnp.zeros_like(carry_ref)
            acc_w[...] = jnp.zeros_like(acc_w)
            acc_b[...] = jnp.zeros_like(acc_b)
            acc_n[...] = jnp.zeros_like(acc_n)
            dup_ring[1] = jnp.zeros((TR, 2 * DFF), BF16)
            dgg_ring[0] = _dot_nt(dh2p_ref[...], wdn_ref[...])

        def norm_bwd(du2, valid):
            x = h_ref[...]
            dh2p = dh2p_ref[...]
            r2 = lax.rsqrt(jnp.mean(x * x, axis=-1, keepdims=True) + EPS)
            n2 = x * r2
            dn = du2 * wn_ref[...]
            dh1_ref[...] = dh2p + r2 * (dn - n2 * jnp.mean(dn * n2, axis=-1, keepdims=True))
            acc_n[...] += jnp.where(valid, jnp.sum(du2 * n2, axis=0, keepdims=True), 0.0)

        @pl.when(s < nt)
        def _():
            tile = nt - 1 - s
            w0, w1, w2 = cw_ref[0:1, :], cw_ref[1:2, :], cw_ref[2:3, :]
            a_before_tile = jnp.where(tile > 0, pa_ref[...].astype(F32)[8:16, :], 0.0)
            dh2n = dh2n_ref[...].astype(BF16)
            du2 = None
            for idx in range(ncht):
                c = ncht - 1 - idx
                r0 = c * CH
                rows = slice(r0, r0 + CH)
                a_pre = up_ref[rows, 0:DFF].astype(F32)
                val = up_ref[rows, DFF:2 * DFF].astype(F32)
                prev8 = up_ref[r0 - 16:r0, 0:DFF].astype(F32)[8:16, :] if c > 0 else a_before_tile
                a_m1 = _shift_down(a_pre, prev8, 1)
                a_m2 = _shift_down(a_pre, prev8, 2)
                a = w0 * a_m2 + w1 * a_m1 + w2 * a_pre + cb_ref[...]
                sig = _sigmoid(a)
                dgg_v = dgg_ring[slot, rows, :]
                da = dgg_v * val * (sig * (1.0 + a * (1.0 - sig)))
                dval = (dgg_v * (a * sig)).astype(BF16)
                next8 = carry_ref[...]
                da_pre = (w2 * da + w1 * _shift_up(da, next8, 1) + w0 * _shift_up(da, next8, 2)).astype(BF16)
                carry_ref[...] = da[0:8, :]
                dup_ref[rows, 0:DFF] = da_pre
                dup_ref[rows, DFF:2 * DFF] = dval
                dup_ring[slot, rows, 0:DFF] = da_pre
                dup_ring[slot, rows, DFF:2 * DFF] = dval
                acc_w[0:1, :] += jnp.sum(da * a_m2, axis=0, keepdims=True)
                acc_w[1:2, :] += jnp.sum(da * a_m1, axis=0, keepdims=True)
                acc_w[2:3, :] += jnp.sum(da * a_pre, axis=0, keepdims=True)
                acc_b[...] += jnp.sum(da, axis=0, keepdims=True)
                c0, c1 = parts[idx]
                dgg_ring[other, :, c0:c1] = _dot_nt(dh2n, wdn_ref[c0:c1, :])
                part = _dot_nt(dup_ring[other, :, idx * wsl:(idx + 1) * wsl], wup_ref[idx])
                du2 = part if du2 is None else du2 + part
            norm_bwd(du2, s > 0)

        @pl.when(s == nt)
        def _():
            du2 = _dot_nt(dup_ring[other, :, 0:wsl], wup_ref[0])
            for sl in range(1, nsl):
                du2 = du2 + _dot_nt(dup_ring[other, :, sl * wsl:(sl + 1) * wsl], wup_ref[sl])
            norm_bwd(du2, True)
            dfw_ref[...] = acc_w[0:3, :]
            dfb_ref[...] = acc_b[...]
            dwn_ref[...] = acc_n[...]

    gate_tile = lambda s: jnp.maximum(nt - 1 - s, 0)
    next_tile = lambda s: jnp.maximum(nt - 2 - s, 0)
    prev_tile = lambda s: jnp.minimum(nt - s, nt - 1)
    vec = lambda w, r=1: pl.BlockSpec((r, w), lambda s: (0, 0))
    vm = pl.BlockSpec(memory_space=pltpu.VMEM)
    return pl.pallas_call(
        body, name="ffn_block_bwd", grid=(nt + 1,),
        in_specs=[pl.BlockSpec((TR, D), lambda s: (next_tile(s), 0)),
                  pl.BlockSpec((TR, D), lambda s: (prev_tile(s), 0)),
                  pl.BlockSpec((TR, 2 * DFF), lambda s: (gate_tile(s), 0)),
                  pl.BlockSpec((16, DFF), lambda s: (jnp.maximum(gate_tile(s) * tb16 - 1, 0), 0)),
                  pl.BlockSpec((TR, D), lambda s: (prev_tile(s), 0)),
                  vec(D), vm, vec(DFF, 3), vec(DFF), vm],
        out_specs=[pl.BlockSpec((TR, 2 * DFF), lambda s: (gate_tile(s), 0)),
                   pl.BlockSpec((TR, D), lambda s: (prev_tile(s), 0)),
                   vec(DFF, 3), vec(DFF), vec(D)],
        out_shape=[jax.ShapeDtypeStruct((T, 2 * DFF), BF16), jax.ShapeDtypeStruct((T, D), F32),
                   jax.ShapeDtypeStruct((3, DFF), F32), jax.ShapeDtypeStruct((1, DFF), F32),
                   jax.ShapeDtypeStruct((1, D), F32)],
        scratch_shapes=[pltpu.VMEM((2, TR, DFF), F32), pltpu.VMEM((2, TR, 2 * DFF), BF16),
                        pltpu.VMEM((8, DFF), F32), pltpu.VMEM((8, DFF), F32), pltpu.VMEM((1, DFF), F32),
                        pltpu.VMEM((1, D), F32)],
        compiler_params=_cparams(("arbitrary",)),
    )(dh2, dh2, upb, upb, h1, wffn, w_up_g, fcw, fcb, w_down)


def _ffn_block_bwd(dh2, upb, h1, wffn, w_up_g, fcw, fcb, w_down):
    T = h1.shape[0]
    nt = T // TR
    ncht = TR // CH
    nsl = w_up_g.shape[0]
    wsl = w_up_g.shape[2]
    tb16 = TR // 16

    def body(dh2_ref, up_ref, pa_ref, h_ref, wn_ref, wup_ref, cw_ref, cb_ref, wdn_ref,
             dup_ref, dh1_ref, dfw_ref, dfb_ref, dwn_ref, dgg_scr, carry_ref, acc_w, acc_b, acc_n):
        s = pl.program_id(0)
        tile = nt - 1 - s

        @pl.when(s == 0)
        def _():
            carry_ref[...] = jnp.zeros_like(carry_ref)
            acc_w[...] = jnp.zeros_like(acc_w)
            acc_b[...] = jnp.zeros_like(acc_b)
            acc_n[...] = jnp.zeros_like(acc_n)

        dh2 = dh2_ref[...]
        dgg_scr[...] = _dot_nt(dh2, wdn_ref[...])
        w0, w1, w2 = cw_ref[0:1, :], cw_ref[1:2, :], cw_ref[2:3, :]
        a_before_tile = jnp.where(tile > 0, pa_ref[...].astype(F32)[8:16, :], 0.0)

        def chunk(cc, carry):
            c = ncht - 1 - cc
            r0 = pl.multiple_of(c * CH, CH)
            rows = pl.ds(r0, CH)
            a_pre = up_ref[rows, 0:DFF].astype(F32)
            val = up_ref[rows, DFF:2 * DFF].astype(F32)
            rprev = pl.multiple_of(jnp.maximum(r0 - 16, 0), 16)
            prev_in = up_ref[pl.ds(rprev, 16), 0:DFF].astype(F32)[8:16, :]
            prev8 = jnp.where(c > 0, prev_in, a_before_tile)
            a_m1 = _shift_down(a_pre, prev8, 1)
            a_m2 = _shift_down(a_pre, prev8, 2)
            a = w0 * a_m2 + w1 * a_m1 + w2 * a_pre + cb_ref[...]
            sig = _sigmoid(a)
            dgg_v = dgg_scr[rows, :]
            da = dgg_v * val * (sig * (1.0 + a * (1.0 - sig)))
            dup_ref[rows, DFF:2 * DFF] = (dgg_v * (a * sig)).astype(BF16)
            next8 = carry_ref[...]
            dup_ref[rows, 0:DFF] = (w2 * da + w1 * _shift_up(da, next8, 1) + w0 * _shift_up(da, next8, 2)).astype(BF16)
            carry_ref[...] = da[0:8, :]
            acc_w[0:1, :] += jnp.sum(da * a_m2, axis=0, keepdims=True)
            acc_w[1:2, :] += jnp.sum(da * a_m1, axis=0, keepdims=True)
            acc_w[2:3, :] += jnp.sum(da * a_pre, axis=0, keepdims=True)
            acc_b[...] += jnp.sum(da, axis=0, keepdims=True)
            return carry

        lax.fori_loop(0, ncht, chunk, 0)
        du2 = _dot_nt(dup_ref[:, 0:wsl], wup_ref[0])
        for sl in range(1, nsl):
            du2 = du2 + _dot_nt(dup_ref[:, sl * wsl:(sl + 1) * wsl], wup_ref[sl])
        x = h_ref[...]
        r2 = lax.rsqrt(jnp.mean(x * x, axis=-1, keepdims=True) + EPS)
        n2 = x * r2
        dn = du2 * wn_ref[...]
        dh1_ref[...] = dh2 + r2 * (dn - n2 * jnp.mean(dn * n2, axis=-1, keepdims=True))
        acc_n[...] += jnp.sum(du2 * n2, axis=0, keepdims=True)

        @pl.when(s == nt - 1)
        def _():
            dfw_ref[...] = acc_w[0:3, :]
            dfb_ref[...] = acc_b[...]
            dwn_ref[...] = acc_n[...]

    rev = lambda w: pl.BlockSpec((TR, w), lambda s: (nt - 1 - s, 0))
    vec = lambda w, r=1: pl.BlockSpec((r, w), lambda s: (0, 0))
    vm = pl.BlockSpec(memory_space=pltpu.VMEM)
    return pl.pallas_call(
        body, name="ffn_block_bwd", grid=(nt,),
        in_specs=[rev(D), rev(2 * DFF),
                  pl.BlockSpec((16, DFF), lambda s: (jnp.maximum((nt - 1 - s) * tb16 - 1, 0), 0)),
                  rev(D), vec(D), vm, vec(DFF, 3), vec(DFF), vm],
        out_specs=[rev(2 * DFF), rev(D), vec(DFF, 3), vec(DFF), vec(D)],
        out_shape=[jax.ShapeDtypeStruct((T, 2 * DFF), BF16), jax.ShapeDtypeStruct((T, D), F32),
                   jax.ShapeDtypeStruct((3, DFF), F32), jax.ShapeDtypeStruct((1, DFF), F32),
                   jax.ShapeDtypeStruct((1, D), F32)],
        scratch_shapes=[pltpu.VMEM((TR, DFF), F32), pltpu.VMEM((8, DFF), F32), pltpu.VMEM((8, DFF), F32),
                        pltpu.VMEM((1, DFF), F32), pltpu.VMEM((1, D), F32)],
        compiler_params=_cparams(("arbitrary",)),
    )(dh2, upb, upb, h1, wffn, w_up_g, fcw, fcb, w_down)


def _place():
    x, y, c = lax.axis_index("x"), lax.axis_index("y"), lax.axis_index("c")
    return x, y, c


_CHIP_FLIPS = ((1, 0), (0, 1), (1, 1))


def _flip(v, bit):
    return 1 - v if bit else v


def _into_slot(w, j_idx, *, rb, dtype, name):
    r, cdim = w.shape

    def body(j_ref, w_ref, out_ref):
        del j_ref
        out_ref[...] = w_ref[...].astype(dtype)

    grid_spec = pltpu.PrefetchScalarGridSpec(
        num_scalar_prefetch=1, grid=(r // rb,),
        in_specs=[pl.BlockSpec((rb, cdim), lambda i, j_ref: (i, 0))],
        out_specs=pl.BlockSpec((None, rb, cdim), lambda i, j_ref: (j_ref[0], i, 0)))
    return pl.pallas_call(
        body, name=name, grid_spec=grid_spec, out_shape=jax.ShapeDtypeStruct((NSHARD, r, cdim), dtype),
        compiler_params=_cparams(("parallel",)),
    )(j_idx, w)


class _Gather:
    def __init__(self, outs, send_sems, recv_sems, whole):
        self.outs, self.send_sems, self.recv_sems, self.whole = outs, send_sems, recv_sems, whole
        self.x, self.y, self.c = _place()
        self.j = 2 * self.x + self.y
        self.sibling = (self.x, self.y, 1 - self.c)

    def _rows(self, w, core):
        r = self.outs[w].shape[1]
        return pl.ds(0, r) if self.whole[w] else pl.ds(core * (r // 2), r // 2)

    def _copy(self, w, slot, core, sem, to):
        piece = self.outs[w].at[slot, self._rows(w, core), :]
        return pltpu.make_async_remote_copy(
            src_ref=piece, dst_ref=piece, send_sem=self.send_sems.at[w, sem], recv_sem=self.recv_sems.at[w, sem],
            device_id=to, device_id_type=MESH)

    def _chips(self):
        for kk, (fx, fy) in enumerate(_CHIP_FLIPS):
            px, py = _flip(self.x, fx), _flip(self.y, fy)
            yield kk, 2 * px + py, (px, py, self.c)

    def send(self):
        for w in range(len(self.outs)):
            for kk, _, to in self._chips():
                self._copy(w, self.j, self.c, kk, to).start()

    def relay(self):
        for w in range(len(self.outs)):
            for kk, jk, _ in self._chips():
                self._copy(w, jk, self.c, kk, self.sibling).wait_recv()
                if not self.whole[w]:
                    self._copy(w, jk, self.c, 3 + kk, self.sibling).start()

    def finish(self):
        for w in range(len(self.outs)):
            for kk, jk, to in self._chips():
                self._copy(w, self.j, self.c, kk, to).wait_send()
                if not self.whole[w]:
                    self._copy(w, jk, 1 - self.c, 3 + kk, self.sibling).wait_recv()
                    self._copy(w, jk, self.c, 3 + kk, self.sibling).wait_send()


def _allgather_weights(slotted, whole):
    n = len(slotted)

    def body(*refs):
        g = _Gather(refs[n:2 * n], refs[2 * n], refs[2 * n + 1], whole)
        g.send()
        g.relay()
        g.finish()

    any_spec = pl.BlockSpec(memory_space=pl.ANY)
    return pl.pallas_call(
        body, name="allgather_weights",
        in_specs=[any_spec] * n, out_specs=[any_spec] * n,
        out_shape=[jax.ShapeDtypeStruct(a.shape, a.dtype) for a in slotted],
        input_output_aliases={i: i for i in range(n)},
        scratch_shapes=[pltpu.SemaphoreType.DMA((n, 6)), pltpu.SemaphoreType.DMA((n, 6))],
    )(*slotted)


class _ChipExchange:
    def __init__(self, ins, outs, send_sems, recv_sems):
        self.ins, self.outs, self.send_sems, self.recv_sems = ins, outs, send_sems, recv_sems
        self.x, self.y, self.c = _place()

    def _copies(self):
        for w in range(len(self.ins)):
            for kk, (fx, fy) in enumerate(_CHIP_FLIPS):
                px, py = _flip(self.x, fx), _flip(self.y, fy)
                yield pltpu.make_async_remote_copy(
                    src_ref=self.ins[w].at[2 * px + py], dst_ref=self.outs[w].at[kk],
                    send_sem=self.send_sems.at[w, kk], recv_sem=self.recv_sems.at[w, kk],
                    device_id=(px, py, self.c), device_id_type=MESH)

    def send(self):
        for cp in self._copies():
            cp.start()

    def finish(self):
        for cp in self._copies():
            cp.wait()


def _pair_exchange(grads, name):
    nw = len(grads)

    def body(*refs):
        ins, outs = refs[:nw], refs[nw:2 * nw]
        send_sems, recv_sems = refs[2 * nw:]
        x, y, c = _place()
        sibling = (x, y, 1 - c)
        cps = []
        for w in range(nw):
            half = ins[w].shape[1] // 2
            cp = pltpu.make_async_remote_copy(
                src_ref=ins[w].at[:, pl.ds((1 - c) * half, half), :], dst_ref=outs[w],
                send_sem=send_sems.at[w], recv_sem=recv_sems.at[w], device_id=sibling, device_id_type=MESH)
            cp.start()
            cps.append(cp)
        for cp in cps:
            cp.wait()

    any_spec = pl.BlockSpec(memory_space=pl.ANY)
    return pl.pallas_call(
        body, name=name,
        in_specs=[any_spec] * nw, out_specs=[any_spec] * nw,
        out_shape=[jax.ShapeDtypeStruct((g.shape[0], g.shape[1] // 2, g.shape[2]), g.dtype) for g in grads],
        scratch_shapes=[pltpu.SemaphoreType.DMA((nw,)), pltpu.SemaphoreType.DMA((nw,))],
    )(*grads)


def _pair_add(g, other, c_idx, *, rb, name):
    S, r, cdim = g.shape
    half = r // 2
    nb = half // rb

    def body(c_ref, g_ref, o_ref, out_ref):
        del c_ref
        out_ref[...] = (g_ref[...] + o_ref[...]).astype(BF16)

    grid_spec = pltpu.PrefetchScalarGridSpec(
        num_scalar_prefetch=1, grid=(S, nb),
        in_specs=[pl.BlockSpec((None, rb, cdim), lambda s, i, c_ref: (s, c_ref[0] * nb + i, 0)),
                  pl.BlockSpec((None, rb, cdim), lambda s, i, c_ref: (s, i, 0))],
        out_specs=pl.BlockSpec((None, rb, cdim), lambda s, i, c_ref: (s, i, 0)))
    return pl.pallas_call(
        body, name=name, grid_spec=grid_spec, out_shape=jax.ShapeDtypeStruct((S, half, cdim), BF16),
        compiler_params=_cparams(("parallel", "parallel")),
    )(c_idx, g, other)


def _chip_exchange(psums):
    nw = len(psums)

    def body(*refs):
        ex = _ChipExchange(refs[:nw], refs[nw:2 * nw], refs[2 * nw], refs[2 * nw + 1])
        ex.send()
        ex.finish()

    any_spec = pl.BlockSpec(memory_space=pl.ANY)
    return pl.pallas_call(
        body, name="grad_chip_exchange",
        in_specs=[any_spec] * nw, out_specs=[any_spec] * nw,
        out_shape=[jax.ShapeDtypeStruct((3,) + p.shape[1:], p.dtype) for p in psums],
        scratch_shapes=[pltpu.SemaphoreType.DMA((nw, 3)), pltpu.SemaphoreType.DMA((nw, 3))],
    )(*psums)


def _chip_sum(psum, parts, cj_idx, *, rb, name):
    S, half, cdim = psum.shape
    nb = half // rb

    def body(cj_ref, own_ref, p_ref, out_ref):
        del cj_ref
        f = lambda v: v.astype(F32)
        out_ref[...] = ((f(own_ref[...]) + f(p_ref[0])) + f(p_ref[1])) + f(p_ref[2])

    grid_spec = pltpu.PrefetchScalarGridSpec(
        num_scalar_prefetch=1, grid=(nb,),
        in_specs=[pl.BlockSpec((None, rb, cdim), lambda i, cj: (cj[1], i, 0)),
                  pl.BlockSpec((3, rb, cdim), lambda i, cj: (0, i, 0))],
        out_specs=pl.BlockSpec((rb, cdim), lambda i, cj: (cj[0] * nb + i, 0)))
    return pl.pallas_call(
        body, name=name, grid_spec=grid_spec, out_shape=jax.ShapeDtypeStruct((2 * half, cdim), F32),
        compiler_params=_cparams(("parallel",)),
    )(cj_idx, psum, parts)


SLAB_W = 1024


def _final_exchange(pieces, grads):
    n, nw = len(pieces), len(grads)
    segs, at = [], 0
    for idx, p in enumerate(pieces):
        r, wd = p.shape
        for c0 in range(0, wd, SLAB_W):
            if r > 1:
                at = -(-at // 8) * 8
            segs.append((idx, c0, min(SLAB_W, wd - c0), at))
            at += r
    rows = -(-at // 8) * 8
    flips = [(fx, fy, fc) for fx in (0, 1) for fy in (0, 1) for fc in (0, 1)][1:]

    def body(*refs):
        ins = refs[:n]
        outs = refs[n + nw:2 * n + nw]
        g_refs = refs[2 * n + nw:2 * n + 2 * nw]
        mine_ref, slots_ref, send_sems, recv_sems, gsend_sems, grecv_sems = refs[2 * n + 2 * nw:]
        x, y, c = _place()
        me = 4 * x + 2 * y + c
        sibling = (x, y, 1 - c)

        def swap(w, core):
            half = g_refs[w].shape[0] // 2
            rows_ = g_refs[w].at[pl.ds(core * half, half), :]
            return pltpu.make_async_remote_copy(
                src_ref=rows_, dst_ref=rows_, send_sem=gsend_sems.at[w], recv_sem=grecv_sems.at[w],
                device_id=sibling, device_id_type=MESH)

        for w in range(nw):
            swap(w, c).start()
        mine_ref[...] = jnp.zeros_like(mine_ref)
        for idx, c0, wd, st in segs:
            r = ins[idx].shape[0]
            mine_ref[st:st + r, 0:wd] = ins[idx][:, c0:c0 + wd]
        slots_ref[me] = mine_ref[...]
        cps = []
        for kk, (fx, fy, fc) in enumerate(flips):
            cp = pltpu.make_async_remote_copy(
                src_ref=mine_ref, dst_ref=slots_ref.at[me], send_sem=send_sems.at[kk], recv_sem=recv_sems.at[kk],
                device_id=(_flip(x, fx), _flip(y, fy), _flip(c, fc)), device_id_type=MESH)
            cp.start()
            cps.append(cp)
        for cp in cps:
            cp.wait()
        tot = slots_ref[0]
        for d in range(1, 8):
            tot = tot + slots_ref[d]
        mine_ref[...] = tot
        for idx, c0, wd, st in segs:
            r = ins[idx].shape[0]
            val = mine_ref[st:st + r, 0:wd]
            if idx == n - 1:
                outs[idx][...] = jnp.sum(val, keepdims=True)
            else:
                outs[idx][:, c0:c0 + wd] = val
        for w in range(nw):
            swap(w, 1 - c).wait_recv()
            swap(w, c).wait_send()

    vm = pl.BlockSpec(memory_space=pltpu.VMEM)
    hbm = pl.BlockSpec(memory_space=pl.ANY)
    out_shape = ([jax.ShapeDtypeStruct(p.shape, F32) for p in pieces[:-1]] + [jax.ShapeDtypeStruct((1, 1), F32)]
                 + [jax.ShapeDtypeStruct(g.shape, g.dtype) for g in grads])
    res = pl.pallas_call(
        body, name="final_exchange", in_specs=[vm] * n + [hbm] * nw, out_specs=[vm] * n + [hbm] * nw,
        out_shape=out_shape, input_output_aliases={n + w: n + w for w in range(nw)},
        scratch_shapes=[pltpu.VMEM((rows, SLAB_W), F32), pltpu.VMEM((8, rows, SLAB_W), F32),
                        pltpu.SemaphoreType.DMA((7,)), pltpu.SemaphoreType.DMA((7,)),
                        pltpu.SemaphoreType.DMA((nw,)), pltpu.SemaphoreType.DMA((nw,))],
    )(*pieces, *grads)
    return res[:n], res[n:]


def _adamw(w, g, m, v, *, rb, name):
    r, cdim = w.shape

    def body(w_ref, g_ref, m_ref, v_ref, go_ref, d_ref, nm_ref, nv_ref):
        go_ref[...] = g_ref[...]
        d_ref[...], nm_ref[...], nv_ref[...] = _adamw_update(w_ref[...], g_ref[...], m_ref[...], v_ref[...])

    spec = pl.BlockSpec((rb, cdim), lambda i: (i, 0))
    shp = jax.ShapeDtypeStruct((r, cdim), F32)
    return pl.pallas_call(
        body, name=name, grid=(r // rb,), in_specs=[spec] * 4, out_specs=[spec] * 4, out_shape=[shp] * 4,
        compiler_params=_cparams(("parallel",)),
    )(w, g, m, v)


def _adamw_update(w, g, m, v):
    nm = ADAM_B1 * m + (1.0 - ADAM_B1) * g
    nv = ADAM_B2 * v + (1.0 - ADAM_B2) * (g * g)
    m_hat = nm / (1.0 - ADAM_B1 ** ADAM_STEP)
    v_hat = nv / (1.0 - ADAM_B2 ** ADAM_STEP)
    return -ADAM_LR * (m_hat / (jnp.sqrt(v_hat) + ADAM_EPS) + ADAM_WD * w), nm, nv


def _adamw_small(params):
    n = len(params)

    def body(*refs):
        ins, outs = refs[:4 * n], refs[4 * n:]
        for p in range(n):
            w_ref, g_ref, m_ref, v_ref = ins[4 * p:4 * p + 4]
            d, nm, nv = _adamw_update(w_ref[...], g_ref[...], m_ref[...], v_ref[...])
            outs[3 * p][...] = d
            outs[3 * p + 1][...] = nm
            outs[3 * p + 2][...] = nv

    vm = pl.BlockSpec(memory_space=pltpu.VMEM)
    flat = [a for p in params for a in p]
    out_shape = [jax.ShapeDtypeStruct(p[0].shape, F32) for p in params for _ in range(3)]
    res = pl.pallas_call(
        body, name="adamw_small", in_specs=[vm] * (4 * n), out_specs=[vm] * (3 * n), out_shape=out_shape,
    )(*flat)
    return [tuple(res[3 * p:3 * p + 3]) for p in range(n)]


_PAIR_ADD_ROWS = {"w_in": 256, "w_out": 128, "w_up": 256, "w_down": 176}


def _pair_sums(grads, names, c_idx, tag):
    others = _pair_exchange(grads, name=f"grad_pair_exchange_{tag}")
    return [_pair_add(g, o_, c_idx, rb=_PAIR_ADD_ROWS[n], name=f"pair_add_{n}") for g, o_, n in zip(grads, others, names)]


def _local_step(x, tgt, meta_full, lb_param, attn_norm_w, w_in_g, hgrn_norm_w, conv_w_full, w_out_full,
                ffn_norm_w, late_slotted, fcw_full, ffn_conv_b, final_norm_w, c_idx, cj_idx):
    seq = x.shape[0]
    T = TR + seq
    head_tile = jnp.concatenate([jnp.zeros((PAD, D), F32), meta_full], axis=0)
    whn_t = jnp.tile(hgrn_norm_w, (1, NH))

    ut, projb, o, sst, mt, h1, w_up_g, w_down_g = _mix_block_fwd(
        x, head_tile, lb_param, attn_norm_w, whn_t, conv_w_full, w_in_g, w_out_full, late_slotted)
    w_down_full = w_down_g.reshape(DFF, D)
    u2t, upb, ggt, dh2, loss_vec, dwfin = _ffn_block_fwd(
        h1, tgt, ffn_norm_w, w_up_g, fcw_full, ffn_conv_b, w_down_full, final_norm_w.reshape(1, D))

    dup, dh1, dfw, dfb, dwffn = _ffn_block_bwd_pipelined(
        dh2, upb, h1, ffn_norm_w, w_up_g, fcw_full, ffn_conv_b, w_down_full)
    kb = 1408 if T % 1408 == 0 else TR
    g_up = _weight_grad(u2t, dup, bn=DFF, bk=kb, name="dw_up_mm", shard_cols=2 * DFF // NSHARD)
    g_down = _weight_grad(ggt, dh2, bn=D, bk=kb, name="dw_down_mm")
    ps_ffn = _pair_sums([g_up, g_down.reshape(NSHARD, DFF // NSHARD, D)], ("w_up", "w_down"), c_idx, "ffn")

    dproj, dlb, dwhn, dcw, *parts_ffn = _mix_block_bwd(
        dh1, projb, o, sst, lb_param, whn_t, conv_w_full, w_out_full, ps_ffn)
    g_in = _weight_grad(ut, dproj, bn=9 * D // NSHARD, bk=kb, name="dw_in_mm", shard_cols=9 * D // NSHARD)
    g_out = _weight_grad(mt, dh1, bn=D, bk=kb, name="dw_out_mm")
    ps_mix = _pair_sums([g_in, g_out.reshape(NSHARD, D // NSHARD, D)], ("w_in", "w_out"), c_idx, "mix")

    grad_x, dmeta, dwattn, *parts_mix = _input_grad_block(dproj, x, head_tile, dh1, attn_norm_w, w_in_g, ps_mix)

    halves = [_chip_sum(ps, p, cj_idx, rb=_PAIR_ADD_ROWS[n], name=f"chip_sum_{n}")
              for ps, p, n in zip(ps_mix + ps_ffn, parts_mix + parts_ffn, ("w_in", "w_out", "w_up", "w_down"))]
    small = dict(dlb=dlb, dwattn=dwattn, dwhn=dwhn, dwffn=dwffn, dfb=dfb, dwfin=dwfin,
                 dcw=dcw, dfw=dfw, dmeta=dmeta, loss=loss_vec)
    return grad_x, small, halves


_SMALL_ORDER = ("dmeta", "dcw", "dfw", "dlb", "dwattn", "dwhn", "dwffn", "dfb", "dwfin", "loss")


def kernel(x, meta_tokens, lb_param, attn_norm_w, w_in, hgrn_norm_w, conv_w, w_out, ffn_norm_w, w_up, ffn_conv_w, ffn_conv_b, w_down, final_norm_w, loss_target, m_meta_tokens, m_lb_param, m_attn_norm_w, m_w_in, m_hgrn_norm_w, m_conv_w, m_w_out, m_ffn_norm_w, m_w_up, m_ffn_conv_w, m_ffn_conv_b, m_w_down, m_final_norm_w, v_meta_tokens, v_lb_param, v_attn_norm_w, v_w_in, v_hgrn_norm_w, v_conv_w, v_w_out, v_ffn_norm_w, v_w_up, v_ffn_conv_w, v_ffn_conv_b, v_w_down, v_final_norm_w):
    xi, yi, ci = _place()
    j = 2 * xi + yi
    c_idx = jnp.reshape(ci, (1,)).astype(jnp.int32)

    j_idx = jnp.reshape(j, (1,)).astype(jnp.int32)
    ds_, fs_ = D // NSHARD, DFF // NSHARD
    widen = lambda a: jnp.pad(a, ((0, 0), (0, 768 - a.shape[1])))
    rows_small = jnp.concatenate([widen(meta_tokens), widen(conv_w[0]), widen(ffn_conv_w[0]),
                                  jnp.zeros((2, 768), F32)], axis=0)
    slotted = [_into_slot(w[0], j_idx, rb=rb, dtype=BF16, name=f"slot_{n}")
               for w, rb, n in ((w_in, 256, "w_in"), (w_out, 128, "w_out"), (w_up, 256, "w_up"), (w_down, 176, "w_down"))]
    s_in, s_out, s_up, s_down = slotted
    s_small = _into_slot(rows_small, j_idx, rb=rows_small.shape[0], dtype=F32, name="slot_small")
    w_in_g, w_out_g, small_g = _allgather_weights([s_in, s_out, s_small], (False, False, True))
    unshard = lambda a: jnp.transpose(a, (1, 0, 2)).reshape(a.shape[1], -1)
    meta_full = unshard(small_g[:, 0:NMETA, 0:ds_])
    conv_w_full = unshard(small_g[:, NMETA:NMETA + 3, 0:ds_])
    fcw_full = unshard(small_g[:, NMETA + 3:NMETA + 6, 0:fs_])

    cj_idx = jnp.stack([ci, j]).astype(jnp.int32)
    grad_x, small, halves = _local_step(
        x[0], loss_target[0], meta_full, lb_param, attn_norm_w, w_in_g, hgrn_norm_w, conv_w_full,
        w_out_g.reshape(D, D), ffn_norm_w, [s_up, s_down], fcw_full, ffn_conv_b, final_norm_w, c_idx, cj_idx)

    names = _SMALL_ORDER
    small_sums, g_big = _final_exchange([small[n] for n in names], halves)
    vals = dict(zip(names, small_sums))
    loss = vals["loss"].reshape(())
    g_small = {
        "meta_tokens": lax.dynamic_slice_in_dim(vals["dmeta"], j * (D // NSHARD), D // NSHARD, axis=1),
        "lb_param": jnp.concatenate([vals["dlb"], -vals["dlb"]], axis=0),
        "attn_norm_w": vals["dwattn"],
        "hgrn_norm_w": vals["dwhn"],
        "conv_w": lax.dynamic_slice_in_dim(vals["dcw"], j * (D // NSHARD), D // NSHARD, axis=1)[None],
        "ffn_norm_w": vals["dwffn"],
        "ffn_conv_w": lax.dynamic_slice_in_dim(vals["dfw"], j * (DFF // NSHARD), DFF // NSHARD, axis=1)[None],
        "ffn_conv_b": vals["dfb"],
        "final_norm_w": vals["dwfin"].reshape(D),
    }


    weights = {"meta_tokens": meta_tokens, "lb_param": lb_param, "attn_norm_w": attn_norm_w, "w_in": w_in,
               "hgrn_norm_w": hgrn_norm_w, "conv_w": conv_w, "w_out": w_out, "ffn_norm_w": ffn_norm_w,
               "w_up": w_up, "ffn_conv_w": ffn_conv_w, "ffn_conv_b": ffn_conv_b, "w_down": w_down,
               "final_norm_w": final_norm_w}
    ms = {"meta_tokens": m_meta_tokens, "lb_param": m_lb_param, "attn_norm_w": m_attn_norm_w, "w_in": m_w_in,
          "hgrn_norm_w": m_hgrn_norm_w, "conv_w": m_conv_w, "w_out": m_w_out, "ffn_norm_w": m_ffn_norm_w,
          "w_up": m_w_up, "ffn_conv_w": m_ffn_conv_w, "ffn_conv_b": m_ffn_conv_b, "w_down": m_w_down,
          "final_norm_w": m_final_norm_w}
    vs = {"meta_tokens": v_meta_tokens, "lb_param": v_lb_param, "attn_norm_w": v_attn_norm_w, "w_in": v_w_in,
          "hgrn_norm_w": v_hgrn_norm_w, "conv_w": v_conv_w, "w_out": v_w_out, "ffn_norm_w": v_ffn_norm_w,
          "w_up": v_w_up, "ffn_conv_w": v_ffn_conv_w, "ffn_conv_b": v_ffn_conv_b, "w_down": v_w_down,
          "final_norm_w": v_final_norm_w}
    order = list(weights)
    grads, deltas, new_m, new_v = {}, {}, {}, {}

    for name, g, rb in zip(("w_in", "w_out", "w_up", "w_down"), g_big, (256, 128, 256, 176)):
        shp = weights[name].shape
        w2, m2, v2 = (a.reshape(shp[1], shp[2]) for a in (weights[name], ms[name], vs[name]))
        g_, d_, nm_, nv_ = _adamw(w2, g, m2, v2, rb=rb, name=f"adamw_{name}")
        grads[name], deltas[name], new_m[name], new_v[name] = (a.reshape(shp) for a in (g_, d_, nm_, nv_))

    small_names = [n for n in order if n not in grads]
    as2d = lambda a: a.reshape(-1, a.shape[-1])
    res = _adamw_small([tuple(as2d(a) for a in (weights[n], g_small[n], ms[n], vs[n])) for n in small_names])
    for n, (d_, nm_, nv_) in zip(small_names, res):
        shp = weights[n].shape
        grads[n], deltas[n], new_m[n], new_v[n] = (a.reshape(shp) for a in (g_small[n], d_, nm_, nv_))

    return (loss, grad_x[None], *[grads[n] for n in order], *[deltas[n] for n in order],
            *[new_m[n] for n in order], *[new_v[n] for n in order])
```

```python
import jax
import jax.numpy as jnp
from jax import lax
from jax.experimental import pallas as pl
from jax.experimental.pallas import tpu as pltpu

F32 = jnp.float32
BF16 = jnp.bfloat16
MESH = pl.DeviceIdType.MESH

D = 1024
NH = 8
HD = 128
DFF = 2816
NMETA = 16
EPS = 1e-6
TR = 256
PAD = TR - NMETA
CH = 64
NSHARD = 4
VMEM_LIMIT = 62 * 1024 * 1024

ADAM_LR = 0.001
ADAM_B1 = 0.9
ADAM_B2 = 0.999
ADAM_EPS = 1e-08
ADAM_WD = 0.01
ADAM_STEP = 10


def _cparams(semantics=None, **kw):
    return pltpu.CompilerParams(dimension_semantics=semantics, vmem_limit_bytes=VMEM_LIMIT, **kw)


def _sigmoid(x):
    return 0.5 * jnp.tanh(0.5 * x) + 0.5


def _weight_grad(at, b, *, bn, bk, name, shard_cols=None):
    M, T = at.shape
    N = b.shape[1]
    assert b.shape[0] == T and T % bk == 0 and N % bn == 0, (name, M, N, T, bn, bk)
    nk = T // bk
    if shard_cols is None:
        nsh = 1
        blk = (M, bn)
        out_shape = jax.ShapeDtypeStruct((M, N), BF16)
        out_spec = pl.BlockSpec(blk, lambda j, k: (0, j))
    else:
        assert bn % shard_cols == 0 and N % shard_cols == 0
        nsh = bn // shard_cols
        blk = (nsh, M, shard_cols)
        out_shape = jax.ShapeDtypeStruct((N // shard_cols, M, shard_cols), BF16)
        out_spec = pl.BlockSpec(blk, lambda j, k: (j, 0, 0))

    def body(a_ref, b_ref, o_ref, acc_ref):
        k = pl.program_id(1)

        @pl.when(k == 0)
        def _():
            acc_ref[...] = jnp.zeros_like(acc_ref)

        p = jnp.dot(a_ref[...].astype(BF16), b_ref[...].astype(BF16), preferred_element_type=F32)
        if shard_cols is None:
            acc_ref[...] += p
        else:
            for q in range(nsh):
                acc_ref[q] += p[:, q * shard_cols:(q + 1) * shard_cols]

        @pl.when(k == nk - 1)
        def _():
            o_ref[...] = acc_ref[...].astype(BF16)

    return pl.pallas_call(
        body, name=name, grid=(N // bn, nk),
        in_specs=[pl.BlockSpec((M, bk), lambda j, k: (0, k)), pl.BlockSpec((bk, bn), lambda j, k: (k, j))],
        out_specs=out_spec, out_shape=out_shape, scratch_shapes=[pltpu.VMEM(blk, F32)],
        compiler_params=_cparams(("parallel", "arbitrary")),
    )(at, b)


def _input_grad_block(dproj, x_seq, head_tile, dres, w, w_in_g, psums):
    T = TR + x_seq.shape[0]
    nt = T // TR
    nsl = w_in_g.shape[0]
    wsl = w_in_g.shape[2]
    wcol = wsl // 2
    nex = len(psums)

    def body(*refs):
        dp_ref, x_ref, head_ref, dres_ref, w_ref, win_ref = refs[:6]
        gx_ref, dmeta_ref, dw_ref = refs[6 + nex:9 + nex]
        exchange = _ChipExchange(refs[6:6 + nex], refs[9 + nex:9 + 2 * nex], refs[-2], refs[-1])
        i = pl.program_id(0)

        @pl.when(i == 0)
        def _():
            exchange.send()

        @pl.when(i == nt - 1)
        def _():
            exchange.finish()

        x = jnp.where(i == 0, head_ref[...], x_ref[...])
        r = lax.rsqrt(jnp.mean(x * x, axis=-1, keepdims=True) + EPS)
        n = x * r
        du_v = None
        for s in range(nsl):
            for hf in range(2):
                part = _dot_nt(dp_ref[:, s * wsl + hf * wcol:s * wsl + (hf + 1) * wcol],
                               win_ref[s, :, hf * wcol:(hf + 1) * wcol])
                du_v = part if du_v is None else du_v + part
        dn = du_v * w_ref[...]
        dh = dres_ref[...] + r * (dn - n * jnp.mean(dn * n, axis=-1, keepdims=True))
        gx_ref[...] = dh
        part = jnp.sum(du_v * n, axis=0, keepdims=True)

        @pl.when(i == 0)
        def _():
            dmeta_ref[...] = dh[PAD:TR, :]
            dw_ref[...] = part

        @pl.when(i > 0)
        def _():
            dw_ref[...] += part

    row = pl.BlockSpec((TR, D), lambda i: (i, 0))
    vec = pl.BlockSpec((1, D), lambda i: (0, 0))
    hbm = pl.BlockSpec(memory_space=pl.ANY)
    return pl.pallas_call(
        body, name="input_grad_block", grid=(nt,),
        in_specs=[pl.BlockSpec((TR, 9 * D), lambda i: (i, 0)),
                  pl.BlockSpec((TR, D), lambda i: (jnp.maximum(i - 1, 0), 0)), pl.BlockSpec((TR, D), lambda i: (0, 0)),
                  row, vec, pl.BlockSpec(memory_space=pltpu.VMEM)] + [hbm] * nex,
        out_specs=[pl.BlockSpec((TR, D), lambda i: (jnp.maximum(i - 1, 0), 0)),
                   pl.BlockSpec((NMETA, D), lambda i: (0, 0)), vec] + [hbm] * nex,
        out_shape=[jax.ShapeDtypeStruct((T - TR, D), F32), jax.ShapeDtypeStruct((NMETA, D), F32),
                   jax.ShapeDtypeStruct((1, D), F32)]
        + [jax.ShapeDtypeStruct((3,) + p.shape[1:], p.dtype) for p in psums],
        scratch_shapes=[pltpu.SemaphoreType.DMA((nex, 3)), pltpu.SemaphoreType.DMA((nex, 3))],
        compiler_params=_cparams(("arbitrary",)),
    )(dproj, x_seq, head_tile, dres, w, w_in_g, *psums)


def _tri_matmul(tri_bf16, x):
    hi = x.astype(BF16)
    lo = (x - hi.astype(F32)).astype(BF16)
    return jnp.dot(tri_bf16, lo, preferred_element_type=F32) + jnp.dot(tri_bf16, hi, preferred_element_type=F32)


def _shift_down(x, prev8, n):
    rows = x.shape[0]
    return pltpu.roll(jnp.concatenate([prev8, x], axis=0), n, 0)[8:8 + rows, :]


def _shift_up(x, next8, n):
    rows = x.shape[0]
    return pltpu.roll(jnp.concatenate([x, next8], axis=0), rows + 8 - n, 0)[0:rows, :]


def _gates(f_raw, lb):
    sg = _sigmoid(f_raw)
    f = lb + (1.0 - lb) * sg
    return sg, f, jnp.log(f), 1.0 - f


def _lower_bound(lbp_ref):
    return _sigmoid(lbp_ref[0:1, :] - lbp_ref[1:2, :])


def _tri_masks():
    r = lax.broadcasted_iota(jnp.int32, (CH, CH), 0)
    c = lax.broadcasted_iota(jnp.int32, (CH, CH), 1)
    return r >= c, r <= c


def _ones_where(mask):
    return jnp.where(mask, 1.0, 0.0).astype(BF16)


def _dot(a, b):
    return jnp.dot(a.astype(BF16), b.astype(BF16), preferred_element_type=F32)


def _dot_nt(a, b):
    return lax.dot_general(a.astype(BF16), b.astype(BF16), (((1,), (1,)), ((), ())), preferred_element_type=F32)


def _dot_tn(a, b):
    return lax.dot_general(a.astype(BF16), b.astype(BF16), (((0,), (0,)), ((), ())), preferred_element_type=F32)


def _mix_block_fwd(x_seq, head_tile, lb_param, wattn, whn, conv_w, w_in_g, w_out, late):
    T = TR + x_seq.shape[0]
    nt = T // TR
    ncht = TR // CH
    nsl = w_in_g.shape[0]
    wsl = w_in_g.shape[2]
    wcol = wsl // 2

    nlate = len(late)

    def body(*refs):
        x_ref, head_ref, lbp_ref, wa_ref, whn_ref, cw_ref, win_ref, wout_ref = refs[:8]
        ut_ref, projb_ref, o_ref, sst_ref, mt_ref, h1_ref = refs[8 + nlate:14 + nlate]
        late_refs = refs[14 + nlate:14 + 2 * nlate]
        proj_ref, m_ref, st_ref, cxc_ref, send_sems, recv_sems = refs[14 + 2 * nlate:]
        i = pl.program_id(0)
        gather = _Gather(late_refs, send_sems, recv_sems, (False,) * nlate)

        @pl.when(i == 0)
        def _():
            st_ref[...] = jnp.zeros_like(st_ref)
            cxc_ref[...] = jnp.zeros_like(cxc_ref)
            gather.send()

        @pl.when(i == nt // 2)
        def _():
            gather.relay()

        x = jnp.where(i == 0, head_ref[...], x_ref[...])
        r1 = lax.rsqrt(jnp.mean(x * x, axis=-1, keepdims=True) + EPS)
        u_f = x * r1 * wa_ref[...]
        u = u_f.astype(BF16)
        ut_ref[...] = u_f.T.astype(BF16)
        for s in range(nsl):
            for hf in range(2):
                cols = slice(s * wsl + hf * wcol, s * wsl + (hf + 1) * wcol)
                p = jnp.dot(u, win_ref[s, :, hf * wcol:(hf + 1) * wcol], preferred_element_type=F32)
                proj_ref[:, cols] = p
                projb_ref[:, cols] = p.astype(BF16)

        lb = _lower_bound(lbp_ref)
        lower, _ = _tri_masks()
        ltri = _ones_where(lower)
        whn_v = whn_ref[...]
        w0, w1, w2 = cw_ref[0:1, :], cw_ref[1:2, :], cw_ref[2:3, :]

        def chunk(c, carry):
            rows = pl.ds(pl.multiple_of(c * CH, CH), CH)
            q_raw = proj_ref[rows, 0:D]
            f_raw = proj_ref[rows, D:2 * D]
            v = proj_ref[rows, 2 * D:3 * D]
            q = q_raw * _sigmoid(q_raw)
            _, _, g, k = _gates(f_raw, lb)
            gam = _tri_matmul(ltri, g)
            gam_l = gam[CH - 1:CH, :]
            e_l = jnp.exp(gam_l)
            qt = (q * jnp.exp(gam)).astype(BF16)
            kt = (k * jnp.exp(-gam)).astype(BF16)
            khat = (k * jnp.exp(gam_l - gam)).astype(BF16)
            vb = v.astype(BF16)
            on_parts = []
            for h in range(NH):
                cs = slice(h * HD, (h + 1) * HD)
                st = st_ref[h]
                sst_ref[c, h] = st
                a = jnp.where(lower, _dot_nt(qt[:, cs], kt[:, cs]), 0.0)
                o_h = _dot_nt(qt[:, cs], st) + _dot(a, vb[:, cs])
                st_ref[h] = st * e_l[:, cs] + _dot_tn(vb[:, cs], khat[:, cs])
                o_ref[rows, cs] = o_h
                ro = lax.rsqrt(jnp.mean(o_h * o_h, axis=-1, keepdims=True) + EPS)
                on_parts.append(o_h * ro)
            on = jnp.concatenate(on_parts, axis=1)
            g_out = proj_ref[rows, 3 * D:4 * D]
            y_a = on * whn_v * (g_out * _sigmoid(g_out))
            cx = proj_ref[rows, 5 * D:6 * D] * proj_ref[rows, 6 * D:7 * D]
            prev8 = cxc_ref[...]
            cv = w0 * _shift_down(cx, prev8, 2) + w1 * _shift_down(cx, prev8, 1) + w2 * cx
            cxc_ref[...] = cx[CH - 8:CH, :]
            y_b = proj_ref[rows, 4 * D:5 * D] * cv
            m = _sigmoid(proj_ref[rows, 7 * D:8 * D]) * y_a + _sigmoid(proj_ref[rows, 8 * D:9 * D]) * y_b
            m_ref[rows, :] = m
            return carry

        lax.fori_loop(0, ncht, chunk, 0)
        m_v = m_ref[...]
        h1_ref[...] = x + jnp.dot(m_v.astype(BF16), wout_ref[...], preferred_element_type=F32)
        mt_ref[...] = m_v.T.astype(BF16)

        @pl.when(i == nt - 1)
        def _():
            gather.finish()

    row = lambda w: pl.BlockSpec((TR, w), lambda i: (i, 0))
    col = lambda w: pl.BlockSpec((w, TR), lambda i: (0, i))
    vec = lambda r: pl.BlockSpec((r, D), lambda i: (0, 0))
    vm = pl.BlockSpec(memory_space=pltpu.VMEM)
    hbm = pl.BlockSpec(memory_space=pl.ANY)
    return pl.pallas_call(
        body, name="mix_block_fwd", grid=(nt,),
        in_specs=[pl.BlockSpec((TR, D), lambda i: (jnp.maximum(i - 1, 0), 0)), pl.BlockSpec((TR, D), lambda i: (0, 0)),
                  vec(2), vec(1), vec(1), vec(3), vm, vm] + [hbm] * nlate,
        out_specs=[col(D), row(9 * D), row(D), pl.BlockSpec((ncht, NH, HD, HD), lambda i: (i, 0, 0, 0)),
                   col(D), row(D)] + [hbm] * nlate,
        out_shape=[jax.ShapeDtypeStruct((D, T), BF16), jax.ShapeDtypeStruct((T, 9 * D), BF16),
                   jax.ShapeDtypeStruct((T, D), F32), jax.ShapeDtypeStruct((T // CH, NH, HD, HD), F32),
                   jax.ShapeDtypeStruct((D, T), BF16), jax.ShapeDtypeStruct((T, D), F32)]
        + [jax.ShapeDtypeStruct(a.shape, a.dtype) for a in late],
        input_output_aliases={8 + n: 6 + n for n in range(nlate)},
        scratch_shapes=[pltpu.VMEM((TR, 9 * D), F32), pltpu.VMEM((TR, D), F32), pltpu.VMEM((NH, HD, HD), F32),
                        pltpu.VMEM((8, D), F32), pltpu.SemaphoreType.DMA((nlate, 6)),
                        pltpu.SemaphoreType.DMA((nlate, 6))],
        compiler_params=_cparams(("arbitrary",)),
    )(x_seq, head_tile, lb_param, wattn, whn, conv_w, w_in_g, w_out, *late)


def _mix_block_bwd(dh1, projb, o, sst, lb_param, whn, conv_w, w_out, psums):
    T = projb.shape[0]
    nt = T // TR
    ncht = TR // CH
    tb16 = TR // 16
    nex = len(psums)

    def body(*refs):
        dh1_ref, proj_ref, pc_ref, px_ref, o_ref, sst_ref, lbp_ref, whn_ref, cw_ref, wout_ref = refs[:10]
        dproj_ref, dlb_ref, dwhn_ref, dcw_ref = refs[10 + nex:14 + nex]
        exchange = _ChipExchange(refs[10:10 + nex], refs[14 + nex:14 + 2 * nex], refs[-2], refs[-1])
        dm_ref, dst_ref, dcvc_ref, acc_lb, acc_hn, acc_cw = refs[14 + 2 * nex:-2]
        s = pl.program_id(0)
        tile = nt - 1 - s

        @pl.when(s == 0)
        def _():
            dst_ref[...] = jnp.zeros_like(dst_ref)
            dcvc_ref[...] = jnp.zeros_like(dcvc_ref)
            acc_lb[...] = jnp.zeros_like(acc_lb)
            acc_hn[...] = jnp.zeros_like(acc_hn)
            acc_cw[...] = jnp.zeros_like(acc_cw)
            exchange.send()

        dm_ref[...] = _dot_nt(dh1_ref[...], wout_ref[...])
        lb = _lower_bound(lbp_ref)
        lower, upper = _tri_masks()
        ltri = _ones_where(lower)
        utri = _ones_where(upper)
        whn_v = whn_ref[...]
        w0, w1, w2 = cw_ref[0:1, :], cw_ref[1:2, :], cw_ref[2:3, :]
        cx_before_tile = jnp.where(tile > 0, (pc_ref[...].astype(F32) * px_ref[...].astype(F32))[8:16, :], 0.0)
        rid = lax.broadcasted_iota(jnp.int32, (CH, D), 0)

        def chunk(cc, carry):
            c = ncht - 1 - cc
            r0 = pl.multiple_of(c * CH, CH)
            rows = pl.ds(r0, CH)
            slab = lambda n: proj_ref[rows, n * D:(n + 1) * D].astype(F32)
            q_raw, f_raw, v, g_out, b_gate, c_gate, x_conv = (slab(n) for n in range(7))
            sa = _sigmoid(slab(7))
            sb = _sigmoid(slab(8))
            dm_v = dm_ref[rows, :]

            sq = _sigmoid(q_raw)
            q = q_raw * sq
            sg, f, g, k = _gates(f_raw, lb)
            gam = _tri_matmul(ltri, g)
            gam_l = gam[CH - 1:CH, :]
            e_l = jnp.exp(gam_l)
            e_g = jnp.exp(gam)
            e_ng = jnp.exp(-gam)
            e_kl = jnp.exp(gam_l - gam)
            qt = q * e_g
            kt = k * e_ng
            khat = k * e_kl
            qt_b, kt_b, khat_b, v_b = qt.astype(BF16), kt.astype(BF16), khat.astype(BF16), v.astype(BF16)
            qt_seen, kt_seen = qt_b.astype(F32), kt_b.astype(F32)
            s_go = _sigmoid(g_out)
            silu_go = g_out * s_go
            cx = c_gate * x_conv
            rprev = pl.ds(pl.multiple_of(jnp.maximum(r0 - 16, 0), 16), 16)
            cx_prev_in = (proj_ref[rprev, 5 * D:6 * D].astype(F32) * proj_ref[rprev, 6 * D:7 * D].astype(F32))[8:16, :]
            prev8 = jnp.where(c > 0, cx_prev_in, cx_before_tile)
            cx_m1 = _shift_down(cx, prev8, 1)
            cx_m2 = _shift_down(cx, prev8, 2)
            cv = w0 * cx_m2 + w1 * cx_m1 + w2 * cx
            y_b = b_gate * cv

            o_v = o_ref[rows, :]
            ro_parts, on_parts = [], []
            for h in range(NH):
                cs = slice(h * HD, (h + 1) * HD)
                o_h = o_v[:, cs]
                ro = lax.rsqrt(jnp.mean(o_h * o_h, axis=-1, keepdims=True) + EPS)
                ro_parts.append(ro)
                on_parts.append(o_h * ro)
            on = jnp.concatenate(on_parts, axis=1)
            y_a = on * whn_v * silu_go

            dy_a = dm_v * sa
            dy_b = dm_v * sb
            dproj_ref[rows, 7 * D:8 * D] = (dm_v * y_a * sa * (1.0 - sa)).astype(BF16)
            dproj_ref[rows, 8 * D:9 * D] = (dm_v * y_b * sb * (1.0 - sb)).astype(BF16)
            dproj_ref[rows, 4 * D:5 * D] = (dy_b * cv).astype(BF16)
            dcv = dy_b * b_gate
            acc_cw[0:1, :] += jnp.sum(dcv * cx_m2, axis=0, keepdims=True)
            acc_cw[1:2, :] += jnp.sum(dcv * cx_m1, axis=0, keepdims=True)
            acc_cw[2:3, :] += jnp.sum(dcv * cx, axis=0, keepdims=True)
            next8 = dcvc_ref[...]
            dcx = w2 * dcv + w1 * _shift_up(dcv, next8, 1) + w0 * _shift_up(dcv, next8, 2)
            dcvc_ref[...] = dcv[0:8, :]
            dproj_ref[rows, 5 * D:6 * D] = (dcx * x_conv).astype(BF16)
            dproj_ref[rows, 6 * D:7 * D] = (dcx * c_gate).astype(BF16)
            don = dy_a * whn_v * silu_go
            dproj_ref[rows, 3 * D:4 * D] = (dy_a * on * whn_v * (s_go * (1.0 + g_out * (1.0 - s_go)))).astype(BF16)
            acc_hn[...] += jnp.sum(dy_a * silu_go * on, axis=0, keepdims=True)

            dq_parts, dk_parts, dv_parts, dgam_parts, ext_parts = [], [], [], [], []
            for h in range(NH):
                cs = slice(h * HD, (h + 1) * HD)
                on_h = on_parts[h]
                don_h = don[:, cs]
                do_h = ro_parts[h] * (don_h - on_h * jnp.mean(don_h * on_h, axis=-1, keepdims=True))
                qt_h, kt_h, khat_h, v_h = qt_b[:, cs], kt_b[:, cs], khat_b[:, cs], v_b[:, cs]
                do_b = do_h.astype(BF16)
                st = sst_ref[c, h]
                dstn = dst_ref[h]
                dstn_b = dstn.astype(BF16)
                a_t = jnp.where(upper, _dot_nt(kt_h, qt_h), 0.0)
                da = jnp.where(lower, _dot_nt(do_b, v_h), 0.0)
                da_t = jnp.where(upper, _dot_nt(v_h, do_b), 0.0)
                dv_h = _dot(a_t, do_b) + _dot_nt(khat_h, dstn_b)
                dqt_state = _dot(do_b, st)
                dqt_chunk = _dot(da, kt_h)
                dkt = _dot(da_t, qt_h)
                dkhat = _dot(v_h, dstn_b)
                dq_h = (dqt_state + dqt_chunk) * e_g[:, cs]
                dk_h = dkt * e_ng[:, cs] + dkhat * e_kl[:, cs]
                khat_dkhat = dkhat * khat[:, cs]
                ext = (jnp.sum(khat_dkhat, axis=0, keepdims=True)
                       + e_l[:, cs] * jnp.sum(st * dstn, axis=0, keepdims=True))
                dst_ref[h] = _dot_tn(do_b, qt_h) + dstn * e_l[:, cs]
                dq_parts.append(dq_h)
                dk_parts.append(dk_h)
                dv_parts.append(dv_h)
                dgam_parts.append(qt[:, cs] * dqt_state + qt_seen[:, cs] * dqt_chunk - kt_seen[:, cs] * dkt
                                  - khat_dkhat)
                ext_parts.append(ext)
            dq = jnp.concatenate(dq_parts, axis=1)
            dk = jnp.concatenate(dk_parts, axis=1)
            dgam = jnp.concatenate(dgam_parts, axis=1)
            ext = jnp.concatenate(ext_parts, axis=1)
            dgam = dgam + jnp.where(rid == CH - 1, ext, 0.0)
            dg = _tri_matmul(utri, dgam)
            dproj_ref[rows, 0:D] = (dq * (sq * (1.0 + q_raw * (1.0 - sq)))).astype(BF16)
            df = dg * jnp.exp(-g) - dk
            dproj_ref[rows, D:2 * D] = (df * (1.0 - lb) * sg * (1.0 - sg)).astype(BF16)
            dproj_ref[rows, 2 * D:3 * D] = jnp.concatenate(dv_parts, axis=1).astype(BF16)
            real = (tile * TR + r0 + rid) >= PAD
            acc_lb[...] += jnp.sum(jnp.where(real, df * (1.0 - sg), 0.0), axis=0, keepdims=True)
            return carry

        lax.fori_loop(0, ncht, chunk, 0)

        @pl.when(s == nt - 1)
        def _():
            dlb_ref[...] = acc_lb[...] * lb * (1.0 - lb)
            hn = acc_hn[...]
            tot = hn[:, 0:HD]
            for h in range(1, NH):
                tot = tot + hn[:, h * HD:(h + 1) * HD]
            dwhn_ref[...] = tot
            dcw_ref[...] = acc_cw[0:3, :]
            exchange.finish()

    hbm = pl.BlockSpec(memory_space=pl.ANY)
    rev = lambda s: (nt - 1 - s, 0)
    prevc = lambda s: (jnp.maximum((nt - 1 - s) * tb16 - 1, 0), 5)
    prevx = lambda s: (jnp.maximum((nt - 1 - s) * tb16 - 1, 0), 6)
    const = lambda s: (0, 0)
    return pl.pallas_call(
        body, name="mix_block_bwd", grid=(nt,),
        in_specs=[pl.BlockSpec((TR, D), rev),
                  pl.BlockSpec((TR, 9 * D), rev),
                  pl.BlockSpec((16, D), prevc),
                  pl.BlockSpec((16, D), prevx),
                  pl.BlockSpec((TR, D), rev),
                  pl.BlockSpec((ncht, NH, HD, HD), lambda s: (nt - 1 - s, 0, 0, 0)),
                  pl.BlockSpec((2, D), const),
                  pl.BlockSpec((1, D), const),
                  pl.BlockSpec((3, D), const),
                  pl.BlockSpec(memory_space=pltpu.VMEM)] + [hbm] * nex,
        out_specs=[pl.BlockSpec((TR, 9 * D), rev),
                   pl.BlockSpec((1, D), const),
                   pl.BlockSpec((1, HD), const),
                   pl.BlockSpec((3, D), const)] + [hbm] * nex,
        out_shape=[jax.ShapeDtypeStruct((T, 9 * D), BF16), jax.ShapeDtypeStruct((1, D), F32),
                   jax.ShapeDtypeStruct((1, HD), F32), jax.ShapeDtypeStruct((3, D), F32)]
        + [jax.ShapeDtypeStruct((3,) + p.shape[1:], p.dtype) for p in psums],
        scratch_shapes=[pltpu.VMEM((TR, D), F32), pltpu.VMEM((NH, HD, HD), F32), pltpu.VMEM((8, D), F32),
                        pltpu.VMEM((1, D), F32), pltpu.VMEM((1, D), F32), pltpu.VMEM((8, D), F32),
                        pltpu.SemaphoreType.DMA((nex, 3)), pltpu.SemaphoreType.DMA((nex, 3))],
        compiler_params=_cparams(("arbitrary",)),
    )(dh1, projb, projb, projb, o, sst, lb_param, whn, conv_w, w_out, *psums)


def _ffn_block_fwd(h1, tgt, wffn, w_up_g, fcw, fcb, w_down, wfin):
    T = h1.shape[0]
    nt = T // TR
    nsl = w_up_g.shape[0]
    wsl = w_up_g.shape[2]

    def body(h_ref, t_ref, wn_ref, wup_ref, cw_ref, cb_ref, wdn_ref, wf_ref,
             u2t_ref, upb_ref, ggt_ref, dh_ref, loss_ref, dwf_ref, up_scr, gg_ref, carry_ref):
        i = pl.program_id(0)

        @pl.when(i == 0)
        def _():
            carry_ref[...] = jnp.zeros_like(carry_ref)
            loss_ref[...] = jnp.zeros_like(loss_ref)
            dwf_ref[...] = jnp.zeros_like(dwf_ref)

        x = h_ref[...]
        r2 = lax.rsqrt(jnp.mean(x * x, axis=-1, keepdims=True) + EPS)
        u2_f = x * r2 * wn_ref[...]
        u2 = u2_f.astype(BF16)
        u2t_ref[...] = u2_f.T.astype(BF16)
        for s in range(nsl):
            up_s = jnp.dot(u2, wup_ref[s], preferred_element_type=F32)
            up_scr[:, s * wsl:(s + 1) * wsl] = up_s
            upb_ref[:, s * wsl:(s + 1) * wsl] = up_s.astype(BF16)
        w0, w1, w2 = cw_ref[0:1, :], cw_ref[1:2, :], cw_ref[2:3, :]

        def chunk(c, carry):
            rows = pl.ds(pl.multiple_of(c * CH, CH), CH)
            a_pre = up_scr[rows, 0:DFF]
            val = up_scr[rows, DFF:2 * DFF]
            prev8 = carry_ref[...]
            a = w0 * _shift_down(a_pre, prev8, 2) + w1 * _shift_down(a_pre, prev8, 1) + w2 * a_pre + cb_ref[...]
            carry_ref[...] = a_pre[CH - 8:CH, :]
            gg_ref[rows, :] = a * _sigmoid(a) * val
            return carry

        lax.fori_loop(0, TR // CH, chunk, 0)
        gg_v = gg_ref[...]
        ggt_ref[...] = gg_v.T.astype(BF16)
        h2 = x + jnp.dot(gg_v.astype(BF16), wdn_ref[...], preferred_element_type=F32)
        r3 = lax.rsqrt(jnp.mean(h2 * h2, axis=-1, keepdims=True) + EPS)
        n3 = h2 * r3
        wf = wf_ref[...]
        diff = jnp.where(i > 0, n3 * wf - t_ref[...], 0.0)
        loss_ref[...] += jnp.sum(diff * diff, axis=0, keepdims=True) * (0.5 / D)
        dy = diff * (1.0 / D)
        dwf_ref[...] += jnp.sum(dy * n3, axis=0, keepdims=True)
        dn = dy * wf
        dh_ref[...] = r3 * (dn - n3 * jnp.mean(dn * n3, axis=-1, keepdims=True))

    row = lambda w: pl.BlockSpec((TR, w), lambda i: (i, 0))
    col = lambda w: pl.BlockSpec((w, TR), lambda i: (0, i))
    vec = lambda w, r=1: pl.BlockSpec((r, w), lambda i: (0, 0))
    vm = pl.BlockSpec(memory_space=pltpu.VMEM)
    return pl.pallas_call(
        body, name="ffn_block_fwd", grid=(nt,),
        in_specs=[row(D), pl.BlockSpec((TR, D), lambda i: (jnp.maximum(i - 1, 0), 0)), vec(D), vm,
                  vec(DFF, 3), vec(DFF), vm, vec(D)],
        out_specs=[col(D), row(2 * DFF), col(DFF), row(D), vec(D), vec(D)],
        out_shape=[jax.ShapeDtypeStruct((D, T), BF16), jax.ShapeDtypeStruct((T, 2 * DFF), BF16),
                   jax.ShapeDtypeStruct((DFF, T), BF16), jax.ShapeDtypeStruct((T, D), F32),
                   jax.ShapeDtypeStruct((1, D), F32), jax.ShapeDtypeStruct((1, D), F32)],
        scratch_shapes=[pltpu.VMEM((TR, 2 * DFF), F32), pltpu.VMEM((TR, DFF), F32), pltpu.VMEM((8, DFF), F32)],
        compiler_params=_cparams(("arbitrary",)),
    )(h1, tgt, wffn, w_up_g, fcw, fcb, w_down, wfin)


def _ffn_block_bwd(dh2, upb, h1, wffn, w_up_g, fcw, fcb, w_down):
    T = h1.shape[0]
    nt = T // TR
    ncht = TR // CH
    nsl = w_up_g.shape[0]
    wsl = w_up_g.shape[2]
    tb16 = TR // 16
    assert nsl == ncht
    step = -(-DFF // (ncht * 128)) * 128
    parts = [(c0, min(c0 + step, DFF)) for c0 in range(0, DFF, step)]
    assert len(parts) == ncht

    def body(dh2n_ref, dh2p_ref, up_ref, pa_ref, h_ref, wn_ref, wup_ref, cw_ref, cb_ref, wdn_ref,
             dup_ref, dh1_ref, dfw_ref, dfb_ref, dwn_ref,
             dgg_ring, dup_ring, carry_ref, acc_w, acc_b, acc_n):
        s = pl.program_id(0)
        slot = lax.rem(s, 2)
        other = 1 - slot

        @pl.when(s == 0)
        def _():
            carry_ref[...] = jnp.zeros_like(carry_ref)
            acc_w[...] = jnp.zeros_like(acc_w)
            acc_b[...] = jnp.zeros_like(acc_b)
            acc_n[...] = jnp.zeros_like(acc_n)
            dup_ring[1] = jnp.zeros((TR, 2 * DFF), BF16)
            dgg_ring[0] = _dot_nt(dh2p_ref[...], wdn_ref[...])

        def norm_bwd(du2, valid):
            x = h_ref[...]
            dh2p = dh2p_ref[...]
            r2 = lax.rsqrt(jnp.mean(x * x, axis=-1, keepdims=True) + EPS)
            n2 = x * r2
            dn = du2 * wn_ref[...]
            dh1_ref[...] = dh2p + r2 * (dn - n2 * jnp.mean(dn * n2, axis=-1, keepdims=True))
            acc_n[...] += jnp.where(valid, jnp.sum(du2 * n2, axis=0, keepdims=True), 0.0)

        @pl.when(s < nt)
        def _():
            tile = nt - 1 - s
            w0, w1, w2 = cw_ref[0:1, :], cw_ref[1:2, :], cw_ref[2:3, :]
            a_before_tile = jnp.where(tile > 0, pa_ref[...].astype(F32)[8:16, :], 0.0)
            dh2n = dh2n_ref[...].astype(BF16)
            du2 = None
            for idx in range(ncht):
                c = ncht - 1 - idx
                r0 = c * CH
                rows = slice(r0, r0 + CH)
                a_pre = up_ref[rows, 0:DFF].astype(F32)
                val = up_ref[rows, DFF:2 * DFF].astype(F32)
                prev8 = up_ref[r0 - 16:r0, 0:DFF].astype(F32)[8:16, :] if c > 0 else a_before_tile
                a_m1 = _shift_down(a_pre, prev8, 1)
                a_m2 = _shift_down(a_pre, prev8, 2)
                a = w0 * a_m2 + w1 * a_m1 + w2 * a_pre + cb_ref[...]
                sig = _sigmoid(a)
                dgg_v = dgg_ring[slot, rows, :]
                da = dgg_v * val * (sig * (1.0 + a * (1.0 - sig)))
                dval = (dgg_v * (a * sig)).astype(BF16)
                next8 = carry_ref[...]
                da_pre = (w2 * da + w1 * _shift_up(da, next8, 1) + w0 * _shift_up(da, next8, 2)).astype(BF16)
                carry_ref[...] = da[0:8, :]
                dup_ref[rows, 0:DFF] = da_pre
                dup_ref[rows, DFF:2 * DFF] = dval
                dup_ring[slot, rows, 0:DFF] = da_pre
                dup_ring[slot, rows, DFF:2 * DFF] = dval
                acc_w[0:1, :] += jnp.sum(da * a_m2, axis=0, keepdims=True)
                acc_w[1:2, :] += jnp.sum(da * a_m1, axis=0, keepdims=True)
                acc_w[2:3, :] += jnp.sum(da * a_pre, axis=0, keepdims=True)
                acc_b[...] += jnp.sum(da, axis=0, keepdims=True)
                c0, c1 = parts[idx]
                dgg_ring[other, :, c0:c1] = _dot_nt(dh2n, wdn_ref[c0:c1, :])
                part = _dot_nt(dup_ring[other, :, idx * wsl:(idx + 1) * wsl], wup_ref[idx])
                du2 = part if du2 is None else du2 + part
            norm_bwd(du2, s > 0)

        @pl.when(s == nt)
        def _():
            du2 = _dot_nt(dup_ring[other, :, 0:wsl], wup_ref[0])
            for sl in range(1, nsl):
                du2 = du2 + _dot_nt(dup_ring[other, :, sl * wsl:(sl + 1) * wsl], wup_ref[sl])
            norm_bwd(du2, True)
            dfw_ref[...] = acc_w[0:3, :]
            dfb_ref[...] = acc_b[...]
            dwn_ref[...] = acc_n[...]

    gate_tile = lambda s: jnp.maximum(nt - 1 - s, 0)
    next_tile = lambda s: jnp.maximum(nt - 2 - s, 0)
    prev_tile = lambda s: jnp.minimum(nt - s, nt - 1)
    vec = lambda w, r=1: pl.BlockSpec((r, w), lambda s: (0, 0))
    vm = pl.BlockSpec(memory_space=pltpu.VMEM)
    return pl.pallas_call(
        body, name="ffn_block_bwd", grid=(nt + 1,),
        in_specs=[pl.BlockSpec((TR, D), lambda s: (next_tile(s), 0)),
                  pl.BlockSpec((TR, D), lambda s: (prev_tile(s), 0)),
                  pl.BlockSpec((TR, 2 * DFF), lambda s: (gate_tile(s), 0)),
                  pl.BlockSpec((16, DFF), lambda s: (jnp.maximum(gate_tile(s) * tb16 - 1, 0), 0)),
                  pl.BlockSpec((TR, D), lambda s: (prev_tile(s), 0)),
                  vec(D), vm, vec(DFF, 3), vec(DFF), vm],
        out_specs=[pl.BlockSpec((TR, 2 * DFF), lambda s: (gate_tile(s), 0)),
                   pl.BlockSpec((TR, D), lambda s: (prev_tile(s), 0)),
                   vec(DFF, 3), vec(DFF), vec(D)],
        out_shape=[jax.ShapeDtypeStruct((T, 2 * DFF), BF16), jax.ShapeDtypeStruct((T, D), F32),
                   jax.ShapeDtypeStruct((3, DFF), F32), jax.ShapeDtypeStruct((1, DFF), F32),
                   jax.ShapeDtypeStruct((1, D), F32)],
        scratch_shapes=[pltpu.VMEM((2, TR, DFF), F32), pltpu.VMEM((2, TR, 2 * DFF), BF16),
                        pltpu.VMEM((8, DFF), F32), pltpu.VMEM((8, DFF), F32), pltpu.VMEM((1, DFF), F32),
                        pltpu.VMEM((1, D), F32)],
        compiler_params=_cparams(("arbitrary",)),
    )(dh2, dh2, upb, upb, h1, wffn, w_up_g, fcw, fcb, w_down)


def _place():
    x, y, c = lax.axis_index("x"), lax.axis_index("y"), lax.axis_index("c")
    return x, y, c


_CHIP_FLIPS = ((1, 0), (0, 1), (1, 1))


def _flip(v, bit):
    return 1 - v if bit else v


def _into_slot(w, j_idx, *, rb, dtype, name):
    r, cdim = w.shape

    def body(j_ref, w_ref, out_ref):
        del j_ref
        out_ref[...] = w_ref[...].astype(dtype)

    grid_spec = pltpu.PrefetchScalarGridSpec(
        num_scalar_prefetch=1, grid=(r // rb,),
        in_specs=[pl.BlockSpec((rb, cdim), lambda i, j_ref: (i, 0))],
        out_specs=pl.BlockSpec((None, rb, cdim), lambda i, j_ref: (j_ref[0], i, 0)))
    return pl.pallas_call(
        body, name=name, grid_spec=grid_spec, out_shape=jax.ShapeDtypeStruct((NSHARD, r, cdim), dtype),
        compiler_params=_cparams(("parallel",)),
    )(j_idx, w)


class _Gather:
    def __init__(self, outs, send_sems, recv_sems, whole):
        self.outs, self.send_sems, self.recv_sems, self.whole = outs, send_sems, recv_sems, whole
        self.x, self.y, self.c = _place()
        self.j = 2 * self.x + self.y
        self.sibling = (self.x, self.y, 1 - self.c)

    def _rows(self, w, core):
        r = self.outs[w].shape[1]
        return pl.ds(0, r) if self.whole[w] else pl.ds(core * (r // 2), r // 2)

    def _copy(self, w, slot, core, sem, to):
        piece = self.outs[w].at[slot, self._rows(w, core), :]
        return pltpu.make_async_remote_copy(
            src_ref=piece, dst_ref=piece, send_sem=self.send_sems.at[w, sem], recv_sem=self.recv_sems.at[w, sem],
            device_id=to, device_id_type=MESH)

    def _chips(self):
        for kk, (fx, fy) in enumerate(_CHIP_FLIPS):
            px, py = _flip(self.x, fx), _flip(self.y, fy)
            yield kk, 2 * px + py, (px, py, self.c)

    def send(self):
        for w in range(len(self.outs)):
            for kk, _, to in self._chips():
                self._copy(w, self.j, self.c, kk, to).start()

    def relay(self):
        for w in range(len(self.outs)):
            for kk, jk, _ in self._chips():
                self._copy(w, jk, self.c, kk, self.sibling).wait_recv()
                if not self.whole[w]:
                    self._copy(w, jk, self.c, 3 + kk, self.sibling).start()

    def finish(self):
        for w in range(len(self.outs)):
            for kk, jk, to in self._chips():
                self._copy(w, self.j, self.c, kk, to).wait_send()
                if not self.whole[w]:
                    self._copy(w, jk, 1 - self.c, 3 + kk, self.sibling).wait_recv()
                    self._copy(w, jk, self.c, 3 + kk, self.sibling).wait_send()


def _allgather_weights(slotted, whole):
    n = len(slotted)

    def body(*refs):
        g = _Gather(refs[n:2 * n], refs[2 * n], refs[2 * n + 1], whole)
        g.send()
        g.relay()
        g.finish()

    any_spec = pl.BlockSpec(memory_space=pl.ANY)
    return pl.pallas_call(
        body, name="allgather_weights",
        in_specs=[any_spec] * n, out_specs=[any_spec] * n,
        out_shape=[jax.ShapeDtypeStruct(a.shape, a.dtype) for a in slotted],
        input_output_aliases={i: i for i in range(n)},
        scratch_shapes=[pltpu.SemaphoreType.DMA((n, 6)), pltpu.SemaphoreType.DMA((n, 6))],
    )(*slotted)


class _ChipExchange:
    def __init__(self, ins, outs, send_sems, recv_sems):
        self.ins, self.outs, self.send_sems, self.recv_sems = ins, outs, send_sems, recv_sems
        self.x, self.y, self.c = _place()

    def _copies(self):
        for w in range(len(self.ins)):
            for kk, (fx, fy) in enumerate(_CHIP_FLIPS):
                px, py = _flip(self.x, fx), _flip(self.y, fy)
                yield pltpu.make_async_remote_copy(
                    src_ref=self.ins[w].at[2 * px + py], dst_ref=self.outs[w].at[kk],
                    send_sem=self.send_sems.at[w, kk], recv_sem=self.recv_sems.at[w, kk],
                    device_id=(px, py, self.c), device_id_type=MESH)

    def send(self):
        for cp in self._copies():
            cp.start()

    def finish(self):
        for cp in self._copies():
            cp.wait()


def _pair_exchange(grads, name):
    nw = len(grads)

    def body(*refs):
        ins, outs = refs[:nw], refs[nw:2 * nw]
        send_sems, recv_sems = refs[2 * nw:]
        x, y, c = _place()
        sibling = (x, y, 1 - c)
        cps = []
        for w in range(nw):
            half = ins[w].shape[1] // 2
            cp = pltpu.make_async_remote_copy(
                src_ref=ins[w].at[:, pl.ds((1 - c) * half, half), :], dst_ref=outs[w],
                send_sem=send_sems.at[w], recv_sem=recv_sems.at[w], device_id=sibling, device_id_type=MESH)
            cp.start()
            cps.append(cp)
        for cp in cps:
            cp.wait()

    any_spec = pl.BlockSpec(memory_space=pl.ANY)
    return pl.pallas_call(
        body, name=name,
        in_specs=[any_spec] * nw, out_specs=[any_spec] * nw,
        out_shape=[jax.ShapeDtypeStruct((g.shape[0], g.shape[1] // 2, g.shape[2]), g.dtype) for g in grads],
        scratch_shapes=[pltpu.SemaphoreType.DMA((nw,)), pltpu.SemaphoreType.DMA((nw,))],
    )(*grads)


def _pair_add(g, other, c_idx, *, rb, name):
    S, r, cdim = g.shape
    half = r // 2
    nb = half // rb

    def body(c_ref, g_ref, o_ref, out_ref):
        del c_ref
        out_ref[...] = (g_ref[...].astype(F32) + o_ref[...].astype(F32)).astype(BF16)

    grid_spec = pltpu.PrefetchScalarGridSpec(
        num_scalar_prefetch=1, grid=(S, nb),
        in_specs=[pl.BlockSpec((None, rb, cdim), lambda s, i, c_ref: (s, c_ref[0] * nb + i, 0)),
                  pl.BlockSpec((None, rb, cdim), lambda s, i, c_ref: (s, i, 0))],
        out_specs=pl.BlockSpec((None, rb, cdim), lambda s, i, c_ref: (s, i, 0)))
    return pl.pallas_call(
        body, name=name, grid_spec=grid_spec, out_shape=jax.ShapeDtypeStruct((S, half, cdim), BF16),
        compiler_params=_cparams(("parallel", "parallel")),
    )(c_idx, g, other)


def _chip_sum(psum, parts, cj_idx, *, rb, name):
    S, half, cdim = psum.shape
    nb = half // rb

    def body(cj_ref, own_ref, p_ref, out_ref):
        del cj_ref
        f = lambda v: v.astype(F32)
        out_ref[...] = ((f(own_ref[...]) + f(p_ref[0])) + f(p_ref[1])) + f(p_ref[2])

    grid_spec = pltpu.PrefetchScalarGridSpec(
        num_scalar_prefetch=1, grid=(nb,),
        in_specs=[pl.BlockSpec((None, rb, cdim), lambda i, cj: (cj[1], i, 0)),
                  pl.BlockSpec((3, rb, cdim), lambda i, cj: (0, i, 0))],
        out_specs=pl.BlockSpec((rb, cdim), lambda i, cj: (cj[0] * nb + i, 0)))
    return pl.pallas_call(
        body, name=name, grid_spec=grid_spec, out_shape=jax.ShapeDtypeStruct((2 * half, cdim), F32),
        compiler_params=_cparams(("parallel",)),
    )(cj_idx, psum, parts)


SLAB_W = 1024


def _final_exchange(pieces, grads):
    n, nw = len(pieces), len(grads)
    segs, at = [], 0
    for idx, p in enumerate(pieces):
        r, wd = p.shape
        for c0 in range(0, wd, SLAB_W):
            if r > 1:
                at = -(-at // 8) * 8
            segs.append((idx, c0, min(SLAB_W, wd - c0), at))
            at += r
    rows = -(-at // 8) * 8
    flips = [(fx, fy, fc) for fx in (0, 1) for fy in (0, 1) for fc in (0, 1)][1:]

    def body(*refs):
        ins = refs[:n]
        outs = refs[n + nw:2 * n + nw]
        g_refs = refs[2 * n + nw:2 * n + 2 * nw]
        mine_ref, slots_ref, send_sems, recv_sems, gsend_sems, grecv_sems = refs[2 * n + 2 * nw:]
        x, y, c = _place()
        me = 4 * x + 2 * y + c
        sibling = (x, y, 1 - c)

        def swap(w, core):
            half = g_refs[w].shape[0] // 2
            rows_ = g_refs[w].at[pl.ds(core * half, half), :]
            return pltpu.make_async_remote_copy(
                src_ref=rows_, dst_ref=rows_, send_sem=gsend_sems.at[w], recv_sem=grecv_sems.at[w],
                device_id=sibling, device_id_type=MESH)

        for w in range(nw):
            swap(w, c).start()
        mine_ref[...] = jnp.zeros_like(mine_ref)
        for idx, c0, wd, st in segs:
            r = ins[idx].shape[0]
            mine_ref[st:st + r, 0:wd] = ins[idx][:, c0:c0 + wd]
        slots_ref[me] = mine_ref[...]
        cps = []
        for kk, (fx, fy, fc) in enumerate(flips):
            cp = pltpu.make_async_remote_copy(
                src_ref=mine_ref, dst_ref=slots_ref.at[me], send_sem=send_sems.at[kk], recv_sem=recv_sems.at[kk],
                device_id=(_flip(x, fx), _flip(y, fy), _flip(c, fc)), device_id_type=MESH)
            cp.start()
            cps.append(cp)
        for cp in cps:
            cp.wait()
        tot = slots_ref[0]
        for d in range(1, 8):
            tot = tot + slots_ref[d]
        mine_ref[...] = tot
        for idx, c0, wd, st in segs:
            r = ins[idx].shape[0]
            val = mine_ref[st:st + r, 0:wd]
            if idx == n - 1:
                outs[idx][...] = jnp.sum(val, keepdims=True)
            else:
                outs[idx][:, c0:c0 + wd] = val
        for w in range(nw):
            swap(w, 1 - c).wait_recv()
            swap(w, c).wait_send()

    vm = pl.BlockSpec(memory_space=pltpu.VMEM)
    hbm = pl.BlockSpec(memory_space=pl.ANY)
    out_shape = ([jax.ShapeDtypeStruct(p.shape, F32) for p in pieces[:-1]] + [jax.ShapeDtypeStruct((1, 1), F32)]
                 + [jax.ShapeDtypeStruct(g.shape, g.dtype) for g in grads])
    res = pl.pallas_call(
        body, name="final_exchange", in_specs=[vm] * n + [hbm] * nw, out_specs=[vm] * n + [hbm] * nw,
        out_shape=out_shape, input_output_aliases={n + w: n + w for w in range(nw)},
        scratch_shapes=[pltpu.VMEM((rows, SLAB_W), F32), pltpu.VMEM((8, rows, SLAB_W), F32),
                        pltpu.SemaphoreType.DMA((7,)), pltpu.SemaphoreType.DMA((7,)),
                        pltpu.SemaphoreType.DMA((nw,)), pltpu.SemaphoreType.DMA((nw,))],
    )(*pieces, *grads)
    return res[:n], res[n:]


def _adamw(w, g, m, v, *, rb, name):
    r, cdim = w.shape

    def body(w_ref, g_ref, m_ref, v_ref, go_ref, d_ref, nm_ref, nv_ref):
        go_ref[...] = g_ref[...]
        d_ref[...], nm_ref[...], nv_ref[...] = _adamw_update(w_ref[...], g_ref[...], m_ref[...], v_ref[...])

    spec = pl.BlockSpec((rb, cdim), lambda i: (i, 0))
    shp = jax.ShapeDtypeStruct((r, cdim), F32)
    return pl.pallas_call(
        body, name=name, grid=(r // rb,), in_specs=[spec] * 4, out_specs=[spec] * 4, out_shape=[shp] * 4,
        compiler_params=_cparams(("parallel",)),
    )(w, g, m, v)


def _adamw_update(w, g, m, v):
    nm = ADAM_B1 * m + (1.0 - ADAM_B1) * g
    nv = ADAM_B2 * v + (1.0 - ADAM_B2) * (g * g)
    m_hat = nm / (1.0 - ADAM_B1 ** ADAM_STEP)
    v_hat = nv / (1.0 - ADAM_B2 ** ADAM_STEP)
    return -ADAM_LR * (m_hat / (jnp.sqrt(v_hat) + ADAM_EPS) + ADAM_WD * w), nm, nv


def _adamw_small(params):
    n = len(params)

    def body(*refs):
        ins, outs = refs[:4 * n], refs[4 * n:]
        for p in range(n):
            w_ref, g_ref, m_ref, v_ref = ins[4 * p:4 * p + 4]
            d, nm, nv = _adamw_update(w_ref[...], g_ref[...], m_ref[...], v_ref[...])
            outs[3 * p][...] = d
            outs[3 * p + 1][...] = nm
            outs[3 * p + 2][...] = nv

    vm = pl.BlockSpec(memory_space=pltpu.VMEM)
    flat = [a for p in params for a in p]
    out_shape = [jax.ShapeDtypeStruct(p[0].shape, F32) for p in params for _ in range(3)]
    res = pl.pallas_call(
        body, name="adamw_small", in_specs=[vm] * (4 * n), out_specs=[vm] * (3 * n), out_shape=out_shape,
    )(*flat)
    return [tuple(res[3 * p:3 * p + 3]) for p in range(n)]


_PAIR_ADD_ROWS = {"w_in": 256, "w_out": 128, "w_up": 256, "w_down": 176}


def _pair_sums(grads, names, c_idx, tag):
    others = _pair_exchange(grads, name=f"grad_pair_exchange_{tag}")
    return [_pair_add(g, o_, c_idx, rb=_PAIR_ADD_ROWS[n], name=f"pair_add_{n}") for g, o_, n in zip(grads, others, names)]


def _local_step(x, tgt, meta_full, lb_param, attn_norm_w, w_in_g, hgrn_norm_w, conv_w_full, w_out_full,
                ffn_norm_w, late_slotted, fcw_full, ffn_conv_b, final_norm_w, c_idx, cj_idx):
    seq = x.shape[0]
    T = TR + seq
    head_tile = jnp.concatenate([jnp.zeros((PAD, D), F32), meta_full], axis=0)
    whn_t = jnp.tile(hgrn_norm_w, (1, NH))

    ut, projb, o, sst, mt, h1, w_up_g, w_down_g = _mix_block_fwd(
        x, head_tile, lb_param, attn_norm_w, whn_t, conv_w_full, w_in_g, w_out_full, late_slotted)
    w_down_full = w_down_g.reshape(DFF, D)
    u2t, upb, ggt, dh2, loss_vec, dwfin = _ffn_block_fwd(
        h1, tgt, ffn_norm_w, w_up_g, fcw_full, ffn_conv_b, w_down_full, final_norm_w.reshape(1, D))

    dup, dh1, dfw, dfb, dwffn = _ffn_block_bwd(
        dh2, upb, h1, ffn_norm_w, w_up_g, fcw_full, ffn_conv_b, w_down_full)
    kb = 1408 if T % 1408 == 0 else TR
    g_up = _weight_grad(u2t, dup, bn=DFF, bk=kb, name="dw_up_mm", shard_cols=2 * DFF // NSHARD)
    g_down = _weight_grad(ggt, dh2, bn=D, bk=kb, name="dw_down_mm")
    ps_ffn = _pair_sums([g_up, g_down.reshape(NSHARD, DFF // NSHARD, D)], ("w_up", "w_down"), c_idx, "ffn")

    dproj, dlb, dwhn, dcw, *parts_ffn = _mix_block_bwd(
        dh1, projb, o, sst, lb_param, whn_t, conv_w_full, w_out_full, ps_ffn)
    g_in = _weight_grad(ut, dproj, bn=9 * D // NSHARD, bk=kb, name="dw_in_mm", shard_cols=9 * D // NSHARD)
    g_out = _weight_grad(mt, dh1, bn=D, bk=kb, name="dw_out_mm")
    ps_mix = _pair_sums([g_in, g_out.reshape(NSHARD, D // NSHARD, D)], ("w_in", "w_out"), c_idx, "mix")

    grad_x, dmeta, dwattn, *parts_mix = _input_grad_block(dproj, x, head_tile, dh1, attn_norm_w, w_in_g, ps_mix)

    halves = [_chip_sum(ps, p, cj_idx, rb=_PAIR_ADD_ROWS[n], name=f"chip_sum_{n}")
              for ps, p, n in zip(ps_mix + ps_ffn, parts_mix + parts_ffn, ("w_in", "w_out", "w_up", "w_down"))]
    small = dict(dlb=dlb, dwattn=dwattn, dwhn=dwhn, dwffn=dwffn, dfb=dfb, dwfin=dwfin,
                 dcw=dcw, dfw=dfw, dmeta=dmeta, loss=loss_vec)
    return grad_x, small, halves


_SMALL_ORDER = ("dmeta", "dcw", "dfw", "dlb", "dwattn", "dwhn", "dwffn", "dfb", "dwfin", "loss")


def kernel(x, meta_tokens, lb_param, attn_norm_w, w_in, hgrn_norm_w, conv_w, w_out, ffn_norm_w, w_up, ffn_conv_w, ffn_conv_b, w_down, final_norm_w, loss_target, m_meta_tokens, m_lb_param, m_attn_norm_w, m_w_in, m_hgrn_norm_w, m_conv_w, m_w_out, m_ffn_norm_w, m_w_up, m_ffn_conv_w, m_ffn_conv_b, m_w_down, m_final_norm_w, v_meta_tokens, v_lb_param, v_attn_norm_w, v_w_in, v_hgrn_norm_w, v_conv_w, v_w_out, v_ffn_norm_w, v_w_up, v_ffn_conv_w, v_ffn_conv_b, v_w_down, v_final_norm_w):
    xi, yi, ci = _place()
    j = 2 * xi + yi
    c_idx = jnp.reshape(ci, (1,)).astype(jnp.int32)

    j_idx = jnp.reshape(j, (1,)).astype(jnp.int32)
    ds_, fs_ = D // NSHARD, DFF // NSHARD
    widen = lambda a: jnp.pad(a, ((0, 0), (0, 768 - a.shape[1])))
    rows_small = jnp.concatenate([widen(meta_tokens), widen(conv_w[0]), widen(ffn_conv_w[0]),
                                  jnp.zeros((2, 768), F32)], axis=0)
    slotted = [_into_slot(w[0], j_idx, rb=rb, dtype=BF16, name=f"slot_{n}")
               for w, rb, n in ((w_in, 256, "w_in"), (w_out, 128, "w_out"), (w_up, 256, "w_up"), (w_down, 176, "w_down"))]
    s_in, s_out, s_up, s_down = slotted
    s_small = _into_slot(rows_small, j_idx, rb=rows_small.shape[0], dtype=F32, name="slot_small")
    w_in_g, w_out_g, small_g = _allgather_weights([s_in, s_out, s_small], (False, False, True))
    unshard = lambda a: jnp.transpose(a, (1, 0, 2)).reshape(a.shape[1], -1)
    meta_full = unshard(small_g[:, 0:NMETA, 0:ds_])
    conv_w_full = unshard(small_g[:, NMETA:NMETA + 3, 0:ds_])
    fcw_full = unshard(small_g[:, NMETA + 3:NMETA + 6, 0:fs_])

    cj_idx = jnp.stack([ci, j]).astype(jnp.int32)
    grad_x, small, halves = _local_step(
        x[0], loss_target[0], meta_full, lb_param, attn_norm_w, w_in_g, hgrn_norm_w, conv_w_full,
        w_out_g.reshape(D, D), ffn_norm_w, [s_up, s_down], fcw_full, ffn_conv_b, final_norm_w, c_idx, cj_idx)

    names = _SMALL_ORDER
    small_sums, g_big = _final_exchange([small[n] for n in names], halves)
    vals = dict(zip(names, small_sums))
    loss = vals["loss"].reshape(())
    g_small = {
        "meta_tokens": lax.dynamic_slice_in_dim(vals["dmeta"], j * (D // NSHARD), D // NSHARD, axis=1),
        "lb_param": jnp.concatenate([vals["dlb"], -vals["dlb"]], axis=0),
        "attn_norm_w": vals["dwattn"],
        "hgrn_norm_w": vals["dwhn"],
        "conv_w": lax.dynamic_slice_in_dim(vals["dcw"], j * (D // NSHARD), D // NSHARD, axis=1)[None],
        "ffn_norm_w": vals["dwffn"],
        "ffn_conv_w": lax.dynamic_slice_in_dim(vals["dfw"], j * (DFF // NSHARD), DFF // NSHARD, axis=1)[None],
        "ffn_conv_b": vals["dfb"],
        "final_norm_w": vals["dwfin"].reshape(D),
    }


    weights = {"meta_tokens": meta_tokens, "lb_param": lb_param, "attn_norm_w": attn_norm_w, "w_in": w_in,
               "hgrn_norm_w": hgrn_norm_w, "conv_w": conv_w, "w_out": w_out, "ffn_norm_w": ffn_norm_w,
               "w_up": w_up, "ffn_conv_w": ffn_conv_w, "ffn_conv_b": ffn_conv_b, "w_down": w_down,
               "final_norm_w": final_norm_w}
    ms = {"meta_tokens": m_meta_tokens, "lb_param": m_lb_param, "attn_norm_w": m_attn_norm_w, "w_in": m_w_in,
          "hgrn_norm_w": m_hgrn_norm_w, "conv_w": m_conv_w, "w_out": m_w_out, "ffn_norm_w": m_ffn_norm_w,
          "w_up": m_w_up, "ffn_conv_w": m_ffn_conv_w, "ffn_conv_b": m_ffn_conv_b, "w_down": m_w_down,
          "final_norm_w": m_final_norm_w}
    vs = {"meta_tokens": v_meta_tokens, "lb_param": v_lb_param, "attn_norm_w": v_attn_norm_w, "w_in": v_w_in,
          "hgrn_norm_w": v_hgrn_norm_w, "conv_w": v_conv_w, "w_out": v_w_out, "ffn_norm_w": v_ffn_norm_w,
          "w_up": v_w_up, "ffn_conv_w": v_ffn_conv_w, "ffn_conv_b": v_ffn_conv_b, "w_down": v_w_down,
          "final_norm_w": v_final_norm_w}
    order = list(weights)
    grads, deltas, new_m, new_v = {}, {}, {}, {}

    for name, g, rb in zip(("w_in", "w_out", "w_up", "w_down"), g_big, (256, 128, 256, 176)):
        shp = weights[name].shape
        w2, m2, v2 = (a.reshape(shp[1], shp[2]) for a in (weights[name], ms[name], vs[name]))
        g_, d_, nm_, nv_ = _adamw(w2, g, m2, v2, rb=rb, name=f"adamw_{name}")
        grads[name], deltas[name], new_m[name], new_v[name] = (a.reshape(shp) for a in (g_, d_, nm_, nv_))

    small_names = [n for n in order if n not in grads]
    as2d = lambda a: a.reshape(-1, a.shape[-1])
    res = _adamw_small([tuple(as2d(a) for a in (weights[n], g_small[n], ms[n], vs[n])) for n in small_names])
    for n, (d_, nm_, nv_) in zip(small_names, res):
        shp = weights[n].shape
        grads[n], deltas[n], new_m[n], new_v[n] = (a.reshape(shp) for a in (g_small[n], d_, nm_, nv_))

    return (loss, grad_x[None], *[grads[n] for n in order], *[deltas[n] for n in order],
            *[new_m[n] for n in order], *[new_v[n] for n in order])
```

```python
import jax
import jax.numpy as jnp
from jax import lax
from jax.experimental import pallas as pl
from jax.experimental.pallas import tpu as pltpu

F32 = jnp.float32
BF16 = jnp.bfloat16
MESH = pl.DeviceIdType.MESH

D = 1024
NH = 8
HD = 128
DFF = 2816
NMETA = 16
EPS = 1e-6
TR = 256
PAD = TR - NMETA
CH = 64
NSHARD = 4
VMEM_LIMIT = 62 * 1024 * 1024

ADAM_LR = 0.001
ADAM_B1 = 0.9
ADAM_B2 = 0.999
ADAM_EPS = 1e-08
ADAM_WD = 0.01
ADAM_STEP = 10


def _cparams(semantics=None, **kw):
    return pltpu.CompilerParams(dimension_semantics=semantics, vmem_limit_bytes=VMEM_LIMIT, **kw)


def _sigmoid(x):
    return 0.5 * jnp.tanh(0.5 * x) + 0.5


def _weight_grad(at, b, *, bn, bk, name, shard_cols=None):
    M, T = at.shape
    N = b.shape[1]
    assert b.shape[0] == T and T % bk == 0 and N % bn == 0, (name, M, N, T, bn, bk)
    nk = T // bk
    if shard_cols is None:
        nsh = 1
        blk = (M, bn)
        out_shape = jax.ShapeDtypeStruct((M, N), BF16)
        out_spec = pl.BlockSpec(blk, lambda j, k: (0, j))
    else:
        assert bn % shard_cols == 0 and N % shard_cols == 0
        nsh = bn // shard_cols
        blk = (nsh, M, shard_cols)
        out_shape = jax.ShapeDtypeStruct((N // shard_cols, M, shard_cols), BF16)
        out_spec = pl.BlockSpec(blk, lambda j, k: (j, 0, 0))

    def body(a_ref, b_ref, o_ref, acc_ref):
        k = pl.program_id(1)

        @pl.when(k == 0)
        def _():
            acc_ref[...] = jnp.zeros_like(acc_ref)

        p = jnp.dot(a_ref[...].astype(BF16), b_ref[...].astype(BF16), preferred_element_type=F32)
        if shard_cols is None:
            acc_ref[...] += p
        else:
            for q in range(nsh):
                acc_ref[q] += p[:, q * shard_cols:(q + 1) * shard_cols]

        @pl.when(k == nk - 1)
        def _():
            o_ref[...] = acc_ref[...].astype(BF16)

    return pl.pallas_call(
        body, name=name, grid=(N // bn, nk),
        in_specs=[pl.BlockSpec((M, bk), lambda j, k: (0, k)), pl.BlockSpec((bk, bn), lambda j, k: (k, j))],
        out_specs=out_spec, out_shape=out_shape, scratch_shapes=[pltpu.VMEM(blk, F32)],
        compiler_params=_cparams(("parallel", "arbitrary")),
    )(at, b)


def _input_grad_block(dproj, x_seq, head_tile, dres, w, w_in_g, psums):
    T = TR + x_seq.shape[0]
    nt = T // TR
    nsl = w_in_g.shape[0]
    wsl = w_in_g.shape[2]
    wcol = wsl // 2
    nex = len(psums)

    def body(*refs):
        dp_ref, x_ref, head_ref, dres_ref, w_ref, win_ref = refs[:6]
        gx_ref, dmeta_ref, dw_ref = refs[6 + nex:9 + nex]
        exchange = _ChipExchange(refs[6:6 + nex], refs[9 + nex:9 + 2 * nex], refs[-2], refs[-1])
        i = pl.program_id(0)

        @pl.when(i == 0)
        def _():
            exchange.send()

        @pl.when(i == nt - 1)
        def _():
            exchange.finish()

        x = jnp.where(i == 0, head_ref[...], x_ref[...])
        r = lax.rsqrt(jnp.mean(x * x, axis=-1, keepdims=True) + EPS)
        n = x * r
        du_v = None
        for s in range(nsl):
            part = _dot_nt(dp_ref[:, s * wsl:(s + 1) * wsl], win_ref[s])
            du_v = part if du_v is None else du_v + part
        dn = du_v * w_ref[...]
        dh = dres_ref[...] + r * (dn - n * jnp.mean(dn * n, axis=-1, keepdims=True))
        gx_ref[...] = dh
        part = jnp.sum(du_v * n, axis=0, keepdims=True)

        @pl.when(i == 0)
        def _():
            dmeta_ref[...] = dh[PAD:TR, :]
            dw_ref[...] = part

        @pl.when(i > 0)
        def _():
            dw_ref[...] += part

    row = pl.BlockSpec((TR, D), lambda i: (i, 0))
    vec = pl.BlockSpec((1, D), lambda i: (0, 0))
    hbm = pl.BlockSpec(memory_space=pl.ANY)
    return pl.pallas_call(
        body, name="input_grad_block", grid=(nt,),
        in_specs=[pl.BlockSpec((TR, 9 * D), lambda i: (i, 0)),
                  pl.BlockSpec((TR, D), lambda i: (jnp.maximum(i - 1, 0), 0)), pl.BlockSpec((TR, D), lambda i: (0, 0)),
                  row, vec, pl.BlockSpec(memory_space=pltpu.VMEM)] + [hbm] * nex,
        out_specs=[pl.BlockSpec((TR, D), lambda i: (jnp.maximum(i - 1, 0), 0)),
                   pl.BlockSpec((NMETA, D), lambda i: (0, 0)), vec] + [hbm] * nex,
        out_shape=[jax.ShapeDtypeStruct((T - TR, D), F32), jax.ShapeDtypeStruct((NMETA, D), F32),
                   jax.ShapeDtypeStruct((1, D), F32)]
        + [jax.ShapeDtypeStruct((3,) + p.shape[1:], p.dtype) for p in psums],
        scratch_shapes=[pltpu.SemaphoreType.DMA((nex, 3)), pltpu.SemaphoreType.DMA((nex, 3))],
        compiler_params=_cparams(("arbitrary",)),
    )(dproj, x_seq, head_tile, dres, w, w_in_g, *psums)


def _tri_matmul(tri_bf16, x):
    hi = x.astype(BF16)
    lo = (x - hi.astype(F32)).astype(BF16)
    return jnp.dot(tri_bf16, lo, preferred_element_type=F32) + jnp.dot(tri_bf16, hi, preferred_element_type=F32)


def _shift_down(x, prev8, n):
    rows = x.shape[0]
    return pltpu.roll(jnp.concatenate([prev8, x], axis=0), n, 0)[8:8 + rows, :]


def _shift_up(x, next8, n):
    rows = x.shape[0]
    return pltpu.roll(jnp.concatenate([x, next8], axis=0), rows + 8 - n, 0)[0:rows, :]


def _gates(f_raw, lb):
    sg = _sigmoid(f_raw)
    f = lb + (1.0 - lb) * sg
    return sg, f, jnp.log(f), 1.0 - f


def _lower_bound(lbp_ref):
    return _sigmoid(lbp_ref[0:1, :] - lbp_ref[1:2, :])


def _tri_masks():
    r = lax.broadcasted_iota(jnp.int32, (CH, CH), 0)
    c = lax.broadcasted_iota(jnp.int32, (CH, CH), 1)
    return r >= c, r <= c


def _ones_where(mask):
    return jnp.where(mask, 1.0, 0.0).astype(BF16)


def _dot(a, b):
    return jnp.dot(a.astype(BF16), b.astype(BF16), preferred_element_type=F32)


def _dot_nt(a, b):
    return lax.dot_general(a.astype(BF16), b.astype(BF16), (((1,), (1,)), ((), ())), preferred_element_type=F32)


def _dot_tn(a, b):
    return lax.dot_general(a.astype(BF16), b.astype(BF16), (((0,), (0,)), ((), ())), preferred_element_type=F32)


def _mix_block_fwd(x_seq, head_tile, lb_param, wattn, whn, conv_w, w_in_g, w_out, late, late_paired):
    T = TR + x_seq.shape[0]
    nt = T // TR
    ncht = TR // CH
    nsl = w_in_g.shape[0]
    wsl = w_in_g.shape[2]
    wcol = wsl // 2

    nlate = len(late)

    def body(*refs):
        x_ref, head_ref, lbp_ref, wa_ref, whn_ref, cw_ref, win_ref, wout_ref = refs[:8]
        ut_ref, projb_ref, o_ref, sst_ref, mt_ref, h1_ref = refs[8 + nlate:14 + nlate]
        late_refs = refs[14 + nlate:14 + 2 * nlate]
        proj_ref, m_ref, st_ref, cxc_ref, send_sems, recv_sems = refs[14 + 2 * nlate:]
        i = pl.program_id(0)
        gather = _Gather(late_refs, send_sems, recv_sems, (False,) * nlate, late_paired)

        @pl.when(i == 0)
        def _():
            st_ref[...] = jnp.zeros_like(st_ref)
            cxc_ref[...] = jnp.zeros_like(cxc_ref)
            gather.send()

        @pl.when(i == nt // 2)
        def _():
            gather.relay()

        x = jnp.where(i == 0, head_ref[...], x_ref[...])
        r1 = lax.rsqrt(jnp.mean(x * x, axis=-1, keepdims=True) + EPS)
        u_f = x * r1 * wa_ref[...]
        u = u_f.astype(BF16)
        ut_ref[...] = u_f.T.astype(BF16)
        for s in range(nsl):
            cols = slice(s * wsl, (s + 1) * wsl)
            p = jnp.dot(u, win_ref[s], preferred_element_type=F32)
            proj_ref[:, cols] = p
            projb_ref[:, cols] = p.astype(BF16)

        lb = _lower_bound(lbp_ref)
        lower, _ = _tri_masks()
        ltri = _ones_where(lower)
        whn_v = whn_ref[...]
        w0, w1, w2 = cw_ref[0:1, :], cw_ref[1:2, :], cw_ref[2:3, :]

        def chunk(c, carry):
            rows = pl.ds(pl.multiple_of(c * CH, CH), CH)
            q_raw = proj_ref[rows, 0:D]
            f_raw = proj_ref[rows, D:2 * D]
            v = proj_ref[rows, 2 * D:3 * D]
            q = q_raw * _sigmoid(q_raw)
            _, _, g, k = _gates(f_raw, lb)
            gam = _tri_matmul(ltri, g)
            gam_l = gam[CH - 1:CH, :]
            e_l = jnp.exp(gam_l)
            qt = (q * jnp.exp(gam)).astype(BF16)
            kt = (k * jnp.exp(-gam)).astype(BF16)
            khat = (k * jnp.exp(gam_l - gam)).astype(BF16)
            vb = v.astype(BF16)
            on_parts = []
            for h in range(NH):
                cs = slice(h * HD, (h + 1) * HD)
                st = st_ref[h]
                sst_ref[c, h] = st
                a = jnp.where(lower, _dot_nt(qt[:, cs], kt[:, cs]), 0.0)
                o_h = _dot_nt(qt[:, cs], st) + _dot(a, vb[:, cs])
                st_ref[h] = st * e_l[:, cs] + _dot_tn(vb[:, cs], khat[:, cs])
                o_ref[rows, cs] = o_h
                ro = lax.rsqrt(jnp.mean(o_h * o_h, axis=-1, keepdims=True) + EPS)
                on_parts.append(o_h * ro)
            on = jnp.concatenate(on_parts, axis=1)
            g_out = proj_ref[rows, 3 * D:4 * D]
            y_a = on * whn_v * (g_out * _sigmoid(g_out))
            cx = proj_ref[rows, 5 * D:6 * D] * proj_ref[rows, 6 * D:7 * D]
            prev8 = cxc_ref[...]
            cv = w0 * _shift_down(cx, prev8, 2) + w1 * _shift_down(cx, prev8, 1) + w2 * cx
            cxc_ref[...] = cx[CH - 8:CH, :]
            y_b = proj_ref[rows, 4 * D:5 * D] * cv
            m = _sigmoid(proj_ref[rows, 7 * D:8 * D]) * y_a + _sigmoid(proj_ref[rows, 8 * D:9 * D]) * y_b
            m_ref[rows, :] = m
            return carry

        lax.fori_loop(0, ncht, chunk, 0)
        m_v = m_ref[...]
        h1_ref[...] = x + jnp.dot(m_v.astype(BF16), wout_ref[...], preferred_element_type=F32)
        mt_ref[...] = m_v.T.astype(BF16)

        @pl.when(i == nt - 1)
        def _():
            gather.finish()

    row = lambda w: pl.BlockSpec((TR, w), lambda i: (i, 0))
    col = lambda w: pl.BlockSpec((w, TR), lambda i: (0, i))
    vec = lambda r: pl.BlockSpec((r, D), lambda i: (0, 0))
    vm = pl.BlockSpec(memory_space=pltpu.VMEM)
    hbm = pl.BlockSpec(memory_space=pl.ANY)
    return pl.pallas_call(
        body, name="mix_block_fwd", grid=(nt,),
        in_specs=[pl.BlockSpec((TR, D), lambda i: (jnp.maximum(i - 1, 0), 0)), pl.BlockSpec((TR, D), lambda i: (0, 0)),
                  vec(2), vec(1), vec(1), vec(3), vm, vm] + [hbm] * nlate,
        out_specs=[col(D), row(9 * D), row(D), pl.BlockSpec((ncht, NH, HD, HD), lambda i: (i, 0, 0, 0)),
                   col(D), row(D)] + [hbm] * nlate,
        out_shape=[jax.ShapeDtypeStruct((D, T), BF16), jax.ShapeDtypeStruct((T, 9 * D), BF16),
                   jax.ShapeDtypeStruct((T, D), F32), jax.ShapeDtypeStruct((T // CH, NH, HD, HD), F32),
                   jax.ShapeDtypeStruct((D, T), BF16), jax.ShapeDtypeStruct((T, D), F32)]
        + [jax.ShapeDtypeStruct(a.shape, a.dtype) for a in late],
        input_output_aliases={8 + n: 6 + n for n in range(nlate)},
        scratch_shapes=[pltpu.VMEM((TR, 9 * D), F32), pltpu.VMEM((TR, D), F32), pltpu.VMEM((NH, HD, HD), F32),
                        pltpu.VMEM((8, D), F32), pltpu.SemaphoreType.DMA((nlate, 6)),
                        pltpu.SemaphoreType.DMA((nlate, 6))],
        compiler_params=_cparams(("arbitrary",)),
    )(x_seq, head_tile, lb_param, wattn, whn, conv_w, w_in_g, w_out, *late)


def _mix_block_bwd(dh1, projb, o, sst, lb_param, whn, conv_w, w_out, psums):
    T = projb.shape[0]
    nt = T // TR
    ncht = TR // CH
    tb16 = TR // 16
    nex = len(psums)

    def body(*refs):
        dh1_ref, proj_ref, pc_ref, px_ref, o_ref, sst_ref, lbp_ref, whn_ref, cw_ref, wout_ref = refs[:10]
        dproj_ref, dlb_ref, dwhn_ref, dcw_ref = refs[10 + nex:14 + nex]
        exchange = _ChipExchange(refs[10:10 + nex], refs[14 + nex:14 + 2 * nex], refs[-2], refs[-1])
        dm_ref, dst_ref, dcvc_ref, acc_lb, acc_hn, acc_cw = refs[14 + 2 * nex:-2]
        s = pl.program_id(0)
        tile = nt - 1 - s

        @pl.when(s == 0)
        def _():
            dst_ref[...] = jnp.zeros_like(dst_ref)
            dcvc_ref[...] = jnp.zeros_like(dcvc_ref)
            acc_lb[...] = jnp.zeros_like(acc_lb)
            acc_hn[...] = jnp.zeros_like(acc_hn)
            acc_cw[...] = jnp.zeros_like(acc_cw)
            exchange.send()

        dm_ref[...] = _dot_nt(dh1_ref[...], wout_ref[...])
        lb = _lower_bound(lbp_ref)
        lower, upper = _tri_masks()
        ltri = _ones_where(lower)
        utri = _ones_where(upper)
        whn_v = whn_ref[...]
        w0, w1, w2 = cw_ref[0:1, :], cw_ref[1:2, :], cw_ref[2:3, :]
        cx_before_tile = jnp.where(tile > 0, (pc_ref[...].astype(F32) * px_ref[...].astype(F32))[8:16, :], 0.0)
        rid = lax.broadcasted_iota(jnp.int32, (CH, D), 0)

        def chunk(cc, carry):
            c = ncht - 1 - cc
            r0 = pl.multiple_of(c * CH, CH)
            rows = pl.ds(r0, CH)
            slab = lambda n: proj_ref[rows, n * D:(n + 1) * D].astype(F32)
            q_raw, f_raw, v, g_out, b_gate, c_gate, x_conv = (slab(n) for n in range(7))
            sa = _sigmoid(slab(7))
            sb = _sigmoid(slab(8))
            dm_v = dm_ref[rows, :]

            sq = _sigmoid(q_raw)
            q = q_raw * sq
            sg, f, g, k = _gates(f_raw, lb)
            gam = _tri_matmul(ltri, g)
            gam_l = gam[CH - 1:CH, :]
            e_l = jnp.exp(gam_l)
            e_g = jnp.exp(gam)
            e_ng = jnp.exp(-gam)
            e_kl = jnp.exp(gam_l - gam)
            qt = q * e_g
            kt = k * e_ng
            khat = k * e_kl
            qt_b, kt_b, khat_b, v_b = qt.astype(BF16), kt.astype(BF16), khat.astype(BF16), v.astype(BF16)
            qt_seen, kt_seen = qt_b.astype(F32), kt_b.astype(F32)
            s_go = _sigmoid(g_out)
            silu_go = g_out * s_go
            cx = c_gate * x_conv
            rprev = pl.ds(pl.multiple_of(jnp.maximum(r0 - 16, 0), 16), 16)
            cx_prev_in = (proj_ref[rprev, 5 * D:6 * D].astype(F32) * proj_ref[rprev, 6 * D:7 * D].astype(F32))[8:16, :]
            prev8 = jnp.where(c > 0, cx_prev_in, cx_before_tile)
            cx_m1 = _shift_down(cx, prev8, 1)
            cx_m2 = _shift_down(cx, prev8, 2)
            cv = w0 * cx_m2 + w1 * cx_m1 + w2 * cx
            y_b = b_gate * cv

            o_v = o_ref[rows, :]
            ro_parts, on_parts = [], []
            for h in range(NH):
                cs = slice(h * HD, (h + 1) * HD)
                o_h = o_v[:, cs]
                ro = lax.rsqrt(jnp.mean(o_h * o_h, axis=-1, keepdims=True) + EPS)
                ro_parts.append(ro)
                on_parts.append(o_h * ro)
            on = jnp.concatenate(on_parts, axis=1)
            y_a = on * whn_v * silu_go

            dy_a = dm_v * sa
            dy_b = dm_v * sb
            dproj_ref[rows, 7 * D:8 * D] = (dm_v * y_a * sa * (1.0 - sa)).astype(BF16)
            dproj_ref[rows, 8 * D:9 * D] = (dm_v * y_b * sb * (1.0 - sb)).astype(BF16)
            dproj_ref[rows, 4 * D:5 * D] = (dy_b * cv).astype(BF16)
            dcv = dy_b * b_gate
            acc_cw[0:1, :] += jnp.sum(dcv * cx_m2, axis=0, keepdims=True)
            acc_cw[1:2, :] += jnp.sum(dcv * cx_m1, axis=0, keepdims=True)
            acc_cw[2:3, :] += jnp.sum(dcv * cx, axis=0, keepdims=True)
            next8 = dcvc_ref[...]
            dcx = w2 * dcv + w1 * _shift_up(dcv, next8, 1) + w0 * _shift_up(dcv, next8, 2)
            dcvc_ref[...] = dcv[0:8, :]
            dproj_ref[rows, 5 * D:6 * D] = (dcx * x_conv).astype(BF16)
            dproj_ref[rows, 6 * D:7 * D] = (dcx * c_gate).astype(BF16)
            don = dy_a * whn_v * silu_go
            dproj_ref[rows, 3 * D:4 * D] = (dy_a * on * whn_v * (s_go * (1.0 + g_out * (1.0 - s_go)))).astype(BF16)
            acc_hn[...] += jnp.sum(dy_a * silu_go * on, axis=0, keepdims=True)

            dq_parts, dk_parts, dv_parts, dgam_parts, ext_parts = [], [], [], [], []
            for h in range(NH):
                cs = slice(h * HD, (h + 1) * HD)
                on_h = on_parts[h]
                don_h = don[:, cs]
                do_h = ro_parts[h] * (don_h - on_h * jnp.mean(don_h * on_h, axis=-1, keepdims=True))
                qt_h, kt_h, khat_h, v_h = qt_b[:, cs], kt_b[:, cs], khat_b[:, cs], v_b[:, cs]
                do_b = do_h.astype(BF16)
                st = sst_ref[c, h]
                dstn = dst_ref[h]
                dstn_b = dstn.astype(BF16)
                a_t = jnp.where(upper, _dot_nt(kt_h, qt_h), 0.0)
                da = jnp.where(lower, _dot_nt(do_b, v_h), 0.0)
                da_t = jnp.where(upper, _dot_nt(v_h, do_b), 0.0)
                dv_h = _dot(a_t, do_b) + _dot_nt(khat_h, dstn_b)
                dqt_state = _dot(do_b, st)
                dqt_chunk = _dot(da, kt_h)
                dkt = _dot(da_t, qt_h)
                dkhat = _dot(v_h, dstn_b)
                dq_h = (dqt_state + dqt_chunk) * e_g[:, cs]
                dk_h = dkt * e_ng[:, cs] + dkhat * e_kl[:, cs]
                khat_dkhat = dkhat * khat[:, cs]
                ext = (jnp.sum(khat_dkhat, axis=0, keepdims=True)
                       + e_l[:, cs] * jnp.sum(st * dstn, axis=0, keepdims=True))
                dst_ref[h] = _dot_tn(do_b, qt_h) + dstn * e_l[:, cs]
                dq_parts.append(dq_h)
                dk_parts.append(dk_h)
                dv_parts.append(dv_h)
                dgam_parts.append(qt[:, cs] * dqt_state + qt_seen[:, cs] * dqt_chunk - kt_seen[:, cs] * dkt
                                  - khat_dkhat)
                ext_parts.append(ext)
            dq = jnp.concatenate(dq_parts, axis=1)
            dk = jnp.concatenate(dk_parts, axis=1)
            dgam = jnp.concatenate(dgam_parts, axis=1)
            ext = jnp.concatenate(ext_parts, axis=1)
            dgam = dgam + jnp.where(rid == CH - 1, ext, 0.0)
            dg = _tri_matmul(utri, dgam)
            dproj_ref[rows, 0:D] = (dq * (sq * (1.0 + q_raw * (1.0 - sq)))).astype(BF16)
            df = dg * jnp.exp(-g) - dk
            dproj_ref[rows, D:2 * D] = (df * (1.0 - lb) * sg * (1.0 - sg)).astype(BF16)
            dproj_ref[rows, 2 * D:3 * D] = jnp.concatenate(dv_parts, axis=1).astype(BF16)
            real = (tile * TR + r0 + rid) >= PAD
            acc_lb[...] += jnp.sum(jnp.where(real, df * (1.0 - sg), 0.0), axis=0, keepdims=True)
            return carry

        lax.fori_loop(0, ncht, chunk, 0)

        @pl.when(s == nt - 1)
        def _():
            dlb_ref[...] = acc_lb[...] * lb * (1.0 - lb)
            hn = acc_hn[...]
            tot = hn[:, 0:HD]
            for h in range(1, NH):
                tot = tot + hn[:, h * HD:(h + 1) * HD]
            dwhn_ref[...] = tot
            dcw_ref[...] = acc_cw[0:3, :]
            exchange.finish()

    hbm = pl.BlockSpec(memory_space=pl.ANY)
    rev = lambda s: (nt - 1 - s, 0)
    prevc = lambda s: (jnp.maximum((nt - 1 - s) * tb16 - 1, 0), 5)
    prevx = lambda s: (jnp.maximum((nt - 1 - s) * tb16 - 1, 0), 6)
    const = lambda s: (0, 0)
    return pl.pallas_call(
        body, name="mix_block_bwd", grid=(nt,),
        in_specs=[pl.BlockSpec((TR, D), rev),
                  pl.BlockSpec((TR, 9 * D), rev),
                  pl.BlockSpec((16, D), prevc),
                  pl.BlockSpec((16, D), prevx),
                  pl.BlockSpec((TR, D), rev),
                  pl.BlockSpec((ncht, NH, HD, HD), lambda s: (nt - 1 - s, 0, 0, 0)),
                  pl.BlockSpec((2, D), const),
                  pl.BlockSpec((1, D), const),
                  pl.BlockSpec((3, D), const),
                  pl.BlockSpec(memory_space=pltpu.VMEM)] + [hbm] * nex,
        out_specs=[pl.BlockSpec((TR, 9 * D), rev),
                   pl.BlockSpec((1, D), const),
                   pl.BlockSpec((1, HD), const),
                   pl.BlockSpec((3, D), const)] + [hbm] * nex,
        out_shape=[jax.ShapeDtypeStruct((T, 9 * D), BF16), jax.ShapeDtypeStruct((1, D), F32),
                   jax.ShapeDtypeStruct((1, HD), F32), jax.ShapeDtypeStruct((3, D), F32)]
        + [jax.ShapeDtypeStruct((3,) + p.shape[1:], p.dtype) for p in psums],
        scratch_shapes=[pltpu.VMEM((TR, D), F32), pltpu.VMEM((NH, HD, HD), F32), pltpu.VMEM((8, D), F32),
                        pltpu.VMEM((1, D), F32), pltpu.VMEM((1, D), F32), pltpu.VMEM((8, D), F32),
                        pltpu.SemaphoreType.DMA((nex, 3)), pltpu.SemaphoreType.DMA((nex, 3))],
        compiler_params=_cparams(("arbitrary",)),
    )(dh1, projb, projb, projb, o, sst, lb_param, whn, conv_w, w_out, *psums)


def _ffn_block_fwd(h1, tgt, wffn, w_up_g, fcw, fcb, w_down, wfin):
    T = h1.shape[0]
    nt = T // TR
    nsl = w_up_g.shape[0]
    wsl = w_up_g.shape[2]

    def body(h_ref, t_ref, wn_ref, wup_ref, cw_ref, cb_ref, wdn_ref, wf_ref,
             u2t_ref, upb_ref, ggt_ref, dh_ref, loss_ref, dwf_ref, up_scr, gg_ref, carry_ref):
        i = pl.program_id(0)

        @pl.when(i == 0)
        def _():
            carry_ref[...] = jnp.zeros_like(carry_ref)
            loss_ref[...] = jnp.zeros_like(loss_ref)
            dwf_ref[...] = jnp.zeros_like(dwf_ref)

        x = h_ref[...]
        r2 = lax.rsqrt(jnp.mean(x * x, axis=-1, keepdims=True) + EPS)
        u2_f = x * r2 * wn_ref[...]
        u2 = u2_f.astype(BF16)
        u2t_ref[...] = u2_f.T.astype(BF16)
        for s in range(nsl):
            up_s = jnp.dot(u2, wup_ref[s], preferred_element_type=F32)
            up_scr[:, s * wsl:(s + 1) * wsl] = up_s
            upb_ref[:, s * wsl:(s + 1) * wsl] = up_s.astype(BF16)
        w0, w1, w2 = cw_ref[0:1, :], cw_ref[1:2, :], cw_ref[2:3, :]

        def chunk(c, carry):
            rows = pl.ds(pl.multiple_of(c * CH, CH), CH)
            a_pre = up_scr[rows, 0:DFF]
            val = up_scr[rows, DFF:2 * DFF]
            prev8 = carry_ref[...]
            a = w0 * _shift_down(a_pre, prev8, 2) + w1 * _shift_down(a_pre, prev8, 1) + w2 * a_pre + cb_ref[...]
            carry_ref[...] = a_pre[CH - 8:CH, :]
            gg_ref[rows, :] = a * _sigmoid(a) * val
            return carry

        lax.fori_loop(0, TR // CH, chunk, 0)
        gg_v = gg_ref[...]
        ggt_ref[...] = gg_v.T.astype(BF16)
        h2 = x + jnp.dot(gg_v.astype(BF16), wdn_ref[...], preferred_element_type=F32)
        r3 = lax.rsqrt(jnp.mean(h2 * h2, axis=-1, keepdims=True) + EPS)
        n3 = h2 * r3
        wf = wf_ref[...]
        diff = jnp.where(i > 0, n3 * wf - t_ref[...], 0.0)
        loss_ref[...] += jnp.sum(diff * diff, axis=0, keepdims=True) * (0.5 / D)
        dy = diff * (1.0 / D)
        dwf_ref[...] += jnp.sum(dy * n3, axis=0, keepdims=True)
        dn = dy * wf
        dh_ref[...] = r3 * (dn - n3 * jnp.mean(dn * n3, axis=-1, keepdims=True))

    row = lambda w: pl.BlockSpec((TR, w), lambda i: (i, 0))
    col = lambda w: pl.BlockSpec((w, TR), lambda i: (0, i))
    vec = lambda w, r=1: pl.BlockSpec((r, w), lambda i: (0, 0))
    vm = pl.BlockSpec(memory_space=pltpu.VMEM)
    return pl.pallas_call(
        body, name="ffn_block_fwd", grid=(nt,),
        in_specs=[row(D), pl.BlockSpec((TR, D), lambda i: (jnp.maximum(i - 1, 0), 0)), vec(D), vm,
                  vec(DFF, 3), vec(DFF), vm, vec(D)],
        out_specs=[col(D), row(2 * DFF), col(DFF), row(D), vec(D), vec(D)],
        out_shape=[jax.ShapeDtypeStruct((D, T), BF16), jax.ShapeDtypeStruct((T, 2 * DFF), BF16),
                   jax.ShapeDtypeStruct((DFF, T), BF16), jax.ShapeDtypeStruct((T, D), F32),
                   jax.ShapeDtypeStruct((1, D), F32), jax.ShapeDtypeStruct((1, D), F32)],
        scratch_shapes=[pltpu.VMEM((TR, 2 * DFF), F32), pltpu.VMEM((TR, DFF), F32), pltpu.VMEM((8, DFF), F32)],
        compiler_params=_cparams(("arbitrary",)),
    )(h1, tgt, wffn, w_up_g, fcw, fcb, w_down, wfin)


def _ffn_block_bwd(dh2, upb, h1, wffn, w_up_g, fcw, fcb, w_down):
    T = h1.shape[0]
    nt = T // TR
    ncht = TR // CH
    nsl = w_up_g.shape[0]
    wsl = w_up_g.shape[2]
    tb16 = TR // 16
    assert ncht % nsl == 0
    every = ncht // nsl
    step = -(-DFF // (ncht * 128)) * 128
    parts = [(c0, min(c0 + step, DFF)) for c0 in range(0, DFF, step)]
    assert len(parts) == ncht

    def body(dh2n_ref, dh2p_ref, up_ref, pa_ref, h_ref, wn_ref, wup_ref, cw_ref, cb_ref, wdn_ref,
             dup_ref, dh1_ref, dfw_ref, dfb_ref, dwn_ref,
             dgg_ring, dup_ring, carry_ref, acc_w, acc_b, acc_n):
        s = pl.program_id(0)
        slot = lax.rem(s, 2)
        other = 1 - slot

        @pl.when(s == 0)
        def _():
            carry_ref[...] = jnp.zeros_like(carry_ref)
            acc_w[...] = jnp.zeros_like(acc_w)
            acc_b[...] = jnp.zeros_like(acc_b)
            acc_n[...] = jnp.zeros_like(acc_n)
            dup_ring[1] = jnp.zeros((TR, 2 * DFF), BF16)
            dgg_ring[0] = _dot_nt(dh2p_ref[...], wdn_ref[...])

        def norm_bwd(du2, valid):
            x = h_ref[...]
            dh2p = dh2p_ref[...]
            r2 = lax.rsqrt(jnp.mean(x * x, axis=-1, keepdims=True) + EPS)
            n2 = x * r2
            dn = du2 * wn_ref[...]
            dh1_ref[...] = dh2p + r2 * (dn - n2 * jnp.mean(dn * n2, axis=-1, keepdims=True))
            acc_n[...] += jnp.where(valid, jnp.sum(du2 * n2, axis=0, keepdims=True), 0.0)

        @pl.when(s < nt)
        def _():
            tile = nt - 1 - s
            w0, w1, w2 = cw_ref[0:1, :], cw_ref[1:2, :], cw_ref[2:3, :]
            a_before_tile = jnp.where(tile > 0, pa_ref[...].astype(F32)[8:16, :], 0.0)
            dh2n = dh2n_ref[...].astype(BF16)
            du2 = None
            for idx in range(ncht):
                c = ncht - 1 - idx
                r0 = c * CH
                rows = slice(r0, r0 + CH)
                a_pre = up_ref[rows, 0:DFF].astype(F32)
                val = up_ref[rows, DFF:2 * DFF].astype(F32)
                prev8 = up_ref[r0 - 16:r0, 0:DFF].astype(F32)[8:16, :] if c > 0 else a_before_tile
                a_m1 = _shift_down(a_pre, prev8, 1)
                a_m2 = _shift_down(a_pre, prev8, 2)
                a = w0 * a_m2 + w1 * a_m1 + w2 * a_pre + cb_ref[...]
                sig = _sigmoid(a)
                dgg_v = dgg_ring[slot, rows, :]
                da = dgg_v * val * (sig * (1.0 + a * (1.0 - sig)))
                dval = (dgg_v * (a * sig)).astype(BF16)
                next8 = carry_ref[...]
                da_pre = (w2 * da + w1 * _shift_up(da, next8, 1) + w0 * _shift_up(da, next8, 2)).astype(BF16)
                carry_ref[...] = da[0:8, :]
                dup_ref[rows, 0:DFF] = da_pre
                dup_ref[rows, DFF:2 * DFF] = dval
                dup_ring[slot, rows, 0:DFF] = da_pre
                dup_ring[slot, rows, DFF:2 * DFF] = dval
                acc_w[0:1, :] += jnp.sum(da * a_m2, axis=0, keepdims=True)
                acc_w[1:2, :] += jnp.sum(da * a_m1, axis=0, keepdims=True)
                acc_w[2:3, :] += jnp.sum(da * a_pre, axis=0, keepdims=True)
                acc_b[...] += jnp.sum(da, axis=0, keepdims=True)
                c0, c1 = parts[idx]
                dgg_ring[other, :, c0:c1] = _dot_nt(dh2n, wdn_ref[c0:c1, :])
                if idx % every == 0:
                    sl = idx // every
                    part = _dot_nt(dup_ring[other, :, sl * wsl:(sl + 1) * wsl], wup_ref[sl])
                    du2 = part if du2 is None else du2 + part
            norm_bwd(du2, s > 0)

        @pl.when(s == nt)
        def _():
            du2 = _dot_nt(dup_ring[other, :, 0:wsl], wup_ref[0])
            for sl in range(1, nsl):
                du2 = du2 + _dot_nt(dup_ring[other, :, sl * wsl:(sl + 1) * wsl], wup_ref[sl])
            norm_bwd(du2, True)
            dfw_ref[...] = acc_w[0:3, :]
            dfb_ref[...] = acc_b[...]
            dwn_ref[...] = acc_n[...]

    gate_tile = lambda s: jnp.maximum(nt - 1 - s, 0)
    next_tile = lambda s: jnp.maximum(nt - 2 - s, 0)
    prev_tile = lambda s: jnp.minimum(nt - s, nt - 1)
    vec = lambda w, r=1: pl.BlockSpec((r, w), lambda s: (0, 0))
    vm = pl.BlockSpec(memory_space=pltpu.VMEM)
    return pl.pallas_call(
        body, name="ffn_block_bwd", grid=(nt + 1,),
        in_specs=[pl.BlockSpec((TR, D), lambda s: (next_tile(s), 0)),
                  pl.BlockSpec((TR, D), lambda s: (prev_tile(s), 0)),
                  pl.BlockSpec((TR, 2 * DFF), lambda s: (gate_tile(s), 0)),
                  pl.BlockSpec((16, DFF), lambda s: (jnp.maximum(gate_tile(s) * tb16 - 1, 0), 0)),
                  pl.BlockSpec((TR, D), lambda s: (prev_tile(s), 0)),
                  vec(D), vm, vec(DFF, 3), vec(DFF), vm],
        out_specs=[pl.BlockSpec((TR, 2 * DFF), lambda s: (gate_tile(s), 0)),
                   pl.BlockSpec((TR, D), lambda s: (prev_tile(s), 0)),
                   vec(DFF, 3), vec(DFF), vec(D)],
        out_shape=[jax.ShapeDtypeStruct((T, 2 * DFF), BF16), jax.ShapeDtypeStruct((T, D), F32),
                   jax.ShapeDtypeStruct((3, DFF), F32), jax.ShapeDtypeStruct((1, DFF), F32),
                   jax.ShapeDtypeStruct((1, D), F32)],
        scratch_shapes=[pltpu.VMEM((2, TR, DFF), F32), pltpu.VMEM((2, TR, 2 * DFF), BF16),
                        pltpu.VMEM((8, DFF), F32), pltpu.VMEM((8, DFF), F32), pltpu.VMEM((1, DFF), F32),
                        pltpu.VMEM((1, D), F32)],
        compiler_params=_cparams(("arbitrary",)),
    )(dh2, dh2, upb, upb, h1, wffn, w_up_g, fcw, fcb, w_down)


def _place():
    x, y, c = lax.axis_index("x"), lax.axis_index("y"), lax.axis_index("c")
    return x, y, c


_CHIP_FLIPS = ((1, 0), (0, 1), (1, 1))


def _flip(v, bit):
    return 1 - v if bit else v


def _into_slot(w, j_idx, *, rb, dtype, name, paired=False):
    r, cdim = w.shape

    def body(j_ref, w_ref, out_ref):
        del j_ref
        out_ref[...] = w_ref[...].astype(dtype)

    if paired:
        out_shape = jax.ShapeDtypeStruct((NSHARD // 2, r, 2 * cdim), dtype)
        out_spec = pl.BlockSpec((None, rb, cdim), lambda i, j_ref: (j_ref[0] // 2, i, j_ref[0] % 2))
    else:
        out_shape = jax.ShapeDtypeStruct((NSHARD, r, cdim), dtype)
        out_spec = pl.BlockSpec((None, rb, cdim), lambda i, j_ref: (j_ref[0], i, 0))
    grid_spec = pltpu.PrefetchScalarGridSpec(
        num_scalar_prefetch=1, grid=(r // rb,),
        in_specs=[pl.BlockSpec((rb, cdim), lambda i, j_ref: (i, 0))], out_specs=out_spec)
    return pl.pallas_call(
        body, name=name, grid_spec=grid_spec, out_shape=out_shape, compiler_params=_cparams(("parallel",)),
    )(j_idx, w)


class _Gather:
    def __init__(self, outs, send_sems, recv_sems, whole, paired=None):
        self.outs, self.send_sems, self.recv_sems, self.whole = outs, send_sems, recv_sems, whole
        self.paired = paired if paired is not None else (False,) * len(outs)
        self.x, self.y, self.c = _place()
        self.j = 2 * self.x + self.y
        self.sibling = (self.x, self.y, 1 - self.c)

    def _rows(self, w, core):
        r = self.outs[w].shape[1]
        return pl.ds(0, r) if self.whole[w] else pl.ds(core * (r // 2), r // 2)

    def _copy(self, w, slot, core, sem, to):
        if self.paired[w]:
            cw = self.outs[w].shape[2] // 2
            piece = self.outs[w].at[slot // 2, self._rows(w, core), pl.ds((slot % 2) * cw, cw)]
        else:
            piece = self.outs[w].at[slot, self._rows(w, core), :]
        return pltpu.make_async_remote_copy(
            src_ref=piece, dst_ref=piece, send_sem=self.send_sems.at[w, sem], recv_sem=self.recv_sems.at[w, sem],
            device_id=to, device_id_type=MESH)

    def _chips(self):
        for kk, (fx, fy) in enumerate(_CHIP_FLIPS):
            px, py = _flip(self.x, fx), _flip(self.y, fy)
            yield kk, 2 * px + py, (px, py, self.c)

    def send(self):
        for w in range(len(self.outs)):
            for kk, _, to in self._chips():
                self._copy(w, self.j, self.c, kk, to).start()

    def relay(self):
        for w in range(len(self.outs)):
            for kk, jk, _ in self._chips():
                self._copy(w, jk, self.c, kk, self.sibling).wait_recv()
                if not self.whole[w]:
                    self._copy(w, jk, self.c, 3 + kk, self.sibling).start()

    def finish(self):
        for w in range(len(self.outs)):
            for kk, jk, to in self._chips():
                self._copy(w, self.j, self.c, kk, to).wait_send()
                if not self.whole[w]:
                    self._copy(w, jk, 1 - self.c, 3 + kk, self.sibling).wait_recv()
                    self._copy(w, jk, self.c, 3 + kk, self.sibling).wait_send()


def _allgather_weights(slotted, whole):
    n = len(slotted)

    def body(*refs):
        g = _Gather(refs[n:2 * n], refs[2 * n], refs[2 * n + 1], whole)
        g.send()
        g.relay()
        g.finish()

    any_spec = pl.BlockSpec(memory_space=pl.ANY)
    return pl.pallas_call(
        body, name="allgather_weights",
        in_specs=[any_spec] * n, out_specs=[any_spec] * n,
        out_shape=[jax.ShapeDtypeStruct(a.shape, a.dtype) for a in slotted],
        input_output_aliases={i: i for i in range(n)},
        scratch_shapes=[pltpu.SemaphoreType.DMA((n, 6)), pltpu.SemaphoreType.DMA((n, 6))],
    )(*slotted)


class _ChipExchange:
    def __init__(self, ins, outs, send_sems, recv_sems):
        self.ins, self.outs, self.send_sems, self.recv_sems = ins, outs, send_sems, recv_sems
        self.x, self.y, self.c = _place()

    def _copies(self):
        for w in range(len(self.ins)):
            for kk, (fx, fy) in enumerate(_CHIP_FLIPS):
                px, py = _flip(self.x, fx), _flip(self.y, fy)
                yield pltpu.make_async_remote_copy(
                    src_ref=self.ins[w].at[2 * px + py], dst_ref=self.outs[w].at[kk],
                    send_sem=self.send_sems.at[w, kk], recv_sem=self.recv_sems.at[w, kk],
                    device_id=(px, py, self.c), device_id_type=MESH)

    def send(self):
        for cp in self._copies():
            cp.start()

    def finish(self):
        for cp in self._copies():
            cp.wait()


def _pair_exchange(grads, name):
    nw = len(grads)

    def body(*refs):
        ins, outs = refs[:nw], refs[nw:2 * nw]
        send_sems, recv_sems = refs[2 * nw:]
        x, y, c = _place()
        sibling = (x, y, 1 - c)
        cps = []
        for w in range(nw):
            half = ins[w].shape[1] // 2
            cp = pltpu.make_async_remote_copy(
                src_ref=ins[w].at[:, pl.ds((1 - c) * half, half), :], dst_ref=outs[w],
                send_sem=send_sems.at[w], recv_sem=recv_sems.at[w], device_id=sibling, device_id_type=MESH)
            cp.start()
            cps.append(cp)
        for cp in cps:
            cp.wait()

    any_spec = pl.BlockSpec(memory_space=pl.ANY)
    return pl.pallas_call(
        body, name=name,
        in_specs=[any_spec] * nw, out_specs=[any_spec] * nw,
        out_shape=[jax.ShapeDtypeStruct((g.shape[0], g.shape[1] // 2, g.shape[2]), g.dtype) for g in grads],
        scratch_shapes=[pltpu.SemaphoreType.DMA((nw,)), pltpu.SemaphoreType.DMA((nw,))],
    )(*grads)


def _pair_add(g, other, c_idx, *, rb, name):
    S, r, cdim = g.shape
    half = r // 2
    nb = half // rb

    def body(c_ref, g_ref, o_ref, out_ref):
        del c_ref
        out_ref[...] = (g_ref[...].astype(F32) + o_ref[...].astype(F32)).astype(BF16)

    grid_spec = pltpu.PrefetchScalarGridSpec(
        num_scalar_prefetch=1, grid=(S, nb),
        in_specs=[pl.BlockSpec((None, rb, cdim), lambda s, i, c_ref: (s, c_ref[0] * nb + i, 0)),
                  pl.BlockSpec((None, rb, cdim), lambda s, i, c_ref: (s, i, 0))],
        out_specs=pl.BlockSpec((None, rb, cdim), lambda s, i, c_ref: (s, i, 0)))
    return pl.pallas_call(
        body, name=name, grid_spec=grid_spec, out_shape=jax.ShapeDtypeStruct((S, half, cdim), BF16),
        compiler_params=_cparams(("parallel", "parallel")),
    )(c_idx, g, other)


def _chip_sum(psum, parts, cj_idx, *, rb, name):
    S, half, cdim = psum.shape
    nb = half // rb

    def body(cj_ref, own_ref, p_ref, out_ref):
        del cj_ref
        f = lambda v: v.astype(F32)
        out_ref[...] = ((f(own_ref[...]) + f(p_ref[0])) + f(p_ref[1])) + f(p_ref[2])

    grid_spec = pltpu.PrefetchScalarGridSpec(
        num_scalar_prefetch=1, grid=(nb,),
        in_specs=[pl.BlockSpec((None, rb, cdim), lambda i, cj: (cj[1], i, 0)),
                  pl.BlockSpec((3, rb, cdim), lambda i, cj: (0, i, 0))],
        out_specs=pl.BlockSpec((rb, cdim), lambda i, cj: (cj[0] * nb + i, 0)))
    return pl.pallas_call(
        body, name=name, grid_spec=grid_spec, out_shape=jax.ShapeDtypeStruct((2 * half, cdim), F32),
        compiler_params=_cparams(("parallel",)),
    )(cj_idx, psum, parts)


SLAB_W = 1024


def _final_exchange(pieces, grads):
    n, nw = len(pieces), len(grads)
    segs, at = [], 0
    for idx, p in enumerate(pieces):
        r, wd = p.shape
        for c0 in range(0, wd, SLAB_W):
            if r > 1:
                at = -(-at // 8) * 8
            segs.append((idx, c0, min(SLAB_W, wd - c0), at))
            at += r
    rows = -(-at // 8) * 8
    flips = [(fx, fy, fc) for fx in (0, 1) for fy in (0, 1) for fc in (0, 1)][1:]

    def body(*refs):
        ins = refs[:n]
        outs = refs[n + nw:2 * n + nw]
        g_refs = refs[2 * n + nw:2 * n + 2 * nw]
        mine_ref, slots_ref, send_sems, recv_sems, gsend_sems, grecv_sems = refs[2 * n + 2 * nw:]
        x, y, c = _place()
        me = 4 * x + 2 * y + c
        sibling = (x, y, 1 - c)

        def swap(w, core):
            half = g_refs[w].shape[0] // 2
            rows_ = g_refs[w].at[pl.ds(core * half, half), :]
            return pltpu.make_async_remote_copy(
                src_ref=rows_, dst_ref=rows_, send_sem=gsend_sems.at[w], recv_sem=grecv_sems.at[w],
                device_id=sibling, device_id_type=MESH)

        for w in range(nw):
            swap(w, c).start()
        mine_ref[...] = jnp.zeros_like(mine_ref)
        for idx, c0, wd, st in segs:
            r = ins[idx].shape[0]
            mine_ref[st:st + r, 0:wd] = ins[idx][:, c0:c0 + wd]
        slots_ref[me] = mine_ref[...]
        cps = []
        for kk, (fx, fy, fc) in enumerate(flips):
            cp = pltpu.make_async_remote_copy(
                src_ref=mine_ref, dst_ref=slots_ref.at[me], send_sem=send_sems.at[kk], recv_sem=recv_sems.at[kk],
                device_id=(_flip(x, fx), _flip(y, fy), _flip(c, fc)), device_id_type=MESH)
            cp.start()
            cps.append(cp)
        for cp in cps:
            cp.wait()
        tot = slots_ref[0]
        for d in range(1, 8):
            tot = tot + slots_ref[d]
        mine_ref[...] = tot
        for idx, c0, wd, st in segs:
            r = ins[idx].shape[0]
            val = mine_ref[st:st + r, 0:wd]
            if idx == n - 1:
                outs[idx][...] = jnp.sum(val, keepdims=True)
            else:
                outs[idx][:, c0:c0 + wd] = val
        for w in range(nw):
            swap(w, 1 - c).wait_recv()
            swap(w, c).wait_send()

    vm = pl.BlockSpec(memory_space=pltpu.VMEM)
    hbm = pl.BlockSpec(memory_space=pl.ANY)
    out_shape = ([jax.ShapeDtypeStruct(p.shape, F32) for p in pieces[:-1]] + [jax.ShapeDtypeStruct((1, 1), F32)]
                 + [jax.ShapeDtypeStruct(g.shape, g.dtype) for g in grads])
    res = pl.pallas_call(
        body, name="final_exchange", in_specs=[vm] * n + [hbm] * nw, out_specs=[vm] * n + [hbm] * nw,
        out_shape=out_shape, input_output_aliases={n + w: n + w for w in range(nw)},
        scratch_shapes=[pltpu.VMEM((rows, SLAB_W), F32), pltpu.VMEM((8, rows, SLAB_W), F32),
                        pltpu.SemaphoreType.DMA((7,)), pltpu.SemaphoreType.DMA((7,)),
                        pltpu.SemaphoreType.DMA((nw,)), pltpu.SemaphoreType.DMA((nw,))],
    )(*pieces, *grads)
    return res[:n], res[n:]


def _adamw(w, g, m, v, *, rb, name):
    r, cdim = w.shape

    def body(w_ref, g_ref, m_ref, v_ref, go_ref, d_ref, nm_ref, nv_ref):
        go_ref[...] = g_ref[...]
        d_ref[...], nm_ref[...], nv_ref[...] = _adamw_update(w_ref[...], g_ref[...], m_ref[...], v_ref[...])

    spec = pl.BlockSpec((rb, cdim), lambda i: (i, 0))
    shp = jax.ShapeDtypeStruct((r, cdim), F32)
    return pl.pallas_call(
        body, name=name, grid=(r // rb,), in_specs=[spec] * 4, out_specs=[spec] * 4, out_shape=[shp] * 4,
        compiler_params=_cparams(("parallel",)),
    )(w, g, m, v)


def _adamw_update(w, g, m, v):
    nm = ADAM_B1 * m + (1.0 - ADAM_B1) * g
    nv = ADAM_B2 * v + (1.0 - ADAM_B2) * (g * g)
    m_hat = nm / (1.0 - ADAM_B1 ** ADAM_STEP)
    v_hat = nv / (1.0 - ADAM_B2 ** ADAM_STEP)
    return -ADAM_LR * (m_hat / (jnp.sqrt(v_hat) + ADAM_EPS) + ADAM_WD * w), nm, nv


def _adamw_small(params):
    n = len(params)

    def body(*refs):
        ins, outs = refs[:4 * n], refs[4 * n:]
        for p in range(n):
            w_ref, g_ref, m_ref, v_ref = ins[4 * p:4 * p + 4]
            d, nm, nv = _adamw_update(w_ref[...], g_ref[...], m_ref[...], v_ref[...])
            outs[3 * p][...] = d
            outs[3 * p + 1][...] = nm
            outs[3 * p + 2][...] = nv

    vm = pl.BlockSpec(memory_space=pltpu.VMEM)
    flat = [a for p in params for a in p]
    out_shape = [jax.ShapeDtypeStruct(p[0].shape, F32) for p in params for _ in range(3)]
    res = pl.pallas_call(
        body, name="adamw_small", in_specs=[vm] * (4 * n), out_specs=[vm] * (3 * n), out_shape=out_shape,
    )(*flat)
    return [tuple(res[3 * p:3 * p + 3]) for p in range(n)]


_PAIR_ADD_ROWS = {"w_in": 256, "w_out": 128, "w_up": 256, "w_down": 176}


def _pair_sums(grads, names, c_idx, tag):
    others = _pair_exchange(grads, name=f"grad_pair_exchange_{tag}")
    return [_pair_add(g, o_, c_idx, rb=_PAIR_ADD_ROWS[n], name=f"pair_add_{n}") for g, o_, n in zip(grads, others, names)]


def _local_step(x, tgt, meta_full, lb_param, attn_norm_w, w_in_g, hgrn_norm_w, conv_w_full, w_out_full,
                ffn_norm_w, late_slotted, fcw_full, ffn_conv_b, final_norm_w, c_idx, cj_idx):
    seq = x.shape[0]
    T = TR + seq
    head_tile = jnp.concatenate([jnp.zeros((PAD, D), F32), meta_full], axis=0)
    whn_t = jnp.tile(hgrn_norm_w, (1, NH))

    ut, projb, o, sst, mt, h1, w_up_g, w_down_g = _mix_block_fwd(
        x, head_tile, lb_param, attn_norm_w, whn_t, conv_w_full, w_in_g, w_out_full, late_slotted, (True, False))
    w_down_full = w_down_g.reshape(DFF, D)
    u2t, upb, ggt, dh2, loss_vec, dwfin = _ffn_block_fwd(
        h1, tgt, ffn_norm_w, w_up_g, fcw_full, ffn_conv_b, w_down_full, final_norm_w.reshape(1, D))

    dup, dh1, dfw, dfb, dwffn = _ffn_block_bwd(
        dh2, upb, h1, ffn_norm_w, w_up_g, fcw_full, ffn_conv_b, w_down_full)
    kb = 1408 if T % 1408 == 0 else TR
    g_up = _weight_grad(u2t, dup, bn=DFF, bk=kb, name="dw_up_mm", shard_cols=2 * DFF // NSHARD)
    g_down = _weight_grad(ggt, dh2, bn=D, bk=kb, name="dw_down_mm")
    ps_ffn = _pair_sums([g_up, g_down.reshape(NSHARD, DFF // NSHARD, D)], ("w_up", "w_down"), c_idx, "ffn")

    dproj, dlb, dwhn, dcw, *parts_ffn = _mix_block_bwd(
        dh1, projb, o, sst, lb_param, whn_t, conv_w_full, w_out_full, ps_ffn)
    kb_deep = 2816 if T % 2816 == 0 else kb
    g_in = _weight_grad(ut, dproj, bn=9 * D // NSHARD, bk=kb_deep, name="dw_in_mm", shard_cols=9 * D // NSHARD)
    g_out = _weight_grad(mt, dh1, bn=D, bk=kb_deep, name="dw_out_mm")
    ps_mix = _pair_sums([g_in, g_out.reshape(NSHARD, D // NSHARD, D)], ("w_in", "w_out"), c_idx, "mix")

    grad_x, dmeta, dwattn, *parts_mix = _input_grad_block(dproj, x, head_tile, dh1, attn_norm_w, w_in_g, ps_mix)

    halves = [_chip_sum(ps, p, cj_idx, rb=_PAIR_ADD_ROWS[n], name=f"chip_sum_{n}")
              for ps, p, n in zip(ps_mix + ps_ffn, parts_mix + parts_ffn, ("w_in", "w_out", "w_up", "w_down"))]
    small = dict(dlb=dlb, dwattn=dwattn, dwhn=dwhn, dwffn=dwffn, dfb=dfb, dwfin=dwfin,
                 dcw=dcw, dfw=dfw, dmeta=dmeta, loss=loss_vec)
    return grad_x, small, halves


_SMALL_ORDER = ("dmeta", "dcw", "dfw", "dlb", "dwattn", "dwhn", "dwffn", "dfb", "dwfin", "loss")


def kernel(x, meta_tokens, lb_param, attn_norm_w, w_in, hgrn_norm_w, conv_w, w_out, ffn_norm_w, w_up, ffn_conv_w, ffn_conv_b, w_down, final_norm_w, loss_target, m_meta_tokens, m_lb_param, m_attn_norm_w, m_w_in, m_hgrn_norm_w, m_conv_w, m_w_out, m_ffn_norm_w, m_w_up, m_ffn_conv_w, m_ffn_conv_b, m_w_down, m_final_norm_w, v_meta_tokens, v_lb_param, v_attn_norm_w, v_w_in, v_hgrn_norm_w, v_conv_w, v_w_out, v_ffn_norm_w, v_w_up, v_ffn_conv_w, v_ffn_conv_b, v_w_down, v_final_norm_w):
    xi, yi, ci = _place()
    j = 2 * xi + yi
    c_idx = jnp.reshape(ci, (1,)).astype(jnp.int32)

    j_idx = jnp.reshape(j, (1,)).astype(jnp.int32)
    ds_, fs_ = D // NSHARD, DFF // NSHARD
    widen = lambda a: jnp.pad(a, ((0, 0), (0, 768 - a.shape[1])))
    rows_small = jnp.concatenate([widen(meta_tokens), widen(conv_w[0]), widen(ffn_conv_w[0]),
                                  jnp.zeros((2, 768), F32)], axis=0)
    s_in, s_out, s_up, s_down = [
        _into_slot(w[0], j_idx, rb=rb, dtype=BF16, name=f"slot_{n}", paired=pr)
        for w, rb, n, pr in ((w_in, 256, "w_in", False), (w_out, 128, "w_out", False), (w_up, 256, "w_up", True),
                             (w_down, 176, "w_down", False))]
    s_small = _into_slot(rows_small, j_idx, rb=rows_small.shape[0], dtype=F32, name="slot_small")
    w_in_g, w_out_g, small_g = _allgather_weights([s_in, s_out, s_small], (False, False, True))
    unshard = lambda a: jnp.transpose(a, (1, 0, 2)).reshape(a.shape[1], -1)
    meta_full = unshard(small_g[:, 0:NMETA, 0:ds_])
    conv_w_full = unshard(small_g[:, NMETA:NMETA + 3, 0:ds_])
    fcw_full = unshard(small_g[:, NMETA + 3:NMETA + 6, 0:fs_])

    cj_idx = jnp.stack([ci, j]).astype(jnp.int32)
    grad_x, small, halves = _local_step(
        x[0], loss_target[0], meta_full, lb_param, attn_norm_w, w_in_g, hgrn_norm_w, conv_w_full,
        w_out_g.reshape(D, D), ffn_norm_w, [s_up, s_down], fcw_full, ffn_conv_b, final_norm_w, c_idx, cj_idx)

    names = _SMALL_ORDER
    small_sums, g_big = _final_exchange([small[n] for n in names], halves)
    vals = dict(zip(names, small_sums))
    loss = vals["loss"].reshape(())
    g_small = {
        "meta_tokens": lax.dynamic_slice_in_dim(vals["dmeta"], j * (D // NSHARD), D // NSHARD, axis=1),
        "lb_param": jnp.concatenate([vals["dlb"], -vals["dlb"]], axis=0),
        "attn_norm_w": vals["dwattn"],
        "hgrn_norm_w": vals["dwhn"],
        "conv_w": lax.dynamic_slice_in_dim(vals["dcw"], j * (D // NSHARD), D // NSHARD, axis=1)[None],
        "ffn_norm_w": vals["dwffn"],
        "ffn_conv_w": lax.dynamic_slice_in_dim(vals["dfw"], j * (DFF // NSHARD), DFF // NSHARD, axis=1)[None],
        "ffn_conv_b": vals["dfb"],
        "final_norm_w": vals["dwfin"].reshape(D),
    }


    weights = {"meta_tokens": meta_tokens, "lb_param": lb_param, "attn_norm_w": attn_norm_w, "w_in": w_in,
               "hgrn_norm_w": hgrn_norm_w, "conv_w": conv_w, "w_out": w_out, "ffn_norm_w": ffn_norm_w,
               "w_up": w_up, "ffn_conv_w": ffn_conv_w, "ffn_conv_b": ffn_conv_b, "w_down": w_down,
               "final_norm_w": final_norm_w}
    ms = {"meta_tokens": m_meta_tokens, "lb_param": m_lb_param, "attn_norm_w": m_attn_norm_w, "w_in": m_w_in,
          "hgrn_norm_w": m_hgrn_norm_w, "conv_w": m_conv_w, "w_out": m_w_out, "ffn_norm_w": m_ffn_norm_w,
          "w_up": m_w_up, "ffn_conv_w": m_ffn_conv_w, "ffn_conv_b": m_ffn_conv_b, "w_down": m_w_down,
          "final_norm_w": m_final_norm_w}
    vs = {"meta_tokens": v_meta_tokens, "lb_param": v_lb_param, "attn_norm_w": v_attn_norm_w, "w_in": v_w_in,
          "hgrn_norm_w": v_hgrn_norm_w, "conv_w": v_conv_w, "w_out": v_w_out, "ffn_norm_w": v_ffn_norm_w,
          "w_up": v_w_up, "ffn_conv_w": v_ffn_conv_w, "ffn_conv_b": v_ffn_conv_b, "w_down": v_w_down,
          "final_norm_w": v_final_norm_w}
    order = list(weights)
    grads, deltas, new_m, new_v = {}, {}, {}, {}

    for name, g, rb in zip(("w_in", "w_out", "w_up", "w_down"), g_big, (256, 128, 256, 176)):
        shp = weights[name].shape
        w2, m2, v2 = (a.reshape(shp[1], shp[2]) for a in (weights[name], ms[name], vs[name]))
        g_, d_, nm_, nv_ = _adamw(w2, g, m2, v2, rb=rb, name=f"adamw_{name}")
        grads[name], deltas[name], new_m[name], new_v[name] = (a.reshape(shp) for a in (g_, d_, nm_, nv_))

    small_names = [n for n in order if n not in grads]
    as2d = lambda a: a.reshape(-1, a.shape[-1])
    res = _adamw_small([tuple(as2d(a) for a in (weights[n], g_small[n], ms[n], vs[n])) for n in small_names])
    for n, (d_, nm_, nv_) in zip(small_names, res):
        shp = weights[n].shape
        grads[n], deltas[n], new_m[n], new_v[n] = (a.reshape(shp) for a in (g_small[n], d_, nm_, nv_))

    return (loss, grad_x[None], *[grads[n] for n in order], *[deltas[n] for n in order],
            *[new_m[n] for n in order], *[new_v[n] for n in order])
```

```python
import jax
import jax.numpy as jnp
from jax import lax
from jax.experimental import pallas as pl
from jax.experimental.pallas import tpu as pltpu

F32 = jnp.float32
BF16 = jnp.bfloat16
MESH = pl.DeviceIdType.MESH

D = 1024
NH = 8
HD = 128
DFF = 2816
NMETA = 16
EPS = 1e-6
TR = 256
PAD = TR - NMETA
CH = 64
NSHARD = 4
VMEM_LIMIT = 62 * 1024 * 1024

ADAM_LR = 0.001
ADAM_B1 = 0.9
ADAM_B2 = 0.999
ADAM_EPS = 1e-08
ADAM_WD = 0.01
ADAM_STEP = 10


def _cparams(semantics=None, **kw):
    return pltpu.CompilerParams(dimension_semantics=semantics, vmem_limit_bytes=VMEM_LIMIT, **kw)


def _sigmoid(x):
    return 0.5 * jnp.tanh(0.5 * x) + 0.5


def _weight_grad(at, b, *, bn, bk, name, shard_cols=None):
    M, T = at.shape
    N = b.shape[1]
    assert b.shape[0] == T and T % bk == 0 and N % bn == 0, (name, M, N, T, bn, bk)
    nk = T // bk
    if shard_cols is None:
        nsh = 1
        blk = (M, bn)
        out_shape = jax.ShapeDtypeStruct((M, N), BF16)
        out_spec = pl.BlockSpec(blk, lambda j, k: (0, j))
    else:
        assert bn % shard_cols == 0 and N % shard_cols == 0
        nsh = bn // shard_cols
        blk = (nsh, M, shard_cols)
        out_shape = jax.ShapeDtypeStruct((N // shard_cols, M, shard_cols), BF16)
        out_spec = pl.BlockSpec(blk, lambda j, k: (j, 0, 0))

    def body(a_ref, b_ref, o_ref, acc_ref):
        k = pl.program_id(1)

        @pl.when(k == 0)
        def _():
            acc_ref[...] = jnp.zeros_like(acc_ref)

        p = jnp.dot(a_ref[...].astype(BF16), b_ref[...].astype(BF16), preferred_element_type=F32)
        if shard_cols is None:
            acc_ref[...] += p
        else:
            for q in range(nsh):
                acc_ref[q] += p[:, q * shard_cols:(q + 1) * shard_cols]

        @pl.when(k == nk - 1)
        def _():
            o_ref[...] = acc_ref[...].astype(BF16)

    return pl.pallas_call(
        body, name=name, grid=(N // bn, nk),
        in_specs=[pl.BlockSpec((M, bk), lambda j, k: (0, k)), pl.BlockSpec((bk, bn), lambda j, k: (k, j))],
        out_specs=out_spec, out_shape=out_shape, scratch_shapes=[pltpu.VMEM(blk, F32)],
        compiler_params=_cparams(("parallel", "arbitrary")),
    )(at, b)


def _input_grad_block(dproj, x_seq, head_tile, dres, w, w_in_g, psums):
    T = TR + x_seq.shape[0]
    nt = T // TR
    nsl = w_in_g.shape[0]
    wsl = w_in_g.shape[2]
    wcol = wsl // 2
    nex = len(psums)

    def body(*refs):
        dp_ref, x_ref, head_ref, dres_ref, w_ref, win_ref = refs[:6]
        gx_ref, dmeta_ref, dw_ref = refs[6 + nex:9 + nex]
        exchange = _ChipExchange(refs[6:6 + nex], refs[9 + nex:9 + 2 * nex], refs[-2], refs[-1])
        i = pl.program_id(0)

        @pl.when(i == 0)
        def _():
            exchange.send()

        @pl.when(i == nt - 1)
        def _():
            exchange.finish()

        x = jnp.where(i == 0, head_ref[...], x_ref[...])
        r = lax.rsqrt(jnp.mean(x * x, axis=-1, keepdims=True) + EPS)
        n = x * r
        du_v = None
        for s in range(nsl):
            part = _dot_nt(dp_ref[:, s * wsl:(s + 1) * wsl], win_ref[s])
            du_v = part if du_v is None else du_v + part
        dn = du_v * w_ref[...]
        dh = dres_ref[...] + r * (dn - n * jnp.mean(dn * n, axis=-1, keepdims=True))
        gx_ref[...] = dh
        part = jnp.sum(du_v * n, axis=0, keepdims=True)

        @pl.when(i == 0)
        def _():
            dmeta_ref[...] = dh[PAD:TR, :]
            dw_ref[...] = part

        @pl.when(i > 0)
        def _():
            dw_ref[...] += part

    row = pl.BlockSpec((TR, D), lambda i: (i, 0))
    vec = pl.BlockSpec((1, D), lambda i: (0, 0))
    hbm = pl.BlockSpec(memory_space=pl.ANY)
    return pl.pallas_call(
        body, name="input_grad_block", grid=(nt,),
        in_specs=[pl.BlockSpec((TR, 9 * D), lambda i: (i, 0)),
                  pl.BlockSpec((TR, D), lambda i: (jnp.maximum(i - 1, 0), 0)), pl.BlockSpec((TR, D), lambda i: (0, 0)),
                  row, vec, pl.BlockSpec(memory_space=pltpu.VMEM)] + [hbm] * nex,
        out_specs=[pl.BlockSpec((TR, D), lambda i: (jnp.maximum(i - 1, 0), 0)),
                   pl.BlockSpec((NMETA, D), lambda i: (0, 0)), vec] + [hbm] * nex,
        out_shape=[jax.ShapeDtypeStruct((T - TR, D), F32), jax.ShapeDtypeStruct((NMETA, D), F32),
                   jax.ShapeDtypeStruct((1, D), F32)]
        + [jax.ShapeDtypeStruct((3,) + p.shape[1:], p.dtype) for p in psums],
        scratch_shapes=[pltpu.SemaphoreType.DMA((nex, 3)), pltpu.SemaphoreType.DMA((nex, 3))],
        compiler_params=_cparams(("arbitrary",)),
    )(dproj, x_seq, head_tile, dres, w, w_in_g, *psums)


def _tri_matmul(tri_bf16, x):
    hi = x.astype(BF16)
    lo = (x - hi.astype(F32)).astype(BF16)
    return jnp.dot(tri_bf16, lo, preferred_element_type=F32) + jnp.dot(tri_bf16, hi, preferred_element_type=F32)


def _shift_down(x, prev8, n):
    rows = x.shape[0]
    return pltpu.roll(jnp.concatenate([prev8, x], axis=0), n, 0)[8:8 + rows, :]


def _shift_up(x, next8, n):
    rows = x.shape[0]
    return pltpu.roll(jnp.concatenate([x, next8], axis=0), rows + 8 - n, 0)[0:rows, :]


def _gates(f_raw, lb):
    sg = _sigmoid(f_raw)
    f = lb + (1.0 - lb) * sg
    return sg, f, jnp.log(f), 1.0 - f


def _lower_bound(lbp_ref):
    return _sigmoid(lbp_ref[0:1, :] - lbp_ref[1:2, :])


def _tri_masks():
    r = lax.broadcasted_iota(jnp.int32, (CH, CH), 0)
    c = lax.broadcasted_iota(jnp.int32, (CH, CH), 1)
    return r >= c, r <= c


def _ones_where(mask):
    return jnp.where(mask, 1.0, 0.0).astype(BF16)


def _dot(a, b):
    return jnp.dot(a.astype(BF16), b.astype(BF16), preferred_element_type=F32)


def _dot_nt(a, b):
    return lax.dot_general(a.astype(BF16), b.astype(BF16), (((1,), (1,)), ((), ())), preferred_element_type=F32)


def _dot_tn(a, b):
    return lax.dot_general(a.astype(BF16), b.astype(BF16), (((0,), (0,)), ((), ())), preferred_element_type=F32)


def _mix_block_fwd(x_seq, head_tile, lb_param, wattn, whn, conv_w, w_in_g, w_out, late, late_paired):
    T = TR + x_seq.shape[0]
    nt = T // TR
    ncht = TR // CH
    nsl = w_in_g.shape[0]
    wsl = w_in_g.shape[2]
    wcol = wsl // 2

    nlate = len(late)

    def body(*refs):
        x_ref, head_ref, lbp_ref, wa_ref, whn_ref, cw_ref, win_ref, wout_ref = refs[:8]
        ut_ref, projb_ref, o_ref, sst_ref, mt_ref, h1_ref = refs[8 + nlate:14 + nlate]
        late_refs = refs[14 + nlate:14 + 2 * nlate]
        proj_ref, m_ref, st_ref, cxc_ref, send_sems, recv_sems = refs[14 + 2 * nlate:]
        i = pl.program_id(0)
        gather = _Gather(late_refs, send_sems, recv_sems, (False,) * nlate, late_paired)

        @pl.when(i == 0)
        def _():
            st_ref[...] = jnp.zeros_like(st_ref)
            cxc_ref[...] = jnp.zeros_like(cxc_ref)
            gather.send()

        @pl.when(i == nt // 2)
        def _():
            gather.relay()

        x = jnp.where(i == 0, head_ref[...], x_ref[...])
        r1 = lax.rsqrt(jnp.mean(x * x, axis=-1, keepdims=True) + EPS)
        u_f = x * r1 * wa_ref[...]
        u = u_f.astype(BF16)
        ut_ref[...] = u_f.T.astype(BF16)
        for s in range(nsl):
            cols = slice(s * wsl, (s + 1) * wsl)
            p = jnp.dot(u, win_ref[s], preferred_element_type=F32)
            proj_ref[:, cols] = p
            projb_ref[:, cols] = p.astype(BF16)

        lb = _lower_bound(lbp_ref)
        lower, _ = _tri_masks()
        ltri = _ones_where(lower)
        whn_v = whn_ref[...]
        w0, w1, w2 = cw_ref[0:1, :], cw_ref[1:2, :], cw_ref[2:3, :]

        def chunk(c, carry):
            rows = pl.ds(pl.multiple_of(c * CH, CH), CH)
            q_raw = proj_ref[rows, 0:D]
            f_raw = proj_ref[rows, D:2 * D]
            v = proj_ref[rows, 2 * D:3 * D]
            q = q_raw * _sigmoid(q_raw)
            _, _, g, k = _gates(f_raw, lb)
            gam = _tri_matmul(ltri, g)
            gam_l = gam[CH - 1:CH, :]
            e_l = jnp.exp(gam_l)
            qt = (q * jnp.exp(gam)).astype(BF16)
            kt = (k * jnp.exp(-gam)).astype(BF16)
            khat = (k * jnp.exp(gam_l - gam)).astype(BF16)
            vb = v.astype(BF16)
            on_parts = []
            for h in range(NH):
                cs = slice(h * HD, (h + 1) * HD)
                st = st_ref[h]
                sst_ref[c, h] = st
                a = jnp.where(lower, _dot_nt(qt[:, cs], kt[:, cs]), 0.0)
                o_h = _dot_nt(qt[:, cs], st) + _dot(a, vb[:, cs])
                st_ref[h] = st * e_l[:, cs] + _dot_tn(vb[:, cs], khat[:, cs])
                o_ref[rows, cs] = o_h
                ro = lax.rsqrt(jnp.mean(o_h * o_h, axis=-1, keepdims=True) + EPS)
                on_parts.append(o_h * ro)
            on = jnp.concatenate(on_parts, axis=1)
            g_out = proj_ref[rows, 3 * D:4 * D]
            y_a = on * whn_v * (g_out * _sigmoid(g_out))
            cx = proj_ref[rows, 5 * D:6 * D] * proj_ref[rows, 6 * D:7 * D]
            prev8 = cxc_ref[...]
            cv = w0 * _shift_down(cx, prev8, 2) + w1 * _shift_down(cx, prev8, 1) + w2 * cx
            cxc_ref[...] = cx[CH - 8:CH, :]
            y_b = proj_ref[rows, 4 * D:5 * D] * cv
            m = _sigmoid(proj_ref[rows, 7 * D:8 * D]) * y_a + _sigmoid(proj_ref[rows, 8 * D:9 * D]) * y_b
            m_ref[rows, :] = m
            return carry

        lax.fori_loop(0, ncht, chunk, 0, unroll=True)
        m_v = m_ref[...]
        h1_ref[...] = x + jnp.dot(m_v.astype(BF16), wout_ref[...], preferred_element_type=F32)
        mt_ref[...] = m_v.T.astype(BF16)

        @pl.when(i == nt - 1)
        def _():
            gather.finish()

    row = lambda w: pl.BlockSpec((TR, w), lambda i: (i, 0))
    col = lambda w: pl.BlockSpec((w, TR), lambda i: (0, i))
    vec = lambda r: pl.BlockSpec((r, D), lambda i: (0, 0))
    vm = pl.BlockSpec(memory_space=pltpu.VMEM)
    hbm = pl.BlockSpec(memory_space=pl.ANY)
    return pl.pallas_call(
        body, name="mix_block_fwd", grid=(nt,),
        in_specs=[pl.BlockSpec((TR, D), lambda i: (jnp.maximum(i - 1, 0), 0)), pl.BlockSpec((TR, D), lambda i: (0, 0)),
                  vec(2), vec(1), vec(1), vec(3), vm, vm] + [hbm] * nlate,
        out_specs=[col(D), row(9 * D), row(D), pl.BlockSpec((ncht, NH, HD, HD), lambda i: (i, 0, 0, 0)),
                   col(D), row(D)] + [hbm] * nlate,
        out_shape=[jax.ShapeDtypeStruct((D, T), BF16), jax.ShapeDtypeStruct((T, 9 * D), BF16),
                   jax.ShapeDtypeStruct((T, D), F32), jax.ShapeDtypeStruct((T // CH, NH, HD, HD), F32),
                   jax.ShapeDtypeStruct((D, T), BF16), jax.ShapeDtypeStruct((T, D), F32)]
        + [jax.ShapeDtypeStruct(a.shape, a.dtype) for a in late],
        input_output_aliases={8 + n: 6 + n for n in range(nlate)},
        scratch_shapes=[pltpu.VMEM((TR, 9 * D), F32), pltpu.VMEM((TR, D), F32), pltpu.VMEM((NH, HD, HD), F32),
                        pltpu.VMEM((8, D), F32), pltpu.SemaphoreType.DMA((nlate, 6)),
                        pltpu.SemaphoreType.DMA((nlate, 6))],
        compiler_params=_cparams(("arbitrary",)),
    )(x_seq, head_tile, lb_param, wattn, whn, conv_w, w_in_g, w_out, *late)


def _mix_block_bwd(dh1, projb, o, sst, lb_param, whn, conv_w, w_out, psums):
    T = projb.shape[0]
    nt = T // TR
    ncht = TR // CH
    tb16 = TR // 16
    nex = len(psums)

    def body(*refs):
        dh1_ref, proj_ref, pc_ref, px_ref, o_ref, sst_ref, lbp_ref, whn_ref, cw_ref, wout_ref = refs[:10]
        dproj_ref, dlb_ref, dwhn_ref, dcw_ref = refs[10 + nex:14 + nex]
        exchange = _ChipExchange(refs[10:10 + nex], refs[14 + nex:14 + 2 * nex], refs[-2], refs[-1])
        dm_ref, dst_ref, dcvc_ref, acc_lb, acc_hn, acc_cw = refs[14 + 2 * nex:-2]
        s = pl.program_id(0)
        tile = nt - 1 - s

        @pl.when(s == 0)
        def _():
            dst_ref[...] = jnp.zeros_like(dst_ref)
            dcvc_ref[...] = jnp.zeros_like(dcvc_ref)
            acc_lb[...] = jnp.zeros_like(acc_lb)
            acc_hn[...] = jnp.zeros_like(acc_hn)
            acc_cw[...] = jnp.zeros_like(acc_cw)
            exchange.send()

        dm_ref[...] = _dot_nt(dh1_ref[...], wout_ref[...])
        lb = _lower_bound(lbp_ref)
        lower, upper = _tri_masks()
        ltri = _ones_where(lower)
        utri = _ones_where(upper)
        whn_v = whn_ref[...]
        w0, w1, w2 = cw_ref[0:1, :], cw_ref[1:2, :], cw_ref[2:3, :]
        cx_before_tile = jnp.where(tile > 0, (pc_ref[...].astype(F32) * px_ref[...].astype(F32))[8:16, :], 0.0)
        rid = lax.broadcasted_iota(jnp.int32, (CH, D), 0)

        def chunk(cc, carry):
            c = ncht - 1 - cc
            r0 = pl.multiple_of(c * CH, CH)
            rows = pl.ds(r0, CH)
            slab = lambda n: proj_ref[rows, n * D:(n + 1) * D].astype(F32)
            q_raw, f_raw, v, g_out, b_gate, c_gate, x_conv = (slab(n) for n in range(7))
            sa = _sigmoid(slab(7))
            sb = _sigmoid(slab(8))
            dm_v = dm_ref[rows, :]

            sq = _sigmoid(q_raw)
            q = q_raw * sq
            sg, f, g, k = _gates(f_raw, lb)
            gam = _tri_matmul(ltri, g)
            gam_l = gam[CH - 1:CH, :]
            e_l = jnp.exp(gam_l)
            e_g = jnp.exp(gam)
            e_ng = jnp.exp(-gam)
            e_kl = jnp.exp(gam_l - gam)
            qt = q * e_g
            kt = k * e_ng
            khat = k * e_kl
            qt_b, kt_b, khat_b, v_b = qt.astype(BF16), kt.astype(BF16), khat.astype(BF16), v.astype(BF16)
            qt_seen, kt_seen = qt_b.astype(F32), kt_b.astype(F32)
            s_go = _sigmoid(g_out)
            silu_go = g_out * s_go
            cx = c_gate * x_conv
            rprev = pl.ds(pl.multiple_of(jnp.maximum(r0 - 16, 0), 16), 16)
            cx_prev_in = (proj_ref[rprev, 5 * D:6 * D].astype(F32) * proj_ref[rprev, 6 * D:7 * D].astype(F32))[8:16, :]
            prev8 = jnp.where(c > 0, cx_prev_in, cx_before_tile)
            cx_m1 = _shift_down(cx, prev8, 1)
            cx_m2 = _shift_down(cx, prev8, 2)
            cv = w0 * cx_m2 + w1 * cx_m1 + w2 * cx
            y_b = b_gate * cv

            o_v = o_ref[rows, :]
            ro_parts, on_parts = [], []
            for h in range(NH):
                cs = slice(h * HD, (h + 1) * HD)
                o_h = o_v[:, cs]
                ro = lax.rsqrt(jnp.mean(o_h * o_h, axis=-1, keepdims=True) + EPS)
                ro_parts.append(ro)
                on_parts.append(o_h * ro)
            on = jnp.concatenate(on_parts, axis=1)
            y_a = on * whn_v * silu_go

            dy_a = dm_v * sa
            dy_b = dm_v * sb
            dproj_ref[rows, 7 * D:8 * D] = (dm_v * y_a * sa * (1.0 - sa)).astype(BF16)
            dproj_ref[rows, 8 * D:9 * D] = (dm_v * y_b * sb * (1.0 - sb)).astype(BF16)
            dproj_ref[rows, 4 * D:5 * D] = (dy_b * cv).astype(BF16)
            dcv = dy_b * b_gate
            acc_cw[0:1, :] += jnp.sum(dcv * cx_m2, axis=0, keepdims=True)
            acc_cw[1:2, :] += jnp.sum(dcv * cx_m1, axis=0, keepdims=True)
            acc_cw[2:3, :] += jnp.sum(dcv * cx, axis=0, keepdims=True)
            next8 = dcvc_ref[...]
            dcx = w2 * dcv + w1 * _shift_up(dcv, next8, 1) + w0 * _shift_up(dcv, next8, 2)
            dcvc_ref[...] = dcv[0:8, :]
            dproj_ref[rows, 5 * D:6 * D] = (dcx * x_conv).astype(BF16)
            dproj_ref[rows, 6 * D:7 * D] = (dcx * c_gate).astype(BF16)
            don = dy_a * whn_v * silu_go
            dproj_ref[rows, 3 * D:4 * D] = (dy_a * on * whn_v * (s_go * (1.0 + g_out * (1.0 - s_go)))).astype(BF16)
            acc_hn[...] += jnp.sum(dy_a * silu_go * on, axis=0, keepdims=True)

            dq_parts, dk_parts, dv_parts, dgam_parts, ext_parts = [], [], [], [], []
            for h in range(NH):
                cs = slice(h * HD, (h + 1) * HD)
                on_h = on_parts[h]
                don_h = don[:, cs]
                do_h = ro_parts[h] * (don_h - on_h * jnp.mean(don_h * on_h, axis=-1, keepdims=True))
                qt_h, kt_h, khat_h, v_h = qt_b[:, cs], kt_b[:, cs], khat_b[:, cs], v_b[:, cs]
                do_b = do_h.astype(BF16)
                st = sst_ref[c, h]
                dstn = dst_ref[h]
                dstn_b = dstn.astype(BF16)
                a_t = jnp.where(upper, _dot_nt(kt_h, qt_h), 0.0)
                da = jnp.where(lower, _dot_nt(do_b, v_h), 0.0)
                da_t = jnp.where(upper, _dot_nt(v_h, do_b), 0.0)
                dv_h = _dot(a_t, do_b) + _dot_nt(khat_h, dstn_b)
                dqt_state = _dot(do_b, st)
                dqt_chunk = _dot(da, kt_h)
                dkt = _dot(da_t, qt_h)
                dkhat = _dot(v_h, dstn_b)
                dq_h = (dqt_state + dqt_chunk) * e_g[:, cs]
                dk_h = dkt * e_ng[:, cs] + dkhat * e_kl[:, cs]
                khat_dkhat = dkhat * khat[:, cs]
                ext = (jnp.sum(khat_dkhat, axis=0, keepdims=True)
                       + e_l[:, cs] * jnp.sum(st * dstn, axis=0, keepdims=True))
                dst_ref[h] = _dot_tn(do_b, qt_h) + dstn * e_l[:, cs]
                dq_parts.append(dq_h)
                dk_parts.append(dk_h)
                dv_parts.append(dv_h)
                dgam_parts.append(qt[:, cs] * dqt_state + qt_seen[:, cs] * dqt_chunk - kt_seen[:, cs] * dkt
                                  - khat_dkhat)
                ext_parts.append(ext)
            dq = jnp.concatenate(dq_parts, axis=1)
            dk = jnp.concatenate(dk_parts, axis=1)
            dgam = jnp.concatenate(dgam_parts, axis=1)
            ext = jnp.concatenate(ext_parts, axis=1)
            dgam = dgam + jnp.where(rid == CH - 1, ext, 0.0)
            dg = _tri_matmul(utri, dgam)
            dproj_ref[rows, 0:D] = (dq * (sq * (1.0 + q_raw * (1.0 - sq)))).astype(BF16)
            df = dg * jnp.exp(-g) - dk
            dproj_ref[rows, D:2 * D] = (df * (1.0 - lb) * sg * (1.0 - sg)).astype(BF16)
            dproj_ref[rows, 2 * D:3 * D] = jnp.concatenate(dv_parts, axis=1).astype(BF16)
            real = (tile * TR + r0 + rid) >= PAD
            acc_lb[...] += jnp.sum(jnp.where(real, df * (1.0 - sg), 0.0), axis=0, keepdims=True)
            return carry

        lax.fori_loop(0, ncht, chunk, 0)

        @pl.when(s == nt - 1)
        def _():
            dlb_ref[...] = acc_lb[...] * lb * (1.0 - lb)
            hn = acc_hn[...]
            tot = hn[:, 0:HD]
            for h in range(1, NH):
                tot = tot + hn[:, h * HD:(h + 1) * HD]
            dwhn_ref[...] = tot
            dcw_ref[...] = acc_cw[0:3, :]
            exchange.finish()

    hbm = pl.BlockSpec(memory_space=pl.ANY)
    rev = lambda s: (nt - 1 - s, 0)
    prevc = lambda s: (jnp.maximum((nt - 1 - s) * tb16 - 1, 0), 5)
    prevx = lambda s: (jnp.maximum((nt - 1 - s) * tb16 - 1, 0), 6)
    const = lambda s: (0, 0)
    return pl.pallas_call(
        body, name="mix_block_bwd", grid=(nt,),
        in_specs=[pl.BlockSpec((TR, D), rev),
                  pl.BlockSpec((TR, 9 * D), rev),
                  pl.BlockSpec((16, D), prevc),
                  pl.BlockSpec((16, D), prevx),
                  pl.BlockSpec((TR, D), rev),
                  pl.BlockSpec((ncht, NH, HD, HD), lambda s: (nt - 1 - s, 0, 0, 0)),
                  pl.BlockSpec((2, D), const),
                  pl.BlockSpec((1, D), const),
                  pl.BlockSpec((3, D), const),
                  pl.BlockSpec(memory_space=pltpu.VMEM)] + [hbm] * nex,
        out_specs=[pl.BlockSpec((TR, 9 * D), rev),
                   pl.BlockSpec((1, D), const),
                   pl.BlockSpec((1, HD), const),
                   pl.BlockSpec((3, D), const)] + [hbm] * nex,
        out_shape=[jax.ShapeDtypeStruct((T, 9 * D), BF16), jax.ShapeDtypeStruct((1, D), F32),
                   jax.ShapeDtypeStruct((1, HD), F32), jax.ShapeDtypeStruct((3, D), F32)]
        + [jax.ShapeDtypeStruct((3,) + p.shape[1:], p.dtype) for p in psums],
        scratch_shapes=[pltpu.VMEM((TR, D), F32), pltpu.VMEM((NH, HD, HD), F32), pltpu.VMEM((8, D), F32),
                        pltpu.VMEM((1, D), F32), pltpu.VMEM((1, D), F32), pltpu.VMEM((8, D), F32),
                        pltpu.SemaphoreType.DMA((nex, 3)), pltpu.SemaphoreType.DMA((nex, 3))],
        compiler_params=_cparams(("arbitrary",)),
    )(dh1, projb, projb, projb, o, sst, lb_param, whn, conv_w, w_out, *psums)


def _ffn_block_fwd(h1, tgt, wffn, w_up_g, fcw, fcb, w_down, wfin):
    T = h1.shape[0]
    nt = T // TR
    nsl = w_up_g.shape[0]
    wsl = w_up_g.shape[2]

    def body(h_ref, t_ref, wn_ref, wup_ref, cw_ref, cb_ref, wdn_ref, wf_ref,
             u2t_ref, upb_ref, ggt_ref, dh_ref, loss_ref, dwf_ref, up_scr, gg_ref, carry_ref):
        i = pl.program_id(0)

        @pl.when(i == 0)
        def _():
            carry_ref[...] = jnp.zeros_like(carry_ref)
            loss_ref[...] = jnp.zeros_like(loss_ref)
            dwf_ref[...] = jnp.zeros_like(dwf_ref)

        x = h_ref[...]
        r2 = lax.rsqrt(jnp.mean(x * x, axis=-1, keepdims=True) + EPS)
        u2_f = x * r2 * wn_ref[...]
        u2 = u2_f.astype(BF16)
        u2t_ref[...] = u2_f.T.astype(BF16)
        for s in range(nsl):
            up_s = jnp.dot(u2, wup_ref[s], preferred_element_type=F32)
            up_scr[:, s * wsl:(s + 1) * wsl] = up_s
            upb_ref[:, s * wsl:(s + 1) * wsl] = up_s.astype(BF16)
        w0, w1, w2 = cw_ref[0:1, :], cw_ref[1:2, :], cw_ref[2:3, :]

        def chunk(c, carry):
            rows = pl.ds(pl.multiple_of(c * CH, CH), CH)
            a_pre = up_scr[rows, 0:DFF]
            val = up_scr[rows, DFF:2 * DFF]
            prev8 = carry_ref[...]
            a = w0 * _shift_down(a_pre, prev8, 2) + w1 * _shift_down(a_pre, prev8, 1) + w2 * a_pre + cb_ref[...]
            carry_ref[...] = a_pre[CH - 8:CH, :]
            gg_ref[rows, :] = a * _sigmoid(a) * val
            return carry

        lax.fori_loop(0, TR // CH, chunk, 0, unroll=True)
        gg_v = gg_ref[...]
        ggt_ref[...] = gg_v.T.astype(BF16)
        h2 = x + jnp.dot(gg_v.astype(BF16), wdn_ref[...], preferred_element_type=F32)
        r3 = lax.rsqrt(jnp.mean(h2 * h2, axis=-1, keepdims=True) + EPS)
        n3 = h2 * r3
        wf = wf_ref[...]
        diff = jnp.where(i > 0, n3 * wf - t_ref[...], 0.0)
        loss_ref[...] += jnp.sum(diff * diff, axis=0, keepdims=True) * (0.5 / D)
        dy = diff * (1.0 / D)
        dwf_ref[...] += jnp.sum(dy * n3, axis=0, keepdims=True)
        dn = dy * wf
        dh_ref[...] = r3 * (dn - n3 * jnp.mean(dn * n3, axis=-1, keepdims=True))

    row = lambda w: pl.BlockSpec((TR, w), lambda i: (i, 0))
    col = lambda w: pl.BlockSpec((w, TR), lambda i: (0, i))
    vec = lambda w, r=1: pl.BlockSpec((r, w), lambda i: (0, 0))
    vm = pl.BlockSpec(memory_space=pltpu.VMEM)
    return pl.pallas_call(
        body, name="ffn_block_fwd", grid=(nt,),
        in_specs=[row(D), pl.BlockSpec((TR, D), lambda i: (jnp.maximum(i - 1, 0), 0)), vec(D), vm,
                  vec(DFF, 3), vec(DFF), vm, vec(D)],
        out_specs=[col(D), row(2 * DFF), col(DFF), row(D), vec(D), vec(D)],
        out_shape=[jax.ShapeDtypeStruct((D, T), BF16), jax.ShapeDtypeStruct((T, 2 * DFF), BF16),
                   jax.ShapeDtypeStruct((DFF, T), BF16), jax.ShapeDtypeStruct((T, D), F32),
                   jax.ShapeDtypeStruct((1, D), F32), jax.ShapeDtypeStruct((1, D), F32)],
        scratch_shapes=[pltpu.VMEM((TR, 2 * DFF), F32), pltpu.VMEM((TR, DFF), F32), pltpu.VMEM((8, DFF), F32)],
        compiler_params=_cparams(("arbitrary",)),
    )(h1, tgt, wffn, w_up_g, fcw, fcb, w_down, wfin)


def _ffn_block_bwd(dh2, upb, h1, wffn, w_up_g, fcw, fcb, w_down):
    T = h1.shape[0]
    nt = T // TR
    ncht = TR // CH
    nsl = w_up_g.shape[0]
    wsl = w_up_g.shape[2]
    tb16 = TR // 16
    assert ncht % nsl == 0
    every = ncht // nsl
    step = -(-DFF // (ncht * 128)) * 128
    parts = [(c0, min(c0 + step, DFF)) for c0 in range(0, DFF, step)]
    assert len(parts) == ncht

    def body(dh2n_ref, dh2p_ref, up_ref, pa_ref, h_ref, wn_ref, wup_ref, cw_ref, cb_ref, wdn_ref,
             dup_ref, dh1_ref, dfw_ref, dfb_ref, dwn_ref,
             dgg_ring, dup_ring, carry_ref, acc_w, acc_b, acc_n):
        s = pl.program_id(0)
        slot = lax.rem(s, 2)
        other = 1 - slot

        @pl.when(s == 0)
        def _():
            carry_ref[...] = jnp.zeros_like(carry_ref)
            acc_w[...] = jnp.zeros_like(acc_w)
            acc_b[...] = jnp.zeros_like(acc_b)
            acc_n[...] = jnp.zeros_like(acc_n)
            dup_ring[1] = jnp.zeros((TR, 2 * DFF), BF16)
            dgg_ring[0] = _dot_nt(dh2p_ref[...], wdn_ref[...])

        def norm_bwd(du2, valid):
            x = h_ref[...]
            dh2p = dh2p_ref[...]
            r2 = lax.rsqrt(jnp.mean(x * x, axis=-1, keepdims=True) + EPS)
            n2 = x * r2
            dn = du2 * wn_ref[...]
            dh1_ref[...] = dh2p + r2 * (dn - n2 * jnp.mean(dn * n2, axis=-1, keepdims=True))
            acc_n[...] += jnp.where(valid, jnp.sum(du2 * n2, axis=0, keepdims=True), 0.0)

        @pl.when(s < nt)
        def _():
            tile = nt - 1 - s
            w0, w1, w2 = cw_ref[0:1, :], cw_ref[1:2, :], cw_ref[2:3, :]
            a_before_tile = jnp.where(tile > 0, pa_ref[...].astype(F32)[8:16, :], 0.0)
            dh2n = dh2n_ref[...].astype(BF16)
            du2 = None
            for idx in range(ncht):
                c = ncht - 1 - idx
                r0 = c * CH
                rows = slice(r0, r0 + CH)
                a_pre = up_ref[rows, 0:DFF].astype(F32)
                val = up_ref[rows, DFF:2 * DFF].astype(F32)
                prev8 = up_ref[r0 - 16:r0, 0:DFF].astype(F32)[8:16, :] if c > 0 else a_before_tile
                a_m1 = _shift_down(a_pre, prev8, 1)
                a_m2 = _shift_down(a_pre, prev8, 2)
                a = w0 * a_m2 + w1 * a_m1 + w2 * a_pre + cb_ref[...]
                sig = _sigmoid(a)
                dgg_v = dgg_ring[slot, rows, :]
                da = dgg_v * val * (sig * (1.0 + a * (1.0 - sig)))
                dval = (dgg_v * (a * sig)).astype(BF16)
                next8 = carry_ref[...]
                da_pre = (w2 * da + w1 * _shift_up(da, next8, 1) + w0 * _shift_up(da, next8, 2)).astype(BF16)
                carry_ref[...] = da[0:8, :]
                dup_ref[rows, 0:DFF] = da_pre
                dup_ref[rows, DFF:2 * DFF] = dval
                dup_ring[slot, rows, 0:DFF] = da_pre
                dup_ring[slot, rows, DFF:2 * DFF] = dval
                acc_w[0:1, :] += jnp.sum(da * a_m2, axis=0, keepdims=True)
                acc_w[1:2, :] += jnp.sum(da * a_m1, axis=0, keepdims=True)
                acc_w[2:3, :] += jnp.sum(da * a_pre, axis=0, keepdims=True)
                acc_b[...] += jnp.sum(da, axis=0, keepdims=True)
                c0, c1 = parts[idx]
                dgg_ring[other, :, c0:c1] = _dot_nt(dh2n, wdn_ref[c0:c1, :])
                if idx % every == 0:
                    sl = idx // every
                    part = _dot_nt(dup_ring[other, :, sl * wsl:(sl + 1) * wsl], wup_ref[sl])
                    du2 = part if du2 is None else du2 + part
            norm_bwd(du2, s > 0)

        @pl.when(s == nt)
        def _():
            du2 = _dot_nt(dup_ring[other, :, 0:wsl], wup_ref[0])
            for sl in range(1, nsl):
                du2 = du2 + _dot_nt(dup_ring[other, :, sl * wsl:(sl + 1) * wsl], wup_ref[sl])
            norm_bwd(du2, True)
            dfw_ref[...] = acc_w[0:3, :]
            dfb_ref[...] = acc_b[...]
            dwn_ref[...] = acc_n[...]

    gate_tile = lambda s: jnp.maximum(nt - 1 - s, 0)
    next_tile = lambda s: jnp.maximum(nt - 2 - s, 0)
    prev_tile = lambda s: jnp.minimum(nt - s, nt - 1)
    vec = lambda w, r=1: pl.BlockSpec((r, w), lambda s: (0, 0))
    vm = pl.BlockSpec(memory_space=pltpu.VMEM)
    return pl.pallas_call(
        body, name="ffn_block_bwd", grid=(nt + 1,),
        in_specs=[pl.BlockSpec((TR, D), lambda s: (next_tile(s), 0)),
                  pl.BlockSpec((TR, D), lambda s: (prev_tile(s), 0)),
                  pl.BlockSpec((TR, 2 * DFF), lambda s: (gate_tile(s), 0)),
                  pl.BlockSpec((16, DFF), lambda s: (jnp.maximum(gate_tile(s) * tb16 - 1, 0), 0)),
                  pl.BlockSpec((TR, D), lambda s: (prev_tile(s), 0)),
                  vec(D), vm, vec(DFF, 3), vec(DFF), vm],
        out_specs=[pl.BlockSpec((TR, 2 * DFF), lambda s: (gate_tile(s), 0)),
                   pl.BlockSpec((TR, D), lambda s: (prev_tile(s), 0)),
                   vec(DFF, 3), vec(DFF), vec(D)],
        out_shape=[jax.ShapeDtypeStruct((T, 2 * DFF), BF16), jax.ShapeDtypeStruct((T, D), F32),
                   jax.ShapeDtypeStruct((3, DFF), F32), jax.ShapeDtypeStruct((1, DFF), F32),
                   jax.ShapeDtypeStruct((1, D), F32)],
        scratch_shapes=[pltpu.VMEM((2, TR, DFF), F32), pltpu.VMEM((2, TR, 2 * DFF), BF16),
                        pltpu.VMEM((8, DFF), F32), pltpu.VMEM((8, DFF), F32), pltpu.VMEM((1, DFF), F32),
                        pltpu.VMEM((1, D), F32)],
        compiler_params=_cparams(("arbitrary",)),
    )(dh2, dh2, upb, upb, h1, wffn, w_up_g, fcw, fcb, w_down)


def _place():
    x, y, c = lax.axis_index("x"), lax.axis_index("y"), lax.axis_index("c")
    return x, y, c


_CHIP_FLIPS = ((1, 0), (0, 1), (1, 1))


def _flip(v, bit):
    return 1 - v if bit else v


def _into_slot(w, j_idx, *, rb, dtype, name, paired=False):
    r, cdim = w.shape

    def body(j_ref, w_ref, out_ref):
        del j_ref
        out_ref[...] = w_ref[...].astype(dtype)

    if paired:
        out_shape = jax.ShapeDtypeStruct((NSHARD // 2, r, 2 * cdim), dtype)
        out_spec = pl.BlockSpec((None, rb, cdim), lambda i, j_ref: (j_ref[0] // 2, i, j_ref[0] % 2))
    else:
        out_shape = jax.ShapeDtypeStruct((NSHARD, r, cdim), dtype)
        out_spec = pl.BlockSpec((None, rb, cdim), lambda i, j_ref: (j_ref[0], i, 0))
    grid_spec = pltpu.PrefetchScalarGridSpec(
        num_scalar_prefetch=1, grid=(r // rb,),
        in_specs=[pl.BlockSpec((rb, cdim), lambda i, j_ref: (i, 0))], out_specs=out_spec)
    return pl.pallas_call(
        body, name=name, grid_spec=grid_spec, out_shape=out_shape, compiler_params=_cparams(("parallel",)),
    )(j_idx, w)


class _Gather:
    def __init__(self, outs, send_sems, recv_sems, whole, paired=None):
        self.outs, self.send_sems, self.recv_sems, self.whole = outs, send_sems, recv_sems, whole
        self.paired = paired if paired is not None else (False,) * len(outs)
        self.x, self.y, self.c = _place()
        self.j = 2 * self.x + self.y
        self.sibling = (self.x, self.y, 1 - self.c)

    def _rows(self, w, core):
        r = self.outs[w].shape[1]
        return pl.ds(0, r) if self.whole[w] else pl.ds(core * (r // 2), r // 2)

    def _copy(self, w, slot, core, sem, to):
        if self.paired[w]:
            cw = self.outs[w].shape[2] // 2
            piece = self.outs[w].at[slot // 2, self._rows(w, core), pl.ds((slot % 2) * cw, cw)]
        else:
            piece = self.outs[w].at[slot, self._rows(w, core), :]
        return pltpu.make_async_remote_copy(
            src_ref=piece, dst_ref=piece, send_sem=self.send_sems.at[w, sem], recv_sem=self.recv_sems.at[w, sem],
            device_id=to, device_id_type=MESH)

    def _chips(self):
        for kk, (fx, fy) in enumerate(_CHIP_FLIPS):
            px, py = _flip(self.x, fx), _flip(self.y, fy)
            yield kk, 2 * px + py, (px, py, self.c)

    def send(self):
        for w in range(len(self.outs)):
            for kk, _, to in self._chips():
                self._copy(w, self.j, self.c, kk, to).start()

    def relay(self):
        for w in range(len(self.outs)):
            for kk, jk, _ in self._chips():
                self._copy(w, jk, self.c, kk, self.sibling).wait_recv()
                if not self.whole[w]:
                    self._copy(w, jk, self.c, 3 + kk, self.sibling).start()

    def finish(self):
        for w in range(len(self.outs)):
            for kk, jk, to in self._chips():
                self._copy(w, self.j, self.c, kk, to).wait_send()
                if not self.whole[w]:
                    self._copy(w, jk, 1 - self.c, 3 + kk, self.sibling).wait_recv()
                    self._copy(w, jk, self.c, 3 + kk, self.sibling).wait_send()


def _allgather_weights(slotted, whole):
    n = len(slotted)

    def body(*refs):
        g = _Gather(refs[n:2 * n], refs[2 * n], refs[2 * n + 1], whole)
        g.send()
        g.relay()
        g.finish()

    any_spec = pl.BlockSpec(memory_space=pl.ANY)
    return pl.pallas_call(
        body, name="allgather_weights",
        in_specs=[any_spec] * n, out_specs=[any_spec] * n,
        out_shape=[jax.ShapeDtypeStruct(a.shape, a.dtype) for a in slotted],
        input_output_aliases={i: i for i in range(n)},
        scratch_shapes=[pltpu.SemaphoreType.DMA((n, 6)), pltpu.SemaphoreType.DMA((n, 6))],
    )(*slotted)


class _ChipExchange:
    def __init__(self, ins, outs, send_sems, recv_sems):
        self.ins, self.outs, self.send_sems, self.recv_sems = ins, outs, send_sems, recv_sems
        self.x, self.y, self.c = _place()

    def _copies(self):
        for w in range(len(self.ins)):
            for kk, (fx, fy) in enumerate(_CHIP_FLIPS):
                px, py = _flip(self.x, fx), _flip(self.y, fy)
                yield pltpu.make_async_remote_copy(
                    src_ref=self.ins[w].at[2 * px + py], dst_ref=self.outs[w].at[kk],
                    send_sem=self.send_sems.at[w, kk], recv_sem=self.recv_sems.at[w, kk],
                    device_id=(px, py, self.c), device_id_type=MESH)

    def send(self):
        for cp in self._copies():
            cp.start()

    def finish(self):
        for cp in self._copies():
            cp.wait()


def _pair_exchange(grads, name):
    nw = len(grads)

    def body(*refs):
        ins, outs = refs[:nw], refs[nw:2 * nw]
        send_sems, recv_sems = refs[2 * nw:]
        x, y, c = _place()
        sibling = (x, y, 1 - c)
        cps = []
        for w in range(nw):
            half = ins[w].shape[1] // 2
            cp = pltpu.make_async_remote_copy(
                src_ref=ins[w].at[:, pl.ds((1 - c) * half, half), :], dst_ref=outs[w],
                send_sem=send_sems.at[w], recv_sem=recv_sems.at[w], device_id=sibling, device_id_type=MESH)
            cp.start()
            cps.append(cp)
        for cp in cps:
            cp.wait()

    any_spec = pl.BlockSpec(memory_space=pl.ANY)
    return pl.pallas_call(
        body, name=name,
        in_specs=[any_spec] * nw, out_specs=[any_spec] * nw,
        out_shape=[jax.ShapeDtypeStruct((g.shape[0], g.shape[1] // 2, g.shape[2]), g.dtype) for g in grads],
        scratch_shapes=[pltpu.SemaphoreType.DMA((nw,)), pltpu.SemaphoreType.DMA((nw,))],
    )(*grads)


def _pair_add(g, other, c_idx, *, rb, name):
    S, r, cdim = g.shape
    half = r // 2
    nb = half // rb

    def body(c_ref, g_ref, o_ref, out_ref):
        del c_ref
        out_ref[...] = (g_ref[...].astype(F32) + o_ref[...].astype(F32)).astype(BF16)

    grid_spec = pltpu.PrefetchScalarGridSpec(
        num_scalar_prefetch=1, grid=(S, nb),
        in_specs=[pl.BlockSpec((None, rb, cdim), lambda s, i, c_ref: (s, c_ref[0] * nb + i, 0)),
                  pl.BlockSpec((None, rb, cdim), lambda s, i, c_ref: (s, i, 0))],
        out_specs=pl.BlockSpec((None, rb, cdim), lambda s, i, c_ref: (s, i, 0)))
    return pl.pallas_call(
        body, name=name, grid_spec=grid_spec, out_shape=jax.ShapeDtypeStruct((S, half, cdim), BF16),
        compiler_params=_cparams(("parallel", "parallel")),
    )(c_idx, g, other)


def _chip_sum(psum, parts, cj_idx, *, rb, name):
    S, half, cdim = psum.shape
    nb = half // rb

    def body(cj_ref, own_ref, p_ref, out_ref):
        del cj_ref
        f = lambda v: v.astype(F32)
        out_ref[...] = ((f(own_ref[...]) + f(p_ref[0])) + f(p_ref[1])) + f(p_ref[2])

    grid_spec = pltpu.PrefetchScalarGridSpec(
        num_scalar_prefetch=1, grid=(nb,),
        in_specs=[pl.BlockSpec((None, rb, cdim), lambda i, cj: (cj[1], i, 0)),
                  pl.BlockSpec((3, rb, cdim), lambda i, cj: (0, i, 0))],
        out_specs=pl.BlockSpec((rb, cdim), lambda i, cj: (cj[0] * nb + i, 0)))
    return pl.pallas_call(
        body, name=name, grid_spec=grid_spec, out_shape=jax.ShapeDtypeStruct((2 * half, cdim), F32),
        compiler_params=_cparams(("parallel",)),
    )(cj_idx, psum, parts)


SLAB_W = 1024


def _final_exchange(pieces, grads):
    n, nw = len(pieces), len(grads)
    segs, at = [], 0
    for idx, p in enumerate(pieces):
        r, wd = p.shape
        for c0 in range(0, wd, SLAB_W):
            if r > 1:
                at = -(-at // 8) * 8
            segs.append((idx, c0, min(SLAB_W, wd - c0), at))
            at += r
    rows = -(-at // 8) * 8
    flips = [(fx, fy, fc) for fx in (0, 1) for fy in (0, 1) for fc in (0, 1)][1:]

    def body(*refs):
        ins = refs[:n]
        outs = refs[n + nw:2 * n + nw]
        g_refs = refs[2 * n + nw:2 * n + 2 * nw]
        mine_ref, slots_ref, send_sems, recv_sems, gsend_sems, grecv_sems = refs[2 * n + 2 * nw:]
        x, y, c = _place()
        me = 4 * x + 2 * y + c
        sibling = (x, y, 1 - c)

        def swap(w, core):
            half = g_refs[w].shape[0] // 2
            rows_ = g_refs[w].at[pl.ds(core * half, half), :]
            return pltpu.make_async_remote_copy(
                src_ref=rows_, dst_ref=rows_, send_sem=gsend_sems.at[w], recv_sem=grecv_sems.at[w],
                device_id=sibling, device_id_type=MESH)

        for w in range(nw):
            swap(w, c).start()
        mine_ref[...] = jnp.zeros_like(mine_ref)
        for idx, c0, wd, st in segs:
            r = ins[idx].shape[0]
            mine_ref[st:st + r, 0:wd] = ins[idx][:, c0:c0 + wd]
        slots_ref[me] = mine_ref[...]
        cps = []
        for kk, (fx, fy, fc) in enumerate(flips):
            cp = pltpu.make_async_remote_copy(
                src_ref=mine_ref, dst_ref=slots_ref.at[me], send_sem=send_sems.at[kk], recv_sem=recv_sems.at[kk],
                device_id=(_flip(x, fx), _flip(y, fy), _flip(c, fc)), device_id_type=MESH)
            cp.start()
            cps.append(cp)
        for cp in cps:
            cp.wait()
        tot = slots_ref[0]
        for d in range(1, 8):
            tot = tot + slots_ref[d]
        mine_ref[...] = tot
        for idx, c0, wd, st in segs:
            r = ins[idx].shape[0]
            val = mine_ref[st:st + r, 0:wd]
            if idx == n - 1:
                outs[idx][...] = jnp.sum(val, keepdims=True)
            else:
                outs[idx][:, c0:c0 + wd] = val
        for w in range(nw):
            swap(w, 1 - c).wait_recv()
            swap(w, c).wait_send()

    vm = pl.BlockSpec(memory_space=pltpu.VMEM)
    hbm = pl.BlockSpec(memory_space=pl.ANY)
    out_shape = ([jax.ShapeDtypeStruct(p.shape, F32) for p in pieces[:-1]] + [jax.ShapeDtypeStruct((1, 1), F32)]
                 + [jax.ShapeDtypeStruct(g.shape, g.dtype) for g in grads])
    res = pl.pallas_call(
        body, name="final_exchange", in_specs=[vm] * n + [hbm] * nw, out_specs=[vm] * n + [hbm] * nw,
        out_shape=out_shape, input_output_aliases={n + w: n + w for w in range(nw)},
        scratch_shapes=[pltpu.VMEM((rows, SLAB_W), F32), pltpu.VMEM((8, rows, SLAB_W), F32),
                        pltpu.SemaphoreType.DMA((7,)), pltpu.SemaphoreType.DMA((7,)),
                        pltpu.SemaphoreType.DMA((nw,)), pltpu.SemaphoreType.DMA((nw,))],
    )(*pieces, *grads)
    return res[:n], res[n:]


def _adamw(w, g, m, v, *, rb, name):
    r, cdim = w.shape

    def body(w_ref, g_ref, m_ref, v_ref, go_ref, d_ref, nm_ref, nv_ref):
        go_ref[...] = g_ref[...]
        d_ref[...], nm_ref[...], nv_ref[...] = _adamw_update(w_ref[...], g_ref[...], m_ref[...], v_ref[...])

    spec = pl.BlockSpec((rb, cdim), lambda i: (i, 0))
    shp = jax.ShapeDtypeStruct((r, cdim), F32)
    return pl.pallas_call(
        body, name=name, grid=(r // rb,), in_specs=[spec] * 4, out_specs=[spec] * 4, out_shape=[shp] * 4,
        compiler_params=_cparams(("parallel",)),
    )(w, g, m, v)


def _adamw_update(w, g, m, v):
    nm = ADAM_B1 * m + (1.0 - ADAM_B1) * g
    nv = ADAM_B2 * v + (1.0 - ADAM_B2) * (g * g)
    m_hat = nm / (1.0 - ADAM_B1 ** ADAM_STEP)
    v_hat = nv / (1.0 - ADAM_B2 ** ADAM_STEP)
    return -ADAM_LR * (m_hat / (jnp.sqrt(v_hat) + ADAM_EPS) + ADAM_WD * w), nm, nv


def _adamw_small(params):
    n = len(params)

    def body(*refs):
        ins, outs = refs[:4 * n], refs[4 * n:]
        for p in range(n):
            w_ref, g_ref, m_ref, v_ref = ins[4 * p:4 * p + 4]
            d, nm, nv = _adamw_update(w_ref[...], g_ref[...], m_ref[...], v_ref[...])
            outs[3 * p][...] = d
            outs[3 * p + 1][...] = nm
            outs[3 * p + 2][...] = nv

    vm = pl.BlockSpec(memory_space=pltpu.VMEM)
    flat = [a for p in params for a in p]
    out_shape = [jax.ShapeDtypeStruct(p[0].shape, F32) for p in params for _ in range(3)]
    res = pl.pallas_call(
        body, name="adamw_small", in_specs=[vm] * (4 * n), out_specs=[vm] * (3 * n), out_shape=out_shape,
    )(*flat)
    return [tuple(res[3 * p:3 * p + 3]) for p in range(n)]


_PAIR_ADD_ROWS = {"w_in": 256, "w_out": 128, "w_up": 256, "w_down": 176}


def _pair_sums(grads, names, c_idx, tag):
    others = _pair_exchange(grads, name=f"grad_pair_exchange_{tag}")
    return [_pair_add(g, o_, c_idx, rb=_PAIR_ADD_ROWS[n], name=f"pair_add_{n}") for g, o_, n in zip(grads, others, names)]


def _local_step(x, tgt, meta_full, lb_param, attn_norm_w, w_in_g, hgrn_norm_w, conv_w_full, w_out_full,
                ffn_norm_w, late_slotted, fcw_full, ffn_conv_b, final_norm_w, c_idx, cj_idx):
    seq = x.shape[0]
    T = TR + seq
    head_tile = jnp.concatenate([jnp.zeros((PAD, D), F32), meta_full], axis=0)
    whn_t = jnp.tile(hgrn_norm_w, (1, NH))

    ut, projb, o, sst, mt, h1, w_up_g, w_down_g = _mix_block_fwd(
        x, head_tile, lb_param, attn_norm_w, whn_t, conv_w_full, w_in_g, w_out_full, late_slotted, (True, False))
    w_down_full = w_down_g.reshape(DFF, D)
    u2t, upb, ggt, dh2, loss_vec, dwfin = _ffn_block_fwd(
        h1, tgt, ffn_norm_w, w_up_g, fcw_full, ffn_conv_b, w_down_full, final_norm_w.reshape(1, D))

    dup, dh1, dfw, dfb, dwffn = _ffn_block_bwd(
        dh2, upb, h1, ffn_norm_w, w_up_g, fcw_full, ffn_conv_b, w_down_full)
    kb = 1408 if T % 1408 == 0 else TR
    g_up = _weight_grad(u2t, dup, bn=DFF, bk=kb, name="dw_up_mm", shard_cols=2 * DFF // NSHARD)
    g_down = _weight_grad(ggt, dh2, bn=D // 2, bk=2816 if T % 2816 == 0 else kb, name="dw_down_mm")
    ps_ffn = _pair_sums([g_up, g_down.reshape(NSHARD, DFF // NSHARD, D)], ("w_up", "w_down"), c_idx, "ffn")

    dproj, dlb, dwhn, dcw, *parts_ffn = _mix_block_bwd(
        dh1, projb, o, sst, lb_param, whn_t, conv_w_full, w_out_full, ps_ffn)
    kb_deep = 2816 if T % 2816 == 0 else kb
    g_in = _weight_grad(ut, dproj, bn=9 * D // NSHARD, bk=kb_deep, name="dw_in_mm", shard_cols=9 * D // NSHARD)
    g_out = _weight_grad(mt, dh1, bn=D, bk=kb_deep, name="dw_out_mm")
    ps_mix = _pair_sums([g_in, g_out.reshape(NSHARD, D // NSHARD, D)], ("w_in", "w_out"), c_idx, "mix")

    grad_x, dmeta, dwattn, *parts_mix = _input_grad_block(dproj, x, head_tile, dh1, attn_norm_w, w_in_g, ps_mix)

    halves = [_chip_sum(ps, p, cj_idx, rb=_PAIR_ADD_ROWS[n], name=f"chip_sum_{n}")
              for ps, p, n in zip(ps_mix + ps_ffn, parts_mix + parts_ffn, ("w_in", "w_out", "w_up", "w_down"))]
    small = dict(dlb=dlb, dwattn=dwattn, dwhn=dwhn, dwffn=dwffn, dfb=dfb, dwfin=dwfin,
                 dcw=dcw, dfw=dfw, dmeta=dmeta, loss=loss_vec)
    return grad_x, small, halves


_SMALL_ORDER = ("dmeta", "dcw", "dfw", "dlb", "dwattn", "dwhn", "dwffn", "dfb", "dwfin", "loss")


def kernel(x, meta_tokens, lb_param, attn_norm_w, w_in, hgrn_norm_w, conv_w, w_out, ffn_norm_w, w_up, ffn_conv_w, ffn_conv_b, w_down, final_norm_w, loss_target, m_meta_tokens, m_lb_param, m_attn_norm_w, m_w_in, m_hgrn_norm_w, m_conv_w, m_w_out, m_ffn_norm_w, m_w_up, m_ffn_conv_w, m_ffn_conv_b, m_w_down, m_final_norm_w, v_meta_tokens, v_lb_param, v_attn_norm_w, v_w_in, v_hgrn_norm_w, v_conv_w, v_w_out, v_ffn_norm_w, v_w_up, v_ffn_conv_w, v_ffn_conv_b, v_w_down, v_final_norm_w):
    xi, yi, ci = _place()
    j = 2 * xi + yi
    c_idx = jnp.reshape(ci, (1,)).astype(jnp.int32)

    j_idx = jnp.reshape(j, (1,)).astype(jnp.int32)
    ds_, fs_ = D // NSHARD, DFF // NSHARD
    widen = lambda a: jnp.pad(a, ((0, 0), (0, 768 - a.shape[1])))
    rows_small = jnp.concatenate([widen(meta_tokens), widen(conv_w[0]), widen(ffn_conv_w[0]),
                                  jnp.zeros((2, 768), F32)], axis=0)
    s_in, s_out, s_up, s_down = [
        _into_slot(w[0], j_idx, rb=rb, dtype=BF16, name=f"slot_{n}", paired=pr)
        for w, rb, n, pr in ((w_in, 256, "w_in", False), (w_out, 128, "w_out", False), (w_up, 256, "w_up", True),
                             (w_down, 176, "w_down", False))]
    s_small = _into_slot(rows_small, j_idx, rb=rows_small.shape[0], dtype=F32, name="slot_small")
    w_in_g, w_out_g, small_g = _allgather_weights([s_in, s_out, s_small], (False, False, True))
    unshard = lambda a: jnp.transpose(a, (1, 0, 2)).reshape(a.shape[1], -1)
    meta_full = unshard(small_g[:, 0:NMETA, 0:ds_])
    conv_w_full = unshard(small_g[:, NMETA:NMETA + 3, 0:ds_])
    fcw_full = unshard(small_g[:, NMETA + 3:NMETA + 6, 0:fs_])

    cj_idx = jnp.stack([ci, j]).astype(jnp.int32)
    grad_x, small, halves = _local_step(
        x[0], loss_target[0], meta_full, lb_param, attn_norm_w, w_in_g, hgrn_norm_w, conv_w_full,
        w_out_g.reshape(D, D), ffn_norm_w, [s_up, s_down], fcw_full, ffn_conv_b, final_norm_w, c_idx, cj_idx)

    names = _SMALL_ORDER
    small_sums, g_big = _final_exchange([small[n] for n in names], halves)
    vals = dict(zip(names, small_sums))
    loss = vals["loss"].reshape(())
    g_small = {
        "meta_tokens": lax.dynamic_slice_in_dim(vals["dmeta"], j * (D // NSHARD), D // NSHARD, axis=1),
        "lb_param": jnp.concatenate([vals["dlb"], -vals["dlb"]], axis=0),
        "attn_norm_w": vals["dwattn"],
        "hgrn_norm_w": vals["dwhn"],
        "conv_w": lax.dynamic_slice_in_dim(vals["dcw"], j * (D // NSHARD), D // NSHARD, axis=1)[None],
        "ffn_norm_w": vals["dwffn"],
        "ffn_conv_w": lax.dynamic_slice_in_dim(vals["dfw"], j * (DFF // NSHARD), DFF // NSHARD, axis=1)[None],
        "ffn_conv_b": vals["dfb"],
        "final_norm_w": vals["dwfin"].reshape(D),
    }


    weights = {"meta_tokens": meta_tokens, "lb_param": lb_param, "attn_norm_w": attn_norm_w, "w_in": w_in,
               "hgrn_norm_w": hgrn_norm_w, "conv_w": conv_w, "w_out": w_out, "ffn_norm_w": ffn_norm_w,
               "w_up": w_up, "ffn_conv_w": ffn_conv_w, "ffn_conv_b": ffn_conv_b, "w_down": w_down,
               "final_norm_w": final_norm_w}
    ms = {"meta_tokens": m_meta_tokens, "lb_param": m_lb_param, "attn_norm_w": m_attn_norm_w, "w_in": m_w_in,
          "hgrn_norm_w": m_hgrn_norm_w, "conv_w": m_conv_w, "w_out": m_w_out, "ffn_norm_w": m_ffn_norm_w,
          "w_up": m_w_up, "ffn_conv_w": m_ffn_conv_w, "ffn_conv_b": m_ffn_conv_b, "w_down": m_w_down,
          "final_norm_w": m_final_norm_w}
    vs = {"meta_tokens": v_meta_tokens, "lb_param": v_lb_param, "attn_norm_w": v_attn_norm_w, "w_in": v_w_in,
          "hgrn_norm_w": v_hgrn_norm_w, "conv_w": v_conv_w, "w_out": v_w_out, "ffn_norm_w": v_ffn_norm_w,
          "w_up": v_w_up, "ffn_conv_w": v_ffn_conv_w, "ffn_conv_b": v_ffn_conv_b, "w_down": v_w_down,
          "final_norm_w": v_final_norm_w}
    order = list(weights)
    grads, deltas, new_m, new_v = {}, {}, {}, {}

    for name, g, rb in zip(("w_in", "w_out", "w_up", "w_down"), g_big, (256, 128, 256, 176)):
        shp = weights[name].shape
        w2, m2, v2 = (a.reshape(shp[1], shp[2]) for a in (weights[name], ms[name], vs[name]))
        g_, d_, nm_, nv_ = _adamw(w2, g, m2, v2, rb=rb, name=f"adamw_{name}")
        grads[name], deltas[name], new_m[name], new_v[name] = (a.reshape(shp) for a in (g_, d_, nm_, nv_))

    small_names = [n for n in order if n not in grads]
    as2d = lambda a: a.reshape(-1, a.shape[-1])
    res = _adamw_small([tuple(as2d(a) for a in (weights[n], g_small[n], ms[n], vs[n])) for n in small_names])
    for n, (d_, nm_, nv_) in zip(small_names, res):
        shp = weights[n].shape
        grads[n], deltas[n], new_m[n], new_v[n] = (a.reshape(shp) for a in (g_small[n], d_, nm_, nv_))

    return (loss, grad_x[None], *[grads[n] for n in order], *[deltas[n] for n in order],
            *[new_m[n] for n in order], *[new_v[n] for n in order])
```

```python
import jax
import jax.numpy as jnp
from jax import lax
from jax.experimental import pallas as pl
from jax.experimental.pallas import tpu as pltpu

F32 = jnp.float32
BF16 = jnp.bfloat16
MESH = pl.DeviceIdType.MESH

D = 1024
NH = 8
HD = 128
DFF = 2816
NMETA = 16
EPS = 1e-6
TR = 256
PAD = TR - NMETA
CH = 64
NSHARD = 4
VMEM_LIMIT = 62 * 1024 * 1024

ADAM_LR = 0.001
ADAM_B1 = 0.9
ADAM_B2 = 0.999
ADAM_EPS = 1e-08
ADAM_WD = 0.01
ADAM_STEP = 10


def _cparams(semantics=None, **kw):
    return pltpu.CompilerParams(dimension_semantics=semantics, vmem_limit_bytes=VMEM_LIMIT, **kw)


def _sigmoid(x):
    return 0.5 * jnp.tanh(0.5 * x) + 0.5


def _weight_grad(at, b, *, bn, bk, name, shard_cols=None):
    M, T = at.shape
    N = b.shape[1]
    assert b.shape[0] == T and T % bk == 0 and N % bn == 0, (name, M, N, T, bn, bk)
    nk = T // bk
    if shard_cols is None:
        nsh = 1
        blk = (M, bn)
        out_shape = jax.ShapeDtypeStruct((M, N), BF16)
        out_spec = pl.BlockSpec(blk, lambda j, k: (0, j))
    else:
        assert bn % shard_cols == 0 and N % shard_cols == 0
        nsh = bn // shard_cols
        blk = (nsh, M, shard_cols)
        out_shape = jax.ShapeDtypeStruct((N // shard_cols, M, shard_cols), BF16)
        out_spec = pl.BlockSpec(blk, lambda j, k: (j, 0, 0))

    def body(a_ref, b_ref, o_ref, acc_ref):
        k = pl.program_id(1)

        @pl.when(k == 0)
        def _():
            acc_ref[...] = jnp.zeros_like(acc_ref)

        p = jnp.dot(a_ref[...].astype(BF16), b_ref[...].astype(BF16), preferred_element_type=F32)
        if shard_cols is None:
            acc_ref[...] += p
        else:
            for q in range(nsh):
                acc_ref[q] += p[:, q * shard_cols:(q + 1) * shard_cols]

        @pl.when(k == nk - 1)
        def _():
            o_ref[...] = acc_ref[...].astype(BF16)

    return pl.pallas_call(
        body, name=name, grid=(N // bn, nk),
        in_specs=[pl.BlockSpec((M, bk), lambda j, k: (0, k)), pl.BlockSpec((bk, bn), lambda j, k: (k, j))],
        out_specs=out_spec, out_shape=out_shape, scratch_shapes=[pltpu.VMEM(blk, F32)],
        compiler_params=_cparams(("parallel", "arbitrary")),
    )(at, b)


def _input_grad_block(dproj, x_seq, head_tile, dres, w, w_in_g, psums):
    T = TR + x_seq.shape[0]
    nt = T // TR
    nsl = w_in_g.shape[0]
    wsl = w_in_g.shape[2]
    wcol = wsl // 2
    nex = len(psums)

    def body(*refs):
        dp_ref, x_ref, head_ref, dres_ref, w_ref, win_ref = refs[:6]
        gx_ref, dmeta_ref, dw_ref = refs[6 + nex:9 + nex]
        exchange = _ChipExchange(refs[6:6 + nex], refs[9 + nex:9 + 2 * nex], refs[-2], refs[-1])
        i = pl.program_id(0)

        @pl.when(i == 0)
        def _():
            exchange.send()

        @pl.when(i == nt - 1)
        def _():
            exchange.finish()

        x = jnp.where(i == 0, head_ref[...], x_ref[...])
        r = lax.rsqrt(jnp.mean(x * x, axis=-1, keepdims=True) + EPS)
        n = x * r
        du_v = None
        for s in range(nsl):
            part = _dot_nt(dp_ref[:, s * wsl:(s + 1) * wsl], win_ref[s])
            du_v = part if du_v is None else du_v + part
        dn = du_v * w_ref[...]
        dh = dres_ref[...] + r * (dn - n * jnp.mean(dn * n, axis=-1, keepdims=True))
        gx_ref[...] = dh
        part = jnp.sum(du_v * n, axis=0, keepdims=True)

        @pl.when(i == 0)
        def _():
            dmeta_ref[...] = dh[PAD:TR, :]
            dw_ref[...] = part

        @pl.when(i > 0)
        def _():
            dw_ref[...] += part

    row = pl.BlockSpec((TR, D), lambda i: (i, 0))
    vec = pl.BlockSpec((1, D), lambda i: (0, 0))
    hbm = pl.BlockSpec(memory_space=pl.ANY)
    return pl.pallas_call(
        body, name="input_grad_block", grid=(nt,),
        in_specs=[pl.BlockSpec((TR, 9 * D), lambda i: (i, 0)),
                  pl.BlockSpec((TR, D), lambda i: (jnp.maximum(i - 1, 0), 0)), pl.BlockSpec((TR, D), lambda i: (0, 0)),
                  row, vec, pl.BlockSpec(memory_space=pltpu.VMEM)] + [hbm] * nex,
        out_specs=[pl.BlockSpec((TR, D), lambda i: (jnp.maximum(i - 1, 0), 0)),
                   pl.BlockSpec((NMETA, D), lambda i: (0, 0)), vec] + [hbm] * nex,
        out_shape=[jax.ShapeDtypeStruct((T - TR, D), F32), jax.ShapeDtypeStruct((NMETA, D), F32),
                   jax.ShapeDtypeStruct((1, D), F32)]
        + [jax.ShapeDtypeStruct((3,) + p.shape[1:], p.dtype) for p in psums],
        scratch_shapes=[pltpu.SemaphoreType.DMA((nex, 3)), pltpu.SemaphoreType.DMA((nex, 3))],
        compiler_params=_cparams(("arbitrary",)),
    )(dproj, x_seq, head_tile, dres, w, w_in_g, *psums)


def _tri_matmul(tri_bf16, x):
    hi = x.astype(BF16)
    lo = (x - hi.astype(F32)).astype(BF16)
    return jnp.dot(tri_bf16, lo, preferred_element_type=F32) + jnp.dot(tri_bf16, hi, preferred_element_type=F32)


def _shift_down(x, prev8, n):
    rows = x.shape[0]
    return pltpu.roll(jnp.concatenate([prev8, x], axis=0), n, 0)[8:8 + rows, :]


def _shift_up(x, next8, n):
    rows = x.shape[0]
    return pltpu.roll(jnp.concatenate([x, next8], axis=0), rows + 8 - n, 0)[0:rows, :]


def _gates(f_raw, lb):
    sg = _sigmoid(f_raw)
    f = lb + (1.0 - lb) * sg
    return sg, f, jnp.log(f), 1.0 - f


def _lower_bound(lbp_ref):
    return _sigmoid(lbp_ref[0:1, :] - lbp_ref[1:2, :])


def _tri_masks():
    r = lax.broadcasted_iota(jnp.int32, (CH, CH), 0)
    c = lax.broadcasted_iota(jnp.int32, (CH, CH), 1)
    return r >= c, r <= c


def _ones_where(mask):
    return jnp.where(mask, 1.0, 0.0).astype(BF16)


def _dot(a, b):
    return jnp.dot(a.astype(BF16), b.astype(BF16), preferred_element_type=F32)


def _dot_nt(a, b):
    return lax.dot_general(a.astype(BF16), b.astype(BF16), (((1,), (1,)), ((), ())), preferred_element_type=F32)


def _dot_tn(a, b):
    return lax.dot_general(a.astype(BF16), b.astype(BF16), (((0,), (0,)), ((), ())), preferred_element_type=F32)


def _mix_block_fwd(x_seq, head_tile, lb_param, wattn, whn, conv_w, w_in_g, w_out, late, late_paired):
    T = TR + x_seq.shape[0]
    nt = T // TR
    ncht = TR // CH
    nsl = w_in_g.shape[0]
    wsl = w_in_g.shape[2]
    wcol = wsl // 2

    nlate = len(late)

    def body(*refs):
        x_ref, head_ref, lbp_ref, wa_ref, whn_ref, cw_ref, win_ref, wout_ref = refs[:8]
        ut_ref, projb_ref, o_ref, sst_ref, mt_ref, h1_ref = refs[8 + nlate:14 + nlate]
        late_refs = refs[14 + nlate:14 + 2 * nlate]
        proj_ref, m_ref, st_ref, cxc_ref, send_sems, recv_sems = refs[14 + 2 * nlate:]
        i = pl.program_id(0)
        gather = _Gather(late_refs, send_sems, recv_sems, (False,) * nlate, late_paired)

        @pl.when(i == 0)
        def _():
            st_ref[...] = jnp.zeros_like(st_ref)
            cxc_ref[...] = jnp.zeros_like(cxc_ref)
            gather.send()

        @pl.when(i == nt // 2)
        def _():
            gather.relay()

        x = jnp.where(i == 0, head_ref[...], x_ref[...])
        r1 = lax.rsqrt(jnp.mean(x * x, axis=-1, keepdims=True) + EPS)
        u_f = x * r1 * wa_ref[...]
        u = u_f.astype(BF16)
        ut_ref[...] = u_f.T.astype(BF16)
        for s in range(nsl):
            cols = slice(s * wsl, (s + 1) * wsl)
            p = jnp.dot(u, win_ref[s], preferred_element_type=F32)
            proj_ref[:, cols] = p
            projb_ref[:, cols] = p.astype(BF16)

        lb = _lower_bound(lbp_ref)
        lower, _ = _tri_masks()
        ltri = _ones_where(lower)
        whn_v = whn_ref[...]
        w0, w1, w2 = cw_ref[0:1, :], cw_ref[1:2, :], cw_ref[2:3, :]

        def chunk(c, carry):
            rows = pl.ds(pl.multiple_of(c * CH, CH), CH)
            q_raw = proj_ref[rows, 0:D]
            f_raw = proj_ref[rows, D:2 * D]
            v = proj_ref[rows, 2 * D:3 * D]
            q = q_raw * _sigmoid(q_raw)
            _, _, g, k = _gates(f_raw, lb)
            gam = _tri_matmul(ltri, g)
            gam_l = gam[CH - 1:CH, :]
            e_l = jnp.exp(gam_l)
            qt = (q * jnp.exp(gam)).astype(BF16)
            kt = (k * jnp.exp(-gam)).astype(BF16)
            khat = (k * jnp.exp(gam_l - gam)).astype(BF16)
            vb = v.astype(BF16)
            heads = [slice(h * HD, (h + 1) * HD) for h in range(NH)]
            sts = [st_ref[h] for h in range(NH)]
            scores = [_dot_nt(qt[:, cs], kt[:, cs]) for cs in heads]
            o_state = [_dot_nt(qt[:, cs], st) for cs, st in zip(heads, sts)]
            st_new = [_dot_tn(vb[:, cs], khat[:, cs]) for cs in heads]
            on_parts = []
            for h, cs in enumerate(heads):
                sst_ref[c, h] = sts[h]
                st_ref[h] = sts[h] * e_l[:, cs] + st_new[h]
                o_h = o_state[h] + _dot(jnp.where(lower, scores[h], 0.0), vb[:, cs])
                o_ref[rows, cs] = o_h
                ro = lax.rsqrt(jnp.mean(o_h * o_h, axis=-1, keepdims=True) + EPS)
                on_parts.append(o_h * ro)
            on = jnp.concatenate(on_parts, axis=1)
            g_out = proj_ref[rows, 3 * D:4 * D]
            y_a = on * whn_v * (g_out * _sigmoid(g_out))
            cx = proj_ref[rows, 5 * D:6 * D] * proj_ref[rows, 6 * D:7 * D]
            prev8 = cxc_ref[...]
            cv = w0 * _shift_down(cx, prev8, 2) + w1 * _shift_down(cx, prev8, 1) + w2 * cx
            cxc_ref[...] = cx[CH - 8:CH, :]
            y_b = proj_ref[rows, 4 * D:5 * D] * cv
            m = _sigmoid(proj_ref[rows, 7 * D:8 * D]) * y_a + _sigmoid(proj_ref[rows, 8 * D:9 * D]) * y_b
            m_ref[rows, :] = m
            return carry

        lax.fori_loop(0, ncht, chunk, 0, unroll=True)
        m_v = m_ref[...]
        h1_ref[...] = x + jnp.dot(m_v.astype(BF16), wout_ref[...], preferred_element_type=F32)
        mt_ref[...] = m_v.T.astype(BF16)

        @pl.when(i == nt - 1)
        def _():
            gather.finish()

    row = lambda w: pl.BlockSpec((TR, w), lambda i: (i, 0))
    col = lambda w: pl.BlockSpec((w, TR), lambda i: (0, i))
    vec = lambda r: pl.BlockSpec((r, D), lambda i: (0, 0))
    vm = pl.BlockSpec(memory_space=pltpu.VMEM)
    hbm = pl.BlockSpec(memory_space=pl.ANY)
    return pl.pallas_call(
        body, name="mix_block_fwd", grid=(nt,),
        in_specs=[pl.BlockSpec((TR, D), lambda i: (jnp.maximum(i - 1, 0), 0)), pl.BlockSpec((TR, D), lambda i: (0, 0)),
                  vec(2), vec(1), vec(1), vec(3), vm, vm] + [hbm] * nlate,
        out_specs=[col(D), row(9 * D), row(D), pl.BlockSpec((ncht, NH, HD, HD), lambda i: (i, 0, 0, 0)),
                   col(D), row(D)] + [hbm] * nlate,
        out_shape=[jax.ShapeDtypeStruct((D, T), BF16), jax.ShapeDtypeStruct((T, 9 * D), BF16),
                   jax.ShapeDtypeStruct((T, D), F32), jax.ShapeDtypeStruct((T // CH, NH, HD, HD), F32),
                   jax.ShapeDtypeStruct((D, T), BF16), jax.ShapeDtypeStruct((T, D), F32)]
        + [jax.ShapeDtypeStruct(a.shape, a.dtype) for a in late],
        input_output_aliases={8 + n: 6 + n for n in range(nlate)},
        scratch_shapes=[pltpu.VMEM((TR, 9 * D), F32), pltpu.VMEM((TR, D), F32), pltpu.VMEM((NH, HD, HD), F32),
                        pltpu.VMEM((8, D), F32), pltpu.SemaphoreType.DMA((nlate, 6)),
                        pltpu.SemaphoreType.DMA((nlate, 6))],
        compiler_params=_cparams(("arbitrary",)),
    )(x_seq, head_tile, lb_param, wattn, whn, conv_w, w_in_g, w_out, *late)


def _mix_block_bwd(dh1, projb, o, sst, lb_param, whn, conv_w, w_out, psums):
    T = projb.shape[0]
    nt = T // TR
    ncht = TR // CH
    tb16 = TR // 16
    nex = len(psums)

    def body(*refs):
        dh1_ref, proj_ref, pc_ref, px_ref, o_ref, sst_ref, lbp_ref, whn_ref, cw_ref, wout_ref = refs[:10]
        dproj_ref, dlb_ref, dwhn_ref, dcw_ref = refs[10 + nex:14 + nex]
        exchange = _ChipExchange(refs[10:10 + nex], refs[14 + nex:14 + 2 * nex], refs[-2], refs[-1])
        dm_ref, dst_ref, dcvc_ref, acc_lb, acc_hn, acc_cw = refs[14 + 2 * nex:-2]
        s = pl.program_id(0)
        tile = nt - 1 - s

        @pl.when(s == 0)
        def _():
            dst_ref[...] = jnp.zeros_like(dst_ref)
            dcvc_ref[...] = jnp.zeros_like(dcvc_ref)
            acc_lb[...] = jnp.zeros_like(acc_lb)
            acc_hn[...] = jnp.zeros_like(acc_hn)
            acc_cw[...] = jnp.zeros_like(acc_cw)
            exchange.send()

        dm_ref[...] = _dot_nt(dh1_ref[...], wout_ref[...])
        lb = _lower_bound(lbp_ref)
        lower, upper = _tri_masks()
        ltri = _ones_where(lower)
        utri = _ones_where(upper)
        whn_v = whn_ref[...]
        w0, w1, w2 = cw_ref[0:1, :], cw_ref[1:2, :], cw_ref[2:3, :]
        cx_before_tile = jnp.where(tile > 0, (pc_ref[...].astype(F32) * px_ref[...].astype(F32))[8:16, :], 0.0)
        rid = lax.broadcasted_iota(jnp.int32, (CH, D), 0)

        def chunk(cc, carry):
            c = ncht - 1 - cc
            r0 = pl.multiple_of(c * CH, CH)
            rows = pl.ds(r0, CH)
            slab = lambda n: proj_ref[rows, n * D:(n + 1) * D].astype(F32)
            q_raw, f_raw, v, g_out, b_gate, c_gate, x_conv = (slab(n) for n in range(7))
            sa = _sigmoid(slab(7))
            sb = _sigmoid(slab(8))
            dm_v = dm_ref[rows, :]

            sq = _sigmoid(q_raw)
            q = q_raw * sq
            sg, f, g, k = _gates(f_raw, lb)
            gam = _tri_matmul(ltri, g)
            gam_l = gam[CH - 1:CH, :]
            e_l = jnp.exp(gam_l)
            e_g = jnp.exp(gam)
            e_ng = jnp.exp(-gam)
            e_kl = jnp.exp(gam_l - gam)
            qt = q * e_g
            kt = k * e_ng
            khat = k * e_kl
            qt_b, kt_b, khat_b, v_b = qt.astype(BF16), kt.astype(BF16), khat.astype(BF16), v.astype(BF16)
            qt_seen, kt_seen = qt_b.astype(F32), kt_b.astype(F32)
            s_go = _sigmoid(g_out)
            silu_go = g_out * s_go
            cx = c_gate * x_conv
            rprev = pl.ds(pl.multiple_of(jnp.maximum(r0 - 16, 0), 16), 16)
            cx_prev_in = (proj_ref[rprev, 5 * D:6 * D].astype(F32) * proj_ref[rprev, 6 * D:7 * D].astype(F32))[8:16, :]
            prev8 = jnp.where(c > 0, cx_prev_in, cx_before_tile)
            cx_m1 = _shift_down(cx, prev8, 1)
            cx_m2 = _shift_down(cx, prev8, 2)
            cv = w0 * cx_m2 + w1 * cx_m1 + w2 * cx
            y_b = b_gate * cv

            o_v = o_ref[rows, :]
            ro_parts, on_parts = [], []
            for h in range(NH):
                cs = slice(h * HD, (h + 1) * HD)
                o_h = o_v[:, cs]
                ro = lax.rsqrt(jnp.mean(o_h * o_h, axis=-1, keepdims=True) + EPS)
                ro_parts.append(ro)
                on_parts.append(o_h * ro)
            on = jnp.concatenate(on_parts, axis=1)
            y_a = on * whn_v * silu_go

            dy_a = dm_v * sa
            dy_b = dm_v * sb
            dproj_ref[rows, 7 * D:8 * D] = (dm_v * y_a * sa * (1.0 - sa)).astype(BF16)
            dproj_ref[rows, 8 * D:9 * D] = (dm_v * y_b * sb * (1.0 - sb)).astype(BF16)
            dproj_ref[rows, 4 * D:5 * D] = (dy_b * cv).astype(BF16)
            dcv = dy_b * b_gate
            acc_cw[0:1, :] += jnp.sum(dcv * cx_m2, axis=0, keepdims=True)
            acc_cw[1:2, :] += jnp.sum(dcv * cx_m1, axis=0, keepdims=True)
            acc_cw[2:3, :] += jnp.sum(dcv * cx, axis=0, keepdims=True)
            next8 = dcvc_ref[...]
            dcx = w2 * dcv + w1 * _shift_up(dcv, next8, 1) + w0 * _shift_up(dcv, next8, 2)
            dcvc_ref[...] = dcv[0:8, :]
            dproj_ref[rows, 5 * D:6 * D] = (dcx * x_conv).astype(BF16)
            dproj_ref[rows, 6 * D:7 * D] = (dcx * c_gate).astype(BF16)
            don = dy_a * whn_v * silu_go
            dproj_ref[rows, 3 * D:4 * D] = (dy_a * on * whn_v * (s_go * (1.0 + g_out * (1.0 - s_go)))).astype(BF16)
            acc_hn[...] += jnp.sum(dy_a * silu_go * on, axis=0, keepdims=True)

            heads = [slice(h * HD, (h + 1) * HD) for h in range(NH)]
            do_bs, sts, dstns, first = [], [], [], []
            for h, cs in enumerate(heads):
                on_h = on_parts[h]
                don_h = don[:, cs]
                do_h = ro_parts[h] * (don_h - on_h * jnp.mean(don_h * on_h, axis=-1, keepdims=True))
                do_bs.append(do_h.astype(BF16))
                sts.append(sst_ref[c, h])
                dstns.append(dst_ref[h])
            for h, cs in enumerate(heads):
                qt_h, kt_h, khat_h, v_h = qt_b[:, cs], kt_b[:, cs], khat_b[:, cs], v_b[:, cs]
                do_b, dstn_b = do_bs[h], dstns[h].astype(BF16)
                first.append(dict(
                    a_t=_dot_nt(kt_h, qt_h), da=_dot_nt(do_b, v_h), da_t=_dot_nt(v_h, do_b),
                    dv_state=_dot_nt(khat_h, dstn_b), dqt_state=_dot(do_b, sts[h]), dkhat=_dot(v_h, dstn_b),
                    dst_chunk=_dot_tn(do_b, qt_h)))
            dq_parts, dk_parts, dv_parts, dgam_parts, ext_parts = [], [], [], [], []
            for h, cs in enumerate(heads):
                qt_h, kt_h = qt_b[:, cs], kt_b[:, cs]
                p1, do_b, st, dstn = first[h], do_bs[h], sts[h], dstns[h]
                dv_h = _dot(jnp.where(upper, p1["a_t"], 0.0), do_b) + p1["dv_state"]
                dqt_state = p1["dqt_state"]
                dqt_chunk = _dot(jnp.where(lower, p1["da"], 0.0), kt_h)
                dkt = _dot(jnp.where(upper, p1["da_t"], 0.0), qt_h)
                dkhat = p1["dkhat"]
                dq_h = (dqt_state + dqt_chunk) * e_g[:, cs]
                dk_h = dkt * e_ng[:, cs] + dkhat * e_kl[:, cs]
                khat_dkhat = dkhat * khat[:, cs]
                ext = (jnp.sum(khat_dkhat, axis=0, keepdims=True)
                       + e_l[:, cs] * jnp.sum(st * dstn, axis=0, keepdims=True))
                dst_ref[h] = p1["dst_chunk"] + dstn * e_l[:, cs]
                dq_parts.append(dq_h)
                dk_parts.append(dk_h)
                dv_parts.append(dv_h)
                dgam_parts.append(qt[:, cs] * dqt_state + qt_seen[:, cs] * dqt_chunk - kt_seen[:, cs] * dkt
                                  - khat_dkhat)
                ext_parts.append(ext)
            dq = jnp.concatenate(dq_parts, axis=1)
            dk = jnp.concatenate(dk_parts, axis=1)
            dgam = jnp.concatenate(dgam_parts, axis=1)
            ext = jnp.concatenate(ext_parts, axis=1)
            dgam = dgam + jnp.where(rid == CH - 1, ext, 0.0)
            dg = _tri_matmul(utri, dgam)
            dproj_ref[rows, 0:D] = (dq * (sq * (1.0 + q_raw * (1.0 - sq)))).astype(BF16)
            df = dg * jnp.exp(-g) - dk
            dproj_ref[rows, D:2 * D] = (df * (1.0 - lb) * sg * (1.0 - sg)).astype(BF16)
            dproj_ref[rows, 2 * D:3 * D] = jnp.concatenate(dv_parts, axis=1).astype(BF16)
            real = (tile * TR + r0 + rid) >= PAD
            acc_lb[...] += jnp.sum(jnp.where(real, df * (1.0 - sg), 0.0), axis=0, keepdims=True)
            return carry

        lax.fori_loop(0, ncht, chunk, 0)

        @pl.when(s == nt - 1)
        def _():
            dlb_ref[...] = acc_lb[...] * lb * (1.0 - lb)
            hn = acc_hn[...]
            tot = hn[:, 0:HD]
            for h in range(1, NH):
                tot = tot + hn[:, h * HD:(h + 1) * HD]
            dwhn_ref[...] = tot
            dcw_ref[...] = acc_cw[0:3, :]
            exchange.finish()

    hbm = pl.BlockSpec(memory_space=pl.ANY)
    rev = lambda s: (nt - 1 - s, 0)
    prevc = lambda s: (jnp.maximum((nt - 1 - s) * tb16 - 1, 0), 5)
    prevx = lambda s: (jnp.maximum((nt - 1 - s) * tb16 - 1, 0), 6)
    const = lambda s: (0, 0)
    return pl.pallas_call(
        body, name="mix_block_bwd", grid=(nt,),
        in_specs=[pl.BlockSpec((TR, D), rev),
                  pl.BlockSpec((TR, 9 * D), rev),
                  pl.BlockSpec((16, D), prevc),
                  pl.BlockSpec((16, D), prevx),
                  pl.BlockSpec((TR, D), rev),
                  pl.BlockSpec((ncht, NH, HD, HD), lambda s: (nt - 1 - s, 0, 0, 0)),
                  pl.BlockSpec((2, D), const),
                  pl.BlockSpec((1, D), const),
                  pl.BlockSpec((3, D), const),
                  pl.BlockSpec(memory_space=pltpu.VMEM)] + [hbm] * nex,
        out_specs=[pl.BlockSpec((TR, 9 * D), rev),
                   pl.BlockSpec((1, D), const),
                   pl.BlockSpec((1, HD), const),
                   pl.BlockSpec((3, D), const)] + [hbm] * nex,
        out_shape=[jax.ShapeDtypeStruct((T, 9 * D), BF16), jax.ShapeDtypeStruct((1, D), F32),
                   jax.ShapeDtypeStruct((1, HD), F32), jax.ShapeDtypeStruct((3, D), F32)]
        + [jax.ShapeDtypeStruct((3,) + p.shape[1:], p.dtype) for p in psums],
        scratch_shapes=[pltpu.VMEM((TR, D), F32), pltpu.VMEM((NH, HD, HD), F32), pltpu.VMEM((8, D), F32),
                        pltpu.VMEM((1, D), F32), pltpu.VMEM((1, D), F32), pltpu.VMEM((8, D), F32),
                        pltpu.SemaphoreType.DMA((nex, 3)), pltpu.SemaphoreType.DMA((nex, 3))],
        compiler_params=_cparams(("arbitrary",)),
    )(dh1, projb, projb, projb, o, sst, lb_param, whn, conv_w, w_out, *psums)


def _ffn_block_fwd(h1, tgt, wffn, w_up_g, fcw, fcb, w_down, wfin):
    T = h1.shape[0]
    nt = T // TR
    nsl = w_up_g.shape[0]
    wsl = w_up_g.shape[2]

    def body(h_ref, t_ref, wn_ref, wup_ref, cw_ref, cb_ref, wdn_ref, wf_ref,
             u2t_ref, upb_ref, ggt_ref, dh_ref, loss_ref, dwf_ref, up_scr, gg_ref, carry_ref):
        i = pl.program_id(0)

        @pl.when(i == 0)
        def _():
            carry_ref[...] = jnp.zeros_like(carry_ref)
            loss_ref[...] = jnp.zeros_like(loss_ref)
            dwf_ref[...] = jnp.zeros_like(dwf_ref)

        x = h_ref[...]
        r2 = lax.rsqrt(jnp.mean(x * x, axis=-1, keepdims=True) + EPS)
        u2_f = x * r2 * wn_ref[...]
        u2 = u2_f.astype(BF16)
        u2t_ref[...] = u2_f.T.astype(BF16)
        for s in range(nsl):
            up_s = jnp.dot(u2, wup_ref[s], preferred_element_type=F32)
            up_scr[:, s * wsl:(s + 1) * wsl] = up_s
            upb_ref[:, s * wsl:(s + 1) * wsl] = up_s.astype(BF16)
        w0, w1, w2 = cw_ref[0:1, :], cw_ref[1:2, :], cw_ref[2:3, :]

        def chunk(c, carry):
            rows = pl.ds(pl.multiple_of(c * CH, CH), CH)
            a_pre = up_scr[rows, 0:DFF]
            val = up_scr[rows, DFF:2 * DFF]
            prev8 = carry_ref[...]
            a = w0 * _shift_down(a_pre, prev8, 2) + w1 * _shift_down(a_pre, prev8, 1) + w2 * a_pre + cb_ref[...]
            carry_ref[...] = a_pre[CH - 8:CH, :]
            gg_ref[rows, :] = a * _sigmoid(a) * val
            return carry

        lax.fori_loop(0, TR // CH, chunk, 0, unroll=True)
        gg_v = gg_ref[...]
        ggt_ref[...] = gg_v.T.astype(BF16)
        h2 = x + jnp.dot(gg_v.astype(BF16), wdn_ref[...], preferred_element_type=F32)
        r3 = lax.rsqrt(jnp.mean(h2 * h2, axis=-1, keepdims=True) + EPS)
        n3 = h2 * r3
        wf = wf_ref[...]
        diff = jnp.where(i > 0, n3 * wf - t_ref[...], 0.0)
        loss_ref[...] += jnp.sum(diff * diff, axis=0, keepdims=True) * (0.5 / D)
        dy = diff * (1.0 / D)
        dwf_ref[...] += jnp.sum(dy * n3, axis=0, keepdims=True)
        dn = dy * wf
        dh_ref[...] = r3 * (dn - n3 * jnp.mean(dn * n3, axis=-1, keepdims=True))

    row = lambda w: pl.BlockSpec((TR, w), lambda i: (i, 0))
    col = lambda w: pl.BlockSpec((w, TR), lambda i: (0, i))
    vec = lambda w, r=1: pl.BlockSpec((r, w), lambda i: (0, 0))
    vm = pl.BlockSpec(memory_space=pltpu.VMEM)
    return pl.pallas_call(
        body, name="ffn_block_fwd", grid=(nt,),
        in_specs=[row(D), pl.BlockSpec((TR, D), lambda i: (jnp.maximum(i - 1, 0), 0)), vec(D), vm,
                  vec(DFF, 3), vec(DFF), vm, vec(D)],
        out_specs=[col(D), row(2 * DFF), col(DFF), row(D), vec(D), vec(D)],
        out_shape=[jax.ShapeDtypeStruct((D, T), BF16), jax.ShapeDtypeStruct((T, 2 * DFF), BF16),
                   jax.ShapeDtypeStruct((DFF, T), BF16), jax.ShapeDtypeStruct((T, D), F32),
                   jax.ShapeDtypeStruct((1, D), F32), jax.ShapeDtypeStruct((1, D), F32)],
        scratch_shapes=[pltpu.VMEM((TR, 2 * DFF), F32), pltpu.VMEM((TR, DFF), F32), pltpu.VMEM((8, DFF), F32)],
        compiler_params=_cparams(("arbitrary",)),
    )(h1, tgt, wffn, w_up_g, fcw, fcb, w_down, wfin)


def _ffn_block_bwd(dh2, upb, h1, wffn, w_up_g, fcw, fcb, w_down):
    T = h1.shape[0]
    nt = T // TR
    ncht = TR // CH
    nsl = w_up_g.shape[0]
    wsl = w_up_g.shape[2]
    tb16 = TR // 16
    assert ncht % nsl == 0
    every = ncht // nsl
    step = -(-DFF // (ncht * 128)) * 128
    parts = [(c0, min(c0 + step, DFF)) for c0 in range(0, DFF, step)]
    assert len(parts) == ncht

    def body(dh2n_ref, dh2p_ref, up_ref, pa_ref, h_ref, wn_ref, wup_ref, cw_ref, cb_ref, wdn_ref,
             dup_ref, dh1_ref, dfw_ref, dfb_ref, dwn_ref,
             dgg_ring, dup_ring, carry_ref, acc_w, acc_b, acc_n):
        s = pl.program_id(0)
        slot = lax.rem(s, 2)
        other = 1 - slot

        @pl.when(s == 0)
        def _():
            carry_ref[...] = jnp.zeros_like(carry_ref)
            acc_w[...] = jnp.zeros_like(acc_w)
            acc_b[...] = jnp.zeros_like(acc_b)
            acc_n[...] = jnp.zeros_like(acc_n)
            dup_ring[1] = jnp.zeros((TR, 2 * DFF), BF16)
            dgg_ring[0] = _dot_nt(dh2p_ref[...], wdn_ref[...])

        def norm_bwd(du2, valid):
            x = h_ref[...]
            dh2p = dh2p_ref[...]
            r2 = lax.rsqrt(jnp.mean(x * x, axis=-1, keepdims=True) + EPS)
            n2 = x * r2
            dn = du2 * wn_ref[...]
            dh1_ref[...] = dh2p + r2 * (dn - n2 * jnp.mean(dn * n2, axis=-1, keepdims=True))
            acc_n[...] += jnp.where(valid, jnp.sum(du2 * n2, axis=0, keepdims=True), 0.0)

        @pl.when(s < nt)
        def _():
            tile = nt - 1 - s
            w0, w1, w2 = cw_ref[0:1, :], cw_ref[1:2, :], cw_ref[2:3, :]
            a_before_tile = jnp.where(tile > 0, pa_ref[...].astype(F32)[8:16, :], 0.0)
            dh2n = dh2n_ref[...].astype(BF16)
            du2 = None
            for idx in range(ncht):
                c = ncht - 1 - idx
                r0 = c * CH
                rows = slice(r0, r0 + CH)
                a_pre = up_ref[rows, 0:DFF].astype(F32)
                val = up_ref[rows, DFF:2 * DFF].astype(F32)
                prev8 = up_ref[r0 - 16:r0, 0:DFF].astype(F32)[8:16, :] if c > 0 else a_before_tile
                a_m1 = _shift_down(a_pre, prev8, 1)
                a_m2 = _shift_down(a_pre, prev8, 2)
                a = w0 * a_m2 + w1 * a_m1 + w2 * a_pre + cb_ref[...]
                sig = _sigmoid(a)
                dgg_v = dgg_ring[slot, rows, :]
                da = dgg_v * val * (sig * (1.0 + a * (1.0 - sig)))
                dval = (dgg_v * (a * sig)).astype(BF16)
                next8 = carry_ref[...]
                da_pre = (w2 * da + w1 * _shift_up(da, next8, 1) + w0 * _shift_up(da, next8, 2)).astype(BF16)
                carry_ref[...] = da[0:8, :]
                dup_ref[rows, 0:DFF] = da_pre
                dup_ref[rows, DFF:2 * DFF] = dval
                dup_ring[slot, rows, 0:DFF] = da_pre
                dup_ring[slot, rows, DFF:2 * DFF] = dval
                acc_w[0:1, :] += jnp.sum(da * a_m2, axis=0, keepdims=True)
                acc_w[1:2, :] += jnp.sum(da * a_m1, axis=0, keepdims=True)
                acc_w[2:3, :] += jnp.sum(da * a_pre, axis=0, keepdims=True)
                acc_b[...] += jnp.sum(da, axis=0, keepdims=True)
                c0, c1 = parts[idx]
                dgg_ring[other, :, c0:c1] = _dot_nt(dh2n, wdn_ref[c0:c1, :])
                if idx % every == 0:
                    sl = idx // every
                    part = _dot_nt(dup_ring[other, :, sl * wsl:(sl + 1) * wsl], wup_ref[sl])
                    du2 = part if du2 is None else du2 + part
            norm_bwd(du2, s > 0)

        @pl.when(s == nt)
        def _():
            du2 = _dot_nt(dup_ring[other, :, 0:wsl], wup_ref[0])
            for sl in range(1, nsl):
                du2 = du2 + _dot_nt(dup_ring[other, :, sl * wsl:(sl + 1) * wsl], wup_ref[sl])
            norm_bwd(du2, True)
            dfw_ref[...] = acc_w[0:3, :]
            dfb_ref[...] = acc_b[...]
            dwn_ref[...] = acc_n[...]

    gate_tile = lambda s: jnp.maximum(nt - 1 - s, 0)
    next_tile = lambda s: jnp.maximum(nt - 2 - s, 0)
    prev_tile = lambda s: jnp.minimum(nt - s, nt - 1)
    vec = lambda w, r=1: pl.BlockSpec((r, w), lambda s: (0, 0))
    vm = pl.BlockSpec(memory_space=pltpu.VMEM)
    return pl.pallas_call(
        body, name="ffn_block_bwd", grid=(nt + 1,),
        in_specs=[pl.BlockSpec((TR, D), lambda s: (next_tile(s), 0)),
                  pl.BlockSpec((TR, D), lambda s: (prev_tile(s), 0)),
                  pl.BlockSpec((TR, 2 * DFF), lambda s: (gate_tile(s), 0)),
                  pl.BlockSpec((16, DFF), lambda s: (jnp.maximum(gate_tile(s) * tb16 - 1, 0), 0)),
                  pl.BlockSpec((TR, D), lambda s: (prev_tile(s), 0)),
                  vec(D), vm, vec(DFF, 3), vec(DFF), vm],
        out_specs=[pl.BlockSpec((TR, 2 * DFF), lambda s: (gate_tile(s), 0)),
                   pl.BlockSpec((TR, D), lambda s: (prev_tile(s), 0)),
                   vec(DFF, 3), vec(DFF), vec(D)],
        out_shape=[jax.ShapeDtypeStruct((T, 2 * DFF), BF16), jax.ShapeDtypeStruct((T, D), F32),
                   jax.ShapeDtypeStruct((3, DFF), F32), jax.ShapeDtypeStruct((1, DFF), F32),
                   jax.ShapeDtypeStruct((1, D), F32)],
        scratch_shapes=[pltpu.VMEM((2, TR, DFF), F32), pltpu.VMEM((2, TR, 2 * DFF), BF16),
                        pltpu.VMEM((8, DFF), F32), pltpu.VMEM((8, DFF), F32), pltpu.VMEM((1, DFF), F32),
                        pltpu.VMEM((1, D), F32)],
        compiler_params=_cparams(("arbitrary",)),
    )(dh2, dh2, upb, upb, h1, wffn, w_up_g, fcw, fcb, w_down)


def _place():
    x, y, c = lax.axis_index("x"), lax.axis_index("y"), lax.axis_index("c")
    return x, y, c


_CHIP_FLIPS = ((1, 0), (0, 1), (1, 1))


def _flip(v, bit):
    return 1 - v if bit else v


def _into_slot(w, j_idx, *, rb, dtype, name, paired=False):
    r, cdim = w.shape

    def body(j_ref, w_ref, out_ref):
        del j_ref
        out_ref[...] = w_ref[...].astype(dtype)

    if paired:
        out_shape = jax.ShapeDtypeStruct((NSHARD // 2, r, 2 * cdim), dtype)
        out_spec = pl.BlockSpec((None, rb, cdim), lambda i, j_ref: (j_ref[0] // 2, i, j_ref[0] % 2))
    else:
        out_shape = jax.ShapeDtypeStruct((NSHARD, r, cdim), dtype)
        out_spec = pl.BlockSpec((None, rb, cdim), lambda i, j_ref: (j_ref[0], i, 0))
    grid_spec = pltpu.PrefetchScalarGridSpec(
        num_scalar_prefetch=1, grid=(r // rb,),
        in_specs=[pl.BlockSpec((rb, cdim), lambda i, j_ref: (i, 0))], out_specs=out_spec)
    return pl.pallas_call(
        body, name=name, grid_spec=grid_spec, out_shape=out_shape, compiler_params=_cparams(("parallel",)),
    )(j_idx, w)


class _Gather:
    def __init__(self, outs, send_sems, recv_sems, whole, paired=None):
        self.outs, self.send_sems, self.recv_sems, self.whole = outs, send_sems, recv_sems, whole
        self.paired = paired if paired is not None else (False,) * len(outs)
        self.x, self.y, self.c = _place()
        self.j = 2 * self.x + self.y
        self.sibling = (self.x, self.y, 1 - self.c)

    def _rows(self, w, core):
        r = self.outs[w].shape[1]
        return pl.ds(0, r) if self.whole[w] else pl.ds(core * (r // 2), r // 2)

    def _copy(self, w, slot, core, sem, to):
        if self.paired[w]:
            cw = self.outs[w].shape[2] // 2
            piece = self.outs[w].at[slot // 2, self._rows(w, core), pl.ds((slot % 2) * cw, cw)]
        else:
            piece = self.outs[w].at[slot, self._rows(w, core), :]
        return pltpu.make_async_remote_copy(
            src_ref=piece, dst_ref=piece, send_sem=self.send_sems.at[w, sem], recv_sem=self.recv_sems.at[w, sem],
            device_id=to, device_id_type=MESH)

    def _chips(self):
        for kk, (fx, fy) in enumerate(_CHIP_FLIPS):
            px, py = _flip(self.x, fx), _flip(self.y, fy)
            yield kk, 2 * px + py, (px, py, self.c)

    def send(self):
        for w in range(len(self.outs)):
            for kk, _, to in self._chips():
                self._copy(w, self.j, self.c, kk, to).start()

    def relay(self):
        for w in range(len(self.outs)):
            for kk, jk, _ in self._chips():
                self._copy(w, jk, self.c, kk, self.sibling).wait_recv()
                if not self.whole[w]:
                    self._copy(w, jk, self.c, 3 + kk, self.sibling).start()

    def finish(self):
        for w in range(len(self.outs)):
            for kk, jk, to in self._chips():
                self._copy(w, self.j, self.c, kk, to).wait_send()
                if not self.whole[w]:
                    self._copy(w, jk, 1 - self.c, 3 + kk, self.sibling).wait_recv()
                    self._copy(w, jk, self.c, 3 + kk, self.sibling).wait_send()


def _allgather_weights(slotted, whole):
    n = len(slotted)

    def body(*refs):
        g = _Gather(refs[n:2 * n], refs[2 * n], refs[2 * n + 1], whole)
        g.send()
        g.relay()
        g.finish()

    any_spec = pl.BlockSpec(memory_space=pl.ANY)
    return pl.pallas_call(
        body, name="allgather_weights",
        in_specs=[any_spec] * n, out_specs=[any_spec] * n,
        out_shape=[jax.ShapeDtypeStruct(a.shape, a.dtype) for a in slotted],
        input_output_aliases={i: i for i in range(n)},
        scratch_shapes=[pltpu.SemaphoreType.DMA((n, 6)), pltpu.SemaphoreType.DMA((n, 6))],
    )(*slotted)


class _ChipExchange:
    def __init__(self, ins, outs, send_sems, recv_sems):
        self.ins, self.outs, self.send_sems, self.recv_sems = ins, outs, send_sems, recv_sems
        self.x, self.y, self.c = _place()

    def _copies(self):
        for w in range(len(self.ins)):
            for kk, (fx, fy) in enumerate(_CHIP_FLIPS):
                px, py = _flip(self.x, fx), _flip(self.y, fy)
                yield pltpu.make_async_remote_copy(
                    src_ref=self.ins[w].at[2 * px + py], dst_ref=self.outs[w].at[kk],
                    send_sem=self.send_sems.at[w, kk], recv_sem=self.recv_sems.at[w, kk],
                    device_id=(px, py, self.c), device_id_type=MESH)

    def send(self):
        for cp in self._copies():
            cp.start()

    def finish(self):
        for cp in self._copies():
            cp.wait()


def _pair_exchange(grads, name):
    nw = len(grads)

    def body(*refs):
        ins, outs = refs[:nw], refs[nw:2 * nw]
        send_sems, recv_sems = refs[2 * nw:]
        x, y, c = _place()
        sibling = (x, y, 1 - c)
        cps = []
        for w in range(nw):
            half = ins[w].shape[1] // 2
            cp = pltpu.make_async_remote_copy(
                src_ref=ins[w].at[:, pl.ds((1 - c) * half, half), :], dst_ref=outs[w],
                send_sem=send_sems.at[w], recv_sem=recv_sems.at[w], device_id=sibling, device_id_type=MESH)
            cp.start()
            cps.append(cp)
        for cp in cps:
            cp.wait()

    any_spec = pl.BlockSpec(memory_space=pl.ANY)
    return pl.pallas_call(
        body, name=name,
        in_specs=[any_spec] * nw, out_specs=[any_spec] * nw,
        out_shape=[jax.ShapeDtypeStruct((g.shape[0], g.shape[1] // 2, g.shape[2]), g.dtype) for g in grads],
        scratch_shapes=[pltpu.SemaphoreType.DMA((nw,)), pltpu.SemaphoreType.DMA((nw,))],
    )(*grads)


def _pair_add(g, other, c_idx, *, rb, name):
    S, r, cdim = g.shape
    half = r // 2
    nb = half // rb

    def body(c_ref, g_ref, o_ref, out_ref):
        del c_ref
        out_ref[...] = (g_ref[...].astype(F32) + o_ref[...].astype(F32)).astype(BF16)

    grid_spec = pltpu.PrefetchScalarGridSpec(
        num_scalar_prefetch=1, grid=(S, nb),
        in_specs=[pl.BlockSpec((None, rb, cdim), lambda s, i, c_ref: (s, c_ref[0] * nb + i, 0)),
                  pl.BlockSpec((None, rb, cdim), lambda s, i, c_ref: (s, i, 0))],
        out_specs=pl.BlockSpec((None, rb, cdim), lambda s, i, c_ref: (s, i, 0)))
    return pl.pallas_call(
        body, name=name, grid_spec=grid_spec, out_shape=jax.ShapeDtypeStruct((S, half, cdim), BF16),
        compiler_params=_cparams(("parallel", "parallel")),
    )(c_idx, g, other)


def _chip_sum(psum, parts, cj_idx, *, rb, name):
    S, half, cdim = psum.shape
    nb = half // rb

    def body(cj_ref, own_ref, p_ref, out_ref):
        del cj_ref
        f = lambda v: v.astype(F32)
        out_ref[...] = ((f(own_ref[...]) + f(p_ref[0])) + f(p_ref[1])) + f(p_ref[2])

    grid_spec = pltpu.PrefetchScalarGridSpec(
        num_scalar_prefetch=1, grid=(nb,),
        in_specs=[pl.BlockSpec((None, rb, cdim), lambda i, cj: (cj[1], i, 0)),
                  pl.BlockSpec((3, rb, cdim), lambda i, cj: (0, i, 0))],
        out_specs=pl.BlockSpec((rb, cdim), lambda i, cj: (cj[0] * nb + i, 0)))
    return pl.pallas_call(
        body, name=name, grid_spec=grid_spec, out_shape=jax.ShapeDtypeStruct((2 * half, cdim), F32),
        compiler_params=_cparams(("parallel",)),
    )(cj_idx, psum, parts)


SLAB_W = 1024


def _final_exchange(pieces, grads):
    n, nw = len(pieces), len(grads)
    segs, at = [], 0
    for idx, p in enumerate(pieces):
        r, wd = p.shape
        for c0 in range(0, wd, SLAB_W):
            if r > 1:
                at = -(-at // 8) * 8
            segs.append((idx, c0, min(SLAB_W, wd - c0), at))
            at += r
    rows = -(-at // 8) * 8
    flips = [(fx, fy, fc) for fx in (0, 1) for fy in (0, 1) for fc in (0, 1)][1:]

    def body(*refs):
        ins = refs[:n]
        outs = refs[n + nw:2 * n + nw]
        g_refs = refs[2 * n + nw:2 * n + 2 * nw]
        mine_ref, slots_ref, send_sems, recv_sems, gsend_sems, grecv_sems = refs[2 * n + 2 * nw:]
        x, y, c = _place()
        me = 4 * x + 2 * y + c
        sibling = (x, y, 1 - c)

        def swap(w, core):
            half = g_refs[w].shape[0] // 2
            rows_ = g_refs[w].at[pl.ds(core * half, half), :]
            return pltpu.make_async_remote_copy(
                src_ref=rows_, dst_ref=rows_, send_sem=gsend_sems.at[w], recv_sem=grecv_sems.at[w],
                device_id=sibling, device_id_type=MESH)

        for w in range(nw):
            swap(w, c).start()
        mine_ref[...] = jnp.zeros_like(mine_ref)
        for idx, c0, wd, st in segs:
            r = ins[idx].shape[0]
            mine_ref[st:st + r, 0:wd] = ins[idx][:, c0:c0 + wd]
        slots_ref[me] = mine_ref[...]
        cps = []
        for kk, (fx, fy, fc) in enumerate(flips):
            cp = pltpu.make_async_remote_copy(
                src_ref=mine_ref, dst_ref=slots_ref.at[me], send_sem=send_sems.at[kk], recv_sem=recv_sems.at[kk],
                device_id=(_flip(x, fx), _flip(y, fy), _flip(c, fc)), device_id_type=MESH)
            cp.start()
            cps.append(cp)
        for cp in cps:
            cp.wait()
        tot = slots_ref[0]
        for d in range(1, 8):
            tot = tot + slots_ref[d]
        mine_ref[...] = tot
        for idx, c0, wd, st in segs:
            r = ins[idx].shape[0]
            val = mine_ref[st:st + r, 0:wd]
            if idx == n - 1:
                outs[idx][...] = jnp.sum(val, keepdims=True)
            else:
                outs[idx][:, c0:c0 + wd] = val
        for w in range(nw):
            swap(w, 1 - c).wait_recv()
            swap(w, c).wait_send()

    vm = pl.BlockSpec(memory_space=pltpu.VMEM)
    hbm = pl.BlockSpec(memory_space=pl.ANY)
    out_shape = ([jax.ShapeDtypeStruct(p.shape, F32) for p in pieces[:-1]] + [jax.ShapeDtypeStruct((1, 1), F32)]
                 + [jax.ShapeDtypeStruct(g.shape, g.dtype) for g in grads])
    res = pl.pallas_call(
        body, name="final_exchange", in_specs=[vm] * n + [hbm] * nw, out_specs=[vm] * n + [hbm] * nw,
        out_shape=out_shape, input_output_aliases={n + w: n + w for w in range(nw)},
        scratch_shapes=[pltpu.VMEM((rows, SLAB_W), F32), pltpu.VMEM((8, rows, SLAB_W), F32),
                        pltpu.SemaphoreType.DMA((7,)), pltpu.SemaphoreType.DMA((7,)),
                        pltpu.SemaphoreType.DMA((nw,)), pltpu.SemaphoreType.DMA((nw,))],
    )(*pieces, *grads)
    return res[:n], res[n:]


def _adamw(w, g, m, v, *, rb, name):
    r, cdim = w.shape

    def body(w_ref, g_ref, m_ref, v_ref, go_ref, d_ref, nm_ref, nv_ref):
        go_ref[...] = g_ref[...]
        d_ref[...], nm_ref[...], nv_ref[...] = _adamw_update(w_ref[...], g_ref[...], m_ref[...], v_ref[...])

    spec = pl.BlockSpec((rb, cdim), lambda i: (i, 0))
    shp = jax.ShapeDtypeStruct((r, cdim), F32)
    return pl.pallas_call(
        body, name=name, grid=(r // rb,), in_specs=[spec] * 4, out_specs=[spec] * 4, out_shape=[shp] * 4,
        compiler_params=_cparams(("parallel",)),
    )(w, g, m, v)


def _adamw_update(w, g, m, v):
    nm = ADAM_B1 * m + (1.0 - ADAM_B1) * g
    nv = ADAM_B2 * v + (1.0 - ADAM_B2) * (g * g)
    m_hat = nm / (1.0 - ADAM_B1 ** ADAM_STEP)
    v_hat = nv / (1.0 - ADAM_B2 ** ADAM_STEP)
    return -ADAM_LR * (m_hat / (jnp.sqrt(v_hat) + ADAM_EPS) + ADAM_WD * w), nm, nv


def _adamw_small(params):
    n = len(params)

    def body(*refs):
        ins, outs = refs[:4 * n], refs[4 * n:]
        for p in range(n):
            w_ref, g_ref, m_ref, v_ref = ins[4 * p:4 * p + 4]
            d, nm, nv = _adamw_update(w_ref[...], g_ref[...], m_ref[...], v_ref[...])
            outs[3 * p][...] = d
            outs[3 * p + 1][...] = nm
            outs[3 * p + 2][...] = nv

    vm = pl.BlockSpec(memory_space=pltpu.VMEM)
    flat = [a for p in params for a in p]
    out_shape = [jax.ShapeDtypeStruct(p[0].shape, F32) for p in params for _ in range(3)]
    res = pl.pallas_call(
        body, name="adamw_small", in_specs=[vm] * (4 * n), out_specs=[vm] * (3 * n), out_shape=out_shape,
    )(*flat)
    return [tuple(res[3 * p:3 * p + 3]) for p in range(n)]


_PAIR_ADD_ROWS = {"w_in": 256, "w_out": 128, "w_up": 256, "w_down": 176}


def _pair_sums(grads, names, c_idx, tag):
    others = _pair_exchange(grads, name=f"grad_pair_exchange_{tag}")
    return [_pair_add(g, o_, c_idx, rb=_PAIR_ADD_ROWS[n], name=f"pair_add_{n}") for g, o_, n in zip(grads, others, names)]


def _local_step(x, tgt, meta_full, lb_param, attn_norm_w, w_in_g, hgrn_norm_w, conv_w_full, w_out_full,
                ffn_norm_w, late_slotted, fcw_full, ffn_conv_b, final_norm_w, c_idx, cj_idx):
    seq = x.shape[0]
    T = TR + seq
    head_tile = jnp.concatenate([jnp.zeros((PAD, D), F32), meta_full], axis=0)
    whn_t = jnp.tile(hgrn_norm_w, (1, NH))

    ut, projb, o, sst, mt, h1, w_up_g, w_down_g = _mix_block_fwd(
        x, head_tile, lb_param, attn_norm_w, whn_t, conv_w_full, w_in_g, w_out_full, late_slotted, (True, False))
    w_down_full = w_down_g.reshape(DFF, D)
    u2t, upb, ggt, dh2, loss_vec, dwfin = _ffn_block_fwd(
        h1, tgt, ffn_norm_w, w_up_g, fcw_full, ffn_conv_b, w_down_full, final_norm_w.reshape(1, D))

    dup, dh1, dfw, dfb, dwffn = _ffn_block_bwd(
        dh2, upb, h1, ffn_norm_w, w_up_g, fcw_full, ffn_conv_b, w_down_full)
    kb = 1408 if T % 1408 == 0 else TR
    g_up = _weight_grad(u2t, dup, bn=DFF, bk=kb, name="dw_up_mm", shard_cols=2 * DFF // NSHARD)
    g_down = _weight_grad(ggt, dh2, bn=D // 2, bk=2816 if T % 2816 == 0 else kb, name="dw_down_mm")
    ps_ffn = _pair_sums([g_up, g_down.reshape(NSHARD, DFF // NSHARD, D)], ("w_up", "w_down"), c_idx, "ffn")

    dproj, dlb, dwhn, dcw, *parts_ffn = _mix_block_bwd(
        dh1, projb, o, sst, lb_param, whn_t, conv_w_full, w_out_full, ps_ffn)
    kb_deep = 2816 if T % 2816 == 0 else kb
    g_in = _weight_grad(ut, dproj, bn=9 * D // NSHARD, bk=kb_deep, name="dw_in_mm", shard_cols=9 * D // NSHARD)
    g_out = _weight_grad(mt, dh1, bn=D, bk=kb_deep, name="dw_out_mm")
    ps_mix = _pair_sums([g_in, g_out.reshape(NSHARD, D // NSHARD, D)], ("w_in", "w_out"), c_idx, "mix")

    grad_x, dmeta, dwattn, *parts_mix = _input_grad_block(dproj, x, head_tile, dh1, attn_norm_w, w_in_g, ps_mix)

    halves = [_chip_sum(ps, p, cj_idx, rb=_PAIR_ADD_ROWS[n], name=f"chip_sum_{n}")
              for ps, p, n in zip(ps_mix + ps_ffn, parts_mix + parts_ffn, ("w_in", "w_out", "w_up", "w_down"))]
    small = dict(dlb=dlb, dwattn=dwattn, dwhn=dwhn, dwffn=dwffn, dfb=dfb, dwfin=dwfin,
                 dcw=dcw, dfw=dfw, dmeta=dmeta, loss=loss_vec)
    return grad_x, small, halves


_SMALL_ORDER = ("dmeta", "dcw", "dfw", "dlb", "dwattn", "dwhn", "dwffn", "dfb", "dwfin", "loss")


def kernel(x, meta_tokens, lb_param, attn_norm_w, w_in, hgrn_norm_w, conv_w, w_out, ffn_norm_w, w_up, ffn_conv_w, ffn_conv_b, w_down, final_norm_w, loss_target, m_meta_tokens, m_lb_param, m_attn_norm_w, m_w_in, m_hgrn_norm_w, m_conv_w, m_w_out, m_ffn_norm_w, m_w_up, m_ffn_conv_w, m_ffn_conv_b, m_w_down, m_final_norm_w, v_meta_tokens, v_lb_param, v_attn_norm_w, v_w_in, v_hgrn_norm_w, v_conv_w, v_w_out, v_ffn_norm_w, v_w_up, v_ffn_conv_w, v_ffn_conv_b, v_w_down, v_final_norm_w):
    xi, yi, ci = _place()
    j = 2 * xi + yi
    c_idx = jnp.reshape(ci, (1,)).astype(jnp.int32)

    j_idx = jnp.reshape(j, (1,)).astype(jnp.int32)
    ds_, fs_ = D // NSHARD, DFF // NSHARD
    widen = lambda a: jnp.pad(a, ((0, 0), (0, 768 - a.shape[1])))
    rows_small = jnp.concatenate([widen(meta_tokens), widen(conv_w[0]), widen(ffn_conv_w[0]),
                                  jnp.zeros((2, 768), F32)], axis=0)
    s_in, s_out, s_up, s_down = [
        _into_slot(w[0], j_idx, rb=rb, dtype=BF16, name=f"slot_{n}", paired=pr)
        for w, rb, n, pr in ((w_in, 256, "w_in", False), (w_out, 128, "w_out", False), (w_up, 256, "w_up", True),
                             (w_down, 176, "w_down", False))]
    s_small = _into_slot(rows_small, j_idx, rb=rows_small.shape[0], dtype=F32, name="slot_small")
    w_in_g, w_out_g, small_g = _allgather_weights([s_in, s_out, s_small], (False, False, True))
    unshard = lambda a: jnp.transpose(a, (1, 0, 2)).reshape(a.shape[1], -1)
    meta_full = unshard(small_g[:, 0:NMETA, 0:ds_])
    conv_w_full = unshard(small_g[:, NMETA:NMETA + 3, 0:ds_])
    fcw_full = unshard(small_g[:, NMETA + 3:NMETA + 6, 0:fs_])

    cj_idx = jnp.stack([ci, j]).astype(jnp.int32)
    grad_x, small, halves = _local_step(
        x[0], loss_target[0], meta_full, lb_param, attn_norm_w, w_in_g, hgrn_norm_w, conv_w_full,
        w_out_g.reshape(D, D), ffn_norm_w, [s_up, s_down], fcw_full, ffn_conv_b, final_norm_w, c_idx, cj_idx)

    names = _SMALL_ORDER
    small_sums, g_big = _final_exchange([small[n] for n in names], halves)
    vals = dict(zip(names, small_sums))
    loss = vals["loss"].reshape(())
    g_small = {
        "meta_tokens": lax.dynamic_slice_in_dim(vals["dmeta"], j * (D // NSHARD), D // NSHARD, axis=1),
        "lb_param": jnp.concatenate([vals["dlb"], -vals["dlb"]], axis=0),
        "attn_norm_w": vals["dwattn"],
        "hgrn_norm_w": vals["dwhn"],
        "conv_w": lax.dynamic_slice_in_dim(vals["dcw"], j * (D // NSHARD), D // NSHARD, axis=1)[None],
        "ffn_norm_w": vals["dwffn"],
        "ffn_conv_w": lax.dynamic_slice_in_dim(vals["dfw"], j * (DFF // NSHARD), DFF // NSHARD, axis=1)[None],
        "ffn_conv_b": vals["dfb"],
        "final_norm_w": vals["dwfin"].reshape(D),
    }


    weights = {"meta_tokens": meta_tokens, "lb_param": lb_param, "attn_norm_w": attn_norm_w, "w_in": w_in,
               "hgrn_norm_w": hgrn_norm_w, "conv_w": conv_w, "w_out": w_out, "ffn_norm_w": ffn_norm_w,
               "w_up": w_up, "ffn_conv_w": ffn_conv_w, "ffn_conv_b": ffn_conv_b, "w_down": w_down,
               "final_norm_w": final_norm_w}
    ms = {"meta_tokens": m_meta_tokens, "lb_param": m_lb_param, "attn_norm_w": m_attn_norm_w, "w_in": m_w_in,
          "hgrn_norm_w": m_hgrn_norm_w, "conv_w": m_conv_w, "w_out": m_w_out, "ffn_norm_w": m_ffn_norm_w,
          "w_up": m_w_up, "ffn_conv_w": m_ffn_conv_w, "ffn_conv_b": m_ffn_conv_b, "w_down": m_w_down,
          "final_norm_w": m_final_norm_w}
    vs = {"meta_tokens": v_meta_tokens, "lb_param": v_lb_param, "attn_norm_w": v_attn_norm_w, "w_in": v_w_in,
          "hgrn_norm_w": v_hgrn_norm_w, "conv_w": v_conv_w, "w_out": v_w_out, "ffn_norm_w": v_ffn_norm_w,
          "w_up": v_w_up, "ffn_conv_w": v_ffn_conv_w, "ffn_conv_b": v_ffn_conv_b, "w_down": v_w_down,
          "final_norm_w": v_final_norm_w}
    order = list(weights)
    grads, deltas, new_m, new_v = {}, {}, {}, {}

    for name, g, rb in zip(("w_in", "w_out", "w_up", "w_down"), g_big, (256, 128, 256, 176)):
        shp = weights[name].shape
        w2, m2, v2 = (a.reshape(shp[1], shp[2]) for a in (weights[name], ms[name], vs[name]))
        g_, d_, nm_, nv_ = _adamw(w2, g, m2, v2, rb=rb, name=f"adamw_{name}")
        grads[name], deltas[name], new_m[name], new_v[name] = (a.reshape(shp) for a in (g_, d_, nm_, nv_))

    small_names = [n for n in order if n not in grads]
    as2d = lambda a: a.reshape(-1, a.shape[-1])
    res = _adamw_small([tuple(as2d(a) for a in (weights[n], g_small[n], ms[n], vs[n])) for n in small_names])
    for n, (d_, nm_, nv_) in zip(small_names, res):
        shp = weights[n].shape
        grads[n], deltas[n], new_m[n], new_v[n] = (a.reshape(shp) for a in (g_small[n], d_, nm_, nv_))

    return (loss, grad_x[None], *[grads[n] for n in order], *[deltas[n] for n in order],
            *[new_m[n] for n in order], *[new_v[n] for n in order])
```

```python
import jax
import jax.numpy as jnp
from jax import lax
from jax.experimental import pallas as pl
from jax.experimental.pallas import tpu as pltpu

F32 = jnp.float32
BF16 = jnp.bfloat16
MESH = pl.DeviceIdType.MESH

D = 1024
NH = 8
HD = 128
DFF = 2816
NMETA = 16
EPS = 1e-6
TR = 256
PAD = TR - NMETA
CH = 64
NSHARD = 4
VMEM_LIMIT = 62 * 1024 * 1024

ADAM_LR = 0.001
ADAM_B1 = 0.9
ADAM_B2 = 0.999
ADAM_EPS = 1e-08
ADAM_WD = 0.01
ADAM_STEP = 10


def _cparams(semantics=None, **kw):
    return pltpu.CompilerParams(dimension_semantics=semantics, vmem_limit_bytes=VMEM_LIMIT, **kw)


def _sigmoid(x):
    return 0.5 * jnp.tanh(0.5 * x) + 0.5


def _weight_grad(at, b, *, bn, bk, name, shard_cols=None):
    M, T = at.shape
    N = b.shape[1]
    assert b.shape[0] == T and T % bk == 0 and N % bn == 0, (name, M, N, T, bn, bk)
    nk = T // bk
    if shard_cols is None:
        nsh = 1
        blk = (M, bn)
        out_shape = jax.ShapeDtypeStruct((M, N), BF16)
        out_spec = pl.BlockSpec(blk, lambda j, k: (0, j))
    else:
        assert bn % shard_cols == 0 and N % shard_cols == 0
        nsh = bn // shard_cols
        blk = (nsh, M, shard_cols)
        out_shape = jax.ShapeDtypeStruct((N // shard_cols, M, shard_cols), BF16)
        out_spec = pl.BlockSpec(blk, lambda j, k: (j, 0, 0))

    def body(a_ref, b_ref, o_ref, acc_ref):
        k = pl.program_id(1)

        @pl.when(k == 0)
        def _():
            acc_ref[...] = jnp.zeros_like(acc_ref)

        p = jnp.dot(a_ref[...].astype(BF16), b_ref[...].astype(BF16), preferred_element_type=F32)
        if shard_cols is None:
            acc_ref[...] += p
        else:
            for q in range(nsh):
                acc_ref[q] += p[:, q * shard_cols:(q + 1) * shard_cols]

        @pl.when(k == nk - 1)
        def _():
            o_ref[...] = acc_ref[...].astype(BF16)

    return pl.pallas_call(
        body, name=name, grid=(N // bn, nk),
        in_specs=[pl.BlockSpec((M, bk), lambda j, k: (0, k)), pl.BlockSpec((bk, bn), lambda j, k: (k, j))],
        out_specs=out_spec, out_shape=out_shape, scratch_shapes=[pltpu.VMEM(blk, F32)],
        compiler_params=_cparams(("parallel", "arbitrary")),
    )(at, b)


def _input_grad_block(dproj, x_seq, head_tile, dres, w, w_in_g, psums):
    T = TR + x_seq.shape[0]
    nt = T // TR
    nsl = w_in_g.shape[0]
    wsl = w_in_g.shape[2]
    wcol = wsl // 2
    nex = len(psums)

    def body(*refs):
        dp_ref, x_ref, head_ref, dres_ref, w_ref, win_ref = refs[:6]
        gx_ref, dmeta_ref, dw_ref = refs[6 + nex:9 + nex]
        exchange = _ChipExchange(refs[6:6 + nex], refs[9 + nex:9 + 2 * nex], refs[-2], refs[-1])
        i = pl.program_id(0)

        @pl.when(i == 0)
        def _():
            exchange.send()

        @pl.when(i == nt - 1)
        def _():
            exchange.finish()

        x = jnp.where(i == 0, head_ref[...], x_ref[...])
        r = lax.rsqrt(jnp.mean(x * x, axis=-1, keepdims=True) + EPS)
        n = x * r
        du_v = None
        for s in range(nsl):
            part = _dot_nt(dp_ref[:, s * wsl:(s + 1) * wsl], win_ref[s])
            du_v = part if du_v is None else du_v + part
        dn = du_v * w_ref[...]
        dh = dres_ref[...] + r * (dn - n * jnp.mean(dn * n, axis=-1, keepdims=True))
        gx_ref[...] = dh
        part = jnp.sum(du_v * n, axis=0, keepdims=True)

        @pl.when(i == 0)
        def _():
            dmeta_ref[...] = dh[PAD:TR, :]
            dw_ref[...] = part

        @pl.when(i > 0)
        def _():
            dw_ref[...] += part

    row = pl.BlockSpec((TR, D), lambda i: (i, 0))
    vec = pl.BlockSpec((1, D), lambda i: (0, 0))
    hbm = pl.BlockSpec(memory_space=pl.ANY)
    return pl.pallas_call(
        body, name="input_grad_block", grid=(nt,),
        in_specs=[pl.BlockSpec((TR, 9 * D), lambda i: (i, 0)),
                  pl.BlockSpec((TR, D), lambda i: (jnp.maximum(i - 1, 0), 0)), pl.BlockSpec((TR, D), lambda i: (0, 0)),
                  row, vec, pl.BlockSpec(memory_space=pltpu.VMEM)] + [hbm] * nex,
        out_specs=[pl.BlockSpec((TR, D), lambda i: (jnp.maximum(i - 1, 0), 0)),
                   pl.BlockSpec((NMETA, D), lambda i: (0, 0)), vec] + [hbm] * nex,
        out_shape=[jax.ShapeDtypeStruct((T - TR, D), F32), jax.ShapeDtypeStruct((NMETA, D), F32),
                   jax.ShapeDtypeStruct((1, D), F32)]
        + [jax.ShapeDtypeStruct((3,) + p.shape[1:], p.dtype) for p in psums],
        scratch_shapes=[pltpu.SemaphoreType.DMA((nex, 3)), pltpu.SemaphoreType.DMA((nex, 3))],
        compiler_params=_cparams(("arbitrary",)),
    )(dproj, x_seq, head_tile, dres, w, w_in_g, *psums)


def _tri_matmul(tri_bf16, x):
    hi = x.astype(BF16)
    lo = (x - hi.astype(F32)).astype(BF16)
    return jnp.dot(tri_bf16, lo, preferred_element_type=F32) + jnp.dot(tri_bf16, hi, preferred_element_type=F32)


def _shift_down(x, prev8, n):
    rows = x.shape[0]
    return pltpu.roll(jnp.concatenate([prev8, x], axis=0), n, 0)[8:8 + rows, :]


def _shift_up(x, next8, n):
    rows = x.shape[0]
    return pltpu.roll(jnp.concatenate([x, next8], axis=0), rows + 8 - n, 0)[0:rows, :]


def _gates(f_raw, lb):
    sg = _sigmoid(f_raw)
    f = lb + (1.0 - lb) * sg
    return sg, f, jnp.log(f), 1.0 - f


def _lower_bound(lbp_ref):
    return _sigmoid(lbp_ref[0:1, :] - lbp_ref[1:2, :])


def _tri_masks():
    r = lax.broadcasted_iota(jnp.int32, (CH, CH), 0)
    c = lax.broadcasted_iota(jnp.int32, (CH, CH), 1)
    return r >= c, r <= c


def _ones_where(mask):
    return jnp.where(mask, 1.0, 0.0).astype(BF16)


def _dot(a, b):
    return jnp.dot(a.astype(BF16), b.astype(BF16), preferred_element_type=F32)


def _dot_nt(a, b):
    return lax.dot_general(a.astype(BF16), b.astype(BF16), (((1,), (1,)), ((), ())), preferred_element_type=F32)


def _dot_tn(a, b):
    return lax.dot_general(a.astype(BF16), b.astype(BF16), (((0,), (0,)), ((), ())), preferred_element_type=F32)


def _mix_block_fwd(x_seq, head_tile, lb_param, wattn, whn, conv_w, w_in_g, w_out, late, late_paired):
    T = TR + x_seq.shape[0]
    nt = T // TR
    ncht = TR // CH
    nsl = w_in_g.shape[0]
    wsl = w_in_g.shape[2]
    wcol = wsl // 2

    nlate = len(late)

    def body(*refs):
        x_ref, head_ref, lbp_ref, wa_ref, whn_ref, cw_ref, win_ref, wout_ref = refs[:8]
        ut_ref, projb_ref, o_ref, sst_ref, mt_ref, h1_ref = refs[8 + nlate:14 + nlate]
        late_refs = refs[14 + nlate:14 + 2 * nlate]
        proj_ref, m_ref, st_ref, cxc_ref, send_sems, recv_sems = refs[14 + 2 * nlate:]
        i = pl.program_id(0)
        gather = _Gather(late_refs, send_sems, recv_sems, (False,) * nlate, late_paired)

        @pl.when(i == 0)
        def _():
            st_ref[...] = jnp.zeros_like(st_ref)
            cxc_ref[...] = jnp.zeros_like(cxc_ref)
            gather.send()

        @pl.when(i == nt // 2)
        def _():
            gather.relay()

        x = jnp.where(i == 0, head_ref[...], x_ref[...])
        r1 = lax.rsqrt(jnp.mean(x * x, axis=-1, keepdims=True) + EPS)
        u_f = x * r1 * wa_ref[...]
        u = u_f.astype(BF16)
        ut_ref[...] = u_f.T.astype(BF16)
        for s in range(nsl):
            cols = slice(s * wsl, (s + 1) * wsl)
            p = jnp.dot(u, win_ref[s], preferred_element_type=F32)
            proj_ref[:, cols] = p
            projb_ref[:, cols] = p.astype(BF16)

        lb = _lower_bound(lbp_ref)
        lower, _ = _tri_masks()
        ltri = _ones_where(lower)
        whn_v = whn_ref[...]
        w0, w1, w2 = cw_ref[0:1, :], cw_ref[1:2, :], cw_ref[2:3, :]

        def gates_and_conv(c):
            rows = slice(c * CH, (c + 1) * CH)
            q_raw = proj_ref[rows, 0:D]
            f_raw = proj_ref[rows, D:2 * D]
            q = q_raw * _sigmoid(q_raw)
            _, _, g, k = _gates(f_raw, lb)
            gam = _tri_matmul(ltri, g)
            gam_l = gam[CH - 1:CH, :]
            g_out = proj_ref[rows, 3 * D:4 * D]
            cx = proj_ref[rows, 5 * D:6 * D] * proj_ref[rows, 6 * D:7 * D]
            prev8 = cxc_ref[...]
            cv = w0 * _shift_down(cx, prev8, 2) + w1 * _shift_down(cx, prev8, 1) + w2 * cx
            cxc_ref[...] = cx[CH - 8:CH, :]
            y_b = proj_ref[rows, 4 * D:5 * D] * cv
            return dict(
                e_l=jnp.exp(gam_l), qt=(q * jnp.exp(gam)).astype(BF16), kt=(k * jnp.exp(-gam)).astype(BF16),
                khat=(k * jnp.exp(gam_l - gam)).astype(BF16), vb=proj_ref[rows, 2 * D:3 * D].astype(BF16),
                out_gate=whn_v * (g_out * _sigmoid(g_out)), merge_a=_sigmoid(proj_ref[rows, 7 * D:8 * D]),
                merged_b=_sigmoid(proj_ref[rows, 8 * D:9 * D]) * y_b)

        def recurrence_and_merge(c, pre):
            rows = slice(c * CH, (c + 1) * CH)
            qt, kt, khat, vb, e_l = pre["qt"], pre["kt"], pre["khat"], pre["vb"], pre["e_l"]
            heads = [slice(h * HD, (h + 1) * HD) for h in range(NH)]
            sts = [st_ref[h] for h in range(NH)]
            scores = [_dot_nt(qt[:, cs], kt[:, cs]) for cs in heads]
            o_state = [_dot_nt(qt[:, cs], st) for cs, st in zip(heads, sts)]
            st_new = [_dot_tn(vb[:, cs], khat[:, cs]) for cs in heads]
            on_parts = []
            for h, cs in enumerate(heads):
                sst_ref[c, h] = sts[h]
                st_ref[h] = sts[h] * e_l[:, cs] + st_new[h]
                o_h = o_state[h] + _dot(jnp.where(lower, scores[h], 0.0), vb[:, cs])
                o_ref[rows, cs] = o_h
                ro = lax.rsqrt(jnp.mean(o_h * o_h, axis=-1, keepdims=True) + EPS)
                on_parts.append(o_h * ro)
            on = jnp.concatenate(on_parts, axis=1)
            m_ref[rows, :] = pre["merge_a"] * (on * pre["out_gate"]) + pre["merged_b"]

        pre = {0: gates_and_conv(0)}
        for c in range(ncht):
            if c + 1 < ncht:
                pre[c + 1] = gates_and_conv(c + 1)
            recurrence_and_merge(c, pre.pop(c))
        m_v = m_ref[...]
        h1_ref[...] = x + jnp.dot(m_v.astype(BF16), wout_ref[...], preferred_element_type=F32)
        mt_ref[...] = m_v.T.astype(BF16)

        @pl.when(i == nt - 1)
        def _():
            gather.finish()

    row = lambda w: pl.BlockSpec((TR, w), lambda i: (i, 0))
    col = lambda w: pl.BlockSpec((w, TR), lambda i: (0, i))
    vec = lambda r: pl.BlockSpec((r, D), lambda i: (0, 0))
    vm = pl.BlockSpec(memory_space=pltpu.VMEM)
    hbm = pl.BlockSpec(memory_space=pl.ANY)
    return pl.pallas_call(
        body, name="mix_block_fwd", grid=(nt,),
        in_specs=[pl.BlockSpec((TR, D), lambda i: (jnp.maximum(i - 1, 0), 0)), pl.BlockSpec((TR, D), lambda i: (0, 0)),
                  vec(2), vec(1), vec(1), vec(3), vm, vm] + [hbm] * nlate,
        out_specs=[col(D), row(9 * D), row(D), pl.BlockSpec((ncht, NH, HD, HD), lambda i: (i, 0, 0, 0)),
                   col(D), row(D)] + [hbm] * nlate,
        out_shape=[jax.ShapeDtypeStruct((D, T), BF16), jax.ShapeDtypeStruct((T, 9 * D), BF16),
                   jax.ShapeDtypeStruct((T, D), F32), jax.ShapeDtypeStruct((T // CH, NH, HD, HD), F32),
                   jax.ShapeDtypeStruct((D, T), BF16), jax.ShapeDtypeStruct((T, D), F32)]
        + [jax.ShapeDtypeStruct(a.shape, a.dtype) for a in late],
        input_output_aliases={8 + n: 6 + n for n in range(nlate)},
        scratch_shapes=[pltpu.VMEM((TR, 9 * D), F32), pltpu.VMEM((TR, D), F32), pltpu.VMEM((NH, HD, HD), F32),
                        pltpu.VMEM((8, D), F32), pltpu.SemaphoreType.DMA((nlate, 6)),
                        pltpu.SemaphoreType.DMA((nlate, 6))],
        compiler_params=_cparams(("arbitrary",)),
    )(x_seq, head_tile, lb_param, wattn, whn, conv_w, w_in_g, w_out, *late)


def _mix_block_bwd(dh1, projb, o, sst, lb_param, whn, conv_w, w_out, psums):
    T = projb.shape[0]
    nt = T // TR
    ncht = TR // CH
    tb16 = TR // 16
    nex = len(psums)

    def body(*refs):
        dh1_ref, proj_ref, pc_ref, px_ref, o_ref, sst_ref, lbp_ref, whn_ref, cw_ref, wout_ref = refs[:10]
        dproj_ref, dlb_ref, dwhn_ref, dcw_ref = refs[10 + nex:14 + nex]
        exchange = _ChipExchange(refs[10:10 + nex], refs[14 + nex:14 + 2 * nex], refs[-2], refs[-1])
        dm_ref, dst_ref, dcvc_ref, acc_lb, acc_hn, acc_cw = refs[14 + 2 * nex:-2]
        s = pl.program_id(0)
        tile = nt - 1 - s

        @pl.when(s == 0)
        def _():
            dst_ref[...] = jnp.zeros_like(dst_ref)
            dcvc_ref[...] = jnp.zeros_like(dcvc_ref)
            acc_lb[...] = jnp.zeros_like(acc_lb)
            acc_hn[...] = jnp.zeros_like(acc_hn)
            acc_cw[...] = jnp.zeros_like(acc_cw)
            exchange.send()

        dm_ref[...] = _dot_nt(dh1_ref[...], wout_ref[...])
        lb = _lower_bound(lbp_ref)
        lower, upper = _tri_masks()
        ltri = _ones_where(lower)
        utri = _ones_where(upper)
        whn_v = whn_ref[...]
        w0, w1, w2 = cw_ref[0:1, :], cw_ref[1:2, :], cw_ref[2:3, :]
        cx_before_tile = jnp.where(tile > 0, (pc_ref[...].astype(F32) * px_ref[...].astype(F32))[8:16, :], 0.0)
        rid = lax.broadcasted_iota(jnp.int32, (CH, D), 0)

        def chunk(cc, carry):
            c = ncht - 1 - cc
            r0 = pl.multiple_of(c * CH, CH)
            rows = pl.ds(r0, CH)
            slab = lambda n: proj_ref[rows, n * D:(n + 1) * D].astype(F32)
            q_raw, f_raw, v, g_out, b_gate, c_gate, x_conv = (slab(n) for n in range(7))
            sa = _sigmoid(slab(7))
            sb = _sigmoid(slab(8))
            dm_v = dm_ref[rows, :]

            sq = _sigmoid(q_raw)
            q = q_raw * sq
            sg, f, g, k = _gates(f_raw, lb)
            gam = _tri_matmul(ltri, g)
            gam_l = gam[CH - 1:CH, :]
            e_l = jnp.exp(gam_l)
            e_g = jnp.exp(gam)
            e_ng = jnp.exp(-gam)
            e_kl = jnp.exp(gam_l - gam)
            qt = q * e_g
            kt = k * e_ng
            khat = k * e_kl
            qt_b, kt_b, khat_b, v_b = qt.astype(BF16), kt.astype(BF16), khat.astype(BF16), v.astype(BF16)
            qt_seen, kt_seen = qt_b.astype(F32), kt_b.astype(F32)
            s_go = _sigmoid(g_out)
            silu_go = g_out * s_go
            cx = c_gate * x_conv
            rprev = pl.ds(pl.multiple_of(jnp.maximum(r0 - 16, 0), 16), 16)
            cx_prev_in = (proj_ref[rprev, 5 * D:6 * D].astype(F32) * proj_ref[rprev, 6 * D:7 * D].astype(F32))[8:16, :]
            prev8 = jnp.where(c > 0, cx_prev_in, cx_before_tile)
            cx_m1 = _shift_down(cx, prev8, 1)
            cx_m2 = _shift_down(cx, prev8, 2)
            cv = w0 * cx_m2 + w1 * cx_m1 + w2 * cx
            y_b = b_gate * cv

            o_v = o_ref[rows, :]
            ro_parts, on_parts = [], []
            for h in range(NH):
                cs = slice(h * HD, (h + 1) * HD)
                o_h = o_v[:, cs]
                ro = lax.rsqrt(jnp.mean(o_h * o_h, axis=-1, keepdims=True) + EPS)
                ro_parts.append(ro)
                on_parts.append(o_h * ro)
            on = jnp.concatenate(on_parts, axis=1)
            y_a = on * whn_v * silu_go

            dy_a = dm_v * sa
            dy_b = dm_v * sb
            dproj_ref[rows, 7 * D:8 * D] = (dm_v * y_a * sa * (1.0 - sa)).astype(BF16)
            dproj_ref[rows, 8 * D:9 * D] = (dm_v * y_b * sb * (1.0 - sb)).astype(BF16)
            dproj_ref[rows, 4 * D:5 * D] = (dy_b * cv).astype(BF16)
            dcv = dy_b * b_gate
            acc_cw[0:1, :] += jnp.sum(dcv * cx_m2, axis=0, keepdims=True)
            acc_cw[1:2, :] += jnp.sum(dcv * cx_m1, axis=0, keepdims=True)
            acc_cw[2:3, :] += jnp.sum(dcv * cx, axis=0, keepdims=True)
            next8 = dcvc_ref[...]
            dcx = w2 * dcv + w1 * _shift_up(dcv, next8, 1) + w0 * _shift_up(dcv, next8, 2)
            dcvc_ref[...] = dcv[0:8, :]
            dproj_ref[rows, 5 * D:6 * D] = (dcx * x_conv).astype(BF16)
            dproj_ref[rows, 6 * D:7 * D] = (dcx * c_gate).astype(BF16)
            don = dy_a * whn_v * silu_go
            dproj_ref[rows, 3 * D:4 * D] = (dy_a * on * whn_v * (s_go * (1.0 + g_out * (1.0 - s_go)))).astype(BF16)
            acc_hn[...] += jnp.sum(dy_a * silu_go * on, axis=0, keepdims=True)

            heads = [slice(h * HD, (h + 1) * HD) for h in range(NH)]
            do_bs, sts, dstns, first = [], [], [], []
            for h, cs in enumerate(heads):
                on_h = on_parts[h]
                don_h = don[:, cs]
                do_h = ro_parts[h] * (don_h - on_h * jnp.mean(don_h * on_h, axis=-1, keepdims=True))
                do_bs.append(do_h.astype(BF16))
                sts.append(sst_ref[c, h])
                dstns.append(dst_ref[h])
            for h, cs in enumerate(heads):
                qt_h, kt_h, khat_h, v_h = qt_b[:, cs], kt_b[:, cs], khat_b[:, cs], v_b[:, cs]
                do_b, dstn_b = do_bs[h], dstns[h].astype(BF16)
                first.append(dict(
                    a_t=_dot_nt(kt_h, qt_h), da=_dot_nt(do_b, v_h), da_t=_dot_nt(v_h, do_b),
                    dv_state=_dot_nt(khat_h, dstn_b), dqt_state=_dot(do_b, sts[h]), dkhat=_dot(v_h, dstn_b),
                    dst_chunk=_dot_tn(do_b, qt_h)))
            dq_parts, dk_parts, dv_parts, dgam_parts, ext_parts = [], [], [], [], []
            for h, cs in enumerate(heads):
                qt_h, kt_h = qt_b[:, cs], kt_b[:, cs]
                p1, do_b, st, dstn = first[h], do_bs[h], sts[h], dstns[h]
                dv_h = _dot(jnp.where(upper, p1["a_t"], 0.0), do_b) + p1["dv_state"]
                dqt_state = p1["dqt_state"]
                dqt_chunk = _dot(jnp.where(lower, p1["da"], 0.0), kt_h)
                dkt = _dot(jnp.where(upper, p1["da_t"], 0.0), qt_h)
                dkhat = p1["dkhat"]
                dq_h = (dqt_state + dqt_chunk) * e_g[:, cs]
                dk_h = dkt * e_ng[:, cs] + dkhat * e_kl[:, cs]
                khat_dkhat = dkhat * khat[:, cs]
                ext = (jnp.sum(khat_dkhat, axis=0, keepdims=True)
                       + e_l[:, cs] * jnp.sum(st * dstn, axis=0, keepdims=True))
                dst_ref[h] = p1["dst_chunk"] + dstn * e_l[:, cs]
                dq_parts.append(dq_h)
                dk_parts.append(dk_h)
                dv_parts.append(dv_h)
                dgam_parts.append(qt[:, cs] * dqt_state + qt_seen[:, cs] * dqt_chunk - kt_seen[:, cs] * dkt
                                  - khat_dkhat)
                ext_parts.append(ext)
            dq = jnp.concatenate(dq_parts, axis=1)
            dk = jnp.concatenate(dk_parts, axis=1)
            dgam = jnp.concatenate(dgam_parts, axis=1)
            ext = jnp.concatenate(ext_parts, axis=1)
            dgam = dgam + jnp.where(rid == CH - 1, ext, 0.0)
            dg = _tri_matmul(utri, dgam)
            dproj_ref[rows, 0:D] = (dq * (sq * (1.0 + q_raw * (1.0 - sq)))).astype(BF16)
            df = dg * jnp.exp(-g) - dk
            dproj_ref[rows, D:2 * D] = (df * (1.0 - lb) * sg * (1.0 - sg)).astype(BF16)
            dproj_ref[rows, 2 * D:3 * D] = jnp.concatenate(dv_parts, axis=1).astype(BF16)
            real = (tile * TR + r0 + rid) >= PAD
            acc_lb[...] += jnp.sum(jnp.where(real, df * (1.0 - sg), 0.0), axis=0, keepdims=True)
            return carry

        lax.fori_loop(0, ncht, chunk, 0)

        @pl.when(s == nt - 1)
        def _():
            dlb_ref[...] = acc_lb[...] * lb * (1.0 - lb)
            hn = acc_hn[...]
            tot = hn[:, 0:HD]
            for h in range(1, NH):
                tot = tot + hn[:, h * HD:(h + 1) * HD]
            dwhn_ref[...] = tot
            dcw_ref[...] = acc_cw[0:3, :]
            exchange.finish()

    hbm = pl.BlockSpec(memory_space=pl.ANY)
    rev = lambda s: (nt - 1 - s, 0)
    prevc = lambda s: (jnp.maximum((nt - 1 - s) * tb16 - 1, 0), 5)
    prevx = lambda s: (jnp.maximum((nt - 1 - s) * tb16 - 1, 0), 6)
    const = lambda s: (0, 0)
    return pl.pallas_call(
        body, name="mix_block_bwd", grid=(nt,),
        in_specs=[pl.BlockSpec((TR, D), rev),
                  pl.BlockSpec((TR, 9 * D), rev),
                  pl.BlockSpec((16, D), prevc),
                  pl.BlockSpec((16, D), prevx),
                  pl.BlockSpec((TR, D), rev),
                  pl.BlockSpec((ncht, NH, HD, HD), lambda s: (nt - 1 - s, 0, 0, 0)),
                  pl.BlockSpec((2, D), const),
                  pl.BlockSpec((1, D), const),
                  pl.BlockSpec((3, D), const),
                  pl.BlockSpec(memory_space=pltpu.VMEM)] + [hbm] * nex,
        out_specs=[pl.BlockSpec((TR, 9 * D), rev),
                   pl.BlockSpec((1, D), const),
                   pl.BlockSpec((1, HD), const),
                   pl.BlockSpec((3, D), const)] + [hbm] * nex,
        out_shape=[jax.ShapeDtypeStruct((T, 9 * D), BF16), jax.ShapeDtypeStruct((1, D), F32),
                   jax.ShapeDtypeStruct((1, HD), F32), jax.ShapeDtypeStruct((3, D), F32)]
        + [jax.ShapeDtypeStruct((3,) + p.shape[1:], p.dtype) for p in psums],
        scratch_shapes=[pltpu.VMEM((TR, D), F32), pltpu.VMEM((NH, HD, HD), F32), pltpu.VMEM((8, D), F32),
                        pltpu.VMEM((1, D), F32), pltpu.VMEM((1, D), F32), pltpu.VMEM((8, D), F32),
                        pltpu.SemaphoreType.DMA((nex, 3)), pltpu.SemaphoreType.DMA((nex, 3))],
        compiler_params=_cparams(("arbitrary",)),
    )(dh1, projb, projb, projb, o, sst, lb_param, whn, conv_w, w_out, *psums)


def _ffn_block_fwd(h1, tgt, wffn, w_up_g, fcw, fcb, w_down, wfin):
    T = h1.shape[0]
    nt = T // TR
    nsl = w_up_g.shape[0]
    wsl = w_up_g.shape[2]

    def body(h_ref, t_ref, wn_ref, wup_ref, cw_ref, cb_ref, wdn_ref, wf_ref,
             u2t_ref, upb_ref, ggt_ref, dh_ref, loss_ref, dwf_ref, up_scr, gg_ref, carry_ref):
        i = pl.program_id(0)

        @pl.when(i == 0)
        def _():
            carry_ref[...] = jnp.zeros_like(carry_ref)
            loss_ref[...] = jnp.zeros_like(loss_ref)
            dwf_ref[...] = jnp.zeros_like(dwf_ref)

        x = h_ref[...]
        r2 = lax.rsqrt(jnp.mean(x * x, axis=-1, keepdims=True) + EPS)
        u2_f = x * r2 * wn_ref[...]
        u2 = u2_f.astype(BF16)
        u2t_ref[...] = u2_f.T.astype(BF16)
        for s in range(nsl):
            up_s = jnp.dot(u2, wup_ref[s], preferred_element_type=F32)
            up_scr[:, s * wsl:(s + 1) * wsl] = up_s
            upb_ref[:, s * wsl:(s + 1) * wsl] = up_s.astype(BF16)
        w0, w1, w2 = cw_ref[0:1, :], cw_ref[1:2, :], cw_ref[2:3, :]

        def chunk(c, carry):
            rows = pl.ds(pl.multiple_of(c * CH, CH), CH)
            a_pre = up_scr[rows, 0:DFF]
            val = up_scr[rows, DFF:2 * DFF]
            prev8 = carry_ref[...]
            a = w0 * _shift_down(a_pre, prev8, 2) + w1 * _shift_down(a_pre, prev8, 1) + w2 * a_pre + cb_ref[...]
            carry_ref[...] = a_pre[CH - 8:CH, :]
            gg_ref[rows, :] = a * _sigmoid(a) * val
            return carry

        lax.fori_loop(0, TR // CH, chunk, 0, unroll=True)
        gg_v = gg_ref[...]
        ggt_ref[...] = gg_v.T.astype(BF16)
        h2 = x + jnp.dot(gg_v.astype(BF16), wdn_ref[...], preferred_element_type=F32)
        r3 = lax.rsqrt(jnp.mean(h2 * h2, axis=-1, keepdims=True) + EPS)
        n3 = h2 * r3
        wf = wf_ref[...]
        diff = jnp.where(i > 0, n3 * wf - t_ref[...], 0.0)
        loss_ref[...] += jnp.sum(diff * diff, axis=0, keepdims=True) * (0.5 / D)
        dy = diff * (1.0 / D)
        dwf_ref[...] += jnp.sum(dy * n3, axis=0, keepdims=True)
        dn = dy * wf
        dh_ref[...] = r3 * (dn - n3 * jnp.mean(dn * n3, axis=-1, keepdims=True))

    row = lambda w: pl.BlockSpec((TR, w), lambda i: (i, 0))
    col = lambda w: pl.BlockSpec((w, TR), lambda i: (0, i))
    vec = lambda w, r=1: pl.BlockSpec((r, w), lambda i: (0, 0))
    vm = pl.BlockSpec(memory_space=pltpu.VMEM)
    return pl.pallas_call(
        body, name="ffn_block_fwd", grid=(nt,),
        in_specs=[row(D), pl.BlockSpec((TR, D), lambda i: (jnp.maximum(i - 1, 0), 0)), vec(D), vm,
                  vec(DFF, 3), vec(DFF), vm, vec(D)],
        out_specs=[col(D), row(2 * DFF), col(DFF), row(D), vec(D), vec(D)],
        out_shape=[jax.ShapeDtypeStruct((D, T), BF16), jax.ShapeDtypeStruct((T, 2 * DFF), BF16),
                   jax.ShapeDtypeStruct((DFF, T), BF16), jax.ShapeDtypeStruct((T, D), F32),
                   jax.ShapeDtypeStruct((1, D), F32), jax.ShapeDtypeStruct((1, D), F32)],
        scratch_shapes=[pltpu.VMEM((TR, 2 * DFF), F32), pltpu.VMEM((TR, DFF), F32), pltpu.VMEM((8, DFF), F32)],
        compiler_params=_cparams(("arbitrary",)),
    )(h1, tgt, wffn, w_up_g, fcw, fcb, w_down, wfin)


def _ffn_block_bwd(dh2, upb, h1, wffn, w_up_g, fcw, fcb, w_down):
    T = h1.shape[0]
    nt = T // TR
    ncht = TR // CH
    nsl = w_up_g.shape[0]
    wsl = w_up_g.shape[2]
    tb16 = TR // 16
    assert ncht % nsl == 0
    every = ncht // nsl
    step = -(-DFF // (ncht * 128)) * 128
    parts = [(c0, min(c0 + step, DFF)) for c0 in range(0, DFF, step)]
    assert len(parts) == ncht

    def body(dh2n_ref, dh2p_ref, up_ref, pa_ref, h_ref, wn_ref, wup_ref, cw_ref, cb_ref, wdn_ref,
             dup_ref, dh1_ref, dfw_ref, dfb_ref, dwn_ref,
             dgg_ring, dup_ring, carry_ref, acc_w, acc_b, acc_n):
        s = pl.program_id(0)
        slot = lax.rem(s, 2)
        other = 1 - slot

        @pl.when(s == 0)
        def _():
            carry_ref[...] = jnp.zeros_like(carry_ref)
            acc_w[...] = jnp.zeros_like(acc_w)
            acc_b[...] = jnp.zeros_like(acc_b)
            acc_n[...] = jnp.zeros_like(acc_n)
            dup_ring[1] = jnp.zeros((TR, 2 * DFF), BF16)
            dgg_ring[0] = _dot_nt(dh2p_ref[...], wdn_ref[...])

        def norm_bwd(du2, valid):
            x = h_ref[...]
            dh2p = dh2p_ref[...]
            r2 = lax.rsqrt(jnp.mean(x * x, axis=-1, keepdims=True) + EPS)
            n2 = x * r2
            dn = du2 * wn_ref[...]
            dh1_ref[...] = dh2p + r2 * (dn - n2 * jnp.mean(dn * n2, axis=-1, keepdims=True))
            acc_n[...] += jnp.where(valid, jnp.sum(du2 * n2, axis=0, keepdims=True), 0.0)

        @pl.when(s < nt)
        def _():
            tile = nt - 1 - s
            w0, w1, w2 = cw_ref[0:1, :], cw_ref[1:2, :], cw_ref[2:3, :]
            a_before_tile = jnp.where(tile > 0, pa_ref[...].astype(F32)[8:16, :], 0.0)
            dh2n = dh2n_ref[...].astype(BF16)
            du2 = None
            for idx in range(ncht):
                c = ncht - 1 - idx
                r0 = c * CH
                rows = slice(r0, r0 + CH)
                a_pre = up_ref[rows, 0:DFF].astype(F32)
                val = up_ref[rows, DFF:2 * DFF].astype(F32)
                prev8 = up_ref[r0 - 16:r0, 0:DFF].astype(F32)[8:16, :] if c > 0 else a_before_tile
                a_m1 = _shift_down(a_pre, prev8, 1)
                a_m2 = _shift_down(a_pre, prev8, 2)
                a = w0 * a_m2 + w1 * a_m1 + w2 * a_pre + cb_ref[...]
                sig = _sigmoid(a)
                dgg_v = dgg_ring[slot, rows, :]
                da = dgg_v * val * (sig * (1.0 + a * (1.0 - sig)))
                dval = (dgg_v * (a * sig)).astype(BF16)
                next8 = carry_ref[...]
                da_pre = (w2 * da + w1 * _shift_up(da, next8, 1) + w0 * _shift_up(da, next8, 2)).astype(BF16)
                carry_ref[...] = da[0:8, :]
                dup_ref[rows, 0:DFF] = da_pre
                dup_ref[rows, DFF:2 * DFF] = dval
                dup_ring[slot, rows, 0:DFF] = da_pre
                dup_ring[slot, rows, DFF:2 * DFF] = dval
                acc_w[0:1, :] += jnp.sum(da * a_m2, axis=0, keepdims=True)
                acc_w[1:2, :] += jnp.sum(da * a_m1, axis=0, keepdims=True)
                acc_w[2:3, :] += jnp.sum(da * a_pre, axis=0, keepdims=True)
                acc_b[...] += jnp.sum(da, axis=0, keepdims=True)
                c0, c1 = parts[idx]
                dgg_ring[other, :, c0:c1] = _dot_nt(dh2n, wdn_ref[c0:c1, :])
                if idx % every == 0:
                    sl = idx // every
                    part = _dot_nt(dup_ring[other, :, sl * wsl:(sl + 1) * wsl], wup_ref[sl])
                    du2 = part if du2 is None else du2 + part
            norm_bwd(du2, s > 0)

        @pl.when(s == nt)
        def _():
            du2 = _dot_nt(dup_ring[other, :, 0:wsl], wup_ref[0])
            for sl in range(1, nsl):
                du2 = du2 + _dot_nt(dup_ring[other, :, sl * wsl:(sl + 1) * wsl], wup_ref[sl])
            norm_bwd(du2, True)
            dfw_ref[...] = acc_w[0:3, :]
            dfb_ref[...] = acc_b[...]
            dwn_ref[...] = acc_n[...]

    gate_tile = lambda s: jnp.maximum(nt - 1 - s, 0)
    next_tile = lambda s: jnp.maximum(nt - 2 - s, 0)
    prev_tile = lambda s: jnp.minimum(nt - s, nt - 1)
    vec = lambda w, r=1: pl.BlockSpec((r, w), lambda s: (0, 0))
    vm = pl.BlockSpec(memory_space=pltpu.VMEM)
    return pl.pallas_call(
        body, name="ffn_block_bwd", grid=(nt + 1,),
        in_specs=[pl.BlockSpec((TR, D), lambda s: (next_tile(s), 0)),
                  pl.BlockSpec((TR, D), lambda s: (prev_tile(s), 0)),
                  pl.BlockSpec((TR, 2 * DFF), lambda s: (gate_tile(s), 0)),
                  pl.BlockSpec((16, DFF), lambda s: (jnp.maximum(gate_tile(s) * tb16 - 1, 0), 0)),
                  pl.BlockSpec((TR, D), lambda s: (prev_tile(s), 0)),
                  vec(D), vm, vec(DFF, 3), vec(DFF), vm],
        out_specs=[pl.BlockSpec((TR, 2 * DFF), lambda s: (gate_tile(s), 0)),
                   pl.BlockSpec((TR, D), lambda s: (prev_tile(s), 0)),
                   vec(DFF, 3), vec(DFF), vec(D)],
        out_shape=[jax.ShapeDtypeStruct((T, 2 * DFF), BF16), jax.ShapeDtypeStruct((T, D), F32),
                   jax.ShapeDtypeStruct((3, DFF), F32), jax.ShapeDtypeStruct((1, DFF), F32),
                   jax.ShapeDtypeStruct((1, D), F32)],
        scratch_shapes=[pltpu.VMEM((2, TR, DFF), F32), pltpu.VMEM((2, TR, 2 * DFF), BF16),
                        pltpu.VMEM((8, DFF), F32), pltpu.VMEM((8, DFF), F32), pltpu.VMEM((1, DFF), F32),
                        pltpu.VMEM((1, D), F32)],
        compiler_params=_cparams(("arbitrary",)),
    )(dh2, dh2, upb, upb, h1, wffn, w_up_g, fcw, fcb, w_down)


def _place():
    x, y, c = lax.axis_index("x"), lax.axis_index("y"), lax.axis_index("c")
    return x, y, c


_CHIP_FLIPS = ((1, 0), (0, 1), (1, 1))


def _flip(v, bit):
    return 1 - v if bit else v


def _into_slot(w, j_idx, *, rb, dtype, name, paired=False):
    r, cdim = w.shape

    def body(j_ref, w_ref, out_ref):
        del j_ref
        out_ref[...] = w_ref[...].astype(dtype)

    if paired:
        out_shape = jax.ShapeDtypeStruct((NSHARD // 2, r, 2 * cdim), dtype)
        out_spec = pl.BlockSpec((None, rb, cdim), lambda i, j_ref: (j_ref[0] // 2, i, j_ref[0] % 2))
    else:
        out_shape = jax.ShapeDtypeStruct((NSHARD, r, cdim), dtype)
        out_spec = pl.BlockSpec((None, rb, cdim), lambda i, j_ref: (j_ref[0], i, 0))
    grid_spec = pltpu.PrefetchScalarGridSpec(
        num_scalar_prefetch=1, grid=(r // rb,),
        in_specs=[pl.BlockSpec((rb, cdim), lambda i, j_ref: (i, 0))], out_specs=out_spec)
    return pl.pallas_call(
        body, name=name, grid_spec=grid_spec, out_shape=out_shape, compiler_params=_cparams(("parallel",)),
    )(j_idx, w)


class _Gather:
    def __init__(self, outs, send_sems, recv_sems, whole, paired=None):
        self.outs, self.send_sems, self.recv_sems, self.whole = outs, send_sems, recv_sems, whole
        self.paired = paired if paired is not None else (False,) * len(outs)
        self.x, self.y, self.c = _place()
        self.j = 2 * self.x + self.y
        self.sibling = (self.x, self.y, 1 - self.c)

    def _rows(self, w, core):
        r = self.outs[w].shape[1]
        return pl.ds(0, r) if self.whole[w] else pl.ds(core * (r // 2), r // 2)

    def _copy(self, w, slot, core, sem, to):
        if self.paired[w]:
            cw = self.outs[w].shape[2] // 2
            piece = self.outs[w].at[slot // 2, self._rows(w, core), pl.ds((slot % 2) * cw, cw)]
        else:
            piece = self.outs[w].at[slot, self._rows(w, core), :]
        return pltpu.make_async_remote_copy(
            src_ref=piece, dst_ref=piece, send_sem=self.send_sems.at[w, sem], recv_sem=self.recv_sems.at[w, sem],
            device_id=to, device_id_type=MESH)

    def _chips(self):
        for kk, (fx, fy) in enumerate(_CHIP_FLIPS):
            px, py = _flip(self.x, fx), _flip(self.y, fy)
            yield kk, 2 * px + py, (px, py, self.c)

    def send(self):
        for w in range(len(self.outs)):
            for kk, _, to in self._chips():
                self._copy(w, self.j, self.c, kk, to).start()

    def relay(self):
        for w in range(len(self.outs)):
            for kk, jk, _ in self._chips():
                self._copy(w, jk, self.c, kk, self.sibling).wait_recv()
                if not self.whole[w]:
                    self._copy(w, jk, self.c, 3 + kk, self.sibling).start()

    def finish(self):
        for w in range(len(self.outs)):
            for kk, jk, to in self._chips():
                self._copy(w, self.j, self.c, kk, to).wait_send()
                if not self.whole[w]:
                    self._copy(w, jk, 1 - self.c, 3 + kk, self.sibling).wait_recv()
                    self._copy(w, jk, self.c, 3 + kk, self.sibling).wait_send()


def _allgather_weights(slotted, whole):
    n = len(slotted)

    def body(*refs):
        g = _Gather(refs[n:2 * n], refs[2 * n], refs[2 * n + 1], whole)
        g.send()
        g.relay()
        g.finish()

    any_spec = pl.BlockSpec(memory_space=pl.ANY)
    return pl.pallas_call(
        body, name="allgather_weights",
        in_specs=[any_spec] * n, out_specs=[any_spec] * n,
        out_shape=[jax.ShapeDtypeStruct(a.shape, a.dtype) for a in slotted],
        input_output_aliases={i: i for i in range(n)},
        scratch_shapes=[pltpu.SemaphoreType.DMA((n, 6)), pltpu.SemaphoreType.DMA((n, 6))],
    )(*slotted)


class _ChipExchange:
    def __init__(self, ins, outs, send_sems, recv_sems):
        self.ins, self.outs, self.send_sems, self.recv_sems = ins, outs, send_sems, recv_sems
        self.x, self.y, self.c = _place()

    def _copies(self):
        for w in range(len(self.ins)):
            for kk, (fx, fy) in enumerate(_CHIP_FLIPS):
                px, py = _flip(self.x, fx), _flip(self.y, fy)
                yield pltpu.make_async_remote_copy(
                    src_ref=self.ins[w].at[2 * px + py], dst_ref=self.outs[w].at[kk],
                    send_sem=self.send_sems.at[w, kk], recv_sem=self.recv_sems.at[w, kk],
                    device_id=(px, py, self.c), device_id_type=MESH)

    def send(self):
        for cp in self._copies():
            cp.start()

    def finish(self):
        for cp in self._copies():
            cp.wait()


def _pair_exchange(grads, name):
    nw = len(grads)

    def body(*refs):
        ins, outs = refs[:nw], refs[nw:2 * nw]
        send_sems, recv_sems = refs[2 * nw:]
        x, y, c = _place()
        sibling = (x, y, 1 - c)
        cps = []
        for w in range(nw):
            half = ins[w].shape[1] // 2
            cp = pltpu.make_async_remote_copy(
                src_ref=ins[w].at[:, pl.ds((1 - c) * half, half), :], dst_ref=outs[w],
                send_sem=send_sems.at[w], recv_sem=recv_sems.at[w], device_id=sibling, device_id_type=MESH)
            cp.start()
            cps.append(cp)
        for cp in cps:
            cp.wait()

    any_spec = pl.BlockSpec(memory_space=pl.ANY)
    return pl.pallas_call(
        body, name=name,
        in_specs=[any_spec] * nw, out_specs=[any_spec] * nw,
        out_shape=[jax.ShapeDtypeStruct((g.shape[0], g.shape[1] // 2, g.shape[2]), g.dtype) for g in grads],
        scratch_shapes=[pltpu.SemaphoreType.DMA((nw,)), pltpu.SemaphoreType.DMA((nw,))],
    )(*grads)


def _pair_add(g, other, c_idx, *, rb, name):
    S, r, cdim = g.shape
    half = r // 2
    nb = half // rb

    def body(c_ref, g_ref, o_ref, out_ref):
        del c_ref
        out_ref[...] = (g_ref[...].astype(F32) + o_ref[...].astype(F32)).astype(BF16)

    grid_spec = pltpu.PrefetchScalarGridSpec(
        num_scalar_prefetch=1, grid=(S, nb),
        in_specs=[pl.BlockSpec((None, rb, cdim), lambda s, i, c_ref: (s, c_ref[0] * nb + i, 0)),
                  pl.BlockSpec((None, rb, cdim), lambda s, i, c_ref: (s, i, 0))],
        out_specs=pl.BlockSpec((None, rb, cdim), lambda s, i, c_ref: (s, i, 0)))
    return pl.pallas_call(
        body, name=name, grid_spec=grid_spec, out_shape=jax.ShapeDtypeStruct((S, half, cdim), BF16),
        compiler_params=_cparams(("parallel", "parallel")),
    )(c_idx, g, other)


def _chip_sum(psum, parts, cj_idx, *, rb, name):
    S, half, cdim = psum.shape
    nb = half // rb

    def body(cj_ref, own_ref, p_ref, out_ref):
        del cj_ref
        f = lambda v: v.astype(F32)
        out_ref[...] = ((f(own_ref[...]) + f(p_ref[0])) + f(p_ref[1])) + f(p_ref[2])

    grid_spec = pltpu.PrefetchScalarGridSpec(
        num_scalar_prefetch=1, grid=(nb,),
        in_specs=[pl.BlockSpec((None, rb, cdim), lambda i, cj: (cj[1], i, 0)),
                  pl.BlockSpec((3, rb, cdim), lambda i, cj: (0, i, 0))],
        out_specs=pl.BlockSpec((rb, cdim), lambda i, cj: (cj[0] * nb + i, 0)))
    return pl.pallas_call(
        body, name=name, grid_spec=grid_spec, out_shape=jax.ShapeDtypeStruct((2 * half, cdim), F32),
        compiler_params=_cparams(("parallel",)),
    )(cj_idx, psum, parts)


SLAB_W = 1024


def _final_exchange(pieces, grads):
    n, nw = len(pieces), len(grads)
    segs, at = [], 0
    for idx, p in enumerate(pieces):
        r, wd = p.shape
        for c0 in range(0, wd, SLAB_W):
            if r > 1:
                at = -(-at // 8) * 8
            segs.append((idx, c0, min(SLAB_W, wd - c0), at))
            at += r
    rows = -(-at // 8) * 8
    flips = [(fx, fy, fc) for fx in (0, 1) for fy in (0, 1) for fc in (0, 1)][1:]

    def body(*refs):
        ins = refs[:n]
        outs = refs[n + nw:2 * n + nw]
        g_refs = refs[2 * n + nw:2 * n + 2 * nw]
        mine_ref, slots_ref, send_sems, recv_sems, gsend_sems, grecv_sems = refs[2 * n + 2 * nw:]
        x, y, c = _place()
        me = 4 * x + 2 * y + c
        sibling = (x, y, 1 - c)

        def swap(w, core):
            half = g_refs[w].shape[0] // 2
            rows_ = g_refs[w].at[pl.ds(core * half, half), :]
            return pltpu.make_async_remote_copy(
                src_ref=rows_, dst_ref=rows_, send_sem=gsend_sems.at[w], recv_sem=grecv_sems.at[w],
                device_id=sibling, device_id_type=MESH)

        for w in range(nw):
            swap(w, c).start()
        mine_ref[...] = jnp.zeros_like(mine_ref)
        for idx, c0, wd, st in segs:
            r = ins[idx].shape[0]
            mine_ref[st:st + r, 0:wd] = ins[idx][:, c0:c0 + wd]
        slots_ref[me] = mine_ref[...]
        cps = []
        for kk, (fx, fy, fc) in enumerate(flips):
            cp = pltpu.make_async_remote_copy(
                src_ref=mine_ref, dst_ref=slots_ref.at[me], send_sem=send_sems.at[kk], recv_sem=recv_sems.at[kk],
                device_id=(_flip(x, fx), _flip(y, fy), _flip(c, fc)), device_id_type=MESH)
            cp.start()
            cps.append(cp)
        for cp in cps:
            cp.wait()
        tot = slots_ref[0]
        for d in range(1, 8):
            tot = tot + slots_ref[d]
        mine_ref[...] = tot
        for idx, c0, wd, st in segs:
            r = ins[idx].shape[0]
            val = mine_ref[st:st + r, 0:wd]
            if idx == n - 1:
                outs[idx][...] = jnp.sum(val, keepdims=True)
            else:
                outs[idx][:, c0:c0 + wd] = val
        for w in range(nw):
            swap(w, 1 - c).wait_recv()
            swap(w, c).wait_send()

    vm = pl.BlockSpec(memory_space=pltpu.VMEM)
    hbm = pl.BlockSpec(memory_space=pl.ANY)
    out_shape = ([jax.ShapeDtypeStruct(p.shape, F32) for p in pieces[:-1]] + [jax.ShapeDtypeStruct((1, 1), F32)]
                 + [jax.ShapeDtypeStruct(g.shape, g.dtype) for g in grads])
    res = pl.pallas_call(
        body, name="final_exchange", in_specs=[vm] * n + [hbm] * nw, out_specs=[vm] * n + [hbm] * nw,
        out_shape=out_shape, input_output_aliases={n + w: n + w for w in range(nw)},
        scratch_shapes=[pltpu.VMEM((rows, SLAB_W), F32), pltpu.VMEM((8, rows, SLAB_W), F32),
                        pltpu.SemaphoreType.DMA((7,)), pltpu.SemaphoreType.DMA((7,)),
                        pltpu.SemaphoreType.DMA((nw,)), pltpu.SemaphoreType.DMA((nw,))],
    )(*pieces, *grads)
    return res[:n], res[n:]


def _adamw(w, g, m, v, *, rb, name):
    r, cdim = w.shape

    def body(w_ref, g_ref, m_ref, v_ref, go_ref, d_ref, nm_ref, nv_ref):
        go_ref[...] = g_ref[...]
        d_ref[...], nm_ref[...], nv_ref[...] = _adamw_update(w_ref[...], g_ref[...], m_ref[...], v_ref[...])

    spec = pl.BlockSpec((rb, cdim), lambda i: (i, 0))
    shp = jax.ShapeDtypeStruct((r, cdim), F32)
    return pl.pallas_call(
        body, name=name, grid=(r // rb,), in_specs=[spec] * 4, out_specs=[spec] * 4, out_shape=[shp] * 4,
        compiler_params=_cparams(("parallel",)),
    )(w, g, m, v)


def _adamw_update(w, g, m, v):
    nm = ADAM_B1 * m + (1.0 - ADAM_B1) * g
    nv = ADAM_B2 * v + (1.0 - ADAM_B2) * (g * g)
    m_hat = nm / (1.0 - ADAM_B1 ** ADAM_STEP)
    v_hat = nv / (1.0 - ADAM_B2 ** ADAM_STEP)
    return -ADAM_LR * (m_hat / (jnp.sqrt(v_hat) + ADAM_EPS) + ADAM_WD * w), nm, nv


def _adamw_small(params):
    n = len(params)

    def body(*refs):
        ins, outs = refs[:4 * n], refs[4 * n:]
        for p in range(n):
            w_ref, g_ref, m_ref, v_ref = ins[4 * p:4 * p + 4]
            d, nm, nv = _adamw_update(w_ref[...], g_ref[...], m_ref[...], v_ref[...])
            outs[3 * p][...] = d
            outs[3 * p + 1][...] = nm
            outs[3 * p + 2][...] = nv

    vm = pl.BlockSpec(memory_space=pltpu.VMEM)
    flat = [a for p in params for a in p]
    out_shape = [jax.ShapeDtypeStruct(p[0].shape, F32) for p in params for _ in range(3)]
    res = pl.pallas_call(
        body, name="adamw_small", in_specs=[vm] * (4 * n), out_specs=[vm] * (3 * n), out_shape=out_shape,
    )(*flat)
    return [tuple(res[3 * p:3 * p + 3]) for p in range(n)]


_PAIR_ADD_ROWS = {"w_in": 256, "w_out": 128, "w_up": 256, "w_down": 176}


def _pair_sums(grads, names, c_idx, tag):
    others = _pair_exchange(grads, name=f"grad_pair_exchange_{tag}")
    return [_pair_add(g, o_, c_idx, rb=_PAIR_ADD_ROWS[n], name=f"pair_add_{n}") for g, o_, n in zip(grads, others, names)]


def _local_step(x, tgt, meta_full, lb_param, attn_norm_w, w_in_g, hgrn_norm_w, conv_w_full, w_out_full,
                ffn_norm_w, late_slotted, fcw_full, ffn_conv_b, final_norm_w, c_idx, cj_idx):
    seq = x.shape[0]
    T = TR + seq
    head_tile = jnp.concatenate([jnp.zeros((PAD, D), F32), meta_full], axis=0)
    whn_t = jnp.tile(hgrn_norm_w, (1, NH))

    ut, projb, o, sst, mt, h1, w_up_g, w_down_g = _mix_block_fwd(
        x, head_tile, lb_param, attn_norm_w, whn_t, conv_w_full, w_in_g, w_out_full, late_slotted, (True, False))
    w_down_full = w_down_g.reshape(DFF, D)
    u2t, upb, ggt, dh2, loss_vec, dwfin = _ffn_block_fwd(
        h1, tgt, ffn_norm_w, w_up_g, fcw_full, ffn_conv_b, w_down_full, final_norm_w.reshape(1, D))

    dup, dh1, dfw, dfb, dwffn = _ffn_block_bwd(
        dh2, upb, h1, ffn_norm_w, w_up_g, fcw_full, ffn_conv_b, w_down_full)
    kb = 1408 if T % 1408 == 0 else TR
    g_up = _weight_grad(u2t, dup, bn=DFF, bk=kb, name="dw_up_mm", shard_cols=2 * DFF // NSHARD)
    g_down = _weight_grad(ggt, dh2, bn=D // 2, bk=2816 if T % 2816 == 0 else kb, name="dw_down_mm")
    ps_ffn = _pair_sums([g_up, g_down.reshape(NSHARD, DFF // NSHARD, D)], ("w_up", "w_down"), c_idx, "ffn")

    dproj, dlb, dwhn, dcw, *parts_ffn = _mix_block_bwd(
        dh1, projb, o, sst, lb_param, whn_t, conv_w_full, w_out_full, ps_ffn)
    kb_deep = 2816 if T % 2816 == 0 else kb
    g_in = _weight_grad(ut, dproj, bn=9 * D // NSHARD, bk=kb_deep, name="dw_in_mm", shard_cols=9 * D // NSHARD)
    g_out = _weight_grad(mt, dh1, bn=D, bk=kb_deep, name="dw_out_mm")
    ps_mix = _pair_sums([g_in, g_out.reshape(NSHARD, D // NSHARD, D)], ("w_in", "w_out"), c_idx, "mix")

    grad_x, dmeta, dwattn, *parts_mix = _input_grad_block(dproj, x, head_tile, dh1, attn_norm_w, w_in_g, ps_mix)

    halves = [_chip_sum(ps, p, cj_idx, rb=_PAIR_ADD_ROWS[n], name=f"chip_sum_{n}")
              for ps, p, n in zip(ps_mix + ps_ffn, parts_mix + parts_ffn, ("w_in", "w_out", "w_up", "w_down"))]
    small = dict(dlb=dlb, dwattn=dwattn, dwhn=dwhn, dwffn=dwffn, dfb=dfb, dwfin=dwfin,
                 dcw=dcw, dfw=dfw, dmeta=dmeta, loss=loss_vec)
    return grad_x, small, halves


_SMALL_ORDER = ("dmeta", "dcw", "dfw", "dlb", "dwattn", "dwhn", "dwffn", "dfb", "dwfin", "loss")


def kernel(x, meta_tokens, lb_param, attn_norm_w, w_in, hgrn_norm_w, conv_w, w_out, ffn_norm_w, w_up, ffn_conv_w, ffn_conv_b, w_down, final_norm_w, loss_target, m_meta_tokens, m_lb_param, m_attn_norm_w, m_w_in, m_hgrn_norm_w, m_conv_w, m_w_out, m_ffn_norm_w, m_w_up, m_ffn_conv_w, m_ffn_conv_b, m_w_down, m_final_norm_w, v_meta_tokens, v_lb_param, v_attn_norm_w, v_w_in, v_hgrn_norm_w, v_conv_w, v_w_out, v_ffn_norm_w, v_w_up, v_ffn_conv_w, v_ffn_conv_b, v_w_down, v_final_norm_w):
    xi, yi, ci = _place()
    j = 2 * xi + yi
    c_idx = jnp.reshape(ci, (1,)).astype(jnp.int32)

    j_idx = jnp.reshape(j, (1,)).astype(jnp.int32)
    ds_, fs_ = D // NSHARD, DFF // NSHARD
    widen = lambda a: jnp.pad(a, ((0, 0), (0, 768 - a.shape[1])))
    rows_small = jnp.concatenate([widen(meta_tokens), widen(conv_w[0]), widen(ffn_conv_w[0]),
                                  jnp.zeros((2, 768), F32)], axis=0)
    s_in, s_out, s_up, s_down = [
        _into_slot(w[0], j_idx, rb=rb, dtype=BF16, name=f"slot_{n}", paired=pr)
        for w, rb, n, pr in ((w_in, 256, "w_in", False), (w_out, 128, "w_out", False), (w_up, 256, "w_up", True),
                             (w_down, 176, "w_down", False))]
    s_small = _into_slot(rows_small, j_idx, rb=rows_small.shape[0], dtype=F32, name="slot_small")
    w_in_g, w_out_g, small_g = _allgather_weights([s_in, s_out, s_small], (False, False, True))
    unshard = lambda a: jnp.transpose(a, (1, 0, 2)).reshape(a.shape[1], -1)
    meta_full = unshard(small_g[:, 0:NMETA, 0:ds_])
    conv_w_full = unshard(small_g[:, NMETA:NMETA + 3, 0:ds_])
    fcw_full = unshard(small_g[:, NMETA + 3:NMETA + 6, 0:fs_])

    cj_idx = jnp.stack([ci, j]).astype(jnp.int32)
    grad_x, small, halves = _local_step(
        x[0], loss_target[0], meta_full, lb_param, attn_norm_w, w_in_g, hgrn_norm_w, conv_w_full,
        w_out_g.reshape(D, D), ffn_norm_w, [s_up, s_down], fcw_full, ffn_conv_b, final_norm_w, c_idx, cj_idx)

    names = _SMALL_ORDER
    small_sums, g_big = _final_exchange([small[n] for n in names], halves)
    vals = dict(zip(names, small_sums))
    loss = vals["loss"].reshape(())
    g_small = {
        "meta_tokens": lax.dynamic_slice_in_dim(vals["dmeta"], j * (D // NSHARD), D // NSHARD, axis=1),
        "lb_param": jnp.concatenate([vals["dlb"], -vals["dlb"]], axis=0),
        "attn_norm_w": vals["dwattn"],
        "hgrn_norm_w": vals["dwhn"],
        "conv_w": lax.dynamic_slice_in_dim(vals["dcw"], j * (D // NSHARD), D // NSHARD, axis=1)[None],
        "ffn_norm_w": vals["dwffn"],
        "ffn_conv_w": lax.dynamic_slice_in_dim(vals["dfw"], j * (DFF // NSHARD), DFF // NSHARD, axis=1)[None],
        "ffn_conv_b": vals["dfb"],
        "final_norm_w": vals["dwfin"].reshape(D),
    }


    weights = {"meta_tokens": meta_tokens, "lb_param": lb_param, "attn_norm_w": attn_norm_w, "w_in": w_in,
               "hgrn_norm_w": hgrn_norm_w, "conv_w": conv_w, "w_out": w_out, "ffn_norm_w": ffn_norm_w,
               "w_up": w_up, "ffn_conv_w": ffn_conv_w, "ffn_conv_b": ffn_conv_b, "w_down": w_down,
               "final_norm_w": final_norm_w}
    ms = {"meta_tokens": m_meta_tokens, "lb_param": m_lb_param, "attn_norm_w": m_attn_norm_w, "w_in": m_w_in,
          "hgrn_norm_w": m_hgrn_norm_w, "conv_w": m_conv_w, "w_out": m_w_out, "ffn_norm_w": m_ffn_norm_w,
          "w_up": m_w_up, "ffn_conv_w": m_ffn_conv_w, "ffn_conv_b": m_ffn_conv_b, "w_down": m_w_down,
          "final_norm_w": m_final_norm_w}
    vs = {"meta_tokens": v_meta_tokens, "lb_param": v_lb_param, "attn_norm_w": v_attn_norm_w, "w_in": v_w_in,
          "hgrn_norm_w": v_hgrn_norm_w, "conv_w": v_conv_w, "w_out": v_w_out, "ffn_norm_w": v_ffn_norm_w,
          "w_up": v_w_up, "ffn_conv_w": v_ffn_conv_w, "ffn_conv_b": v_ffn_conv_b, "w_down": v_w_down,
          "final_norm_w": v_final_norm_w}
    order = list(weights)
    grads, deltas, new_m, new_v = {}, {}, {}, {}

    for name, g, rb in zip(("w_in", "w_out", "w_up", "w_down"), g_big, (256, 128, 256, 176)):
        shp = weights[name].shape
        w2, m2, v2 = (a.reshape(shp[1], shp[2]) for a in (weights[name], ms[name], vs[name]))
        g_, d_, nm_, nv_ = _adamw(w2, g, m2, v2, rb=rb, name=f"adamw_{name}")
        grads[name], deltas[name], new_m[name], new_v[name] = (a.reshape(shp) for a in (g_, d_, nm_, nv_))

    small_names = [n for n in order if n not in grads]
    as2d = lambda a: a.reshape(-1, a.shape[-1])
    res = _adamw_small([tuple(as2d(a) for a in (weights[n], g_small[n], ms[n], vs[n])) for n in small_names])
    for n, (d_, nm_, nv_) in zip(small_names, res):
        shp = weights[n].shape
        grads[n], deltas[n], new_m[n], new_v[n] = (a.reshape(shp) for a in (g_small[n], d_, nm_, nv_))

    return (loss, grad_x[None], *[grads[n] for n in order], *[deltas[n] for n in order],
            *[new_m[n] for n in order], *[new_v[n] for n in order])
```

```python
import jax
import jax.numpy as jnp
from jax import lax
from jax.experimental import pallas as pl
from jax.experimental.pallas import tpu as pltpu

F32 = jnp.float32
BF16 = jnp.bfloat16
MESH = pl.DeviceIdType.MESH

D = 1024
NH = 8
HD = 128
DFF = 2816
NMETA = 16
EPS = 1e-6
TR = 256
PAD = TR - NMETA
CH = 64
NSHARD = 4
VMEM_LIMIT = 62 * 1024 * 1024

ADAM_LR = 0.001
ADAM_B1 = 0.9
ADAM_B2 = 0.999
ADAM_EPS = 1e-08
ADAM_WD = 0.01
ADAM_STEP = 10


def _cparams(semantics=None, **kw):
    return pltpu.CompilerParams(dimension_semantics=semantics, vmem_limit_bytes=VMEM_LIMIT, **kw)


def _sigmoid(x):
    return 0.5 * jnp.tanh(0.5 * x) + 0.5


def _weight_grad(at, b, *, bn, bk, name, shard_cols=None, ride=()):
    M, T = at.shape
    N = b.shape[1]
    assert b.shape[0] == T and T % bk == 0 and N % bn == 0, (name, M, N, T, bn, bk)
    nk = T // bk
    nj = N // bn
    nex = len(ride)
    if shard_cols is None:
        nsh = 1
        blk = (M, bn)
        out_shape = jax.ShapeDtypeStruct((M, N), BF16)
        out_spec = pl.BlockSpec(blk, lambda j, k: (0, j))
    else:
        assert bn % shard_cols == 0 and N % shard_cols == 0
        nsh = bn // shard_cols
        blk = (nsh, M, shard_cols)
        out_shape = jax.ShapeDtypeStruct((N // shard_cols, M, shard_cols), BF16)
        out_spec = pl.BlockSpec(blk, lambda j, k: (j, 0, 0))

    def body(*refs):
        a_ref, b_ref = refs[:2]
        o_ref = refs[2 + nex]
        acc_ref = refs[3 + 2 * nex]
        j, k = pl.program_id(0), pl.program_id(1)
        if nex:
            exchange = _PairExchange(refs[2:2 + nex], refs[3 + nex:3 + 2 * nex], refs[-2], refs[-1])

            @pl.when(jnp.logical_and(j == 0, k == 0))
            def _():
                exchange.send()

        @pl.when(k == 0)
        def _():
            acc_ref[...] = jnp.zeros_like(acc_ref)

        p = jnp.dot(a_ref[...].astype(BF16), b_ref[...].astype(BF16), preferred_element_type=F32)
        if shard_cols is None:
            acc_ref[...] += p
        else:
            for q in range(nsh):
                acc_ref[q] += p[:, q * shard_cols:(q + 1) * shard_cols]

        @pl.when(k == nk - 1)
        def _():
            o_ref[...] = acc_ref[...].astype(BF16)

        if nex:
            @pl.when(jnp.logical_and(j == nj - 1, k == nk - 1))
            def _():
                exchange.finish()

    hbm = pl.BlockSpec(memory_space=pl.ANY)
    scratch = [pltpu.VMEM(blk, F32)]
    if nex:
        scratch += [pltpu.SemaphoreType.DMA((nex,)), pltpu.SemaphoreType.DMA((nex,))]
    res = pl.pallas_call(
        body, name=name, grid=(nj, nk),
        in_specs=[pl.BlockSpec((M, bk), lambda j, k: (0, k)), pl.BlockSpec((bk, bn), lambda j, k: (k, j))] + [hbm] * nex,
        out_specs=[out_spec] + [hbm] * nex, out_shape=[out_shape] + _pair_shapes(ride), scratch_shapes=scratch,
        compiler_params=_cparams(("arbitrary", "arbitrary") if nex else ("parallel", "arbitrary")),
    )(at, b, *ride)
    return (res[0], list(res[1:])) if nex else res[0]


def _input_grad_block(dproj, x_seq, head_tile, dres, w, w_in_g, psums):
    T = TR + x_seq.shape[0]
    nt = T // TR
    nsl = w_in_g.shape[0]
    wsl = w_in_g.shape[2]
    wcol = wsl // 2
    nex = len(psums)

    def body(*refs):
        dp_ref, x_ref, head_ref, dres_ref, w_ref, win_ref = refs[:6]
        gx_ref, dmeta_ref, dw_ref = refs[6 + nex:9 + nex]
        exchange = _ChipExchange(refs[6:6 + nex], refs[9 + nex:9 + 2 * nex], refs[-2], refs[-1])
        i = pl.program_id(0)

        @pl.when(i == 0)
        def _():
            exchange.send()

        @pl.when(i == nt - 1)
        def _():
            exchange.finish()

        x = jnp.where(i == 0, head_ref[...], x_ref[...])
        r = lax.rsqrt(jnp.mean(x * x, axis=-1, keepdims=True) + EPS)
        n = x * r
        du_v = None
        for s in range(nsl):
            part = _dot_nt(dp_ref[:, s * wsl:(s + 1) * wsl], win_ref[s])
            du_v = part if du_v is None else du_v + part
        dn = du_v * w_ref[...]
        dh = dres_ref[...] + r * (dn - n * jnp.mean(dn * n, axis=-1, keepdims=True))
        gx_ref[...] = dh
        part = jnp.sum(du_v * n, axis=0, keepdims=True)

        @pl.when(i == 0)
        def _():
            dmeta_ref[...] = dh[PAD:TR, :]
            dw_ref[...] = part

        @pl.when(i > 0)
        def _():
            dw_ref[...] += part

    row = pl.BlockSpec((TR, D), lambda i: (i, 0))
    vec = pl.BlockSpec((1, D), lambda i: (0, 0))
    hbm = pl.BlockSpec(memory_space=pl.ANY)
    return pl.pallas_call(
        body, name="input_grad_block", grid=(nt,),
        in_specs=[pl.BlockSpec((TR, 9 * D), lambda i: (i, 0)),
                  pl.BlockSpec((TR, D), lambda i: (jnp.maximum(i - 1, 0), 0)), pl.BlockSpec((TR, D), lambda i: (0, 0)),
                  row, vec, pl.BlockSpec(memory_space=pltpu.VMEM)] + [hbm] * nex,
        out_specs=[pl.BlockSpec((TR, D), lambda i: (jnp.maximum(i - 1, 0), 0)),
                   pl.BlockSpec((NMETA, D), lambda i: (0, 0)), vec] + [hbm] * nex,
        out_shape=[jax.ShapeDtypeStruct((T - TR, D), F32), jax.ShapeDtypeStruct((NMETA, D), F32),
                   jax.ShapeDtypeStruct((1, D), F32)]
        + [jax.ShapeDtypeStruct((3,) + p.shape[1:], p.dtype) for p in psums],
        scratch_shapes=[pltpu.SemaphoreType.DMA((nex, 3)), pltpu.SemaphoreType.DMA((nex, 3))],
        compiler_params=_cparams(("arbitrary",)),
    )(dproj, x_seq, head_tile, dres, w, w_in_g, *psums)


def _tri_matmul(tri_bf16, x):
    hi = x.astype(BF16)
    lo = (x - hi.astype(F32)).astype(BF16)
    return jnp.dot(tri_bf16, lo, preferred_element_type=F32) + jnp.dot(tri_bf16, hi, preferred_element_type=F32)


def _shift_down(x, prev8, n):
    rows = x.shape[0]
    return pltpu.roll(jnp.concatenate([prev8, x], axis=0), n, 0)[8:8 + rows, :]


def _shift_up(x, next8, n):
    rows = x.shape[0]
    return pltpu.roll(jnp.concatenate([x, next8], axis=0), rows + 8 - n, 0)[0:rows, :]


def _gates(f_raw, lb):
    sg = _sigmoid(f_raw)
    f = lb + (1.0 - lb) * sg
    return sg, f, jnp.log(f), 1.0 - f


def _lower_bound(lbp_ref):
    return _sigmoid(lbp_ref[0:1, :] - lbp_ref[1:2, :])


def _tri_masks():
    r = lax.broadcasted_iota(jnp.int32, (CH, CH), 0)
    c = lax.broadcasted_iota(jnp.int32, (CH, CH), 1)
    return r >= c, r <= c


def _ones_where(mask):
    return jnp.where(mask, 1.0, 0.0).astype(BF16)


def _dot(a, b):
    return jnp.dot(a.astype(BF16), b.astype(BF16), preferred_element_type=F32)


def _dot_nt(a, b):
    return lax.dot_general(a.astype(BF16), b.astype(BF16), (((1,), (1,)), ((), ())), preferred_element_type=F32)


def _dot_tn(a, b):
    return lax.dot_general(a.astype(BF16), b.astype(BF16), (((0,), (0,)), ((), ())), preferred_element_type=F32)


def _mix_block_fwd(x_seq, head_tile, lb_param, wattn, whn, conv_w, w_in_g, w_out, late, late_paired):
    T = TR + x_seq.shape[0]
    nt = T // TR
    ncht = TR // CH
    nsl = w_in_g.shape[0]
    wsl = w_in_g.shape[2]
    wcol = wsl // 2

    nlate = len(late)

    def body(*refs):
        x_ref, head_ref, lbp_ref, wa_ref, whn_ref, cw_ref, win_ref, wout_ref = refs[:8]
        ut_ref, projb_ref, o_ref, sst_ref, mt_ref, h1_ref = refs[8 + nlate:14 + nlate]
        late_refs = refs[14 + nlate:14 + 2 * nlate]
        proj_ref, m_ref, st_ref, cxc_ref, send_sems, recv_sems = refs[14 + 2 * nlate:]
        i = pl.program_id(0)
        gather = _Gather(late_refs, send_sems, recv_sems, (False,) * nlate, late_paired)

        @pl.when(i == 0)
        def _():
            st_ref[...] = jnp.zeros_like(st_ref)
            cxc_ref[...] = jnp.zeros_like(cxc_ref)
            gather.send()

        @pl.when(i == nt // 2)
        def _():
            gather.relay()

        x = jnp.where(i == 0, head_ref[...], x_ref[...])
        r1 = lax.rsqrt(jnp.mean(x * x, axis=-1, keepdims=True) + EPS)
        u_f = x * r1 * wa_ref[...]
        u = u_f.astype(BF16)
        ut_ref[...] = u_f.T.astype(BF16)
        for s in range(nsl):
            cols = slice(s * wsl, (s + 1) * wsl)
            p = jnp.dot(u, win_ref[s], preferred_element_type=F32)
            proj_ref[:, cols] = p
            projb_ref[:, cols] = p.astype(BF16)

        lb = _lower_bound(lbp_ref)
        lower, _ = _tri_masks()
        ltri = _ones_where(lower)
        whn_v = whn_ref[...]
        w0, w1, w2 = cw_ref[0:1, :], cw_ref[1:2, :], cw_ref[2:3, :]

        def gates_and_conv(c):
            rows = slice(c * CH, (c + 1) * CH)
            q_raw = proj_ref[rows, 0:D]
            f_raw = proj_ref[rows, D:2 * D]
            q = q_raw * _sigmoid(q_raw)
            _, _, g, k = _gates(f_raw, lb)
            gam = _tri_matmul(ltri, g)
            gam_l = gam[CH - 1:CH, :]
            g_out = proj_ref[rows, 3 * D:4 * D]
            cx = proj_ref[rows, 5 * D:6 * D] * proj_ref[rows, 6 * D:7 * D]
            prev8 = cxc_ref[...]
            cv = w0 * _shift_down(cx, prev8, 2) + w1 * _shift_down(cx, prev8, 1) + w2 * cx
            cxc_ref[...] = cx[CH - 8:CH, :]
            y_b = proj_ref[rows, 4 * D:5 * D] * cv
            return dict(
                e_l=jnp.exp(gam_l), qt=(q * jnp.exp(gam)).astype(BF16), kt=(k * jnp.exp(-gam)).astype(BF16),
                khat=(k * jnp.exp(gam_l - gam)).astype(BF16), vb=proj_ref[rows, 2 * D:3 * D].astype(BF16),
                out_gate=whn_v * (g_out * _sigmoid(g_out)), merge_a=_sigmoid(proj_ref[rows, 7 * D:8 * D]),
                merged_b=_sigmoid(proj_ref[rows, 8 * D:9 * D]) * y_b)

        def recurrence_and_merge(c, pre):
            rows = slice(c * CH, (c + 1) * CH)
            qt, kt, khat, vb, e_l = pre["qt"], pre["kt"], pre["khat"], pre["vb"], pre["e_l"]
            heads = [slice(h * HD, (h + 1) * HD) for h in range(NH)]
            sts = [st_ref[h] for h in range(NH)]
            scores = [_dot_nt(qt[:, cs], kt[:, cs]) for cs in heads]
            o_state = [_dot_nt(qt[:, cs], st) for cs, st in zip(heads, sts)]
            st_new = [_dot_tn(vb[:, cs], khat[:, cs]) for cs in heads]
            on_parts = []
            for h, cs in enumerate(heads):
                sst_ref[c, h] = sts[h]
                st_ref[h] = sts[h] * e_l[:, cs] + st_new[h]
                o_h = o_state[h] + _dot(jnp.where(lower, scores[h], 0.0), vb[:, cs])
                o_ref[rows, cs] = o_h
                ro = lax.rsqrt(jnp.mean(o_h * o_h, axis=-1, keepdims=True) + EPS)
                on_parts.append(o_h * ro)
            on = jnp.concatenate(on_parts, axis=1)
            m_ref[rows, :] = pre["merge_a"] * (on * pre["out_gate"]) + pre["merged_b"]

        pre = {0: gates_and_conv(0)}
        for c in range(ncht):
            if c + 1 < ncht:
                pre[c + 1] = gates_and_conv(c + 1)
            recurrence_and_merge(c, pre.pop(c))
        m_v = m_ref[...]
        h1_ref[...] = x + jnp.dot(m_v.astype(BF16), wout_ref[...], preferred_element_type=F32)
        mt_ref[...] = m_v.T.astype(BF16)

        @pl.when(i == nt - 1)
        def _():
            gather.finish()

    row = lambda w: pl.BlockSpec((TR, w), lambda i: (i, 0))
    col = lambda w: pl.BlockSpec((w, TR), lambda i: (0, i))
    vec = lambda r: pl.BlockSpec((r, D), lambda i: (0, 0))
    vm = pl.BlockSpec(memory_space=pltpu.VMEM)
    hbm = pl.BlockSpec(memory_space=pl.ANY)
    return pl.pallas_call(
        body, name="mix_block_fwd", grid=(nt,),
        in_specs=[pl.BlockSpec((TR, D), lambda i: (jnp.maximum(i - 1, 0), 0)), pl.BlockSpec((TR, D), lambda i: (0, 0)),
                  vec(2), vec(1), vec(1), vec(3), vm, vm] + [hbm] * nlate,
        out_specs=[col(D), row(9 * D), row(D), pl.BlockSpec((ncht, NH, HD, HD), lambda i: (i, 0, 0, 0)),
                   col(D), row(D)] + [hbm] * nlate,
        out_shape=[jax.ShapeDtypeStruct((D, T), BF16), jax.ShapeDtypeStruct((T, 9 * D), BF16),
                   jax.ShapeDtypeStruct((T, D), F32), jax.ShapeDtypeStruct((T // CH, NH, HD, HD), F32),
                   jax.ShapeDtypeStruct((D, T), BF16), jax.ShapeDtypeStruct((T, D), F32)]
        + [jax.ShapeDtypeStruct(a.shape, a.dtype) for a in late],
        input_output_aliases={8 + n: 6 + n for n in range(nlate)},
        scratch_shapes=[pltpu.VMEM((TR, 9 * D), F32), pltpu.VMEM((TR, D), F32), pltpu.VMEM((NH, HD, HD), F32),
                        pltpu.VMEM((8, D), F32), pltpu.SemaphoreType.DMA((nlate, 6)),
                        pltpu.SemaphoreType.DMA((nlate, 6))],
        compiler_params=_cparams(("arbitrary",)),
    )(x_seq, head_tile, lb_param, wattn, whn, conv_w, w_in_g, w_out, *late)


def _mix_block_bwd(dh1, projb, o, sst, lb_param, whn, conv_w, w_out, psums):
    T = projb.shape[0]
    nt = T // TR
    ncht = TR // CH
    tb16 = TR // 16
    nex = len(psums)

    def body(*refs):
        dh1_ref, proj_ref, pc_ref, px_ref, o_ref, sst_ref, lbp_ref, whn_ref, cw_ref, wout_ref = refs[:10]
        dproj_ref, dlb_ref, dwhn_ref, dcw_ref = refs[10 + nex:14 + nex]
        exchange = _ChipExchange(refs[10:10 + nex], refs[14 + nex:14 + 2 * nex], refs[-2], refs[-1])
        dm_ref, dst_ref, dcvc_ref, acc_lb, acc_hn, acc_cw = refs[14 + 2 * nex:-2]
        s = pl.program_id(0)
        tile = nt - 1 - s

        @pl.when(s == 0)
        def _():
            dst_ref[...] = jnp.zeros_like(dst_ref)
            dcvc_ref[...] = jnp.zeros_like(dcvc_ref)
            acc_lb[...] = jnp.zeros_like(acc_lb)
            acc_hn[...] = jnp.zeros_like(acc_hn)
            acc_cw[...] = jnp.zeros_like(acc_cw)
            exchange.send()

        dm_ref[...] = _dot_nt(dh1_ref[...], wout_ref[...])
        lb = _lower_bound(lbp_ref)
        lower, upper = _tri_masks()
        ltri = _ones_where(lower)
        utri = _ones_where(upper)
        whn_v = whn_ref[...]
        w0, w1, w2 = cw_ref[0:1, :], cw_ref[1:2, :], cw_ref[2:3, :]
        cx_before_tile = jnp.where(tile > 0, (pc_ref[...].astype(F32) * px_ref[...].astype(F32))[8:16, :], 0.0)
        rid = lax.broadcasted_iota(jnp.int32, (CH, D), 0)

        def chunk(cc, carry):
            c = ncht - 1 - cc
            r0 = pl.multiple_of(c * CH, CH)
            rows = pl.ds(r0, CH)
            slab = lambda n: proj_ref[rows, n * D:(n + 1) * D].astype(F32)
            q_raw, f_raw, v, g_out, b_gate, c_gate, x_conv = (slab(n) for n in range(7))
            sa = _sigmoid(slab(7))
            sb = _sigmoid(slab(8))
            dm_v = dm_ref[rows, :]

            sq = _sigmoid(q_raw)
            q = q_raw * sq
            sg, f, g, k = _gates(f_raw, lb)
            gam = _tri_matmul(ltri, g)
            gam_l = gam[CH - 1:CH, :]
            e_l = jnp.exp(gam_l)
            e_g = jnp.exp(gam)
            e_ng = jnp.exp(-gam)
            e_kl = jnp.exp(gam_l - gam)
            qt = q * e_g
            kt = k * e_ng
            khat = k * e_kl
            qt_b, kt_b, khat_b, v_b = qt.astype(BF16), kt.astype(BF16), khat.astype(BF16), v.astype(BF16)
            qt_seen, kt_seen = qt_b.astype(F32), kt_b.astype(F32)
            s_go = _sigmoid(g_out)
            silu_go = g_out * s_go
            cx = c_gate * x_conv
            rprev = pl.ds(pl.multiple_of(jnp.maximum(r0 - 16, 0), 16), 16)
            cx_prev_in = (proj_ref[rprev, 5 * D:6 * D].astype(F32) * proj_ref[rprev, 6 * D:7 * D].astype(F32))[8:16, :]
            prev8 = jnp.where(c > 0, cx_prev_in, cx_before_tile)
            cx_m1 = _shift_down(cx, prev8, 1)
            cx_m2 = _shift_down(cx, prev8, 2)
            cv = w0 * cx_m2 + w1 * cx_m1 + w2 * cx
            y_b = b_gate * cv

            o_v = o_ref[rows, :]
            ro_parts, on_parts = [], []
            for h in range(NH):
                cs = slice(h * HD, (h + 1) * HD)
                o_h = o_v[:, cs]
                ro = lax.rsqrt(jnp.mean(o_h * o_h, axis=-1, keepdims=True) + EPS)
                ro_parts.append(ro)
                on_parts.append(o_h * ro)
            on = jnp.concatenate(on_parts, axis=1)
            out_gate = whn_v * silu_go
            y_a = on * out_gate

            dy_a = dm_v * sa
            dy_b = dm_v * sb
            dproj_ref[rows, 7 * D:8 * D] = (dy_a * y_a * (1.0 - sa)).astype(BF16)
            dproj_ref[rows, 8 * D:9 * D] = (dy_b * y_b * (1.0 - sb)).astype(BF16)
            dproj_ref[rows, 4 * D:5 * D] = (dy_b * cv).astype(BF16)
            dcv = dy_b * b_gate
            acc_cw[0:1, :] += jnp.sum(dcv * cx_m2, axis=0, keepdims=True)
            acc_cw[1:2, :] += jnp.sum(dcv * cx_m1, axis=0, keepdims=True)
            acc_cw[2:3, :] += jnp.sum(dcv * cx, axis=0, keepdims=True)
            next8 = dcvc_ref[...]
            dcx = w2 * dcv + w1 * _shift_up(dcv, next8, 1) + w0 * _shift_up(dcv, next8, 2)
            dcvc_ref[...] = dcv[0:8, :]
            dproj_ref[rows, 5 * D:6 * D] = (dcx * x_conv).astype(BF16)
            dproj_ref[rows, 6 * D:7 * D] = (dcx * c_gate).astype(BF16)
            don = dy_a * out_gate
            dya_on = dy_a * on
            dproj_ref[rows, 3 * D:4 * D] = (dya_on * (whn_v * (s_go * (1.0 + g_out * (1.0 - s_go))))).astype(BF16)
            acc_hn[...] += jnp.sum(dya_on * silu_go, axis=0, keepdims=True)

            heads = [slice(h * HD, (h + 1) * HD) for h in range(NH)]
            do_bs, sts, dstns, first = [], [], [], []
            for h, cs in enumerate(heads):
                on_h = on_parts[h]
                don_h = don[:, cs]
                do_h = ro_parts[h] * (don_h - on_h * jnp.mean(don_h * on_h, axis=-1, keepdims=True))
                do_bs.append(do_h.astype(BF16))
                sts.append(sst_ref[c, h])
                dstns.append(dst_ref[h])
            for h, cs in enumerate(heads):
                qt_h, kt_h, khat_h, v_h = qt_b[:, cs], kt_b[:, cs], khat_b[:, cs], v_b[:, cs]
                do_b, dstn_b = do_bs[h], dstns[h].astype(BF16)
                first.append(dict(
                    a_t=_dot_nt(kt_h, qt_h), da=_dot_nt(do_b, v_h), da_t=_dot_nt(v_h, do_b),
                    dv_state=_dot_nt(khat_h, dstn_b), dqt_state=_dot(do_b, sts[h]), dkhat=_dot(v_h, dstn_b),
                    dst_chunk=_dot_tn(do_b, qt_h)))
            dq_parts, dk_parts, dv_parts, dgam_parts, ext_parts = [], [], [], [], []
            for h, cs in enumerate(heads):
                qt_h, kt_h = qt_b[:, cs], kt_b[:, cs]
                p1, do_b, st, dstn = first[h], do_bs[h], sts[h], dstns[h]
                dv_h = _dot(jnp.where(upper, p1["a_t"], 0.0), do_b) + p1["dv_state"]
                dqt_state = p1["dqt_state"]
                dqt_chunk = _dot(jnp.where(lower, p1["da"], 0.0), kt_h)
                dkt = _dot(jnp.where(upper, p1["da_t"], 0.0), qt_h)
                dkhat = p1["dkhat"]
                dq_h = (dqt_state + dqt_chunk) * e_g[:, cs]
                dk_h = dkt * e_ng[:, cs] + dkhat * e_kl[:, cs]
                khat_dkhat = dkhat * khat[:, cs]
                ext = (jnp.sum(khat_dkhat, axis=0, keepdims=True)
                       + e_l[:, cs] * jnp.sum(st * dstn, axis=0, keepdims=True))
                dst_ref[h] = p1["dst_chunk"] + dstn * e_l[:, cs]
                dq_parts.append(dq_h)
                dk_parts.append(dk_h)
                dv_parts.append(dv_h)
                dgam_parts.append(qt[:, cs] * dqt_state + qt_seen[:, cs] * dqt_chunk - kt_seen[:, cs] * dkt
                                  - khat_dkhat)
                ext_parts.append(ext)
            dq = jnp.concatenate(dq_parts, axis=1)
            dk = jnp.concatenate(dk_parts, axis=1)
            dgam = jnp.concatenate(dgam_parts, axis=1)
            ext = jnp.concatenate(ext_parts, axis=1)
            dgam = dgam + jnp.where(rid == CH - 1, ext, 0.0)
            dg = _tri_matmul(utri, dgam)
            dproj_ref[rows, 0:D] = (dq * (sq * (1.0 + q_raw * (1.0 - sq)))).astype(BF16)
            df = dg * jnp.exp(-g) - dk
            dproj_ref[rows, D:2 * D] = (df * (1.0 - lb) * sg * (1.0 - sg)).astype(BF16)
            dproj_ref[rows, 2 * D:3 * D] = jnp.concatenate(dv_parts, axis=1).astype(BF16)
            real = (tile * TR + r0 + rid) >= PAD
            acc_lb[...] += jnp.sum(jnp.where(real, df * (1.0 - sg), 0.0), axis=0, keepdims=True)
            return carry

        lax.fori_loop(0, ncht, chunk, 0)

        @pl.when(s == nt - 1)
        def _():
            dlb_ref[...] = acc_lb[...] * lb * (1.0 - lb)
            hn = acc_hn[...]
            tot = hn[:, 0:HD]
            for h in range(1, NH):
                tot = tot + hn[:, h * HD:(h + 1) * HD]
            dwhn_ref[...] = tot
            dcw_ref[...] = acc_cw[0:3, :]
            exchange.finish()

    hbm = pl.BlockSpec(memory_space=pl.ANY)
    rev = lambda s: (nt - 1 - s, 0)
    prevc = lambda s: (jnp.maximum((nt - 1 - s) * tb16 - 1, 0), 5)
    prevx = lambda s: (jnp.maximum((nt - 1 - s) * tb16 - 1, 0), 6)
    const = lambda s: (0, 0)
    return pl.pallas_call(
        body, name="mix_block_bwd", grid=(nt,),
        in_specs=[pl.BlockSpec((TR, D), rev),
                  pl.BlockSpec((TR, 9 * D), rev),
                  pl.BlockSpec((16, D), prevc),
                  pl.BlockSpec((16, D), prevx),
                  pl.BlockSpec((TR, D), rev),
                  pl.BlockSpec((ncht, NH, HD, HD), lambda s: (nt - 1 - s, 0, 0, 0)),
                  pl.BlockSpec((2, D), const),
                  pl.BlockSpec((1, D), const),
                  pl.BlockSpec((3, D), const),
                  pl.BlockSpec(memory_space=pltpu.VMEM)] + [hbm] * nex,
        out_specs=[pl.BlockSpec((TR, 9 * D), rev),
                   pl.BlockSpec((1, D), const),
                   pl.BlockSpec((1, HD), const),
                   pl.BlockSpec((3, D), const)] + [hbm] * nex,
        out_shape=[jax.ShapeDtypeStruct((T, 9 * D), BF16), jax.ShapeDtypeStruct((1, D), F32),
                   jax.ShapeDtypeStruct((1, HD), F32), jax.ShapeDtypeStruct((3, D), F32)]
        + [jax.ShapeDtypeStruct((3,) + p.shape[1:], p.dtype) for p in psums],
        scratch_shapes=[pltpu.VMEM((TR, D), F32), pltpu.VMEM((NH, HD, HD), F32), pltpu.VMEM((8, D), F32),
                        pltpu.VMEM((1, D), F32), pltpu.VMEM((1, D), F32), pltpu.VMEM((8, D), F32),
                        pltpu.SemaphoreType.DMA((nex, 3)), pltpu.SemaphoreType.DMA((nex, 3))],
        compiler_params=_cparams(("arbitrary",)),
    )(dh1, projb, projb, projb, o, sst, lb_param, whn, conv_w, w_out, *psums)


def _ffn_block_fwd(h1, tgt, wffn, w_up_g, fcw, fcb, w_down, wfin):
    T = h1.shape[0]
    nt = T // TR
    nsl = w_up_g.shape[0]
    wsl = w_up_g.shape[2]

    def body(h_ref, t_ref, wn_ref, wup_ref, cw_ref, cb_ref, wdn_ref, wf_ref,
             u2t_ref, upb_ref, ggt_ref, dh_ref, loss_ref, dwf_ref, up_scr, gg_ref, carry_ref):
        i = pl.program_id(0)

        @pl.when(i == 0)
        def _():
            carry_ref[...] = jnp.zeros_like(carry_ref)
            loss_ref[...] = jnp.zeros_like(loss_ref)
            dwf_ref[...] = jnp.zeros_like(dwf_ref)

        x = h_ref[...]
        r2 = lax.rsqrt(jnp.mean(x * x, axis=-1, keepdims=True) + EPS)
        u2_f = x * r2 * wn_ref[...]
        u2 = u2_f.astype(BF16)
        u2t_ref[...] = u2_f.T.astype(BF16)
        for s in range(nsl):
            up_s = jnp.dot(u2, wup_ref[s], preferred_element_type=F32)
            up_scr[:, s * wsl:(s + 1) * wsl] = up_s
            upb_ref[:, s * wsl:(s + 1) * wsl] = up_s.astype(BF16)
        w0, w1, w2 = cw_ref[0:1, :], cw_ref[1:2, :], cw_ref[2:3, :]

        def chunk(c, carry):
            rows = pl.ds(pl.multiple_of(c * CH, CH), CH)
            a_pre = up_scr[rows, 0:DFF]
            val = up_scr[rows, DFF:2 * DFF]
            prev8 = carry_ref[...]
            a = w0 * _shift_down(a_pre, prev8, 2) + w1 * _shift_down(a_pre, prev8, 1) + w2 * a_pre + cb_ref[...]
            carry_ref[...] = a_pre[CH - 8:CH, :]
            gg_ref[rows, :] = a * _sigmoid(a) * val
            return carry

        lax.fori_loop(0, TR // CH, chunk, 0, unroll=True)
        gg_v = gg_ref[...]
        ggt_ref[...] = gg_v.T.astype(BF16)
        h2 = x + jnp.dot(gg_v.astype(BF16), wdn_ref[...], preferred_element_type=F32)
        r3 = lax.rsqrt(jnp.mean(h2 * h2, axis=-1, keepdims=True) + EPS)
        n3 = h2 * r3
        wf = wf_ref[...]
        diff = jnp.where(i > 0, n3 * wf - t_ref[...], 0.0)
        loss_ref[...] += jnp.sum(diff * diff, axis=0, keepdims=True) * (0.5 / D)
        dy = diff * (1.0 / D)
        dwf_ref[...] += jnp.sum(dy * n3, axis=0, keepdims=True)
        dn = dy * wf
        dh_ref[...] = r3 * (dn - n3 * jnp.mean(dn * n3, axis=-1, keepdims=True))

    row = lambda w: pl.BlockSpec((TR, w), lambda i: (i, 0))
    col = lambda w: pl.BlockSpec((w, TR), lambda i: (0, i))
    vec = lambda w, r=1: pl.BlockSpec((r, w), lambda i: (0, 0))
    vm = pl.BlockSpec(memory_space=pltpu.VMEM)
    return pl.pallas_call(
        body, name="ffn_block_fwd", grid=(nt,),
        in_specs=[row(D), pl.BlockSpec((TR, D), lambda i: (jnp.maximum(i - 1, 0), 0)), vec(D), vm,
                  vec(DFF, 3), vec(DFF), vm, vec(D)],
        out_specs=[col(D), row(2 * DFF), col(DFF), row(D), vec(D), vec(D)],
        out_shape=[jax.ShapeDtypeStruct((D, T), BF16), jax.ShapeDtypeStruct((T, 2 * DFF), BF16),
                   jax.ShapeDtypeStruct((DFF, T), BF16), jax.ShapeDtypeStruct((T, D), F32),
                   jax.ShapeDtypeStruct((1, D), F32), jax.ShapeDtypeStruct((1, D), F32)],
        scratch_shapes=[pltpu.VMEM((TR, 2 * DFF), F32), pltpu.VMEM((TR, DFF), F32), pltpu.VMEM((8, DFF), F32)],
        compiler_params=_cparams(("arbitrary",)),
    )(h1, tgt, wffn, w_up_g, fcw, fcb, w_down, wfin)


def _ffn_block_bwd(dh2, upb, h1, wffn, w_up_g, fcw, fcb, w_down):
    T = h1.shape[0]
    nt = T // TR
    ncht = TR // CH
    nsl = w_up_g.shape[0]
    wsl = w_up_g.shape[2]
    tb16 = TR // 16
    assert ncht % nsl == 0
    every = ncht // nsl
    step = -(-DFF // (ncht * 128)) * 128
    parts = [(c0, min(c0 + step, DFF)) for c0 in range(0, DFF, step)]
    assert len(parts) == ncht

    def body(dh2n_ref, dh2p_ref, up_ref, pa_ref, h_ref, wn_ref, wup_ref, cw_ref, cb_ref, wdn_ref,
             dup_ref, dh1_ref, dfw_ref, dfb_ref, dwn_ref,
             dgg_ring, dup_ring, carry_ref, acc_w, acc_b, acc_n):
        s = pl.program_id(0)
        slot = lax.rem(s, 2)
        other = 1 - slot

        @pl.when(s == 0)
        def _():
            carry_ref[...] = jnp.zeros_like(carry_ref)
            acc_w[...] = jnp.zeros_like(acc_w)
            acc_b[...] = jnp.zeros_like(acc_b)
            acc_n[...] = jnp.zeros_like(acc_n)
            dup_ring[1] = jnp.zeros((TR, 2 * DFF), BF16)
            dgg_ring[0] = _dot_nt(dh2p_ref[...], wdn_ref[...])

        def norm_bwd(du2, valid):
            x = h_ref[...]
            dh2p = dh2p_ref[...]
            r2 = lax.rsqrt(jnp.mean(x * x, axis=-1, keepdims=True) + EPS)
            n2 = x * r2
            dn = du2 * wn_ref[...]
            dh1_ref[...] = dh2p + r2 * (dn - n2 * jnp.mean(dn * n2, axis=-1, keepdims=True))
            acc_n[...] += jnp.where(valid, jnp.sum(du2 * n2, axis=0, keepdims=True), 0.0)

        @pl.when(s < nt)
        def _():
            tile = nt - 1 - s
            w0, w1, w2 = cw_ref[0:1, :], cw_ref[1:2, :], cw_ref[2:3, :]
            a_before_tile = jnp.where(tile > 0, pa_ref[...].astype(F32)[8:16, :], 0.0)
            dh2n = dh2n_ref[...].astype(BF16)
            du2 = None
            for idx in range(ncht):
                c = ncht - 1 - idx
                r0 = c * CH
                rows = slice(r0, r0 + CH)
                a_pre = up_ref[rows, 0:DFF].astype(F32)
                val = up_ref[rows, DFF:2 * DFF].astype(F32)
                prev8 = up_ref[r0 - 16:r0, 0:DFF].astype(F32)[8:16, :] if c > 0 else a_before_tile
                a_m1 = _shift_down(a_pre, prev8, 1)
                a_m2 = _shift_down(a_pre, prev8, 2)
                a = w0 * a_m2 + w1 * a_m1 + w2 * a_pre + cb_ref[...]
                sig = _sigmoid(a)
                dgg_v = dgg_ring[slot, rows, :]
                da = dgg_v * val * (sig * (1.0 + a * (1.0 - sig)))
                dval = (dgg_v * (a * sig)).astype(BF16)
                next8 = carry_ref[...]
                da_pre = (w2 * da + w1 * _shift_up(da, next8, 1) + w0 * _shift_up(da, next8, 2)).astype(BF16)
                carry_ref[...] = da[0:8, :]
                dup_ref[rows, 0:DFF] = da_pre
                dup_ref[rows, DFF:2 * DFF] = dval
                dup_ring[slot, rows, 0:DFF] = da_pre
                dup_ring[slot, rows, DFF:2 * DFF] = dval
                acc_w[0:1, :] += jnp.sum(da * a_m2, axis=0, keepdims=True)
                acc_w[1:2, :] += jnp.sum(da * a_m1, axis=0, keepdims=True)
                acc_w[2:3, :] += jnp.sum(da * a_pre, axis=0, keepdims=True)
                acc_b[...] += jnp.sum(da, axis=0, keepdims=True)
                c0, c1 = parts[idx]
                dgg_ring[other, :, c0:c1] = _dot_nt(dh2n, wdn_ref[c0:c1, :])
                if idx % every == 0:
                    sl = idx // every
                    part = _dot_nt(dup_ring[other, :, sl * wsl:(sl + 1) * wsl], wup_ref[sl])
                    du2 = part if du2 is None else du2 + part
            norm_bwd(du2, s > 0)

        @pl.when(s == nt)
        def _():
            du2 = _dot_nt(dup_ring[other, :, 0:wsl], wup_ref[0])
            for sl in range(1, nsl):
                du2 = du2 + _dot_nt(dup_ring[other, :, sl * wsl:(sl + 1) * wsl], wup_ref[sl])
            norm_bwd(du2, True)
            dfw_ref[...] = acc_w[0:3, :]
            dfb_ref[...] = acc_b[...]
            dwn_ref[...] = acc_n[...]

    gate_tile = lambda s: jnp.maximum(nt - 1 - s, 0)
    next_tile = lambda s: jnp.maximum(nt - 2 - s, 0)
    prev_tile = lambda s: jnp.minimum(nt - s, nt - 1)
    vec = lambda w, r=1: pl.BlockSpec((r, w), lambda s: (0, 0))
    vm = pl.BlockSpec(memory_space=pltpu.VMEM)
    return pl.pallas_call(
        body, name="ffn_block_bwd", grid=(nt + 1,),
        in_specs=[pl.BlockSpec((TR, D), lambda s: (next_tile(s), 0)),
                  pl.BlockSpec((TR, D), lambda s: (prev_tile(s), 0)),
                  pl.BlockSpec((TR, 2 * DFF), lambda s: (gate_tile(s), 0)),
                  pl.BlockSpec((16, DFF), lambda s: (jnp.maximum(gate_tile(s) * tb16 - 1, 0), 0)),
                  pl.BlockSpec((TR, D), lambda s: (prev_tile(s), 0)),
                  vec(D), vm, vec(DFF, 3), vec(DFF), vm],
        out_specs=[pl.BlockSpec((TR, 2 * DFF), lambda s: (gate_tile(s), 0)),
                   pl.BlockSpec((TR, D), lambda s: (prev_tile(s), 0)),
                   vec(DFF, 3), vec(DFF), vec(D)],
        out_shape=[jax.ShapeDtypeStruct((T, 2 * DFF), BF16), jax.ShapeDtypeStruct((T, D), F32),
                   jax.ShapeDtypeStruct((3, DFF), F32), jax.ShapeDtypeStruct((1, DFF), F32),
                   jax.ShapeDtypeStruct((1, D), F32)],
        scratch_shapes=[pltpu.VMEM((2, TR, DFF), F32), pltpu.VMEM((2, TR, 2 * DFF), BF16),
                        pltpu.VMEM((8, DFF), F32), pltpu.VMEM((8, DFF), F32), pltpu.VMEM((1, DFF), F32),
                        pltpu.VMEM((1, D), F32)],
        compiler_params=_cparams(("arbitrary",)),
    )(dh2, dh2, upb, upb, h1, wffn, w_up_g, fcw, fcb, w_down)


def _place():
    x, y, c = lax.axis_index("x"), lax.axis_index("y"), lax.axis_index("c")
    return x, y, c


_CHIP_FLIPS = ((1, 0), (0, 1), (1, 1))


def _flip(v, bit):
    return 1 - v if bit else v


def _into_slot(w, j_idx, *, rb, dtype, name, paired=False):
    r, cdim = w.shape

    def body(j_ref, w_ref, out_ref):
        del j_ref
        out_ref[...] = w_ref[...].astype(dtype)

    if paired:
        out_shape = jax.ShapeDtypeStruct((NSHARD // 2, r, 2 * cdim), dtype)
        out_spec = pl.BlockSpec((None, rb, cdim), lambda i, j_ref: (j_ref[0] // 2, i, j_ref[0] % 2))
    else:
        out_shape = jax.ShapeDtypeStruct((NSHARD, r, cdim), dtype)
        out_spec = pl.BlockSpec((None, rb, cdim), lambda i, j_ref: (j_ref[0], i, 0))
    grid_spec = pltpu.PrefetchScalarGridSpec(
        num_scalar_prefetch=1, grid=(r // rb,),
        in_specs=[pl.BlockSpec((rb, cdim), lambda i, j_ref: (i, 0))], out_specs=out_spec)
    return pl.pallas_call(
        body, name=name, grid_spec=grid_spec, out_shape=out_shape, compiler_params=_cparams(("parallel",)),
    )(j_idx, w)


class _Gather:
    def __init__(self, outs, send_sems, recv_sems, whole, paired=None):
        self.outs, self.send_sems, self.recv_sems, self.whole = outs, send_sems, recv_sems, whole
        self.paired = paired if paired is not None else (False,) * len(outs)
        self.x, self.y, self.c = _place()
        self.j = 2 * self.x + self.y
        self.sibling = (self.x, self.y, 1 - self.c)

    def _rows(self, w, core):
        r = self.outs[w].shape[1]
        return pl.ds(0, r) if self.whole[w] else pl.ds(core * (r // 2), r // 2)

    def _copy(self, w, slot, core, sem, to):
        if self.paired[w]:
            cw = self.outs[w].shape[2] // 2
            piece = self.outs[w].at[slot // 2, self._rows(w, core), pl.ds((slot % 2) * cw, cw)]
        else:
            piece = self.outs[w].at[slot, self._rows(w, core), :]
        return pltpu.make_async_remote_copy(
            src_ref=piece, dst_ref=piece, send_sem=self.send_sems.at[w, sem], recv_sem=self.recv_sems.at[w, sem],
            device_id=to, device_id_type=MESH)

    def _chips(self):
        for kk, (fx, fy) in enumerate(_CHIP_FLIPS):
            px, py = _flip(self.x, fx), _flip(self.y, fy)
            yield kk, 2 * px + py, (px, py, self.c)

    def send(self):
        for w in range(len(self.outs)):
            for kk, _, to in self._chips():
                self._copy(w, self.j, self.c, kk, to).start()

    def relay(self):
        for w in range(len(self.outs)):
            for kk, jk, _ in self._chips():
                self._copy(w, jk, self.c, kk, self.sibling).wait_recv()
                if not self.whole[w]:
                    self._copy(w, jk, self.c, 3 + kk, self.sibling).start()

    def finish(self):
        for w in range(len(self.outs)):
            for kk, jk, to in self._chips():
                self._copy(w, self.j, self.c, kk, to).wait_send()
                if not self.whole[w]:
                    self._copy(w, jk, 1 - self.c, 3 + kk, self.sibling).wait_recv()
                    self._copy(w, jk, self.c, 3 + kk, self.sibling).wait_send()


def _allgather_weights(slotted, whole):
    n = len(slotted)

    def body(*refs):
        g = _Gather(refs[n:2 * n], refs[2 * n], refs[2 * n + 1], whole)
        g.send()
        g.relay()
        g.finish()

    any_spec = pl.BlockSpec(memory_space=pl.ANY)
    return pl.pallas_call(
        body, name="allgather_weights",
        in_specs=[any_spec] * n, out_specs=[any_spec] * n,
        out_shape=[jax.ShapeDtypeStruct(a.shape, a.dtype) for a in slotted],
        input_output_aliases={i: i for i in range(n)},
        scratch_shapes=[pltpu.SemaphoreType.DMA((n, 6)), pltpu.SemaphoreType.DMA((n, 6))],
    )(*slotted)


class _ChipExchange:
    def __init__(self, ins, outs, send_sems, recv_sems):
        self.ins, self.outs, self.send_sems, self.recv_sems = ins, outs, send_sems, recv_sems
        self.x, self.y, self.c = _place()

    def _copies(self):
        for w in range(len(self.ins)):
            for kk, (fx, fy) in enumerate(_CHIP_FLIPS):
                px, py = _flip(self.x, fx), _flip(self.y, fy)
                yield pltpu.make_async_remote_copy(
                    src_ref=self.ins[w].at[2 * px + py], dst_ref=self.outs[w].at[kk],
                    send_sem=self.send_sems.at[w, kk], recv_sem=self.recv_sems.at[w, kk],
                    device_id=(px, py, self.c), device_id_type=MESH)

    def send(self):
        for cp in self._copies():
            cp.start()

    def finish(self):
        for cp in self._copies():
            cp.wait()


class _PairExchange:
    def __init__(self, ins, outs, send_sems, recv_sems):
        self.ins, self.outs, self.send_sems, self.recv_sems = ins, outs, send_sems, recv_sems
        self.x, self.y, self.c = _place()

    def _copies(self):
        for w in range(len(self.ins)):
            half = self.ins[w].shape[1] // 2
            yield pltpu.make_async_remote_copy(
                src_ref=self.ins[w].at[:, pl.ds((1 - self.c) * half, half), :], dst_ref=self.outs[w],
                send_sem=self.send_sems.at[w], recv_sem=self.recv_sems.at[w],
                device_id=(self.x, self.y, 1 - self.c), device_id_type=MESH)

    def send(self):
        for cp in self._copies():
            cp.start()

    def finish(self):
        for cp in self._copies():
            cp.wait()


def _pair_shapes(grads):
    return [jax.ShapeDtypeStruct((g.shape[0], g.shape[1] // 2, g.shape[2]), g.dtype) for g in grads]


def _pair_exchange(grads, name):
    nw = len(grads)

    def body(*refs):
        ex = _PairExchange(refs[:nw], refs[nw:2 * nw], refs[2 * nw], refs[2 * nw + 1])
        ex.send()
        ex.finish()

    any_spec = pl.BlockSpec(memory_space=pl.ANY)
    return pl.pallas_call(
        body, name=name,
        in_specs=[any_spec] * nw, out_specs=[any_spec] * nw, out_shape=_pair_shapes(grads),
        scratch_shapes=[pltpu.SemaphoreType.DMA((nw,)), pltpu.SemaphoreType.DMA((nw,))],
    )(*grads)


def _pair_add(g, other, c_idx, *, rb, name):
    S, r, cdim = g.shape
    half = r // 2
    nb = half // rb

    def body(c_ref, g_ref, o_ref, out_ref):
        del c_ref
        out_ref[...] = (g_ref[...].astype(F32) + o_ref[...].astype(F32)).astype(BF16)

    grid_spec = pltpu.PrefetchScalarGridSpec(
        num_scalar_prefetch=1, grid=(S, nb),
        in_specs=[pl.BlockSpec((None, rb, cdim), lambda s, i, c_ref: (s, c_ref[0] * nb + i, 0)),
                  pl.BlockSpec((None, rb, cdim), lambda s, i, c_ref: (s, i, 0))],
        out_specs=pl.BlockSpec((None, rb, cdim), lambda s, i, c_ref: (s, i, 0)))
    return pl.pallas_call(
        body, name=name, grid_spec=grid_spec, out_shape=jax.ShapeDtypeStruct((S, half, cdim), BF16),
        compiler_params=_cparams(("parallel", "parallel")),
    )(c_idx, g, other)


def _chip_sum(psum, parts, cj_idx, *, rb, name):
    S, half, cdim = psum.shape
    nb = half // rb

    def body(cj_ref, own_ref, p_ref, out_ref):
        del cj_ref
        f = lambda v: v.astype(F32)
        out_ref[...] = ((f(own_ref[...]) + f(p_ref[0])) + f(p_ref[1])) + f(p_ref[2])

    grid_spec = pltpu.PrefetchScalarGridSpec(
        num_scalar_prefetch=1, grid=(nb,),
        in_specs=[pl.BlockSpec((None, rb, cdim), lambda i, cj: (cj[1], i, 0)),
                  pl.BlockSpec((3, rb, cdim), lambda i, cj: (0, i, 0))],
        out_specs=pl.BlockSpec((rb, cdim), lambda i, cj: (cj[0] * nb + i, 0)))
    return pl.pallas_call(
        body, name=name, grid_spec=grid_spec, out_shape=jax.ShapeDtypeStruct((2 * half, cdim), F32),
        compiler_params=_cparams(("parallel",)),
    )(cj_idx, psum, parts)


SLAB_W = 1024


def _final_exchange(pieces, grads):
    n, nw = len(pieces), len(grads)
    segs, at = [], 0
    for idx, p in enumerate(pieces):
        r, wd = p.shape
        for c0 in range(0, wd, SLAB_W):
            if r > 1:
                at = -(-at // 8) * 8
            segs.append((idx, c0, min(SLAB_W, wd - c0), at))
            at += r
    rows = -(-at // 8) * 8
    flips = [(fx, fy, fc) for fx in (0, 1) for fy in (0, 1) for fc in (0, 1)][1:]

    def body(*refs):
        ins = refs[:n]
        outs = refs[n + nw:2 * n + nw]
        g_refs = refs[2 * n + nw:2 * n + 2 * nw]
        mine_ref, slots_ref, send_sems, recv_sems, gsend_sems, grecv_sems = refs[2 * n + 2 * nw:]
        x, y, c = _place()
        me = 4 * x + 2 * y + c
        sibling = (x, y, 1 - c)

        def swap(w, core):
            half = g_refs[w].shape[0] // 2
            rows_ = g_refs[w].at[pl.ds(core * half, half), :]
            return pltpu.make_async_remote_copy(
                src_ref=rows_, dst_ref=rows_, send_sem=gsend_sems.at[w], recv_sem=grecv_sems.at[w],
                device_id=sibling, device_id_type=MESH)

        for w in range(nw):
            swap(w, c).start()
        mine_ref[...] = jnp.zeros_like(mine_ref)
        for idx, c0, wd, st in segs:
            r = ins[idx].shape[0]
            mine_ref[st:st + r, 0:wd] = ins[idx][:, c0:c0 + wd]
        slots_ref[me] = mine_ref[...]
        cps = []
        for kk, (fx, fy, fc) in enumerate(flips):
            cp = pltpu.make_async_remote_copy(
                src_ref=mine_ref, dst_ref=slots_ref.at[me], send_sem=send_sems.at[kk], recv_sem=recv_sems.at[kk],
                device_id=(_flip(x, fx), _flip(y, fy), _flip(c, fc)), device_id_type=MESH)
            cp.start()
            cps.append(cp)
        for cp in cps:
            cp.wait()
        tot = slots_ref[0]
        for d in range(1, 8):
            tot = tot + slots_ref[d]
        mine_ref[...] = tot
        for idx, c0, wd, st in segs:
            r = ins[idx].shape[0]
            val = mine_ref[st:st + r, 0:wd]
            if idx == n - 1:
                outs[idx][...] = jnp.sum(val, keepdims=True)
            else:
                outs[idx][:, c0:c0 + wd] = val
        for w in range(nw):
            swap(w, 1 - c).wait_recv()
            swap(w, c).wait_send()

    vm = pl.BlockSpec(memory_space=pltpu.VMEM)
    hbm = pl.BlockSpec(memory_space=pl.ANY)
    out_shape = ([jax.ShapeDtypeStruct(p.shape, F32) for p in pieces[:-1]] + [jax.ShapeDtypeStruct((1, 1), F32)]
                 + [jax.ShapeDtypeStruct(g.shape, g.dtype) for g in grads])
    res = pl.pallas_call(
        body, name="final_exchange", in_specs=[vm] * n + [hbm] * nw, out_specs=[vm] * n + [hbm] * nw,
        out_shape=out_shape, input_output_aliases={n + w: n + w for w in range(nw)},
        scratch_shapes=[pltpu.VMEM((rows, SLAB_W), F32), pltpu.VMEM((8, rows, SLAB_W), F32),
                        pltpu.SemaphoreType.DMA((7,)), pltpu.SemaphoreType.DMA((7,)),
                        pltpu.SemaphoreType.DMA((nw,)), pltpu.SemaphoreType.DMA((nw,))],
    )(*pieces, *grads)
    return res[:n], res[n:]


def _adamw(w, g, m, v, *, rb, name):
    r, cdim = w.shape

    def body(w_ref, g_ref, m_ref, v_ref, go_ref, d_ref, nm_ref, nv_ref):
        go_ref[...] = g_ref[...]
        d_ref[...], nm_ref[...], nv_ref[...] = _adamw_update(w_ref[...], g_ref[...], m_ref[...], v_ref[...])

    spec = pl.BlockSpec((rb, cdim), lambda i: (i, 0))
    shp = jax.ShapeDtypeStruct((r, cdim), F32)
    return pl.pallas_call(
        body, name=name, grid=(r // rb,), in_specs=[spec] * 4, out_specs=[spec] * 4, out_shape=[shp] * 4,
        compiler_params=_cparams(("parallel",)),
    )(w, g, m, v)


def _adamw_update(w, g, m, v):
    nm = ADAM_B1 * m + (1.0 - ADAM_B1) * g
    nv = ADAM_B2 * v + (1.0 - ADAM_B2) * (g * g)
    m_hat = nm / (1.0 - ADAM_B1 ** ADAM_STEP)
    v_hat = nv / (1.0 - ADAM_B2 ** ADAM_STEP)
    return -ADAM_LR * (m_hat / (jnp.sqrt(v_hat) + ADAM_EPS) + ADAM_WD * w), nm, nv


def _adamw_small(params):
    n = len(params)

    def body(*refs):
        ins, outs = refs[:4 * n], refs[4 * n:]
        for p in range(n):
            w_ref, g_ref, m_ref, v_ref = ins[4 * p:4 * p + 4]
            d, nm, nv = _adamw_update(w_ref[...], g_ref[...], m_ref[...], v_ref[...])
            outs[3 * p][...] = d
            outs[3 * p + 1][...] = nm
            outs[3 * p + 2][...] = nv

    vm = pl.BlockSpec(memory_space=pltpu.VMEM)
    flat = [a for p in params for a in p]
    out_shape = [jax.ShapeDtypeStruct(p[0].shape, F32) for p in params for _ in range(3)]
    res = pl.pallas_call(
        body, name="adamw_small", in_specs=[vm] * (4 * n), out_specs=[vm] * (3 * n), out_shape=out_shape,
    )(*flat)
    return [tuple(res[3 * p:3 * p + 3]) for p in range(n)]


_PAIR_ADD_ROWS = {"w_in": 256, "w_out": 128, "w_up": 256, "w_down": 176}


def _pair_sums(grads, others, names, c_idx):
    return [_pair_add(g, o_, c_idx, rb=_PAIR_ADD_ROWS[n], name=f"pair_add_{n}") for g, o_, n in zip(grads, others, names)]


def _local_step(x, tgt, meta_full, lb_param, attn_norm_w, w_in_g, hgrn_norm_w, conv_w_full, w_out_full,
                ffn_norm_w, late_slotted, fcw_full, ffn_conv_b, final_norm_w, c_idx, cj_idx):
    seq = x.shape[0]
    T = TR + seq
    head_tile = jnp.concatenate([jnp.zeros((PAD, D), F32), meta_full], axis=0)
    whn_t = jnp.tile(hgrn_norm_w, (1, NH))

    ut, projb, o, sst, mt, h1, w_up_g, w_down_g = _mix_block_fwd(
        x, head_tile, lb_param, attn_norm_w, whn_t, conv_w_full, w_in_g, w_out_full, late_slotted, (True, False))
    w_down_full = w_down_g.reshape(DFF, D)
    u2t, upb, ggt, dh2, loss_vec, dwfin = _ffn_block_fwd(
        h1, tgt, ffn_norm_w, w_up_g, fcw_full, ffn_conv_b, w_down_full, final_norm_w.reshape(1, D))

    dup, dh1, dfw, dfb, dwffn = _ffn_block_bwd(
        dh2, upb, h1, ffn_norm_w, w_up_g, fcw_full, ffn_conv_b, w_down_full)
    kb = 1408 if T % 1408 == 0 else TR
    g_up = _weight_grad(u2t, dup, bn=DFF, bk=kb, name="dw_up_mm", shard_cols=2 * DFF // NSHARD)
    g_down, sib_up = _weight_grad(ggt, dh2, bn=D // 2, bk=2816 if T % 2816 == 0 else kb, name="dw_down_mm", ride=[g_up])
    g_down = g_down.reshape(NSHARD, DFF // NSHARD, D)
    sib_down = _pair_exchange([g_down], name="grad_pair_exchange_down")
    ps_ffn = _pair_sums([g_up, g_down], sib_up + list(sib_down), ("w_up", "w_down"), c_idx)

    dproj, dlb, dwhn, dcw, *parts_ffn = _mix_block_bwd(
        dh1, projb, o, sst, lb_param, whn_t, conv_w_full, w_out_full, ps_ffn)
    kb_deep = 2816 if T % 2816 == 0 else kb
    g_in = _weight_grad(ut, dproj, bn=9 * D // NSHARD, bk=kb_deep, name="dw_in_mm", shard_cols=9 * D // NSHARD)
    g_out, sib_in = _weight_grad(mt, dh1, bn=D, bk=kb_deep, name="dw_out_mm", ride=[g_in])
    g_out = g_out.reshape(NSHARD, D // NSHARD, D)
    sib_out = _pair_exchange([g_out], name="grad_pair_exchange_out")
    ps_mix = _pair_sums([g_in, g_out], sib_in + list(sib_out), ("w_in", "w_out"), c_idx)

    grad_x, dmeta, dwattn, *parts_mix = _input_grad_block(dproj, x, head_tile, dh1, attn_norm_w, w_in_g, ps_mix)

    halves = [_chip_sum(ps, p, cj_idx, rb=_PAIR_ADD_ROWS[n], name=f"chip_sum_{n}")
              for ps, p, n in zip(ps_mix + ps_ffn, parts_mix + parts_ffn, ("w_in", "w_out", "w_up", "w_down"))]
    small = dict(dlb=dlb, dwattn=dwattn, dwhn=dwhn, dwffn=dwffn, dfb=dfb, dwfin=dwfin,
                 dcw=dcw, dfw=dfw, dmeta=dmeta, loss=loss_vec)
    return grad_x, small, halves


_SMALL_ORDER = ("dmeta", "dcw", "dfw", "dlb", "dwattn", "dwhn", "dwffn", "dfb", "dwfin", "loss")


def kernel(x, meta_tokens, lb_param, attn_norm_w, w_in, hgrn_norm_w, conv_w, w_out, ffn_norm_w, w_up, ffn_conv_w, ffn_conv_b, w_down, final_norm_w, loss_target, m_meta_tokens, m_lb_param, m_attn_norm_w, m_w_in, m_hgrn_norm_w, m_conv_w, m_w_out, m_ffn_norm_w, m_w_up, m_ffn_conv_w, m_ffn_conv_b, m_w_down, m_final_norm_w, v_meta_tokens, v_lb_param, v_attn_norm_w, v_w_in, v_hgrn_norm_w, v_conv_w, v_w_out, v_ffn_norm_w, v_w_up, v_ffn_conv_w, v_ffn_conv_b, v_w_down, v_final_norm_w):
    xi, yi, ci = _place()
    j = 2 * xi + yi
    c_idx = jnp.reshape(ci, (1,)).astype(jnp.int32)

    j_idx = jnp.reshape(j, (1,)).astype(jnp.int32)
    ds_, fs_ = D // NSHARD, DFF // NSHARD
    widen = lambda a: jnp.pad(a, ((0, 0), (0, 768 - a.shape[1])))
    rows_small = jnp.concatenate([widen(meta_tokens), widen(conv_w[0]), widen(ffn_conv_w[0]),
                                  jnp.zeros((2, 768), F32)], axis=0)
    s_in, s_out, s_up, s_down = [
        _into_slot(w[0], j_idx, rb=rb, dtype=BF16, name=f"slot_{n}", paired=pr)
        for w, rb, n, pr in ((w_in, 256, "w_in", False), (w_out, 128, "w_out", False), (w_up, 256, "w_up", True),
                             (w_down, 176, "w_down", False))]
    s_small = _into_slot(rows_small, j_idx, rb=rows_small.shape[0], dtype=F32, name="slot_small")
    w_in_g, w_out_g, small_g = _allgather_weights([s_in, s_out, s_small], (False, False, True))
    unshard = lambda a: jnp.transpose(a, (1, 0, 2)).reshape(a.shape[1], -1)
    meta_full = unshard(small_g[:, 0:NMETA, 0:ds_])
    conv_w_full = unshard(small_g[:, NMETA:NMETA + 3, 0:ds_])
    fcw_full = unshard(small_g[:, NMETA + 3:NMETA + 6, 0:fs_])

    cj_idx = jnp.stack([ci, j]).astype(jnp.int32)
    grad_x, small, halves = _local_step(
        x[0], loss_target[0], meta_full, lb_param, attn_norm_w, w_in_g, hgrn_norm_w, conv_w_full,
        w_out_g.reshape(D, D), ffn_norm_w, [s_up, s_down], fcw_full, ffn_conv_b, final_norm_w, c_idx, cj_idx)

    names = _SMALL_ORDER
    small_sums, g_big = _final_exchange([small[n] for n in names], halves)
    vals = dict(zip(names, small_sums))
    loss = vals["loss"].reshape(())
    g_small = {
        "meta_tokens": lax.dynamic_slice_in_dim(vals["dmeta"], j * (D // NSHARD), D // NSHARD, axis=1),
        "lb_param": jnp.concatenate([vals["dlb"], -vals["dlb"]], axis=0),
        "attn_norm_w": vals["dwattn"],
        "hgrn_norm_w": vals["dwhn"],
        "conv_w": lax.dynamic_slice_in_dim(vals["dcw"], j * (D // NSHARD), D // NSHARD, axis=1)[None],
        "ffn_norm_w": vals["dwffn"],
        "ffn_conv_w": lax.dynamic_slice_in_dim(vals["dfw"], j * (DFF // NSHARD), DFF // NSHARD, axis=1)[None],
        "ffn_conv_b": vals["dfb"],
        "final_norm_w": vals["dwfin"].reshape(D),
    }


    weights = {"meta_tokens": meta_tokens, "lb_param": lb_param, "attn_norm_w": attn_norm_w, "w_in": w_in,
               "hgrn_norm_w": hgrn_norm_w, "conv_w": conv_w, "w_out": w_out, "ffn_norm_w": ffn_norm_w,
               "w_up": w_up, "ffn_conv_w": ffn_conv_w, "ffn_conv_b": ffn_conv_b, "w_down": w_down,
               "final_norm_w": final_norm_w}
    ms = {"meta_tokens": m_meta_tokens, "lb_param": m_lb_param, "attn_norm_w": m_attn_norm_w, "w_in": m_w_in,
          "hgrn_norm_w": m_hgrn_norm_w, "conv_w": m_conv_w, "w_out": m_w_out, "ffn_norm_w": m_ffn_norm_w,
          "w_up": m_w_up, "ffn_conv_w": m_ffn_conv_w, "ffn_conv_b": m_ffn_conv_b, "w_down": m_w_down,
          "final_norm_w": m_final_norm_w}
    vs = {"meta_tokens": v_meta_tokens, "lb_param": v_lb_param, "attn_norm_w": v_attn_norm_w, "w_in": v_w_in,
          "hgrn_norm_w": v_hgrn_norm_w, "conv_w": v_conv_w, "w_out": v_w_out, "ffn_norm_w": v_ffn_norm_w,
          "w_up": v_w_up, "ffn_conv_w": v_ffn_conv_w, "ffn_conv_b": v_ffn_conv_b, "w_down": v_w_down,
          "final_norm_w": v_final_norm_w}
    order = list(weights)
    grads, deltas, new_m, new_v = {}, {}, {}, {}

    for name, g, rb in zip(("w_in", "w_out", "w_up", "w_down"), g_big, (256, 128, 256, 176)):
        shp = weights[name].shape
        w2, m2, v2 = (a.reshape(shp[1], shp[2]) for a in (weights[name], ms[name], vs[name]))
        g_, d_, nm_, nv_ = _adamw(w2, g, m2, v2, rb=rb, name=f"adamw_{name}")
        grads[name], deltas[name], new_m[name], new_v[name] = (a.reshape(shp) for a in (g_, d_, nm_, nv_))

    small_names = [n for n in order if n not in grads]
    as2d = lambda a: a.reshape(-1, a.shape[-1])
    res = _adamw_small([tuple(as2d(a) for a in (weights[n], g_small[n], ms[n], vs[n])) for n in small_names])
    for n, (d_, nm_, nv_) in zip(small_names, res):
        shp = weights[n].shape
        grads[n], deltas[n], new_m[n], new_v[n] = (a.reshape(shp) for a in (g_small[n], d_, nm_, nv_))

    return (loss, grad_x[None], *[grads[n] for n in order], *[deltas[n] for n in order],
            *[new_m[n] for n in order], *[new_v[n] for n in order])
```

```python
import jax
import jax.numpy as jnp
from jax import lax
from jax.experimental import pallas as pl
from jax.experimental.pallas import tpu as pltpu

F32 = jnp.float32
BF16 = jnp.bfloat16
MESH = pl.DeviceIdType.MESH

D = 1024
NH = 8
HD = 128
DFF = 2816
NMETA = 16
EPS = 1e-6
TR = 256
PAD = TR - NMETA
CH = 64
NSHARD = 4
VMEM_LIMIT = 62 * 1024 * 1024

ADAM_LR = 0.001
ADAM_B1 = 0.9
ADAM_B2 = 0.999
ADAM_EPS = 1e-08
ADAM_WD = 0.01
ADAM_STEP = 10


def _cparams(semantics=None, **kw):
    return pltpu.CompilerParams(dimension_semantics=semantics, vmem_limit_bytes=VMEM_LIMIT, **kw)


def _sigmoid(x):
    return 0.5 * jnp.tanh(0.5 * x) + 0.5


def _weight_grad(at, b, *, bn, bk, name, shard_cols=None, ride=()):
    M, T = at.shape
    N = b.shape[1]
    assert b.shape[0] == T and T % bk == 0 and N % bn == 0, (name, M, N, T, bn, bk)
    nk = T // bk
    nj = N // bn
    nex = len(ride)
    if shard_cols is None:
        nsh = 1
        blk = (M, bn)
        out_shape = jax.ShapeDtypeStruct((M, N), BF16)
        out_spec = pl.BlockSpec(blk, lambda j, k: (0, j))
    else:
        assert bn % shard_cols == 0 and N % shard_cols == 0
        nsh = bn // shard_cols
        blk = (nsh, M, shard_cols)
        out_shape = jax.ShapeDtypeStruct((N // shard_cols, M, shard_cols), BF16)
        out_spec = pl.BlockSpec(blk, lambda j, k: (j, 0, 0))

    def body(*refs):
        a_ref, b_ref = refs[:2]
        o_ref = refs[2 + nex]
        acc_ref = refs[3 + 2 * nex]
        j, k = pl.program_id(0), pl.program_id(1)
        if nex:
            exchange = _PairExchange(refs[2:2 + nex], refs[3 + nex:3 + 2 * nex], refs[-2], refs[-1])

            @pl.when(jnp.logical_and(j == 0, k == 0))
            def _():
                exchange.send()

        @pl.when(k == 0)
        def _():
            acc_ref[...] = jnp.zeros_like(acc_ref)

        p = jnp.dot(a_ref[...].astype(BF16), b_ref[...].astype(BF16), preferred_element_type=F32)
        if shard_cols is None:
            acc_ref[...] += p
        else:
            for q in range(nsh):
                acc_ref[q] += p[:, q * shard_cols:(q + 1) * shard_cols]

        @pl.when(k == nk - 1)
        def _():
            o_ref[...] = acc_ref[...].astype(BF16)

        if nex:
            @pl.when(jnp.logical_and(j == nj - 1, k == nk - 1))
            def _():
                exchange.finish()

    hbm = pl.BlockSpec(memory_space=pl.ANY)
    scratch = [pltpu.VMEM(blk, F32)]
    if nex:
        scratch += [pltpu.SemaphoreType.DMA((nex,)), pltpu.SemaphoreType.DMA((nex,))]
    res = pl.pallas_call(
        body, name=name, grid=(nj, nk),
        in_specs=[pl.BlockSpec((M, bk), lambda j, k: (0, k)), pl.BlockSpec((bk, bn), lambda j, k: (k, j))] + [hbm] * nex,
        out_specs=[out_spec] + [hbm] * nex, out_shape=[out_shape] + _pair_shapes(ride), scratch_shapes=scratch,
        compiler_params=_cparams(("arbitrary", "arbitrary") if nex else ("parallel", "arbitrary")),
    )(at, b, *ride)
    return (res[0], list(res[1:])) if nex else res[0]


def _input_grad_block(dproj, x_seq, head_tile, dres, w, w_in_g, psums):
    T = TR + x_seq.shape[0]
    nt = T // TR
    nsl = w_in_g.shape[0]
    wsl = w_in_g.shape[2]
    wcol = wsl // 2
    nex = len(psums)

    def body(*refs):
        dp_ref, x_ref, head_ref, dres_ref, w_ref, win_ref = refs[:6]
        gx_ref, dmeta_ref, dw_ref = refs[6 + nex:9 + nex]
        exchange = _ChipExchange(refs[6:6 + nex], refs[9 + nex:9 + 2 * nex], refs[-2], refs[-1])
        i = pl.program_id(0)

        @pl.when(i == 0)
        def _():
            exchange.send()

        @pl.when(i == nt - 1)
        def _():
            exchange.finish()

        x = jnp.where(i == 0, head_ref[...], x_ref[...])
        r = lax.rsqrt(jnp.mean(x * x, axis=-1, keepdims=True) + EPS)
        n = x * r
        du_v = None
        for s in range(nsl):
            part = _dot_nt(dp_ref[:, s * wsl:(s + 1) * wsl], win_ref[s])
            du_v = part if du_v is None else du_v + part
        dn = du_v * w_ref[...]
        dh = dres_ref[...] + r * (dn - n * jnp.mean(dn * n, axis=-1, keepdims=True))
        gx_ref[...] = dh
        part = jnp.sum(du_v * n, axis=0, keepdims=True)

        @pl.when(i == 0)
        def _():
            dmeta_ref[...] = dh[PAD:TR, :]
            dw_ref[...] = part

        @pl.when(i > 0)
        def _():
            dw_ref[...] += part

    row = pl.BlockSpec((TR, D), lambda i: (i, 0))
    vec = pl.BlockSpec((1, D), lambda i: (0, 0))
    hbm = pl.BlockSpec(memory_space=pl.ANY)
    return pl.pallas_call(
        body, name="input_grad_block", grid=(nt,),
        in_specs=[pl.BlockSpec((TR, 9 * D), lambda i: (i, 0)),
                  pl.BlockSpec((TR, D), lambda i: (jnp.maximum(i - 1, 0), 0)), pl.BlockSpec((TR, D), lambda i: (0, 0)),
                  row, vec, pl.BlockSpec(memory_space=pltpu.VMEM)] + [hbm] * nex,
        out_specs=[pl.BlockSpec((TR, D), lambda i: (jnp.maximum(i - 1, 0), 0)),
                   pl.BlockSpec((NMETA, D), lambda i: (0, 0)), vec] + [hbm] * nex,
        out_shape=[jax.ShapeDtypeStruct((T - TR, D), F32), jax.ShapeDtypeStruct((NMETA, D), F32),
                   jax.ShapeDtypeStruct((1, D), F32)]
        + [jax.ShapeDtypeStruct((3,) + p.shape[1:], p.dtype) for p in psums],
        scratch_shapes=[pltpu.SemaphoreType.DMA((nex, 3)), pltpu.SemaphoreType.DMA((nex, 3))],
        compiler_params=_cparams(("arbitrary",)),
    )(dproj, x_seq, head_tile, dres, w, w_in_g, *psums)


def _tri_matmul(tri_bf16, x):
    hi = x.astype(BF16)
    lo = (x - hi.astype(F32)).astype(BF16)
    return jnp.dot(tri_bf16, lo, preferred_element_type=F32) + jnp.dot(tri_bf16, hi, preferred_element_type=F32)


def _shift_down(x, prev8, n):
    rows = x.shape[0]
    return pltpu.roll(jnp.concatenate([prev8, x], axis=0), n, 0)[8:8 + rows, :]


def _shift_up(x, next8, n):
    rows = x.shape[0]
    return pltpu.roll(jnp.concatenate([x, next8], axis=0), rows + 8 - n, 0)[0:rows, :]


def _gates(f_raw, lb):
    sg = _sigmoid(f_raw)
    f = lb + (1.0 - lb) * sg
    return sg, f, jnp.log(f), 1.0 - f


def _lower_bound(lbp_ref):
    return _sigmoid(lbp_ref[0:1, :] - lbp_ref[1:2, :])


def _tri_masks():
    r = lax.broadcasted_iota(jnp.int32, (CH, CH), 0)
    c = lax.broadcasted_iota(jnp.int32, (CH, CH), 1)
    return r >= c, r <= c


def _ones_where(mask):
    return jnp.where(mask, 1.0, 0.0).astype(BF16)


def _dot(a, b):
    return jnp.dot(a.astype(BF16), b.astype(BF16), preferred_element_type=F32)


def _dot_nt(a, b):
    return lax.dot_general(a.astype(BF16), b.astype(BF16), (((1,), (1,)), ((), ())), preferred_element_type=F32)


def _dot_tn(a, b):
    return lax.dot_general(a.astype(BF16), b.astype(BF16), (((0,), (0,)), ((), ())), preferred_element_type=F32)


def _mix_block_fwd(x_seq, head_tile, lb_param, wattn, whn, conv_w, w_in_g, w_out, late, late_paired):
    T = TR + x_seq.shape[0]
    nt = T // TR
    ncht = TR // CH
    nsl = w_in_g.shape[0]
    wsl = w_in_g.shape[2]
    wcol = wsl // 2

    nlate = len(late)

    def body(*refs):
        x_ref, head_ref, lbp_ref, wa_ref, whn_ref, cw_ref, win_ref, wout_ref = refs[:8]
        ut_ref, projb_ref, o_ref, sst_ref, mt_ref, h1_ref = refs[8 + nlate:14 + nlate]
        late_refs = refs[14 + nlate:14 + 2 * nlate]
        proj_ref, m_ref, st_ref, cxc_ref, send_sems, recv_sems = refs[14 + 2 * nlate:]
        i = pl.program_id(0)
        gather = _Gather(late_refs, send_sems, recv_sems, (False,) * nlate, late_paired)

        @pl.when(i == 0)
        def _():
            st_ref[...] = jnp.zeros_like(st_ref)
            cxc_ref[...] = jnp.zeros_like(cxc_ref)
            gather.send()

        @pl.when(i == nt // 2)
        def _():
            gather.relay()

        x = jnp.where(i == 0, head_ref[...], x_ref[...])
        r1 = lax.rsqrt(jnp.mean(x * x, axis=-1, keepdims=True) + EPS)
        u_f = x * r1 * wa_ref[...]
        u = u_f.astype(BF16)
        ut_ref[...] = u_f.T.astype(BF16)
        for s in range(nsl):
            cols = slice(s * wsl, (s + 1) * wsl)
            p = jnp.dot(u, win_ref[s], preferred_element_type=F32)
            proj_ref[:, cols] = p
            projb_ref[:, cols] = p.astype(BF16)

        lb = _lower_bound(lbp_ref)
        lower, _ = _tri_masks()
        ltri = _ones_where(lower)
        whn_v = whn_ref[...]
        w0, w1, w2 = cw_ref[0:1, :], cw_ref[1:2, :], cw_ref[2:3, :]

        def gates_and_conv(c):
            rows = slice(c * CH, (c + 1) * CH)
            q_raw = proj_ref[rows, 0:D]
            f_raw = proj_ref[rows, D:2 * D]
            q = q_raw * _sigmoid(q_raw)
            _, _, g, k = _gates(f_raw, lb)
            gam = _tri_matmul(ltri, g)
            gam_l = gam[CH - 1:CH, :]
            g_out = proj_ref[rows, 3 * D:4 * D]
            cx = proj_ref[rows, 5 * D:6 * D] * proj_ref[rows, 6 * D:7 * D]
            prev8 = cxc_ref[...]
            cv = w0 * _shift_down(cx, prev8, 2) + w1 * _shift_down(cx, prev8, 1) + w2 * cx
            cxc_ref[...] = cx[CH - 8:CH, :]
            y_b = proj_ref[rows, 4 * D:5 * D] * cv
            return dict(
                e_l=jnp.exp(gam_l), qt=(q * jnp.exp(gam)).astype(BF16), kt=(k * jnp.exp(-gam)).astype(BF16),
                khat=(k * jnp.exp(gam_l - gam)).astype(BF16), vb=proj_ref[rows, 2 * D:3 * D].astype(BF16),
                out_gate=whn_v * (g_out * _sigmoid(g_out)), merge_a=_sigmoid(proj_ref[rows, 7 * D:8 * D]),
                merged_b=_sigmoid(proj_ref[rows, 8 * D:9 * D]) * y_b)

        def recurrence_and_merge(c, pre):
            rows = slice(c * CH, (c + 1) * CH)
            qt, kt, khat, vb, e_l = pre["qt"], pre["kt"], pre["khat"], pre["vb"], pre["e_l"]
            heads = [slice(h * HD, (h + 1) * HD) for h in range(NH)]
            sts = [st_ref[h] for h in range(NH)]
            scores = [_dot_nt(qt[:, cs], kt[:, cs]) for cs in heads]
            o_state = [_dot_nt(qt[:, cs], st) for cs, st in zip(heads, sts)]
            st_new = [_dot_tn(vb[:, cs], khat[:, cs]) for cs in heads]
            on_parts = []
            for h, cs in enumerate(heads):
                sst_ref[c, h] = sts[h]
                st_ref[h] = sts[h] * e_l[:, cs] + st_new[h]
                o_h = o_state[h] + _dot(jnp.where(lower, scores[h], 0.0), vb[:, cs])
                o_ref[rows, cs] = o_h
                ro = lax.rsqrt(jnp.mean(o_h * o_h, axis=-1, keepdims=True) + EPS)
                on_parts.append(o_h * ro)
            on = jnp.concatenate(on_parts, axis=1)
            m_ref[rows, :] = pre["merge_a"] * (on * pre["out_gate"]) + pre["merged_b"]

        pre = {0: gates_and_conv(0)}
        for c in range(ncht):
            if c + 1 < ncht:
                pre[c + 1] = gates_and_conv(c + 1)
            recurrence_and_merge(c, pre.pop(c))
        m_v = m_ref[...]
        h1_ref[...] = x + jnp.dot(m_v.astype(BF16), wout_ref[...], preferred_element_type=F32)
        mt_ref[...] = m_v.T.astype(BF16)

        @pl.when(i == nt - 1)
        def _():
            gather.finish()

    row = lambda w: pl.BlockSpec((TR, w), lambda i: (i, 0))
    col = lambda w: pl.BlockSpec((w, TR), lambda i: (0, i))
    vec = lambda r: pl.BlockSpec((r, D), lambda i: (0, 0))
    vm = pl.BlockSpec(memory_space=pltpu.VMEM)
    hbm = pl.BlockSpec(memory_space=pl.ANY)
    return pl.pallas_call(
        body, name="mix_block_fwd", grid=(nt,),
        in_specs=[pl.BlockSpec((TR, D), lambda i: (jnp.maximum(i - 1, 0), 0)), pl.BlockSpec((TR, D), lambda i: (0, 0)),
                  vec(2), vec(1), vec(1), vec(3), vm, vm] + [hbm] * nlate,
        out_specs=[col(D), row(9 * D), row(D), pl.BlockSpec((ncht, NH, HD, HD), lambda i: (i, 0, 0, 0)),
                   col(D), row(D)] + [hbm] * nlate,
        out_shape=[jax.ShapeDtypeStruct((D, T), BF16), jax.ShapeDtypeStruct((T, 9 * D), BF16),
                   jax.ShapeDtypeStruct((T, D), F32), jax.ShapeDtypeStruct((T // CH, NH, HD, HD), F32),
                   jax.ShapeDtypeStruct((D, T), BF16), jax.ShapeDtypeStruct((T, D), F32)]
        + [jax.ShapeDtypeStruct(a.shape, a.dtype) for a in late],
        input_output_aliases={8 + n: 6 + n for n in range(nlate)},
        scratch_shapes=[pltpu.VMEM((TR, 9 * D), F32), pltpu.VMEM((TR, D), F32), pltpu.VMEM((NH, HD, HD), F32),
                        pltpu.VMEM((8, D), F32), pltpu.SemaphoreType.DMA((nlate, 8)),
                        pltpu.SemaphoreType.DMA((nlate, 8))],
        compiler_params=_cparams(("arbitrary",)),
    )(x_seq, head_tile, lb_param, wattn, whn, conv_w, w_in_g, w_out, *late)


def _mix_block_bwd(dh1, projb, o, sst, lb_param, whn, conv_w, w_out, psums):
    T = projb.shape[0]
    nt = T // TR
    ncht = TR // CH
    tb16 = TR // 16
    nex = len(psums)

    def body(*refs):
        dh1_ref, proj_ref, pc_ref, px_ref, o_ref, sst_ref, lbp_ref, whn_ref, cw_ref, wout_ref = refs[:10]
        dproj_ref, dlb_ref, dwhn_ref, dcw_ref = refs[10 + nex:14 + nex]
        exchange = _ChipExchange(refs[10:10 + nex], refs[14 + nex:14 + 2 * nex], refs[-2], refs[-1])
        dm_ref, dst_ref, dcvc_ref, acc_lb, acc_hn, acc_cw = refs[14 + 2 * nex:-2]
        s = pl.program_id(0)
        tile = nt - 1 - s

        @pl.when(s == 0)
        def _():
            dst_ref[...] = jnp.zeros_like(dst_ref)
            dcvc_ref[...] = jnp.zeros_like(dcvc_ref)
            acc_lb[...] = jnp.zeros_like(acc_lb)
            acc_hn[...] = jnp.zeros_like(acc_hn)
            acc_cw[...] = jnp.zeros_like(acc_cw)
            exchange.send()

        dm_ref[...] = _dot_nt(dh1_ref[...], wout_ref[...])
        lb = _lower_bound(lbp_ref)
        lower, upper = _tri_masks()
        ltri = _ones_where(lower)
        utri = _ones_where(upper)
        whn_v = whn_ref[...]
        w0, w1, w2 = cw_ref[0:1, :], cw_ref[1:2, :], cw_ref[2:3, :]
        cx_before_tile = jnp.where(tile > 0, (pc_ref[...].astype(F32) * px_ref[...].astype(F32))[8:16, :], 0.0)
        rid = lax.broadcasted_iota(jnp.int32, (CH, D), 0)

        def chunk(cc, carry):
            c = ncht - 1 - cc
            r0 = pl.multiple_of(c * CH, CH)
            rows = pl.ds(r0, CH)
            slab = lambda n: proj_ref[rows, n * D:(n + 1) * D].astype(F32)
            q_raw, f_raw, v, g_out, b_gate, c_gate, x_conv = (slab(n) for n in range(7))
            sa = _sigmoid(slab(7))
            sb = _sigmoid(slab(8))
            dm_v = dm_ref[rows, :]

            sq = _sigmoid(q_raw)
            q = q_raw * sq
            sg, f, g, k = _gates(f_raw, lb)
            gam = _tri_matmul(ltri, g)
            gam_l = gam[CH - 1:CH, :]
            e_l = jnp.exp(gam_l)
            e_g = jnp.exp(gam)
            e_ng = jnp.exp(-gam)
            e_kl = jnp.exp(gam_l - gam)
            qt = q * e_g
            kt = k * e_ng
            khat = k * e_kl
            qt_b, kt_b, khat_b, v_b = qt.astype(BF16), kt.astype(BF16), khat.astype(BF16), v.astype(BF16)
            qt_seen, kt_seen = qt_b.astype(F32), kt_b.astype(F32)
            s_go = _sigmoid(g_out)
            silu_go = g_out * s_go
            cx = c_gate * x_conv
            rprev = pl.ds(pl.multiple_of(jnp.maximum(r0 - 16, 0), 16), 16)
            cx_prev_in = (proj_ref[rprev, 5 * D:6 * D].astype(F32) * proj_ref[rprev, 6 * D:7 * D].astype(F32))[8:16, :]
            prev8 = jnp.where(c > 0, cx_prev_in, cx_before_tile)
            cx_m1 = _shift_down(cx, prev8, 1)
            cx_m2 = _shift_down(cx, prev8, 2)
            cv = w0 * cx_m2 + w1 * cx_m1 + w2 * cx
            y_b = b_gate * cv

            o_v = o_ref[rows, :]
            ro_parts, on_parts = [], []
            for h in range(NH):
                cs = slice(h * HD, (h + 1) * HD)
                o_h = o_v[:, cs]
                ro = lax.rsqrt(jnp.mean(o_h * o_h, axis=-1, keepdims=True) + EPS)
                ro_parts.append(ro)
                on_parts.append(o_h * ro)
            on = jnp.concatenate(on_parts, axis=1)
            out_gate = whn_v * silu_go
            y_a = on * out_gate

            dy_a = dm_v * sa
            dy_b = dm_v * sb
            dproj_ref[rows, 7 * D:8 * D] = (dy_a * y_a * (1.0 - sa)).astype(BF16)
            dproj_ref[rows, 8 * D:9 * D] = (dy_b * y_b * (1.0 - sb)).astype(BF16)
            dproj_ref[rows, 4 * D:5 * D] = (dy_b * cv).astype(BF16)
            dcv = dy_b * b_gate
            acc_cw[0:1, :] += jnp.sum(dcv * cx_m2, axis=0, keepdims=True)
            acc_cw[1:2, :] += jnp.sum(dcv * cx_m1, axis=0, keepdims=True)
            acc_cw[2:3, :] += jnp.sum(dcv * cx, axis=0, keepdims=True)
            next8 = dcvc_ref[...]
            dcx = w2 * dcv + w1 * _shift_up(dcv, next8, 1) + w0 * _shift_up(dcv, next8, 2)
            dcvc_ref[...] = dcv[0:8, :]
            dproj_ref[rows, 5 * D:6 * D] = (dcx * x_conv).astype(BF16)
            dproj_ref[rows, 6 * D:7 * D] = (dcx * c_gate).astype(BF16)
            don = dy_a * out_gate
            dya_on = dy_a * on
            dproj_ref[rows, 3 * D:4 * D] = (dya_on * (whn_v * (s_go * (1.0 + g_out * (1.0 - s_go))))).astype(BF16)
            acc_hn[...] += jnp.sum(dya_on * silu_go, axis=0, keepdims=True)

            heads = [slice(h * HD, (h + 1) * HD) for h in range(NH)]
            do_bs, sts, dstns, first = [], [], [], []
            for h, cs in enumerate(heads):
                on_h = on_parts[h]
                don_h = don[:, cs]
                do_h = ro_parts[h] * (don_h - on_h * jnp.mean(don_h * on_h, axis=-1, keepdims=True))
                do_bs.append(do_h.astype(BF16))
                sts.append(sst_ref[c, h])
                dstns.append(dst_ref[h])
            for h, cs in enumerate(heads):
                qt_h, kt_h, khat_h, v_h = qt_b[:, cs], kt_b[:, cs], khat_b[:, cs], v_b[:, cs]
                do_b, dstn_b = do_bs[h], dstns[h].astype(BF16)
                first.append(dict(
                    a_t=_dot_nt(kt_h, qt_h), da=_dot_nt(do_b, v_h), da_t=_dot_nt(v_h, do_b),
                    dv_state=_dot_nt(khat_h, dstn_b), dqt_state=_dot(do_b, sts[h]), dkhat=_dot(v_h, dstn_b),
                    dst_chunk=_dot_tn(do_b, qt_h)))
            dq_parts, dk_parts, dv_parts, dgam_parts, ext_parts = [], [], [], [], []
            for h, cs in enumerate(heads):
                qt_h, kt_h = qt_b[:, cs], kt_b[:, cs]
                p1, do_b, st, dstn = first[h], do_bs[h], sts[h], dstns[h]
                dv_h = _dot(jnp.where(upper, p1["a_t"], 0.0), do_b) + p1["dv_state"]
                dqt_state = p1["dqt_state"]
                dqt_chunk = _dot(jnp.where(lower, p1["da"], 0.0), kt_h)
                dkt = _dot(jnp.where(upper, p1["da_t"], 0.0), qt_h)
                dkhat = p1["dkhat"]
                dq_h = (dqt_state + dqt_chunk) * e_g[:, cs]
                dk_h = dkt * e_ng[:, cs] + dkhat * e_kl[:, cs]
                khat_dkhat = dkhat * khat[:, cs]
                ext = (jnp.sum(khat_dkhat, axis=0, keepdims=True)
                       + e_l[:, cs] * jnp.sum(st * dstn, axis=0, keepdims=True))
                dst_ref[h] = p1["dst_chunk"] + dstn * e_l[:, cs]
                dq_parts.append(dq_h)
                dk_parts.append(dk_h)
                dv_parts.append(dv_h)
                dgam_parts.append(qt[:, cs] * dqt_state + qt_seen[:, cs] * dqt_chunk - kt_seen[:, cs] * dkt
                                  - khat_dkhat)
                ext_parts.append(ext)
            dq = jnp.concatenate(dq_parts, axis=1)
            dk = jnp.concatenate(dk_parts, axis=1)
            dgam = jnp.concatenate(dgam_parts, axis=1)
            ext = jnp.concatenate(ext_parts, axis=1)
            dgam = dgam + jnp.where(rid == CH - 1, ext, 0.0)
            dg = _tri_matmul(utri, dgam)
            dproj_ref[rows, 0:D] = (dq * (sq * (1.0 + q_raw * (1.0 - sq)))).astype(BF16)
            df = dg * jnp.exp(-g) - dk
            dproj_ref[rows, D:2 * D] = (df * (1.0 - lb) * sg * (1.0 - sg)).astype(BF16)
            dproj_ref[rows, 2 * D:3 * D] = jnp.concatenate(dv_parts, axis=1).astype(BF16)
            real = (tile * TR + r0 + rid) >= PAD
            acc_lb[...] += jnp.sum(jnp.where(real, df * (1.0 - sg), 0.0), axis=0, keepdims=True)
            return carry

        lax.fori_loop(0, ncht, chunk, 0)

        @pl.when(s == nt - 1)
        def _():
            dlb_ref[...] = acc_lb[...] * lb * (1.0 - lb)
            hn = acc_hn[...]
            tot = hn[:, 0:HD]
            for h in range(1, NH):
                tot = tot + hn[:, h * HD:(h + 1) * HD]
            dwhn_ref[...] = tot
            dcw_ref[...] = acc_cw[0:3, :]
            exchange.finish()

    hbm = pl.BlockSpec(memory_space=pl.ANY)
    rev = lambda s: (nt - 1 - s, 0)
    prevc = lambda s: (jnp.maximum((nt - 1 - s) * tb16 - 1, 0), 5)
    prevx = lambda s: (jnp.maximum((nt - 1 - s) * tb16 - 1, 0), 6)
    const = lambda s: (0, 0)
    return pl.pallas_call(
        body, name="mix_block_bwd", grid=(nt,),
        in_specs=[pl.BlockSpec((TR, D), rev),
                  pl.BlockSpec((TR, 9 * D), rev),
                  pl.BlockSpec((16, D), prevc),
                  pl.BlockSpec((16, D), prevx),
                  pl.BlockSpec((TR, D), rev),
                  pl.BlockSpec((ncht, NH, HD, HD), lambda s: (nt - 1 - s, 0, 0, 0)),
                  pl.BlockSpec((2, D), const),
                  pl.BlockSpec((1, D), const),
                  pl.BlockSpec((3, D), const),
                  pl.BlockSpec(memory_space=pltpu.VMEM)] + [hbm] * nex,
        out_specs=[pl.BlockSpec((TR, 9 * D), rev),
                   pl.BlockSpec((1, D), const),
                   pl.BlockSpec((1, HD), const),
                   pl.BlockSpec((3, D), const)] + [hbm] * nex,
        out_shape=[jax.ShapeDtypeStruct((T, 9 * D), BF16), jax.ShapeDtypeStruct((1, D), F32),
                   jax.ShapeDtypeStruct((1, HD), F32), jax.ShapeDtypeStruct((3, D), F32)]
        + [jax.ShapeDtypeStruct((3,) + p.shape[1:], p.dtype) for p in psums],
        scratch_shapes=[pltpu.VMEM((TR, D), F32), pltpu.VMEM((NH, HD, HD), F32), pltpu.VMEM((8, D), F32),
                        pltpu.VMEM((1, D), F32), pltpu.VMEM((1, D), F32), pltpu.VMEM((8, D), F32),
                        pltpu.SemaphoreType.DMA((nex, 3)), pltpu.SemaphoreType.DMA((nex, 3))],
        compiler_params=_cparams(("arbitrary",)),
    )(dh1, projb, projb, projb, o, sst, lb_param, whn, conv_w, w_out, *psums)


def _ffn_block_fwd(h1, tgt, wffn, w_up_g, fcw, fcb, w_down, wfin):
    T = h1.shape[0]
    nt = T // TR
    nsl = w_up_g.shape[0]
    wsl = w_up_g.shape[2]

    def body(h_ref, t_ref, wn_ref, wup_ref, cw_ref, cb_ref, wdn_ref, wf_ref,
             u2t_ref, upb_ref, ggt_ref, dh_ref, loss_ref, dwf_ref, up_scr, gg_ref, carry_ref):
        i = pl.program_id(0)

        @pl.when(i == 0)
        def _():
            carry_ref[...] = jnp.zeros_like(carry_ref)
            loss_ref[...] = jnp.zeros_like(loss_ref)
            dwf_ref[...] = jnp.zeros_like(dwf_ref)

        x = h_ref[...]
        r2 = lax.rsqrt(jnp.mean(x * x, axis=-1, keepdims=True) + EPS)
        u2_f = x * r2 * wn_ref[...]
        u2 = u2_f.astype(BF16)
        u2t_ref[...] = u2_f.T.astype(BF16)
        for s in range(nsl):
            up_s = jnp.dot(u2, wup_ref[s], preferred_element_type=F32)
            up_scr[:, s * wsl:(s + 1) * wsl] = up_s
            upb_ref[:, s * wsl:(s + 1) * wsl] = up_s.astype(BF16)
        w0, w1, w2 = cw_ref[0:1, :], cw_ref[1:2, :], cw_ref[2:3, :]

        def chunk(c, carry):
            rows = pl.ds(pl.multiple_of(c * CH, CH), CH)
            a_pre = up_scr[rows, 0:DFF]
            val = up_scr[rows, DFF:2 * DFF]
            prev8 = carry_ref[...]
            a = w0 * _shift_down(a_pre, prev8, 2) + w1 * _shift_down(a_pre, prev8, 1) + w2 * a_pre + cb_ref[...]
            carry_ref[...] = a_pre[CH - 8:CH, :]
            gg_ref[rows, :] = a * _sigmoid(a) * val
            return carry

        lax.fori_loop(0, TR // CH, chunk, 0, unroll=True)
        gg_v = gg_ref[...]
        ggt_ref[...] = gg_v.T.astype(BF16)
        h2 = x + jnp.dot(gg_v.astype(BF16), wdn_ref[...], preferred_element_type=F32)
        r3 = lax.rsqrt(jnp.mean(h2 * h2, axis=-1, keepdims=True) + EPS)
        n3 = h2 * r3
        wf = wf_ref[...]
        diff = jnp.where(i > 0, n3 * wf - t_ref[...], 0.0)
        loss_ref[...] += jnp.sum(diff * diff, axis=0, keepdims=True) * (0.5 / D)
        dy = diff * (1.0 / D)
        dwf_ref[...] += jnp.sum(dy * n3, axis=0, keepdims=True)
        dn = dy * wf
        dh_ref[...] = r3 * (dn - n3 * jnp.mean(dn * n3, axis=-1, keepdims=True))

    row = lambda w: pl.BlockSpec((TR, w), lambda i: (i, 0))
    col = lambda w: pl.BlockSpec((w, TR), lambda i: (0, i))
    vec = lambda w, r=1: pl.BlockSpec((r, w), lambda i: (0, 0))
    vm = pl.BlockSpec(memory_space=pltpu.VMEM)
    return pl.pallas_call(
        body, name="ffn_block_fwd", grid=(nt,),
        in_specs=[row(D), pl.BlockSpec((TR, D), lambda i: (jnp.maximum(i - 1, 0), 0)), vec(D), vm,
                  vec(DFF, 3), vec(DFF), vm, vec(D)],
        out_specs=[col(D), row(2 * DFF), col(DFF), row(D), vec(D), vec(D)],
        out_shape=[jax.ShapeDtypeStruct((D, T), BF16), jax.ShapeDtypeStruct((T, 2 * DFF), BF16),
                   jax.ShapeDtypeStruct((DFF, T), BF16), jax.ShapeDtypeStruct((T, D), F32),
                   jax.ShapeDtypeStruct((1, D), F32), jax.ShapeDtypeStruct((1, D), F32)],
        scratch_shapes=[pltpu.VMEM((TR, 2 * DFF), F32), pltpu.VMEM((TR, DFF), F32), pltpu.VMEM((8, DFF), F32)],
        compiler_params=_cparams(("arbitrary",)),
    )(h1, tgt, wffn, w_up_g, fcw, fcb, w_down, wfin)


def _ffn_block_bwd(dh2, upb, h1, wffn, w_up_g, fcw, fcb, w_down):
    T = h1.shape[0]
    nt = T // TR
    ncht = TR // CH
    nsl = w_up_g.shape[0]
    wsl = w_up_g.shape[2]
    tb16 = TR // 16
    assert ncht % nsl == 0
    every = ncht // nsl
    step = -(-DFF // (ncht * 128)) * 128
    parts = [(c0, min(c0 + step, DFF)) for c0 in range(0, DFF, step)]
    assert len(parts) == ncht

    def body(dh2n_ref, dh2p_ref, up_ref, pa_ref, h_ref, wn_ref, wup_ref, cw_ref, cb_ref, wdn_ref,
             dup_ref, dh1_ref, dfw_ref, dfb_ref, dwn_ref,
             dgg_ring, dup_ring, carry_ref, acc_w, acc_b, acc_n):
        s = pl.program_id(0)
        slot = lax.rem(s, 2)
        other = 1 - slot

        @pl.when(s == 0)
        def _():
            carry_ref[...] = jnp.zeros_like(carry_ref)
            acc_w[...] = jnp.zeros_like(acc_w)
            acc_b[...] = jnp.zeros_like(acc_b)
            acc_n[...] = jnp.zeros_like(acc_n)
            dup_ring[1] = jnp.zeros((TR, 2 * DFF), BF16)
            dgg_ring[0] = _dot_nt(dh2p_ref[...], wdn_ref[...])

        def norm_bwd(du2, valid):
            x = h_ref[...]
            dh2p = dh2p_ref[...]
            r2 = lax.rsqrt(jnp.mean(x * x, axis=-1, keepdims=True) + EPS)
            n2 = x * r2
            dn = du2 * wn_ref[...]
            dh1_ref[...] = dh2p + r2 * (dn - n2 * jnp.mean(dn * n2, axis=-1, keepdims=True))
            acc_n[...] += jnp.where(valid, jnp.sum(du2 * n2, axis=0, keepdims=True), 0.0)

        @pl.when(s < nt)
        def _():
            tile = nt - 1 - s
            w0, w1, w2 = cw_ref[0:1, :], cw_ref[1:2, :], cw_ref[2:3, :]
            a_before_tile = jnp.where(tile > 0, pa_ref[...].astype(F32)[8:16, :], 0.0)
            dh2n = dh2n_ref[...].astype(BF16)
            du2 = None
            for idx in range(ncht):
                c = ncht - 1 - idx
                r0 = c * CH
                rows = slice(r0, r0 + CH)
                a_pre = up_ref[rows, 0:DFF].astype(F32)
                val = up_ref[rows, DFF:2 * DFF].astype(F32)
                prev8 = up_ref[r0 - 16:r0, 0:DFF].astype(F32)[8:16, :] if c > 0 else a_before_tile
                a_m1 = _shift_down(a_pre, prev8, 1)
                a_m2 = _shift_down(a_pre, prev8, 2)
                a = w0 * a_m2 + w1 * a_m1 + w2 * a_pre + cb_ref[...]
                sig = _sigmoid(a)
                dgg_v = dgg_ring[slot, rows, :]
                da = dgg_v * val * (sig * (1.0 + a * (1.0 - sig)))
                dval = (dgg_v * (a * sig)).astype(BF16)
                next8 = carry_ref[...]
                da_pre = (w2 * da + w1 * _shift_up(da, next8, 1) + w0 * _shift_up(da, next8, 2)).astype(BF16)
                carry_ref[...] = da[0:8, :]
                dup_ref[rows, 0:DFF] = da_pre
                dup_ref[rows, DFF:2 * DFF] = dval
                dup_ring[slot, rows, 0:DFF] = da_pre
                dup_ring[slot, rows, DFF:2 * DFF] = dval
                acc_w[0:1, :] += jnp.sum(da * a_m2, axis=0, keepdims=True)
                acc_w[1:2, :] += jnp.sum(da * a_m1, axis=0, keepdims=True)
                acc_w[2:3, :] += jnp.sum(da * a_pre, axis=0, keepdims=True)
                acc_b[...] += jnp.sum(da, axis=0, keepdims=True)
                c0, c1 = parts[idx]
                dgg_ring[other, :, c0:c1] = _dot_nt(dh2n, wdn_ref[c0:c1, :])
                if idx % every == 0:
                    sl = idx // every
                    part = _dot_nt(dup_ring[other, :, sl * wsl:(sl + 1) * wsl], wup_ref[sl])
                    du2 = part if du2 is None else du2 + part
            norm_bwd(du2, s > 0)

        @pl.when(s == nt)
        def _():
            du2 = _dot_nt(dup_ring[other, :, 0:wsl], wup_ref[0])
            for sl in range(1, nsl):
                du2 = du2 + _dot_nt(dup_ring[other, :, sl * wsl:(sl + 1) * wsl], wup_ref[sl])
            norm_bwd(du2, True)
            dfw_ref[...] = acc_w[0:3, :]
            dfb_ref[...] = acc_b[...]
            dwn_ref[...] = acc_n[...]

    gate_tile = lambda s: jnp.maximum(nt - 1 - s, 0)
    next_tile = lambda s: jnp.maximum(nt - 2 - s, 0)
    prev_tile = lambda s: jnp.minimum(nt - s, nt - 1)
    vec = lambda w, r=1: pl.BlockSpec((r, w), lambda s: (0, 0))
    vm = pl.BlockSpec(memory_space=pltpu.VMEM)
    return pl.pallas_call(
        body, name="ffn_block_bwd", grid=(nt + 1,),
        in_specs=[pl.BlockSpec((TR, D), lambda s: (next_tile(s), 0)),
                  pl.BlockSpec((TR, D), lambda s: (prev_tile(s), 0)),
                  pl.BlockSpec((TR, 2 * DFF), lambda s: (gate_tile(s), 0)),
                  pl.BlockSpec((16, DFF), lambda s: (jnp.maximum(gate_tile(s) * tb16 - 1, 0), 0)),
                  pl.BlockSpec((TR, D), lambda s: (prev_tile(s), 0)),
                  vec(D), vm, vec(DFF, 3), vec(DFF), vm],
        out_specs=[pl.BlockSpec((TR, 2 * DFF), lambda s: (gate_tile(s), 0)),
                   pl.BlockSpec((TR, D), lambda s: (prev_tile(s), 0)),
                   vec(DFF, 3), vec(DFF), vec(D)],
        out_shape=[jax.ShapeDtypeStruct((T, 2 * DFF), BF16), jax.ShapeDtypeStruct((T, D), F32),
                   jax.ShapeDtypeStruct((3, DFF), F32), jax.ShapeDtypeStruct((1, DFF), F32),
                   jax.ShapeDtypeStruct((1, D), F32)],
        scratch_shapes=[pltpu.VMEM((2, TR, DFF), F32), pltpu.VMEM((2, TR, 2 * DFF), BF16),
                        pltpu.VMEM((8, DFF), F32), pltpu.VMEM((8, DFF), F32), pltpu.VMEM((1, DFF), F32),
                        pltpu.VMEM((1, D), F32)],
        compiler_params=_cparams(("arbitrary",)),
    )(dh2, dh2, upb, upb, h1, wffn, w_up_g, fcw, fcb, w_down)


def _place():
    x, y, c = lax.axis_index("x"), lax.axis_index("y"), lax.axis_index("c")
    return x, y, c


_CHIP_FLIPS = ((1, 0), (0, 1), (1, 1))


def _flip(v, bit):
    return 1 - v if bit else v


def _into_slot(w, j_idx, *, rb, dtype, name, paired=False):
    r, cdim = w.shape

    def body(j_ref, w_ref, out_ref):
        del j_ref
        out_ref[...] = w_ref[...].astype(dtype)

    if paired:
        out_shape = jax.ShapeDtypeStruct((NSHARD // 2, r, 2 * cdim), dtype)
        out_spec = pl.BlockSpec((None, rb, cdim), lambda i, j_ref: (j_ref[0] // 2, i, j_ref[0] % 2))
    else:
        out_shape = jax.ShapeDtypeStruct((NSHARD, r, cdim), dtype)
        out_spec = pl.BlockSpec((None, rb, cdim), lambda i, j_ref: (j_ref[0], i, 0))
    grid_spec = pltpu.PrefetchScalarGridSpec(
        num_scalar_prefetch=1, grid=(r // rb,),
        in_specs=[pl.BlockSpec((rb, cdim), lambda i, j_ref: (i, 0))], out_specs=out_spec)
    return pl.pallas_call(
        body, name=name, grid_spec=grid_spec, out_shape=out_shape, compiler_params=_cparams(("parallel",)),
    )(j_idx, w)


class _Gather:
    def __init__(self, outs, send_sems, recv_sems, whole, paired=None):
        self.outs, self.send_sems, self.recv_sems, self.whole = outs, send_sems, recv_sems, whole
        self.paired = paired if paired is not None else (False,) * len(outs)
        self.x, self.y, self.c = _place()
        self.j = 2 * self.x + self.y
        self.sibling = (self.x, self.y, 1 - self.c)
        chips = [(_flip(self.x, fx), _flip(self.y, fy)) for fx, fy in _CHIP_FLIPS]
        self.slots = [2 * px + py for px, py in chips]
        self.peers = [(px, py, self.c) for px, py in chips]

    def _copy(self, w, slot, core, sem, to, quarter=None):
        r = self.outs[w].shape[1]
        if self.whole[w]:
            rows = pl.ds(0, r)
        elif quarter is None:
            rows = pl.ds(core * (r // 2), r // 2)
        else:
            rows = pl.ds(core * (r // 2) + quarter * (r // 4), r // 4)
        if self.paired[w]:
            cw = self.outs[w].shape[2] // 2
            piece = self.outs[w].at[slot // 2, rows, pl.ds((slot % 2) * cw, cw)]
        else:
            piece = self.outs[w].at[slot, rows, :]
        return pltpu.make_async_remote_copy(
            src_ref=piece, dst_ref=piece, send_sem=self.send_sems.at[w, sem], recv_sem=self.recv_sems.at[w, sem],
            device_id=to, device_id_type=MESH)

    def _direct(self, w):
        return (0, 1, 2) if self.whole[w] else (0, 1)

    def send(self):
        for w in range(len(self.outs)):
            for k in self._direct(w):
                self._copy(w, self.j, self.c, k, self.peers[k]).start()

    def relay(self):
        jx, jy, jd = self.slots
        for w in range(len(self.outs)):
            if self.whole[w]:
                for k in range(3):
                    self._copy(w, self.slots[k], self.c, k, self.sibling).wait_recv()
                continue
            self._copy(w, jx, self.c, 0, self.sibling).wait_recv()
            self._copy(w, jx, self.c, 6, self.peers[1], quarter=0).start()
            self._copy(w, jx, self.c, 3, self.sibling).start()
            self._copy(w, jy, self.c, 1, self.sibling).wait_recv()
            self._copy(w, jy, self.c, 7, self.peers[0], quarter=1).start()
            self._copy(w, jy, self.c, 4, self.sibling).start()
        for w in range(len(self.outs)):
            if not self.whole[w]:
                self._copy(w, jd, self.c, 6, self.sibling, quarter=0).wait_recv()
                self._copy(w, jd, self.c, 7, self.sibling, quarter=1).wait_recv()
                self._copy(w, jd, self.c, 5, self.sibling).start()

    def finish(self):
        jx, jy, jd = self.slots
        for w in range(len(self.outs)):
            for k in self._direct(w):
                self._copy(w, self.j, self.c, k, self.peers[k]).wait_send()
            if self.whole[w]:
                continue
            self._copy(w, jx, self.c, 6, self.peers[1], quarter=0).wait_send()
            self._copy(w, jy, self.c, 7, self.peers[0], quarter=1).wait_send()
            for k, slot in enumerate(self.slots):
                self._copy(w, slot, 1 - self.c, 3 + k, self.sibling).wait_recv()
                self._copy(w, slot, self.c, 3 + k, self.sibling).wait_send()


def _allgather_weights(slotted, whole):
    n = len(slotted)

    def body(*refs):
        g = _Gather(refs[n:2 * n], refs[2 * n], refs[2 * n + 1], whole)
        g.send()
        g.relay()
        g.finish()

    any_spec = pl.BlockSpec(memory_space=pl.ANY)
    return pl.pallas_call(
        body, name="allgather_weights",
        in_specs=[any_spec] * n, out_specs=[any_spec] * n,
        out_shape=[jax.ShapeDtypeStruct(a.shape, a.dtype) for a in slotted],
        input_output_aliases={i: i for i in range(n)},
        scratch_shapes=[pltpu.SemaphoreType.DMA((n, 8)), pltpu.SemaphoreType.DMA((n, 8))],
    )(*slotted)


class _ChipExchange:
    def __init__(self, ins, outs, send_sems, recv_sems):
        self.ins, self.outs, self.send_sems, self.recv_sems = ins, outs, send_sems, recv_sems
        self.x, self.y, self.c = _place()

    def _copies(self):
        for w in range(len(self.ins)):
            for kk, (fx, fy) in enumerate(_CHIP_FLIPS):
                px, py = _flip(self.x, fx), _flip(self.y, fy)
                yield pltpu.make_async_remote_copy(
                    src_ref=self.ins[w].at[2 * px + py], dst_ref=self.outs[w].at[kk],
                    send_sem=self.send_sems.at[w, kk], recv_sem=self.recv_sems.at[w, kk],
                    device_id=(px, py, self.c), device_id_type=MESH)

    def send(self):
        for cp in self._copies():
            cp.start()

    def finish(self):
        for cp in self._copies():
            cp.wait()


class _PairExchange:
    def __init__(self, ins, outs, send_sems, recv_sems):
        self.ins, self.outs, self.send_sems, self.recv_sems = ins, outs, send_sems, recv_sems
        self.x, self.y, self.c = _place()

    def _copies(self):
        for w in range(len(self.ins)):
            half = self.ins[w].shape[1] // 2
            yield pltpu.make_async_remote_copy(
                src_ref=self.ins[w].at[:, pl.ds((1 - self.c) * half, half), :], dst_ref=self.outs[w],
                send_sem=self.send_sems.at[w], recv_sem=self.recv_sems.at[w],
                device_id=(self.x, self.y, 1 - self.c), device_id_type=MESH)

    def send(self):
        for cp in self._copies():
            cp.start()

    def finish(self):
        for cp in self._copies():
            cp.wait()


def _pair_shapes(grads):
    return [jax.ShapeDtypeStruct((g.shape[0], g.shape[1] // 2, g.shape[2]), g.dtype) for g in grads]


def _pair_exchange(grads, name):
    nw = len(grads)

    def body(*refs):
        ex = _PairExchange(refs[:nw], refs[nw:2 * nw], refs[2 * nw], refs[2 * nw + 1])
        ex.send()
        ex.finish()

    any_spec = pl.BlockSpec(memory_space=pl.ANY)
    return pl.pallas_call(
        body, name=name,
        in_specs=[any_spec] * nw, out_specs=[any_spec] * nw, out_shape=_pair_shapes(grads),
        scratch_shapes=[pltpu.SemaphoreType.DMA((nw,)), pltpu.SemaphoreType.DMA((nw,))],
    )(*grads)


def _pair_add(g, other, c_idx, *, rb, name):
    S, r, cdim = g.shape
    half = r // 2
    nb = half // rb

    def body(c_ref, g_ref, o_ref, out_ref):
        del c_ref
        out_ref[...] = (g_ref[...].astype(F32) + o_ref[...].astype(F32)).astype(BF16)

    grid_spec = pltpu.PrefetchScalarGridSpec(
        num_scalar_prefetch=1, grid=(S, nb),
        in_specs=[pl.BlockSpec((None, rb, cdim), lambda s, i, c_ref: (s, c_ref[0] * nb + i, 0)),
                  pl.BlockSpec((None, rb, cdim), lambda s, i, c_ref: (s, i, 0))],
        out_specs=pl.BlockSpec((None, rb, cdim), lambda s, i, c_ref: (s, i, 0)))
    return pl.pallas_call(
        body, name=name, grid_spec=grid_spec, out_shape=jax.ShapeDtypeStruct((S, half, cdim), BF16),
        compiler_params=_cparams(("parallel", "parallel")),
    )(c_idx, g, other)


def _chip_sum(psum, parts, cj_idx, *, rb, name):
    S, half, cdim = psum.shape
    nb = half // rb

    def body(cj_ref, own_ref, p_ref, out_ref):
        del cj_ref
        f = lambda v: v.astype(F32)
        out_ref[...] = ((f(own_ref[...]) + f(p_ref[0])) + f(p_ref[1])) + f(p_ref[2])

    grid_spec = pltpu.PrefetchScalarGridSpec(
        num_scalar_prefetch=1, grid=(nb,),
        in_specs=[pl.BlockSpec((None, rb, cdim), lambda i, cj: (cj[1], i, 0)),
                  pl.BlockSpec((3, rb, cdim), lambda i, cj: (0, i, 0))],
        out_specs=pl.BlockSpec((rb, cdim), lambda i, cj: (cj[0] * nb + i, 0)))
    return pl.pallas_call(
        body, name=name, grid_spec=grid_spec, out_shape=jax.ShapeDtypeStruct((2 * half, cdim), F32),
        compiler_params=_cparams(("parallel",)),
    )(cj_idx, psum, parts)


SLAB_W = 1024


def _final_exchange(pieces, grads):
    n, nw = len(pieces), len(grads)
    segs, at = [], 0
    for idx, p in enumerate(pieces):
        r, wd = p.shape
        for c0 in range(0, wd, SLAB_W):
            if r > 1:
                at = -(-at // 8) * 8
            segs.append((idx, c0, min(SLAB_W, wd - c0), at))
            at += r
    rows = -(-at // 8) * 8
    flips = [(fx, fy, fc) for fx in (0, 1) for fy in (0, 1) for fc in (0, 1)][1:]

    def body(*refs):
        ins = refs[:n]
        outs = refs[n + nw:2 * n + nw]
        g_refs = refs[2 * n + nw:2 * n + 2 * nw]
        mine_ref, slots_ref, send_sems, recv_sems, gsend_sems, grecv_sems = refs[2 * n + 2 * nw:]
        x, y, c = _place()
        me = 4 * x + 2 * y + c
        sibling = (x, y, 1 - c)

        def swap(w, core):
            half = g_refs[w].shape[0] // 2
            rows_ = g_refs[w].at[pl.ds(core * half, half), :]
            return pltpu.make_async_remote_copy(
                src_ref=rows_, dst_ref=rows_, send_sem=gsend_sems.at[w], recv_sem=grecv_sems.at[w],
                device_id=sibling, device_id_type=MESH)

        for w in range(nw):
            swap(w, c).start()
        mine_ref[...] = jnp.zeros_like(mine_ref)
        for idx, c0, wd, st in segs:
            r = ins[idx].shape[0]
            mine_ref[st:st + r, 0:wd] = ins[idx][:, c0:c0 + wd]
        slots_ref[me] = mine_ref[...]
        cps = []
        for kk, (fx, fy, fc) in enumerate(flips):
            cp = pltpu.make_async_remote_copy(
                src_ref=mine_ref, dst_ref=slots_ref.at[me], send_sem=send_sems.at[kk], recv_sem=recv_sems.at[kk],
                device_id=(_flip(x, fx), _flip(y, fy), _flip(c, fc)), device_id_type=MESH)
            cp.start()
            cps.append(cp)
        for cp in cps:
            cp.wait()
        tot = slots_ref[0]
        for d in range(1, 8):
            tot = tot + slots_ref[d]
        mine_ref[...] = tot
        for idx, c0, wd, st in segs:
            r = ins[idx].shape[0]
            val = mine_ref[st:st + r, 0:wd]
            if idx == n - 1:
                outs[idx][...] = jnp.sum(val, keepdims=True)
            else:
                outs[idx][:, c0:c0 + wd] = val
        for w in range(nw):
            swap(w, 1 - c).wait_recv()
            swap(w, c).wait_send()

    vm = pl.BlockSpec(memory_space=pltpu.VMEM)
    hbm = pl.BlockSpec(memory_space=pl.ANY)
    out_shape = ([jax.ShapeDtypeStruct(p.shape, F32) for p in pieces[:-1]] + [jax.ShapeDtypeStruct((1, 1), F32)]
                 + [jax.ShapeDtypeStruct(g.shape, g.dtype) for g in grads])
    res = pl.pallas_call(
        body, name="final_exchange", in_specs=[vm] * n + [hbm] * nw, out_specs=[vm] * n + [hbm] * nw,
        out_shape=out_shape, input_output_aliases={n + w: n + w for w in range(nw)},
        scratch_shapes=[pltpu.VMEM((rows, SLAB_W), F32), pltpu.VMEM((8, rows, SLAB_W), F32),
                        pltpu.SemaphoreType.DMA((7,)), pltpu.SemaphoreType.DMA((7,)),
                        pltpu.SemaphoreType.DMA((nw,)), pltpu.SemaphoreType.DMA((nw,))],
    )(*pieces, *grads)
    return res[:n], res[n:]


def _adamw(w, g, m, v, *, rb, name):
    r, cdim = w.shape

    def body(w_ref, g_ref, m_ref, v_ref, go_ref, d_ref, nm_ref, nv_ref):
        go_ref[...] = g_ref[...]
        d_ref[...], nm_ref[...], nv_ref[...] = _adamw_update(w_ref[...], g_ref[...], m_ref[...], v_ref[...])

    spec = pl.BlockSpec((rb, cdim), lambda i: (i, 0))
    shp = jax.ShapeDtypeStruct((r, cdim), F32)
    return pl.pallas_call(
        body, name=name, grid=(r // rb,), in_specs=[spec] * 4, out_specs=[spec] * 4, out_shape=[shp] * 4,
        compiler_params=_cparams(("parallel",)),
    )(w, g, m, v)


def _adamw_update(w, g, m, v):
    nm = ADAM_B1 * m + (1.0 - ADAM_B1) * g
    nv = ADAM_B2 * v + (1.0 - ADAM_B2) * (g * g)
    m_hat = nm / (1.0 - ADAM_B1 ** ADAM_STEP)
    v_hat = nv / (1.0 - ADAM_B2 ** ADAM_STEP)
    return -ADAM_LR * (m_hat / (jnp.sqrt(v_hat) + ADAM_EPS) + ADAM_WD * w), nm, nv


def _adamw_small(params):
    n = len(params)

    def body(*refs):
        ins, outs = refs[:4 * n], refs[4 * n:]
        for p in range(n):
            w_ref, g_ref, m_ref, v_ref = ins[4 * p:4 * p + 4]
            d, nm, nv = _adamw_update(w_ref[...], g_ref[...], m_ref[...], v_ref[...])
            outs[3 * p][...] = d
            outs[3 * p + 1][...] = nm
            outs[3 * p + 2][...] = nv

    vm = pl.BlockSpec(memory_space=pltpu.VMEM)
    flat = [a for p in params for a in p]
    out_shape = [jax.ShapeDtypeStruct(p[0].shape, F32) for p in params for _ in range(3)]
    res = pl.pallas_call(
        body, name="adamw_small", in_specs=[vm] * (4 * n), out_specs=[vm] * (3 * n), out_shape=out_shape,
    )(*flat)
    return [tuple(res[3 * p:3 * p + 3]) for p in range(n)]


_PAIR_ADD_ROWS = {"w_in": 256, "w_out": 128, "w_up": 256, "w_down": 176}


def _pair_sums(grads, others, names, c_idx):
    return [_pair_add(g, o_, c_idx, rb=_PAIR_ADD_ROWS[n], name=f"pair_add_{n}") for g, o_, n in zip(grads, others, names)]


def _local_step(x, tgt, meta_full, lb_param, attn_norm_w, w_in_g, hgrn_norm_w, conv_w_full, w_out_full,
                ffn_norm_w, late_slotted, fcw_full, ffn_conv_b, final_norm_w, c_idx, cj_idx):
    seq = x.shape[0]
    T = TR + seq
    head_tile = jnp.concatenate([jnp.zeros((PAD, D), F32), meta_full], axis=0)
    whn_t = jnp.tile(hgrn_norm_w, (1, NH))

    ut, projb, o, sst, mt, h1, w_up_g, w_down_g = _mix_block_fwd(
        x, head_tile, lb_param, attn_norm_w, whn_t, conv_w_full, w_in_g, w_out_full, late_slotted, (True, False))
    w_down_full = w_down_g.reshape(DFF, D)
    u2t, upb, ggt, dh2, loss_vec, dwfin = _ffn_block_fwd(
        h1, tgt, ffn_norm_w, w_up_g, fcw_full, ffn_conv_b, w_down_full, final_norm_w.reshape(1, D))

    dup, dh1, dfw, dfb, dwffn = _ffn_block_bwd(
        dh2, upb, h1, ffn_norm_w, w_up_g, fcw_full, ffn_conv_b, w_down_full)
    kb = 1408 if T % 1408 == 0 else TR
    g_up = _weight_grad(u2t, dup, bn=DFF, bk=kb, name="dw_up_mm", shard_cols=2 * DFF // NSHARD)
    g_down, sib_up = _weight_grad(ggt, dh2, bn=D // 2, bk=2816 if T % 2816 == 0 else kb, name="dw_down_mm", ride=[g_up])
    g_down = g_down.reshape(NSHARD, DFF // NSHARD, D)
    sib_down = _pair_exchange([g_down], name="grad_pair_exchange_down")
    ps_ffn = _pair_sums([g_up, g_down], sib_up + list(sib_down), ("w_up", "w_down"), c_idx)

    dproj, dlb, dwhn, dcw, *parts_ffn = _mix_block_bwd(
        dh1, projb, o, sst, lb_param, whn_t, conv_w_full, w_out_full, ps_ffn)
    kb_deep = 2816 if T % 2816 == 0 else kb
    g_in = _weight_grad(ut, dproj, bn=9 * D // NSHARD, bk=kb_deep, name="dw_in_mm", shard_cols=9 * D // NSHARD)
    g_out, sib_in = _weight_grad(mt, dh1, bn=D, bk=kb_deep, name="dw_out_mm", ride=[g_in])
    g_out = g_out.reshape(NSHARD, D // NSHARD, D)
    sib_out = _pair_exchange([g_out], name="grad_pair_exchange_out")
    ps_mix = _pair_sums([g_in, g_out], sib_in + list(sib_out), ("w_in", "w_out"), c_idx)

    grad_x, dmeta, dwattn, *parts_mix = _input_grad_block(dproj, x, head_tile, dh1, attn_norm_w, w_in_g, ps_mix)

    halves = [_chip_sum(ps, p, cj_idx, rb=_PAIR_ADD_ROWS[n], name=f"chip_sum_{n}")
              for ps, p, n in zip(ps_mix + ps_ffn, parts_mix + parts_ffn, ("w_in", "w_out", "w_up", "w_down"))]
    small = dict(dlb=dlb, dwattn=dwattn, dwhn=dwhn, dwffn=dwffn, dfb=dfb, dwfin=dwfin,
                 dcw=dcw, dfw=dfw, dmeta=dmeta, loss=loss_vec)
    return grad_x, small, halves


_SMALL_ORDER = ("dmeta", "dcw", "dfw", "dlb", "dwattn", "dwhn", "dwffn", "dfb", "dwfin", "loss")


def kernel(x, meta_tokens, lb_param, attn_norm_w, w_in, hgrn_norm_w, conv_w, w_out, ffn_norm_w, w_up, ffn_conv_w, ffn_conv_b, w_down, final_norm_w, loss_target, m_meta_tokens, m_lb_param, m_attn_norm_w, m_w_in, m_hgrn_norm_w, m_conv_w, m_w_out, m_ffn_norm_w, m_w_up, m_ffn_conv_w, m_ffn_conv_b, m_w_down, m_final_norm_w, v_meta_tokens, v_lb_param, v_attn_norm_w, v_w_in, v_hgrn_norm_w, v_conv_w, v_w_out, v_ffn_norm_w, v_w_up, v_ffn_conv_w, v_ffn_conv_b, v_w_down, v_final_norm_w):
    xi, yi, ci = _place()
    j = 2 * xi + yi
    c_idx = jnp.reshape(ci, (1,)).astype(jnp.int32)

    j_idx = jnp.reshape(j, (1,)).astype(jnp.int32)
    ds_, fs_ = D // NSHARD, DFF // NSHARD
    widen = lambda a: jnp.pad(a, ((0, 0), (0, 768 - a.shape[1])))
    rows_small = jnp.concatenate([widen(meta_tokens), widen(conv_w[0]), widen(ffn_conv_w[0]),
                                  jnp.zeros((2, 768), F32)], axis=0)
    s_in, s_out, s_up, s_down = [
        _into_slot(w[0], j_idx, rb=rb, dtype=BF16, name=f"slot_{n}", paired=pr)
        for w, rb, n, pr in ((w_in, 256, "w_in", False), (w_out, 128, "w_out", False), (w_up, 256, "w_up", True),
                             (w_down, 176, "w_down", False))]
    s_small = _into_slot(rows_small, j_idx, rb=rows_small.shape[0], dtype=F32, name="slot_small")
    w_in_g, w_out_g, small_g = _allgather_weights([s_in, s_out, s_small], (False, False, True))
    unshard = lambda a: jnp.transpose(a, (1, 0, 2)).reshape(a.shape[1], -1)
    meta_full = unshard(small_g[:, 0:NMETA, 0:ds_])
    conv_w_full = unshard(small_g[:, NMETA:NMETA + 3, 0:ds_])
    fcw_full = unshard(small_g[:, NMETA + 3:NMETA + 6, 0:fs_])

    cj_idx = jnp.stack([ci, j]).astype(jnp.int32)
    grad_x, small, halves = _local_step(
        x[0], loss_target[0], meta_full, lb_param, attn_norm_w, w_in_g, hgrn_norm_w, conv_w_full,
        w_out_g.reshape(D, D), ffn_norm_w, [s_up, s_down], fcw_full, ffn_conv_b, final_norm_w, c_idx, cj_idx)

    names = _SMALL_ORDER
    small_sums, g_big = _final_exchange([small[n] for n in names], halves)
    vals = dict(zip(names, small_sums))
    loss = vals["loss"].reshape(())
    g_small = {
        "meta_tokens": lax.dynamic_slice_in_dim(vals["dmeta"], j * (D // NSHARD), D // NSHARD, axis=1),
        "lb_param": jnp.concatenate([vals["dlb"], -vals["dlb"]], axis=0),
        "attn_norm_w": vals["dwattn"],
        "hgrn_norm_w": vals["dwhn"],
        "conv_w": lax.dynamic_slice_in_dim(vals["dcw"], j * (D // NSHARD), D // NSHARD, axis=1)[None],
        "ffn_norm_w": vals["dwffn"],
        "ffn_conv_w": lax.dynamic_slice_in_dim(vals["dfw"], j * (DFF // NSHARD), DFF // NSHARD, axis=1)[None],
        "ffn_conv_b": vals["dfb"],
        "final_norm_w": vals["dwfin"].reshape(D),
    }


    weights = {"meta_tokens": meta_tokens, "lb_param": lb_param, "attn_norm_w": attn_norm_w, "w_in": w_in,
               "hgrn_norm_w": hgrn_norm_w, "conv_w": conv_w, "w_out": w_out, "ffn_norm_w": ffn_norm_w,
               "w_up": w_up, "ffn_conv_w": ffn_conv_w, "ffn_conv_b": ffn_conv_b, "w_down": w_down,
               "final_norm_w": final_norm_w}
    ms = {"meta_tokens": m_meta_tokens, "lb_param": m_lb_param, "attn_norm_w": m_attn_norm_w, "w_in": m_w_in,
          "hgrn_norm_w": m_hgrn_norm_w, "conv_w": m_conv_w, "w_out": m_w_out, "ffn_norm_w": m_ffn_norm_w,
          "w_up": m_w_up, "ffn_conv_w": m_ffn_conv_w, "ffn_conv_b": m_ffn_conv_b, "w_down": m_w_down,
          "final_norm_w": m_final_norm_w}
    vs = {"meta_tokens": v_meta_tokens, "lb_param": v_lb_param, "attn_norm_w": v_attn_norm_w, "w_in": v_w_in,
          "hgrn_norm_w": v_hgrn_norm_w, "conv_w": v_conv_w, "w_out": v_w_out, "ffn_norm_w": v_ffn_norm_w,
          "w_up": v_w_up, "ffn_conv_w": v_ffn_conv_w, "ffn_conv_b": v_ffn_conv_b, "w_down": v_w_down,
          "final_norm_w": v_final_norm_w}
    order = list(weights)
    grads, deltas, new_m, new_v = {}, {}, {}, {}

    for name, g, rb in zip(("w_in", "w_out", "w_up", "w_down"), g_big, (256, 128, 256, 176)):
        shp = weights[name].shape
        w2, m2, v2 = (a.reshape(shp[1], shp[2]) for a in (weights[name], ms[name], vs[name]))
        g_, d_, nm_, nv_ = _adamw(w2, g, m2, v2, rb=rb, name=f"adamw_{name}")
        grads[name], deltas[name], new_m[name], new_v[name] = (a.reshape(shp) for a in (g_, d_, nm_, nv_))

    small_names = [n for n in order if n not in grads]
    as2d = lambda a: a.reshape(-1, a.shape[-1])
    res = _adamw_small([tuple(as2d(a) for a in (weights[n], g_small[n], ms[n], vs[n])) for n in small_names])
    for n, (d_, nm_, nv_) in zip(small_names, res):
        shp = weights[n].shape
        grads[n], deltas[n], new_m[n], new_v[n] = (a.reshape(shp) for a in (g_small[n], d_, nm_, nv_))

    return (loss, grad_x[None], *[grads[n] for n in order], *[deltas[n] for n in order],
            *[new_m[n] for n in order], *[new_v[n] for n in order])
```

```python
import jax
import jax.numpy as jnp
from jax import lax
from jax.experimental import pallas as pl
from jax.experimental.pallas import tpu as pltpu

F32 = jnp.float32
BF16 = jnp.bfloat16
MESH = pl.DeviceIdType.MESH

D = 1024
NH = 8
HD = 128
DFF = 2816
NMETA = 16
EPS = 1e-6
TR = 256
PAD = TR - NMETA
CH = 64
NSHARD = 4
VMEM_LIMIT = 62 * 1024 * 1024

ADAM_LR = 0.001
ADAM_B1 = 0.9
ADAM_B2 = 0.999
ADAM_EPS = 1e-08
ADAM_WD = 0.01
ADAM_STEP = 10


def _cparams(semantics=None, **kw):
    return pltpu.CompilerParams(dimension_semantics=semantics, vmem_limit_bytes=VMEM_LIMIT, **kw)


def _sigmoid(x):
    return 0.5 * jnp.tanh(0.5 * x) + 0.5


def _weight_grad(at, b, *, bn, bk, name, shard_cols=None, ride=()):
    M, T = at.shape
    N = b.shape[1]
    assert b.shape[0] == T and T % bk == 0 and N % bn == 0, (name, M, N, T, bn, bk)
    nk = T // bk
    nj = N // bn
    nex = len(ride)
    if shard_cols is None:
        nsh = 1
        blk = (M, bn)
        out_shape = jax.ShapeDtypeStruct((M, N), BF16)
        out_spec = pl.BlockSpec(blk, lambda j, k: (0, j))
    else:
        assert bn % shard_cols == 0 and N % shard_cols == 0
        nsh = bn // shard_cols
        blk = (nsh, M, shard_cols)
        out_shape = jax.ShapeDtypeStruct((N // shard_cols, M, shard_cols), BF16)
        out_spec = pl.BlockSpec(blk, lambda j, k: (j, 0, 0))

    def body(*refs):
        a_ref, b_ref = refs[:2]
        o_ref = refs[2 + nex]
        acc_ref = refs[3 + 2 * nex]
        j, k = pl.program_id(0), pl.program_id(1)
        if nex:
            exchange = _PairExchange(refs[2:2 + nex], refs[3 + nex:3 + 2 * nex], refs[-2], refs[-1])

            @pl.when(jnp.logical_and(j == 0, k == 0))
            def _():
                exchange.send()

        @pl.when(k == 0)
        def _():
            acc_ref[...] = jnp.zeros_like(acc_ref)

        p = jnp.dot(a_ref[...].astype(BF16), b_ref[...].astype(BF16), preferred_element_type=F32)
        if shard_cols is None:
            acc_ref[...] += p
        else:
            for q in range(nsh):
                acc_ref[q] += p[:, q * shard_cols:(q + 1) * shard_cols]

        @pl.when(k == nk - 1)
        def _():
            o_ref[...] = acc_ref[...].astype(BF16)

        if nex:
            @pl.when(jnp.logical_and(j == nj - 1, k == nk - 1))
            def _():
                exchange.finish()

    hbm = pl.BlockSpec(memory_space=pl.ANY)
    scratch = [pltpu.VMEM(blk, F32)]
    if nex:
        scratch += [pltpu.SemaphoreType.DMA((nex,)), pltpu.SemaphoreType.DMA((nex,))]
    res = pl.pallas_call(
        body, name=name, grid=(nj, nk),
        in_specs=[pl.BlockSpec((M, bk), lambda j, k: (0, k)), pl.BlockSpec((bk, bn), lambda j, k: (k, j))] + [hbm] * nex,
        out_specs=[out_spec] + [hbm] * nex, out_shape=[out_shape] + _pair_shapes(ride), scratch_shapes=scratch,
        compiler_params=_cparams(("arbitrary", "arbitrary") if nex else ("parallel", "arbitrary")),
    )(at, b, *ride)
    return (res[0], list(res[1:])) if nex else res[0]


def _input_grad_block(dproj, x_seq, head_tile, dres, w, w_in_g, psums):
    T = TR + x_seq.shape[0]
    nt = T // TR
    nsl = w_in_g.shape[0]
    wsl = w_in_g.shape[2]
    wcol = wsl // 2
    nex = len(psums)

    def body(*refs):
        dp_ref, x_ref, head_ref, dres_ref, w_ref, win_ref = refs[:6]
        gx_ref, dmeta_ref, dw_ref = refs[6 + nex:9 + nex]
        exchange = _ChipExchange(refs[6:6 + nex], refs[9 + nex:9 + 2 * nex], refs[-2], refs[-1])
        i = pl.program_id(0)

        @pl.when(i == 0)
        def _():
            exchange.send()

        @pl.when(i == nt - 1)
        def _():
            exchange.finish()

        x = jnp.where(i == 0, head_ref[...], x_ref[...])
        r = lax.rsqrt(jnp.mean(x * x, axis=-1, keepdims=True) + EPS)
        n = x * r
        du_v = None
        for s in range(nsl):
            part = _dot_nt(dp_ref[:, s * wsl:(s + 1) * wsl], win_ref[s])
            du_v = part if du_v is None else du_v + part
        dn = du_v * w_ref[...]
        dh = dres_ref[...] + r * (dn - n * jnp.mean(dn * n, axis=-1, keepdims=True))
        gx_ref[...] = dh
        part = jnp.sum(du_v * n, axis=0, keepdims=True)

        @pl.when(i == 0)
        def _():
            dmeta_ref[...] = dh[PAD:TR, :]
            dw_ref[...] = part

        @pl.when(i > 0)
        def _():
            dw_ref[...] += part

    row = pl.BlockSpec((TR, D), lambda i: (i, 0))
    vec = pl.BlockSpec((1, D), lambda i: (0, 0))
    hbm = pl.BlockSpec(memory_space=pl.ANY)
    return pl.pallas_call(
        body, name="input_grad_block", grid=(nt,),
        in_specs=[pl.BlockSpec((TR, 9 * D), lambda i: (i, 0)),
                  pl.BlockSpec((TR, D), lambda i: (jnp.maximum(i - 1, 0), 0)), pl.BlockSpec((TR, D), lambda i: (0, 0)),
                  row, vec, pl.BlockSpec(memory_space=pltpu.VMEM)] + [hbm] * nex,
        out_specs=[pl.BlockSpec((TR, D), lambda i: (jnp.maximum(i - 1, 0), 0)),
                   pl.BlockSpec((NMETA, D), lambda i: (0, 0)), vec] + [hbm] * nex,
        out_shape=[jax.ShapeDtypeStruct((T - TR, D), F32), jax.ShapeDtypeStruct((NMETA, D), F32),
                   jax.ShapeDtypeStruct((1, D), F32)]
        + [jax.ShapeDtypeStruct((3,) + p.shape[1:], p.dtype) for p in psums],
        scratch_shapes=[pltpu.SemaphoreType.DMA((nex, 3)), pltpu.SemaphoreType.DMA((nex, 3))],
        compiler_params=_cparams(("arbitrary",)),
    )(dproj, x_seq, head_tile, dres, w, w_in_g, *psums)


def _tri_matmul(tri_bf16, x):
    hi = x.astype(BF16)
    lo = (x - hi.astype(F32)).astype(BF16)
    return jnp.dot(tri_bf16, lo, preferred_element_type=F32) + jnp.dot(tri_bf16, hi, preferred_element_type=F32)


def _shift_down(x, prev8, n):
    rows = x.shape[0]
    return pltpu.roll(jnp.concatenate([prev8, x], axis=0), n, 0)[8:8 + rows, :]


def _shift_up(x, next8, n):
    rows = x.shape[0]
    return pltpu.roll(jnp.concatenate([x, next8], axis=0), rows + 8 - n, 0)[0:rows, :]


def _gates(f_raw, lb):
    sg = _sigmoid(f_raw)
    f = lb + (1.0 - lb) * sg
    return sg, f, jnp.log(f), 1.0 - f


def _lower_bound(lbp_ref):
    return _sigmoid(lbp_ref[0:1, :] - lbp_ref[1:2, :])


def _tri_masks():
    r = lax.broadcasted_iota(jnp.int32, (CH, CH), 0)
    c = lax.broadcasted_iota(jnp.int32, (CH, CH), 1)
    return r >= c, r <= c


def _ones_where(mask):
    return jnp.where(mask, 1.0, 0.0).astype(BF16)


def _dot(a, b):
    return jnp.dot(a.astype(BF16), b.astype(BF16), preferred_element_type=F32)


def _dot_nt(a, b):
    return lax.dot_general(a.astype(BF16), b.astype(BF16), (((1,), (1,)), ((), ())), preferred_element_type=F32)


def _dot_tn(a, b):
    return lax.dot_general(a.astype(BF16), b.astype(BF16), (((0,), (0,)), ((), ())), preferred_element_type=F32)


def _mix_block_fwd(x_seq, head_tile, lb_param, wattn, whn, conv_w, w_in_g, w_out, late, late_paired):
    T = TR + x_seq.shape[0]
    nt = T // TR
    ncht = TR // CH
    nsl = w_in_g.shape[0]
    wsl = w_in_g.shape[2]
    wcol = wsl // 2

    nlate = len(late)

    def body(*refs):
        x_ref, head_ref, lbp_ref, wa_ref, whn_ref, cw_ref, win_ref, wout_ref = refs[:8]
        ut_ref, projb_ref, o_ref, sst_ref, mt_ref, h1_ref = refs[8 + nlate:14 + nlate]
        late_refs = refs[14 + nlate:14 + 2 * nlate]
        proj_ref, m_ref, st_ref, cxc_ref, send_sems, recv_sems = refs[14 + 2 * nlate:]
        i = pl.program_id(0)
        gather = _Gather(late_refs, send_sems, recv_sems, (False,) * nlate, late_paired)

        @pl.when(i == 0)
        def _():
            st_ref[...] = jnp.zeros_like(st_ref)
            cxc_ref[...] = jnp.zeros_like(cxc_ref)
            gather.send()

        @pl.when(i == nt // 2)
        def _():
            gather.relay()

        @pl.when(i == (3 * nt) // 4)
        def _():
            gather.relay_diagonal()

        x = jnp.where(i == 0, head_ref[...], x_ref[...])
        r1 = lax.rsqrt(jnp.mean(x * x, axis=-1, keepdims=True) + EPS)
        u_f = x * r1 * wa_ref[...]
        u = u_f.astype(BF16)
        ut_ref[...] = u_f.T.astype(BF16)
        for s in range(nsl):
            cols = slice(s * wsl, (s + 1) * wsl)
            p = jnp.dot(u, win_ref[s], preferred_element_type=F32)
            proj_ref[:, cols] = p
            projb_ref[:, cols] = p.astype(BF16)

        lb = _lower_bound(lbp_ref)
        lower, _ = _tri_masks()
        ltri = _ones_where(lower)
        whn_v = whn_ref[...]
        w0, w1, w2 = cw_ref[0:1, :], cw_ref[1:2, :], cw_ref[2:3, :]

        def gates_and_conv(c):
            rows = slice(c * CH, (c + 1) * CH)
            q_raw = proj_ref[rows, 0:D]
            f_raw = proj_ref[rows, D:2 * D]
            q = q_raw * _sigmoid(q_raw)
            _, _, g, k = _gates(f_raw, lb)
            gam = _tri_matmul(ltri, g)
            gam_l = gam[CH - 1:CH, :]
            g_out = proj_ref[rows, 3 * D:4 * D]
            cx = proj_ref[rows, 5 * D:6 * D] * proj_ref[rows, 6 * D:7 * D]
            prev8 = cxc_ref[...]
            cv = w0 * _shift_down(cx, prev8, 2) + w1 * _shift_down(cx, prev8, 1) + w2 * cx
            cxc_ref[...] = cx[CH - 8:CH, :]
            y_b = proj_ref[rows, 4 * D:5 * D] * cv
            return dict(
                e_l=jnp.exp(gam_l), qt=(q * jnp.exp(gam)).astype(BF16), kt=(k * jnp.exp(-gam)).astype(BF16),
                khat=(k * jnp.exp(gam_l - gam)).astype(BF16), vb=proj_ref[rows, 2 * D:3 * D].astype(BF16),
                out_gate=whn_v * (g_out * _sigmoid(g_out)), merge_a=_sigmoid(proj_ref[rows, 7 * D:8 * D]),
                merged_b=_sigmoid(proj_ref[rows, 8 * D:9 * D]) * y_b)

        def recurrence_and_merge(c, pre):
            rows = slice(c * CH, (c + 1) * CH)
            qt, kt, khat, vb, e_l = pre["qt"], pre["kt"], pre["khat"], pre["vb"], pre["e_l"]
            heads = [slice(h * HD, (h + 1) * HD) for h in range(NH)]
            sts = [st_ref[h] for h in range(NH)]
            scores = [_dot_nt(qt[:, cs], kt[:, cs]) for cs in heads]
            o_state = [_dot_nt(qt[:, cs], st) for cs, st in zip(heads, sts)]
            st_new = [_dot_tn(vb[:, cs], khat[:, cs]) for cs in heads]
            on_parts = []
            for h, cs in enumerate(heads):
                sst_ref[c, h] = sts[h]
                st_ref[h] = sts[h] * e_l[:, cs] + st_new[h]
                o_h = o_state[h] + _dot(jnp.where(lower, scores[h], 0.0), vb[:, cs])
                o_ref[rows, cs] = o_h
                ro = lax.rsqrt(jnp.mean(o_h * o_h, axis=-1, keepdims=True) + EPS)
                on_parts.append(o_h * ro)
            on = jnp.concatenate(on_parts, axis=1)
            m_ref[rows, :] = pre["merge_a"] * (on * pre["out_gate"]) + pre["merged_b"]

        pre = {0: gates_and_conv(0)}
        for c in range(ncht):
            if c + 1 < ncht:
                pre[c + 1] = gates_and_conv(c + 1)
            recurrence_and_merge(c, pre.pop(c))
        m_v = m_ref[...]
        h1_ref[...] = x + jnp.dot(m_v.astype(BF16), wout_ref[...], preferred_element_type=F32)
        mt_ref[...] = m_v.T.astype(BF16)

        @pl.when(i == nt - 1)
        def _():
            gather.finish()

    row = lambda w: pl.BlockSpec((TR, w), lambda i: (i, 0))
    col = lambda w: pl.BlockSpec((w, TR), lambda i: (0, i))
    vec = lambda r: pl.BlockSpec((r, D), lambda i: (0, 0))
    vm = pl.BlockSpec(memory_space=pltpu.VMEM)
    hbm = pl.BlockSpec(memory_space=pl.ANY)
    return pl.pallas_call(
        body, name="mix_block_fwd", grid=(nt,),
        in_specs=[pl.BlockSpec((TR, D), lambda i: (jnp.maximum(i - 1, 0), 0)), pl.BlockSpec((TR, D), lambda i: (0, 0)),
                  vec(2), vec(1), vec(1), vec(3), vm, vm] + [hbm] * nlate,
        out_specs=[col(D), row(9 * D), row(D), pl.BlockSpec((ncht, NH, HD, HD), lambda i: (i, 0, 0, 0)),
                   col(D), row(D)] + [hbm] * nlate,
        out_shape=[jax.ShapeDtypeStruct((D, T), BF16), jax.ShapeDtypeStruct((T, 9 * D), BF16),
                   jax.ShapeDtypeStruct((T, D), F32), jax.ShapeDtypeStruct((T // CH, NH, HD, HD), F32),
                   jax.ShapeDtypeStruct((D, T), BF16), jax.ShapeDtypeStruct((T, D), F32)]
        + [jax.ShapeDtypeStruct(a.shape, a.dtype) for a in late],
        input_output_aliases={8 + n: 6 + n for n in range(nlate)},
        scratch_shapes=[pltpu.VMEM((TR, 9 * D), F32), pltpu.VMEM((TR, D), F32), pltpu.VMEM((NH, HD, HD), F32),
                        pltpu.VMEM((8, D), F32), pltpu.SemaphoreType.DMA((nlate, 8)),
                        pltpu.SemaphoreType.DMA((nlate, 8))],
        compiler_params=_cparams(("arbitrary",)),
    )(x_seq, head_tile, lb_param, wattn, whn, conv_w, w_in_g, w_out, *late)


def _mix_block_bwd(dh1, projb, o, sst, lb_param, whn, conv_w, w_out, psums):
    T = projb.shape[0]
    nt = T // TR
    ncht = TR // CH
    tb16 = TR // 16
    nex = len(psums)

    def body(*refs):
        dh1_ref, proj_ref, pc_ref, px_ref, o_ref, sst_ref, lbp_ref, whn_ref, cw_ref, wout_ref = refs[:10]
        dproj_ref, dlb_ref, dwhn_ref, dcw_ref = refs[10 + nex:14 + nex]
        exchange = _ChipExchange(refs[10:10 + nex], refs[14 + nex:14 + 2 * nex], refs[-2], refs[-1])
        dm_ref, dst_ref, dcvc_ref, acc_lb, acc_hn, acc_cw = refs[14 + 2 * nex:-2]
        s = pl.program_id(0)
        tile = nt - 1 - s

        @pl.when(s == 0)
        def _():
            dst_ref[...] = jnp.zeros_like(dst_ref)
            dcvc_ref[...] = jnp.zeros_like(dcvc_ref)
            acc_lb[...] = jnp.zeros_like(acc_lb)
            acc_hn[...] = jnp.zeros_like(acc_hn)
            acc_cw[...] = jnp.zeros_like(acc_cw)
            exchange.send()

        dm_ref[...] = _dot_nt(dh1_ref[...], wout_ref[...])
        lb = _lower_bound(lbp_ref)
        lower, upper = _tri_masks()
        ltri = _ones_where(lower)
        utri = _ones_where(upper)
        whn_v = whn_ref[...]
        w0, w1, w2 = cw_ref[0:1, :], cw_ref[1:2, :], cw_ref[2:3, :]
        cx_before_tile = jnp.where(tile > 0, (pc_ref[...].astype(F32) * px_ref[...].astype(F32))[8:16, :], 0.0)
        rid = lax.broadcasted_iota(jnp.int32, (CH, D), 0)

        def chunk(cc, carry):
            c = ncht - 1 - cc
            r0 = pl.multiple_of(c * CH, CH)
            rows = pl.ds(r0, CH)
            slab = lambda n: proj_ref[rows, n * D:(n + 1) * D].astype(F32)
            q_raw, f_raw, v, g_out, b_gate, c_gate, x_conv = (slab(n) for n in range(7))
            sa = _sigmoid(slab(7))
            sb = _sigmoid(slab(8))
            dm_v = dm_ref[rows, :]

            sq = _sigmoid(q_raw)
            q = q_raw * sq
            sg, f, g, k = _gates(f_raw, lb)
            gam = _tri_matmul(ltri, g)
            gam_l = gam[CH - 1:CH, :]
            e_l = jnp.exp(gam_l)
            e_g = jnp.exp(gam)
            e_ng = jnp.exp(-gam)
            e_kl = jnp.exp(gam_l - gam)
            qt = q * e_g
            kt = k * e_ng
            khat = k * e_kl
            qt_b, kt_b, khat_b, v_b = qt.astype(BF16), kt.astype(BF16), khat.astype(BF16), v.astype(BF16)
            qt_seen, kt_seen = qt_b.astype(F32), kt_b.astype(F32)
            s_go = _sigmoid(g_out)
            silu_go = g_out * s_go
            cx = c_gate * x_conv
            rprev = pl.ds(pl.multiple_of(jnp.maximum(r0 - 16, 0), 16), 16)
            cx_prev_in = (proj_ref[rprev, 5 * D:6 * D].astype(F32) * proj_ref[rprev, 6 * D:7 * D].astype(F32))[8:16, :]
            prev8 = jnp.where(c > 0, cx_prev_in, cx_before_tile)
            cx_m1 = _shift_down(cx, prev8, 1)
            cx_m2 = _shift_down(cx, prev8, 2)
            cv = w0 * cx_m2 + w1 * cx_m1 + w2 * cx
            y_b = b_gate * cv

            o_v = o_ref[rows, :]
            ro_parts, on_parts = [], []
            for h in range(NH):
                cs = slice(h * HD, (h + 1) * HD)
                o_h = o_v[:, cs]
                ro = lax.rsqrt(jnp.mean(o_h * o_h, axis=-1, keepdims=True) + EPS)
                ro_parts.append(ro)
                on_parts.append(o_h * ro)
            on = jnp.concatenate(on_parts, axis=1)
            out_gate = whn_v * silu_go
            y_a = on * out_gate

            dy_a = dm_v * sa
            dy_b = dm_v * sb
            dproj_ref[rows, 7 * D:8 * D] = (dy_a * y_a * (1.0 - sa)).astype(BF16)
            dproj_ref[rows, 8 * D:9 * D] = (dy_b * y_b * (1.0 - sb)).astype(BF16)
            dproj_ref[rows, 4 * D:5 * D] = (dy_b * cv).astype(BF16)
            dcv = dy_b * b_gate
            acc_cw[0:1, :] += jnp.sum(dcv * cx_m2, axis=0, keepdims=True)
            acc_cw[1:2, :] += jnp.sum(dcv * cx_m1, axis=0, keepdims=True)
            acc_cw[2:3, :] += jnp.sum(dcv * cx, axis=0, keepdims=True)
            next8 = dcvc_ref[...]
            dcx = w2 * dcv + w1 * _shift_up(dcv, next8, 1) + w0 * _shift_up(dcv, next8, 2)
            dcvc_ref[...] = dcv[0:8, :]
            dproj_ref[rows, 5 * D:6 * D] = (dcx * x_conv).astype(BF16)
            dproj_ref[rows, 6 * D:7 * D] = (dcx * c_gate).astype(BF16)
            don = dy_a * out_gate
            dya_on = dy_a * on
            dproj_ref[rows, 3 * D:4 * D] = (dya_on * (whn_v * (s_go * (1.0 + g_out * (1.0 - s_go))))).astype(BF16)
            acc_hn[...] += jnp.sum(dya_on * silu_go, axis=0, keepdims=True)

            heads = [slice(h * HD, (h + 1) * HD) for h in range(NH)]
            do_bs, sts, dstns, first = [], [], [], []
            for h, cs in enumerate(heads):
                on_h = on_parts[h]
                don_h = don[:, cs]
                do_h = ro_parts[h] * (don_h - on_h * jnp.mean(don_h * on_h, axis=-1, keepdims=True))
                do_bs.append(do_h.astype(BF16))
                sts.append(sst_ref[c, h])
                dstns.append(dst_ref[h])
            for h, cs in enumerate(heads):
                qt_h, kt_h, khat_h, v_h = qt_b[:, cs], kt_b[:, cs], khat_b[:, cs], v_b[:, cs]
                do_b, dstn_b = do_bs[h], dstns[h].astype(BF16)
                first.append(dict(
                    a_t=_dot_nt(kt_h, qt_h), da=_dot_nt(do_b, v_h), da_t=_dot_nt(v_h, do_b),
                    dv_state=_dot_nt(khat_h, dstn_b), dqt_state=_dot(do_b, sts[h]), dkhat=_dot(v_h, dstn_b),
                    dst_chunk=_dot_tn(do_b, qt_h)))
            dq_parts, dk_parts, dv_parts, dgam_parts, ext_parts = [], [], [], [], []
            for h, cs in enumerate(heads):
                qt_h, kt_h = qt_b[:, cs], kt_b[:, cs]
                p1, do_b, st, dstn = first[h], do_bs[h], sts[h], dstns[h]
                dv_h = _dot(jnp.where(upper, p1["a_t"], 0.0), do_b) + p1["dv_state"]
                dqt_state = p1["dqt_state"]
                dqt_chunk = _dot(jnp.where(lower, p1["da"], 0.0), kt_h)
                dkt = _dot(jnp.where(upper, p1["da_t"], 0.0), qt_h)
                dkhat = p1["dkhat"]
                dq_h = (dqt_state + dqt_chunk) * e_g[:, cs]
                dk_h = dkt * e_ng[:, cs] + dkhat * e_kl[:, cs]
                khat_dkhat = dkhat * khat[:, cs]
                ext = (jnp.sum(khat_dkhat, axis=0, keepdims=True)
                       + e_l[:, cs] * jnp.sum(st * dstn, axis=0, keepdims=True))
                dst_ref[h] = p1["dst_chunk"] + dstn * e_l[:, cs]
                dq_parts.append(dq_h)
                dk_parts.append(dk_h)
                dv_parts.append(dv_h)
                dgam_parts.append(qt[:, cs] * dqt_state + qt_seen[:, cs] * dqt_chunk - kt_seen[:, cs] * dkt
                                  - khat_dkhat)
                ext_parts.append(ext)
            dq = jnp.concatenate(dq_parts, axis=1)
            dk = jnp.concatenate(dk_parts, axis=1)
            dgam = jnp.concatenate(dgam_parts, axis=1)
            ext = jnp.concatenate(ext_parts, axis=1)
            dgam = dgam + jnp.where(rid == CH - 1, ext, 0.0)
            dg = _tri_matmul(utri, dgam)
            dproj_ref[rows, 0:D] = (dq * (sq * (1.0 + q_raw * (1.0 - sq)))).astype(BF16)
            df = dg * jnp.exp(-g) - dk
            dproj_ref[rows, D:2 * D] = (df * (1.0 - lb) * sg * (1.0 - sg)).astype(BF16)
            dproj_ref[rows, 2 * D:3 * D] = jnp.concatenate(dv_parts, axis=1).astype(BF16)
            real = (tile * TR + r0 + rid) >= PAD
            acc_lb[...] += jnp.sum(jnp.where(real, df * (1.0 - sg), 0.0), axis=0, keepdims=True)
            return carry

        lax.fori_loop(0, ncht, chunk, 0)

        @pl.when(s == nt - 1)
        def _():
            dlb_ref[...] = acc_lb[...] * lb * (1.0 - lb)
            hn = acc_hn[...]
            tot = hn[:, 0:HD]
            for h in range(1, NH):
                tot = tot + hn[:, h * HD:(h + 1) * HD]
            dwhn_ref[...] = tot
            dcw_ref[...] = acc_cw[0:3, :]
            exchange.finish()

    hbm = pl.BlockSpec(memory_space=pl.ANY)
    rev = lambda s: (nt - 1 - s, 0)
    prevc = lambda s: (jnp.maximum((nt - 1 - s) * tb16 - 1, 0), 5)
    prevx = lambda s: (jnp.maximum((nt - 1 - s) * tb16 - 1, 0), 6)
    const = lambda s: (0, 0)
    return pl.pallas_call(
        body, name="mix_block_bwd", grid=(nt,),
        in_specs=[pl.BlockSpec((TR, D), rev),
                  pl.BlockSpec((TR, 9 * D), rev),
                  pl.BlockSpec((16, D), prevc),
                  pl.BlockSpec((16, D), prevx),
                  pl.BlockSpec((TR, D), rev),
                  pl.BlockSpec((ncht, NH, HD, HD), lambda s: (nt - 1 - s, 0, 0, 0)),
                  pl.BlockSpec((2, D), const),
                  pl.BlockSpec((1, D), const),
                  pl.BlockSpec((3, D), const),
                  pl.BlockSpec(memory_space=pltpu.VMEM)] + [hbm] * nex,
        out_specs=[pl.BlockSpec((TR, 9 * D), rev),
                   pl.BlockSpec((1, D), const),
                   pl.BlockSpec((1, HD), const),
                   pl.BlockSpec((3, D), const)] + [hbm] * nex,
        out_shape=[jax.ShapeDtypeStruct((T, 9 * D), BF16), jax.ShapeDtypeStruct((1, D), F32),
                   jax.ShapeDtypeStruct((1, HD), F32), jax.ShapeDtypeStruct((3, D), F32)]
        + [jax.ShapeDtypeStruct((3,) + p.shape[1:], p.dtype) for p in psums],
        scratch_shapes=[pltpu.VMEM((TR, D), F32), pltpu.VMEM((NH, HD, HD), F32), pltpu.VMEM((8, D), F32),
                        pltpu.VMEM((1, D), F32), pltpu.VMEM((1, D), F32), pltpu.VMEM((8, D), F32),
                        pltpu.SemaphoreType.DMA((nex, 3)), pltpu.SemaphoreType.DMA((nex, 3))],
        compiler_params=_cparams(("arbitrary",)),
    )(dh1, projb, projb, projb, o, sst, lb_param, whn, conv_w, w_out, *psums)


def _ffn_block_fwd(h1, tgt, wffn, w_up_g, fcw, fcb, w_down, wfin):
    T = h1.shape[0]
    nt = T // TR
    nsl = w_up_g.shape[0]
    wsl = w_up_g.shape[2]

    def body(h_ref, t_ref, wn_ref, wup_ref, cw_ref, cb_ref, wdn_ref, wf_ref,
             u2t_ref, upb_ref, ggt_ref, dh_ref, loss_ref, dwf_ref, up_scr, gg_ref, carry_ref):
        i = pl.program_id(0)

        @pl.when(i == 0)
        def _():
            carry_ref[...] = jnp.zeros_like(carry_ref)
            loss_ref[...] = jnp.zeros_like(loss_ref)
            dwf_ref[...] = jnp.zeros_like(dwf_ref)

        x = h_ref[...]
        r2 = lax.rsqrt(jnp.mean(x * x, axis=-1, keepdims=True) + EPS)
        u2_f = x * r2 * wn_ref[...]
        u2 = u2_f.astype(BF16)
        u2t_ref[...] = u2_f.T.astype(BF16)
        for s in range(nsl):
            up_s = jnp.dot(u2, wup_ref[s], preferred_element_type=F32)
            up_scr[:, s * wsl:(s + 1) * wsl] = up_s
            upb_ref[:, s * wsl:(s + 1) * wsl] = up_s.astype(BF16)
        w0, w1, w2 = cw_ref[0:1, :], cw_ref[1:2, :], cw_ref[2:3, :]

        def chunk(c, carry):
            rows = pl.ds(pl.multiple_of(c * CH, CH), CH)
            a_pre = up_scr[rows, 0:DFF]
            val = up_scr[rows, DFF:2 * DFF]
            prev8 = carry_ref[...]
            a = w0 * _shift_down(a_pre, prev8, 2) + w1 * _shift_down(a_pre, prev8, 1) + w2 * a_pre + cb_ref[...]
            carry_ref[...] = a_pre[CH - 8:CH, :]
            gg_ref[rows, :] = a * _sigmoid(a) * val
            return carry

        lax.fori_loop(0, TR // CH, chunk, 0, unroll=True)
        gg_v = gg_ref[...]
        ggt_ref[...] = gg_v.T.astype(BF16)
        h2 = x + jnp.dot(gg_v.astype(BF16), wdn_ref[...], preferred_element_type=F32)
        r3 = lax.rsqrt(jnp.mean(h2 * h2, axis=-1, keepdims=True) + EPS)
        n3 = h2 * r3
        wf = wf_ref[...]
        diff = jnp.where(i > 0, n3 * wf - t_ref[...], 0.0)
        loss_ref[...] += jnp.sum(diff * diff, axis=0, keepdims=True) * (0.5 / D)
        dy = diff * (1.0 / D)
        dwf_ref[...] += jnp.sum(dy * n3, axis=0, keepdims=True)
        dn = dy * wf
        dh_ref[...] = r3 * (dn - n3 * jnp.mean(dn * n3, axis=-1, keepdims=True))

    row = lambda w: pl.BlockSpec((TR, w), lambda i: (i, 0))
    col = lambda w: pl.BlockSpec((w, TR), lambda i: (0, i))
    vec = lambda w, r=1: pl.BlockSpec((r, w), lambda i: (0, 0))
    vm = pl.BlockSpec(memory_space=pltpu.VMEM)
    return pl.pallas_call(
        body, name="ffn_block_fwd", grid=(nt,),
        in_specs=[row(D), pl.BlockSpec((TR, D), lambda i: (jnp.maximum(i - 1, 0), 0)), vec(D), vm,
                  vec(DFF, 3), vec(DFF), vm, vec(D)],
        out_specs=[col(D), row(2 * DFF), col(DFF), row(D), vec(D), vec(D)],
        out_shape=[jax.ShapeDtypeStruct((D, T), BF16), jax.ShapeDtypeStruct((T, 2 * DFF), BF16),
                   jax.ShapeDtypeStruct((DFF, T), BF16), jax.ShapeDtypeStruct((T, D), F32),
                   jax.ShapeDtypeStruct((1, D), F32), jax.ShapeDtypeStruct((1, D), F32)],
        scratch_shapes=[pltpu.VMEM((TR, 2 * DFF), F32), pltpu.VMEM((TR, DFF), F32), pltpu.VMEM((8, DFF), F32)],
        compiler_params=_cparams(("arbitrary",)),
    )(h1, tgt, wffn, w_up_g, fcw, fcb, w_down, wfin)


def _ffn_block_bwd(dh2, upb, h1, wffn, w_up_g, fcw, fcb, w_down):
    T = h1.shape[0]
    nt = T // TR
    ncht = TR // CH
    nsl = w_up_g.shape[0]
    wsl = w_up_g.shape[2]
    tb16 = TR // 16
    assert ncht % nsl == 0
    every = ncht // nsl
    step = -(-DFF // (ncht * 128)) * 128
    parts = [(c0, min(c0 + step, DFF)) for c0 in range(0, DFF, step)]
    assert len(parts) == ncht

    def body(dh2n_ref, dh2p_ref, up_ref, pa_ref, h_ref, wn_ref, wup_ref, cw_ref, cb_ref, wdn_ref,
             dup_ref, dh1_ref, dfw_ref, dfb_ref, dwn_ref,
             dgg_ring, dup_ring, carry_ref, acc_w, acc_b, acc_n):
        s = pl.program_id(0)
        slot = lax.rem(s, 2)
        other = 1 - slot

        @pl.when(s == 0)
        def _():
            carry_ref[...] = jnp.zeros_like(carry_ref)
            acc_w[...] = jnp.zeros_like(acc_w)
            acc_b[...] = jnp.zeros_like(acc_b)
            acc_n[...] = jnp.zeros_like(acc_n)
            dup_ring[1] = jnp.zeros((TR, 2 * DFF), BF16)
            dgg_ring[0] = _dot_nt(dh2p_ref[...], wdn_ref[...])

        def norm_bwd(du2, valid):
            x = h_ref[...]
            dh2p = dh2p_ref[...]
            r2 = lax.rsqrt(jnp.mean(x * x, axis=-1, keepdims=True) + EPS)
            n2 = x * r2
            dn = du2 * wn_ref[...]
            dh1_ref[...] = dh2p + r2 * (dn - n2 * jnp.mean(dn * n2, axis=-1, keepdims=True))
            acc_n[...] += jnp.where(valid, jnp.sum(du2 * n2, axis=0, keepdims=True), 0.0)

        @pl.when(s < nt)
        def _():
            tile = nt - 1 - s
            w0, w1, w2 = cw_ref[0:1, :], cw_ref[1:2, :], cw_ref[2:3, :]
            a_before_tile = jnp.where(tile > 0, pa_ref[...].astype(F32)[8:16, :], 0.0)
            dh2n = dh2n_ref[...].astype(BF16)
            du2 = None
            for idx in range(ncht):
                c = ncht - 1 - idx
                r0 = c * CH
                rows = slice(r0, r0 + CH)
                a_pre = up_ref[rows, 0:DFF].astype(F32)
                val = up_ref[rows, DFF:2 * DFF].astype(F32)
                prev8 = up_ref[r0 - 16:r0, 0:DFF].astype(F32)[8:16, :] if c > 0 else a_before_tile
                a_m1 = _shift_down(a_pre, prev8, 1)
                a_m2 = _shift_down(a_pre, prev8, 2)
                a = w0 * a_m2 + w1 * a_m1 + w2 * a_pre + cb_ref[...]
                sig = _sigmoid(a)
                dgg_v = dgg_ring[slot, rows, :]
                da = dgg_v * val * (sig * (1.0 + a * (1.0 - sig)))
                dval = (dgg_v * (a * sig)).astype(BF16)
                next8 = carry_ref[...]
                da_pre = (w2 * da + w1 * _shift_up(da, next8, 1) + w0 * _shift_up(da, next8, 2)).astype(BF16)
                carry_ref[...] = da[0:8, :]
                dup_ref[rows, 0:DFF] = da_pre
                dup_ref[rows, DFF:2 * DFF] = dval
                dup_ring[slot, rows, 0:DFF] = da_pre
                dup_ring[slot, rows, DFF:2 * DFF] = dval
                acc_w[0:1, :] += jnp.sum(da * a_m2, axis=0, keepdims=True)
                acc_w[1:2, :] += jnp.sum(da * a_m1, axis=0, keepdims=True)
                acc_w[2:3, :] += jnp.sum(da * a_pre, axis=0, keepdims=True)
                acc_b[...] += jnp.sum(da, axis=0, keepdims=True)
                c0, c1 = parts[idx]
                dgg_ring[other, :, c0:c1] = _dot_nt(dh2n, wdn_ref[c0:c1, :])
                if idx % every == 0:
                    sl = idx // every
                    part = _dot_nt(dup_ring[other, :, sl * wsl:(sl + 1) * wsl], wup_ref[sl])
                    du2 = part if du2 is None else du2 + part
            norm_bwd(du2, s > 0)

        @pl.when(s == nt)
        def _():
            du2 = _dot_nt(dup_ring[other, :, 0:wsl], wup_ref[0])
            for sl in range(1, nsl):
                du2 = du2 + _dot_nt(dup_ring[other, :, sl * wsl:(sl + 1) * wsl], wup_ref[sl])
            norm_bwd(du2, True)
            dfw_ref[...] = acc_w[0:3, :]
            dfb_ref[...] = acc_b[...]
            dwn_ref[...] = acc_n[...]

    gate_tile = lambda s: jnp.maximum(nt - 1 - s, 0)
    next_tile = lambda s: jnp.maximum(nt - 2 - s, 0)
    prev_tile = lambda s: jnp.minimum(nt - s, nt - 1)
    vec = lambda w, r=1: pl.BlockSpec((r, w), lambda s: (0, 0))
    vm = pl.BlockSpec(memory_space=pltpu.VMEM)
    return pl.pallas_call(
        body, name="ffn_block_bwd", grid=(nt + 1,),
        in_specs=[pl.BlockSpec((TR, D), lambda s: (next_tile(s), 0)),
                  pl.BlockSpec((TR, D), lambda s: (prev_tile(s), 0)),
                  pl.BlockSpec((TR, 2 * DFF), lambda s: (gate_tile(s), 0)),
                  pl.BlockSpec((16, DFF), lambda s: (jnp.maximum(gate_tile(s) * tb16 - 1, 0), 0)),
                  pl.BlockSpec((TR, D), lambda s: (prev_tile(s), 0)),
                  vec(D), vm, vec(DFF, 3), vec(DFF), vm],
        out_specs=[pl.BlockSpec((TR, 2 * DFF), lambda s: (gate_tile(s), 0)),
                   pl.BlockSpec((TR, D), lambda s: (prev_tile(s), 0)),
                   vec(DFF, 3), vec(DFF), vec(D)],
        out_shape=[jax.ShapeDtypeStruct((T, 2 * DFF), BF16), jax.ShapeDtypeStruct((T, D), F32),
                   jax.ShapeDtypeStruct((3, DFF), F32), jax.ShapeDtypeStruct((1, DFF), F32),
                   jax.ShapeDtypeStruct((1, D), F32)],
        scratch_shapes=[pltpu.VMEM((2, TR, DFF), F32), pltpu.VMEM((2, TR, 2 * DFF), BF16),
                        pltpu.VMEM((8, DFF), F32), pltpu.VMEM((8, DFF), F32), pltpu.VMEM((1, DFF), F32),
                        pltpu.VMEM((1, D), F32)],
        compiler_params=_cparams(("arbitrary",)),
    )(dh2, dh2, upb, upb, h1, wffn, w_up_g, fcw, fcb, w_down)


def _place():
    x, y, c = lax.axis_index("x"), lax.axis_index("y"), lax.axis_index("c")
    return x, y, c


_CHIP_FLIPS = ((1, 0), (0, 1), (1, 1))


def _flip(v, bit):
    return 1 - v if bit else v


def _into_slot(w, j_idx, *, rb, dtype, name, paired=False):
    r, cdim = w.shape

    def body(j_ref, w_ref, out_ref):
        del j_ref
        out_ref[...] = w_ref[...].astype(dtype)

    if paired:
        out_shape = jax.ShapeDtypeStruct((NSHARD // 2, r, 2 * cdim), dtype)
        out_spec = pl.BlockSpec((None, rb, cdim), lambda i, j_ref: (j_ref[0] // 2, i, j_ref[0] % 2))
    else:
        out_shape = jax.ShapeDtypeStruct((NSHARD, r, cdim), dtype)
        out_spec = pl.BlockSpec((None, rb, cdim), lambda i, j_ref: (j_ref[0], i, 0))
    grid_spec = pltpu.PrefetchScalarGridSpec(
        num_scalar_prefetch=1, grid=(r // rb,),
        in_specs=[pl.BlockSpec((rb, cdim), lambda i, j_ref: (i, 0))], out_specs=out_spec)
    return pl.pallas_call(
        body, name=name, grid_spec=grid_spec, out_shape=out_shape, compiler_params=_cparams(("parallel",)),
    )(j_idx, w)


class _Gather:
    def __init__(self, outs, send_sems, recv_sems, whole, paired=None):
        self.outs, self.send_sems, self.recv_sems, self.whole = outs, send_sems, recv_sems, whole
        self.paired = paired if paired is not None else (False,) * len(outs)
        self.x, self.y, self.c = _place()
        self.j = 2 * self.x + self.y
        self.sibling = (self.x, self.y, 1 - self.c)
        chips = [(_flip(self.x, fx), _flip(self.y, fy)) for fx, fy in _CHIP_FLIPS]
        self.slots = [2 * px + py for px, py in chips]
        self.peers = [(px, py, self.c) for px, py in chips]

    def _copy(self, w, slot, core, sem, to, quarter=None):
        r = self.outs[w].shape[1]
        if self.whole[w]:
            rows = pl.ds(0, r)
        elif quarter is None:
            rows = pl.ds(core * (r // 2), r // 2)
        else:
            rows = pl.ds(core * (r // 2) + quarter * (r // 4), r // 4)
        if self.paired[w]:
            cw = self.outs[w].shape[2] // 2
            piece = self.outs[w].at[slot // 2, rows, pl.ds((slot % 2) * cw, cw)]
        else:
            piece = self.outs[w].at[slot, rows, :]
        return pltpu.make_async_remote_copy(
            src_ref=piece, dst_ref=piece, send_sem=self.send_sems.at[w, sem], recv_sem=self.recv_sems.at[w, sem],
            device_id=to, device_id_type=MESH)

    def _direct(self, w):
        return (0, 1, 2) if self.whole[w] else (0, 1)

    def send(self):
        for w in range(len(self.outs)):
            for k in self._direct(w):
                self._copy(w, self.j, self.c, k, self.peers[k]).start()

    def relay(self):
        jx, jy, _ = self.slots
        for w in range(len(self.outs)):
            if self.whole[w]:
                for k in range(3):
                    self._copy(w, self.slots[k], self.c, k, self.sibling).wait_recv()
                continue
            self._copy(w, jx, self.c, 0, self.sibling).wait_recv()
            self._copy(w, jx, self.c, 6, self.peers[1], quarter=0).start()
            self._copy(w, jx, self.c, 3, self.sibling).start()
            self._copy(w, jy, self.c, 1, self.sibling).wait_recv()
            self._copy(w, jy, self.c, 7, self.peers[0], quarter=1).start()
            self._copy(w, jy, self.c, 4, self.sibling).start()

    def relay_diagonal(self):
        jd = self.slots[2]
        for w in range(len(self.outs)):
            if not self.whole[w]:
                self._copy(w, jd, self.c, 6, self.sibling, quarter=0).wait_recv()
                self._copy(w, jd, self.c, 7, self.sibling, quarter=1).wait_recv()
                self._copy(w, jd, self.c, 5, self.sibling).start()

    def finish(self):
        jx, jy, jd = self.slots
        for w in range(len(self.outs)):
            for k in self._direct(w):
                self._copy(w, self.j, self.c, k, self.peers[k]).wait_send()
            if self.whole[w]:
                continue
            self._copy(w, jx, self.c, 6, self.peers[1], quarter=0).wait_send()
            self._copy(w, jy, self.c, 7, self.peers[0], quarter=1).wait_send()
            for k, slot in enumerate(self.slots):
                self._copy(w, slot, 1 - self.c, 3 + k, self.sibling).wait_recv()
                self._copy(w, slot, self.c, 3 + k, self.sibling).wait_send()


def _allgather_weights(slotted, whole):
    n = len(slotted)

    def body(*refs):
        g = _Gather(refs[n:2 * n], refs[2 * n], refs[2 * n + 1], whole)
        g.send()
        g.relay()
        g.relay_diagonal()
        g.finish()

    any_spec = pl.BlockSpec(memory_space=pl.ANY)
    return pl.pallas_call(
        body, name="allgather_weights",
        in_specs=[any_spec] * n, out_specs=[any_spec] * n,
        out_shape=[jax.ShapeDtypeStruct(a.shape, a.dtype) for a in slotted],
        input_output_aliases={i: i for i in range(n)},
        scratch_shapes=[pltpu.SemaphoreType.DMA((n, 8)), pltpu.SemaphoreType.DMA((n, 8))],
    )(*slotted)


class _ChipExchange:
    def __init__(self, ins, outs, send_sems, recv_sems):
        self.ins, self.outs, self.send_sems, self.recv_sems = ins, outs, send_sems, recv_sems
        self.x, self.y, self.c = _place()

    def _copies(self):
        for w in range(len(self.ins)):
            for kk, (fx, fy) in enumerate(_CHIP_FLIPS):
                px, py = _flip(self.x, fx), _flip(self.y, fy)
                yield pltpu.make_async_remote_copy(
                    src_ref=self.ins[w].at[2 * px + py], dst_ref=self.outs[w].at[kk],
                    send_sem=self.send_sems.at[w, kk], recv_sem=self.recv_sems.at[w, kk],
                    device_id=(px, py, self.c), device_id_type=MESH)

    def send(self):
        for cp in self._copies():
            cp.start()

    def finish(self):
        for cp in self._copies():
            cp.wait()


class _PairExchange:
    def __init__(self, ins, outs, send_sems, recv_sems):
        self.ins, self.outs, self.send_sems, self.recv_sems = ins, outs, send_sems, recv_sems
        self.x, self.y, self.c = _place()

    def _copies(self):
        for w in range(len(self.ins)):
            half = self.ins[w].shape[1] // 2
            yield pltpu.make_async_remote_copy(
                src_ref=self.ins[w].at[:, pl.ds((1 - self.c) * half, half), :], dst_ref=self.outs[w],
                send_sem=self.send_sems.at[w], recv_sem=self.recv_sems.at[w],
                device_id=(self.x, self.y, 1 - self.c), device_id_type=MESH)

    def send(self):
        for cp in self._copies():
            cp.start()

    def finish(self):
        for cp in self._copies():
            cp.wait()


def _pair_shapes(grads):
    return [jax.ShapeDtypeStruct((g.shape[0], g.shape[1] // 2, g.shape[2]), g.dtype) for g in grads]


def _pair_exchange(grads, name):
    nw = len(grads)

    def body(*refs):
        ex = _PairExchange(refs[:nw], refs[nw:2 * nw], refs[2 * nw], refs[2 * nw + 1])
        ex.send()
        ex.finish()

    any_spec = pl.BlockSpec(memory_space=pl.ANY)
    return pl.pallas_call(
        body, name=name,
        in_specs=[any_spec] * nw, out_specs=[any_spec] * nw, out_shape=_pair_shapes(grads),
        scratch_shapes=[pltpu.SemaphoreType.DMA((nw,)), pltpu.SemaphoreType.DMA((nw,))],
    )(*grads)


def _pair_add(g, other, c_idx, *, rb, name):
    S, r, cdim = g.shape
    half = r // 2
    nb = half // rb

    def body(c_ref, g_ref, o_ref, out_ref):
        del c_ref
        out_ref[...] = (g_ref[...].astype(F32) + o_ref[...].astype(F32)).astype(BF16)

    grid_spec = pltpu.PrefetchScalarGridSpec(
        num_scalar_prefetch=1, grid=(S, nb),
        in_specs=[pl.BlockSpec((None, rb, cdim), lambda s, i, c_ref: (s, c_ref[0] * nb + i, 0)),
                  pl.BlockSpec((None, rb, cdim), lambda s, i, c_ref: (s, i, 0))],
        out_specs=pl.BlockSpec((None, rb, cdim), lambda s, i, c_ref: (s, i, 0)))
    return pl.pallas_call(
        body, name=name, grid_spec=grid_spec, out_shape=jax.ShapeDtypeStruct((S, half, cdim), BF16),
        compiler_params=_cparams(("parallel", "parallel")),
    )(c_idx, g, other)


def _chip_sum(psum, parts, cj_idx, *, rb, name):
    S, half, cdim = psum.shape
    nb = half // rb

    def body(cj_ref, own_ref, p_ref, out_ref):
        del cj_ref
        f = lambda v: v.astype(F32)
        out_ref[...] = ((f(own_ref[...]) + f(p_ref[0])) + f(p_ref[1])) + f(p_ref[2])

    grid_spec = pltpu.PrefetchScalarGridSpec(
        num_scalar_prefetch=1, grid=(nb,),
        in_specs=[pl.BlockSpec((None, rb, cdim), lambda i, cj: (cj[1], i, 0)),
                  pl.BlockSpec((3, rb, cdim), lambda i, cj: (0, i, 0))],
        out_specs=pl.BlockSpec((rb, cdim), lambda i, cj: (cj[0] * nb + i, 0)))
    return pl.pallas_call(
        body, name=name, grid_spec=grid_spec, out_shape=jax.ShapeDtypeStruct((2 * half, cdim), F32),
        compiler_params=_cparams(("parallel",)),
    )(cj_idx, psum, parts)


SLAB_W = 1024


def _final_exchange(pieces, grads):
    n, nw = len(pieces), len(grads)
    segs, at = [], 0
    for idx, p in enumerate(pieces):
        r, wd = p.shape
        for c0 in range(0, wd, SLAB_W):
            if r > 1:
                at = -(-at // 8) * 8
            segs.append((idx, c0, min(SLAB_W, wd - c0), at))
            at += r
    rows = -(-at // 8) * 8
    flips = [(fx, fy, fc) for fx in (0, 1) for fy in (0, 1) for fc in (0, 1)][1:]

    def body(*refs):
        ins = refs[:n]
        outs = refs[n + nw:2 * n + nw]
        g_refs = refs[2 * n + nw:2 * n + 2 * nw]
        mine_ref, slots_ref, send_sems, recv_sems, gsend_sems, grecv_sems = refs[2 * n + 2 * nw:]
        x, y, c = _place()
        me = 4 * x + 2 * y + c
        sibling = (x, y, 1 - c)

        def swap(w, core):
            half = g_refs[w].shape[0] // 2
            rows_ = g_refs[w].at[pl.ds(core * half, half), :]
            return pltpu.make_async_remote_copy(
                src_ref=rows_, dst_ref=rows_, send_sem=gsend_sems.at[w], recv_sem=grecv_sems.at[w],
                device_id=sibling, device_id_type=MESH)

        for w in range(nw):
            swap(w, c).start()
        mine_ref[...] = jnp.zeros_like(mine_ref)
        for idx, c0, wd, st in segs:
            r = ins[idx].shape[0]
            mine_ref[st:st + r, 0:wd] = ins[idx][:, c0:c0 + wd]
        slots_ref[me] = mine_ref[...]
        cps = []
        for kk, (fx, fy, fc) in enumerate(flips):
            cp = pltpu.make_async_remote_copy(
                src_ref=mine_ref, dst_ref=slots_ref.at[me], send_sem=send_sems.at[kk], recv_sem=recv_sems.at[kk],
                device_id=(_flip(x, fx), _flip(y, fy), _flip(c, fc)), device_id_type=MESH)
            cp.start()
            cps.append(cp)
        for cp in cps:
            cp.wait()
        tot = slots_ref[0]
        for d in range(1, 8):
            tot = tot + slots_ref[d]
        mine_ref[...] = tot
        for idx, c0, wd, st in segs:
            r = ins[idx].shape[0]
            val = mine_ref[st:st + r, 0:wd]
            if idx == n - 1:
                outs[idx][...] = jnp.sum(val, keepdims=True)
            else:
                outs[idx][:, c0:c0 + wd] = val
        for w in range(nw):
            swap(w, 1 - c).wait_recv()
            swap(w, c).wait_send()

    vm = pl.BlockSpec(memory_space=pltpu.VMEM)
    hbm = pl.BlockSpec(memory_space=pl.ANY)
    out_shape = ([jax.ShapeDtypeStruct(p.shape, F32) for p in pieces[:-1]] + [jax.ShapeDtypeStruct((1, 1), F32)]
                 + [jax.ShapeDtypeStruct(g.shape, g.dtype) for g in grads])
    res = pl.pallas_call(
        body, name="final_exchange", in_specs=[vm] * n + [hbm] * nw, out_specs=[vm] * n + [hbm] * nw,
        out_shape=out_shape, input_output_aliases={n + w: n + w for w in range(nw)},
        scratch_shapes=[pltpu.VMEM((rows, SLAB_W), F32), pltpu.VMEM((8, rows, SLAB_W), F32),
                        pltpu.SemaphoreType.DMA((7,)), pltpu.SemaphoreType.DMA((7,)),
                        pltpu.SemaphoreType.DMA((nw,)), pltpu.SemaphoreType.DMA((nw,))],
    )(*pieces, *grads)
    return res[:n], res[n:]


def _adamw(w, g, m, v, *, rb, name):
    r, cdim = w.shape

    def body(w_ref, g_ref, m_ref, v_ref, go_ref, d_ref, nm_ref, nv_ref):
        go_ref[...] = g_ref[...]
        d_ref[...], nm_ref[...], nv_ref[...] = _adamw_update(w_ref[...], g_ref[...], m_ref[...], v_ref[...])

    spec = pl.BlockSpec((rb, cdim), lambda i: (i, 0))
    shp = jax.ShapeDtypeStruct((r, cdim), F32)
    return pl.pallas_call(
        body, name=name, grid=(r // rb,), in_specs=[spec] * 4, out_specs=[spec] * 4, out_shape=[shp] * 4,
        compiler_params=_cparams(("parallel",)),
    )(w, g, m, v)


def _adamw_update(w, g, m, v):
    nm = ADAM_B1 * m + (1.0 - ADAM_B1) * g
    nv = ADAM_B2 * v + (1.0 - ADAM_B2) * (g * g)
    m_hat = nm / (1.0 - ADAM_B1 ** ADAM_STEP)
    v_hat = nv / (1.0 - ADAM_B2 ** ADAM_STEP)
    return -ADAM_LR * (m_hat / (jnp.sqrt(v_hat) + ADAM_EPS) + ADAM_WD * w), nm, nv


def _adamw_small(params):
    n = len(params)

    def body(*refs):
        ins, outs = refs[:4 * n], refs[4 * n:]
        for p in range(n):
            w_ref, g_ref, m_ref, v_ref = ins[4 * p:4 * p + 4]
            d, nm, nv = _adamw_update(w_ref[...], g_ref[...], m_ref[...], v_ref[...])
            outs[3 * p][...] = d
            outs[3 * p + 1][...] = nm
            outs[3 * p + 2][...] = nv

    vm = pl.BlockSpec(memory_space=pltpu.VMEM)
    flat = [a for p in params for a in p]
    out_shape = [jax.ShapeDtypeStruct(p[0].shape, F32) for p in params for _ in range(3)]
    res = pl.pallas_call(
        body, name="adamw_small", in_specs=[vm] * (4 * n), out_specs=[vm] * (3 * n), out_shape=out_shape,
    )(*flat)
    return [tuple(res[3 * p:3 * p + 3]) for p in range(n)]


_PAIR_ADD_ROWS = {"w_in": 256, "w_out": 128, "w_up": 256, "w_down": 176}


def _pair_sums(grads, others, names, c_idx):
    return [_pair_add(g, o_, c_idx, rb=_PAIR_ADD_ROWS[n], name=f"pair_add_{n}") for g, o_, n in zip(grads, others, names)]


def _local_step(x, tgt, meta_full, lb_param, attn_norm_w, w_in_g, hgrn_norm_w, conv_w_full, w_out_full,
                ffn_norm_w, late_slotted, fcw_full, ffn_conv_b, final_norm_w, c_idx, cj_idx):
    seq = x.shape[0]
    T = TR + seq
    head_tile = jnp.concatenate([jnp.zeros((PAD, D), F32), meta_full], axis=0)
    whn_t = jnp.tile(hgrn_norm_w, (1, NH))

    ut, projb, o, sst, mt, h1, w_up_g, w_down_g = _mix_block_fwd(
        x, head_tile, lb_param, attn_norm_w, whn_t, conv_w_full, w_in_g, w_out_full, late_slotted, (True, False))
    w_down_full = w_down_g.reshape(DFF, D)
    u2t, upb, ggt, dh2, loss_vec, dwfin = _ffn_block_fwd(
        h1, tgt, ffn_norm_w, w_up_g, fcw_full, ffn_conv_b, w_down_full, final_norm_w.reshape(1, D))

    dup, dh1, dfw, dfb, dwffn = _ffn_block_bwd(
        dh2, upb, h1, ffn_norm_w, w_up_g, fcw_full, ffn_conv_b, w_down_full)
    kb = 1408 if T % 1408 == 0 else TR
    g_up = _weight_grad(u2t, dup, bn=DFF, bk=kb, name="dw_up_mm", shard_cols=2 * DFF // NSHARD)
    g_down, sib_up = _weight_grad(ggt, dh2, bn=D // 2, bk=2816 if T % 2816 == 0 else kb, name="dw_down_mm", ride=[g_up])
    g_down = g_down.reshape(NSHARD, DFF // NSHARD, D)
    sib_down = _pair_exchange([g_down], name="grad_pair_exchange_down")
    ps_ffn = _pair_sums([g_up, g_down], sib_up + list(sib_down), ("w_up", "w_down"), c_idx)

    dproj, dlb, dwhn, dcw, *parts_ffn = _mix_block_bwd(
        dh1, projb, o, sst, lb_param, whn_t, conv_w_full, w_out_full, ps_ffn)
    kb_deep = 2816 if T % 2816 == 0 else kb
    g_in = _weight_grad(ut, dproj, bn=9 * D // NSHARD, bk=kb_deep, name="dw_in_mm", shard_cols=9 * D // NSHARD)
    g_out, sib_in = _weight_grad(mt, dh1, bn=D, bk=kb_deep, name="dw_out_mm", ride=[g_in])
    g_out = g_out.reshape(NSHARD, D // NSHARD, D)
    sib_out = _pair_exchange([g_out], name="grad_pair_exchange_out")
    ps_mix = _pair_sums([g_in, g_out], sib_in + list(sib_out), ("w_in", "w_out"), c_idx)

    grad_x, dmeta, dwattn, *parts_mix = _input_grad_block(dproj, x, head_tile, dh1, attn_norm_w, w_in_g, ps_mix)

    halves = [_chip_sum(ps, p, cj_idx, rb=_PAIR_ADD_ROWS[n], name=f"chip_sum_{n}")
              for ps, p, n in zip(ps_mix + ps_ffn, parts_mix + parts_ffn, ("w_in", "w_out", "w_up", "w_down"))]
    small = dict(dlb=dlb, dwattn=dwattn, dwhn=dwhn, dwffn=dwffn, dfb=dfb, dwfin=dwfin,
                 dcw=dcw, dfw=dfw, dmeta=dmeta, loss=loss_vec)
    return grad_x, small, halves


_SMALL_ORDER = ("dmeta", "dcw", "dfw", "dlb", "dwattn", "dwhn", "dwffn", "dfb", "dwfin", "loss")


def kernel(x, meta_tokens, lb_param, attn_norm_w, w_in, hgrn_norm_w, conv_w, w_out, ffn_norm_w, w_up, ffn_conv_w, ffn_conv_b, w_down, final_norm_w, loss_target, m_meta_tokens, m_lb_param, m_attn_norm_w, m_w_in, m_hgrn_norm_w, m_conv_w, m_w_out, m_ffn_norm_w, m_w_up, m_ffn_conv_w, m_ffn_conv_b, m_w_down, m_final_norm_w, v_meta_tokens, v_lb_param, v_attn_norm_w, v_w_in, v_hgrn_norm_w, v_conv_w, v_w_out, v_ffn_norm_w, v_w_up, v_ffn_conv_w, v_ffn_conv_b, v_w_down, v_final_norm_w):
    xi, yi, ci = _place()
    j = 2 * xi + yi
    c_idx = jnp.reshape(ci, (1,)).astype(jnp.int32)

    j_idx = jnp.reshape(j, (1,)).astype(jnp.int32)
    ds_, fs_ = D // NSHARD, DFF // NSHARD
    widen = lambda a: jnp.pad(a, ((0, 0), (0, 768 - a.shape[1])))
    rows_small = jnp.concatenate([widen(meta_tokens), widen(conv_w[0]), widen(ffn_conv_w[0]),
                                  jnp.zeros((2, 768), F32)], axis=0)
    s_in, s_out, s_up, s_down = [
        _into_slot(w[0], j_idx, rb=rb, dtype=BF16, name=f"slot_{n}", paired=pr)
        for w, rb, n, pr in ((w_in, 256, "w_in", False), (w_out, 128, "w_out", False), (w_up, 256, "w_up", True),
                             (w_down, 176, "w_down", False))]
    s_small = _into_slot(rows_small, j_idx, rb=rows_small.shape[0], dtype=F32, name="slot_small")
    w_in_g, w_out_g, small_g = _allgather_weights([s_in, s_out, s_small], (False, False, True))
    unshard = lambda a: jnp.transpose(a, (1, 0, 2)).reshape(a.shape[1], -1)
    meta_full = unshard(small_g[:, 0:NMETA, 0:ds_])
    conv_w_full = unshard(small_g[:, NMETA:NMETA + 3, 0:ds_])
    fcw_full = unshard(small_g[:, NMETA + 3:NMETA + 6, 0:fs_])

    cj_idx = jnp.stack([ci, j]).astype(jnp.int32)
    grad_x, small, halves = _local_step(
        x[0], loss_target[0], meta_full, lb_param, attn_norm_w, w_in_g, hgrn_norm_w, conv_w_full,
        w_out_g.reshape(D, D), ffn_norm_w, [s_up, s_down], fcw_full, ffn_conv_b, final_norm_w, c_idx, cj_idx)

    names = _SMALL_ORDER
    small_sums, g_big = _final_exchange([small[n] for n in names], halves)
    vals = dict(zip(names, small_sums))
    loss = vals["loss"].reshape(())
    g_small = {
        "meta_tokens": lax.dynamic_slice_in_dim(vals["dmeta"], j * (D // NSHARD), D // NSHARD, axis=1),
        "lb_param": jnp.concatenate([vals["dlb"], -vals["dlb"]], axis=0),
        "attn_norm_w": vals["dwattn"],
        "hgrn_norm_w": vals["dwhn"],
        "conv_w": lax.dynamic_slice_in_dim(vals["dcw"], j * (D // NSHARD), D // NSHARD, axis=1)[None],
        "ffn_norm_w": vals["dwffn"],
        "ffn_conv_w": lax.dynamic_slice_in_dim(vals["dfw"], j * (DFF // NSHARD), DFF // NSHARD, axis=1)[None],
        "ffn_conv_b": vals["dfb"],
        "final_norm_w": vals["dwfin"].reshape(D),
    }


    weights = {"meta_tokens": meta_tokens, "lb_param": lb_param, "attn_norm_w": attn_norm_w, "w_in": w_in,
               "hgrn_norm_w": hgrn_norm_w, "conv_w": conv_w, "w_out": w_out, "ffn_norm_w": ffn_norm_w,
               "w_up": w_up, "ffn_conv_w": ffn_conv_w, "ffn_conv_b": ffn_conv_b, "w_down": w_down,
               "final_norm_w": final_norm_w}
    ms = {"meta_tokens": m_meta_tokens, "lb_param": m_lb_param, "attn_norm_w": m_attn_norm_w, "w_in": m_w_in,
          "hgrn_norm_w": m_hgrn_norm_w, "conv_w": m_conv_w, "w_out": m_w_out, "ffn_norm_w": m_ffn_norm_w,
          "w_up": m_w_up, "ffn_conv_w": m_ffn_conv_w, "ffn_conv_b": m_ffn_conv_b, "w_down": m_w_down,
          "final_norm_w": m_final_norm_w}
    vs = {"meta_tokens": v_meta_tokens, "lb_param": v_lb_param, "attn_norm_w": v_attn_norm_w, "w_in": v_w_in,
          "hgrn_norm_w": v_hgrn_norm_w, "conv_w": v_conv_w, "w_out": v_w_out, "ffn_norm_w": v_ffn_norm_w,
          "w_up": v_w_up, "ffn_conv_w": v_ffn_conv_w, "ffn_conv_b": v_ffn_conv_b, "w_down": v_w_down,
          "final_norm_w": v_final_norm_w}
    order = list(weights)
    grads, deltas, new_m, new_v = {}, {}, {}, {}

    for name, g, rb in zip(("w_in", "w_out", "w_up", "w_down"), g_big, (256, 128, 256, 176)):
        shp = weights[name].shape
        w2, m2, v2 = (a.reshape(shp[1], shp[2]) for a in (weights[name], ms[name], vs[name]))
        g_, d_, nm_, nv_ = _adamw(w2, g, m2, v2, rb=rb, name=f"adamw_{name}")
        grads[name], deltas[name], new_m[name], new_v[name] = (a.reshape(shp) for a in (g_, d_, nm_, nv_))

    small_names = [n for n in order if n not in grads]
    as2d = lambda a: a.reshape(-1, a.shape[-1])
    res = _adamw_small([tuple(as2d(a) for a in (weights[n], g_small[n], ms[n], vs[n])) for n in small_names])
    for n, (d_, nm_, nv_) in zip(small_names, res):
        shp = weights[n].shape
        grads[n], deltas[n], new_m[n], new_v[n] = (a.reshape(shp) for a in (g_small[n], d_, nm_, nv_))

    return (loss, grad_x[None], *[grads[n] for n in order], *[deltas[n] for n in order],
            *[new_m[n] for n in order], *[new_v[n] for n in order])
```

```python
import jax
import jax.numpy as jnp
from jax import lax
from jax.experimental import pallas as pl
from jax.experimental.pallas import tpu as pltpu

F32 = jnp.float32
BF16 = jnp.bfloat16
MESH = pl.DeviceIdType.MESH

D = 1024
NH = 8
HD = 128
DFF = 2816
NMETA = 16
EPS = 1e-6
TR = 256
PAD = TR - NMETA
CH = 64
NSHARD = 4
VMEM_LIMIT = 62 * 1024 * 1024

ADAM_LR = 0.001
ADAM_B1 = 0.9
ADAM_B2 = 0.999
ADAM_EPS = 1e-08
ADAM_WD = 0.01
ADAM_STEP = 10


def _cparams(semantics=None, **kw):
    return pltpu.CompilerParams(dimension_semantics=semantics, vmem_limit_bytes=VMEM_LIMIT, **kw)


def _sigmoid(x):
    return 0.5 * jnp.tanh(0.5 * x) + 0.5


def _weight_grad(at, b, *, bn, bk, name, bm=None, shard_cols=None, ride=()):
    M, T = at.shape
    N = b.shape[1]
    bm = M if bm is None else bm
    assert b.shape[0] == T and T % bk == 0 and N % bn == 0 and M % bm == 0, (name, M, N, T, bm, bn, bk)
    nk = T // bk
    nj = N // bn
    ni = M // bm
    nex = len(ride)
    if shard_cols is None:
        nsh = 1
        blk = (bm, bn)
        out_shape = jax.ShapeDtypeStruct((M, N), BF16)
        out_spec = pl.BlockSpec(blk, lambda i, j, k: (i, j))
    else:
        assert bn % shard_cols == 0 and N % shard_cols == 0
        nsh = bn // shard_cols
        blk = (nsh, bm, shard_cols)
        out_shape = jax.ShapeDtypeStruct((N // shard_cols, M, shard_cols), BF16)
        out_spec = pl.BlockSpec(blk, lambda i, j, k: (j, i, 0))

    def body(*refs):
        a_ref, b_ref = refs[:2]
        o_ref = refs[2 + nex]
        acc_ref = refs[3 + 2 * nex]
        j, k = pl.program_id(0) * nj + pl.program_id(1), pl.program_id(2)
        if nex:
            exchange = _PairExchange(refs[2:2 + nex], refs[3 + nex:3 + 2 * nex], refs[-2], refs[-1])

            @pl.when(jnp.logical_and(j == 0, k == 0))
            def _():
                exchange.send()

        @pl.when(k == 0)
        def _():
            acc_ref[...] = jnp.zeros_like(acc_ref)

        p = jnp.dot(a_ref[...].astype(BF16), b_ref[...].astype(BF16), preferred_element_type=F32)
        if shard_cols is None:
            acc_ref[...] += p
        else:
            for q in range(nsh):
                acc_ref[q] += p[:, q * shard_cols:(q + 1) * shard_cols]

        @pl.when(k == nk - 1)
        def _():
            o_ref[...] = acc_ref[...].astype(BF16)

        if nex:
            @pl.when(jnp.logical_and(j == ni * nj - 1, k == nk - 1))
            def _():
                exchange.finish()

    hbm = pl.BlockSpec(memory_space=pl.ANY)
    scratch = [pltpu.VMEM(blk, F32)]
    if nex:
        scratch += [pltpu.SemaphoreType.DMA((nex,)), pltpu.SemaphoreType.DMA((nex,))]
    res = pl.pallas_call(
        body, name=name, grid=(ni, nj, nk),
        in_specs=[pl.BlockSpec((bm, bk), lambda i, j, k: (i, k)),
                  pl.BlockSpec((bk, bn), lambda i, j, k: (k, j))] + [hbm] * nex,
        out_specs=[out_spec] + [hbm] * nex, out_shape=[out_shape] + _pair_shapes(ride), scratch_shapes=scratch,
        compiler_params=_cparams(("arbitrary",) * 3 if nex else ("parallel", "parallel", "arbitrary")),
    )(at, b, *ride)
    return (res[0], list(res[1:])) if nex else res[0]


def _input_grad_block(dproj, x_seq, head_tile, dres, w, w_in_g, psums):
    T = TR + x_seq.shape[0]
    nt = T // TR
    nsl = w_in_g.shape[0]
    wsl = w_in_g.shape[2]
    nex = len(psums)

    def body(*refs):
        dp_ref, x_ref, head_ref, dres_ref, w_ref, win_ref = refs[:6]
        gx_ref, dmeta_ref, dw_ref = refs[6 + nex:9 + nex]
        exchange = _ChipExchange(refs[6:6 + nex], refs[9 + nex:9 + 2 * nex], refs[-2], refs[-1])
        i = pl.program_id(0)

        @pl.when(i == 0)
        def _():
            exchange.send()

        @pl.when(i == nt - 1)
        def _():
            exchange.finish()

        x = jnp.where(i == 0, head_ref[...], x_ref[...])
        r = lax.rsqrt(jnp.mean(x * x, axis=-1, keepdims=True) + EPS)
        n = x * r
        du_v = None
        for s in range(nsl):
            part = _dot_nt(dp_ref[:, s * wsl:(s + 1) * wsl], win_ref[s])
            du_v = part if du_v is None else du_v + part
        dn = du_v * w_ref[...]
        dh = dres_ref[...] + r * (dn - n * jnp.mean(dn * n, axis=-1, keepdims=True))
        gx_ref[...] = dh
        part = jnp.sum(du_v * n, axis=0, keepdims=True)

        @pl.when(i == 0)
        def _():
            dmeta_ref[...] = dh[PAD:TR, :]
            dw_ref[...] = part

        @pl.when(i > 0)
        def _():
            dw_ref[...] += part

    row = pl.BlockSpec((TR, D), lambda i: (i, 0))
    vec = pl.BlockSpec((1, D), lambda i: (0, 0))
    hbm = pl.BlockSpec(memory_space=pl.ANY)
    return pl.pallas_call(
        body, name="input_grad_block", grid=(nt,),
        in_specs=[pl.BlockSpec((TR, 9 * D), lambda i: (i, 0)),
                  pl.BlockSpec((TR, D), lambda i: (jnp.maximum(i - 1, 0), 0)), pl.BlockSpec((TR, D), lambda i: (0, 0)),
                  row, vec, pl.BlockSpec(memory_space=pltpu.VMEM)] + [hbm] * nex,
        out_specs=[pl.BlockSpec((TR, D), lambda i: (jnp.maximum(i - 1, 0), 0)),
                   pl.BlockSpec((NMETA, D), lambda i: (0, 0)), vec] + [hbm] * nex,
        out_shape=[jax.ShapeDtypeStruct((T - TR, D), F32), jax.ShapeDtypeStruct((NMETA, D), F32),
                   jax.ShapeDtypeStruct((1, D), F32)]
        + [jax.ShapeDtypeStruct((3,) + p.shape[1:], p.dtype) for p in psums],
        scratch_shapes=[pltpu.SemaphoreType.DMA((nex, 3)), pltpu.SemaphoreType.DMA((nex, 3))],
        compiler_params=_cparams(("arbitrary",)),
    )(dproj, x_seq, head_tile, dres, w, w_in_g, *psums)


def _tri_matmul(tri_bf16, x):
    hi = x.astype(BF16)
    lo = (x - hi.astype(F32)).astype(BF16)
    return jnp.dot(tri_bf16, lo, preferred_element_type=F32) + jnp.dot(tri_bf16, hi, preferred_element_type=F32)


def _shift_down(x, prev8, n):
    rows = x.shape[0]
    return pltpu.roll(jnp.concatenate([prev8, x], axis=0), n, 0)[8:8 + rows, :]


def _shift_up(x, next8, n):
    rows = x.shape[0]
    return pltpu.roll(jnp.concatenate([x, next8], axis=0), rows + 8 - n, 0)[0:rows, :]


def _gates(f_raw, lb):
    sg = _sigmoid(f_raw)
    f = lb + (1.0 - lb) * sg
    return sg, f, jnp.log(f), 1.0 - f


def _lower_bound(lbp_ref):
    return _sigmoid(lbp_ref[0:1, :] - lbp_ref[1:2, :])


def _tri_masks():
    r = lax.broadcasted_iota(jnp.int32, (CH, CH), 0)
    c = lax.broadcasted_iota(jnp.int32, (CH, CH), 1)
    return r >= c, r <= c


def _ones_where(mask):
    return jnp.where(mask, 1.0, 0.0).astype(BF16)


def _dot(a, b):
    return jnp.dot(a.astype(BF16), b.astype(BF16), preferred_element_type=F32)


def _dot_nt(a, b):
    return lax.dot_general(a.astype(BF16), b.astype(BF16), (((1,), (1,)), ((), ())), preferred_element_type=F32)


def _dot_tn(a, b):
    return lax.dot_general(a.astype(BF16), b.astype(BF16), (((0,), (0,)), ((), ())), preferred_element_type=F32)


def _mix_block_fwd(x_seq, head_tile, lb_param, wattn, whn, conv_w, w_in_g, w_out, late, late_paired):
    T = TR + x_seq.shape[0]
    nt = T // TR
    ncht = TR // CH
    nsl = w_in_g.shape[0]
    wsl = w_in_g.shape[2]
    nlate = len(late)

    def body(*refs):
        x_ref, head_ref, lbp_ref, wa_ref, whn_ref, cw_ref, win_ref, wout_ref = refs[:8]
        ut_ref, projb_ref, o_ref, sst_ref, mt_ref, h1_ref = refs[8 + nlate:14 + nlate]
        late_refs = refs[14 + nlate:14 + 2 * nlate]
        proj_ref, m_ref, st_ref, cxc_ref, send_sems, recv_sems = refs[14 + 2 * nlate:]
        i = pl.program_id(0)
        gather = _Gather(late_refs, send_sems, recv_sems, (False,) * nlate, late_paired)

        @pl.when(i == 0)
        def _():
            st_ref[...] = jnp.zeros_like(st_ref)
            cxc_ref[...] = jnp.zeros_like(cxc_ref)
            gather.send()

        @pl.when(i == nt // 2)
        def _():
            gather.relay()

        @pl.when(i == (3 * nt) // 4)
        def _():
            gather.relay_diagonal()

        x = jnp.where(i == 0, head_ref[...], x_ref[...])
        r1 = lax.rsqrt(jnp.mean(x * x, axis=-1, keepdims=True) + EPS)
        u_f = x * r1 * wa_ref[...]
        u = u_f.astype(BF16)
        ut_ref[...] = u_f.T.astype(BF16)
        for s in range(nsl):
            cols = slice(s * wsl, (s + 1) * wsl)
            p = jnp.dot(u, win_ref[s], preferred_element_type=F32)
            proj_ref[:, cols] = p
            projb_ref[:, cols] = p.astype(BF16)

        lb = _lower_bound(lbp_ref)
        lower, _ = _tri_masks()
        ltri = _ones_where(lower)
        whn_v = whn_ref[...]
        w0, w1, w2 = cw_ref[0:1, :], cw_ref[1:2, :], cw_ref[2:3, :]

        def gates_and_conv(c):
            rows = slice(c * CH, (c + 1) * CH)
            q_raw = proj_ref[rows, 0:D]
            f_raw = proj_ref[rows, D:2 * D]
            q = q_raw * _sigmoid(q_raw)
            _, _, g, k = _gates(f_raw, lb)
            gam = _tri_matmul(ltri, g)
            gam_l = gam[CH - 1:CH, :]
            g_out = proj_ref[rows, 3 * D:4 * D]
            cx = proj_ref[rows, 5 * D:6 * D] * proj_ref[rows, 6 * D:7 * D]
            prev8 = cxc_ref[...]
            cv = w0 * _shift_down(cx, prev8, 2) + w1 * _shift_down(cx, prev8, 1) + w2 * cx
            cxc_ref[...] = cx[CH - 8:CH, :]
            y_b = proj_ref[rows, 4 * D:5 * D] * cv
            return dict(
                e_l=jnp.exp(gam_l), qt=(q * jnp.exp(gam)).astype(BF16), kt=(k * jnp.exp(-gam)).astype(BF16),
                khat=(k * jnp.exp(gam_l - gam)).astype(BF16), vb=proj_ref[rows, 2 * D:3 * D].astype(BF16),
                out_gate=whn_v * (g_out * _sigmoid(g_out)), merge_a=_sigmoid(proj_ref[rows, 7 * D:8 * D]),
                merged_b=_sigmoid(proj_ref[rows, 8 * D:9 * D]) * y_b)

        def recurrence_and_merge(c, pre):
            rows = slice(c * CH, (c + 1) * CH)
            qt, kt, khat, vb, e_l = pre["qt"], pre["kt"], pre["khat"], pre["vb"], pre["e_l"]
            heads = [slice(h * HD, (h + 1) * HD) for h in range(NH)]
            sts = [st_ref[h] for h in range(NH)]
            scores = [_dot_nt(qt[:, cs], kt[:, cs]) for cs in heads]
            o_state = [_dot_nt(qt[:, cs], st) for cs, st in zip(heads, sts)]
            st_new = [_dot_tn(vb[:, cs], khat[:, cs]) for cs in heads]
            on_parts = []
            for h, cs in enumerate(heads):
                sst_ref[c, h] = sts[h]
                st_ref[h] = sts[h] * e_l[:, cs] + st_new[h]
                o_h = o_state[h] + _dot(jnp.where(lower, scores[h], 0.0), vb[:, cs])
                o_ref[rows, cs] = o_h
                ro = lax.rsqrt(jnp.mean(o_h * o_h, axis=-1, keepdims=True) + EPS)
                on_parts.append(o_h * ro)
            on = jnp.concatenate(on_parts, axis=1)
            m_ref[rows, :] = pre["merge_a"] * (on * pre["out_gate"]) + pre["merged_b"]

        pre = {0: gates_and_conv(0)}
        for c in range(ncht):
            if c + 1 < ncht:
                pre[c + 1] = gates_and_conv(c + 1)
            recurrence_and_merge(c, pre.pop(c))
        m_v = m_ref[...]
        h1_ref[...] = x + jnp.dot(m_v.astype(BF16), wout_ref[...], preferred_element_type=F32)
        mt_ref[...] = m_v.T.astype(BF16)

        @pl.when(i == nt - 1)
        def _():
            gather.finish()

    row = lambda w: pl.BlockSpec((TR, w), lambda i: (i, 0))
    col = lambda w: pl.BlockSpec((w, TR), lambda i: (0, i))
    vec = lambda r: pl.BlockSpec((r, D), lambda i: (0, 0))
    vm = pl.BlockSpec(memory_space=pltpu.VMEM)
    hbm = pl.BlockSpec(memory_space=pl.ANY)
    return pl.pallas_call(
        body, name="mix_block_fwd", grid=(nt,),
        in_specs=[pl.BlockSpec((TR, D), lambda i: (jnp.maximum(i - 1, 0), 0)), pl.BlockSpec((TR, D), lambda i: (0, 0)),
                  vec(2), vec(1), vec(1), vec(3), vm, vm] + [hbm] * nlate,
        out_specs=[col(D), row(9 * D), row(D), pl.BlockSpec((ncht, NH, HD, HD), lambda i: (i, 0, 0, 0)),
                   col(D), row(D)] + [hbm] * nlate,
        out_shape=[jax.ShapeDtypeStruct((D, T), BF16), jax.ShapeDtypeStruct((T, 9 * D), BF16),
                   jax.ShapeDtypeStruct((T, D), F32), jax.ShapeDtypeStruct((T // CH, NH, HD, HD), F32),
                   jax.ShapeDtypeStruct((D, T), BF16), jax.ShapeDtypeStruct((T, D), F32)]
        + [jax.ShapeDtypeStruct(a.shape, a.dtype) for a in late],
        input_output_aliases={8 + n: 6 + n for n in range(nlate)},
        scratch_shapes=[pltpu.VMEM((TR, 9 * D), F32), pltpu.VMEM((TR, D), F32), pltpu.VMEM((NH, HD, HD), F32),
                        pltpu.VMEM((8, D), F32), pltpu.SemaphoreType.DMA((nlate, 8)),
                        pltpu.SemaphoreType.DMA((nlate, 8))],
        compiler_params=_cparams(("arbitrary",)),
    )(x_seq, head_tile, lb_param, wattn, whn, conv_w, w_in_g, w_out, *late)


def _mix_block_bwd(dh1, projb, o, sst, lb_param, whn, conv_w, w_out, psums):
    T = projb.shape[0]
    nt = T // TR
    ncht = TR // CH
    tb16 = TR // 16
    nex = len(psums)

    def body(*refs):
        dh1_ref, proj_ref, pc_ref, px_ref, o_ref, sst_ref, lbp_ref, whn_ref, cw_ref, wout_ref = refs[:10]
        dproj_ref, dlb_ref, dwhn_ref, dcw_ref = refs[10 + nex:14 + nex]
        exchange = _ChipExchange(refs[10:10 + nex], refs[14 + nex:14 + 2 * nex], refs[-2], refs[-1])
        dm_ref, dst_ref, dcvc_ref, acc_lb, acc_hn, acc_cw = refs[14 + 2 * nex:-2]
        s = pl.program_id(0)
        tile = nt - 1 - s

        @pl.when(s == 0)
        def _():
            dst_ref[...] = jnp.zeros_like(dst_ref)
            dcvc_ref[...] = jnp.zeros_like(dcvc_ref)
            acc_lb[...] = jnp.zeros_like(acc_lb)
            acc_hn[...] = jnp.zeros_like(acc_hn)
            acc_cw[...] = jnp.zeros_like(acc_cw)
            exchange.send()

        dm_ref[...] = _dot_nt(dh1_ref[...], wout_ref[...])
        lb = _lower_bound(lbp_ref)
        lower, upper = _tri_masks()
        ltri = _ones_where(lower)
        utri = _ones_where(upper)
        whn_v = whn_ref[...]
        w0, w1, w2 = cw_ref[0:1, :], cw_ref[1:2, :], cw_ref[2:3, :]
        cx_before_tile = jnp.where(tile > 0, (pc_ref[...].astype(F32) * px_ref[...].astype(F32))[8:16, :], 0.0)
        rid = lax.broadcasted_iota(jnp.int32, (CH, D), 0)

        def chunk(cc, carry):
            c = ncht - 1 - cc
            r0 = pl.multiple_of(c * CH, CH)
            rows = pl.ds(r0, CH)
            slab = lambda n: proj_ref[rows, n * D:(n + 1) * D].astype(F32)
            q_raw, f_raw, v, g_out, b_gate, c_gate, x_conv = (slab(n) for n in range(7))
            sa = _sigmoid(slab(7))
            sb = _sigmoid(slab(8))
            dm_v = dm_ref[rows, :]

            sq = _sigmoid(q_raw)
            q = q_raw * sq
            sg, f, g, k = _gates(f_raw, lb)
            gam = _tri_matmul(ltri, g)
            gam_l = gam[CH - 1:CH, :]
            e_l = jnp.exp(gam_l)
            e_g = jnp.exp(gam)
            e_ng = jnp.exp(-gam)
            e_kl = jnp.exp(gam_l - gam)
            qt = q * e_g
            kt = k * e_ng
            khat = k * e_kl
            qt_b, kt_b, khat_b, v_b = qt.astype(BF16), kt.astype(BF16), khat.astype(BF16), v.astype(BF16)
            qt_seen, kt_seen = qt_b.astype(F32), kt_b.astype(F32)
            s_go = _sigmoid(g_out)
            silu_go = g_out * s_go
            cx = c_gate * x_conv
            rprev = pl.ds(pl.multiple_of(jnp.maximum(r0 - 16, 0), 16), 16)
            cx_prev_in = (proj_ref[rprev, 5 * D:6 * D].astype(F32) * proj_ref[rprev, 6 * D:7 * D].astype(F32))[8:16, :]
            prev8 = jnp.where(c > 0, cx_prev_in, cx_before_tile)
            cx_m1 = _shift_down(cx, prev8, 1)
            cx_m2 = _shift_down(cx, prev8, 2)
            cv = w0 * cx_m2 + w1 * cx_m1 + w2 * cx
            y_b = b_gate * cv

            o_v = o_ref[rows, :]
            ro_parts, on_parts = [], []
            for h in range(NH):
                cs = slice(h * HD, (h + 1) * HD)
                o_h = o_v[:, cs]
                ro = lax.rsqrt(jnp.mean(o_h * o_h, axis=-1, keepdims=True) + EPS)
                ro_parts.append(ro)
                on_parts.append(o_h * ro)
            on = jnp.concatenate(on_parts, axis=1)
            out_gate = whn_v * silu_go
            y_a = on * out_gate

            dy_a = dm_v * sa
            dy_b = dm_v * sb
            dproj_ref[rows, 7 * D:8 * D] = (dy_a * y_a * (1.0 - sa)).astype(BF16)
            dproj_ref[rows, 8 * D:9 * D] = (dy_b * y_b * (1.0 - sb)).astype(BF16)
            dproj_ref[rows, 4 * D:5 * D] = (dy_b * cv).astype(BF16)
            dcv = dy_b * b_gate
            acc_cw[0:1, :] += jnp.sum(dcv * cx_m2, axis=0, keepdims=True)
            acc_cw[1:2, :] += jnp.sum(dcv * cx_m1, axis=0, keepdims=True)
            acc_cw[2:3, :] += jnp.sum(dcv * cx, axis=0, keepdims=True)
            next8 = dcvc_ref[...]
            dcx = w2 * dcv + w1 * _shift_up(dcv, next8, 1) + w0 * _shift_up(dcv, next8, 2)
            dcvc_ref[...] = dcv[0:8, :]
            dproj_ref[rows, 5 * D:6 * D] = (dcx * x_conv).astype(BF16)
            dproj_ref[rows, 6 * D:7 * D] = (dcx * c_gate).astype(BF16)
            don = dy_a * out_gate
            dya_on = dy_a * on
            dproj_ref[rows, 3 * D:4 * D] = (dya_on * (whn_v * (s_go * (1.0 + g_out * (1.0 - s_go))))).astype(BF16)
            acc_hn[...] += jnp.sum(dya_on * silu_go, axis=0, keepdims=True)

            heads = [slice(h * HD, (h + 1) * HD) for h in range(NH)]
            do_bs, sts, dstns, first = [], [], [], []
            for h, cs in enumerate(heads):
                on_h = on_parts[h]
                don_h = don[:, cs]
                do_h = ro_parts[h] * (don_h - on_h * jnp.mean(don_h * on_h, axis=-1, keepdims=True))
                do_bs.append(do_h.astype(BF16))
                sts.append(sst_ref[c, h])
                dstns.append(dst_ref[h])
            for h, cs in enumerate(heads):
                qt_h, kt_h, khat_h, v_h = qt_b[:, cs], kt_b[:, cs], khat_b[:, cs], v_b[:, cs]
                do_b, dstn_b = do_bs[h], dstns[h].astype(BF16)
                first.append(dict(
                    a_t=_dot_nt(kt_h, qt_h), da=_dot_nt(do_b, v_h), da_t=_dot_nt(v_h, do_b),
                    dv_state=_dot_nt(khat_h, dstn_b), dqt_state=_dot(do_b, sts[h]), dkhat=_dot(v_h, dstn_b),
                    dst_chunk=_dot_tn(do_b, qt_h)))
            dq_parts, dk_parts, dv_parts, dgam_parts, ext_parts = [], [], [], [], []
            for h, cs in enumerate(heads):
                qt_h, kt_h = qt_b[:, cs], kt_b[:, cs]
                p1, do_b, st, dstn = first[h], do_bs[h], sts[h], dstns[h]
                dv_h = _dot(jnp.where(upper, p1["a_t"], 0.0), do_b) + p1["dv_state"]
                dqt_state = p1["dqt_state"]
                dqt_chunk = _dot(jnp.where(lower, p1["da"], 0.0), kt_h)
                dkt = _dot(jnp.where(upper, p1["da_t"], 0.0), qt_h)
                dkhat = p1["dkhat"]
                dq_h = (dqt_state + dqt_chunk) * e_g[:, cs]
                dk_h = dkt * e_ng[:, cs] + dkhat * e_kl[:, cs]
                khat_dkhat = dkhat * khat[:, cs]
                ext = (jnp.sum(khat_dkhat, axis=0, keepdims=True)
                       + e_l[:, cs] * jnp.sum(st * dstn, axis=0, keepdims=True))
                dst_ref[h] = p1["dst_chunk"] + dstn * e_l[:, cs]
                dq_parts.append(dq_h)
                dk_parts.append(dk_h)
                dv_parts.append(dv_h)
                dgam_parts.append(qt[:, cs] * dqt_state + qt_seen[:, cs] * dqt_chunk - kt_seen[:, cs] * dkt
                                  - khat_dkhat)
                ext_parts.append(ext)
            dq = jnp.concatenate(dq_parts, axis=1)
            dk = jnp.concatenate(dk_parts, axis=1)
            dgam = jnp.concatenate(dgam_parts, axis=1)
            ext = jnp.concatenate(ext_parts, axis=1)
            dgam = dgam + jnp.where(rid == CH - 1, ext, 0.0)
            dg = _tri_matmul(utri, dgam)
            dproj_ref[rows, 0:D] = (dq * (sq * (1.0 + q_raw * (1.0 - sq)))).astype(BF16)
            df = dg * jnp.exp(-g) - dk
            dproj_ref[rows, D:2 * D] = (df * (1.0 - lb) * sg * (1.0 - sg)).astype(BF16)
            dproj_ref[rows, 2 * D:3 * D] = jnp.concatenate(dv_parts, axis=1).astype(BF16)
            real = (tile * TR + r0 + rid) >= PAD
            acc_lb[...] += jnp.sum(jnp.where(real, df * (1.0 - sg), 0.0), axis=0, keepdims=True)
            return carry

        lax.fori_loop(0, ncht, chunk, 0)

        @pl.when(s == nt - 1)
        def _():
            dlb_ref[...] = acc_lb[...] * lb * (1.0 - lb)
            hn = acc_hn[...]
            tot = hn[:, 0:HD]
            for h in range(1, NH):
                tot = tot + hn[:, h * HD:(h + 1) * HD]
            dwhn_ref[...] = tot
            dcw_ref[...] = acc_cw[0:3, :]
            exchange.finish()

    hbm = pl.BlockSpec(memory_space=pl.ANY)
    rev = lambda s: (nt - 1 - s, 0)
    prevc = lambda s: (jnp.maximum((nt - 1 - s) * tb16 - 1, 0), 5)
    prevx = lambda s: (jnp.maximum((nt - 1 - s) * tb16 - 1, 0), 6)
    const = lambda s: (0, 0)
    return pl.pallas_call(
        body, name="mix_block_bwd", grid=(nt,),
        in_specs=[pl.BlockSpec((TR, D), rev),
                  pl.BlockSpec((TR, 9 * D), rev),
                  pl.BlockSpec((16, D), prevc),
                  pl.BlockSpec((16, D), prevx),
                  pl.BlockSpec((TR, D), rev),
                  pl.BlockSpec((ncht, NH, HD, HD), lambda s: (nt - 1 - s, 0, 0, 0)),
                  pl.BlockSpec((2, D), const),
                  pl.BlockSpec((1, D), const),
                  pl.BlockSpec((3, D), const),
                  pl.BlockSpec(memory_space=pltpu.VMEM)] + [hbm] * nex,
        out_specs=[pl.BlockSpec((TR, 9 * D), rev),
                   pl.BlockSpec((1, D), const),
                   pl.BlockSpec((1, HD), const),
                   pl.BlockSpec((3, D), const)] + [hbm] * nex,
        out_shape=[jax.ShapeDtypeStruct((T, 9 * D), BF16), jax.ShapeDtypeStruct((1, D), F32),
                   jax.ShapeDtypeStruct((1, HD), F32), jax.ShapeDtypeStruct((3, D), F32)]
        + [jax.ShapeDtypeStruct((3,) + p.shape[1:], p.dtype) for p in psums],
        scratch_shapes=[pltpu.VMEM((TR, D), F32), pltpu.VMEM((NH, HD, HD), F32), pltpu.VMEM((8, D), F32),
                        pltpu.VMEM((1, D), F32), pltpu.VMEM((1, D), F32), pltpu.VMEM((8, D), F32),
                        pltpu.SemaphoreType.DMA((nex, 3)), pltpu.SemaphoreType.DMA((nex, 3))],
        compiler_params=_cparams(("arbitrary",)),
    )(dh1, projb, projb, projb, o, sst, lb_param, whn, conv_w, w_out, *psums)


def _ffn_block_fwd(h1, tgt, wffn, w_up_g, fcw, fcb, w_down, wfin):
    T = h1.shape[0]
    nt = T // TR
    nsl = w_up_g.shape[0]
    wsl = w_up_g.shape[2]

    def body(h_ref, t_ref, wn_ref, wup_ref, cw_ref, cb_ref, wdn_ref, wf_ref,
             u2t_ref, upb_ref, ggt_ref, dh_ref, loss_ref, dwf_ref, up_scr, gg_ref, carry_ref):
        i = pl.program_id(0)

        @pl.when(i == 0)
        def _():
            carry_ref[...] = jnp.zeros_like(carry_ref)
            loss_ref[...] = jnp.zeros_like(loss_ref)
            dwf_ref[...] = jnp.zeros_like(dwf_ref)

        x = h_ref[...]
        r2 = lax.rsqrt(jnp.mean(x * x, axis=-1, keepdims=True) + EPS)
        u2_f = x * r2 * wn_ref[...]
        u2 = u2_f.astype(BF16)
        u2t_ref[...] = u2_f.T.astype(BF16)
        for s in range(nsl):
            up_s = jnp.dot(u2, wup_ref[s], preferred_element_type=F32)
            up_scr[:, s * wsl:(s + 1) * wsl] = up_s
            upb_ref[:, s * wsl:(s + 1) * wsl] = up_s.astype(BF16)
        w0, w1, w2 = cw_ref[0:1, :], cw_ref[1:2, :], cw_ref[2:3, :]

        def chunk(c, carry):
            rows = pl.ds(pl.multiple_of(c * CH, CH), CH)
            a_pre = up_scr[rows, 0:DFF]
            val = up_scr[rows, DFF:2 * DFF]
            prev8 = carry_ref[...]
            a = w0 * _shift_down(a_pre, prev8, 2) + w1 * _shift_down(a_pre, prev8, 1) + w2 * a_pre + cb_ref[...]
            carry_ref[...] = a_pre[CH - 8:CH, :]
            gg_ref[rows, :] = a * _sigmoid(a) * val
            return carry

        lax.fori_loop(0, TR // CH, chunk, 0, unroll=True)
        gg_v = gg_ref[...]
        ggt_ref[...] = gg_v.T.astype(BF16)
        h2 = x + jnp.dot(gg_v.astype(BF16), wdn_ref[...], preferred_element_type=F32)
        r3 = lax.rsqrt(jnp.mean(h2 * h2, axis=-1, keepdims=True) + EPS)
        n3 = h2 * r3
        wf = wf_ref[...]
        diff = jnp.where(i > 0, n3 * wf - t_ref[...], 0.0)
        loss_ref[...] += jnp.sum(diff * diff, axis=0, keepdims=True) * (0.5 / D)
        dy = diff * (1.0 / D)
        dwf_ref[...] += jnp.sum(dy * n3, axis=0, keepdims=True)
        dn = dy * wf
        dh_ref[...] = r3 * (dn - n3 * jnp.mean(dn * n3, axis=-1, keepdims=True))

    row = lambda w: pl.BlockSpec((TR, w), lambda i: (i, 0))
    col = lambda w: pl.BlockSpec((w, TR), lambda i: (0, i))
    vec = lambda w, r=1: pl.BlockSpec((r, w), lambda i: (0, 0))
    vm = pl.BlockSpec(memory_space=pltpu.VMEM)
    return pl.pallas_call(
        body, name="ffn_block_fwd", grid=(nt,),
        in_specs=[row(D), pl.BlockSpec((TR, D), lambda i: (jnp.maximum(i - 1, 0), 0)), vec(D), vm,
                  vec(DFF, 3), vec(DFF), vm, vec(D)],
        out_specs=[col(D), row(2 * DFF), col(DFF), row(D), vec(D), vec(D)],
        out_shape=[jax.ShapeDtypeStruct((D, T), BF16), jax.ShapeDtypeStruct((T, 2 * DFF), BF16),
                   jax.ShapeDtypeStruct((DFF, T), BF16), jax.ShapeDtypeStruct((T, D), F32),
                   jax.ShapeDtypeStruct((1, D), F32), jax.ShapeDtypeStruct((1, D), F32)],
        scratch_shapes=[pltpu.VMEM((TR, 2 * DFF), F32), pltpu.VMEM((TR, DFF), F32), pltpu.VMEM((8, DFF), F32)],
        compiler_params=_cparams(("arbitrary",)),
    )(h1, tgt, wffn, w_up_g, fcw, fcb, w_down, wfin)


def _ffn_block_bwd(dh2, upb, h1, wffn, w_up_g, fcw, fcb, w_down):
    T = h1.shape[0]
    nt = T // TR
    ncht = TR // CH
    nsl = w_up_g.shape[0]
    wsl = w_up_g.shape[2]
    tb16 = TR // 16
    assert ncht % nsl == 0
    every = ncht // nsl
    step = -(-DFF // (ncht * 128)) * 128
    parts = [(c0, min(c0 + step, DFF)) for c0 in range(0, DFF, step)]
    assert len(parts) == ncht

    def body(dh2n_ref, dh2p_ref, up_ref, pa_ref, h_ref, wn_ref, wup_ref, cw_ref, cb_ref, wdn_ref,
             dup_ref, dh1_ref, dfw_ref, dfb_ref, dwn_ref,
             dgg_ring, dup_ring, carry_ref, acc_w, acc_b, acc_n):
        s = pl.program_id(0)
        slot = lax.rem(s, 2)
        other = 1 - slot

        @pl.when(s == 0)
        def _():
            carry_ref[...] = jnp.zeros_like(carry_ref)
            acc_w[...] = jnp.zeros_like(acc_w)
            acc_b[...] = jnp.zeros_like(acc_b)
            acc_n[...] = jnp.zeros_like(acc_n)
            dup_ring[1] = jnp.zeros((TR, 2 * DFF), BF16)
            dgg_ring[0] = _dot_nt(dh2p_ref[...], wdn_ref[...])

        def norm_bwd(du2, valid):
            x = h_ref[...]
            dh2p = dh2p_ref[...]
            r2 = lax.rsqrt(jnp.mean(x * x, axis=-1, keepdims=True) + EPS)
            n2 = x * r2
            dn = du2 * wn_ref[...]
            dh1_ref[...] = dh2p + r2 * (dn - n2 * jnp.mean(dn * n2, axis=-1, keepdims=True))
            acc_n[...] += jnp.where(valid, jnp.sum(du2 * n2, axis=0, keepdims=True), 0.0)

        @pl.when(s < nt)
        def _():
            tile = nt - 1 - s
            w0, w1, w2 = cw_ref[0:1, :], cw_ref[1:2, :], cw_ref[2:3, :]
            a_before_tile = jnp.where(tile > 0, pa_ref[...].astype(F32)[8:16, :], 0.0)
            dh2n = dh2n_ref[...].astype(BF16)
            du2 = None
            for idx in range(ncht):
                c = ncht - 1 - idx
                r0 = c * CH
                rows = slice(r0, r0 + CH)
                a_pre = up_ref[rows, 0:DFF].astype(F32)
                val = up_ref[rows, DFF:2 * DFF].astype(F32)
                prev8 = up_ref[r0 - 16:r0, 0:DFF].astype(F32)[8:16, :] if c > 0 else a_before_tile
                a_m1 = _shift_down(a_pre, prev8, 1)
                a_m2 = _shift_down(a_pre, prev8, 2)
                a = w0 * a_m2 + w1 * a_m1 + w2 * a_pre + cb_ref[...]
                sig = _sigmoid(a)
                dgg_v = dgg_ring[slot, rows, :]
                da = dgg_v * val * (sig * (1.0 + a * (1.0 - sig)))
                dval = (dgg_v * (a * sig)).astype(BF16)
                next8 = carry_ref[...]
                da_pre = (w2 * da + w1 * _shift_up(da, next8, 1) + w0 * _shift_up(da, next8, 2)).astype(BF16)
                carry_ref[...] = da[0:8, :]
                dup_ref[rows, 0:DFF] = da_pre
                dup_ref[rows, DFF:2 * DFF] = dval
                dup_ring[slot, rows, 0:DFF] = da_pre
                dup_ring[slot, rows, DFF:2 * DFF] = dval
                acc_w[0:1, :] += jnp.sum(da * a_m2, axis=0, keepdims=True)
                acc_w[1:2, :] += jnp.sum(da * a_m1, axis=0, keepdims=True)
                acc_w[2:3, :] += jnp.sum(da * a_pre, axis=0, keepdims=True)
                acc_b[...] += jnp.sum(da, axis=0, keepdims=True)
                c0, c1 = parts[idx]
                dgg_ring[other, :, c0:c1] = _dot_nt(dh2n, wdn_ref[c0:c1, :])
                if idx % every == 0:
                    sl = idx // every
                    part = _dot_nt(dup_ring[other, :, sl * wsl:(sl + 1) * wsl], wup_ref[sl])
                    du2 = part if du2 is None else du2 + part
            norm_bwd(du2, s > 0)

        @pl.when(s == nt)
        def _():
            du2 = _dot_nt(dup_ring[other, :, 0:wsl], wup_ref[0])
            for sl in range(1, nsl):
                du2 = du2 + _dot_nt(dup_ring[other, :, sl * wsl:(sl + 1) * wsl], wup_ref[sl])
            norm_bwd(du2, True)
            dfw_ref[...] = acc_w[0:3, :]
            dfb_ref[...] = acc_b[...]
            dwn_ref[...] = acc_n[...]

    gate_tile = lambda s: jnp.maximum(nt - 1 - s, 0)
    next_tile = lambda s: jnp.maximum(nt - 2 - s, 0)
    prev_tile = lambda s: jnp.minimum(nt - s, nt - 1)
    vec = lambda w, r=1: pl.BlockSpec((r, w), lambda s: (0, 0))
    vm = pl.BlockSpec(memory_space=pltpu.VMEM)
    return pl.pallas_call(
        body, name="ffn_block_bwd", grid=(nt + 1,),
        in_specs=[pl.BlockSpec((TR, D), lambda s: (next_tile(s), 0)),
                  pl.BlockSpec((TR, D), lambda s: (prev_tile(s), 0)),
                  pl.BlockSpec((TR, 2 * DFF), lambda s: (gate_tile(s), 0)),
                  pl.BlockSpec((16, DFF), lambda s: (jnp.maximum(gate_tile(s) * tb16 - 1, 0), 0)),
                  pl.BlockSpec((TR, D), lambda s: (prev_tile(s), 0)),
                  vec(D), vm, vec(DFF, 3), vec(DFF), vm],
        out_specs=[pl.BlockSpec((TR, 2 * DFF), lambda s: (gate_tile(s), 0)),
                   pl.BlockSpec((TR, D), lambda s: (prev_tile(s), 0)),
                   vec(DFF, 3), vec(DFF), vec(D)],
        out_shape=[jax.ShapeDtypeStruct((T, 2 * DFF), BF16), jax.ShapeDtypeStruct((T, D), F32),
                   jax.ShapeDtypeStruct((3, DFF), F32), jax.ShapeDtypeStruct((1, DFF), F32),
                   jax.ShapeDtypeStruct((1, D), F32)],
        scratch_shapes=[pltpu.VMEM((2, TR, DFF), F32), pltpu.VMEM((2, TR, 2 * DFF), BF16),
                        pltpu.VMEM((8, DFF), F32), pltpu.VMEM((8, DFF), F32), pltpu.VMEM((1, DFF), F32),
                        pltpu.VMEM((1, D), F32)],
        compiler_params=_cparams(("arbitrary",)),
    )(dh2, dh2, upb, upb, h1, wffn, w_up_g, fcw, fcb, w_down)


def _place():
    x, y, c = lax.axis_index("x"), lax.axis_index("y"), lax.axis_index("c")
    return x, y, c


_CHIP_FLIPS = ((1, 0), (0, 1), (1, 1))


def _flip(v, bit):
    return 1 - v if bit else v


def _into_slot(w, j_idx, *, rb, dtype, name, paired=False):
    r, cdim = w.shape

    def body(j_ref, w_ref, out_ref):
        del j_ref
        out_ref[...] = w_ref[...].astype(dtype)

    if paired:
        out_shape = jax.ShapeDtypeStruct((NSHARD // 2, r, 2 * cdim), dtype)
        out_spec = pl.BlockSpec((None, rb, cdim), lambda i, j_ref: (j_ref[0] // 2, i, j_ref[0] % 2))
    else:
        out_shape = jax.ShapeDtypeStruct((NSHARD, r, cdim), dtype)
        out_spec = pl.BlockSpec((None, rb, cdim), lambda i, j_ref: (j_ref[0], i, 0))
    grid_spec = pltpu.PrefetchScalarGridSpec(
        num_scalar_prefetch=1, grid=(r // rb,),
        in_specs=[pl.BlockSpec((rb, cdim), lambda i, j_ref: (i, 0))], out_specs=out_spec)
    return pl.pallas_call(
        body, name=name, grid_spec=grid_spec, out_shape=out_shape, compiler_params=_cparams(("parallel",)),
    )(j_idx, w)


class _Gather:
    def __init__(self, outs, send_sems, recv_sems, whole, paired=None):
        self.outs, self.send_sems, self.recv_sems, self.whole = outs, send_sems, recv_sems, whole
        self.paired = paired if paired is not None else (False,) * len(outs)
        self.x, self.y, self.c = _place()
        self.j = 2 * self.x + self.y
        self.sibling = (self.x, self.y, 1 - self.c)
        chips = [(_flip(self.x, fx), _flip(self.y, fy)) for fx, fy in _CHIP_FLIPS]
        self.slots = [2 * px + py for px, py in chips]
        self.peers = [(px, py, self.c) for px, py in chips]

    def _copy(self, w, slot, core, sem, to, quarter=None):
        r = self.outs[w].shape[1]
        if self.whole[w]:
            rows = pl.ds(0, r)
        elif quarter is None:
            rows = pl.ds(core * (r // 2), r // 2)
        else:
            rows = pl.ds(core * (r // 2) + quarter * (r // 4), r // 4)
        if self.paired[w]:
            cw = self.outs[w].shape[2] // 2
            piece = self.outs[w].at[slot // 2, rows, pl.ds((slot % 2) * cw, cw)]
        else:
            piece = self.outs[w].at[slot, rows, :]
        return pltpu.make_async_remote_copy(
            src_ref=piece, dst_ref=piece, send_sem=self.send_sems.at[w, sem], recv_sem=self.recv_sems.at[w, sem],
            device_id=to, device_id_type=MESH)

    def _direct(self, w):
        return (0, 1, 2) if self.whole[w] else (0, 1)

    def send(self):
        for w in range(len(self.outs)):
            for k in self._direct(w):
                self._copy(w, self.j, self.c, k, self.peers[k]).start()

    def relay(self):
        jx, jy, _ = self.slots
        for w in range(len(self.outs)):
            if self.whole[w]:
                for k in range(3):
                    self._copy(w, self.slots[k], self.c, k, self.sibling).wait_recv()
                continue
            self._copy(w, jx, self.c, 0, self.sibling).wait_recv()
            self._copy(w, jx, self.c, 6, self.peers[1], quarter=0).start()
            self._copy(w, jx, self.c, 3, self.sibling).start()
            self._copy(w, jy, self.c, 1, self.sibling).wait_recv()
            self._copy(w, jy, self.c, 7, self.peers[0], quarter=1).start()
            self._copy(w, jy, self.c, 4, self.sibling).start()

    def relay_diagonal(self):
        jd = self.slots[2]
        for w in range(len(self.outs)):
            if not self.whole[w]:
                self._copy(w, jd, self.c, 6, self.sibling, quarter=0).wait_recv()
                self._copy(w, jd, self.c, 7, self.sibling, quarter=1).wait_recv()
                self._copy(w, jd, self.c, 5, self.sibling).start()

    def finish(self):
        jx, jy, jd = self.slots
        for w in range(len(self.outs)):
            for k in self._direct(w):
                self._copy(w, self.j, self.c, k, self.peers[k]).wait_send()
            if self.whole[w]:
                continue
            self._copy(w, jx, self.c, 6, self.peers[1], quarter=0).wait_send()
            self._copy(w, jy, self.c, 7, self.peers[0], quarter=1).wait_send()
            for k, slot in enumerate(self.slots):
                self._copy(w, slot, 1 - self.c, 3 + k, self.sibling).wait_recv()
                self._copy(w, slot, self.c, 3 + k, self.sibling).wait_send()


def _allgather_weights(slotted, whole):
    n = len(slotted)

    def body(*refs):
        g = _Gather(refs[n:2 * n], refs[2 * n], refs[2 * n + 1], whole)
        g.send()
        g.relay()
        g.relay_diagonal()
        g.finish()

    any_spec = pl.BlockSpec(memory_space=pl.ANY)
    return pl.pallas_call(
        body, name="allgather_weights",
        in_specs=[any_spec] * n, out_specs=[any_spec] * n,
        out_shape=[jax.ShapeDtypeStruct(a.shape, a.dtype) for a in slotted],
        input_output_aliases={i: i for i in range(n)},
        scratch_shapes=[pltpu.SemaphoreType.DMA((n, 8)), pltpu.SemaphoreType.DMA((n, 8))],
    )(*slotted)


class _ChipExchange:
    def __init__(self, ins, outs, send_sems, recv_sems):
        self.ins, self.outs, self.send_sems, self.recv_sems = ins, outs, send_sems, recv_sems
        self.x, self.y, self.c = _place()

    def _copies(self):
        for w in range(len(self.ins)):
            for kk, (fx, fy) in enumerate(_CHIP_FLIPS):
                px, py = _flip(self.x, fx), _flip(self.y, fy)
                yield pltpu.make_async_remote_copy(
                    src_ref=self.ins[w].at[2 * px + py], dst_ref=self.outs[w].at[kk],
                    send_sem=self.send_sems.at[w, kk], recv_sem=self.recv_sems.at[w, kk],
                    device_id=(px, py, self.c), device_id_type=MESH)

    def send(self):
        for cp in self._copies():
            cp.start()

    def finish(self):
        for cp in self._copies():
            cp.wait()


class _PairExchange:
    def __init__(self, ins, outs, send_sems, recv_sems):
        self.ins, self.outs, self.send_sems, self.recv_sems = ins, outs, send_sems, recv_sems
        self.x, self.y, self.c = _place()

    def _copies(self):
        for w in range(len(self.ins)):
            half = self.ins[w].shape[1] // 2
            yield pltpu.make_async_remote_copy(
                src_ref=self.ins[w].at[:, pl.ds((1 - self.c) * half, half), :], dst_ref=self.outs[w],
                send_sem=self.send_sems.at[w], recv_sem=self.recv_sems.at[w],
                device_id=(self.x, self.y, 1 - self.c), device_id_type=MESH)

    def send(self):
        for cp in self._copies():
            cp.start()

    def finish(self):
        for cp in self._copies():
            cp.wait()


def _pair_shapes(grads):
    return [jax.ShapeDtypeStruct((g.shape[0], g.shape[1] // 2, g.shape[2]), g.dtype) for g in grads]


def _pair_exchange(grads, name):
    nw = len(grads)

    def body(*refs):
        ex = _PairExchange(refs[:nw], refs[nw:2 * nw], refs[2 * nw], refs[2 * nw + 1])
        ex.send()
        ex.finish()

    any_spec = pl.BlockSpec(memory_space=pl.ANY)
    return pl.pallas_call(
        body, name=name,
        in_specs=[any_spec] * nw, out_specs=[any_spec] * nw, out_shape=_pair_shapes(grads),
        scratch_shapes=[pltpu.SemaphoreType.DMA((nw,)), pltpu.SemaphoreType.DMA((nw,))],
    )(*grads)


def _pair_add(g, other, c_idx, *, rb, name):
    S, r, cdim = g.shape
    half = r // 2
    nb = half // rb

    def body(c_ref, g_ref, o_ref, out_ref):
        del c_ref
        out_ref[...] = (g_ref[...].astype(F32) + o_ref[...].astype(F32)).astype(BF16)

    grid_spec = pltpu.PrefetchScalarGridSpec(
        num_scalar_prefetch=1, grid=(S, nb),
        in_specs=[pl.BlockSpec((None, rb, cdim), lambda s, i, c_ref: (s, c_ref[0] * nb + i, 0)),
                  pl.BlockSpec((None, rb, cdim), lambda s, i, c_ref: (s, i, 0))],
        out_specs=pl.BlockSpec((None, rb, cdim), lambda s, i, c_ref: (s, i, 0)))
    return pl.pallas_call(
        body, name=name, grid_spec=grid_spec, out_shape=jax.ShapeDtypeStruct((S, half, cdim), BF16),
        compiler_params=_cparams(("parallel", "parallel")),
    )(c_idx, g, other)


def _chip_sum(psum, parts, cj_idx, *, rb, name):
    S, half, cdim = psum.shape
    nb = half // rb

    def body(cj_ref, own_ref, p_ref, out_ref):
        del cj_ref
        f = lambda v: v.astype(F32)
        out_ref[...] = ((f(own_ref[...]) + f(p_ref[0])) + f(p_ref[1])) + f(p_ref[2])

    grid_spec = pltpu.PrefetchScalarGridSpec(
        num_scalar_prefetch=1, grid=(nb,),
        in_specs=[pl.BlockSpec((None, rb, cdim), lambda i, cj: (cj[1], i, 0)),
                  pl.BlockSpec((3, rb, cdim), lambda i, cj: (0, i, 0))],
        out_specs=pl.BlockSpec((rb, cdim), lambda i, cj: (cj[0] * nb + i, 0)))
    return pl.pallas_call(
        body, name=name, grid_spec=grid_spec, out_shape=jax.ShapeDtypeStruct((2 * half, cdim), F32),
        compiler_params=_cparams(("parallel",)),
    )(cj_idx, psum, parts)


SLAB_W = 1024


def _final_exchange(pieces, grads):
    n, nw = len(pieces), len(grads)
    segs, at = [], 0
    for idx, p in enumerate(pieces):
        r, wd = p.shape
        for c0 in range(0, wd, SLAB_W):
            if r > 1:
                at = -(-at // 8) * 8
            segs.append((idx, c0, min(SLAB_W, wd - c0), at))
            at += r
    rows = -(-at // 8) * 8
    flips = [(fx, fy, fc) for fx in (0, 1) for fy in (0, 1) for fc in (0, 1)][1:]

    def body(*refs):
        ins = refs[:n]
        outs = refs[n + nw:2 * n + nw]
        g_refs = refs[2 * n + nw:2 * n + 2 * nw]
        mine_ref, slots_ref, send_sems, recv_sems, gsend_sems, grecv_sems = refs[2 * n + 2 * nw:]
        x, y, c = _place()
        me = 4 * x + 2 * y + c
        sibling = (x, y, 1 - c)

        def swap(w, core):
            half = g_refs[w].shape[0] // 2
            rows_ = g_refs[w].at[pl.ds(core * half, half), :]
            return pltpu.make_async_remote_copy(
                src_ref=rows_, dst_ref=rows_, send_sem=gsend_sems.at[w], recv_sem=grecv_sems.at[w],
                device_id=sibling, device_id_type=MESH)

        for w in range(nw):
            swap(w, c).start()
        mine_ref[...] = jnp.zeros_like(mine_ref)
        for idx, c0, wd, st in segs:
            r = ins[idx].shape[0]
            mine_ref[st:st + r, 0:wd] = ins[idx][:, c0:c0 + wd]
        slots_ref[me] = mine_ref[...]
        cps = []
        for kk, (fx, fy, fc) in enumerate(flips):
            cp = pltpu.make_async_remote_copy(
                src_ref=mine_ref, dst_ref=slots_ref.at[me], send_sem=send_sems.at[kk], recv_sem=recv_sems.at[kk],
                device_id=(_flip(x, fx), _flip(y, fy), _flip(c, fc)), device_id_type=MESH)
            cp.start()
            cps.append(cp)
        for cp in cps:
            cp.wait()
        tot = slots_ref[0]
        for d in range(1, 8):
            tot = tot + slots_ref[d]
        mine_ref[...] = tot
        for idx, c0, wd, st in segs:
            r = ins[idx].shape[0]
            val = mine_ref[st:st + r, 0:wd]
            if idx == n - 1:
                outs[idx][...] = jnp.sum(val, keepdims=True)
            else:
                outs[idx][:, c0:c0 + wd] = val
        for w in range(nw):
            swap(w, 1 - c).wait_recv()
            swap(w, c).wait_send()

    vm = pl.BlockSpec(memory_space=pltpu.VMEM)
    hbm = pl.BlockSpec(memory_space=pl.ANY)
    out_shape = ([jax.ShapeDtypeStruct(p.shape, F32) for p in pieces[:-1]] + [jax.ShapeDtypeStruct((1, 1), F32)]
                 + [jax.ShapeDtypeStruct(g.shape, g.dtype) for g in grads])
    res = pl.pallas_call(
        body, name="final_exchange", in_specs=[vm] * n + [hbm] * nw, out_specs=[vm] * n + [hbm] * nw,
        out_shape=out_shape, input_output_aliases={n + w: n + w for w in range(nw)},
        scratch_shapes=[pltpu.VMEM((rows, SLAB_W), F32), pltpu.VMEM((8, rows, SLAB_W), F32),
                        pltpu.SemaphoreType.DMA((7,)), pltpu.SemaphoreType.DMA((7,)),
                        pltpu.SemaphoreType.DMA((nw,)), pltpu.SemaphoreType.DMA((nw,))],
    )(*pieces, *grads)
    return res[:n], res[n:]


def _adamw(w, g, m, v, *, rb, name):
    r, cdim = w.shape

    def body(w_ref, g_ref, m_ref, v_ref, go_ref, d_ref, nm_ref, nv_ref):
        go_ref[...] = g_ref[...]
        d_ref[...], nm_ref[...], nv_ref[...] = _adamw_update(w_ref[...], g_ref[...], m_ref[...], v_ref[...])

    spec = pl.BlockSpec((rb, cdim), lambda i: (i, 0))
    shp = jax.ShapeDtypeStruct((r, cdim), F32)
    return pl.pallas_call(
        body, name=name, grid=(r // rb,), in_specs=[spec] * 4, out_specs=[spec] * 4, out_shape=[shp] * 4,
        compiler_params=_cparams(("parallel",)),
    )(w, g, m, v)


def _adamw_update(w, g, m, v):
    nm = ADAM_B1 * m + (1.0 - ADAM_B1) * g
    nv = ADAM_B2 * v + (1.0 - ADAM_B2) * (g * g)
    m_hat = nm / (1.0 - ADAM_B1 ** ADAM_STEP)
    v_hat = nv / (1.0 - ADAM_B2 ** ADAM_STEP)
    return -ADAM_LR * (m_hat / (jnp.sqrt(v_hat) + ADAM_EPS) + ADAM_WD * w), nm, nv


def _adamw_small(params):
    n = len(params)

    def body(*refs):
        ins, outs = refs[:4 * n], refs[4 * n:]
        for p in range(n):
            w_ref, g_ref, m_ref, v_ref = ins[4 * p:4 * p + 4]
            d, nm, nv = _adamw_update(w_ref[...], g_ref[...], m_ref[...], v_ref[...])
            outs[3 * p][...] = d
            outs[3 * p + 1][...] = nm
            outs[3 * p + 2][...] = nv

    vm = pl.BlockSpec(memory_space=pltpu.VMEM)
    flat = [a for p in params for a in p]
    out_shape = [jax.ShapeDtypeStruct(p[0].shape, F32) for p in params for _ in range(3)]
    res = pl.pallas_call(
        body, name="adamw_small", in_specs=[vm] * (4 * n), out_specs=[vm] * (3 * n), out_shape=out_shape,
    )(*flat)
    return [tuple(res[3 * p:3 * p + 3]) for p in range(n)]


_PAIR_ADD_ROWS = {"w_in": 256, "w_out": 128, "w_up": 256, "w_down": 176}


def _pair_sums(grads, others, names, c_idx):
    return [_pair_add(g, o_, c_idx, rb=_PAIR_ADD_ROWS[n], name=f"pair_add_{n}") for g, o_, n in zip(grads, others, names)]


def _local_step(x, tgt, meta_full, lb_param, attn_norm_w, w_in_g, hgrn_norm_w, conv_w_full, w_out_full,
                ffn_norm_w, late_slotted, fcw_full, ffn_conv_b, final_norm_w, c_idx, cj_idx):
    seq = x.shape[0]
    T = TR + seq
    head_tile = jnp.concatenate([jnp.zeros((PAD, D), F32), meta_full], axis=0)
    whn_t = jnp.tile(hgrn_norm_w, (1, NH))

    ut, projb, o, sst, mt, h1, w_up_g, w_down_g = _mix_block_fwd(
        x, head_tile, lb_param, attn_norm_w, whn_t, conv_w_full, w_in_g, w_out_full, late_slotted, (True, False))
    w_down_full = w_down_g.reshape(DFF, D)
    u2t, upb, ggt, dh2, loss_vec, dwfin = _ffn_block_fwd(
        h1, tgt, ffn_norm_w, w_up_g, fcw_full, ffn_conv_b, w_down_full, final_norm_w.reshape(1, D))

    dup, dh1, dfw, dfb, dwffn = _ffn_block_bwd(
        dh2, upb, h1, ffn_norm_w, w_up_g, fcw_full, ffn_conv_b, w_down_full)
    kb = 1408 if T % 1408 == 0 else TR
    kb_deep = 2816 if T % 2816 == 0 else kb
    g_up = _weight_grad(u2t, dup, bm=D // 2, bn=DFF, bk=kb_deep, name="dw_up_mm", shard_cols=2 * DFF // NSHARD)
    g_down, sib_up = _weight_grad(ggt, dh2, bn=D // 2, bk=kb_deep, name="dw_down_mm", ride=[g_up])
    g_down = g_down.reshape(NSHARD, DFF // NSHARD, D)
    sib_down = _pair_exchange([g_down], name="grad_pair_exchange_down")
    ps_ffn = _pair_sums([g_up, g_down], sib_up + list(sib_down), ("w_up", "w_down"), c_idx)

    dproj, dlb, dwhn, dcw, *parts_ffn = _mix_block_bwd(
        dh1, projb, o, sst, lb_param, whn_t, conv_w_full, w_out_full, ps_ffn)
    g_in = _weight_grad(ut, dproj, bn=9 * D // NSHARD, bk=kb_deep, name="dw_in_mm", shard_cols=9 * D // NSHARD)
    g_out, sib_in = _weight_grad(mt, dh1, bn=D, bk=kb_deep, name="dw_out_mm", ride=[g_in])
    g_out = g_out.reshape(NSHARD, D // NSHARD, D)
    sib_out = _pair_exchange([g_out], name="grad_pair_exchange_out")
    ps_mix = _pair_sums([g_in, g_out], sib_in + list(sib_out), ("w_in", "w_out"), c_idx)

    grad_x, dmeta, dwattn, *parts_mix = _input_grad_block(dproj, x, head_tile, dh1, attn_norm_w, w_in_g, ps_mix)

    halves = [_chip_sum(ps, p, cj_idx, rb=_PAIR_ADD_ROWS[n], name=f"chip_sum_{n}")
              for ps, p, n in zip(ps_mix + ps_ffn, parts_mix + parts_ffn, ("w_in", "w_out", "w_up", "w_down"))]
    small = dict(dlb=dlb, dwattn=dwattn, dwhn=dwhn, dwffn=dwffn, dfb=dfb, dwfin=dwfin,
                 dcw=dcw, dfw=dfw, dmeta=dmeta, loss=loss_vec)
    return grad_x, small, halves


_SMALL_ORDER = ("dmeta", "dcw", "dfw", "dlb", "dwattn", "dwhn", "dwffn", "dfb", "dwfin", "loss")


def kernel(x, meta_tokens, lb_param, attn_norm_w, w_in, hgrn_norm_w, conv_w, w_out, ffn_norm_w, w_up, ffn_conv_w, ffn_conv_b, w_down, final_norm_w, loss_target, m_meta_tokens, m_lb_param, m_attn_norm_w, m_w_in, m_hgrn_norm_w, m_conv_w, m_w_out, m_ffn_norm_w, m_w_up, m_ffn_conv_w, m_ffn_conv_b, m_w_down, m_final_norm_w, v_meta_tokens, v_lb_param, v_attn_norm_w, v_w_in, v_hgrn_norm_w, v_conv_w, v_w_out, v_ffn_norm_w, v_w_up, v_ffn_conv_w, v_ffn_conv_b, v_w_down, v_final_norm_w):
    xi, yi, ci = _place()
    j = 2 * xi + yi
    c_idx = jnp.reshape(ci, (1,)).astype(jnp.int32)

    j_idx = jnp.reshape(j, (1,)).astype(jnp.int32)
    ds_, fs_ = D // NSHARD, DFF // NSHARD
    widen = lambda a: jnp.pad(a, ((0, 0), (0, 768 - a.shape[1])))
    rows_small = jnp.concatenate([widen(meta_tokens), widen(conv_w[0]), widen(ffn_conv_w[0]),
                                  jnp.zeros((2, 768), F32)], axis=0)
    s_in, s_out, s_up, s_down = [
        _into_slot(w[0], j_idx, rb=rb, dtype=BF16, name=f"slot_{n}", paired=pr)
        for w, rb, n, pr in ((w_in, 256, "w_in", False), (w_out, 128, "w_out", False), (w_up, 256, "w_up", True),
                             (w_down, 176, "w_down", False))]
    s_small = _into_slot(rows_small, j_idx, rb=rows_small.shape[0], dtype=F32, name="slot_small")
    w_in_g, w_out_g, small_g = _allgather_weights([s_in, s_out, s_small], (False, False, True))
    unshard = lambda a: jnp.transpose(a, (1, 0, 2)).reshape(a.shape[1], -1)
    meta_full = unshard(small_g[:, 0:NMETA, 0:ds_])
    conv_w_full = unshard(small_g[:, NMETA:NMETA + 3, 0:ds_])
    fcw_full = unshard(small_g[:, NMETA + 3:NMETA + 6, 0:fs_])

    cj_idx = jnp.stack([ci, j]).astype(jnp.int32)
    grad_x, small, halves = _local_step(
        x[0], loss_target[0], meta_full, lb_param, attn_norm_w, w_in_g, hgrn_norm_w, conv_w_full,
        w_out_g.reshape(D, D), ffn_norm_w, [s_up, s_down], fcw_full, ffn_conv_b, final_norm_w, c_idx, cj_idx)

    names = _SMALL_ORDER
    small_sums, g_big = _final_exchange([small[n] for n in names], halves)
    vals = dict(zip(names, small_sums))
    loss = vals["loss"].reshape(())
    g_small = {
        "meta_tokens": lax.dynamic_slice_in_dim(vals["dmeta"], j * (D // NSHARD), D // NSHARD, axis=1),
        "lb_param": jnp.concatenate([vals["dlb"], -vals["dlb"]], axis=0),
        "attn_norm_w": vals["dwattn"],
        "hgrn_norm_w": vals["dwhn"],
        "conv_w": lax.dynamic_slice_in_dim(vals["dcw"], j * (D // NSHARD), D // NSHARD, axis=1)[None],
        "ffn_norm_w": vals["dwffn"],
        "ffn_conv_w": lax.dynamic_slice_in_dim(vals["dfw"], j * (DFF // NSHARD), DFF // NSHARD, axis=1)[None],
        "ffn_conv_b": vals["dfb"],
        "final_norm_w": vals["dwfin"].reshape(D),
    }


    weights = {"meta_tokens": meta_tokens, "lb_param": lb_param, "attn_norm_w": attn_norm_w, "w_in": w_in,
               "hgrn_norm_w": hgrn_norm_w, "conv_w": conv_w, "w_out": w_out, "ffn_norm_w": ffn_norm_w,
               "w_up": w_up, "ffn_conv_w": ffn_conv_w, "ffn_conv_b": ffn_conv_b, "w_down": w_down,
               "final_norm_w": final_norm_w}
    ms = {"meta_tokens": m_meta_tokens, "lb_param": m_lb_param, "attn_norm_w": m_attn_norm_w, "w_in": m_w_in,
          "hgrn_norm_w": m_hgrn_norm_w, "conv_w": m_conv_w, "w_out": m_w_out, "ffn_norm_w": m_ffn_norm_w,
          "w_up": m_w_up, "ffn_conv_w": m_ffn_conv_w, "ffn_conv_b": m_ffn_conv_b, "w_down": m_w_down,
          "final_norm_w": m_final_norm_w}
    vs = {"meta_tokens": v_meta_tokens, "lb_param": v_lb_param, "attn_norm_w": v_attn_norm_w, "w_in": v_w_in,
          "hgrn_norm_w": v_hgrn_norm_w, "conv_w": v_conv_w, "w_out": v_w_out, "ffn_norm_w": v_ffn_norm_w,
          "w_up": v_w_up, "ffn_conv_w": v_ffn_conv_w, "ffn_conv_b": v_ffn_conv_b, "w_down": v_w_down,
          "final_norm_w": v_final_norm_w}
    order = list(weights)
    grads, deltas, new_m, new_v = {}, {}, {}, {}

    for name, g, rb in zip(("w_in", "w_out", "w_up", "w_down"), g_big, (256, 128, 256, 176)):
        shp = weights[name].shape
        w2, m2, v2 = (a.reshape(shp[1], shp[2]) for a in (weights[name], ms[name], vs[name]))
        g_, d_, nm_, nv_ = _adamw(w2, g, m2, v2, rb=rb, name=f"adamw_{name}")
        grads[name], deltas[name], new_m[name], new_v[name] = (a.reshape(shp) for a in (g_, d_, nm_, nv_))

    small_names = [n for n in order if n not in grads]
    as2d = lambda a: a.reshape(-1, a.shape[-1])
    res = _adamw_small([tuple(as2d(a) for a in (weights[n], g_small[n], ms[n], vs[n])) for n in small_names])
    for n, (d_, nm_, nv_) in zip(small_names, res):
        shp = weights[n].shape
        grads[n], deltas[n], new_m[n], new_v[n] = (a.reshape(shp) for a in (g_small[n], d_, nm_, nv_))

    return (loss, grad_x[None], *[grads[n] for n in order], *[deltas[n] for n in order],
            *[new_m[n] for n in order], *[new_v[n] for n in order])
```

```python
import jax
import jax.numpy as jnp
from jax import lax
from jax.experimental import pallas as pl
from jax.experimental.pallas import tpu as pltpu

F32 = jnp.float32
BF16 = jnp.bfloat16
MESH = pl.DeviceIdType.MESH

D = 1024
NH = 8
HD = 128
DFF = 2816
NMETA = 16
EPS = 1e-6
TR = 256
PAD = TR - NMETA
CH = 64
NSHARD = 4
VMEM_LIMIT = 62 * 1024 * 1024

ADAM_LR = 0.001
ADAM_B1 = 0.9
ADAM_B2 = 0.999
ADAM_EPS = 1e-08
ADAM_WD = 0.01
ADAM_STEP = 10


def _cparams(semantics=None, **kw):
    return pltpu.CompilerParams(dimension_semantics=semantics, vmem_limit_bytes=VMEM_LIMIT, **kw)


def _sigmoid(x):
    return 0.5 * jnp.tanh(0.5 * x) + 0.5


def _weight_grad(at, b, *, bn, bk, name, shard_cols=None, ride=()):
    M, T = at.shape
    N = b.shape[1]
    assert b.shape[0] == T and T % bk == 0 and N % bn == 0, (name, M, N, T, bn, bk)
    nk = T // bk
    nj = N // bn
    nex = len(ride)
    if shard_cols is None:
        nsh = 1
        blk = (M, bn)
        out_shape = jax.ShapeDtypeStruct((M, N), BF16)
        out_spec = pl.BlockSpec(blk, lambda j, k: (0, j))
    else:
        assert bn % shard_cols == 0 and N % shard_cols == 0
        nsh = bn // shard_cols
        blk = (nsh, M, shard_cols)
        out_shape = jax.ShapeDtypeStruct((N // shard_cols, M, shard_cols), BF16)
        out_spec = pl.BlockSpec(blk, lambda j, k: (j, 0, 0))

    def body(*refs):
        a_ref, b_ref = refs[:2]
        o_ref = refs[2 + nex]
        acc_ref = refs[3 + 2 * nex]
        j, k = pl.program_id(0), pl.program_id(1)
        if nex:
            exchange = _PairExchange(refs[2:2 + nex], refs[3 + nex:3 + 2 * nex], refs[-2], refs[-1])

            @pl.when(jnp.logical_and(j == 0, k == 0))
            def _():
                exchange.send()

        @pl.when(k == 0)
        def _():
            acc_ref[...] = jnp.zeros_like(acc_ref)

        p = jnp.dot(a_ref[...].astype(BF16), b_ref[...].astype(BF16), preferred_element_type=F32)
        if shard_cols is None:
            acc_ref[...] += p
        else:
            for q in range(nsh):
                acc_ref[q] += p[:, q * shard_cols:(q + 1) * shard_cols]

        @pl.when(k == nk - 1)
        def _():
            o_ref[...] = acc_ref[...].astype(BF16)

        if nex:
            @pl.when(jnp.logical_and(j == nj - 1, k == nk - 1))
            def _():
                exchange.finish()

    hbm = pl.BlockSpec(memory_space=pl.ANY)
    scratch = [pltpu.VMEM(blk, F32)]
    if nex:
        scratch += [pltpu.SemaphoreType.DMA((nex,)), pltpu.SemaphoreType.DMA((nex,))]
    res = pl.pallas_call(
        body, name=name, grid=(nj, nk),
        in_specs=[pl.BlockSpec((M, bk), lambda j, k: (0, k)), pl.BlockSpec((bk, bn), lambda j, k: (k, j))] + [hbm] * nex,
        out_specs=[out_spec] + [hbm] * nex, out_shape=[out_shape] + _pair_shapes(ride), scratch_shapes=scratch,
        compiler_params=_cparams(("arbitrary", "arbitrary") if nex else ("parallel", "arbitrary")),
    )(at, b, *ride)
    return (res[0], list(res[1:])) if nex else res[0]


def _input_grad_block(dproj, x_seq, head_tile, dres, w, w_in_g, psums):
    T = TR + x_seq.shape[0]
    nt = T // TR
    nsl = w_in_g.shape[0]
    wsl = w_in_g.shape[2]
    nex = len(psums)

    def body(*refs):
        dp_ref, x_ref, head_ref, dres_ref, w_ref, win_ref = refs[:6]
        gx_ref, dmeta_ref, dw_ref = refs[6 + nex:9 + nex]
        exchange = _ChipExchange(refs[6:6 + nex], refs[9 + nex:9 + 2 * nex], refs[-2], refs[-1])
        i = pl.program_id(0)

        @pl.when(i == 0)
        def _():
            exchange.send()

        @pl.when(i == nt - 1)
        def _():
            exchange.finish()

        x = jnp.where(i == 0, head_ref[...], x_ref[...])
        r = lax.rsqrt(jnp.mean(x * x, axis=-1, keepdims=True) + EPS)
        n = x * r
        du_v = None
        for s in range(nsl):
            part = _dot_nt(dp_ref[:, s * wsl:(s + 1) * wsl], win_ref[s])
            du_v = part if du_v is None else du_v + part
        dn = du_v * w_ref[...]
        dh = dres_ref[...] + r * (dn - n * jnp.mean(dn * n, axis=-1, keepdims=True))
        gx_ref[...] = dh
        part = jnp.sum(du_v * n, axis=0, keepdims=True)

        @pl.when(i == 0)
        def _():
            dmeta_ref[...] = dh[PAD:TR, :]
            dw_ref[...] = part

        @pl.when(i > 0)
        def _():
            dw_ref[...] += part

    row = pl.BlockSpec((TR, D), lambda i: (i, 0))
    vec = pl.BlockSpec((1, D), lambda i: (0, 0))
    hbm = pl.BlockSpec(memory_space=pl.ANY)
    return pl.pallas_call(
        body, name="input_grad_block", grid=(nt,),
        in_specs=[pl.BlockSpec((TR, 9 * D), lambda i: (i, 0)),
                  pl.BlockSpec((TR, D), lambda i: (jnp.maximum(i - 1, 0), 0)), pl.BlockSpec((TR, D), lambda i: (0, 0)),
                  row, vec, pl.BlockSpec(memory_space=pltpu.VMEM)] + [hbm] * nex,
        out_specs=[pl.BlockSpec((TR, D), lambda i: (jnp.maximum(i - 1, 0), 0)),
                   pl.BlockSpec((NMETA, D), lambda i: (0, 0)), vec] + [hbm] * nex,
        out_shape=[jax.ShapeDtypeStruct((T - TR, D), F32), jax.ShapeDtypeStruct((NMETA, D), F32),
                   jax.ShapeDtypeStruct((1, D), F32)]
        + [jax.ShapeDtypeStruct((3,) + p.shape[1:], p.dtype) for p in psums],
        scratch_shapes=[pltpu.SemaphoreType.DMA((nex, 3)), pltpu.SemaphoreType.DMA((nex, 3))],
        compiler_params=_cparams(("arbitrary",)),
    )(dproj, x_seq, head_tile, dres, w, w_in_g, *psums)


def _hand_on_in_place(a):
    def body(a_ref, o_ref):
        o_ref[...] = a_ref[...]

    first = pl.BlockSpec((TR, D), lambda i: (0, 0))
    return pl.pallas_call(
        body, name="grad_x_hand_on", grid=(1,), in_specs=[first], out_specs=first,
        out_shape=jax.ShapeDtypeStruct(a.shape, a.dtype), input_output_aliases={0: 0},
        compiler_params=_cparams(("arbitrary",)),
    )(a)


def _tri_matmul(tri_bf16, x):
    hi = x.astype(BF16)
    lo = (x - hi.astype(F32)).astype(BF16)
    return jnp.dot(tri_bf16, lo, preferred_element_type=F32) + jnp.dot(tri_bf16, hi, preferred_element_type=F32)


def _shift_down(x, prev8, n):
    rows = x.shape[0]
    return pltpu.roll(jnp.concatenate([prev8, x], axis=0), n, 0)[8:8 + rows, :]


def _shift_up(x, next8, n):
    rows = x.shape[0]
    return pltpu.roll(jnp.concatenate([x, next8], axis=0), rows + 8 - n, 0)[0:rows, :]


def _gates(f_raw, lb):
    sg = _sigmoid(f_raw)
    f = lb + (1.0 - lb) * sg
    return sg, f, jnp.log(f), 1.0 - f


def _lower_bound(lbp_ref):
    return _sigmoid(lbp_ref[0:1, :] - lbp_ref[1:2, :])


def _tri_masks():
    r = lax.broadcasted_iota(jnp.int32, (CH, CH), 0)
    c = lax.broadcasted_iota(jnp.int32, (CH, CH), 1)
    return r >= c, r <= c


def _ones_where(mask):
    return jnp.where(mask, 1.0, 0.0).astype(BF16)


def _dot(a, b):
    return jnp.dot(a.astype(BF16), b.astype(BF16), preferred_element_type=F32)


def _dot_nt(a, b):
    return lax.dot_general(a.astype(BF16), b.astype(BF16), (((1,), (1,)), ((), ())), preferred_element_type=F32)


def _dot_tn(a, b):
    return lax.dot_general(a.astype(BF16), b.astype(BF16), (((0,), (0,)), ((), ())), preferred_element_type=F32)


def _mix_block_fwd(x_seq, head_tile, lb_param, wattn, whn, conv_w, w_in_g, w_out, late, late_paired):
    T = TR + x_seq.shape[0]
    nt = T // TR
    ncht = TR // CH
    nsl = w_in_g.shape[0]
    wsl = w_in_g.shape[2]
    nlate = len(late)

    def body(*refs):
        x_ref, head_ref, lbp_ref, wa_ref, whn_ref, cw_ref, win_ref, wout_ref = refs[:8]
        ut_ref, projb_ref, o_ref, sst_ref, mt_ref, h1_ref = refs[8 + nlate:14 + nlate]
        late_refs = refs[14 + nlate:14 + 2 * nlate]
        proj_ref, m_ref, st_ref, cxc_ref, send_sems, recv_sems = refs[14 + 2 * nlate:]
        i = pl.program_id(0)
        gather = _Gather(late_refs, send_sems, recv_sems, (False,) * nlate, late_paired)

        @pl.when(i == 0)
        def _():
            st_ref[...] = jnp.zeros_like(st_ref)
            cxc_ref[...] = jnp.zeros_like(cxc_ref)
            gather.send()

        @pl.when(i == nt // 2)
        def _():
            gather.relay()

        @pl.when(i == (3 * nt) // 4)
        def _():
            gather.relay_diagonal()

        x = jnp.where(i == 0, head_ref[...], x_ref[...])
        r1 = lax.rsqrt(jnp.mean(x * x, axis=-1, keepdims=True) + EPS)
        u_f = x * r1 * wa_ref[...]
        u = u_f.astype(BF16)
        ut_ref[...] = u_f.T.astype(BF16)
        for s in range(nsl):
            cols = slice(s * wsl, (s + 1) * wsl)
            p = jnp.dot(u, win_ref[s], preferred_element_type=F32)
            proj_ref[:, cols] = p
            projb_ref[:, cols] = p.astype(BF16)

        lb = _lower_bound(lbp_ref)
        lower, _ = _tri_masks()
        ltri = _ones_where(lower)
        whn_v = whn_ref[...]
        w0, w1, w2 = cw_ref[0:1, :], cw_ref[1:2, :], cw_ref[2:3, :]

        def gates_and_conv(c):
            rows = slice(c * CH, (c + 1) * CH)
            q_raw = proj_ref[rows, 0:D]
            f_raw = proj_ref[rows, D:2 * D]
            q = q_raw * _sigmoid(q_raw)
            _, _, g, k = _gates(f_raw, lb)
            gam = _tri_matmul(ltri, g)
            gam_l = gam[CH - 1:CH, :]
            g_out = proj_ref[rows, 3 * D:4 * D]
            cx = proj_ref[rows, 5 * D:6 * D] * proj_ref[rows, 6 * D:7 * D]
            prev8 = cxc_ref[...]
            cv = w0 * _shift_down(cx, prev8, 2) + w1 * _shift_down(cx, prev8, 1) + w2 * cx
            cxc_ref[...] = cx[CH - 8:CH, :]
            y_b = proj_ref[rows, 4 * D:5 * D] * cv
            return dict(
                e_l=jnp.exp(gam_l), qt=(q * jnp.exp(gam)).astype(BF16), kt=(k * jnp.exp(-gam)).astype(BF16),
                khat=(k * jnp.exp(gam_l - gam)).astype(BF16), vb=proj_ref[rows, 2 * D:3 * D].astype(BF16),
                out_gate=whn_v * (g_out * _sigmoid(g_out)), merge_a=_sigmoid(proj_ref[rows, 7 * D:8 * D]),
                merged_b=_sigmoid(proj_ref[rows, 8 * D:9 * D]) * y_b)

        def recurrence_and_merge(c, pre):
            rows = slice(c * CH, (c + 1) * CH)
            qt, kt, khat, vb, e_l = pre["qt"], pre["kt"], pre["khat"], pre["vb"], pre["e_l"]
            heads = [slice(h * HD, (h + 1) * HD) for h in range(NH)]
            sts = [st_ref[h] for h in range(NH)]
            scores = [_dot_nt(qt[:, cs], kt[:, cs]) for cs in heads]
            o_state = [_dot_nt(qt[:, cs], st) for cs, st in zip(heads, sts)]
            st_new = [_dot_tn(vb[:, cs], khat[:, cs]) for cs in heads]
            on_parts = []
            for h, cs in enumerate(heads):
                sst_ref[c, h] = sts[h]
                st_ref[h] = sts[h] * e_l[:, cs] + st_new[h]
                o_h = o_state[h] + _dot(jnp.where(lower, scores[h], 0.0), vb[:, cs])
                o_ref[rows, cs] = o_h
                ro = lax.rsqrt(jnp.mean(o_h * o_h, axis=-1, keepdims=True) + EPS)
                on_parts.append(o_h * ro)
            on = jnp.concatenate(on_parts, axis=1)
            m_ref[rows, :] = pre["merge_a"] * (on * pre["out_gate"]) + pre["merged_b"]

        pre = {0: gates_and_conv(0)}
        for c in range(ncht):
            if c + 1 < ncht:
                pre[c + 1] = gates_and_conv(c + 1)
            recurrence_and_merge(c, pre.pop(c))
        m_v = m_ref[...]
        h1_ref[...] = x + jnp.dot(m_v.astype(BF16), wout_ref[...], preferred_element_type=F32)
        mt_ref[...] = m_v.T.astype(BF16)

        @pl.when(i == nt - 1)
        def _():
            gather.finish()

    row = lambda w: pl.BlockSpec((TR, w), lambda i: (i, 0))
    col = lambda w: pl.BlockSpec((w, TR), lambda i: (0, i))
    vec = lambda r: pl.BlockSpec((r, D), lambda i: (0, 0))
    vm = pl.BlockSpec(memory_space=pltpu.VMEM)
    hbm = pl.BlockSpec(memory_space=pl.ANY)
    return pl.pallas_call(
        body, name="mix_block_fwd", grid=(nt,),
        in_specs=[pl.BlockSpec((TR, D), lambda i: (jnp.maximum(i - 1, 0), 0)), pl.BlockSpec((TR, D), lambda i: (0, 0)),
                  vec(2), vec(1), vec(1), vec(3), vm, vm] + [hbm] * nlate,
        out_specs=[col(D), row(9 * D), row(D), pl.BlockSpec((ncht, NH, HD, HD), lambda i: (i, 0, 0, 0)),
                   col(D), row(D)] + [hbm] * nlate,
        out_shape=[jax.ShapeDtypeStruct((D, T), BF16), jax.ShapeDtypeStruct((T, 9 * D), BF16),
                   jax.ShapeDtypeStruct((T, D), F32), jax.ShapeDtypeStruct((T // CH, NH, HD, HD), F32),
                   jax.ShapeDtypeStruct((D, T), BF16), jax.ShapeDtypeStruct((T, D), F32)]
        + [jax.ShapeDtypeStruct(a.shape, a.dtype) for a in late],
        input_output_aliases={8 + n: 6 + n for n in range(nlate)},
        scratch_shapes=[pltpu.VMEM((TR, 9 * D), F32), pltpu.VMEM((TR, D), F32), pltpu.VMEM((NH, HD, HD), F32),
                        pltpu.VMEM((8, D), F32), pltpu.SemaphoreType.DMA((nlate, 8)),
                        pltpu.SemaphoreType.DMA((nlate, 8))],
        compiler_params=_cparams(("arbitrary",)),
    )(x_seq, head_tile, lb_param, wattn, whn, conv_w, w_in_g, w_out, *late)


def _mix_block_bwd(dh1, projb, o, sst, lb_param, whn, conv_w, w_out, psums):
    T = projb.shape[0]
    nt = T // TR
    ncht = TR // CH
    tb16 = TR // 16
    nex = len(psums)

    def body(*refs):
        dh1_ref, proj_ref, pc_ref, px_ref, o_ref, sst_ref, lbp_ref, whn_ref, cw_ref, wout_ref = refs[:10]
        dproj_ref, dlb_ref, dwhn_ref, dcw_ref = refs[10 + nex:14 + nex]
        exchange = _ChipExchange(refs[10:10 + nex], refs[14 + nex:14 + 2 * nex], refs[-2], refs[-1])
        dm_ref, dst_ref, dcvc_ref, acc_lb, acc_hn, acc_cw = refs[14 + 2 * nex:-2]
        s = pl.program_id(0)
        tile = nt - 1 - s

        @pl.when(s == 0)
        def _():
            dst_ref[...] = jnp.zeros_like(dst_ref)
            dcvc_ref[...] = jnp.zeros_like(dcvc_ref)
            acc_lb[...] = jnp.zeros_like(acc_lb)
            acc_hn[...] = jnp.zeros_like(acc_hn)
            acc_cw[...] = jnp.zeros_like(acc_cw)
            exchange.send()

        dm_ref[...] = _dot_nt(dh1_ref[...], wout_ref[...])
        lb = _lower_bound(lbp_ref)
        lower, upper = _tri_masks()
        ltri = _ones_where(lower)
        utri = _ones_where(upper)
        whn_v = whn_ref[...]
        w0, w1, w2 = cw_ref[0:1, :], cw_ref[1:2, :], cw_ref[2:3, :]
        cx_before_tile = jnp.where(tile > 0, (pc_ref[...].astype(F32) * px_ref[...].astype(F32))[8:16, :], 0.0)
        rid = lax.broadcasted_iota(jnp.int32, (CH, D), 0)

        def chunk(cc, carry):
            c = ncht - 1 - cc
            r0 = pl.multiple_of(c * CH, CH)
            rows = pl.ds(r0, CH)
            slab = lambda n: proj_ref[rows, n * D:(n + 1) * D].astype(F32)
            q_raw, f_raw, v, g_out, b_gate, c_gate, x_conv = (slab(n) for n in range(7))
            sa = _sigmoid(slab(7))
            sb = _sigmoid(slab(8))
            dm_v = dm_ref[rows, :]

            sq = _sigmoid(q_raw)
            q = q_raw * sq
            sg, f, g, k = _gates(f_raw, lb)
            gam = _tri_matmul(ltri, g)
            gam_l = gam[CH - 1:CH, :]
            e_l = jnp.exp(gam_l)
            e_g = jnp.exp(gam)
            e_ng = jnp.exp(-gam)
            e_kl = jnp.exp(gam_l - gam)
            qt = q * e_g
            kt = k * e_ng
            khat = k * e_kl
            qt_b, kt_b, khat_b, v_b = qt.astype(BF16), kt.astype(BF16), khat.astype(BF16), v.astype(BF16)
            qt_seen, kt_seen = qt_b.astype(F32), kt_b.astype(F32)
            s_go = _sigmoid(g_out)
            silu_go = g_out * s_go
            cx = c_gate * x_conv
            rprev = pl.ds(pl.multiple_of(jnp.maximum(r0 - 16, 0), 16), 16)
            cx_prev_in = (proj_ref[rprev, 5 * D:6 * D].astype(F32) * proj_ref[rprev, 6 * D:7 * D].astype(F32))[8:16, :]
            prev8 = jnp.where(c > 0, cx_prev_in, cx_before_tile)
            cx_m1 = _shift_down(cx, prev8, 1)
            cx_m2 = _shift_down(cx, prev8, 2)
            cv = w0 * cx_m2 + w1 * cx_m1 + w2 * cx
            y_b = b_gate * cv

            o_v = o_ref[rows, :]
            ro_parts, on_parts = [], []
            for h in range(NH):
                cs = slice(h * HD, (h + 1) * HD)
                o_h = o_v[:, cs]
                ro = lax.rsqrt(jnp.mean(o_h * o_h, axis=-1, keepdims=True) + EPS)
                ro_parts.append(ro)
                on_parts.append(o_h * ro)
            on = jnp.concatenate(on_parts, axis=1)
            out_gate = whn_v * silu_go
            y_a = on * out_gate

            dy_a = dm_v * sa
            dy_b = dm_v * sb
            dproj_ref[rows, 7 * D:8 * D] = (dy_a * y_a * (1.0 - sa)).astype(BF16)
            dproj_ref[rows, 8 * D:9 * D] = (dy_b * y_b * (1.0 - sb)).astype(BF16)
            dproj_ref[rows, 4 * D:5 * D] = (dy_b * cv).astype(BF16)
            dcv = dy_b * b_gate
            acc_cw[0:1, :] += jnp.sum(dcv * cx_m2, axis=0, keepdims=True)
            acc_cw[1:2, :] += jnp.sum(dcv * cx_m1, axis=0, keepdims=True)
            acc_cw[2:3, :] += jnp.sum(dcv * cx, axis=0, keepdims=True)
            next8 = dcvc_ref[...]
            dcx = w2 * dcv + w1 * _shift_up(dcv, next8, 1) + w0 * _shift_up(dcv, next8, 2)
            dcvc_ref[...] = dcv[0:8, :]
            dproj_ref[rows, 5 * D:6 * D] = (dcx * x_conv).astype(BF16)
            dproj_ref[rows, 6 * D:7 * D] = (dcx * c_gate).astype(BF16)
            don = dy_a * out_gate
            dya_on = dy_a * on
            dproj_ref[rows, 3 * D:4 * D] = (dya_on * (whn_v * (s_go * (1.0 + g_out * (1.0 - s_go))))).astype(BF16)
            acc_hn[...] += jnp.sum(dya_on * silu_go, axis=0, keepdims=True)

            heads = [slice(h * HD, (h + 1) * HD) for h in range(NH)]
            do_bs, sts, dstns, first = [], [], [], []
            for h, cs in enumerate(heads):
                on_h = on_parts[h]
                don_h = don[:, cs]
                do_h = ro_parts[h] * (don_h - on_h * jnp.mean(don_h * on_h, axis=-1, keepdims=True))
                do_bs.append(do_h.astype(BF16))
                sts.append(sst_ref[c, h])
                dstns.append(dst_ref[h])
            for h, cs in enumerate(heads):
                qt_h, kt_h, khat_h, v_h = qt_b[:, cs], kt_b[:, cs], khat_b[:, cs], v_b[:, cs]
                do_b, dstn_b = do_bs[h], dstns[h].astype(BF16)
                first.append(dict(
                    a_t=_dot_nt(kt_h, qt_h), da=_dot_nt(do_b, v_h), da_t=_dot_nt(v_h, do_b),
                    dv_state=_dot_nt(khat_h, dstn_b), dqt_state=_dot(do_b, sts[h]), dkhat=_dot(v_h, dstn_b),
                    dst_chunk=_dot_tn(do_b, qt_h)))
            dq_parts, dk_parts, dv_parts, dgam_parts, ext_parts = [], [], [], [], []
            for h, cs in enumerate(heads):
                qt_h, kt_h = qt_b[:, cs], kt_b[:, cs]
                p1, do_b, st, dstn = first[h], do_bs[h], sts[h], dstns[h]
                dv_h = _dot(jnp.where(upper, p1["a_t"], 0.0), do_b) + p1["dv_state"]
                dqt_state = p1["dqt_state"]
                dqt_chunk = _dot(jnp.where(lower, p1["da"], 0.0), kt_h)
                dkt = _dot(jnp.where(upper, p1["da_t"], 0.0), qt_h)
                dkhat = p1["dkhat"]
                dq_h = (dqt_state + dqt_chunk) * e_g[:, cs]
                dk_h = dkt * e_ng[:, cs] + dkhat * e_kl[:, cs]
                khat_dkhat = dkhat * khat[:, cs]
                ext = (jnp.sum(khat_dkhat, axis=0, keepdims=True)
                       + e_l[:, cs] * jnp.sum(st * dstn, axis=0, keepdims=True))
                dst_ref[h] = p1["dst_chunk"] + dstn * e_l[:, cs]
                dq_parts.append(dq_h)
                dk_parts.append(dk_h)
                dv_parts.append(dv_h)
                dgam_parts.append(qt[:, cs] * dqt_state + qt_seen[:, cs] * dqt_chunk - kt_seen[:, cs] * dkt
                                  - khat_dkhat)
                ext_parts.append(ext)
            dq = jnp.concatenate(dq_parts, axis=1)
            dk = jnp.concatenate(dk_parts, axis=1)
            dgam = jnp.concatenate(dgam_parts, axis=1)
            ext = jnp.concatenate(ext_parts, axis=1)
            dgam = dgam + jnp.where(rid == CH - 1, ext, 0.0)
            dg = _tri_matmul(utri, dgam)
            dproj_ref[rows, 0:D] = (dq * (sq * (1.0 + q_raw * (1.0 - sq)))).astype(BF16)
            df = dg * jnp.exp(-g) - dk
            dproj_ref[rows, D:2 * D] = (df * (1.0 - lb) * sg * (1.0 - sg)).astype(BF16)
            dproj_ref[rows, 2 * D:3 * D] = jnp.concatenate(dv_parts, axis=1).astype(BF16)
            real = (tile * TR + r0 + rid) >= PAD
            acc_lb[...] += jnp.sum(jnp.where(real, df * (1.0 - sg), 0.0), axis=0, keepdims=True)
            return carry

        lax.fori_loop(0, ncht, chunk, 0)

        @pl.when(s == nt - 1)
        def _():
            dlb_ref[...] = acc_lb[...] * lb * (1.0 - lb)
            hn = acc_hn[...]
            tot = hn[:, 0:HD]
            for h in range(1, NH):
                tot = tot + hn[:, h * HD:(h + 1) * HD]
            dwhn_ref[...] = tot
            dcw_ref[...] = acc_cw[0:3, :]
            exchange.finish()

    hbm = pl.BlockSpec(memory_space=pl.ANY)
    rev = lambda s: (nt - 1 - s, 0)
    prevc = lambda s: (jnp.maximum((nt - 1 - s) * tb16 - 1, 0), 5)
    prevx = lambda s: (jnp.maximum((nt - 1 - s) * tb16 - 1, 0), 6)
    const = lambda s: (0, 0)
    return pl.pallas_call(
        body, name="mix_block_bwd", grid=(nt,),
        in_specs=[pl.BlockSpec((TR, D), rev),
                  pl.BlockSpec((TR, 9 * D), rev),
                  pl.BlockSpec((16, D), prevc),
                  pl.BlockSpec((16, D), prevx),
                  pl.BlockSpec((TR, D), rev),
                  pl.BlockSpec((ncht, NH, HD, HD), lambda s: (nt - 1 - s, 0, 0, 0)),
                  pl.BlockSpec((2, D), const),
                  pl.BlockSpec((1, D), const),
                  pl.BlockSpec((3, D), const),
                  pl.BlockSpec(memory_space=pltpu.VMEM)] + [hbm] * nex,
        out_specs=[pl.BlockSpec((TR, 9 * D), rev),
                   pl.BlockSpec((1, D), const),
                   pl.BlockSpec((1, HD), const),
                   pl.BlockSpec((3, D), const)] + [hbm] * nex,
        out_shape=[jax.ShapeDtypeStruct((T, 9 * D), BF16), jax.ShapeDtypeStruct((1, D), F32),
                   jax.ShapeDtypeStruct((1, HD), F32), jax.ShapeDtypeStruct((3, D), F32)]
        + [jax.ShapeDtypeStruct((3,) + p.shape[1:], p.dtype) for p in psums],
        scratch_shapes=[pltpu.VMEM((TR, D), F32), pltpu.VMEM((NH, HD, HD), F32), pltpu.VMEM((8, D), F32),
                        pltpu.VMEM((1, D), F32), pltpu.VMEM((1, D), F32), pltpu.VMEM((8, D), F32),
                        pltpu.SemaphoreType.DMA((nex, 3)), pltpu.SemaphoreType.DMA((nex, 3))],
        compiler_params=_cparams(("arbitrary",)),
    )(dh1, projb, projb, projb, o, sst, lb_param, whn, conv_w, w_out, *psums)


def _ffn_block_fwd(h1, tgt, wffn, w_up_g, fcw, fcb, w_down, wfin):
    T = h1.shape[0]
    nt = T // TR
    nsl = w_up_g.shape[0]
    wsl = w_up_g.shape[2]

    def body(h_ref, t_ref, wn_ref, wup_ref, cw_ref, cb_ref, wdn_ref, wf_ref,
             u2t_ref, upb_ref, ggt_ref, dh_ref, loss_ref, dwf_ref, up_scr, gg_ref, carry_ref):
        i = pl.program_id(0)

        @pl.when(i == 0)
        def _():
            carry_ref[...] = jnp.zeros_like(carry_ref)
            loss_ref[...] = jnp.zeros_like(loss_ref)
            dwf_ref[...] = jnp.zeros_like(dwf_ref)

        x = h_ref[...]
        r2 = lax.rsqrt(jnp.mean(x * x, axis=-1, keepdims=True) + EPS)
        u2_f = x * r2 * wn_ref[...]
        u2 = u2_f.astype(BF16)
        u2t_ref[...] = u2_f.T.astype(BF16)
        for s in range(nsl):
            up_s = jnp.dot(u2, wup_ref[s], preferred_element_type=F32)
            up_scr[:, s * wsl:(s + 1) * wsl] = up_s
            upb_ref[:, s * wsl:(s + 1) * wsl] = up_s.astype(BF16)
        w0, w1, w2 = cw_ref[0:1, :], cw_ref[1:2, :], cw_ref[2:3, :]

        def chunk(c, carry):
            rows = pl.ds(pl.multiple_of(c * CH, CH), CH)
            a_pre = up_scr[rows, 0:DFF]
            val = up_scr[rows, DFF:2 * DFF]
            prev8 = carry_ref[...]
            a = w0 * _shift_down(a_pre, prev8, 2) + w1 * _shift_down(a_pre, prev8, 1) + w2 * a_pre + cb_ref[...]
            carry_ref[...] = a_pre[CH - 8:CH, :]
            gg_ref[rows, :] = a * _sigmoid(a) * val
            return carry

        lax.fori_loop(0, TR // CH, chunk, 0, unroll=True)
        gg_v = gg_ref[...]
        ggt_ref[...] = gg_v.T.astype(BF16)
        h2 = x + jnp.dot(gg_v.astype(BF16), wdn_ref[...], preferred_element_type=F32)
        r3 = lax.rsqrt(jnp.mean(h2 * h2, axis=-1, keepdims=True) + EPS)
        n3 = h2 * r3
        wf = wf_ref[...]
        diff = jnp.where(i > 0, n3 * wf - t_ref[...], 0.0)
        loss_ref[...] += jnp.sum(diff * diff, axis=0, keepdims=True) * (0.5 / D)
        dy = diff * (1.0 / D)
        dwf_ref[...] += jnp.sum(dy * n3, axis=0, keepdims=True)
        dn = dy * wf
        dh_ref[...] = r3 * (dn - n3 * jnp.mean(dn * n3, axis=-1, keepdims=True))

    row = lambda w: pl.BlockSpec((TR, w), lambda i: (i, 0))
    col = lambda w: pl.BlockSpec((w, TR), lambda i: (0, i))
    vec = lambda w, r=1: pl.BlockSpec((r, w), lambda i: (0, 0))
    vm = pl.BlockSpec(memory_space=pltpu.VMEM)
    return pl.pallas_call(
        body, name="ffn_block_fwd", grid=(nt,),
        in_specs=[row(D), pl.BlockSpec((TR, D), lambda i: (jnp.maximum(i - 1, 0), 0)), vec(D), vm,
                  vec(DFF, 3), vec(DFF), vm, vec(D)],
        out_specs=[col(D), row(2 * DFF), col(DFF), row(D), vec(D), vec(D)],
        out_shape=[jax.ShapeDtypeStruct((D, T), BF16), jax.ShapeDtypeStruct((T, 2 * DFF), BF16),
                   jax.ShapeDtypeStruct((DFF, T), BF16), jax.ShapeDtypeStruct((T, D), F32),
                   jax.ShapeDtypeStruct((1, D), F32), jax.ShapeDtypeStruct((1, D), F32)],
        scratch_shapes=[pltpu.VMEM((TR, 2 * DFF), F32), pltpu.VMEM((TR, DFF), F32), pltpu.VMEM((8, DFF), F32)],
        compiler_params=_cparams(("arbitrary",)),
    )(h1, tgt, wffn, w_up_g, fcw, fcb, w_down, wfin)


def _ffn_block_bwd(dh2, upb, h1, wffn, w_up_g, fcw, fcb, w_down):
    T = h1.shape[0]
    nt = T // TR
    ncht = TR // CH
    nsl = w_up_g.shape[0]
    wsl = w_up_g.shape[2]
    tb16 = TR // 16
    assert ncht % nsl == 0
    every = ncht // nsl
    step = -(-DFF // (ncht * 128)) * 128
    parts = [(c0, min(c0 + step, DFF)) for c0 in range(0, DFF, step)]
    assert len(parts) == ncht

    def body(dh2n_ref, dh2p_ref, up_ref, pa_ref, h_ref, wn_ref, wup_ref, cw_ref, cb_ref, wdn_ref,
             dup_ref, dh1_ref, dfw_ref, dfb_ref, dwn_ref,
             dgg_ring, dup_ring, carry_ref, acc_w, acc_b, acc_n):
        s = pl.program_id(0)
        slot = lax.rem(s, 2)
        other = 1 - slot

        @pl.when(s == 0)
        def _():
            carry_ref[...] = jnp.zeros_like(carry_ref)
            acc_w[...] = jnp.zeros_like(acc_w)
            acc_b[...] = jnp.zeros_like(acc_b)
            acc_n[...] = jnp.zeros_like(acc_n)
            dup_ring[1] = jnp.zeros((TR, 2 * DFF), BF16)
            dgg_ring[0] = _dot_nt(dh2p_ref[...], wdn_ref[...])

        def norm_bwd(du2, valid):
            x = h_ref[...]
            dh2p = dh2p_ref[...]
            r2 = lax.rsqrt(jnp.mean(x * x, axis=-1, keepdims=True) + EPS)
            n2 = x * r2
            dn = du2 * wn_ref[...]
            dh1_ref[...] = dh2p + r2 * (dn - n2 * jnp.mean(dn * n2, axis=-1, keepdims=True))
            acc_n[...] += jnp.where(valid, jnp.sum(du2 * n2, axis=0, keepdims=True), 0.0)

        @pl.when(s < nt)
        def _():
            tile = nt - 1 - s
            w0, w1, w2 = cw_ref[0:1, :], cw_ref[1:2, :], cw_ref[2:3, :]
            a_before_tile = jnp.where(tile > 0, pa_ref[...].astype(F32)[8:16, :], 0.0)
            dh2n = dh2n_ref[...].astype(BF16)
            du2 = None
            for idx in range(ncht):
                c = ncht - 1 - idx
                r0 = c * CH
                rows = slice(r0, r0 + CH)
                a_pre = up_ref[rows, 0:DFF].astype(F32)
                val = up_ref[rows, DFF:2 * DFF].astype(F32)
                prev8 = up_ref[r0 - 16:r0, 0:DFF].astype(F32)[8:16, :] if c > 0 else a_before_tile
                a_m1 = _shift_down(a_pre, prev8, 1)
                a_m2 = _shift_down(a_pre, prev8, 2)
                a = w0 * a_m2 + w1 * a_m1 + w2 * a_pre + cb_ref[...]
                sig = _sigmoid(a)
                dgg_v = dgg_ring[slot, rows, :]
                da = dgg_v * val * (sig * (1.0 + a * (1.0 - sig)))
                dval = (dgg_v * (a * sig)).astype(BF16)
                next8 = carry_ref[...]
                da_pre = (w2 * da + w1 * _shift_up(da, next8, 1) + w0 * _shift_up(da, next8, 2)).astype(BF16)
                carry_ref[...] = da[0:8, :]
                dup_ref[rows, 0:DFF] = da_pre
                dup_ref[rows, DFF:2 * DFF] = dval
                dup_ring[slot, rows, 0:DFF] = da_pre
                dup_ring[slot, rows, DFF:2 * DFF] = dval
                acc_w[0:1, :] += jnp.sum(da * a_m2, axis=0, keepdims=True)
                acc_w[1:2, :] += jnp.sum(da * a_m1, axis=0, keepdims=True)
                acc_w[2:3, :] += jnp.sum(da * a_pre, axis=0, keepdims=True)
                acc_b[...] += jnp.sum(da, axis=0, keepdims=True)
                c0, c1 = parts[idx]
                dgg_ring[other, :, c0:c1] = _dot_nt(dh2n, wdn_ref[c0:c1, :])
                if idx % every == 0:
                    sl = idx // every
                    part = _dot_nt(dup_ring[other, :, sl * wsl:(sl + 1) * wsl], wup_ref[sl])
                    du2 = part if du2 is None else du2 + part
            norm_bwd(du2, s > 0)

        @pl.when(s == nt)
        def _():
            du2 = _dot_nt(dup_ring[other, :, 0:wsl], wup_ref[0])
            for sl in range(1, nsl):
                du2 = du2 + _dot_nt(dup_ring[other, :, sl * wsl:(sl + 1) * wsl], wup_ref[sl])
            norm_bwd(du2, True)
            dfw_ref[...] = acc_w[0:3, :]
            dfb_ref[...] = acc_b[...]
            dwn_ref[...] = acc_n[...]

    gate_tile = lambda s: jnp.maximum(nt - 1 - s, 0)
    next_tile = lambda s: jnp.maximum(nt - 2 - s, 0)
    prev_tile = lambda s: jnp.minimum(nt - s, nt - 1)
    vec = lambda w, r=1: pl.BlockSpec((r, w), lambda s: (0, 0))
    vm = pl.BlockSpec(memory_space=pltpu.VMEM)
    return pl.pallas_call(
        body, name="ffn_block_bwd", grid=(nt + 1,),
        in_specs=[pl.BlockSpec((TR, D), lambda s: (next_tile(s), 0)),
                  pl.BlockSpec((TR, D), lambda s: (prev_tile(s), 0)),
                  pl.BlockSpec((TR, 2 * DFF), lambda s: (gate_tile(s), 0)),
                  pl.BlockSpec((16, DFF), lambda s: (jnp.maximum(gate_tile(s) * tb16 - 1, 0), 0)),
                  pl.BlockSpec((TR, D), lambda s: (prev_tile(s), 0)),
                  vec(D), vm, vec(DFF, 3), vec(DFF), vm],
        out_specs=[pl.BlockSpec((TR, 2 * DFF), lambda s: (gate_tile(s), 0)),
                   pl.BlockSpec((TR, D), lambda s: (prev_tile(s), 0)),
                   vec(DFF, 3), vec(DFF), vec(D)],
        out_shape=[jax.ShapeDtypeStruct((T, 2 * DFF), BF16), jax.ShapeDtypeStruct((T, D), F32),
                   jax.ShapeDtypeStruct((3, DFF), F32), jax.ShapeDtypeStruct((1, DFF), F32),
                   jax.ShapeDtypeStruct((1, D), F32)],
        scratch_shapes=[pltpu.VMEM((2, TR, DFF), F32), pltpu.VMEM((2, TR, 2 * DFF), BF16),
                        pltpu.VMEM((8, DFF), F32), pltpu.VMEM((8, DFF), F32), pltpu.VMEM((1, DFF), F32),
                        pltpu.VMEM((1, D), F32)],
        compiler_params=_cparams(("arbitrary",)),
    )(dh2, dh2, upb, upb, h1, wffn, w_up_g, fcw, fcb, w_down)


def _place():
    x, y, c = lax.axis_index("x"), lax.axis_index("y"), lax.axis_index("c")
    return x, y, c


_CHIP_FLIPS = ((1, 0), (0, 1), (1, 1))


def _flip(v, bit):
    return 1 - v if bit else v


def _into_slot(w, j_idx, *, rb, dtype, name, paired=False):
    r, cdim = w.shape

    def body(j_ref, w_ref, out_ref):
        del j_ref
        out_ref[...] = w_ref[...].astype(dtype)

    if paired:
        out_shape = jax.ShapeDtypeStruct((NSHARD // 2, r, 2 * cdim), dtype)
        out_spec = pl.BlockSpec((None, rb, cdim), lambda i, j_ref: (j_ref[0] // 2, i, j_ref[0] % 2))
    else:
        out_shape = jax.ShapeDtypeStruct((NSHARD, r, cdim), dtype)
        out_spec = pl.BlockSpec((None, rb, cdim), lambda i, j_ref: (j_ref[0], i, 0))
    grid_spec = pltpu.PrefetchScalarGridSpec(
        num_scalar_prefetch=1, grid=(r // rb,),
        in_specs=[pl.BlockSpec((rb, cdim), lambda i, j_ref: (i, 0))], out_specs=out_spec)
    return pl.pallas_call(
        body, name=name, grid_spec=grid_spec, out_shape=out_shape, compiler_params=_cparams(("parallel",)),
    )(j_idx, w)


class _Gather:
    def __init__(self, outs, send_sems, recv_sems, whole, paired=None):
        self.outs, self.send_sems, self.recv_sems, self.whole = outs, send_sems, recv_sems, whole
        self.paired = paired if paired is not None else (False,) * len(outs)
        self.x, self.y, self.c = _place()
        self.j = 2 * self.x + self.y
        self.sibling = (self.x, self.y, 1 - self.c)
        chips = [(_flip(self.x, fx), _flip(self.y, fy)) for fx, fy in _CHIP_FLIPS]
        self.slots = [2 * px + py for px, py in chips]
        self.peers = [(px, py, self.c) for px, py in chips]

    def _copy(self, w, slot, core, sem, to, quarter=None):
        r = self.outs[w].shape[1]
        if self.whole[w]:
            rows = pl.ds(0, r)
        elif quarter is None:
            rows = pl.ds(core * (r // 2), r // 2)
        else:
            rows = pl.ds(core * (r // 2) + quarter * (r // 4), r // 4)
        if self.paired[w]:
            cw = self.outs[w].shape[2] // 2
            piece = self.outs[w].at[slot // 2, rows, pl.ds((slot % 2) * cw, cw)]
        else:
            piece = self.outs[w].at[slot, rows, :]
        return pltpu.make_async_remote_copy(
            src_ref=piece, dst_ref=piece, send_sem=self.send_sems.at[w, sem], recv_sem=self.recv_sems.at[w, sem],
            device_id=to, device_id_type=MESH)

    def _direct(self, w):
        return (0, 1, 2) if self.whole[w] else (0, 1)

    def send(self):
        for w in range(len(self.outs)):
            for k in self._direct(w):
                self._copy(w, self.j, self.c, k, self.peers[k]).start()

    def relay(self):
        jx, jy, _ = self.slots
        for w in range(len(self.outs)):
            if self.whole[w]:
                for k in range(3):
                    self._copy(w, self.slots[k], self.c, k, self.sibling).wait_recv()
                continue
            self._copy(w, jx, self.c, 0, self.sibling).wait_recv()
            self._copy(w, jx, self.c, 6, self.peers[1], quarter=0).start()
            self._copy(w, jx, self.c, 3, self.sibling).start()
            self._copy(w, jy, self.c, 1, self.sibling).wait_recv()
            self._copy(w, jy, self.c, 7, self.peers[0], quarter=1).start()
            self._copy(w, jy, self.c, 4, self.sibling).start()

    def relay_diagonal(self):
        jd = self.slots[2]
        for w in range(len(self.outs)):
            if not self.whole[w]:
                self._copy(w, jd, self.c, 6, self.sibling, quarter=0).wait_recv()
                self._copy(w, jd, self.c, 7, self.sibling, quarter=1).wait_recv()
                self._copy(w, jd, self.c, 5, self.sibling).start()

    def finish(self):
        jx, jy, jd = self.slots
        for w in range(len(self.outs)):
            for k in self._direct(w):
                self._copy(w, self.j, self.c, k, self.peers[k]).wait_send()
            if self.whole[w]:
                continue
            self._copy(w, jx, self.c, 6, self.peers[1], quarter=0).wait_send()
            self._copy(w, jy, self.c, 7, self.peers[0], quarter=1).wait_send()
            for k, slot in enumerate(self.slots):
                self._copy(w, slot, 1 - self.c, 3 + k, self.sibling).wait_recv()
                self._copy(w, slot, self.c, 3 + k, self.sibling).wait_send()


def _allgather_weights(slotted, whole):
    n = len(slotted)

    def body(*refs):
        g = _Gather(refs[n:2 * n], refs[2 * n], refs[2 * n + 1], whole)
        g.send()
        g.relay()
        g.relay_diagonal()
        g.finish()

    any_spec = pl.BlockSpec(memory_space=pl.ANY)
    return pl.pallas_call(
        body, name="allgather_weights",
        in_specs=[any_spec] * n, out_specs=[any_spec] * n,
        out_shape=[jax.ShapeDtypeStruct(a.shape, a.dtype) for a in slotted],
        input_output_aliases={i: i for i in range(n)},
        scratch_shapes=[pltpu.SemaphoreType.DMA((n, 8)), pltpu.SemaphoreType.DMA((n, 8))],
    )(*slotted)


class _ChipExchange:
    def __init__(self, ins, outs, send_sems, recv_sems):
        self.ins, self.outs, self.send_sems, self.recv_sems = ins, outs, send_sems, recv_sems
        self.x, self.y, self.c = _place()

    def _copies(self):
        for w in range(len(self.ins)):
            for kk, (fx, fy) in enumerate(_CHIP_FLIPS):
                px, py = _flip(self.x, fx), _flip(self.y, fy)
                yield pltpu.make_async_remote_copy(
                    src_ref=self.ins[w].at[2 * px + py], dst_ref=self.outs[w].at[kk],
                    send_sem=self.send_sems.at[w, kk], recv_sem=self.recv_sems.at[w, kk],
                    device_id=(px, py, self.c), device_id_type=MESH)

    def send(self):
        for cp in self._copies():
            cp.start()

    def finish(self):
        for cp in self._copies():
            cp.wait()


class _PairExchange:
    def __init__(self, ins, outs, send_sems, recv_sems):
        self.ins, self.outs, self.send_sems, self.recv_sems = ins, outs, send_sems, recv_sems
        self.x, self.y, self.c = _place()

    def _copies(self):
        for w in range(len(self.ins)):
            half = self.ins[w].shape[1] // 2
            yield pltpu.make_async_remote_copy(
                src_ref=self.ins[w].at[:, pl.ds((1 - self.c) * half, half), :], dst_ref=self.outs[w],
                send_sem=self.send_sems.at[w], recv_sem=self.recv_sems.at[w],
                device_id=(self.x, self.y, 1 - self.c), device_id_type=MESH)

    def send(self):
        for cp in self._copies():
            cp.start()

    def finish(self):
        for cp in self._copies():
            cp.wait()


def _pair_shapes(grads):
    return [jax.ShapeDtypeStruct((g.shape[0], g.shape[1] // 2, g.shape[2]), g.dtype) for g in grads]


def _pair_exchange(grads, name):
    nw = len(grads)

    def body(*refs):
        ex = _PairExchange(refs[:nw], refs[nw:2 * nw], refs[2 * nw], refs[2 * nw + 1])
        ex.send()
        ex.finish()

    any_spec = pl.BlockSpec(memory_space=pl.ANY)
    return pl.pallas_call(
        body, name=name,
        in_specs=[any_spec] * nw, out_specs=[any_spec] * nw, out_shape=_pair_shapes(grads),
        scratch_shapes=[pltpu.SemaphoreType.DMA((nw,)), pltpu.SemaphoreType.DMA((nw,))],
    )(*grads)


def _pair_add(g, other, c_idx, *, rb, name):
    S, r, cdim = g.shape
    half = r // 2
    nb = half // rb

    def body(c_ref, g_ref, o_ref, out_ref):
        del c_ref
        out_ref[...] = (g_ref[...].astype(F32) + o_ref[...].astype(F32)).astype(BF16)

    grid_spec = pltpu.PrefetchScalarGridSpec(
        num_scalar_prefetch=1, grid=(S, nb),
        in_specs=[pl.BlockSpec((None, rb, cdim), lambda s, i, c_ref: (s, c_ref[0] * nb + i, 0)),
                  pl.BlockSpec((None, rb, cdim), lambda s, i, c_ref: (s, i, 0))],
        out_specs=pl.BlockSpec((None, rb, cdim), lambda s, i, c_ref: (s, i, 0)))
    return pl.pallas_call(
        body, name=name, grid_spec=grid_spec, out_shape=jax.ShapeDtypeStruct((S, half, cdim), BF16),
        compiler_params=_cparams(("parallel", "parallel")),
    )(c_idx, g, other)


def _chip_sum(psum, parts, cj_idx, *, rb, name):
    S, half, cdim = psum.shape
    nb = half // rb

    def body(cj_ref, own_ref, p_ref, out_ref):
        del cj_ref
        f = lambda v: v.astype(F32)
        out_ref[...] = ((f(own_ref[...]) + f(p_ref[0])) + f(p_ref[1])) + f(p_ref[2])

    grid_spec = pltpu.PrefetchScalarGridSpec(
        num_scalar_prefetch=1, grid=(nb,),
        in_specs=[pl.BlockSpec((None, rb, cdim), lambda i, cj: (cj[1], i, 0)),
                  pl.BlockSpec((3, rb, cdim), lambda i, cj: (0, i, 0))],
        out_specs=pl.BlockSpec((rb, cdim), lambda i, cj: (cj[0] * nb + i, 0)))
    return pl.pallas_call(
        body, name=name, grid_spec=grid_spec, out_shape=jax.ShapeDtypeStruct((2 * half, cdim), F32),
        compiler_params=_cparams(("parallel",)),
    )(cj_idx, psum, parts)


SLAB_W = 1024


def _final_exchange(pieces, grads):
    n, nw = len(pieces), len(grads)
    segs, at = [], 0
    for idx, p in enumerate(pieces):
        r, wd = p.shape
        for c0 in range(0, wd, SLAB_W):
            if r > 1:
                at = -(-at // 8) * 8
            segs.append((idx, c0, min(SLAB_W, wd - c0), at))
            at += r
    rows = -(-at // 8) * 8
    flips = [(fx, fy, fc) for fx in (0, 1) for fy in (0, 1) for fc in (0, 1)][1:]

    def body(*refs):
        ins = refs[:n]
        outs = refs[n + nw:2 * n + nw]
        g_refs = refs[2 * n + nw:2 * n + 2 * nw]
        mine_ref, slots_ref, send_sems, recv_sems, gsend_sems, grecv_sems = refs[2 * n + 2 * nw:]
        x, y, c = _place()
        me = 4 * x + 2 * y + c
        sibling = (x, y, 1 - c)

        def swap(w, core):
            half = g_refs[w].shape[0] // 2
            rows_ = g_refs[w].at[pl.ds(core * half, half), :]
            return pltpu.make_async_remote_copy(
                src_ref=rows_, dst_ref=rows_, send_sem=gsend_sems.at[w], recv_sem=grecv_sems.at[w],
                device_id=sibling, device_id_type=MESH)

        for w in range(nw):
            swap(w, c).start()
        mine_ref[...] = jnp.zeros_like(mine_ref)
        for idx, c0, wd, st in segs:
            r = ins[idx].shape[0]
            mine_ref[st:st + r, 0:wd] = ins[idx][:, c0:c0 + wd]
        slots_ref[me] = mine_ref[...]
        cps = []
        for kk, (fx, fy, fc) in enumerate(flips):
            cp = pltpu.make_async_remote_copy(
                src_ref=mine_ref, dst_ref=slots_ref.at[me], send_sem=send_sems.at[kk], recv_sem=recv_sems.at[kk],
                device_id=(_flip(x, fx), _flip(y, fy), _flip(c, fc)), device_id_type=MESH)
            cp.start()
            cps.append(cp)
        for cp in cps:
            cp.wait()
        tot = slots_ref[0]
        for d in range(1, 8):
            tot = tot + slots_ref[d]
        mine_ref[...] = tot
        for idx, c0, wd, st in segs:
            r = ins[idx].shape[0]
            val = mine_ref[st:st + r, 0:wd]
            if idx == n - 1:
                outs[idx][...] = jnp.sum(val, keepdims=True)
            else:
                outs[idx][:, c0:c0 + wd] = val
        for w in range(nw):
            swap(w, 1 - c).wait_recv()
            swap(w, c).wait_send()

    vm = pl.BlockSpec(memory_space=pltpu.VMEM)
    hbm = pl.BlockSpec(memory_space=pl.ANY)
    out_shape = ([jax.ShapeDtypeStruct(p.shape, F32) for p in pieces[:-1]] + [jax.ShapeDtypeStruct((1, 1), F32)]
                 + [jax.ShapeDtypeStruct(g.shape, g.dtype) for g in grads])
    res = pl.pallas_call(
        body, name="final_exchange", in_specs=[vm] * n + [hbm] * nw, out_specs=[vm] * n + [hbm] * nw,
        out_shape=out_shape, input_output_aliases={n + w: n + w for w in range(nw)},
        scratch_shapes=[pltpu.VMEM((rows, SLAB_W), F32), pltpu.VMEM((8, rows, SLAB_W), F32),
                        pltpu.SemaphoreType.DMA((7,)), pltpu.SemaphoreType.DMA((7,)),
                        pltpu.SemaphoreType.DMA((nw,)), pltpu.SemaphoreType.DMA((nw,))],
    )(*pieces, *grads)
    return res[:n], res[n:]


def _adamw(w, g, m, v, *, rb, name):
    r, cdim = w.shape

    def body(w_ref, g_ref, m_ref, v_ref, go_ref, d_ref, nm_ref, nv_ref):
        go_ref[...] = g_ref[...]
        d_ref[...], nm_ref[...], nv_ref[...] = _adamw_update(w_ref[...], g_ref[...], m_ref[...], v_ref[...])

    spec = pl.BlockSpec((rb, cdim), lambda i: (i, 0))
    shp = jax.ShapeDtypeStruct((r, cdim), F32)
    return pl.pallas_call(
        body, name=name, grid=(r // rb,), in_specs=[spec] * 4, out_specs=[spec] * 4, out_shape=[shp] * 4,
        compiler_params=_cparams(("parallel",)),
    )(w, g, m, v)


def _adamw_update(w, g, m, v):
    nm = ADAM_B1 * m + (1.0 - ADAM_B1) * g
    nv = ADAM_B2 * v + (1.0 - ADAM_B2) * (g * g)
    m_hat = nm / (1.0 - ADAM_B1 ** ADAM_STEP)
    v_hat = nv / (1.0 - ADAM_B2 ** ADAM_STEP)
    return -ADAM_LR * (m_hat / (jnp.sqrt(v_hat) + ADAM_EPS) + ADAM_WD * w), nm, nv


def _adamw_small(params):
    n = len(params)

    def body(*refs):
        ins, outs = refs[:4 * n], refs[4 * n:]
        for p in range(n):
            w_ref, g_ref, m_ref, v_ref = ins[4 * p:4 * p + 4]
            d, nm, nv = _adamw_update(w_ref[...], g_ref[...], m_ref[...], v_ref[...])
            outs[3 * p][...] = d
            outs[3 * p + 1][...] = nm
            outs[3 * p + 2][...] = nv

    vm = pl.BlockSpec(memory_space=pltpu.VMEM)
    flat = [a for p in params for a in p]
    out_shape = [jax.ShapeDtypeStruct(p[0].shape, F32) for p in params for _ in range(3)]
    res = pl.pallas_call(
        body, name="adamw_small", in_specs=[vm] * (4 * n), out_specs=[vm] * (3 * n), out_shape=out_shape,
    )(*flat)
    return [tuple(res[3 * p:3 * p + 3]) for p in range(n)]


_PAIR_ADD_ROWS = {"w_in": 256, "w_out": 128, "w_up": 256, "w_down": 176}


def _pair_sums(grads, others, names, c_idx):
    return [_pair_add(g, o_, c_idx, rb=_PAIR_ADD_ROWS[n], name=f"pair_add_{n}") for g, o_, n in zip(grads, others, names)]


def _local_step(x, tgt, meta_full, lb_param, attn_norm_w, w_in_g, hgrn_norm_w, conv_w_full, w_out_full,
                ffn_norm_w, late_slotted, fcw_full, ffn_conv_b, final_norm_w, c_idx, cj_idx):
    seq = x.shape[0]
    T = TR + seq
    head_tile = jnp.concatenate([jnp.zeros((PAD, D), F32), meta_full], axis=0)
    whn_t = jnp.tile(hgrn_norm_w, (1, NH))

    ut, projb, o, sst, mt, h1, w_up_g, w_down_g = _mix_block_fwd(
        x, head_tile, lb_param, attn_norm_w, whn_t, conv_w_full, w_in_g, w_out_full, late_slotted, (True, False))
    w_down_full = w_down_g.reshape(DFF, D)
    u2t, upb, ggt, dh2, loss_vec, dwfin = _ffn_block_fwd(
        h1, tgt, ffn_norm_w, w_up_g, fcw_full, ffn_conv_b, w_down_full, final_norm_w.reshape(1, D))

    dup, dh1, dfw, dfb, dwffn = _ffn_block_bwd(
        dh2, upb, h1, ffn_norm_w, w_up_g, fcw_full, ffn_conv_b, w_down_full)
    kb = 1408 if T % 1408 == 0 else TR
    g_up = _weight_grad(u2t, dup, bn=DFF, bk=kb, name="dw_up_mm", shard_cols=2 * DFF // NSHARD)
    g_down, sib_up = _weight_grad(ggt, dh2, bn=D // 2, bk=2816 if T % 2816 == 0 else kb, name="dw_down_mm", ride=[g_up])
    g_down = g_down.reshape(NSHARD, DFF // NSHARD, D)
    sib_down = _pair_exchange([g_down], name="grad_pair_exchange_down")
    ps_ffn = _pair_sums([g_up, g_down], sib_up + list(sib_down), ("w_up", "w_down"), c_idx)

    dproj, dlb, dwhn, dcw, *parts_ffn = _mix_block_bwd(
        dh1, projb, o, sst, lb_param, whn_t, conv_w_full, w_out_full, ps_ffn)
    kb_deep = 2816 if T % 2816 == 0 else kb
    g_in = _weight_grad(ut, dproj, bn=9 * D // NSHARD, bk=kb_deep, name="dw_in_mm", shard_cols=9 * D // NSHARD)
    g_out, sib_in = _weight_grad(mt, dh1, bn=D, bk=kb_deep, name="dw_out_mm", ride=[g_in])
    g_out = g_out.reshape(NSHARD, D // NSHARD, D)
    sib_out = _pair_exchange([g_out], name="grad_pair_exchange_out")
    ps_mix = _pair_sums([g_in, g_out], sib_in + list(sib_out), ("w_in", "w_out"), c_idx)

    grad_x, dmeta, dwattn, *parts_mix = _input_grad_block(dproj, x, head_tile, dh1, attn_norm_w, w_in_g, ps_mix)
    grad_x = _hand_on_in_place(grad_x)

    halves = [_chip_sum(ps, p, cj_idx, rb=_PAIR_ADD_ROWS[n], name=f"chip_sum_{n}")
              for ps, p, n in zip(ps_mix + ps_ffn, parts_mix + parts_ffn, ("w_in", "w_out", "w_up", "w_down"))]
    small = dict(dlb=dlb, dwattn=dwattn, dwhn=dwhn, dwffn=dwffn, dfb=dfb, dwfin=dwfin,
                 dcw=dcw, dfw=dfw, dmeta=dmeta, loss=loss_vec)
    return grad_x, small, halves


_SMALL_ORDER = ("dmeta", "dcw", "dfw", "dlb", "dwattn", "dwhn", "dwffn", "dfb", "dwfin", "loss")


def kernel(x, meta_tokens, lb_param, attn_norm_w, w_in, hgrn_norm_w, conv_w, w_out, ffn_norm_w, w_up, ffn_conv_w, ffn_conv_b, w_down, final_norm_w, loss_target, m_meta_tokens, m_lb_param, m_attn_norm_w, m_w_in, m_hgrn_norm_w, m_conv_w, m_w_out, m_ffn_norm_w, m_w_up, m_ffn_conv_w, m_ffn_conv_b, m_w_down, m_final_norm_w, v_meta_tokens, v_lb_param, v_attn_norm_w, v_w_in, v_hgrn_norm_w, v_conv_w, v_w_out, v_ffn_norm_w, v_w_up, v_ffn_conv_w, v_ffn_conv_b, v_w_down, v_final_norm_w):
    xi, yi, ci = _place()
    j = 2 * xi + yi
    c_idx = jnp.reshape(ci, (1,)).astype(jnp.int32)

    j_idx = jnp.reshape(j, (1,)).astype(jnp.int32)
    ds_, fs_ = D // NSHARD, DFF // NSHARD
    widen = lambda a: jnp.pad(a, ((0, 0), (0, 768 - a.shape[1])))
    rows_small = jnp.concatenate([widen(meta_tokens), widen(conv_w[0]), widen(ffn_conv_w[0]),
                                  jnp.zeros((2, 768), F32)], axis=0)
    s_in, s_out, s_up, s_down = [
        _into_slot(w[0], j_idx, rb=rb, dtype=BF16, name=f"slot_{n}", paired=pr)
        for w, rb, n, pr in ((w_in, 256, "w_in", False), (w_out, 128, "w_out", False), (w_up, 256, "w_up", True),
                             (w_down, 176, "w_down", False))]
    s_small = _into_slot(rows_small, j_idx, rb=rows_small.shape[0], dtype=F32, name="slot_small")
    w_in_g, w_out_g, small_g = _allgather_weights([s_in, s_out, s_small], (False, False, True))
    unshard = lambda a: jnp.transpose(a, (1, 0, 2)).reshape(a.shape[1], -1)
    meta_full = unshard(small_g[:, 0:NMETA, 0:ds_])
    conv_w_full = unshard(small_g[:, NMETA:NMETA + 3, 0:ds_])
    fcw_full = unshard(small_g[:, NMETA + 3:NMETA + 6, 0:fs_])

    cj_idx = jnp.stack([ci, j]).astype(jnp.int32)
    grad_x, small, halves = _local_step(
        x[0], loss_target[0], meta_full, lb_param, attn_norm_w, w_in_g, hgrn_norm_w, conv_w_full,
        w_out_g.reshape(D, D), ffn_norm_w, [s_up, s_down], fcw_full, ffn_conv_b, final_norm_w, c_idx, cj_idx)

    names = _SMALL_ORDER
    small_sums, g_big = _final_exchange([small[n] for n in names], halves)
    vals = dict(zip(names, small_sums))
    loss = vals["loss"].reshape(())
    g_small = {
        "meta_tokens": lax.dynamic_slice_in_dim(vals["dmeta"], j * (D // NSHARD), D // NSHARD, axis=1),
        "lb_param": jnp.concatenate([vals["dlb"], -vals["dlb"]], axis=0),
        "attn_norm_w": vals["dwattn"],
        "hgrn_norm_w": vals["dwhn"],
        "conv_w": lax.dynamic_slice_in_dim(vals["dcw"], j * (D // NSHARD), D // NSHARD, axis=1)[None],
        "ffn_norm_w": vals["dwffn"],
        "ffn_conv_w": lax.dynamic_slice_in_dim(vals["dfw"], j * (DFF // NSHARD), DFF // NSHARD, axis=1)[None],
        "ffn_conv_b": vals["dfb"],
        "final_norm_w": vals["dwfin"].reshape(D),
    }


    weights = {"meta_tokens": meta_tokens, "lb_param": lb_param, "attn_norm_w": attn_norm_w, "w_in": w_in,
               "hgrn_norm_w": hgrn_norm_w, "conv_w": conv_w, "w_out": w_out, "ffn_norm_w": ffn_norm_w,
               "w_up": w_up, "ffn_conv_w": ffn_conv_w, "ffn_conv_b": ffn_conv_b, "w_down": w_down,
               "final_norm_w": final_norm_w}
    ms = {"meta_tokens": m_meta_tokens, "lb_param": m_lb_param, "attn_norm_w": m_attn_norm_w, "w_in": m_w_in,
          "hgrn_norm_w": m_hgrn_norm_w, "conv_w": m_conv_w, "w_out": m_w_out, "ffn_norm_w": m_ffn_norm_w,
          "w_up": m_w_up, "ffn_conv_w": m_ffn_conv_w, "ffn_conv_b": m_ffn_conv_b, "w_down": m_w_down,
          "final_norm_w": m_final_norm_w}
    vs = {"meta_tokens": v_meta_tokens, "lb_param": v_lb_param, "attn_norm_w": v_attn_norm_w, "w_in": v_w_in,
          "hgrn_norm_w": v_hgrn_norm_w, "conv_w": v_conv_w, "w_out": v_w_out, "ffn_norm_w": v_ffn_norm_w,
          "w_up": v_w_up, "ffn_conv_w": v_ffn_conv_w, "ffn_conv_b": v_ffn_conv_b, "w_down": v_w_down,
          "final_norm_w": v_final_norm_w}
    order = list(weights)
    grads, deltas, new_m, new_v = {}, {}, {}, {}

    for name, g, rb in zip(("w_in", "w_out", "w_up", "w_down"), g_big, (256, 128, 256, 176)):
        shp = weights[name].shape
        w2, m2, v2 = (a.reshape(shp[1], shp[2]) for a in (weights[name], ms[name], vs[name]))
        g_, d_, nm_, nv_ = _adamw(w2, g, m2, v2, rb=rb, name=f"adamw_{name}")
        grads[name], deltas[name], new_m[name], new_v[name] = (a.reshape(shp) for a in (g_, d_, nm_, nv_))

    small_names = [n for n in order if n not in grads]
    as2d = lambda a: a.reshape(-1, a.shape[-1])
    res = _adamw_small([tuple(as2d(a) for a in (weights[n], g_small[n], ms[n], vs[n])) for n in small_names])
    for n, (d_, nm_, nv_) in zip(small_names, res):
        shp = weights[n].shape
        grads[n], deltas[n], new_m[n], new_v[n] = (a.reshape(shp) for a in (g_small[n], d_, nm_, nv_))

    return (loss, grad_x[None], *[grads[n] for n in order], *[deltas[n] for n in order],
            *[new_m[n] for n in order], *[new_v[n] for n in order])
```

```python
import jax
import jax.numpy as jnp
from jax import lax
from jax.experimental import pallas as pl
from jax.experimental.pallas import tpu as pltpu

F32 = jnp.float32
BF16 = jnp.bfloat16
MESH = pl.DeviceIdType.MESH

D = 1024
NH = 8
HD = 128
DFF = 2816
NMETA = 16
EPS = 1e-6
TR = 256
PAD = TR - NMETA
CH = 64
NSHARD = 4
VMEM_LIMIT = 62 * 1024 * 1024

ADAM_LR = 0.001
ADAM_B1 = 0.9
ADAM_B2 = 0.999
ADAM_EPS = 1e-08
ADAM_WD = 0.01
ADAM_STEP = 10


def _cparams(semantics=None, **kw):
    return pltpu.CompilerParams(dimension_semantics=semantics, vmem_limit_bytes=VMEM_LIMIT, **kw)


def _sigmoid(x):
    return 0.5 * jnp.tanh(0.5 * x) + 0.5


def _weight_grad(at, b, *, bn, bk, name, shard_cols=None, ride=(), ring=False):
    M, T = at.shape
    N = b.shape[1]
    assert b.shape[0] == T and T % bk == 0 and N % bn == 0, (name, M, N, T, bn, bk)
    nk = T // bk
    nj = N // bn
    nex = len(ride)
    if shard_cols is None:
        nsh = 1
        blk = (M, bn)
        out_shape = jax.ShapeDtypeStruct((M, N), BF16)
        out_spec = pl.BlockSpec(blk, lambda j, k: (0, j))
    else:
        assert bn % shard_cols == 0 and N % shard_cols == 0
        nsh = bn // shard_cols
        blk = (nsh, M, shard_cols)
        out_shape = jax.ShapeDtypeStruct((N // shard_cols, M, shard_cols), BF16)
        out_spec = pl.BlockSpec(blk, lambda j, k: (j, 0, 0))

    nsteps = nj * nk

    def body(*refs):
        a_ref, b_ref = refs[:2]
        o_ref = refs[2 + nex]
        acc_ref = refs[3 + 2 * nex]
        j, k = pl.program_id(0), pl.program_id(1)
        if ring:
            a_buf, b_buf, ring_sem = refs[4 + 2 * nex:7 + 2 * nex]
            s = j * nk + k

            def fetch(t):
                kt = pl.multiple_of((t % nk) * bk, 128)
                jt = pl.multiple_of((t // nk) * bn, 128)
                slot = t % 3
                return (pltpu.make_async_copy(a_ref.at[:, pl.ds(kt, bk)], a_buf.at[slot], ring_sem.at[0, slot]),
                        pltpu.make_async_copy(b_ref.at[pl.ds(kt, bk), pl.ds(jt, bn)], b_buf.at[slot], ring_sem.at[1, slot]))

            @pl.when(s == 0)
            def _():
                for t in range(min(2, nsteps)):
                    for c in fetch(t):
                        c.start()

            @pl.when(s + 2 < nsteps)
            def _():
                for c in fetch(s + 2):
                    c.start()

            for c in fetch(s):
                c.wait()
            a_ref, b_ref = a_buf.at[s % 3], b_buf.at[s % 3]
        if nex:
            exchange = _PairExchange(refs[2:2 + nex], refs[3 + nex:3 + 2 * nex], refs[-2], refs[-1])

            @pl.when(jnp.logical_and(j == 0, k == 0))
            def _():
                exchange.send()

        @pl.when(k == 0)
        def _():
            acc_ref[...] = jnp.zeros_like(acc_ref)

        p = jnp.dot(a_ref[...].astype(BF16), b_ref[...].astype(BF16), preferred_element_type=F32)
        if shard_cols is None:
            acc_ref[...] += p
        else:
            for q in range(nsh):
                acc_ref[q] += p[:, q * shard_cols:(q + 1) * shard_cols]

        @pl.when(k == nk - 1)
        def _():
            o_ref[...] = acc_ref[...].astype(BF16)

        if nex:
            @pl.when(jnp.logical_and(j == nj - 1, k == nk - 1))
            def _():
                exchange.finish()

    hbm = pl.BlockSpec(memory_space=pl.ANY)
    scratch = [pltpu.VMEM(blk, F32)]
    if ring:
        scratch += [pltpu.VMEM((3, M, bk), at.dtype), pltpu.VMEM((3, bk, bn), b.dtype), pltpu.SemaphoreType.DMA((2, 3))]
        operand_specs = [hbm, hbm]
    else:
        operand_specs = [pl.BlockSpec((M, bk), lambda j, k: (0, k)), pl.BlockSpec((bk, bn), lambda j, k: (k, j))]
    if nex:
        scratch += [pltpu.SemaphoreType.DMA((nex,)), pltpu.SemaphoreType.DMA((nex,))]
    res = pl.pallas_call(
        body, name=name, grid=(nj, nk), in_specs=operand_specs + [hbm] * nex,
        out_specs=[out_spec] + [hbm] * nex, out_shape=[out_shape] + _pair_shapes(ride), scratch_shapes=scratch,
        compiler_params=_cparams(("arbitrary", "arbitrary") if nex or ring else ("parallel", "arbitrary")),
    )(at, b, *ride)
    return (res[0], list(res[1:])) if nex else res[0]


def _input_grad_block(dproj, x_seq, head_tile, dres, w, w_in_g, psums):
    T = TR + x_seq.shape[0]
    nt = T // TR
    nsl = w_in_g.shape[0]
    wsl = w_in_g.shape[2]
    nex = len(psums)

    def body(*refs):
        dp_ref, x_ref, head_ref, dres_ref, w_ref, win_ref = refs[:6]
        gx_ref, dmeta_ref, dw_ref = refs[6 + nex:9 + nex]
        exchange = _ChipExchange(refs[6:6 + nex], refs[9 + nex:9 + 2 * nex], refs[-2], refs[-1])
        i = pl.program_id(0)

        @pl.when(i == 0)
        def _():
            exchange.send()

        @pl.when(i == nt - 1)
        def _():
            exchange.finish()

        x = jnp.where(i == 0, head_ref[...], x_ref[...])
        r = lax.rsqrt(jnp.mean(x * x, axis=-1, keepdims=True) + EPS)
        n = x * r
        du_v = None
        for s in range(nsl):
            part = _dot_nt(dp_ref[:, s * wsl:(s + 1) * wsl], win_ref[s])
            du_v = part if du_v is None else du_v + part
        dn = du_v * w_ref[...]
        dh = dres_ref[...] + r * (dn - n * jnp.mean(dn * n, axis=-1, keepdims=True))
        gx_ref[...] = dh
        part = jnp.sum(du_v * n, axis=0, keepdims=True)

        @pl.when(i == 0)
        def _():
            dmeta_ref[...] = dh[PAD:TR, :]
            dw_ref[...] = part

        @pl.when(i > 0)
        def _():
            dw_ref[...] += part

    row = pl.BlockSpec((TR, D), lambda i: (i, 0))
    vec = pl.BlockSpec((1, D), lambda i: (0, 0))
    hbm = pl.BlockSpec(memory_space=pl.ANY)
    return pl.pallas_call(
        body, name="input_grad_block", grid=(nt,),
        in_specs=[pl.BlockSpec((TR, 9 * D), lambda i: (i, 0)),
                  pl.BlockSpec((TR, D), lambda i: (jnp.maximum(i - 1, 0), 0)), pl.BlockSpec((TR, D), lambda i: (0, 0)),
                  row, vec, pl.BlockSpec(memory_space=pltpu.VMEM)] + [hbm] * nex,
        out_specs=[pl.BlockSpec((TR, D), lambda i: (jnp.maximum(i - 1, 0), 0)),
                   pl.BlockSpec((NMETA, D), lambda i: (0, 0)), vec] + [hbm] * nex,
        out_shape=[jax.ShapeDtypeStruct((T - TR, D), F32), jax.ShapeDtypeStruct((NMETA, D), F32),
                   jax.ShapeDtypeStruct((1, D), F32)]
        + [jax.ShapeDtypeStruct((3,) + p.shape[1:], p.dtype) for p in psums],
        scratch_shapes=[pltpu.SemaphoreType.DMA((nex, 3)), pltpu.SemaphoreType.DMA((nex, 3))],
        compiler_params=_cparams(("arbitrary",)),
    )(dproj, x_seq, head_tile, dres, w, w_in_g, *psums)


def _tri_matmul(tri_bf16, x):
    hi = x.astype(BF16)
    lo = (x - hi.astype(F32)).astype(BF16)
    return jnp.dot(tri_bf16, lo, preferred_element_type=F32) + jnp.dot(tri_bf16, hi, preferred_element_type=F32)


def _shift_down(x, prev8, n):
    rows = x.shape[0]
    return pltpu.roll(jnp.concatenate([prev8, x], axis=0), n, 0)[8:8 + rows, :]


def _shift_up(x, next8, n):
    rows = x.shape[0]
    return pltpu.roll(jnp.concatenate([x, next8], axis=0), rows + 8 - n, 0)[0:rows, :]


def _gates(f_raw, lb):
    sg = _sigmoid(f_raw)
    f = lb + (1.0 - lb) * sg
    return sg, f, jnp.log(f), 1.0 - f


def _lower_bound(lbp_ref):
    return _sigmoid(lbp_ref[0:1, :] - lbp_ref[1:2, :])


def _tri_masks():
    r = lax.broadcasted_iota(jnp.int32, (CH, CH), 0)
    c = lax.broadcasted_iota(jnp.int32, (CH, CH), 1)
    return r >= c, r <= c


def _ones_where(mask):
    return jnp.where(mask, 1.0, 0.0).astype(BF16)


def _dot(a, b):
    return jnp.dot(a.astype(BF16), b.astype(BF16), preferred_element_type=F32)


def _dot_nt(a, b):
    return lax.dot_general(a.astype(BF16), b.astype(BF16), (((1,), (1,)), ((), ())), preferred_element_type=F32)


def _dot_tn(a, b):
    return lax.dot_general(a.astype(BF16), b.astype(BF16), (((0,), (0,)), ((), ())), preferred_element_type=F32)


def _mix_block_fwd(x_seq, head_tile, lb_param, wattn, whn, conv_w, w_in_g, w_out, late, late_paired):
    T = TR + x_seq.shape[0]
    nt = T // TR
    ncht = TR // CH
    nsl = w_in_g.shape[0]
    wsl = w_in_g.shape[2]
    nlate = len(late)

    def body(*refs):
        x_ref, head_ref, lbp_ref, wa_ref, whn_ref, cw_ref, win_ref, wout_ref = refs[:8]
        ut_ref, projb_ref, o_ref, sst_ref, mt_ref, h1_ref = refs[8 + nlate:14 + nlate]
        late_refs = refs[14 + nlate:14 + 2 * nlate]
        proj_ref, m_ref, st_ref, cxc_ref, send_sems, recv_sems = refs[14 + 2 * nlate:]
        i = pl.program_id(0)
        gather = _Gather(late_refs, send_sems, recv_sems, (False,) * nlate, late_paired)

        @pl.when(i == 0)
        def _():
            st_ref[...] = jnp.zeros_like(st_ref)
            cxc_ref[...] = jnp.zeros_like(cxc_ref)
            gather.send()

        @pl.when(i == nt // 2)
        def _():
            gather.relay()

        @pl.when(i == (3 * nt) // 4)
        def _():
            gather.relay_diagonal()

        x = jnp.where(i == 0, head_ref[...], x_ref[...])
        r1 = lax.rsqrt(jnp.mean(x * x, axis=-1, keepdims=True) + EPS)
        u_f = x * r1 * wa_ref[...]
        u = u_f.astype(BF16)
        ut_ref[...] = u_f.T.astype(BF16)
        for s in range(nsl):
            cols = slice(s * wsl, (s + 1) * wsl)
            p = jnp.dot(u, win_ref[s], preferred_element_type=F32)
            proj_ref[:, cols] = p
            projb_ref[:, cols] = p.astype(BF16)

        lb = _lower_bound(lbp_ref)
        lower, _ = _tri_masks()
        ltri = _ones_where(lower)
        whn_v = whn_ref[...]
        w0, w1, w2 = cw_ref[0:1, :], cw_ref[1:2, :], cw_ref[2:3, :]

        def gates_and_conv(c):
            rows = slice(c * CH, (c + 1) * CH)
            q_raw = proj_ref[rows, 0:D]
            f_raw = proj_ref[rows, D:2 * D]
            q = q_raw * _sigmoid(q_raw)
            _, _, g, k = _gates(f_raw, lb)
            gam = _tri_matmul(ltri, g)
            gam_l = gam[CH - 1:CH, :]
            g_out = proj_ref[rows, 3 * D:4 * D]
            cx = proj_ref[rows, 5 * D:6 * D] * proj_ref[rows, 6 * D:7 * D]
            prev8 = cxc_ref[...]
            cv = w0 * _shift_down(cx, prev8, 2) + w1 * _shift_down(cx, prev8, 1) + w2 * cx
            cxc_ref[...] = cx[CH - 8:CH, :]
            y_b = proj_ref[rows, 4 * D:5 * D] * cv
            return dict(
                e_l=jnp.exp(gam_l), qt=(q * jnp.exp(gam)).astype(BF16), kt=(k * jnp.exp(-gam)).astype(BF16),
                khat=(k * jnp.exp(gam_l - gam)).astype(BF16), vb=proj_ref[rows, 2 * D:3 * D].astype(BF16),
                out_gate=whn_v * (g_out * _sigmoid(g_out)), merge_a=_sigmoid(proj_ref[rows, 7 * D:8 * D]),
                merged_b=_sigmoid(proj_ref[rows, 8 * D:9 * D]) * y_b)

        def recurrence_and_merge(c, pre):
            rows = slice(c * CH, (c + 1) * CH)
            qt, kt, khat, vb, e_l = pre["qt"], pre["kt"], pre["khat"], pre["vb"], pre["e_l"]
            heads = [slice(h * HD, (h + 1) * HD) for h in range(NH)]
            sts = [st_ref[h] for h in range(NH)]
            scores = [_dot_nt(qt[:, cs], kt[:, cs]) for cs in heads]
            o_state = [_dot_nt(qt[:, cs], st) for cs, st in zip(heads, sts)]
            st_new = [_dot_tn(vb[:, cs], khat[:, cs]) for cs in heads]
            on_parts = []
            for h, cs in enumerate(heads):
                sst_ref[c, h] = sts[h]
                st_ref[h] = sts[h] * e_l[:, cs] + st_new[h]
                o_h = o_state[h] + _dot(jnp.where(lower, scores[h], 0.0), vb[:, cs])
                o_ref[rows, cs] = o_h
                ro = lax.rsqrt(jnp.mean(o_h * o_h, axis=-1, keepdims=True) + EPS)
                on_parts.append(o_h * ro)
            on = jnp.concatenate(on_parts, axis=1)
            m_ref[rows, :] = pre["merge_a"] * (on * pre["out_gate"]) + pre["merged_b"]

        pre = {0: gates_and_conv(0)}
        for c in range(ncht):
            if c + 1 < ncht:
                pre[c + 1] = gates_and_conv(c + 1)
            recurrence_and_merge(c, pre.pop(c))
        m_v = m_ref[...]
        h1_ref[...] = x + jnp.dot(m_v.astype(BF16), wout_ref[...], preferred_element_type=F32)
        mt_ref[...] = m_v.T.astype(BF16)

        @pl.when(i == nt - 1)
        def _():
            gather.finish()

    row = lambda w: pl.BlockSpec((TR, w), lambda i: (i, 0))
    col = lambda w: pl.BlockSpec((w, TR), lambda i: (0, i))
    vec = lambda r: pl.BlockSpec((r, D), lambda i: (0, 0))
    vm = pl.BlockSpec(memory_space=pltpu.VMEM)
    hbm = pl.BlockSpec(memory_space=pl.ANY)
    return pl.pallas_call(
        body, name="mix_block_fwd", grid=(nt,),
        in_specs=[pl.BlockSpec((TR, D), lambda i: (jnp.maximum(i - 1, 0), 0)), pl.BlockSpec((TR, D), lambda i: (0, 0)),
                  vec(2), vec(1), vec(1), vec(3), vm, vm] + [hbm] * nlate,
        out_specs=[col(D), row(9 * D), row(D), pl.BlockSpec((ncht, NH, HD, HD), lambda i: (i, 0, 0, 0)),
                   col(D), row(D)] + [hbm] * nlate,
        out_shape=[jax.ShapeDtypeStruct((D, T), BF16), jax.ShapeDtypeStruct((T, 9 * D), BF16),
                   jax.ShapeDtypeStruct((T, D), F32), jax.ShapeDtypeStruct((T // CH, NH, HD, HD), F32),
                   jax.ShapeDtypeStruct((D, T), BF16), jax.ShapeDtypeStruct((T, D), F32)]
        + [jax.ShapeDtypeStruct(a.shape, a.dtype) for a in late],
        input_output_aliases={8 + n: 6 + n for n in range(nlate)},
        scratch_shapes=[pltpu.VMEM((TR, 9 * D), F32), pltpu.VMEM((TR, D), F32), pltpu.VMEM((NH, HD, HD), F32),
                        pltpu.VMEM((8, D), F32), pltpu.SemaphoreType.DMA((nlate, 8)),
                        pltpu.SemaphoreType.DMA((nlate, 8))],
        compiler_params=_cparams(("arbitrary",)),
    )(x_seq, head_tile, lb_param, wattn, whn, conv_w, w_in_g, w_out, *late)


def _mix_block_bwd(dh1, projb, o, sst, lb_param, whn, conv_w, w_out, psums):
    T = projb.shape[0]
    nt = T // TR
    ncht = TR // CH
    tb16 = TR // 16
    nex = len(psums)

    def body(*refs):
        dh1_ref, proj_ref, pc_ref, px_ref, o_ref, sst_ref, lbp_ref, whn_ref, cw_ref, wout_ref = refs[:10]
        dproj_ref, dlb_ref, dwhn_ref, dcw_ref = refs[10 + nex:14 + nex]
        exchange = _ChipExchange(refs[10:10 + nex], refs[14 + nex:14 + 2 * nex], refs[-2], refs[-1])
        dm_ref, dst_ref, dcvc_ref, acc_lb, acc_hn, acc_cw = refs[14 + 2 * nex:-2]
        s = pl.program_id(0)
        tile = nt - 1 - s

        @pl.when(s == 0)
        def _():
            dst_ref[...] = jnp.zeros_like(dst_ref)
            dcvc_ref[...] = jnp.zeros_like(dcvc_ref)
            acc_lb[...] = jnp.zeros_like(acc_lb)
            acc_hn[...] = jnp.zeros_like(acc_hn)
            acc_cw[...] = jnp.zeros_like(acc_cw)
            exchange.send()

        dm_ref[...] = _dot_nt(dh1_ref[...], wout_ref[...])
        lb = _lower_bound(lbp_ref)
        lower, upper = _tri_masks()
        ltri = _ones_where(lower)
        utri = _ones_where(upper)
        whn_v = whn_ref[...]
        w0, w1, w2 = cw_ref[0:1, :], cw_ref[1:2, :], cw_ref[2:3, :]
        cx_before_tile = jnp.where(tile > 0, (pc_ref[...].astype(F32) * px_ref[...].astype(F32))[8:16, :], 0.0)
        rid = lax.broadcasted_iota(jnp.int32, (CH, D), 0)

        def chunk(cc, carry):
            c = ncht - 1 - cc
            r0 = pl.multiple_of(c * CH, CH)
            rows = pl.ds(r0, CH)
            slab = lambda n: proj_ref[rows, n * D:(n + 1) * D].astype(F32)
            q_raw, f_raw, v, g_out, b_gate, c_gate, x_conv = (slab(n) for n in range(7))
            sa = _sigmoid(slab(7))
            sb = _sigmoid(slab(8))
            dm_v = dm_ref[rows, :]

            sq = _sigmoid(q_raw)
            q = q_raw * sq
            sg, f, g, k = _gates(f_raw, lb)
            gam = _tri_matmul(ltri, g)
            gam_l = gam[CH - 1:CH, :]
            e_l = jnp.exp(gam_l)
            e_g = jnp.exp(gam)
            e_ng = jnp.exp(-gam)
            e_kl = jnp.exp(gam_l - gam)
            qt = q * e_g
            kt = k * e_ng
            khat = k * e_kl
            qt_b, kt_b, khat_b, v_b = qt.astype(BF16), kt.astype(BF16), khat.astype(BF16), v.astype(BF16)
            qt_seen, kt_seen = qt_b.astype(F32), kt_b.astype(F32)
            s_go = _sigmoid(g_out)
            silu_go = g_out * s_go
            cx = c_gate * x_conv
            rprev = pl.ds(pl.multiple_of(jnp.maximum(r0 - 16, 0), 16), 16)
            cx_prev_in = (proj_ref[rprev, 5 * D:6 * D].astype(F32) * proj_ref[rprev, 6 * D:7 * D].astype(F32))[8:16, :]
            prev8 = jnp.where(c > 0, cx_prev_in, cx_before_tile)
            cx_m1 = _shift_down(cx, prev8, 1)
            cx_m2 = _shift_down(cx, prev8, 2)
            cv = w0 * cx_m2 + w1 * cx_m1 + w2 * cx
            y_b = b_gate * cv

            o_v = o_ref[rows, :]
            ro_parts, on_parts = [], []
            for h in range(NH):
                cs = slice(h * HD, (h + 1) * HD)
                o_h = o_v[:, cs]
                ro = lax.rsqrt(jnp.mean(o_h * o_h, axis=-1, keepdims=True) + EPS)
                ro_parts.append(ro)
                on_parts.append(o_h * ro)
            on = jnp.concatenate(on_parts, axis=1)
            out_gate = whn_v * silu_go
            y_a = on * out_gate

            dy_a = dm_v * sa
            dy_b = dm_v * sb
            dproj_ref[rows, 7 * D:8 * D] = (dy_a * y_a * (1.0 - sa)).astype(BF16)
            dproj_ref[rows, 8 * D:9 * D] = (dy_b * y_b * (1.0 - sb)).astype(BF16)
            dproj_ref[rows, 4 * D:5 * D] = (dy_b * cv).astype(BF16)
            dcv = dy_b * b_gate
            acc_cw[0:1, :] += jnp.sum(dcv * cx_m2, axis=0, keepdims=True)
            acc_cw[1:2, :] += jnp.sum(dcv * cx_m1, axis=0, keepdims=True)
            acc_cw[2:3, :] += jnp.sum(dcv * cx, axis=0, keepdims=True)
            next8 = dcvc_ref[...]
            dcx = w2 * dcv + w1 * _shift_up(dcv, next8, 1) + w0 * _shift_up(dcv, next8, 2)
            dcvc_ref[...] = dcv[0:8, :]
            dproj_ref[rows, 5 * D:6 * D] = (dcx * x_conv).astype(BF16)
            dproj_ref[rows, 6 * D:7 * D] = (dcx * c_gate).astype(BF16)
            don = dy_a * out_gate
            dya_on = dy_a * on
            dproj_ref[rows, 3 * D:4 * D] = (dya_on * (whn_v * (s_go * (1.0 + g_out * (1.0 - s_go))))).astype(BF16)
            acc_hn[...] += jnp.sum(dya_on * silu_go, axis=0, keepdims=True)

            heads = [slice(h * HD, (h + 1) * HD) for h in range(NH)]
            do_bs, sts, dstns, first = [], [], [], []
            for h, cs in enumerate(heads):
                on_h = on_parts[h]
                don_h = don[:, cs]
                do_h = ro_parts[h] * (don_h - on_h * jnp.mean(don_h * on_h, axis=-1, keepdims=True))
                do_bs.append(do_h.astype(BF16))
                sts.append(sst_ref[c, h])
                dstns.append(dst_ref[h])
            for h, cs in enumerate(heads):
                qt_h, kt_h, khat_h, v_h = qt_b[:, cs], kt_b[:, cs], khat_b[:, cs], v_b[:, cs]
                do_b, dstn_b = do_bs[h], dstns[h].astype(BF16)
                first.append(dict(
                    a_t=_dot_nt(kt_h, qt_h), da=_dot_nt(do_b, v_h), da_t=_dot_nt(v_h, do_b),
                    dv_state=_dot_nt(khat_h, dstn_b), dqt_state=_dot(do_b, sts[h]), dkhat=_dot(v_h, dstn_b),
                    dst_chunk=_dot_tn(do_b, qt_h)))
            dq_parts, dk_parts, dv_parts, dgam_parts, ext_parts = [], [], [], [], []
            for h, cs in enumerate(heads):
                qt_h, kt_h = qt_b[:, cs], kt_b[:, cs]
                p1, do_b, st, dstn = first[h], do_bs[h], sts[h], dstns[h]
                dv_h = _dot(jnp.where(upper, p1["a_t"], 0.0), do_b) + p1["dv_state"]
                dqt_state = p1["dqt_state"]
                dqt_chunk = _dot(jnp.where(lower, p1["da"], 0.0), kt_h)
                dkt = _dot(jnp.where(upper, p1["da_t"], 0.0), qt_h)
                dkhat = p1["dkhat"]
                dq_h = (dqt_state + dqt_chunk) * e_g[:, cs]
                dk_h = dkt * e_ng[:, cs] + dkhat * e_kl[:, cs]
                khat_dkhat = dkhat * khat[:, cs]
                ext = (jnp.sum(khat_dkhat, axis=0, keepdims=True)
                       + e_l[:, cs] * jnp.sum(st * dstn, axis=0, keepdims=True))
                dst_ref[h] = p1["dst_chunk"] + dstn * e_l[:, cs]
                dq_parts.append(dq_h)
                dk_parts.append(dk_h)
                dv_parts.append(dv_h)
                dgam_parts.append(qt[:, cs] * dqt_state + qt_seen[:, cs] * dqt_chunk - kt_seen[:, cs] * dkt
                                  - khat_dkhat)
                ext_parts.append(ext)
            dq = jnp.concatenate(dq_parts, axis=1)
            dk = jnp.concatenate(dk_parts, axis=1)
            dgam = jnp.concatenate(dgam_parts, axis=1)
            ext = jnp.concatenate(ext_parts, axis=1)
            dgam = dgam + jnp.where(rid == CH - 1, ext, 0.0)
            dg = _tri_matmul(utri, dgam)
            dproj_ref[rows, 0:D] = (dq * (sq * (1.0 + q_raw * (1.0 - sq)))).astype(BF16)
            df = dg * jnp.exp(-g) - dk
            dproj_ref[rows, D:2 * D] = (df * (1.0 - lb) * sg * (1.0 - sg)).astype(BF16)
            dproj_ref[rows, 2 * D:3 * D] = jnp.concatenate(dv_parts, axis=1).astype(BF16)
            real = (tile * TR + r0 + rid) >= PAD
            acc_lb[...] += jnp.sum(jnp.where(real, df * (1.0 - sg), 0.0), axis=0, keepdims=True)
            return carry

        lax.fori_loop(0, ncht, chunk, 0)

        @pl.when(s == nt - 1)
        def _():
            dlb_ref[...] = acc_lb[...] * lb * (1.0 - lb)
            hn = acc_hn[...]
            tot = hn[:, 0:HD]
            for h in range(1, NH):
                tot = tot + hn[:, h * HD:(h + 1) * HD]
            dwhn_ref[...] = tot
            dcw_ref[...] = acc_cw[0:3, :]
            exchange.finish()

    hbm = pl.BlockSpec(memory_space=pl.ANY)
    rev = lambda s: (nt - 1 - s, 0)
    prevc = lambda s: (jnp.maximum((nt - 1 - s) * tb16 - 1, 0), 5)
    prevx = lambda s: (jnp.maximum((nt - 1 - s) * tb16 - 1, 0), 6)
    const = lambda s: (0, 0)
    return pl.pallas_call(
        body, name="mix_block_bwd", grid=(nt,),
        in_specs=[pl.BlockSpec((TR, D), rev),
                  pl.BlockSpec((TR, 9 * D), rev),
                  pl.BlockSpec((16, D), prevc),
                  pl.BlockSpec((16, D), prevx),
                  pl.BlockSpec((TR, D), rev),
                  pl.BlockSpec((ncht, NH, HD, HD), lambda s: (nt - 1 - s, 0, 0, 0)),
                  pl.BlockSpec((2, D), const),
                  pl.BlockSpec((1, D), const),
                  pl.BlockSpec((3, D), const),
                  pl.BlockSpec(memory_space=pltpu.VMEM)] + [hbm] * nex,
        out_specs=[pl.BlockSpec((TR, 9 * D), rev),
                   pl.BlockSpec((1, D), const),
                   pl.BlockSpec((1, HD), const),
                   pl.BlockSpec((3, D), const)] + [hbm] * nex,
        out_shape=[jax.ShapeDtypeStruct((T, 9 * D), BF16), jax.ShapeDtypeStruct((1, D), F32),
                   jax.ShapeDtypeStruct((1, HD), F32), jax.ShapeDtypeStruct((3, D), F32)]
        + [jax.ShapeDtypeStruct((3,) + p.shape[1:], p.dtype) for p in psums],
        scratch_shapes=[pltpu.VMEM((TR, D), F32), pltpu.VMEM((NH, HD, HD), F32), pltpu.VMEM((8, D), F32),
                        pltpu.VMEM((1, D), F32), pltpu.VMEM((1, D), F32), pltpu.VMEM((8, D), F32),
                        pltpu.SemaphoreType.DMA((nex, 3)), pltpu.SemaphoreType.DMA((nex, 3))],
        compiler_params=_cparams(("arbitrary",)),
    )(dh1, projb, projb, projb, o, sst, lb_param, whn, conv_w, w_out, *psums)


def _ffn_block_fwd(h1, tgt, wffn, w_up_g, fcw, fcb, w_down, wfin):
    T = h1.shape[0]
    nt = T // TR
    nsl = w_up_g.shape[0]
    wsl = w_up_g.shape[2]

    def body(h_ref, t_ref, wn_ref, wup_ref, cw_ref, cb_ref, wdn_ref, wf_ref,
             u2t_ref, upb_ref, ggt_ref, dh_ref, loss_ref, dwf_ref, up_scr, gg_ref, carry_ref):
        i = pl.program_id(0)

        @pl.when(i == 0)
        def _():
            carry_ref[...] = jnp.zeros_like(carry_ref)
            loss_ref[...] = jnp.zeros_like(loss_ref)
            dwf_ref[...] = jnp.zeros_like(dwf_ref)

        x = h_ref[...]
        r2 = lax.rsqrt(jnp.mean(x * x, axis=-1, keepdims=True) + EPS)
        u2_f = x * r2 * wn_ref[...]
        u2 = u2_f.astype(BF16)
        u2t_ref[...] = u2_f.T.astype(BF16)
        for s in range(nsl):
            up_s = jnp.dot(u2, wup_ref[s], preferred_element_type=F32)
            up_scr[:, s * wsl:(s + 1) * wsl] = up_s
            upb_ref[:, s * wsl:(s + 1) * wsl] = up_s.astype(BF16)
        w0, w1, w2 = cw_ref[0:1, :], cw_ref[1:2, :], cw_ref[2:3, :]

        def chunk(c, carry):
            rows = pl.ds(pl.multiple_of(c * CH, CH), CH)
            a_pre = up_scr[rows, 0:DFF]
            val = up_scr[rows, DFF:2 * DFF]
            prev8 = carry_ref[...]
            a = w0 * _shift_down(a_pre, prev8, 2) + w1 * _shift_down(a_pre, prev8, 1) + w2 * a_pre + cb_ref[...]
            carry_ref[...] = a_pre[CH - 8:CH, :]
            gg_ref[rows, :] = a * _sigmoid(a) * val
            return carry

        lax.fori_loop(0, TR // CH, chunk, 0, unroll=True)
        gg_v = gg_ref[...]
        ggt_ref[...] = gg_v.T.astype(BF16)
        h2 = x + jnp.dot(gg_v.astype(BF16), wdn_ref[...], preferred_element_type=F32)
        r3 = lax.rsqrt(jnp.mean(h2 * h2, axis=-1, keepdims=True) + EPS)
        n3 = h2 * r3
        wf = wf_ref[...]
        diff = jnp.where(i > 0, n3 * wf - t_ref[...], 0.0)
        loss_ref[...] += jnp.sum(diff * diff, axis=0, keepdims=True) * (0.5 / D)
        dy = diff * (1.0 / D)
        dwf_ref[...] += jnp.sum(dy * n3, axis=0, keepdims=True)
        dn = dy * wf
        dh_ref[...] = r3 * (dn - n3 * jnp.mean(dn * n3, axis=-1, keepdims=True))

    row = lambda w: pl.BlockSpec((TR, w), lambda i: (i, 0))
    col = lambda w: pl.BlockSpec((w, TR), lambda i: (0, i))
    vec = lambda w, r=1: pl.BlockSpec((r, w), lambda i: (0, 0))
    vm = pl.BlockSpec(memory_space=pltpu.VMEM)
    return pl.pallas_call(
        body, name="ffn_block_fwd", grid=(nt,),
        in_specs=[row(D), pl.BlockSpec((TR, D), lambda i: (jnp.maximum(i - 1, 0), 0)), vec(D), vm,
                  vec(DFF, 3), vec(DFF), vm, vec(D)],
        out_specs=[col(D), row(2 * DFF), col(DFF), row(D), vec(D), vec(D)],
        out_shape=[jax.ShapeDtypeStruct((D, T), BF16), jax.ShapeDtypeStruct((T, 2 * DFF), BF16),
                   jax.ShapeDtypeStruct((DFF, T), BF16), jax.ShapeDtypeStruct((T, D), F32),
                   jax.ShapeDtypeStruct((1, D), F32), jax.ShapeDtypeStruct((1, D), F32)],
        scratch_shapes=[pltpu.VMEM((TR, 2 * DFF), F32), pltpu.VMEM((TR, DFF), F32), pltpu.VMEM((8, DFF), F32)],
        compiler_params=_cparams(("arbitrary",)),
    )(h1, tgt, wffn, w_up_g, fcw, fcb, w_down, wfin)


def _ffn_block_bwd(dh2, upb, h1, wffn, w_up_g, fcw, fcb, w_down):
    T = h1.shape[0]
    nt = T // TR
    ncht = TR // CH
    nsl = w_up_g.shape[0]
    wsl = w_up_g.shape[2]
    tb16 = TR // 16
    assert ncht % nsl == 0
    every = ncht // nsl
    step = -(-DFF // (ncht * 128)) * 128
    parts = [(c0, min(c0 + step, DFF)) for c0 in range(0, DFF, step)]
    assert len(parts) == ncht

    def body(dh2n_ref, dh2p_ref, up_ref, pa_ref, h_ref, wn_ref, wup_ref, cw_ref, cb_ref, wdn_ref,
             dup_ref, dh1_ref, dfw_ref, dfb_ref, dwn_ref,
             dgg_ring, dup_ring, carry_ref, acc_w, acc_b, acc_n):
        s = pl.program_id(0)
        slot = lax.rem(s, 2)
        other = 1 - slot

        @pl.when(s == 0)
        def _():
            carry_ref[...] = jnp.zeros_like(carry_ref)
            acc_w[...] = jnp.zeros_like(acc_w)
            acc_b[...] = jnp.zeros_like(acc_b)
            acc_n[...] = jnp.zeros_like(acc_n)
            dup_ring[1] = jnp.zeros((TR, 2 * DFF), BF16)
            dgg_ring[0] = _dot_nt(dh2p_ref[...], wdn_ref[...])

        def norm_bwd(du2, valid):
            x = h_ref[...]
            dh2p = dh2p_ref[...]
            r2 = lax.rsqrt(jnp.mean(x * x, axis=-1, keepdims=True) + EPS)
            n2 = x * r2
            dn = du2 * wn_ref[...]
            dh1_ref[...] = dh2p + r2 * (dn - n2 * jnp.mean(dn * n2, axis=-1, keepdims=True))
            acc_n[...] += jnp.where(valid, jnp.sum(du2 * n2, axis=0, keepdims=True), 0.0)

        @pl.when(s < nt)
        def _():
            tile = nt - 1 - s
            w0, w1, w2 = cw_ref[0:1, :], cw_ref[1:2, :], cw_ref[2:3, :]
            a_before_tile = jnp.where(tile > 0, pa_ref[...].astype(F32)[8:16, :], 0.0)
            dh2n = dh2n_ref[...].astype(BF16)
            du2 = None
            for idx in range(ncht):
                c = ncht - 1 - idx
                r0 = c * CH
                rows = slice(r0, r0 + CH)
                a_pre = up_ref[rows, 0:DFF].astype(F32)
                val = up_ref[rows, DFF:2 * DFF].astype(F32)
                prev8 = up_ref[r0 - 16:r0, 0:DFF].astype(F32)[8:16, :] if c > 0 else a_before_tile
                a_m1 = _shift_down(a_pre, prev8, 1)
                a_m2 = _shift_down(a_pre, prev8, 2)
                a = w0 * a_m2 + w1 * a_m1 + w2 * a_pre + cb_ref[...]
                sig = _sigmoid(a)
                dgg_v = dgg_ring[slot, rows, :]
                da = dgg_v * val * (sig * (1.0 + a * (1.0 - sig)))
                dval = (dgg_v * (a * sig)).astype(BF16)
                next8 = carry_ref[...]
                da_pre = (w2 * da + w1 * _shift_up(da, next8, 1) + w0 * _shift_up(da, next8, 2)).astype(BF16)
                carry_ref[...] = da[0:8, :]
                dup_ref[rows, 0:DFF] = da_pre
                dup_ref[rows, DFF:2 * DFF] = dval
                dup_ring[slot, rows, 0:DFF] = da_pre
                dup_ring[slot, rows, DFF:2 * DFF] = dval
                acc_w[0:1, :] += jnp.sum(da * a_m2, axis=0, keepdims=True)
                acc_w[1:2, :] += jnp.sum(da * a_m1, axis=0, keepdims=True)
                acc_w[2:3, :] += jnp.sum(da * a_pre, axis=0, keepdims=True)
                acc_b[...] += jnp.sum(da, axis=0, keepdims=True)
                c0, c1 = parts[idx]
                dgg_ring[other, :, c0:c1] = _dot_nt(dh2n, wdn_ref[c0:c1, :])
                if idx % every == 0:
                    sl = idx // every
                    part = _dot_nt(dup_ring[other, :, sl * wsl:(sl + 1) * wsl], wup_ref[sl])
                    du2 = part if du2 is None else du2 + part
            norm_bwd(du2, s > 0)

        @pl.when(s == nt)
        def _():
            du2 = _dot_nt(dup_ring[other, :, 0:wsl], wup_ref[0])
            for sl in range(1, nsl):
                du2 = du2 + _dot_nt(dup_ring[other, :, sl * wsl:(sl + 1) * wsl], wup_ref[sl])
            norm_bwd(du2, True)
            dfw_ref[...] = acc_w[0:3, :]
            dfb_ref[...] = acc_b[...]
            dwn_ref[...] = acc_n[...]

    gate_tile = lambda s: jnp.maximum(nt - 1 - s, 0)
    next_tile = lambda s: jnp.maximum(nt - 2 - s, 0)
    prev_tile = lambda s: jnp.minimum(nt - s, nt - 1)
    vec = lambda w, r=1: pl.BlockSpec((r, w), lambda s: (0, 0))
    vm = pl.BlockSpec(memory_space=pltpu.VMEM)
    return pl.pallas_call(
        body, name="ffn_block_bwd", grid=(nt + 1,),
        in_specs=[pl.BlockSpec((TR, D), lambda s: (next_tile(s), 0)),
                  pl.BlockSpec((TR, D), lambda s: (prev_tile(s), 0)),
                  pl.BlockSpec((TR, 2 * DFF), lambda s: (gate_tile(s), 0)),
                  pl.BlockSpec((16, DFF), lambda s: (jnp.maximum(gate_tile(s) * tb16 - 1, 0), 0)),
                  pl.BlockSpec((TR, D), lambda s: (prev_tile(s), 0)),
                  vec(D), vm, vec(DFF, 3), vec(DFF), vm],
        out_specs=[pl.BlockSpec((TR, 2 * DFF), lambda s: (gate_tile(s), 0)),
                   pl.BlockSpec((TR, D), lambda s: (prev_tile(s), 0)),
                   vec(DFF, 3), vec(DFF), vec(D)],
        out_shape=[jax.ShapeDtypeStruct((T, 2 * DFF), BF16), jax.ShapeDtypeStruct((T, D), F32),
                   jax.ShapeDtypeStruct((3, DFF), F32), jax.ShapeDtypeStruct((1, DFF), F32),
                   jax.ShapeDtypeStruct((1, D), F32)],
        scratch_shapes=[pltpu.VMEM((2, TR, DFF), F32), pltpu.VMEM((2, TR, 2 * DFF), BF16),
                        pltpu.VMEM((8, DFF), F32), pltpu.VMEM((8, DFF), F32), pltpu.VMEM((1, DFF), F32),
                        pltpu.VMEM((1, D), F32)],
        compiler_params=_cparams(("arbitrary",)),
    )(dh2, dh2, upb, upb, h1, wffn, w_up_g, fcw, fcb, w_down)


def _place():
    x, y, c = lax.axis_index("x"), lax.axis_index("y"), lax.axis_index("c")
    return x, y, c


_CHIP_FLIPS = ((1, 0), (0, 1), (1, 1))


def _flip(v, bit):
    return 1 - v if bit else v


def _into_slot(w, j_idx, *, rb, dtype, name, paired=False):
    r, cdim = w.shape

    def body(j_ref, w_ref, out_ref):
        del j_ref
        out_ref[...] = w_ref[...].astype(dtype)

    if paired:
        out_shape = jax.ShapeDtypeStruct((NSHARD // 2, r, 2 * cdim), dtype)
        out_spec = pl.BlockSpec((None, rb, cdim), lambda i, j_ref: (j_ref[0] // 2, i, j_ref[0] % 2))
    else:
        out_shape = jax.ShapeDtypeStruct((NSHARD, r, cdim), dtype)
        out_spec = pl.BlockSpec((None, rb, cdim), lambda i, j_ref: (j_ref[0], i, 0))
    grid_spec = pltpu.PrefetchScalarGridSpec(
        num_scalar_prefetch=1, grid=(r // rb,),
        in_specs=[pl.BlockSpec((rb, cdim), lambda i, j_ref: (i, 0))], out_specs=out_spec)
    return pl.pallas_call(
        body, name=name, grid_spec=grid_spec, out_shape=out_shape, compiler_params=_cparams(("parallel",)),
    )(j_idx, w)


class _Gather:
    def __init__(self, outs, send_sems, recv_sems, whole, paired=None):
        self.outs, self.send_sems, self.recv_sems, self.whole = outs, send_sems, recv_sems, whole
        self.paired = paired if paired is not None else (False,) * len(outs)
        self.x, self.y, self.c = _place()
        self.j = 2 * self.x + self.y
        self.sibling = (self.x, self.y, 1 - self.c)
        chips = [(_flip(self.x, fx), _flip(self.y, fy)) for fx, fy in _CHIP_FLIPS]
        self.slots = [2 * px + py for px, py in chips]
        self.peers = [(px, py, self.c) for px, py in chips]

    def _copy(self, w, slot, core, sem, to, quarter=None):
        r = self.outs[w].shape[1]
        if self.whole[w]:
            rows = pl.ds(0, r)
        elif quarter is None:
            rows = pl.ds(core * (r // 2), r // 2)
        else:
            rows = pl.ds(core * (r // 2) + quarter * (r // 4), r // 4)
        if self.paired[w]:
            cw = self.outs[w].shape[2] // 2
            piece = self.outs[w].at[slot // 2, rows, pl.ds((slot % 2) * cw, cw)]
        else:
            piece = self.outs[w].at[slot, rows, :]
        return pltpu.make_async_remote_copy(
            src_ref=piece, dst_ref=piece, send_sem=self.send_sems.at[w, sem], recv_sem=self.recv_sems.at[w, sem],
            device_id=to, device_id_type=MESH)

    def _direct(self, w):
        return (0, 1, 2) if self.whole[w] else (0, 1)

    def send(self):
        for w in range(len(self.outs)):
            for k in self._direct(w):
                self._copy(w, self.j, self.c, k, self.peers[k]).start()

    def relay(self):
        jx, jy, _ = self.slots
        for w in range(len(self.outs)):
            if self.whole[w]:
                for k in range(3):
                    self._copy(w, self.slots[k], self.c, k, self.sibling).wait_recv()
                continue
            self._copy(w, jx, self.c, 0, self.sibling).wait_recv()
            self._copy(w, jx, self.c, 6, self.peers[1], quarter=0).start()
            self._copy(w, jx, self.c, 3, self.sibling).start()
            self._copy(w, jy, self.c, 1, self.sibling).wait_recv()
            self._copy(w, jy, self.c, 7, self.peers[0], quarter=1).start()
            self._copy(w, jy, self.c, 4, self.sibling).start()

    def relay_diagonal(self):
        jd = self.slots[2]
        for w in range(len(self.outs)):
            if not self.whole[w]:
                self._copy(w, jd, self.c, 6, self.sibling, quarter=0).wait_recv()
                self._copy(w, jd, self.c, 7, self.sibling, quarter=1).wait_recv()
                self._copy(w, jd, self.c, 5, self.sibling).start()

    def finish(self):
        jx, jy, jd = self.slots
        for w in range(len(self.outs)):
            for k in self._direct(w):
                self._copy(w, self.j, self.c, k, self.peers[k]).wait_send()
            if self.whole[w]:
                continue
            self._copy(w, jx, self.c, 6, self.peers[1], quarter=0).wait_send()
            self._copy(w, jy, self.c, 7, self.peers[0], quarter=1).wait_send()
            for k, slot in enumerate(self.slots):
                self._copy(w, slot, 1 - self.c, 3 + k, self.sibling).wait_recv()
                self._copy(w, slot, self.c, 3 + k, self.sibling).wait_send()


def _allgather_weights(slotted, whole):
    n = len(slotted)

    def body(*refs):
        g = _Gather(refs[n:2 * n], refs[2 * n], refs[2 * n + 1], whole)
        g.send()
        g.relay()
        g.relay_diagonal()
        g.finish()

    any_spec = pl.BlockSpec(memory_space=pl.ANY)
    return pl.pallas_call(
        body, name="allgather_weights",
        in_specs=[any_spec] * n, out_specs=[any_spec] * n,
        out_shape=[jax.ShapeDtypeStruct(a.shape, a.dtype) for a in slotted],
        input_output_aliases={i: i for i in range(n)},
        scratch_shapes=[pltpu.SemaphoreType.DMA((n, 8)), pltpu.SemaphoreType.DMA((n, 8))],
    )(*slotted)


class _ChipExchange:
    def __init__(self, ins, outs, send_sems, recv_sems):
        self.ins, self.outs, self.send_sems, self.recv_sems = ins, outs, send_sems, recv_sems
        self.x, self.y, self.c = _place()

    def _copies(self):
        for w in range(len(self.ins)):
            for kk, (fx, fy) in enumerate(_CHIP_FLIPS):
                px, py = _flip(self.x, fx), _flip(self.y, fy)
                yield pltpu.make_async_remote_copy(
                    src_ref=self.ins[w].at[2 * px + py], dst_ref=self.outs[w].at[kk],
                    send_sem=self.send_sems.at[w, kk], recv_sem=self.recv_sems.at[w, kk],
                    device_id=(px, py, self.c), device_id_type=MESH)

    def send(self):
        for cp in self._copies():
            cp.start()

    def finish(self):
        for cp in self._copies():
            cp.wait()


class _PairExchange:
    def __init__(self, ins, outs, send_sems, recv_sems):
        self.ins, self.outs, self.send_sems, self.recv_sems = ins, outs, send_sems, recv_sems
        self.x, self.y, self.c = _place()

    def _copies(self):
        for w in range(len(self.ins)):
            half = self.ins[w].shape[1] // 2
            yield pltpu.make_async_remote_copy(
                src_ref=self.ins[w].at[:, pl.ds((1 - self.c) * half, half), :], dst_ref=self.outs[w],
                send_sem=self.send_sems.at[w], recv_sem=self.recv_sems.at[w],
                device_id=(self.x, self.y, 1 - self.c), device_id_type=MESH)

    def send(self):
        for cp in self._copies():
            cp.start()

    def finish(self):
        for cp in self._copies():
            cp.wait()


def _pair_shapes(grads):
    return [jax.ShapeDtypeStruct((g.shape[0], g.shape[1] // 2, g.shape[2]), g.dtype) for g in grads]


def _pair_exchange(grads, name):
    nw = len(grads)

    def body(*refs):
        ex = _PairExchange(refs[:nw], refs[nw:2 * nw], refs[2 * nw], refs[2 * nw + 1])
        ex.send()
        ex.finish()

    any_spec = pl.BlockSpec(memory_space=pl.ANY)
    return pl.pallas_call(
        body, name=name,
        in_specs=[any_spec] * nw, out_specs=[any_spec] * nw, out_shape=_pair_shapes(grads),
        scratch_shapes=[pltpu.SemaphoreType.DMA((nw,)), pltpu.SemaphoreType.DMA((nw,))],
    )(*grads)


def _pair_add(g, other, c_idx, *, rb, name):
    S, r, cdim = g.shape
    half = r // 2
    nb = half // rb

    def body(c_ref, g_ref, o_ref, out_ref):
        del c_ref
        out_ref[...] = (g_ref[...].astype(F32) + o_ref[...].astype(F32)).astype(BF16)

    grid_spec = pltpu.PrefetchScalarGridSpec(
        num_scalar_prefetch=1, grid=(S, nb),
        in_specs=[pl.BlockSpec((None, rb, cdim), lambda s, i, c_ref: (s, c_ref[0] * nb + i, 0)),
                  pl.BlockSpec((None, rb, cdim), lambda s, i, c_ref: (s, i, 0))],
        out_specs=pl.BlockSpec((None, rb, cdim), lambda s, i, c_ref: (s, i, 0)))
    return pl.pallas_call(
        body, name=name, grid_spec=grid_spec, out_shape=jax.ShapeDtypeStruct((S, half, cdim), BF16),
        compiler_params=_cparams(("parallel", "parallel")),
    )(c_idx, g, other)


def _chip_sum(psum, parts, cj_idx, *, rb, name):
    S, half, cdim = psum.shape
    nb = half // rb

    def body(cj_ref, own_ref, p_ref, out_ref):
        del cj_ref
        f = lambda v: v.astype(F32)
        out_ref[...] = ((f(own_ref[...]) + f(p_ref[0])) + f(p_ref[1])) + f(p_ref[2])

    grid_spec = pltpu.PrefetchScalarGridSpec(
        num_scalar_prefetch=1, grid=(nb,),
        in_specs=[pl.BlockSpec((None, rb, cdim), lambda i, cj: (cj[1], i, 0)),
                  pl.BlockSpec((3, rb, cdim), lambda i, cj: (0, i, 0))],
        out_specs=pl.BlockSpec((rb, cdim), lambda i, cj: (cj[0] * nb + i, 0)))
    return pl.pallas_call(
        body, name=name, grid_spec=grid_spec, out_shape=jax.ShapeDtypeStruct((2 * half, cdim), F32),
        compiler_params=_cparams(("parallel",)),
    )(cj_idx, psum, parts)


SLAB_W = 1024


def _final_exchange(pieces, grads):
    n, nw = len(pieces), len(grads)
    segs, at = [], 0
    for idx, p in enumerate(pieces):
        r, wd = p.shape
        for c0 in range(0, wd, SLAB_W):
            if r > 1:
                at = -(-at // 8) * 8
            segs.append((idx, c0, min(SLAB_W, wd - c0), at))
            at += r
    rows = -(-at // 8) * 8
    flips = [(fx, fy, fc) for fx in (0, 1) for fy in (0, 1) for fc in (0, 1)][1:]

    def body(*refs):
        ins = refs[:n]
        outs = refs[n + nw:2 * n + nw]
        g_refs = refs[2 * n + nw:2 * n + 2 * nw]
        mine_ref, slots_ref, send_sems, recv_sems, gsend_sems, grecv_sems = refs[2 * n + 2 * nw:]
        x, y, c = _place()
        me = 4 * x + 2 * y + c
        sibling = (x, y, 1 - c)

        def swap(w, core):
            half = g_refs[w].shape[0] // 2
            rows_ = g_refs[w].at[pl.ds(core * half, half), :]
            return pltpu.make_async_remote_copy(
                src_ref=rows_, dst_ref=rows_, send_sem=gsend_sems.at[w], recv_sem=grecv_sems.at[w],
                device_id=sibling, device_id_type=MESH)

        for w in range(nw):
            swap(w, c).start()
        mine_ref[...] = jnp.zeros_like(mine_ref)
        for idx, c0, wd, st in segs:
            r = ins[idx].shape[0]
            mine_ref[st:st + r, 0:wd] = ins[idx][:, c0:c0 + wd]
        slots_ref[me] = mine_ref[...]
        cps = []
        for kk, (fx, fy, fc) in enumerate(flips):
            cp = pltpu.make_async_remote_copy(
                src_ref=mine_ref, dst_ref=slots_ref.at[me], send_sem=send_sems.at[kk], recv_sem=recv_sems.at[kk],
                device_id=(_flip(x, fx), _flip(y, fy), _flip(c, fc)), device_id_type=MESH)
            cp.start()
            cps.append(cp)
        for cp in cps:
            cp.wait()
        tot = slots_ref[0]
        for d in range(1, 8):
            tot = tot + slots_ref[d]
        mine_ref[...] = tot
        for idx, c0, wd, st in segs:
            r = ins[idx].shape[0]
            val = mine_ref[st:st + r, 0:wd]
            if idx == n - 1:
                outs[idx][...] = jnp.sum(val, keepdims=True)
            else:
                outs[idx][:, c0:c0 + wd] = val
        for w in range(nw):
            swap(w, 1 - c).wait_recv()
            swap(w, c).wait_send()

    vm = pl.BlockSpec(memory_space=pltpu.VMEM)
    hbm = pl.BlockSpec(memory_space=pl.ANY)
    out_shape = ([jax.ShapeDtypeStruct(p.shape, F32) for p in pieces[:-1]] + [jax.ShapeDtypeStruct((1, 1), F32)]
                 + [jax.ShapeDtypeStruct(g.shape, g.dtype) for g in grads])
    res = pl.pallas_call(
        body, name="final_exchange", in_specs=[vm] * n + [hbm] * nw, out_specs=[vm] * n + [hbm] * nw,
        out_shape=out_shape, input_output_aliases={n + w: n + w for w in range(nw)},
        scratch_shapes=[pltpu.VMEM((rows, SLAB_W), F32), pltpu.VMEM((8, rows, SLAB_W), F32),
                        pltpu.SemaphoreType.DMA((7,)), pltpu.SemaphoreType.DMA((7,)),
                        pltpu.SemaphoreType.DMA((nw,)), pltpu.SemaphoreType.DMA((nw,))],
    )(*pieces, *grads)
    return res[:n], res[n:]


def _adamw(w, g, m, v, *, rb, name):
    r, cdim = w.shape

    def body(w_ref, g_ref, m_ref, v_ref, go_ref, d_ref, nm_ref, nv_ref):
        go_ref[...] = g_ref[...]
        d_ref[...], nm_ref[...], nv_ref[...] = _adamw_update(w_ref[...], g_ref[...], m_ref[...], v_ref[...])

    spec = pl.BlockSpec((rb, cdim), lambda i: (i, 0))
    shp = jax.ShapeDtypeStruct((r, cdim), F32)
    return pl.pallas_call(
        body, name=name, grid=(r // rb,), in_specs=[spec] * 4, out_specs=[spec] * 4, out_shape=[shp] * 4,
        compiler_params=_cparams(("parallel",)),
    )(w, g, m, v)


def _adamw_update(w, g, m, v):
    nm = ADAM_B1 * m + (1.0 - ADAM_B1) * g
    nv = ADAM_B2 * v + (1.0 - ADAM_B2) * (g * g)
    m_hat = nm / (1.0 - ADAM_B1 ** ADAM_STEP)
    v_hat = nv / (1.0 - ADAM_B2 ** ADAM_STEP)
    return -ADAM_LR * (m_hat / (jnp.sqrt(v_hat) + ADAM_EPS) + ADAM_WD * w), nm, nv


def _adamw_small(params):
    n = len(params)

    def body(*refs):
        ins, outs = refs[:4 * n], refs[4 * n:]
        for p in range(n):
            w_ref, g_ref, m_ref, v_ref = ins[4 * p:4 * p + 4]
            d, nm, nv = _adamw_update(w_ref[...], g_ref[...], m_ref[...], v_ref[...])
            outs[3 * p][...] = d
            outs[3 * p + 1][...] = nm
            outs[3 * p + 2][...] = nv

    vm = pl.BlockSpec(memory_space=pltpu.VMEM)
    flat = [a for p in params for a in p]
    out_shape = [jax.ShapeDtypeStruct(p[0].shape, F32) for p in params for _ in range(3)]
    res = pl.pallas_call(
        body, name="adamw_small", in_specs=[vm] * (4 * n), out_specs=[vm] * (3 * n), out_shape=out_shape,
    )(*flat)
    return [tuple(res[3 * p:3 * p + 3]) for p in range(n)]


_PAIR_ADD_ROWS = {"w_in": 256, "w_out": 128, "w_up": 256, "w_down": 176}


def _pair_sums(grads, others, names, c_idx):
    return [_pair_add(g, o_, c_idx, rb=_PAIR_ADD_ROWS[n], name=f"pair_add_{n}") for g, o_, n in zip(grads, others, names)]


def _local_step(x, tgt, meta_full, lb_param, attn_norm_w, w_in_g, hgrn_norm_w, conv_w_full, w_out_full,
                ffn_norm_w, late_slotted, fcw_full, ffn_conv_b, final_norm_w, c_idx, cj_idx):
    seq = x.shape[0]
    T = TR + seq
    head_tile = jnp.concatenate([jnp.zeros((PAD, D), F32), meta_full], axis=0)
    whn_t = jnp.tile(hgrn_norm_w, (1, NH))

    ut, projb, o, sst, mt, h1, w_up_g, w_down_g = _mix_block_fwd(
        x, head_tile, lb_param, attn_norm_w, whn_t, conv_w_full, w_in_g, w_out_full, late_slotted, (True, False))
    w_down_full = w_down_g.reshape(DFF, D)
    u2t, upb, ggt, dh2, loss_vec, dwfin = _ffn_block_fwd(
        h1, tgt, ffn_norm_w, w_up_g, fcw_full, ffn_conv_b, w_down_full, final_norm_w.reshape(1, D))

    dup, dh1, dfw, dfb, dwffn = _ffn_block_bwd(
        dh2, upb, h1, ffn_norm_w, w_up_g, fcw_full, ffn_conv_b, w_down_full)
    kb = 1408 if T % 1408 == 0 else TR
    g_up = _weight_grad(u2t, dup, bn=DFF, bk=kb, name="dw_up_mm", shard_cols=2 * DFF // NSHARD, ring=True)
    g_down, sib_up = _weight_grad(ggt, dh2, bn=D // 2, bk=2816 if T % 2816 == 0 else kb, name="dw_down_mm", ride=[g_up])
    g_down = g_down.reshape(NSHARD, DFF // NSHARD, D)
    sib_down = _pair_exchange([g_down], name="grad_pair_exchange_down")
    ps_ffn = _pair_sums([g_up, g_down], sib_up + list(sib_down), ("w_up", "w_down"), c_idx)

    dproj, dlb, dwhn, dcw, *parts_ffn = _mix_block_bwd(
        dh1, projb, o, sst, lb_param, whn_t, conv_w_full, w_out_full, ps_ffn)
    kb_deep = 2816 if T % 2816 == 0 else kb
    g_in = _weight_grad(ut, dproj, bn=9 * D // NSHARD, bk=kb_deep, name="dw_in_mm", shard_cols=9 * D // NSHARD)
    g_out, sib_in = _weight_grad(mt, dh1, bn=D, bk=kb_deep, name="dw_out_mm", ride=[g_in])
    g_out = g_out.reshape(NSHARD, D // NSHARD, D)
    sib_out = _pair_exchange([g_out], name="grad_pair_exchange_out")
    ps_mix = _pair_sums([g_in, g_out], sib_in + list(sib_out), ("w_in", "w_out"), c_idx)

    grad_x, dmeta, dwattn, *parts_mix = _input_grad_block(dproj, x, head_tile, dh1, attn_norm_w, w_in_g, ps_mix)

    halves = [_chip_sum(ps, p, cj_idx, rb=_PAIR_ADD_ROWS[n], name=f"chip_sum_{n}")
              for ps, p, n in zip(ps_mix + ps_ffn, parts_mix + parts_ffn, ("w_in", "w_out", "w_up", "w_down"))]
    small = dict(dlb=dlb, dwattn=dwattn, dwhn=dwhn, dwffn=dwffn, dfb=dfb, dwfin=dwfin,
                 dcw=dcw, dfw=dfw, dmeta=dmeta, loss=loss_vec)
    return grad_x, small, halves


_SMALL_ORDER = ("dmeta", "dcw", "dfw", "dlb", "dwattn", "dwhn", "dwffn", "dfb", "dwfin", "loss")


def kernel(x, meta_tokens, lb_param, attn_norm_w, w_in, hgrn_norm_w, conv_w, w_out, ffn_norm_w, w_up, ffn_conv_w, ffn_conv_b, w_down, final_norm_w, loss_target, m_meta_tokens, m_lb_param, m_attn_norm_w, m_w_in, m_hgrn_norm_w, m_conv_w, m_w_out, m_ffn_norm_w, m_w_up, m_ffn_conv_w, m_ffn_conv_b, m_w_down, m_final_norm_w, v_meta_tokens, v_lb_param, v_attn_norm_w, v_w_in, v_hgrn_norm_w, v_conv_w, v_w_out, v_ffn_norm_w, v_w_up, v_ffn_conv_w, v_ffn_conv_b, v_w_down, v_final_norm_w):
    xi, yi, ci = _place()
    j = 2 * xi + yi
    c_idx = jnp.reshape(ci, (1,)).astype(jnp.int32)

    j_idx = jnp.reshape(j, (1,)).astype(jnp.int32)
    ds_, fs_ = D // NSHARD, DFF // NSHARD
    widen = lambda a: jnp.pad(a, ((0, 0), (0, 768 - a.shape[1])))
    rows_small = jnp.concatenate([widen(meta_tokens), widen(conv_w[0]), widen(ffn_conv_w[0]),
                                  jnp.zeros((2, 768), F32)], axis=0)
    s_in, s_out, s_up, s_down = [
        _into_slot(w[0], j_idx, rb=rb, dtype=BF16, name=f"slot_{n}", paired=pr)
        for w, rb, n, pr in ((w_in, 256, "w_in", False), (w_out, 128, "w_out", False), (w_up, 256, "w_up", True),
                             (w_down, 176, "w_down", False))]
    s_small = _into_slot(rows_small, j_idx, rb=rows_small.shape[0], dtype=F32, name="slot_small")
    w_in_g, w_out_g, small_g = _allgather_weights([s_in, s_out, s_small], (False, False, True))
    unshard = lambda a: jnp.transpose(a, (1, 0, 2)).reshape(a.shape[1], -1)
    meta_full = unshard(small_g[:, 0:NMETA, 0:ds_])
    conv_w_full = unshard(small_g[:, NMETA:NMETA + 3, 0:ds_])
    fcw_full = unshard(small_g[:, NMETA + 3:NMETA + 6, 0:fs_])

    cj_idx = jnp.stack([ci, j]).astype(jnp.int32)
    grad_x, small, halves = _local_step(
        x[0], loss_target[0], meta_full, lb_param, attn_norm_w, w_in_g, hgrn_norm_w, conv_w_full,
        w_out_g.reshape(D, D), ffn_norm_w, [s_up, s_down], fcw_full, ffn_conv_b, final_norm_w, c_idx, cj_idx)

    names = _SMALL_ORDER
    small_sums, g_big = _final_exchange([small[n] for n in names], halves)
    vals = dict(zip(names, small_sums))
    loss = vals["loss"].reshape(())
    g_small = {
        "meta_tokens": lax.dynamic_slice_in_dim(vals["dmeta"], j * (D // NSHARD), D // NSHARD, axis=1),
        "lb_param": jnp.concatenate([vals["dlb"], -vals["dlb"]], axis=0),
        "attn_norm_w": vals["dwattn"],
        "hgrn_norm_w": vals["dwhn"],
        "conv_w": lax.dynamic_slice_in_dim(vals["dcw"], j * (D // NSHARD), D // NSHARD, axis=1)[None],
        "ffn_norm_w": vals["dwffn"],
        "ffn_conv_w": lax.dynamic_slice_in_dim(vals["dfw"], j * (DFF // NSHARD), DFF // NSHARD, axis=1)[None],
        "ffn_conv_b": vals["dfb"],
        "final_norm_w": vals["dwfin"].reshape(D),
    }


    weights = {"meta_tokens": meta_tokens, "lb_param": lb_param, "attn_norm_w": attn_norm_w, "w_in": w_in,
               "hgrn_norm_w": hgrn_norm_w, "conv_w": conv_w, "w_out": w_out, "ffn_norm_w": ffn_norm_w,
               "w_up": w_up, "ffn_conv_w": ffn_conv_w, "ffn_conv_b": ffn_conv_b, "w_down": w_down,
               "final_norm_w": final_norm_w}
    ms = {"meta_tokens": m_meta_tokens, "lb_param": m_lb_param, "attn_norm_w": m_attn_norm_w, "w_in": m_w_in,
          "hgrn_norm_w": m_hgrn_norm_w, "conv_w": m_conv_w, "w_out": m_w_out, "ffn_norm_w": m_ffn_norm_w,
          "w_up": m_w_up, "ffn_conv_w": m_ffn_conv_w, "ffn_conv_b": m_ffn_conv_b, "w_down": m_w_down,
          "final_norm_w": m_final_norm_w}
    vs = {"meta_tokens": v_meta_tokens, "lb_param": v_lb_param, "attn_norm_w": v_attn_norm_w, "w_in": v_w_in,
          "hgrn_norm_w": v_hgrn_norm_w, "conv_w": v_conv_w, "w_out": v_w_out, "ffn_norm_w": v_ffn_norm_w,
          "w_up": v_w_up, "ffn_conv_w": v_ffn_conv_w, "ffn_conv_b": v_ffn_conv_b, "w_down": v_w_down,
          "final_norm_w": v_final_norm_w}
    order = list(weights)
    grads, deltas, new_m, new_v = {}, {}, {}, {}

    for name, g, rb in zip(("w_in", "w_out", "w_up", "w_down"), g_big, (256, 128, 256, 176)):
        shp = weights[name].shape
        w2, m2, v2 = (a.reshape(shp[1], shp[2]) for a in (weights[name], ms[name], vs[name]))
        g_, d_, nm_, nv_ = _adamw(w2, g, m2, v2, rb=rb, name=f"adamw_{name}")
        grads[name], deltas[name], new_m[name], new_v[name] = (a.reshape(shp) for a in (g_, d_, nm_, nv_))

    small_names = [n for n in order if n not in grads]
    as2d = lambda a: a.reshape(-1, a.shape[-1])
    res = _adamw_small([tuple(as2d(a) for a in (weights[n], g_small[n], ms[n], vs[n])) for n in small_names])
    for n, (d_, nm_, nv_) in zip(small_names, res):
        shp = weights[n].shape
        grads[n], deltas[n], new_m[n], new_v[n] = (a.reshape(shp) for a in (g_small[n], d_, nm_, nv_))

    return (loss, grad_x[None], *[grads[n] for n in order], *[deltas[n] for n in order],
            *[new_m[n] for n in order], *[new_v[n] for n in order])
```

```python
import jax
import jax.numpy as jnp
from jax import lax
from jax.experimental import pallas as pl
from jax.experimental.pallas import tpu as pltpu

F32 = jnp.float32
BF16 = jnp.bfloat16
MESH = pl.DeviceIdType.MESH

D = 1024
NH = 8
HD = 128
DFF = 2816
NMETA = 16
EPS = 1e-6
TR = 256
PAD = TR - NMETA
CH = 64
NSHARD = 4
VMEM_LIMIT = 62 * 1024 * 1024

ADAM_LR = 0.001
ADAM_B1 = 0.9
ADAM_B2 = 0.999
ADAM_EPS = 1e-08
ADAM_WD = 0.01
ADAM_STEP = 10


def _cparams(semantics=None, **kw):
    return pltpu.CompilerParams(dimension_semantics=semantics, vmem_limit_bytes=VMEM_LIMIT, **kw)


def _sigmoid(x):
    return 0.5 * jnp.tanh(0.5 * x) + 0.5


def _weight_grad(at, b, *, bn, bk, name, shard_cols=None, ride=()):
    M, T = at.shape
    N = b.shape[1]
    assert b.shape[0] == T and T % bk == 0 and N % bn == 0, (name, M, N, T, bn, bk)
    nk = T // bk
    nj = N // bn
    nex = len(ride)
    if shard_cols is None:
        nsh = 1
        blk = (M, bn)
        out_shape = jax.ShapeDtypeStruct((M, N), BF16)
        out_spec = pl.BlockSpec(blk, lambda j, k: (0, j))
    else:
        assert bn % shard_cols == 0 and N % shard_cols == 0
        nsh = bn // shard_cols
        blk = (nsh, M, shard_cols)
        out_shape = jax.ShapeDtypeStruct((N // shard_cols, M, shard_cols), BF16)
        out_spec = pl.BlockSpec(blk, lambda j, k: (j, 0, 0))

    def body(*refs):
        a_ref, b_ref = refs[:2]
        o_ref = refs[2 + nex]
        acc_ref = refs[3 + 2 * nex]
        j, k = pl.program_id(0), pl.program_id(1)
        if nex:
            exchange = _PairExchange(refs[2:2 + nex], refs[3 + nex:3 + 2 * nex], refs[-2], refs[-1])

            @pl.when(jnp.logical_and(j == 0, k == 0))
            def _():
                exchange.send()

        @pl.when(k == 0)
        def _():
            acc_ref[...] = jnp.zeros_like(acc_ref)

        p = jnp.dot(a_ref[...].astype(BF16), b_ref[...].astype(BF16), preferred_element_type=F32)
        if shard_cols is None:
            acc_ref[...] += p
        else:
            for q in range(nsh):
                acc_ref[q] += p[:, q * shard_cols:(q + 1) * shard_cols]

        @pl.when(k == nk - 1)
        def _():
            o_ref[...] = acc_ref[...].astype(BF16)

        if nex:
            @pl.when(jnp.logical_and(j == nj - 1, k == nk - 1))
            def _():
                exchange.finish()

    hbm = pl.BlockSpec(memory_space=pl.ANY)
    scratch = [pltpu.VMEM(blk, F32)]
    if nex:
        scratch += [pltpu.SemaphoreType.DMA((nex,)), pltpu.SemaphoreType.DMA((nex,))]
    res = pl.pallas_call(
        body, name=name, grid=(nj, nk),
        in_specs=[pl.BlockSpec((M, bk), lambda j, k: (0, k)), pl.BlockSpec((bk, bn), lambda j, k: (k, j))] + [hbm] * nex,
        out_specs=[out_spec] + [hbm] * nex, out_shape=[out_shape] + _pair_shapes(ride), scratch_shapes=scratch,
        compiler_params=_cparams(("arbitrary", "arbitrary") if nex else ("parallel", "arbitrary")),
    )(at, b, *ride)
    return (res[0], list(res[1:])) if nex else res[0]


def _input_grad_block(dproj, x_seq, head_tile, dres, w, w_in_g, psums):
    T = TR + x_seq.shape[0]
    nt = T // TR
    nsl = w_in_g.shape[0]
    wsl = w_in_g.shape[2]
    nex = len(psums)

    def body(*refs):
        dp_ref, x_ref, head_ref, dres_ref, w_ref, win_ref = refs[:6]
        gx_ref, dmeta_ref, dw_ref = refs[6 + nex:9 + nex]
        exchange = _ChipExchange(refs[6:6 + nex], refs[9 + nex:9 + 2 * nex], refs[-2], refs[-1])
        i = pl.program_id(0)

        @pl.when(i == 0)
        def _():
            exchange.send()

        @pl.when(i == nt - 1)
        def _():
            exchange.finish()

        x = jnp.where(i == 0, head_ref[...], x_ref[...])
        r = lax.rsqrt(jnp.mean(x * x, axis=-1, keepdims=True) + EPS)
        n = x * r
        du_v = None
        for s in range(nsl):
            part = _dot_nt(dp_ref[:, s * wsl:(s + 1) * wsl], win_ref[s])
            du_v = part if du_v is None else du_v + part
        dn = du_v * w_ref[...]
        dh = dres_ref[...] + r * (dn - n * jnp.mean(dn * n, axis=-1, keepdims=True))
        gx_ref[...] = dh
        part = jnp.sum(du_v * n, axis=0, keepdims=True)

        @pl.when(i == 0)
        def _():
            dmeta_ref[...] = dh[PAD:TR, :]
            dw_ref[...] = part

        @pl.when(i > 0)
        def _():
            dw_ref[...] += part

    row = pl.BlockSpec((TR, D), lambda i: (i, 0))
    vec = pl.BlockSpec((1, D), lambda i: (0, 0))
    hbm = pl.BlockSpec(memory_space=pl.ANY)
    return pl.pallas_call(
        body, name="input_grad_block", grid=(nt,),
        in_specs=[pl.BlockSpec((TR, 9 * D), lambda i: (i, 0)),
                  pl.BlockSpec((TR, D), lambda i: (jnp.maximum(i - 1, 0), 0)), pl.BlockSpec((TR, D), lambda i: (0, 0)),
                  row, vec, pl.BlockSpec(memory_space=pltpu.VMEM)] + [hbm] * nex,
        out_specs=[pl.BlockSpec((TR, D), lambda i: (jnp.maximum(i - 1, 0), 0)),
                   pl.BlockSpec((NMETA, D), lambda i: (0, 0)), vec] + [hbm] * nex,
        out_shape=[jax.ShapeDtypeStruct((T - TR, D), F32), jax.ShapeDtypeStruct((NMETA, D), F32),
                   jax.ShapeDtypeStruct((1, D), F32)]
        + [jax.ShapeDtypeStruct((3,) + p.shape[1:], p.dtype) for p in psums],
        scratch_shapes=[pltpu.SemaphoreType.DMA((nex, 3)), pltpu.SemaphoreType.DMA((nex, 3))],
        compiler_params=_cparams(("arbitrary",)),
    )(dproj, x_seq, head_tile, dres, w, w_in_g, *psums)


def _tri_matmul(tri_bf16, x):
    hi = x.astype(BF16)
    lo = (x - hi.astype(F32)).astype(BF16)
    return jnp.dot(tri_bf16, lo, preferred_element_type=F32) + jnp.dot(tri_bf16, hi, preferred_element_type=F32)


def _shift_down(x, prev8, n):
    rows = x.shape[0]
    return pltpu.roll(jnp.concatenate([prev8, x], axis=0), n, 0)[8:8 + rows, :]


def _shift_up(x, next8, n):
    rows = x.shape[0]
    return pltpu.roll(jnp.concatenate([x, next8], axis=0), rows + 8 - n, 0)[0:rows, :]


def _gates(f_raw, lb):
    sg = _sigmoid(f_raw)
    f = lb + (1.0 - lb) * sg
    return sg, f, jnp.log(f), 1.0 - f


def _lower_bound(lbp_ref):
    return _sigmoid(lbp_ref[0:1, :] - lbp_ref[1:2, :])


def _tri_masks():
    r = lax.broadcasted_iota(jnp.int32, (CH, CH), 0)
    c = lax.broadcasted_iota(jnp.int32, (CH, CH), 1)
    return r >= c, r <= c


def _ones_where(mask):
    return jnp.where(mask, 1.0, 0.0).astype(BF16)


def _dot(a, b):
    return jnp.dot(a.astype(BF16), b.astype(BF16), preferred_element_type=F32)


def _dot_nt(a, b):
    return lax.dot_general(a.astype(BF16), b.astype(BF16), (((1,), (1,)), ((), ())), preferred_element_type=F32)


def _dot_tn(a, b):
    return lax.dot_general(a.astype(BF16), b.astype(BF16), (((0,), (0,)), ((), ())), preferred_element_type=F32)


def _mix_block_fwd(x_seq, head_tile, lb_param, wattn, whn, conv_w, w_in_g, w_out, late, late_paired):
    T = TR + x_seq.shape[0]
    nt = T // TR
    ncht = TR // CH
    nsl = w_in_g.shape[0]
    wsl = w_in_g.shape[2]
    nlate = len(late)

    def body(*refs):
        x_ref, head_ref, lbp_ref, wa_ref, whn_ref, cw_ref, win_ref, wout_ref = refs[:8]
        ut_ref, projb_ref, o_ref, sst_ref, mt_ref, h1_ref = refs[8 + nlate:14 + nlate]
        late_refs = refs[14 + nlate:14 + 2 * nlate]
        proj_ref, m_ref, st_ref, cxc_ref, send_sems, recv_sems = refs[14 + 2 * nlate:]
        i = pl.program_id(0)
        gather = _Gather(late_refs, send_sems, recv_sems, (False,) * nlate, late_paired)

        @pl.when(i == 0)
        def _():
            st_ref[...] = jnp.zeros_like(st_ref)
            cxc_ref[...] = jnp.zeros_like(cxc_ref)
            gather.send()

        @pl.when(i == nt // 2)
        def _():
            gather.relay()

        @pl.when(i == (3 * nt) // 4)
        def _():
            gather.relay_diagonal()

        x = jnp.where(i == 0, head_ref[...], x_ref[...])
        r1 = lax.rsqrt(jnp.mean(x * x, axis=-1, keepdims=True) + EPS)
        u_f = x * r1 * wa_ref[...]
        u = u_f.astype(BF16)
        ut_ref[...] = u_f.T.astype(BF16)
        for s in range(nsl):
            cols = slice(s * wsl, (s + 1) * wsl)
            p = jnp.dot(u, win_ref[s], preferred_element_type=F32)
            proj_ref[:, cols] = p
            projb_ref[:, cols] = p.astype(BF16)

        lb = _lower_bound(lbp_ref)
        lower, _ = _tri_masks()
        ltri = _ones_where(lower)
        whn_v = whn_ref[...]
        w0, w1, w2 = cw_ref[0:1, :], cw_ref[1:2, :], cw_ref[2:3, :]

        def gates_and_conv(c):
            rows = slice(c * CH, (c + 1) * CH)
            q_raw = proj_ref[rows, 0:D]
            f_raw = proj_ref[rows, D:2 * D]
            q = q_raw * _sigmoid(q_raw)
            _, _, g, k = _gates(f_raw, lb)
            gam = _tri_matmul(ltri, g)
            gam_l = gam[CH - 1:CH, :]
            g_out = proj_ref[rows, 3 * D:4 * D]
            cx = proj_ref[rows, 5 * D:6 * D] * proj_ref[rows, 6 * D:7 * D]
            prev8 = cxc_ref[...]
            cv = w0 * _shift_down(cx, prev8, 2) + w1 * _shift_down(cx, prev8, 1) + w2 * cx
            cxc_ref[...] = cx[CH - 8:CH, :]
            y_b = proj_ref[rows, 4 * D:5 * D] * cv
            return dict(
                e_l=jnp.exp(gam_l), qt=(q * jnp.exp(gam)).astype(BF16), kt=(k * jnp.exp(-gam)).astype(BF16),
                khat=(k * jnp.exp(gam_l - gam)).astype(BF16), vb=proj_ref[rows, 2 * D:3 * D].astype(BF16),
                out_gate=whn_v * (g_out * _sigmoid(g_out)), merge_a=_sigmoid(proj_ref[rows, 7 * D:8 * D]),
                merged_b=_sigmoid(proj_ref[rows, 8 * D:9 * D]) * y_b)

        def recurrence_and_merge(c, pre):
            rows = slice(c * CH, (c + 1) * CH)
            qt, kt, khat, vb, e_l = pre["qt"], pre["kt"], pre["khat"], pre["vb"], pre["e_l"]
            heads = [slice(h * HD, (h + 1) * HD) for h in range(NH)]
            sts = [st_ref[h] for h in range(NH)]
            scores = [_dot_nt(qt[:, cs], kt[:, cs]) for cs in heads]
            o_state = [_dot_nt(qt[:, cs], st) for cs, st in zip(heads, sts)]
            st_new = [_dot_tn(vb[:, cs], khat[:, cs]) for cs in heads]
            on_parts = []
            for h, cs in enumerate(heads):
                sst_ref[c, h] = sts[h]
                st_ref[h] = sts[h] * e_l[:, cs] + st_new[h]
                o_h = o_state[h] + _dot(jnp.where(lower, scores[h], 0.0), vb[:, cs])
                o_ref[rows, cs] = o_h
                ro = lax.rsqrt(jnp.mean(o_h * o_h, axis=-1, keepdims=True) + EPS)
                on_parts.append(o_h * ro)
            on = jnp.concatenate(on_parts, axis=1)
            m_ref[rows, :] = pre["merge_a"] * (on * pre["out_gate"]) + pre["merged_b"]

        pre = {0: gates_and_conv(0)}
        for c in range(ncht):
            if c + 1 < ncht:
                pre[c + 1] = gates_and_conv(c + 1)
            recurrence_and_merge(c, pre.pop(c))
        m_v = m_ref[...]
        h1_ref[...] = x + jnp.dot(m_v.astype(BF16), wout_ref[...], preferred_element_type=F32)
        mt_ref[...] = m_v.T.astype(BF16)

        @pl.when(i == nt - 1)
        def _():
            gather.finish()

    row = lambda w: pl.BlockSpec((TR, w), lambda i: (i, 0))
    col = lambda w: pl.BlockSpec((w, TR), lambda i: (0, i))
    vec = lambda r: pl.BlockSpec((r, D), lambda i: (0, 0))
    vm = pl.BlockSpec(memory_space=pltpu.VMEM)
    hbm = pl.BlockSpec(memory_space=pl.ANY)
    return pl.pallas_call(
        body, name="mix_block_fwd", grid=(nt,),
        in_specs=[pl.BlockSpec((TR, D), lambda i: (jnp.maximum(i - 1, 0), 0)), pl.BlockSpec((TR, D), lambda i: (0, 0)),
                  vec(2), vec(1), vec(1), vec(3), vm, vm] + [hbm] * nlate,
        out_specs=[col(D), row(9 * D), row(D), pl.BlockSpec((ncht, NH, HD, HD), lambda i: (i, 0, 0, 0)),
                   col(D), row(D)] + [hbm] * nlate,
        out_shape=[jax.ShapeDtypeStruct((D, T), BF16), jax.ShapeDtypeStruct((T, 9 * D), BF16),
                   jax.ShapeDtypeStruct((T, D), F32), jax.ShapeDtypeStruct((T // CH, NH, HD, HD), F32),
                   jax.ShapeDtypeStruct((D, T), BF16), jax.ShapeDtypeStruct((T, D), F32)]
        + [jax.ShapeDtypeStruct(a.shape, a.dtype) for a in late],
        input_output_aliases={8 + n: 6 + n for n in range(nlate)},
        scratch_shapes=[pltpu.VMEM((TR, 9 * D), F32), pltpu.VMEM((TR, D), F32), pltpu.VMEM((NH, HD, HD), F32),
                        pltpu.VMEM((8, D), F32), pltpu.SemaphoreType.DMA((nlate, 8)),
                        pltpu.SemaphoreType.DMA((nlate, 8))],
        compiler_params=_cparams(("arbitrary",)),
    )(x_seq, head_tile, lb_param, wattn, whn, conv_w, w_in_g, w_out, *late)


def _mix_block_bwd(dh1, projb, o, sst, lb_param, whn, conv_w, w_out, psums):
    T = projb.shape[0]
    nt = T // TR
    ncht = TR // CH
    tb16 = TR // 16
    nex = len(psums)

    def body(*refs):
        dh1_ref, proj_ref, pc_ref, px_ref, o_ref, sst_ref, lbp_ref, whn_ref, cw_ref, wout_ref = refs[:10]
        dproj_ref, dlb_ref, dwhn_ref, dcw_ref = refs[10 + nex:14 + nex]
        exchange = _ChipExchange(refs[10:10 + nex], refs[14 + nex:14 + 2 * nex], refs[-2], refs[-1])
        dm_ref, dst_ref, dcvc_ref, acc_lb, acc_hn, acc_cw = refs[14 + 2 * nex:-2]
        s = pl.program_id(0)
        tile = nt - 1 - s

        @pl.when(s == 0)
        def _():
            dst_ref[...] = jnp.zeros_like(dst_ref)
            dcvc_ref[...] = jnp.zeros_like(dcvc_ref)
            acc_lb[...] = jnp.zeros_like(acc_lb)
            acc_hn[...] = jnp.zeros_like(acc_hn)
            acc_cw[...] = jnp.zeros_like(acc_cw)
            exchange.send()

        dm_ref[...] = _dot_nt(dh1_ref[...], wout_ref[...])
        lb = _lower_bound(lbp_ref)
        lower, upper = _tri_masks()
        ltri = _ones_where(lower)
        utri = _ones_where(upper)
        whn_v = whn_ref[...]
        w0, w1, w2 = cw_ref[0:1, :], cw_ref[1:2, :], cw_ref[2:3, :]
        cx_before_tile = jnp.where(tile > 0, (pc_ref[...].astype(F32) * px_ref[...].astype(F32))[8:16, :], 0.0)
        rid = lax.broadcasted_iota(jnp.int32, (CH, D), 0)

        def chunk(cc, carry):
            c = ncht - 1 - cc
            r0 = pl.multiple_of(c * CH, CH)
            rows = pl.ds(r0, CH)
            slab = lambda n: proj_ref[rows, n * D:(n + 1) * D].astype(F32)
            q_raw, f_raw, v, g_out, b_gate, c_gate, x_conv = (slab(n) for n in range(7))
            sa = _sigmoid(slab(7))
            sb = _sigmoid(slab(8))
            dm_v = dm_ref[rows, :]

            sq = _sigmoid(q_raw)
            q = q_raw * sq
            sg, f, g, k = _gates(f_raw, lb)
            gam = _tri_matmul(ltri, g)
            gam_l = gam[CH - 1:CH, :]
            e_l = jnp.exp(gam_l)
            e_g = jnp.exp(gam)
            e_ng = jnp.exp(-gam)
            e_kl = jnp.exp(gam_l - gam)
            qt = q * e_g
            kt = k * e_ng
            khat = k * e_kl
            qt_b, kt_b, khat_b, v_b = qt.astype(BF16), kt.astype(BF16), khat.astype(BF16), v.astype(BF16)
            qt_seen, kt_seen = qt_b.astype(F32), kt_b.astype(F32)
            s_go = _sigmoid(g_out)
            silu_go = g_out * s_go
            cx = c_gate * x_conv
            rprev = pl.ds(pl.multiple_of(jnp.maximum(r0 - 16, 0), 16), 16)
            cx_prev_in = (proj_ref[rprev, 5 * D:6 * D].astype(F32) * proj_ref[rprev, 6 * D:7 * D].astype(F32))[8:16, :]
            prev8 = jnp.where(c > 0, cx_prev_in, cx_before_tile)
            cx_m1 = _shift_down(cx, prev8, 1)
            cx_m2 = _shift_down(cx, prev8, 2)
            cv = w0 * cx_m2 + w1 * cx_m1 + w2 * cx
            y_b = b_gate * cv

            o_v = o_ref[rows, :]
            ro_parts, on_parts = [], []
            for h in range(NH):
                cs = slice(h * HD, (h + 1) * HD)
                o_h = o_v[:, cs]
                ro = lax.rsqrt(jnp.mean(o_h * o_h, axis=-1, keepdims=True) + EPS)
                ro_parts.append(ro)
                on_parts.append(o_h * ro)
            on = jnp.concatenate(on_parts, axis=1)
            out_gate = whn_v * silu_go
            y_a = on * out_gate

            dy_a = dm_v * sa
            dy_b = dm_v * sb
            dproj_ref[rows, 7 * D:8 * D] = (dy_a * y_a * (1.0 - sa)).astype(BF16)
            dproj_ref[rows, 8 * D:9 * D] = (dy_b * y_b * (1.0 - sb)).astype(BF16)
            dproj_ref[rows, 4 * D:5 * D] = (dy_b * cv).astype(BF16)
            dcv = dy_b * b_gate
            acc_cw[0:1, :] += jnp.sum(dcv * cx_m2, axis=0, keepdims=True)
            acc_cw[1:2, :] += jnp.sum(dcv * cx_m1, axis=0, keepdims=True)
            acc_cw[2:3, :] += jnp.sum(dcv * cx, axis=0, keepdims=True)
            next8 = dcvc_ref[...]
            dcx = w2 * dcv + w1 * _shift_up(dcv, next8, 1) + w0 * _shift_up(dcv, next8, 2)
            dcvc_ref[...] = dcv[0:8, :]
            dproj_ref[rows, 5 * D:6 * D] = (dcx * x_conv).astype(BF16)
            dproj_ref[rows, 6 * D:7 * D] = (dcx * c_gate).astype(BF16)
            don = dy_a * out_gate
            dya_on = dy_a * on
            dproj_ref[rows, 3 * D:4 * D] = (dya_on * (whn_v * (s_go * (1.0 + g_out * (1.0 - s_go))))).astype(BF16)
            acc_hn[...] += jnp.sum(dya_on * silu_go, axis=0, keepdims=True)

            heads = [slice(h * HD, (h + 1) * HD) for h in range(NH)]
            do_bs, sts, dstns, first = [], [], [], []
            for h, cs in enumerate(heads):
                on_h = on_parts[h]
                don_h = don[:, cs]
                do_h = ro_parts[h] * (don_h - on_h * jnp.mean(don_h * on_h, axis=-1, keepdims=True))
                do_bs.append(do_h.astype(BF16))
                sts.append(sst_ref[c, h])
                dstns.append(dst_ref[h])
            for h, cs in enumerate(heads):
                qt_h, kt_h, khat_h, v_h = qt_b[:, cs], kt_b[:, cs], khat_b[:, cs], v_b[:, cs]
                do_b, dstn_b = do_bs[h], dstns[h].astype(BF16)
                first.append(dict(
                    a_t=_dot_nt(kt_h, qt_h), da=_dot_nt(do_b, v_h), da_t=_dot_nt(v_h, do_b),
                    dv_state=_dot_nt(khat_h, dstn_b), dqt_state=_dot(do_b, sts[h]), dkhat=_dot(v_h, dstn_b),
                    dst_chunk=_dot_tn(do_b, qt_h)))
            dq_parts, dk_parts, dv_parts, dgam_parts, ext_parts = [], [], [], [], []
            for h, cs in enumerate(heads):
                qt_h, kt_h = qt_b[:, cs], kt_b[:, cs]
                p1, do_b, st, dstn = first[h], do_bs[h], sts[h], dstns[h]
                dv_h = _dot(jnp.where(upper, p1["a_t"], 0.0), do_b) + p1["dv_state"]
                dqt_state = p1["dqt_state"]
                dqt_chunk = _dot(jnp.where(lower, p1["da"], 0.0), kt_h)
                dkt = _dot(jnp.where(upper, p1["da_t"], 0.0), qt_h)
                dkhat = p1["dkhat"]
                dq_h = (dqt_state + dqt_chunk) * e_g[:, cs]
                dk_h = dkt * e_ng[:, cs] + dkhat * e_kl[:, cs]
                khat_dkhat = dkhat * khat[:, cs]
                ext = (jnp.sum(khat_dkhat, axis=0, keepdims=True)
                       + e_l[:, cs] * jnp.sum(st * dstn, axis=0, keepdims=True))
                dst_ref[h] = p1["dst_chunk"] + dstn * e_l[:, cs]
                dq_parts.append(dq_h)
                dk_parts.append(dk_h)
                dv_parts.append(dv_h)
                dgam_parts.append(qt[:, cs] * dqt_state + qt_seen[:, cs] * dqt_chunk - kt_seen[:, cs] * dkt
                                  - khat_dkhat)
                ext_parts.append(ext)
            dq = jnp.concatenate(dq_parts, axis=1)
            dk = jnp.concatenate(dk_parts, axis=1)
            dgam = jnp.concatenate(dgam_parts, axis=1)
            ext = jnp.concatenate(ext_parts, axis=1)
            dgam = dgam + jnp.where(rid == CH - 1, ext, 0.0)
            dg = _tri_matmul(utri, dgam)
            dproj_ref[rows, 0:D] = (dq * (sq * (1.0 + q_raw * (1.0 - sq)))).astype(BF16)
            df = dg * jnp.exp(-g) - dk
            dproj_ref[rows, D:2 * D] = (df * (1.0 - lb) * sg * (1.0 - sg)).astype(BF16)
            dproj_ref[rows, 2 * D:3 * D] = jnp.concatenate(dv_parts, axis=1).astype(BF16)
            real = (tile * TR + r0 + rid) >= PAD
            acc_lb[...] += jnp.sum(jnp.where(real, df * (1.0 - sg), 0.0), axis=0, keepdims=True)
            return carry

        lax.fori_loop(0, ncht, chunk, 0)

        @pl.when(s == nt - 1)
        def _():
            dlb_ref[...] = acc_lb[...] * lb * (1.0 - lb)
            hn = acc_hn[...]
            tot = hn[:, 0:HD]
            for h in range(1, NH):
                tot = tot + hn[:, h * HD:(h + 1) * HD]
            dwhn_ref[...] = tot
            dcw_ref[...] = acc_cw[0:3, :]
            exchange.finish()

    hbm = pl.BlockSpec(memory_space=pl.ANY)
    rev = lambda s: (nt - 1 - s, 0)
    prevc = lambda s: (jnp.maximum((nt - 1 - s) * tb16 - 1, 0), 5)
    prevx = lambda s: (jnp.maximum((nt - 1 - s) * tb16 - 1, 0), 6)
    const = lambda s: (0, 0)
    return pl.pallas_call(
        body, name="mix_block_bwd", grid=(nt,),
        in_specs=[pl.BlockSpec((TR, D), rev),
                  pl.BlockSpec((TR, 9 * D), rev),
                  pl.BlockSpec((16, D), prevc),
                  pl.BlockSpec((16, D), prevx),
                  pl.BlockSpec((TR, D), rev),
                  pl.BlockSpec((ncht, NH, HD, HD), lambda s: (nt - 1 - s, 0, 0, 0)),
                  pl.BlockSpec((2, D), const),
                  pl.BlockSpec((1, D), const),
                  pl.BlockSpec((3, D), const),
                  pl.BlockSpec(memory_space=pltpu.VMEM)] + [hbm] * nex,
        out_specs=[pl.BlockSpec((TR, 9 * D), rev),
                   pl.BlockSpec((1, D), const),
                   pl.BlockSpec((1, HD), const),
                   pl.BlockSpec((3, D), const)] + [hbm] * nex,
        out_shape=[jax.ShapeDtypeStruct((T, 9 * D), BF16), jax.ShapeDtypeStruct((1, D), F32),
                   jax.ShapeDtypeStruct((1, HD), F32), jax.ShapeDtypeStruct((3, D), F32)]
        + [jax.ShapeDtypeStruct((3,) + p.shape[1:], p.dtype) for p in psums],
        scratch_shapes=[pltpu.VMEM((TR, D), F32), pltpu.VMEM((NH, HD, HD), F32), pltpu.VMEM((8, D), F32),
                        pltpu.VMEM((1, D), F32), pltpu.VMEM((1, D), F32), pltpu.VMEM((8, D), F32),
                        pltpu.SemaphoreType.DMA((nex, 3)), pltpu.SemaphoreType.DMA((nex, 3))],
        compiler_params=_cparams(("arbitrary",)),
    )(dh1, projb, projb, projb, o, sst, lb_param, whn, conv_w, w_out, *psums)


def _ffn_block_fwd(h1, tgt, wffn, w_up_g, fcw, fcb, w_down, wfin):
    T = h1.shape[0]
    nt = T // TR
    nsl = w_up_g.shape[0]
    wsl = w_up_g.shape[2]

    def body(h_ref, t_ref, wn_ref, wup_ref, cw_ref, cb_ref, wdn_ref, wf_ref,
             u2t_ref, upb_ref, ggt_ref, dh_ref, loss_ref, dwf_ref, up_scr, gg_ref, carry_ref):
        i = pl.program_id(0)

        @pl.when(i == 0)
        def _():
            carry_ref[...] = jnp.zeros_like(carry_ref)
            loss_ref[...] = jnp.zeros_like(loss_ref)
            dwf_ref[...] = jnp.zeros_like(dwf_ref)

        x = h_ref[...]
        r2 = lax.rsqrt(jnp.mean(x * x, axis=-1, keepdims=True) + EPS)
        u2_f = x * r2 * wn_ref[...]
        u2 = u2_f.astype(BF16)
        u2t_ref[...] = u2_f.T.astype(BF16)
        for s in range(nsl):
            up_s = jnp.dot(u2, wup_ref[s], preferred_element_type=F32)
            up_scr[:, s * wsl:(s + 1) * wsl] = up_s
            upb_ref[:, s * wsl:(s + 1) * wsl] = up_s.astype(BF16)
        w0, w1, w2 = cw_ref[0:1, :], cw_ref[1:2, :], cw_ref[2:3, :]

        def chunk(c, carry):
            rows = pl.ds(pl.multiple_of(c * CH, CH), CH)
            a_pre = up_scr[rows, 0:DFF]
            val = up_scr[rows, DFF:2 * DFF]
            prev8 = carry_ref[...]
            a = w0 * _shift_down(a_pre, prev8, 2) + w1 * _shift_down(a_pre, prev8, 1) + w2 * a_pre + cb_ref[...]
            carry_ref[...] = a_pre[CH - 8:CH, :]
            gg_ref[rows, :] = a * _sigmoid(a) * val
            return carry

        lax.fori_loop(0, TR // CH, chunk, 0, unroll=True)
        gg_v = gg_ref[...]
        ggt_ref[...] = gg_v.T.astype(BF16)
        h2 = x + jnp.dot(gg_v.astype(BF16), wdn_ref[...], preferred_element_type=F32)
        r3 = lax.rsqrt(jnp.mean(h2 * h2, axis=-1, keepdims=True) + EPS)
        n3 = h2 * r3
        wf = wf_ref[...]
        diff = jnp.where(i > 0, n3 * wf - t_ref[...], 0.0)
        loss_ref[...] += jnp.sum(diff * diff, axis=0, keepdims=True) * (0.5 / D)
        dy = diff * (1.0 / D)
        dwf_ref[...] += jnp.sum(dy * n3, axis=0, keepdims=True)
        dn = dy * wf
        dh_ref[...] = r3 * (dn - n3 * jnp.mean(dn * n3, axis=-1, keepdims=True))

    row = lambda w: pl.BlockSpec((TR, w), lambda i: (i, 0))
    col = lambda w: pl.BlockSpec((w, TR), lambda i: (0, i))
    vec = lambda w, r=1: pl.BlockSpec((r, w), lambda i: (0, 0))
    vm = pl.BlockSpec(memory_space=pltpu.VMEM)
    return pl.pallas_call(
        body, name="ffn_block_fwd", grid=(nt,),
        in_specs=[row(D), pl.BlockSpec((TR, D), lambda i: (jnp.maximum(i - 1, 0), 0)), vec(D), vm,
                  vec(DFF, 3), vec(DFF), vm, vec(D)],
        out_specs=[col(D), row(2 * DFF), col(DFF), row(D), vec(D), vec(D)],
        out_shape=[jax.ShapeDtypeStruct((D, T), BF16), jax.ShapeDtypeStruct((T, 2 * DFF), BF16),
                   jax.ShapeDtypeStruct((DFF, T), BF16), jax.ShapeDtypeStruct((T, D), F32),
                   jax.ShapeDtypeStruct((1, D), F32), jax.ShapeDtypeStruct((1, D), F32)],
        scratch_shapes=[pltpu.VMEM((TR, 2 * DFF), F32), pltpu.VMEM((TR, DFF), F32), pltpu.VMEM((8, DFF), F32)],
        compiler_params=_cparams(("arbitrary",)),
    )(h1, tgt, wffn, w_up_g, fcw, fcb, w_down, wfin)


def _ffn_block_bwd(dh2, upb, h1, wffn, w_up_g, fcw, fcb, w_down):
    T = h1.shape[0]
    nt = T // TR
    ncht = TR // CH
    nsl = w_up_g.shape[0]
    wsl = w_up_g.shape[2]
    tb16 = TR // 16
    assert ncht % nsl == 0
    every = ncht // nsl
    step = -(-DFF // (ncht * 128)) * 128
    parts = [(c0, min(c0 + step, DFF)) for c0 in range(0, DFF, step)]
    assert len(parts) == ncht

    def body(dh2n_ref, dh2p_ref, up_ref, pa_ref, h_ref, wn_ref, wup_ref, cw_ref, cb_ref, wdn_ref,
             dup_ref, dh1_ref, dfw_ref, dfb_ref, dwn_ref,
             dgg_ring, dup_ring, carry_ref, acc_w, acc_b, acc_n):
        s = pl.program_id(0)
        slot = lax.rem(s, 2)
        other = 1 - slot

        @pl.when(s == 0)
        def _():
            carry_ref[...] = jnp.zeros_like(carry_ref)
            acc_w[...] = jnp.zeros_like(acc_w)
            acc_b[...] = jnp.zeros_like(acc_b)
            acc_n[...] = jnp.zeros_like(acc_n)
            dup_ring[1] = jnp.zeros((TR, 2 * DFF), BF16)
            dgg_ring[0] = _dot_nt(dh2p_ref[...], wdn_ref[...])

        def norm_bwd(du2, valid):
            x = h_ref[...]
            dh2p = dh2p_ref[...]
            r2 = lax.rsqrt(jnp.mean(x * x, axis=-1, keepdims=True) + EPS)
            n2 = x * r2
            dn = du2 * wn_ref[...]
            dh1_ref[...] = dh2p + r2 * (dn - n2 * jnp.mean(dn * n2, axis=-1, keepdims=True))
            acc_n[...] += jnp.where(valid, jnp.sum(du2 * n2, axis=0, keepdims=True), 0.0)

        @pl.when(s < nt)
        def _():
            tile = nt - 1 - s
            w0, w1, w2 = cw_ref[0:1, :], cw_ref[1:2, :], cw_ref[2:3, :]
            a_before_tile = jnp.where(tile > 0, pa_ref[...].astype(F32)[8:16, :], 0.0)
            dh2n = dh2n_ref[...].astype(BF16)
            du2 = None
            for idx in range(ncht):
                c = ncht - 1 - idx
                r0 = c * CH
                rows = slice(r0, r0 + CH)
                a_pre = up_ref[rows, 0:DFF].astype(F32)
                val = up_ref[rows, DFF:2 * DFF].astype(F32)
                prev8 = up_ref[r0 - 16:r0, 0:DFF].astype(F32)[8:16, :] if c > 0 else a_before_tile
                a_m1 = _shift_down(a_pre, prev8, 1)
                a_m2 = _shift_down(a_pre, prev8, 2)
                a = w0 * a_m2 + w1 * a_m1 + w2 * a_pre + cb_ref[...]
                sig = _sigmoid(a)
                dgg_v = dgg_ring[slot, rows, :]
                da = dgg_v * val * (sig * (1.0 + a * (1.0 - sig)))
                dval = (dgg_v * (a * sig)).astype(BF16)
                next8 = carry_ref[...]
                da_pre = (w2 * da + w1 * _shift_up(da, next8, 1) + w0 * _shift_up(da, next8, 2)).astype(BF16)
                carry_ref[...] = da[0:8, :]
                dup_ref[rows, 0:DFF] = da_pre
                dup_ref[rows, DFF:2 * DFF] = dval
                dup_ring[slot, rows, 0:DFF] = da_pre
                dup_ring[slot, rows, DFF:2 * DFF] = dval
                acc_w[0:1, :] += jnp.sum(da * a_m2, axis=0, keepdims=True)
                acc_w[1:2, :] += jnp.sum(da * a_m1, axis=0, keepdims=True)
                acc_w[2:3, :] += jnp.sum(da * a_pre, axis=0, keepdims=True)
                acc_b[...] += jnp.sum(da, axis=0, keepdims=True)
                c0, c1 = parts[idx]
                dgg_ring[other, :, c0:c1] = _dot_nt(dh2n, wdn_ref[c0:c1, :])
                if idx % every == 0:
                    sl = idx // every
                    part = _dot_nt(dup_ring[other, :, sl * wsl:(sl + 1) * wsl], wup_ref[sl])
                    du2 = part if du2 is None else du2 + part
            norm_bwd(du2, s > 0)

        @pl.when(s == nt)
        def _():
            du2 = _dot_nt(dup_ring[other, :, 0:wsl], wup_ref[0])
            for sl in range(1, nsl):
                du2 = du2 + _dot_nt(dup_ring[other, :, sl * wsl:(sl + 1) * wsl], wup_ref[sl])
            norm_bwd(du2, True)
            dfw_ref[...] = acc_w[0:3, :]
            dfb_ref[...] = acc_b[...]
            dwn_ref[...] = acc_n[...]

    gate_tile = lambda s: jnp.maximum(nt - 1 - s, 0)
    next_tile = lambda s: jnp.maximum(nt - 2 - s, 0)
    prev_tile = lambda s: jnp.minimum(nt - s, nt - 1)
    vec = lambda w, r=1: pl.BlockSpec((r, w), lambda s: (0, 0))
    vm = pl.BlockSpec(memory_space=pltpu.VMEM)
    return pl.pallas_call(
        body, name="ffn_block_bwd", grid=(nt + 1,),
        in_specs=[pl.BlockSpec((TR, D), lambda s: (next_tile(s), 0)),
                  pl.BlockSpec((TR, D), lambda s: (prev_tile(s), 0)),
                  pl.BlockSpec((TR, 2 * DFF), lambda s: (gate_tile(s), 0)),
                  pl.BlockSpec((16, DFF), lambda s: (jnp.maximum(gate_tile(s) * tb16 - 1, 0), 0)),
                  pl.BlockSpec((TR, D), lambda s: (prev_tile(s), 0)),
                  vec(D), vm, vec(DFF, 3), vec(DFF), vm],
        out_specs=[pl.BlockSpec((TR, 2 * DFF), lambda s: (gate_tile(s), 0)),
                   pl.BlockSpec((TR, D), lambda s: (prev_tile(s), 0)),
                   vec(DFF, 3), vec(DFF), vec(D)],
        out_shape=[jax.ShapeDtypeStruct((T, 2 * DFF), BF16), jax.ShapeDtypeStruct((T, D), F32),
                   jax.ShapeDtypeStruct((3, DFF), F32), jax.ShapeDtypeStruct((1, DFF), F32),
                   jax.ShapeDtypeStruct((1, D), F32)],
        scratch_shapes=[pltpu.VMEM((2, TR, DFF), F32), pltpu.VMEM((2, TR, 2 * DFF), BF16),
                        pltpu.VMEM((8, DFF), F32), pltpu.VMEM((8, DFF), F32), pltpu.VMEM((1, DFF), F32),
                        pltpu.VMEM((1, D), F32)],
        compiler_params=_cparams(("arbitrary",)),
    )(dh2, dh2, upb, upb, h1, wffn, w_up_g, fcw, fcb, w_down)


def _place():
    x, y, c = lax.axis_index("x"), lax.axis_index("y"), lax.axis_index("c")
    return x, y, c


_CHIP_FLIPS = ((1, 0), (0, 1), (1, 1))


def _flip(v, bit):
    return 1 - v if bit else v


def _into_slot(w, j_idx, *, rb, dtype, name, paired=False):
    r, cdim = w.shape

    def body(j_ref, w_ref, out_ref):
        del j_ref
        out_ref[...] = w_ref[...].astype(dtype)

    if paired:
        out_shape = jax.ShapeDtypeStruct((NSHARD // 2, r, 2 * cdim), dtype)
        out_spec = pl.BlockSpec((None, rb, cdim), lambda i, j_ref: (j_ref[0] // 2, i, j_ref[0] % 2))
    else:
        out_shape = jax.ShapeDtypeStruct((NSHARD, r, cdim), dtype)
        out_spec = pl.BlockSpec((None, rb, cdim), lambda i, j_ref: (j_ref[0], i, 0))
    grid_spec = pltpu.PrefetchScalarGridSpec(
        num_scalar_prefetch=1, grid=(r // rb,),
        in_specs=[pl.BlockSpec((rb, cdim), lambda i, j_ref: (i, 0))], out_specs=out_spec)
    return pl.pallas_call(
        body, name=name, grid_spec=grid_spec, out_shape=out_shape, compiler_params=_cparams(("parallel",)),
    )(j_idx, w)


class _Gather:
    def __init__(self, outs, send_sems, recv_sems, whole, paired=None):
        self.outs, self.send_sems, self.recv_sems, self.whole = outs, send_sems, recv_sems, whole
        self.paired = paired if paired is not None else (False,) * len(outs)
        self.x, self.y, self.c = _place()
        self.j = 2 * self.x + self.y
        self.sibling = (self.x, self.y, 1 - self.c)
        chips = [(_flip(self.x, fx), _flip(self.y, fy)) for fx, fy in _CHIP_FLIPS]
        self.slots = [2 * px + py for px, py in chips]
        self.peers = [(px, py, self.c) for px, py in chips]

    def _copy(self, w, slot, core, sem, to, quarter=None):
        r = self.outs[w].shape[1]
        if self.whole[w]:
            rows = pl.ds(0, r)
        elif quarter is None:
            rows = pl.ds(core * (r // 2), r // 2)
        else:
            rows = pl.ds(core * (r // 2) + quarter * (r // 4), r // 4)
        if self.paired[w]:
            cw = self.outs[w].shape[2] // 2
            piece = self.outs[w].at[slot // 2, rows, pl.ds((slot % 2) * cw, cw)]
        else:
            piece = self.outs[w].at[slot, rows, :]
        return pltpu.make_async_remote_copy(
            src_ref=piece, dst_ref=piece, send_sem=self.send_sems.at[w, sem], recv_sem=self.recv_sems.at[w, sem],
            device_id=to, device_id_type=MESH)

    def _direct(self, w):
        return (0, 1, 2) if self.whole[w] else (0, 1)

    def send(self):
        for w in range(len(self.outs)):
            for k in self._direct(w):
                self._copy(w, self.j, self.c, k, self.peers[k]).start()

    def relay(self):
        jx, jy, _ = self.slots
        for w in range(len(self.outs)):
            if self.whole[w]:
                for k in range(3):
                    self._copy(w, self.slots[k], self.c, k, self.sibling).wait_recv()
                continue
            self._copy(w, jx, self.c, 0, self.sibling).wait_recv()
            self._copy(w, jx, self.c, 6, self.peers[1], quarter=0).start()
            self._copy(w, jx, self.c, 3, self.sibling).start()
            self._copy(w, jy, self.c, 1, self.sibling).wait_recv()
            self._copy(w, jy, self.c, 7, self.peers[0], quarter=1).start()
            self._copy(w, jy, self.c, 4, self.sibling).start()

    def relay_diagonal(self):
        jd = self.slots[2]
        for w in range(len(self.outs)):
            if not self.whole[w]:
                self._copy(w, jd, self.c, 6, self.sibling, quarter=0).wait_recv()
                self._copy(w, jd, self.c, 7, self.sibling, quarter=1).wait_recv()
                self._copy(w, jd, self.c, 5, self.sibling).start()

    def finish(self):
        jx, jy, jd = self.slots
        for w in range(len(self.outs)):
            for k in self._direct(w):
                self._copy(w, self.j, self.c, k, self.peers[k]).wait_send()
            if self.whole[w]:
                continue
            self._copy(w, jx, self.c, 6, self.peers[1], quarter=0).wait_send()
            self._copy(w, jy, self.c, 7, self.peers[0], quarter=1).wait_send()
            for k, slot in enumerate(self.slots):
                self._copy(w, slot, 1 - self.c, 3 + k, self.sibling).wait_recv()
                self._copy(w, slot, self.c, 3 + k, self.sibling).wait_send()


def _allgather_weights(slotted, whole):
    n = len(slotted)

    def body(*refs):
        g = _Gather(refs[n:2 * n], refs[2 * n], refs[2 * n + 1], whole)
        g.send()
        g.relay()
        g.relay_diagonal()
        g.finish()

    any_spec = pl.BlockSpec(memory_space=pl.ANY)
    return pl.pallas_call(
        body, name="allgather_weights",
        in_specs=[any_spec] * n, out_specs=[any_spec] * n,
        out_shape=[jax.ShapeDtypeStruct(a.shape, a.dtype) for a in slotted],
        input_output_aliases={i: i for i in range(n)},
        scratch_shapes=[pltpu.SemaphoreType.DMA((n, 8)), pltpu.SemaphoreType.DMA((n, 8))],
    )(*slotted)


class _ChipExchange:
    def __init__(self, ins, outs, send_sems, recv_sems):
        self.ins, self.outs, self.send_sems, self.recv_sems = ins, outs, send_sems, recv_sems
        self.x, self.y, self.c = _place()

    def _copies(self):
        for w in range(len(self.ins)):
            for kk, (fx, fy) in enumerate(_CHIP_FLIPS):
                px, py = _flip(self.x, fx), _flip(self.y, fy)
                yield pltpu.make_async_remote_copy(
                    src_ref=self.ins[w].at[2 * px + py], dst_ref=self.outs[w].at[kk],
                    send_sem=self.send_sems.at[w, kk], recv_sem=self.recv_sems.at[w, kk],
                    device_id=(px, py, self.c), device_id_type=MESH)

    def send(self):
        for cp in self._copies():
            cp.start()

    def finish(self):
        for cp in self._copies():
            cp.wait()


class _PairExchange:
    def __init__(self, ins, outs, send_sems, recv_sems):
        self.ins, self.outs, self.send_sems, self.recv_sems = ins, outs, send_sems, recv_sems
        self.x, self.y, self.c = _place()

    def _copies(self):
        for w in range(len(self.ins)):
            half = self.ins[w].shape[1] // 2
            yield pltpu.make_async_remote_copy(
                src_ref=self.ins[w].at[:, pl.ds((1 - self.c) * half, half), :], dst_ref=self.outs[w],
                send_sem=self.send_sems.at[w], recv_sem=self.recv_sems.at[w],
                device_id=(self.x, self.y, 1 - self.c), device_id_type=MESH)

    def send(self):
        for cp in self._copies():
            cp.start()

    def finish(self):
        for cp in self._copies():
            cp.wait()


def _pair_shapes(grads):
    return [jax.ShapeDtypeStruct((g.shape[0], g.shape[1] // 2, g.shape[2]), g.dtype) for g in grads]


def _pair_exchange(grads, name):
    nw = len(grads)

    def body(*refs):
        ex = _PairExchange(refs[:nw], refs[nw:2 * nw], refs[2 * nw], refs[2 * nw + 1])
        ex.send()
        ex.finish()

    any_spec = pl.BlockSpec(memory_space=pl.ANY)
    return pl.pallas_call(
        body, name=name,
        in_specs=[any_spec] * nw, out_specs=[any_spec] * nw, out_shape=_pair_shapes(grads),
        scratch_shapes=[pltpu.SemaphoreType.DMA((nw,)), pltpu.SemaphoreType.DMA((nw,))],
    )(*grads)


def _pair_add(g, other, c_idx, *, rb, name, ride=()):
    S, r, cdim = g.shape
    half = r // 2
    nb = half // rb
    nex = len(ride)

    def body(c_ref, g_ref, o_ref, *rest):
        del c_ref
        out_ref = rest[nex]
        s, i = pl.program_id(0), pl.program_id(1)
        if nex:
            exchange = _PairExchange(rest[:nex], rest[nex + 1:2 * nex + 1], rest[-2], rest[-1])

            @pl.when(jnp.logical_and(s == 0, i == 0))
            def _():
                exchange.send()

        out_ref[...] = (g_ref[...].astype(F32) + o_ref[...].astype(F32)).astype(BF16)

        if nex:
            @pl.when(jnp.logical_and(s == S - 1, i == nb - 1))
            def _():
                exchange.finish()

    hbm = pl.BlockSpec(memory_space=pl.ANY)
    grid_spec = pltpu.PrefetchScalarGridSpec(
        num_scalar_prefetch=1, grid=(S, nb),
        in_specs=[pl.BlockSpec((None, rb, cdim), lambda s, i, c_ref: (s, c_ref[0] * nb + i, 0)),
                  pl.BlockSpec((None, rb, cdim), lambda s, i, c_ref: (s, i, 0))] + [hbm] * nex,
        out_specs=[pl.BlockSpec((None, rb, cdim), lambda s, i, c_ref: (s, i, 0))] + [hbm] * nex,
        scratch_shapes=[pltpu.SemaphoreType.DMA((nex,)), pltpu.SemaphoreType.DMA((nex,))] if nex else [])
    res = pl.pallas_call(
        body, name=name, grid_spec=grid_spec,
        out_shape=[jax.ShapeDtypeStruct((S, half, cdim), BF16)] + _pair_shapes(ride),
        compiler_params=_cparams(("arbitrary", "arbitrary") if nex else ("parallel", "parallel")),
    )(c_idx, g, other, *ride)
    return (res[0], list(res[1:])) if nex else res[0]


def _chip_sum(psum, parts, cj_idx, *, rb, name):
    S, half, cdim = psum.shape
    nb = half // rb

    def body(cj_ref, own_ref, p_ref, out_ref):
        del cj_ref
        f = lambda v: v.astype(F32)
        out_ref[...] = ((f(own_ref[...]) + f(p_ref[0])) + f(p_ref[1])) + f(p_ref[2])

    grid_spec = pltpu.PrefetchScalarGridSpec(
        num_scalar_prefetch=1, grid=(nb,),
        in_specs=[pl.BlockSpec((None, rb, cdim), lambda i, cj: (cj[1], i, 0)),
                  pl.BlockSpec((3, rb, cdim), lambda i, cj: (0, i, 0))],
        out_specs=pl.BlockSpec((rb, cdim), lambda i, cj: (cj[0] * nb + i, 0)))
    return pl.pallas_call(
        body, name=name, grid_spec=grid_spec, out_shape=jax.ShapeDtypeStruct((2 * half, cdim), F32),
        compiler_params=_cparams(("parallel",)),
    )(cj_idx, psum, parts)


SLAB_W = 1024


def _final_exchange(pieces, grads):
    n, nw = len(pieces), len(grads)
    segs, at = [], 0
    for idx, p in enumerate(pieces):
        r, wd = p.shape
        for c0 in range(0, wd, SLAB_W):
            if r > 1:
                at = -(-at // 8) * 8
            segs.append((idx, c0, min(SLAB_W, wd - c0), at))
            at += r
    rows = -(-at // 8) * 8
    flips = [(fx, fy, fc) for fx in (0, 1) for fy in (0, 1) for fc in (0, 1)][1:]

    def body(*refs):
        ins = refs[:n]
        outs = refs[n + nw:2 * n + nw]
        g_refs = refs[2 * n + nw:2 * n + 2 * nw]
        mine_ref, slots_ref, send_sems, recv_sems, gsend_sems, grecv_sems = refs[2 * n + 2 * nw:]
        x, y, c = _place()
        me = 4 * x + 2 * y + c
        sibling = (x, y, 1 - c)

        def swap(w, core):
            half = g_refs[w].shape[0] // 2
            rows_ = g_refs[w].at[pl.ds(core * half, half), :]
            return pltpu.make_async_remote_copy(
                src_ref=rows_, dst_ref=rows_, send_sem=gsend_sems.at[w], recv_sem=grecv_sems.at[w],
                device_id=sibling, device_id_type=MESH)

        for w in range(nw):
            swap(w, c).start()
        mine_ref[...] = jnp.zeros_like(mine_ref)
        for idx, c0, wd, st in segs:
            r = ins[idx].shape[0]
            mine_ref[st:st + r, 0:wd] = ins[idx][:, c0:c0 + wd]
        slots_ref[me] = mine_ref[...]
        cps = []
        for kk, (fx, fy, fc) in enumerate(flips):
            cp = pltpu.make_async_remote_copy(
                src_ref=mine_ref, dst_ref=slots_ref.at[me], send_sem=send_sems.at[kk], recv_sem=recv_sems.at[kk],
                device_id=(_flip(x, fx), _flip(y, fy), _flip(c, fc)), device_id_type=MESH)
            cp.start()
            cps.append(cp)
        for cp in cps:
            cp.wait()
        tot = slots_ref[0]
        for d in range(1, 8):
            tot = tot + slots_ref[d]
        mine_ref[...] = tot
        for idx, c0, wd, st in segs:
            r = ins[idx].shape[0]
            val = mine_ref[st:st + r, 0:wd]
            if idx == n - 1:
                outs[idx][...] = jnp.sum(val, keepdims=True)
            else:
                outs[idx][:, c0:c0 + wd] = val
        for w in range(nw):
            swap(w, 1 - c).wait_recv()
            swap(w, c).wait_send()

    vm = pl.BlockSpec(memory_space=pltpu.VMEM)
    hbm = pl.BlockSpec(memory_space=pl.ANY)
    out_shape = ([jax.ShapeDtypeStruct(p.shape, F32) for p in pieces[:-1]] + [jax.ShapeDtypeStruct((1, 1), F32)]
                 + [jax.ShapeDtypeStruct(g.shape, g.dtype) for g in grads])
    res = pl.pallas_call(
        body, name="final_exchange", in_specs=[vm] * n + [hbm] * nw, out_specs=[vm] * n + [hbm] * nw,
        out_shape=out_shape, input_output_aliases={n + w: n + w for w in range(nw)},
        scratch_shapes=[pltpu.VMEM((rows, SLAB_W), F32), pltpu.VMEM((8, rows, SLAB_W), F32),
                        pltpu.SemaphoreType.DMA((7,)), pltpu.SemaphoreType.DMA((7,)),
                        pltpu.SemaphoreType.DMA((nw,)), pltpu.SemaphoreType.DMA((nw,))],
    )(*pieces, *grads)
    return res[:n], res[n:]


def _adamw(w, g, m, v, *, rb, name):
    r, cdim = w.shape

    def body(w_ref, g_ref, m_ref, v_ref, go_ref, d_ref, nm_ref, nv_ref):
        go_ref[...] = g_ref[...]
        d_ref[...], nm_ref[...], nv_ref[...] = _adamw_update(w_ref[...], g_ref[...], m_ref[...], v_ref[...])

    spec = pl.BlockSpec((rb, cdim), lambda i: (i, 0))
    shp = jax.ShapeDtypeStruct((r, cdim), F32)
    return pl.pallas_call(
        body, name=name, grid=(r // rb,), in_specs=[spec] * 4, out_specs=[spec] * 4, out_shape=[shp] * 4,
        compiler_params=_cparams(("parallel",)),
    )(w, g, m, v)


def _adamw_update(w, g, m, v):
    nm = ADAM_B1 * m + (1.0 - ADAM_B1) * g
    nv = ADAM_B2 * v + (1.0 - ADAM_B2) * (g * g)
    m_hat = nm / (1.0 - ADAM_B1 ** ADAM_STEP)
    v_hat = nv / (1.0 - ADAM_B2 ** ADAM_STEP)
    return -ADAM_LR * (m_hat / (jnp.sqrt(v_hat) + ADAM_EPS) + ADAM_WD * w), nm, nv


def _adamw_small(params):
    n = len(params)

    def body(*refs):
        ins, outs = refs[:4 * n], refs[4 * n:]
        for p in range(n):
            w_ref, g_ref, m_ref, v_ref = ins[4 * p:4 * p + 4]
            d, nm, nv = _adamw_update(w_ref[...], g_ref[...], m_ref[...], v_ref[...])
            outs[3 * p][...] = d
            outs[3 * p + 1][...] = nm
            outs[3 * p + 2][...] = nv

    vm = pl.BlockSpec(memory_space=pltpu.VMEM)
    flat = [a for p in params for a in p]
    out_shape = [jax.ShapeDtypeStruct(p[0].shape, F32) for p in params for _ in range(3)]
    res = pl.pallas_call(
        body, name="adamw_small", in_specs=[vm] * (4 * n), out_specs=[vm] * (3 * n), out_shape=out_shape,
    )(*flat)
    return [tuple(res[3 * p:3 * p + 3]) for p in range(n)]


_PAIR_ADD_ROWS = {"w_in": 256, "w_out": 128, "w_up": 256, "w_down": 176}


def _pair_sums(grads, others, names, c_idx):
    return [_pair_add(g, o_, c_idx, rb=_PAIR_ADD_ROWS[n], name=f"pair_add_{n}") for g, o_, n in zip(grads, others, names)]


def _local_step(x, tgt, meta_full, lb_param, attn_norm_w, w_in_g, hgrn_norm_w, conv_w_full, w_out_full,
                ffn_norm_w, late_slotted, fcw_full, ffn_conv_b, final_norm_w, c_idx, cj_idx):
    seq = x.shape[0]
    T = TR + seq
    head_tile = jnp.concatenate([jnp.zeros((PAD, D), F32), meta_full], axis=0)
    whn_t = jnp.tile(hgrn_norm_w, (1, NH))

    ut, projb, o, sst, mt, h1, w_up_g, w_down_g = _mix_block_fwd(
        x, head_tile, lb_param, attn_norm_w, whn_t, conv_w_full, w_in_g, w_out_full, late_slotted, (True, False))
    w_down_full = w_down_g.reshape(DFF, D)
    u2t, upb, ggt, dh2, loss_vec, dwfin = _ffn_block_fwd(
        h1, tgt, ffn_norm_w, w_up_g, fcw_full, ffn_conv_b, w_down_full, final_norm_w.reshape(1, D))

    dup, dh1, dfw, dfb, dwffn = _ffn_block_bwd(
        dh2, upb, h1, ffn_norm_w, w_up_g, fcw_full, ffn_conv_b, w_down_full)
    kb = 1408 if T % 1408 == 0 else TR
    g_up = _weight_grad(u2t, dup, bn=DFF, bk=kb, name="dw_up_mm", shard_cols=2 * DFF // NSHARD)
    g_down, sib_up = _weight_grad(ggt, dh2, bn=D // 2, bk=2816 if T % 2816 == 0 else kb, name="dw_down_mm", ride=[g_up])
    g_down = g_down.reshape(NSHARD, DFF // NSHARD, D)
    ps_up, sib_down = _pair_add(g_up, sib_up[0], c_idx, rb=_PAIR_ADD_ROWS["w_up"], name="pair_add_w_up", ride=[g_down])
    ps_ffn = [ps_up, _pair_add(g_down, sib_down[0], c_idx, rb=_PAIR_ADD_ROWS["w_down"], name="pair_add_w_down")]

    dproj, dlb, dwhn, dcw, *parts_ffn = _mix_block_bwd(
        dh1, projb, o, sst, lb_param, whn_t, conv_w_full, w_out_full, ps_ffn)
    kb_deep = 2816 if T % 2816 == 0 else kb
    g_in = _weight_grad(ut, dproj, bn=9 * D // NSHARD, bk=kb_deep, name="dw_in_mm", shard_cols=9 * D // NSHARD)
    g_out, sib_in = _weight_grad(mt, dh1, bn=D, bk=kb_deep, name="dw_out_mm", ride=[g_in])
    g_out = g_out.reshape(NSHARD, D // NSHARD, D)
    ps_in, sib_out = _pair_add(g_in, sib_in[0], c_idx, rb=_PAIR_ADD_ROWS["w_in"], name="pair_add_w_in", ride=[g_out])
    ps_mix = [ps_in, _pair_add(g_out, sib_out[0], c_idx, rb=_PAIR_ADD_ROWS["w_out"], name="pair_add_w_out")]

    grad_x, dmeta, dwattn, *parts_mix = _input_grad_block(dproj, x, head_tile, dh1, attn_norm_w, w_in_g, ps_mix)

    halves = [_chip_sum(ps, p, cj_idx, rb=_PAIR_ADD_ROWS[n], name=f"chip_sum_{n}")
              for ps, p, n in zip(ps_mix + ps_ffn, parts_mix + parts_ffn, ("w_in", "w_out", "w_up", "w_down"))]
    small = dict(dlb=dlb, dwattn=dwattn, dwhn=dwhn, dwffn=dwffn, dfb=dfb, dwfin=dwfin,
                 dcw=dcw, dfw=dfw, dmeta=dmeta, loss=loss_vec)
    return grad_x, small, halves


_SMALL_ORDER = ("dmeta", "dcw", "dfw", "dlb", "dwattn", "dwhn", "dwffn", "dfb", "dwfin", "loss")


def kernel(x, meta_tokens, lb_param, attn_norm_w, w_in, hgrn_norm_w, conv_w, w_out, ffn_norm_w, w_up, ffn_conv_w, ffn_conv_b, w_down, final_norm_w, loss_target, m_meta_tokens, m_lb_param, m_attn_norm_w, m_w_in, m_hgrn_norm_w, m_conv_w, m_w_out, m_ffn_norm_w, m_w_up, m_ffn_conv_w, m_ffn_conv_b, m_w_down, m_final_norm_w, v_meta_tokens, v_lb_param, v_attn_norm_w, v_w_in, v_hgrn_norm_w, v_conv_w, v_w_out, v_ffn_norm_w, v_w_up, v_ffn_conv_w, v_ffn_conv_b, v_w_down, v_final_norm_w):
    xi, yi, ci = _place()
    j = 2 * xi + yi
    c_idx = jnp.reshape(ci, (1,)).astype(jnp.int32)

    j_idx = jnp.reshape(j, (1,)).astype(jnp.int32)
    ds_, fs_ = D // NSHARD, DFF // NSHARD
    widen = lambda a: jnp.pad(a, ((0, 0), (0, 768 - a.shape[1])))
    rows_small = jnp.concatenate([widen(meta_tokens), widen(conv_w[0]), widen(ffn_conv_w[0]),
                                  jnp.zeros((2, 768), F32)], axis=0)
    s_in, s_out, s_up, s_down = [
        _into_slot(w[0], j_idx, rb=rb, dtype=BF16, name=f"slot_{n}", paired=pr)
        for w, rb, n, pr in ((w_in, 256, "w_in", False), (w_out, 128, "w_out", False), (w_up, 256, "w_up", True),
                             (w_down, 176, "w_down", False))]
    s_small = _into_slot(rows_small, j_idx, rb=rows_small.shape[0], dtype=F32, name="slot_small")
    w_in_g, w_out_g, small_g = _allgather_weights([s_in, s_out, s_small], (False, False, True))
    unshard = lambda a: jnp.transpose(a, (1, 0, 2)).reshape(a.shape[1], -1)
    meta_full = unshard(small_g[:, 0:NMETA, 0:ds_])
    conv_w_full = unshard(small_g[:, NMETA:NMETA + 3, 0:ds_])
    fcw_full = unshard(small_g[:, NMETA + 3:NMETA + 6, 0:fs_])

    cj_idx = jnp.stack([ci, j]).astype(jnp.int32)
    grad_x, small, halves = _local_step(
        x[0], loss_target[0], meta_full, lb_param, attn_norm_w, w_in_g, hgrn_norm_w, conv_w_full,
        w_out_g.reshape(D, D), ffn_norm_w, [s_up, s_down], fcw_full, ffn_conv_b, final_norm_w, c_idx, cj_idx)

    names = _SMALL_ORDER
    small_sums, g_big = _final_exchange([small[n] for n in names], halves)
    vals = dict(zip(names, small_sums))
    loss = vals["loss"].reshape(())
    g_small = {
        "meta_tokens": lax.dynamic_slice_in_dim(vals["dmeta"], j * (D // NSHARD), D // NSHARD, axis=1),
        "lb_param": jnp.concatenate([vals["dlb"], -vals["dlb"]], axis=0),
        "attn_norm_w": vals["dwattn"],
        "hgrn_norm_w": vals["dwhn"],
        "conv_w": lax.dynamic_slice_in_dim(vals["dcw"], j * (D // NSHARD), D // NSHARD, axis=1)[None],
        "ffn_norm_w": vals["dwffn"],
        "ffn_conv_w": lax.dynamic_slice_in_dim(vals["dfw"], j * (DFF // NSHARD), DFF // NSHARD, axis=1)[None],
        "ffn_conv_b": vals["dfb"],
        "final_norm_w": vals["dwfin"].reshape(D),
    }


    weights = {"meta_tokens": meta_tokens, "lb_param": lb_param, "attn_norm_w": attn_norm_w, "w_in": w_in,
               "hgrn_norm_w": hgrn_norm_w, "conv_w": conv_w, "w_out": w_out, "ffn_norm_w": ffn_norm_w,
               "w_up": w_up, "ffn_conv_w": ffn_conv_w, "ffn_conv_b": ffn_conv_b, "w_down": w_down,
               "final_norm_w": final_norm_w}
    ms = {"meta_tokens": m_meta_tokens, "lb_param": m_lb_param, "attn_norm_w": m_attn_norm_w, "w_in": m_w_in,
          "hgrn_norm_w": m_hgrn_norm_w, "conv_w": m_conv_w, "w_out": m_w_out, "ffn_norm_w": m_ffn_norm_w,
          "w_up": m_w_up, "ffn_conv_w": m_ffn_conv_w, "ffn_conv_b": m_ffn_conv_b, "w_down": m_w_down,
          "final_norm_w": m_final_norm_w}
    vs = {"meta_tokens": v_meta_tokens, "lb_param": v_lb_param, "attn_norm_w": v_attn_norm_w, "w_in": v_w_in,
          "hgrn_norm_w": v_hgrn_norm_w, "conv_w": v_conv_w, "w_out": v_w_out, "ffn_norm_w": v_ffn_norm_w,
          "w_up": v_w_up, "ffn_conv_w": v_ffn_conv_w, "ffn_conv_b": v_ffn_conv_b, "w_down": v_w_down,
          "final_norm_w": v_final_norm_w}
    order = list(weights)
    grads, deltas, new_m, new_v = {}, {}, {}, {}

    for name, g, rb in zip(("w_in", "w_out", "w_up", "w_down"), g_big, (256, 128, 256, 176)):
        shp = weights[name].shape
        w2, m2, v2 = (a.reshape(shp[1], shp[2]) for a in (weights[name], ms[name], vs[name]))
        g_, d_, nm_, nv_ = _adamw(w2, g, m2, v2, rb=rb, name=f"adamw_{name}")
        grads[name], deltas[name], new_m[name], new_v[name] = (a.reshape(shp) for a in (g_, d_, nm_, nv_))

    small_names = [n for n in order if n not in grads]
    as2d = lambda a: a.reshape(-1, a.shape[-1])
    res = _adamw_small([tuple(as2d(a) for a in (weights[n], g_small[n], ms[n], vs[n])) for n in small_names])
    for n, (d_, nm_, nv_) in zip(small_names, res):
        shp = weights[n].shape
        grads[n], deltas[n], new_m[n], new_v[n] = (a.reshape(shp) for a in (g_small[n], d_, nm_, nv_))

    return (loss, grad_x[None], *[grads[n] for n in order], *[deltas[n] for n in order],
            *[new_m[n] for n in order], *[new_v[n] for n in order])
```
